```python
import math
import jax, jax.numpy as jnp
from jax import lax
import numpy as np

D_MODEL = 1024
BATCH = 4
SEQ = 4096
DEPTH = 1

CTX_LEN = 256
GRID_W = 64
F_GROUPS = 4
F_GROUP_DIM = 128
F_DIM = F_GROUPS * F_GROUP_DIM
NQK_HEADS = 4
NV_HEADS = 8
DK = 128
DV = 128
QK_DIM = NQK_HEADS * DK
V_DIM = NV_HEADS * DV
QKV_DIM = 2 * QK_DIM + V_DIM
CONV_K = 3
CHUNK = 64
BETA_OFF = QKV_DIM
A_OFF = BETA_OFF + 2 * NV_HEADS
GDN_IN_COLS = A_OFF + 2 * NV_HEADS
Z_OFF = GDN_IN_COLS
F_OFF = Z_OFF + V_DIM
GA_OFF = F_OFF + F_DIM
GB_OFF = GA_OFF + D_MODEL
IN_COLS = GB_OFF + D_MODEL
N_EXPERTS = 32
TOP_K = 4
D_EXPERT = D_MODEL
SWIGLU_ALPHA = 1.702
SWIGLU_LIMIT = 7.0
MOE_BLOCK = 128
EPS = 1e-6

kernel_name = 'hybrid_fnet_gdn_moe_dit_block'


def rmsnorm(x, g):
    xf = x.astype(jnp.float32)
    y = xf * lax.rsqrt(jnp.mean(xf * xf, axis=-1, keepdims=True) + EPS)
    return (y * g.astype(jnp.float32)).astype(x.dtype)


def modulate(h, shift, scale):
    return h * (1.0 + scale) + shift


def l2norm(t):
    t = t.astype(jnp.float32)
    return t * lax.rsqrt(jnp.sum(t * t, axis=-1, keepdims=True) + EPS)


def short_conv_grid(t, w, rows):
    B, L, C = t.shape
    y = lax.conv_general_dilated(t.reshape(B, rows, GRID_W, C), w[:, :, None, :], (1, 1), 'SAME',
                                 dimension_numbers=('NHWC', 'HWIO', 'NHWC'), feature_group_count=C)
    return y.reshape(B, L, C)


def short_conv_seq(t, w):
    C = t.shape[-1]
    return lax.conv_general_dilated(t, w[CONV_K // 2][:, None, :], (1,), 'SAME',
                                    dimension_numbers=('NWC', 'WIO', 'NWC'), feature_group_count=C)


def gdn_inputs(p, conv_w, rows):
    B, L, _ = p.shape
    qkv = p[..., :QKV_DIM]
    qkv = jax.nn.silu(short_conv_grid(qkv, conv_w, rows) if rows is not None else short_conv_seq(qkv, conv_w))
    rep = NV_HEADS // NQK_HEADS
    q = jnp.repeat(l2norm(qkv[..., :QK_DIM].reshape(B, L, NQK_HEADS, DK)) * DK ** -0.5, rep, axis=2)
    k = jnp.repeat(l2norm(qkv[..., QK_DIM:2 * QK_DIM].reshape(B, L, NQK_HEADS, DK)), rep, axis=2)
    v = qkv[..., 2 * QK_DIM:].reshape(B, L, NV_HEADS, DV)
    beta = jax.nn.sigmoid(p[..., BETA_OFF:A_OFF].astype(jnp.float32)).reshape(B, L, 2, NV_HEADS)
    a_raw = p[..., A_OFF:GDN_IN_COLS].astype(jnp.float32).reshape(B, L, 2, NV_HEADS)
    return q, k, v, a_raw, beta


def chunk_gated_delta(q, k, v, log_g, beta, s0):
    B, L, H, _ = q.shape
    n = L // CHUNK
    f32 = jnp.float32

    def blocks(t):
        t = t.astype(f32).reshape((B, n, CHUNK, H) + t.shape[3:])
        return jnp.moveaxis(t, 3, 1)

    qc, kc, vc, bc = blocks(q), blocks(k), blocks(v), blocks(beta)
    gc = jnp.cumsum(blocks(log_g), axis=-1)
    incl = jnp.tril(jnp.ones((CHUNK, CHUNK), bool))
    strict = jnp.tril(jnp.ones((CHUNK, CHUNK), bool), -1)
    decay = jnp.exp(jnp.where(incl, gc[..., :, None] - gc[..., None, :], -jnp.inf))
    kk = jnp.einsum('bhnid,bhnjd->bhnij', kc, kc)
    lower = jnp.where(strict, bc[..., :, None] * kk * decay, 0.0)
    rhs = jnp.concatenate([bc[..., None] * vc, (bc * jnp.exp(gc))[..., None] * kc], axis=-1)
    sol = lax.linalg.triangular_solve(jnp.eye(CHUNK, dtype=f32) + lower, rhs,
                                      left_side=True, lower=True, unit_diagonal=True)
    u_new, w = sol[..., :DV], sol[..., DV:]
    qk = jnp.einsum('bhnid,bhnjd->bhnij', qc, kc) * decay
    q_dec = qc * jnp.exp(gc)[..., None]
    g_end = gc[..., -1]
    k_dec = kc * jnp.exp(g_end[..., None] - gc)[..., None]

    def step(S, xs):
        u_c, w_c, qk_c, qd_c, kd_c, ge_c = xs
        u = u_c - jnp.einsum('bhck,bhkv->bhcv', w_c, S)
        o = jnp.einsum('bhck,bhkv->bhcv', qd_c, S) + jnp.einsum('bhij,bhjv->bhiv', qk_c, u)
        S = jnp.exp(ge_c)[..., None, None] * S + jnp.einsum('bhck,bhcv->bhkv', kd_c, u)
        return S, o

    xs = tuple(jnp.moveaxis(t, 2, 0) for t in (u_new, w, qk, q_dec, k_dec, g_end))
    s_fin, o = lax.scan(step, s0.astype(f32), xs)
    o = jnp.moveaxis(jnp.moveaxis(o, 0, 2), 1, 3).reshape(B, L, H, DV)
    return o, s_fin


def gdn_bidir(q, k, v, a_raw, beta, a_log, dt_bias, s0_f, s0_b):
    log_g = -jnp.exp(a_log.astype(jnp.float32)) * jax.nn.softplus(a_raw + dt_bias.astype(jnp.float32))
    o_f, s_f = chunk_gated_delta(q, k, v, log_g[:, :, 0], beta[:, :, 0], s0_f)
    flip = lambda t: jnp.flip(t, axis=1)
    o_b, s_b = chunk_gated_delta(flip(q), flip(k), flip(v), flip(log_g[:, :, 1]), flip(beta[:, :, 1]), s0_b)
    return o_f + flip(o_b), s_f, s_b


def fourier_mix(f):
    B, L, _ = f.shape
    fg = f.astype(jnp.float32).reshape(B, L, F_GROUPS, F_GROUP_DIM)
    return jnp.fft.fftn(fg, axes=(1, 3), norm='ortho').real.reshape(B, L, F_DIM).astype(f.dtype)


def merge_branches(p, o, gdn_norm_g, w_fourier_out, w_gdn_out, w_merge_out):
    B, L, _ = p.shape
    z = p[..., Z_OFF:F_OFF].astype(jnp.float32).reshape(B, L, NV_HEADS, DV)
    yb = o * lax.rsqrt(jnp.mean(o * o, axis=-1, keepdims=True) + EPS) * gdn_norm_g.astype(jnp.float32) * jax.nn.silu(z)
    yb = yb.reshape(B, L, V_DIM).astype(p.dtype) @ w_gdn_out
    ya = fourier_mix(p[..., F_OFF:GA_OFF]) @ w_fourier_out
    ga = jax.nn.sigmoid(p[..., GA_OFF:GB_OFF])
    gb = jax.nn.sigmoid(p[..., GB_OFF:IN_COLS])
    return (ga * ya + gb * yb) @ w_merge_out


def moe_ffn(h, w_router, b_router, w_gate, b_gate, w_up, b_up, w_down, b_down):
    N, D = h.shape
    logits = h.astype(jnp.float32) @ w_router.astype(jnp.float32) + b_router.astype(jnp.float32)
    top_val, top_idx = lax.top_k(logits, TOP_K)
    top_w = jax.nn.softmax(top_val, axis=-1)
    flat_e = top_idx.reshape(-1)
    flat_tok = jnp.repeat(jnp.arange(N, dtype=jnp.int32), TOP_K)
    order = jnp.argsort(flat_e)
    e_sorted = flat_e[order]
    counts = jnp.bincount(flat_e, length=N_EXPERTS)
    padded = (counts + MOE_BLOCK - 1) // MOE_BLOCK * MOE_BLOCK
    pad_end = jnp.cumsum(padded)
    pad_start = pad_end - padded
    grp_start = jnp.cumsum(counts) - counts
    dest = pad_start[e_sorted] + jnp.arange(N * TOP_K, dtype=jnp.int32) - grp_start[e_sorted]
    n_blocks = -(-(N * TOP_K + N_EXPERTS * (MOE_BLOCK - 1)) // MOE_BLOCK)
    n_slots = n_blocks * MOE_BLOCK
    slot_tok = jnp.full((n_slots,), N, jnp.int32).at[dest].set(flat_tok[order])
    slot_w = jnp.zeros((n_slots,), jnp.float32).at[dest].set(top_w.reshape(-1)[order])
    block_e = jnp.minimum(jnp.searchsorted(pad_end, jnp.arange(n_blocks, dtype=jnp.int32) * MOE_BLOCK, side='right'),
                          N_EXPERTS - 1)
    h_pad = jnp.concatenate([h, jnp.zeros((1, D), h.dtype)], axis=0)
    xb = h_pad[slot_tok].reshape(n_blocks, MOE_BLOCK, D)

    def expert_block(args):
        xe, e = args
        gate = jnp.minimum(xe @ w_gate[e] + b_gate[e], SWIGLU_LIMIT)
        up = jnp.clip(xe @ w_up[e] + b_up[e], -SWIGLU_LIMIT, SWIGLU_LIMIT)
        act = (up + 1.0) * gate * jax.nn.sigmoid(SWIGLU_ALPHA * gate)
        return act @ w_down[e] + b_down[e]

    yb = lax.map(expert_block, (xb, block_e)).reshape(n_slots, D)
    out = jnp.zeros((N + 1, D), jnp.float32).at[slot_tok].add(yb.astype(jnp.float32) * slot_w[:, None])
    return out[:N].astype(h.dtype)


def setup_inputs(seed: int = 0) -> dict:
    key = jax.random.key(seed)
    ks = jax.random.split(key, 32)
    D = D_MODEL
    nrm = lambda k, shape, s: jax.random.normal(k, shape, jnp.float32) * s
    dt = jnp.exp(jax.random.uniform(ks[11], (DEPTH, 2, NV_HEADS), jnp.float32, math.log(1e-3), math.log(1e-1)))
    return {
        'x': nrm(ks[0], (BATCH, SEQ, D), 1.0),
        'c': nrm(ks[1], (BATCH, D), 1.0),
        'ctx': nrm(ks[2], (BATCH, CTX_LEN, D), 1.0),
        'c_ctx': nrm(ks[3], (D,), 1.0),
        'w_mod': nrm(ks[4], (DEPTH, D, 6 * D), 0.5 * D ** -0.5),
        'b_mod': nrm(ks[5], (DEPTH, 6 * D), 0.02),
        'norm1_g': 1.0 + nrm(ks[6], (DEPTH, D), 0.05),
        'norm2_g': 1.0 + nrm(ks[7], (DEPTH, D), 0.05),
        'w_in': nrm(ks[8], (DEPTH, D, IN_COLS), D ** -0.5),
        'conv_w': nrm(ks[9], (DEPTH, CONV_K, CONV_K, QKV_DIM), 1.0 / CONV_K),
        'a_log': jnp.log(jax.random.uniform(ks[10], (DEPTH, 2, NV_HEADS), jnp.float32, 1.0, 16.0)),
        'dt_bias': dt + jnp.log(-jnp.expm1(-dt)),
        'gdn_norm_g': 1.0 + nrm(ks[12], (DEPTH, DV), 0.05),
        'w_fourier_out': nrm(ks[13], (DEPTH, F_DIM, D), F_DIM ** -0.5),
        'w_gdn_out': nrm(ks[14], (DEPTH, V_DIM, D), V_DIM ** -0.5),
        'w_merge_out': nrm(ks[15], (DEPTH, D, D), D ** -0.5),
        'w_router': nrm(ks[16], (DEPTH, D, N_EXPERTS), D ** -0.5),
        'b_router': nrm(ks[17], (DEPTH, N_EXPERTS), 0.01),
        'w_gate': nrm(ks[18], (DEPTH, N_EXPERTS, D, D_EXPERT), D ** -0.5),
        'b_gate': nrm(ks[19], (DEPTH, N_EXPERTS, D_EXPERT), 0.01),
        'w_up': nrm(ks[20], (DEPTH, N_EXPERTS, D, D_EXPERT), D ** -0.5),
        'b_up': nrm(ks[21], (DEPTH, N_EXPERTS, D_EXPERT), 0.01),
        'w_down': nrm(ks[22], (DEPTH, N_EXPERTS, D_EXPERT, D), D_EXPERT ** -0.5),
        'b_down': nrm(ks[23], (DEPTH, N_EXPERTS, D), 0.01),
        'final_norm_g': 1.0 + nrm(ks[24], (D,), 0.05),
    }


def reference(x, c, ctx, c_ctx, w_mod, b_mod, norm1_g, norm2_g, w_in, conv_w, a_log, dt_bias, gdn_norm_g,
              w_fourier_out, w_gdn_out, w_merge_out, w_router, b_router, w_gate, b_gate, w_up, b_up,
              w_down, b_down, final_norm_g):
    B, L, D = x.shape
    rows = L // GRID_W
    h_ctx = ctx
    for l in range(DEPTH):
        last = l == DEPTH - 1
        mod = (jax.nn.silu(c) @ w_mod[l] + b_mod[l])[:, None, :]
        sh1, sc1, g1, sh2, sc2, g2 = jnp.split(mod, 6, axis=-1)
        mod_c = jax.nn.silu(c_ctx) @ w_mod[l] + b_mod[l]
        csh1, csc1, cg1, csh2, csc2, cg2 = jnp.split(mod_c, 6)

        u_ctx = modulate(rmsnorm(h_ctx, norm1_g[l]), csh1, csc1)
        p_ctx = u_ctx @ (w_in[l][:, :GDN_IN_COLS] if last else w_in[l])
        q, k, v, a_raw, beta = gdn_inputs(p_ctx, conv_w[l], None)
        zero = jnp.zeros((B, NV_HEADS, DK, DV), jnp.float32)
        o_ctx, s_f, s_b = gdn_bidir(q, k, v, a_raw, beta, a_log[l], dt_bias[l], zero, zero)

        u = modulate(rmsnorm(x, norm1_g[l]), sh1, sc1)
        p = u @ w_in[l]
        q, k, v, a_raw, beta = gdn_inputs(p, conv_w[l], rows)
        o, _, _ = gdn_bidir(q, k, v, a_raw, beta, a_log[l], dt_bias[l], s_f, s_b)
        x = x + g1 * merge_branches(p, o, gdn_norm_g[l], w_fourier_out[l], w_gdn_out[l], w_merge_out[l])

        h2 = modulate(rmsnorm(x, norm2_g[l]), sh2, sc2)
        x = x + g2 * moe_ffn(h2.reshape(B * L, D), w_router[l], b_router[l], w_gate[l], b_gate[l],
                             w_up[l], b_up[l], w_down[l], b_down[l]).reshape(B, L, D)

        if not last:
            h_ctx = h_ctx + cg1 * merge_branches(p_ctx, o_ctx, gdn_norm_g[l], w_fourier_out[l], w_gdn_out[l], w_merge_out[l])
            hc2 = modulate(rmsnorm(h_ctx, norm2_g[l]), csh2, csc2)
            n_ctx = h_ctx.shape[1]
            h_ctx = h_ctx + cg2 * moe_ffn(hc2.reshape(B * n_ctx, D), w_router[l], b_router[l], w_gate[l], b_gate[l],
                                          w_up[l], b_up[l], w_down[l], b_down[l]).reshape(B, n_ctx, D)
    return rmsnorm(x, final_norm_g)
```

```python
import functools
import math

import jax
import jax.numpy as jnp
import numpy as np
from jax import lax
from jax.experimental import pallas as pl
from jax.experimental.pallas import tpu as pltpu

F32 = jnp.float32
BF16 = jnp.bfloat16
I32 = jnp.int32
HIGHEST = lax.Precision.HIGHEST

GRID_W = 64
NQK_HEADS = 4
NV_HEADS = 8
HEAD_DIM = 128
QK_DIM = NQK_HEADS * HEAD_DIM
V_DIM = NV_HEADS * HEAD_DIM
QKV_DIM = 2 * QK_DIM + V_DIM
F_GROUPS = 4
F_DIM = F_GROUPS * HEAD_DIM
N_EXPERTS = 32
TOP_K = 4
SWIGLU_ALPHA = 1.702
SWIGLU_LIMIT = 7.0
EPS = 1e-6

LANES = 128
GATE_LANE0 = 16
GDN_CHUNK = 128
EXPERT_BLOCK = 256
NEG_BIG = -1e30
MIB = 2 ** 20


def _cparams(sem, vmem_mib):
    return pltpu.CompilerParams(dimension_semantics=sem, vmem_limit_bytes=vmem_mib * MIB)


def _mm(a, b, prec=None, dims=(((1,), (0,)), ((), ()))):
    if prec is None:
        return lax.dot_general(a.astype(BF16), b.astype(BF16), dims, preferred_element_type=F32)
    return lax.dot_general(a.astype(F32), b.astype(F32), dims, precision=prec, preferred_element_type=F32)


def _sigmoid(x):
    return 1.0 / (1.0 + jnp.exp(-x))


def _rmsnorm(x, g):
    return x * lax.rsqrt(jnp.mean(x * x, axis=-1, keepdims=True) + EPS) * g


def _mod_kernel(c_ref, w_ref, b_ref, o_ref):
    c = c_ref[...]
    o_ref[...] = _mm(c * _sigmoid(c), w_ref[...], HIGHEST) + b_ref[...]


def _mod_call(c8, w_mod, b_mod):
    d, n = w_mod.shape
    tn = 1536
    return pl.pallas_call(
        _mod_kernel,
        grid=(n // tn,),
        in_specs=[pl.BlockSpec((8, d), lambda j: (0, 0)),
                  pl.BlockSpec((d, tn), lambda j: (0, j)),
                  pl.BlockSpec((1, tn), lambda j: (0, j))],
        out_specs=pl.BlockSpec((8, tn), lambda j: (0, j)),
        out_shape=jax.ShapeDtypeStruct((8, n), F32),
        compiler_params=_cparams(("parallel",), 32),
        name="mod",
    )(c8, w_mod, b_mod.reshape(1, n))


def _inproj_kernel(x_ref, sh_ref, sc_ref, g_ref, w_ref, *o_refs, widths):
    u = (_rmsnorm(x_ref[...], g_ref[...]) * (1.0 + sc_ref[0]) + sh_ref[0]).astype(BF16)
    start = 0
    for o_ref, width in zip(o_refs, widths):
        step = min(width, 512)
        for c0 in range(0, width, step):
            o_ref[:, c0:c0 + step] = _mm(u, w_ref[:, start + c0:start + c0 + step]).astype(o_ref.dtype)
        start += width


def _inproj_call(x2, sh, sc, g, w, widths, dtypes, tokens_per_batch, tm):
    n, d = x2.shape
    per = tokens_per_batch // tm
    out_shape = [jax.ShapeDtypeStruct((n, wd), dt) for wd, dt in zip(widths, dtypes)]
    return pl.pallas_call(
        functools.partial(_inproj_kernel, widths=widths),
        grid=(n // tm,),
        in_specs=[pl.BlockSpec((tm, d), lambda i: (i, 0)),
                  pl.BlockSpec((1, 1, d), lambda i: (i // per, 0, 0)),
                  pl.BlockSpec((1, 1, d), lambda i: (i // per, 0, 0)),
                  pl.BlockSpec((1, d), lambda i: (0, 0)),
                  pl.BlockSpec(w.shape, lambda i: (0, 0))],
        out_specs=[pl.BlockSpec((tm, wd), lambda i: (i, 0)) for wd in widths],
        out_shape=out_shape,
        compiler_params=_cparams(("parallel",), 56),
        name="inproj",
    )(x2, sh, sc, g, w)


def _conv_kernel(*refs, grid_w, use_rows, tm, cw):
    if use_rows:
        prev_ref, main_ref, next_ref, w_ref, gates_ref, par_ref, q_ref, k_ref, v_ref, go_ref = refs
    else:
        main_ref, w_ref, gates_ref, par_ref, q_ref, k_ref, v_ref, go_ref = refs
    r = pl.program_id(1)
    nr = pl.num_programs(1)
    t = lax.broadcasted_iota(I32, (tm, 1), 0)
    col = jnp.bitwise_and(t, grid_w - 1)
    m_left = (col != 0).astype(F32)
    m_right = (col != grid_w - 1).astype(F32)
    has_prev = (r > 0).astype(F32)
    has_next = (r < nr - 1).astype(F32)
    for c0 in range(0, QKV_DIM, cw):
        xm = main_ref[0, :, c0:c0 + cw].astype(F32)
        if use_rows:
            xp = prev_ref[0, :, c0:c0 + cw].astype(F32) * has_prev
            xn = next_ref[0, :, c0:c0 + cw].astype(F32) * has_next
            up = jnp.concatenate([xp, xm[:tm - grid_w]], axis=0)
            dn = jnp.concatenate([xm[grid_w:], xn], axis=0)

        def colsum(kc):
            y = xm * w_ref[3 + kc:4 + kc, c0:c0 + cw]
            if use_rows:
                y = y + up * w_ref[kc:kc + 1, c0:c0 + cw] + dn * w_ref[6 + kc:7 + kc, c0:c0 + cw]
            return y

        acc = (colsum(1) + pltpu.roll(colsum(0), 1, axis=0) * m_left
               + pltpu.roll(colsum(2), tm - 1, axis=0) * m_right)
        s = acc * _sigmoid(acc)
        for h0 in range(0, cw, HEAD_DIM):
            c = c0 + h0
            seg = s[:, h0:h0 + HEAD_DIM]
            if c < 2 * QK_DIM:
                seg = seg * lax.rsqrt(jnp.sum(seg * seg, axis=-1, keepdims=True) + EPS)
            if c < QK_DIM:
                q_ref[0, :, c:c + HEAD_DIM] = (seg * HEAD_DIM ** -0.5).astype(q_ref.dtype)
            elif c < 2 * QK_DIM:
                k_ref[0, :, c - QK_DIM:c - QK_DIM + HEAD_DIM] = seg.astype(k_ref.dtype)
            else:
                v_ref[0, :, c - 2 * QK_DIM:c - 2 * QK_DIM + HEAD_DIM] = seg.astype(v_ref.dtype)
    g = gates_ref[0]
    a = g + par_ref[1:2, :]
    softplus = jnp.maximum(a, 0.0) + jnp.log1p(jnp.exp(-jnp.abs(a)))
    log_g = -jnp.exp(par_ref[0:1, :]) * softplus
    lane = lax.broadcasted_iota(I32, g.shape, 1)
    go_ref[0] = jnp.where(lane < GATE_LANE0, _sigmoid(g), log_g)


def _conv_call(qkv, conv_w, gates, par, grid_w, use_rows, tm):
    b, t, c = qkv.shape
    kern = functools.partial(_conv_kernel, grid_w=grid_w, use_rows=use_rows, tm=tm, cw=512)
    per = tm // grid_w
    nrow = t // grid_w
    in_specs = []
    args = []
    if use_rows:
        in_specs.append(pl.BlockSpec((1, grid_w, c), lambda i, r: (i, jnp.maximum(r * per - 1, 0), 0)))
        args.append(qkv)
    in_specs.append(pl.BlockSpec((1, tm, c), lambda i, r: (i, r, 0)))
    args.append(qkv)
    if use_rows:
        in_specs.append(pl.BlockSpec((1, grid_w, c), lambda i, r: (i, jnp.minimum((r + 1) * per, nrow - 1), 0)))
        args.append(qkv)
    in_specs += [pl.BlockSpec(conv_w.shape, lambda i, r: (0, 0)),
                 pl.BlockSpec((1, tm, LANES), lambda i, r: (i, r, 0)),
                 pl.BlockSpec(par.shape, lambda i, r: (0, 0))]
    args += [conv_w, gates, par]
    out_shape = [jax.ShapeDtypeStruct((b, t, QK_DIM), BF16), jax.ShapeDtypeStruct((b, t, QK_DIM), BF16),
                 jax.ShapeDtypeStruct((b, t, V_DIM), BF16), jax.ShapeDtypeStruct((b, t, LANES), F32)]
    out_specs = [pl.BlockSpec((1, tm, QK_DIM), lambda i, r: (i, r, 0)),
                 pl.BlockSpec((1, tm, QK_DIM), lambda i, r: (i, r, 0)),
                 pl.BlockSpec((1, tm, V_DIM), lambda i, r: (i, r, 0)),
                 pl.BlockSpec((1, tm, LANES), lambda i, r: (i, r, 0))]
    return pl.pallas_call(
        kern, grid=(b, t // tm), in_specs=in_specs, out_specs=out_specs, out_shape=out_shape,
        compiler_params=_cparams(("parallel", "parallel"), 48),
        name="conv_rows" if use_rows else "conv_seq",
    )(*args)


def _gdn_seq_chunk(q, k, v, kk, qk, beta_c, gc_c, gc_r, ge, s, reverse, prec):
    c = q.shape[0]
    row = lax.broadcasted_iota(I32, (c, c), 0)
    colj = lax.broadcasted_iota(I32, (c, c), 1)
    incl = (colj >= row) if reverse else (colj <= row)
    strict = (colj > row) if reverse else (colj < row)
    decay = jnp.where(incl, jnp.exp(jnp.where(incl, gc_c - gc_r, 0.0)), 0.0)
    a = jnp.where(strict, beta_c * kk * decay, 0.0)
    p = -a
    tinv = jnp.where(row == colj, 1.0, 0.0) + p
    for _ in range(int(math.log2(c)) - 1):
        p = _mm(p, p, prec)
        tinv = tinv + _mm(tinv, p, prec)
    egc = jnp.exp(gc_c)
    rhs = jnp.concatenate([beta_c * v, (beta_c * egc) * k], axis=1)
    sol = _mm(tinv, rhs, prec)
    u_new, w = sol[:, :HEAD_DIM], sol[:, HEAD_DIM:]
    ws = _mm(jnp.concatenate([w, q * egc], axis=0), s)
    u = u_new - ws[:c]
    o = ws[c:] + _mm(qk * decay, u)
    k_dec = k * jnp.exp(ge - gc_c)
    s_new = jnp.exp(ge) * s + _mm(k_dec, u, dims=(((0,), (0,)), ((), ())))
    return o, s_new


def _gdn_kernel(qf, kf, vf, gf, rf, qb, kb, vb, gb, rb, s0_ref, of, ob, sfin_ref, s_ref, *, prec):
    i = pl.program_id(2)
    nc = pl.num_programs(2)

    @pl.when(i == 0)
    def _():
        s_ref[...] = s0_ref[0]

    c = qf.shape[1]
    row = lax.broadcasted_iota(I32, (c, c), 0)
    colj = lax.broadcasted_iota(I32, (c, c), 1)
    for d, (q_r, k_r, v_r, g_r, r_r, o_r) in enumerate(((qf, kf, vf, gf, rf, of), (qb, kb, vb, gb, rb, ob))):
        rev = d == 1
        q = q_r[0].astype(F32)
        k = k_r[0].astype(F32)
        kq = _mm(jnp.concatenate([q, k], axis=0), k, dims=(((1,), (1,)), ((), ())))
        qk, kk = kq[:c], kq[c:]
        gates = g_r[0, 0]
        tri_c = jnp.where((colj >= row) if rev else (colj <= row), 1.0, 0.0)
        gcm = _mm(tri_c, gates, HIGHEST)
        gcr = _mm(r_r[0, 0, 0], tri_c, HIGHEST, dims=(((1,), (1,)), ((), ())))
        outs = []
        for j in range(2):
            idx = d * 2 + j
            gc_c = gcm[:, GATE_LANE0 + idx:GATE_LANE0 + idx + 1]
            ge = gc_c[0:1] if rev else gc_c[c - 1:c]
            o, s_new = _gdn_seq_chunk(
                q, k, v_r[0, :, j * HEAD_DIM:(j + 1) * HEAD_DIM].astype(F32), kk, qk,
                gates[:, idx:idx + 1], gc_c, gcr[idx:idx + 1, :], ge, s_ref[d, j], rev, prec)
            s_ref[d, j] = s_new
            outs.append(o)
        o_r[0] = jnp.concatenate(outs, axis=1).astype(o_r.dtype)

    @pl.when(i == nc - 1)
    def _():
        sfin_ref[0] = s_ref[...]


def _gdn_call(q, k, v, gates_g, rows_g, s0, prec):
    b, t, _ = q.shape
    c = GDN_CHUNK
    nc = t // c
    fwd = lambda i, h, n: (i, n, h)
    bwd = lambda i, h, n: (i, nc - 1 - n, h)
    gfwd = lambda i, h, n: (i, h, n, 0)
    gbwd = lambda i, h, n: (i, h, nc - 1 - n, 0)
    rfwd = lambda i, h, n: (i, h, n, 0, 0)
    rbwd = lambda i, h, n: (i, h, nc - 1 - n, 0, 0)
    state_spec = pl.BlockSpec((1, 2, 2, HEAD_DIM, HEAD_DIM), lambda i, h, n: (i, 0, h, 0, 0))

    def specs(m3, mg, mr):
        return [pl.BlockSpec((1, c, HEAD_DIM), m3), pl.BlockSpec((1, c, HEAD_DIM), m3),
                pl.BlockSpec((1, c, 2 * HEAD_DIM), m3), pl.BlockSpec((1, 1, c, LANES), mg),
                pl.BlockSpec((1, 1, 1, 8, c), mr)]

    return pl.pallas_call(
        functools.partial(_gdn_kernel, prec=prec),
        grid=(b, NQK_HEADS, nc),
        in_specs=specs(fwd, gfwd, rfwd) + specs(bwd, gbwd, rbwd) + [state_spec],
        out_specs=[pl.BlockSpec((1, c, 2 * HEAD_DIM), fwd), pl.BlockSpec((1, c, 2 * HEAD_DIM), bwd), state_spec],
        out_shape=[jax.ShapeDtypeStruct((b, t, V_DIM), BF16), jax.ShapeDtypeStruct((b, t, V_DIM), BF16),
                   jax.ShapeDtypeStruct(s0.shape, F32)],
        scratch_shapes=[pltpu.VMEM((2, 2, HEAD_DIM, HEAD_DIM), F32)],
        compiler_params=_cparams(("parallel", "parallel", "arbitrary"), 32),
        name="gdn",
    )(q, k, v, gates_g, rows_g, q, k, v, gates_g, rows_g, s0)


def _gdn_gate_layouts(go):
    b, t, _ = go.shape
    c = GDN_CHUNK

    def group(x):
        x = x.reshape(b, t, 2, NQK_HEADS, 2)
        return jnp.transpose(x, (0, 3, 1, 2, 4)).reshape(b, NQK_HEADS, t, 4)

    beta = group(go[..., :GATE_LANE0])
    lg = group(go[..., GATE_LANE0:2 * GATE_LANE0])
    z12 = jnp.zeros((b, NQK_HEADS, t, GATE_LANE0 - 4), F32)
    zrest = jnp.zeros((b, NQK_HEADS, t, LANES - GATE_LANE0 - 4), F32)
    gates_g = jnp.concatenate([beta, z12, lg, zrest], axis=-1)
    rows = jnp.transpose(lg.reshape(b, NQK_HEADS, t // c, c, 4), (0, 1, 2, 4, 3))
    rows_g = jnp.concatenate([rows, jnp.zeros_like(rows)], axis=3)
    return gates_g, rows_g


def _fnet1_kernel(x_ref, f_ref, ar_ref, ai_ref):
    a = _mm(f_ref[...], x_ref[0], HIGHEST)
    n = ar_ref.shape[1]
    ar_ref[0] = a[:n]
    ai_ref[0] = a[n:]


def _fnet2_kernel(ar_ref, ai_ref, gc_ref, gs_ref, cc_ref, sc_ref, o_ref, *, mb, scale):
    n = gc_ref.shape[1]
    cdim = ar_ref.shape[2]
    for m in range(mb):
        ar = ar_ref[0, m * n:(m + 1) * n, :]
        ai = ai_ref[0, m * n:(m + 1) * n, :]
        gc = gc_ref[m]
        gs = gs_ref[m]
        zr = _mm(gc, ar, HIGHEST) + _mm(gs, ai, HIGHEST)
        zi = _mm(gc, ai, HIGHEST) - _mm(gs, ar, HIGHEST)
        for g0 in range(0, cdim, HEAD_DIM):
            y = (_mm(zr[:, g0:g0 + HEAD_DIM], cc_ref[...], HIGHEST)
                 + _mm(zi[:, g0:g0 + HEAD_DIM], sc_ref[...], HIGHEST))
            o_ref[0, :, m * cdim + g0:m * cdim + g0 + HEAD_DIM] = (y * scale).astype(o_ref.dtype)


def _fnet_tables(n):
    a = np.arange(n)
    ang1 = 2.0 * np.pi * np.outer(a, a) / n
    f1 = np.concatenate([np.cos(ang1), -np.sin(ang1)], axis=0)
    m = a[:, None] + n * a[None, :]
    ang2 = 2.0 * np.pi * ((m[:, :, None] * a[None, None, :]) % (n * n)) / (n * n)
    angc = 2.0 * np.pi * np.outer(np.arange(HEAD_DIM), np.arange(HEAD_DIM)) / HEAD_DIM
    f = lambda x: jnp.asarray(x, F32)
    return f(f1), f(np.cos(ang2)), f(np.sin(ang2)), f(np.cos(angc)), f(np.sin(angc))


def _fnet_call(f):
    b, l, c = f.shape
    n = GRID_W
    assert l == n * n
    f1, gcos, gsin, ccos, csin = _fnet_tables(n)
    tn = 4096
    ar, ai = pl.pallas_call(
        _fnet1_kernel,
        grid=(b, n * c // tn),
        in_specs=[pl.BlockSpec((1, n, tn), lambda i, j: (i, 0, j)),
                  pl.BlockSpec((2 * n, n), lambda i, j: (0, 0))],
        out_specs=[pl.BlockSpec((1, n, tn), lambda i, j: (i, 0, j))] * 2,
        out_shape=[jax.ShapeDtypeStruct((b, n, n * c), F32)] * 2,
        compiler_params=_cparams(("parallel", "parallel"), 32),
        name="fnet1",
    )(f.reshape(b, n, n * c), f1)
    mb = 8
    out = pl.pallas_call(
        functools.partial(_fnet2_kernel, mb=mb, scale=1.0 / math.sqrt(l * HEAD_DIM)),
        grid=(b, n // mb),
        in_specs=[pl.BlockSpec((1, n * mb, c), lambda i, j: (i, j, 0)),
                  pl.BlockSpec((1, n * mb, c), lambda i, j: (i, j, 0)),
                  pl.BlockSpec((mb, n, n), lambda i, j: (j, 0, 0)),
                  pl.BlockSpec((mb, n, n), lambda i, j: (j, 0, 0)),
                  pl.BlockSpec((HEAD_DIM, HEAD_DIM), lambda i, j: (0, 0)),
                  pl.BlockSpec((HEAD_DIM, HEAD_DIM), lambda i, j: (0, 0))],
        out_specs=pl.BlockSpec((1, n, c * mb), lambda i, j: (i, 0, j)),
        out_shape=jax.ShapeDtypeStruct((b, n, n * c), BF16),
        compiler_params=_cparams(("parallel", "parallel"), 32),
        name="fnet2",
    )(ar.reshape(b, l, c), ai.reshape(b, l, c), gcos, gsin, ccos, csin)
    return out.reshape(b, l, c)


def _merge_kernel(of_ref, ob_ref, z_ref, fm_ref, gab_ref, x_ref, g1_ref, sh2_ref, sc2_ref, gn_ref, n2_ref,
                  wg_ref, wf_ref, wm_ref, wr_ref, br_ref, x1_ref, h2_ref, lg_ref):
    d = x_ref.shape[1]
    o = of_ref[...].astype(F32) + ob_ref[...].astype(F32)
    z = z_ref[...].astype(F32)
    parts = []
    for h0 in range(0, V_DIM, HEAD_DIM):
        oh = o[:, h0:h0 + HEAD_DIM]
        parts.append(oh * lax.rsqrt(jnp.mean(oh * oh, axis=-1, keepdims=True) + EPS) * gn_ref[...])
    yb_in = jnp.concatenate(parts, axis=1) * (z * _sigmoid(z))
    yb = _mm(yb_in, wg_ref[...])
    ya = _mm(fm_ref[...], wf_ref[...])
    ga = _sigmoid(gab_ref[:, :d].astype(F32))
    gb = _sigmoid(gab_ref[:, d:].astype(F32))
    mm = _mm(ga * ya + gb * yb, wm_ref[...])
    x1 = x_ref[...] + g1_ref[0] * mm
    x1_ref[...] = x1
    h2 = _rmsnorm(x1, n2_ref[...]) * (1.0 + sc2_ref[0]) + sh2_ref[0]
    h2_ref[...] = h2
    lg_ref[...] = _mm(h2, wr_ref[...], HIGHEST) + br_ref[...]


def _merge_call(of, ob, z, fm, gab, x2, g1, sh2, sc2, gn, n2, wg, wf, wm, wr, br, tokens_per_batch, tm):
    n, d = x2.shape
    per = tokens_per_batch // tm
    tok = lambda wd: pl.BlockSpec((tm, wd), lambda i: (i, 0))
    vec = pl.BlockSpec((1, 1, d), lambda i: (i // per, 0, 0))
    full = lambda a: pl.BlockSpec(a.shape, lambda i: (0,) * a.ndim)
    return pl.pallas_call(
        _merge_kernel,
        grid=(n // tm,),
        in_specs=[tok(V_DIM), tok(V_DIM), tok(V_DIM), tok(F_DIM), tok(2 * d), tok(d), vec, vec, vec,
                  full(gn), full(n2), full(wg), full(wf), full(wm), full(wr), full(br)],
        out_specs=[tok(d), tok(d), tok(LANES)],
        out_shape=[jax.ShapeDtypeStruct((n, d), F32), jax.ShapeDtypeStruct((n, d), F32),
                   jax.ShapeDtypeStruct((n, LANES), F32)],
        compiler_params=_cparams(("parallel",), 56),
        name="merge",
    )(of, ob, z, fm, gab, x2, g1, sh2, sc2, gn, n2, wg, wf, wm, wr, br)


def _route_kernel(lg_ref, idx_ref, w_ref, rank_ref, cnt_ref, run_ref):
    i = pl.program_id(0)

    @pl.when(i == 0)
    def _():
        run_ref[...] = jnp.zeros_like(run_ref)

    l = lg_ref[...]
    tm = l.shape[0]
    lane = lax.broadcasted_iota(I32, l.shape, 1)
    vals, idxs = [], []
    for _ in range(TOP_K):
        m = jnp.max(l, axis=-1, keepdims=True)
        idx = jnp.min(jnp.where(l == m, lane, LANES), axis=-1, keepdims=True)
        vals.append(m)
        idxs.append(idx)
        l = jnp.where(lane == idx, NEG_BIG * 2.0, l)
    es = [jnp.exp(v - vals[0]) for v in vals]
    inv = 1.0 / (es[0] + es[1] + es[2] + es[3])
    picked = jnp.zeros(l.shape, F32)
    for idx in idxs:
        picked = picked + (lane == idx).astype(F32)
    r = lax.broadcasted_iota(I32, (tm, tm), 0)
    cidx = lax.broadcasted_iota(I32, (tm, tm), 1)
    before = _mm(jnp.where(cidx < r, 1.0, 0.0), picked) + run_ref[...]
    idx_out = jnp.zeros(l.shape, I32)
    w_out = jnp.zeros(l.shape, F32)
    rank_out = jnp.zeros(l.shape, F32)
    for k in range(TOP_K):
        rk = jnp.sum(jnp.where(lane == idxs[k], before, 0.0), axis=-1, keepdims=True)
        idx_out = jnp.where(lane == k, idxs[k], idx_out)
        w_out = jnp.where(lane == k, es[k] * inv, w_out)
        rank_out = jnp.where(lane == k, rk, rank_out)
    idx_ref[...] = idx_out
    w_ref[...] = w_out
    rank_ref[...] = rank_out.astype(I32)
    run_ref[...] = run_ref[...] + jnp.sum(picked, axis=0, keepdims=True)
    cnt_ref[...] = run_ref[...]


def _route_call(logits, tm):
    n = logits.shape[0]
    tok = pl.BlockSpec((tm, LANES), lambda i: (i, 0))
    return pl.pallas_call(
        _route_kernel,
        grid=(n // tm,),
        in_specs=[tok],
        out_specs=[tok, tok, tok, pl.BlockSpec((1, LANES), lambda i: (0, 0))],
        out_shape=[jax.ShapeDtypeStruct((n, LANES), I32), jax.ShapeDtypeStruct((n, LANES), F32),
                   jax.ShapeDtypeStruct((n, LANES), I32), jax.ShapeDtypeStruct((1, LANES), F32)],
        scratch_shapes=[pltpu.VMEM((1, LANES), F32)],
        compiler_params=_cparams(("arbitrary",), 32),
        name="route",
    )(logits)


def _expert_kernel(be_ref, nv_ref, src_ref, nxt_ref, dst_ref, h_hbm, wg_ref, wu_ref, wd_ref, bg_ref, bu_ref, bd_ref,
                   y_hbm, xbuf, ybuf, wgb, wub, wdb, sem_in, sem_out):
    i = pl.program_id(0)
    nb = pl.num_programs(0)
    nv = nv_ref[0]
    tmb = xbuf.shape[1]
    slot = lax.rem(i, 2)

    def gather(idx_ref, s):
        def body(r, carry):
            pltpu.make_async_copy(h_hbm.at[pl.ds(idx_ref[0, 0, r], 1)], xbuf.at[s, pl.ds(r, 1)], sem_in.at[s]).start()
            return carry
        lax.fori_loop(0, tmb, body, 0)

    def wait_scatter():
        pltpu.make_async_copy(ybuf, ybuf, sem_out.at[0]).wait()

    @pl.when(i == 0)
    def _():
        gather(src_ref, 0)
        ybuf[...] = jnp.zeros_like(ybuf)
        spare = pltpu.make_async_copy(ybuf, y_hbm.at[pl.ds(y_hbm.shape[0] - tmb, tmb)], sem_out.at[0])
        spare.start()
        spare.wait()

    @pl.when(i + 1 < nv)
    def _():
        gather(nxt_ref, 1 - slot)

    @pl.when((i >= 1) & (i <= nv))
    def _():
        wait_scatter()

    @pl.when(i < nv)
    def _():
        changed = jnp.logical_or(i == 0, be_ref[i] != be_ref[jnp.maximum(i - 1, 0)])

        @pl.when(changed)
        def _():
            wgb[...] = wg_ref[0].astype(BF16)
            wub[...] = wu_ref[0].astype(BF16)
            wdb[...] = wd_ref[0].astype(BF16)

        pltpu.make_async_copy(xbuf.at[slot], xbuf.at[slot], sem_in.at[slot]).wait()
        x = xbuf[slot].astype(BF16)
        gate = jnp.minimum(_mm(x, wgb[...]) + bg_ref[0], SWIGLU_LIMIT)
        up = jnp.clip(_mm(x, wub[...]) + bu_ref[0], -SWIGLU_LIMIT, SWIGLU_LIMIT)
        act = (up + 1.0) * gate * _sigmoid(SWIGLU_ALPHA * gate)
        ybuf[...] = _mm(act, wdb[...]) + bd_ref[0]

        def body(r, carry):
            pltpu.make_async_copy(ybuf.at[pl.ds(r, 1)], y_hbm.at[pl.ds(dst_ref[0, 0, r], 1)], sem_out.at[0]).start()
            return carry
        lax.fori_loop(0, tmb, body, 0)

        @pl.when(i == nb - 1)
        def _():
            wait_scatter()


def _expert_call(block_e, n_valid, slot_src, slot_dst, h2, w_gate, w_up, w_down, b_gate, b_up, b_down, n_rows_out):
    nb = block_e.shape[0]
    tmb = EXPERT_BLOCK
    n, d = h2.shape
    de = w_gate.shape[2]
    src3 = slot_src.reshape(nb, 1, tmb)
    dst3 = slot_dst.reshape(nb, 1, tmb)
    smem_blk = lambda m: pl.BlockSpec((1, 1, tmb), m, memory_space=pltpu.SMEM)
    wspec = lambda s: pl.BlockSpec((1,) + s, lambda i, be, nv: (be[i], 0, 0))
    grid_spec = pltpu.PrefetchScalarGridSpec(
        num_scalar_prefetch=2,
        grid=(nb,),
        in_specs=[smem_blk(lambda i, be, nv: (i, 0, 0)),
                  smem_blk(lambda i, be, nv: (jnp.minimum(i + 1, nb - 1), 0, 0)),
                  smem_blk(lambda i, be, nv: (i, 0, 0)),
                  pl.BlockSpec(memory_space=pl.ANY),
                  wspec((d, de)), wspec((d, de)), wspec((de, d)),
                  wspec((1, de)), wspec((1, de)), wspec((1, d))],
        out_specs=pl.BlockSpec(memory_space=pl.ANY),
        scratch_shapes=[pltpu.VMEM((2, tmb, d), F32), pltpu.VMEM((tmb, d), F32),
                        pltpu.VMEM((d, de), BF16), pltpu.VMEM((d, de), BF16), pltpu.VMEM((de, d), BF16),
                        pltpu.SemaphoreType.DMA((2,)), pltpu.SemaphoreType.DMA((1,))],
    )
    ne = w_gate.shape[0]
    return pl.pallas_call(
        _expert_kernel,
        grid_spec=grid_spec,
        out_shape=jax.ShapeDtypeStruct((n_rows_out, d), F32),
        compiler_params=_cparams(("arbitrary",), 56),
        name="expert",
    )(block_e, n_valid, src3, src3, dst3, h2, w_gate, w_up, w_down,
      b_gate.reshape(ne, 1, de), b_up.reshape(ne, 1, de), b_down.reshape(ne, 1, d))


def _combine_kernel(y0, y1, y2, y3, w_ref, x1_ref, g2_ref, fg_ref, o_ref):
    w = w_ref[...]
    moe = (w[:, 0:1] * y0[...] + w[:, 1:2] * y1[...]) + (w[:, 2:3] * y2[...] + w[:, 3:4] * y3[...])
    o_ref[...] = _rmsnorm(x1_ref[...] + g2_ref[0] * moe, fg_ref[...])


def _combine_call(y4, top_w, x1, g2, fg, tokens_per_batch, tm):
    n, d = x1.shape
    per = tokens_per_batch // tm
    nt = n // tm
    yspec = lambda k: pl.BlockSpec((tm, d), lambda i: (k * nt + i, 0))
    return pl.pallas_call(
        _combine_kernel,
        grid=(nt,),
        in_specs=[yspec(0), yspec(1), yspec(2), yspec(3),
                  pl.BlockSpec((tm, LANES), lambda i: (i, 0)),
                  pl.BlockSpec((tm, d), lambda i: (i, 0)),
                  pl.BlockSpec((1, 1, d), lambda i: (i // per, 0, 0)),
                  pl.BlockSpec((1, d), lambda i: (0, 0))],
        out_specs=pl.BlockSpec((tm, d), lambda i: (i, 0)),
        out_shape=jax.ShapeDtypeStruct((n, d), F32),
        compiler_params=_cparams(("parallel",), 48),
        name="combine",
    )(y4, y4, y4, y4, top_w, x1, g2, fg)


def _routing_tables(top_idx, rank, counts, n):
    tmb = EXPERT_BLOCK
    counts = counts.astype(I32)
    padded = (counts + tmb - 1) // tmb * tmb
    pad_end = jnp.cumsum(padded)
    pad_start = pad_end - padded
    n_blocks = -(-(n * TOP_K + N_EXPERTS * (tmb - 1)) // tmb)
    n_slots = n_blocks * tmb
    onehot = top_idx[:, :, None] == jnp.arange(N_EXPERTS, dtype=I32)[None, None, :]
    dest = jnp.sum(jnp.where(onehot, pad_start[None, None, :], 0), axis=-1) + rank
    flat = jnp.arange(TOP_K, dtype=I32)[None, :] * n + jnp.arange(n, dtype=I32)[:, None]
    slot_flat = jnp.full((n_slots,), -1, I32).at[dest.reshape(-1)].set(flat.reshape(-1))
    used = slot_flat >= 0
    slot_src = jnp.where(used, slot_flat % n, 0)
    slot_dst = jnp.where(used, slot_flat, n * TOP_K + jnp.arange(n_slots, dtype=I32) % tmb)
    block_start = jnp.arange(n_blocks, dtype=I32) * tmb
    block_e = jnp.minimum(jnp.sum((pad_end[None, :] <= block_start[:, None]).astype(I32), axis=1), N_EXPERTS - 1)
    n_valid = (pad_end[-1:] // tmb).astype(I32)
    return block_e, n_valid, slot_src, slot_dst


def _gdn_branch(xtok, sh, sc, norm_g, w_cols, widths, dtypes, conv_w, par, tokens_per_batch, grid_w, use_rows,
                tm_proj, tm_conv, s0, prec):
    b = xtok.shape[0] // tokens_per_batch
    outs = _inproj_call(xtok, sh, sc, norm_g, w_cols, widths, dtypes, tokens_per_batch, tm_proj)
    qkv, gates = outs[0], outs[-1]
    q, k, v, go = _conv_call(qkv.reshape(b, tokens_per_batch, QKV_DIM), conv_w,
                             gates.reshape(b, tokens_per_batch, LANES), par, grid_w, use_rows, tm_conv)
    gates_g, rows_g = _gdn_gate_layouts(go)
    o_f, o_b, s_fin = _gdn_call(q, k, v, gates_g, rows_g, s0, prec)
    return outs, o_f, o_b, s_fin


def kernel(x, c, ctx, c_ctx, w_mod, b_mod, norm1_g, norm2_g, w_in, conv_w, a_log, dt_bias, gdn_norm_g,
           w_fourier_out, w_gdn_out, w_merge_out, w_router, b_router, w_gate, b_gate, w_up, b_up,
           w_down, b_down, final_norm_g):
    b, l, d = x.shape
    n = b * l
    n_ctx = ctx.shape[1]
    assert w_mod.shape[0] == 1 and l == GRID_W * GRID_W and d == V_DIM
    prec = HIGHEST

    c8 = jnp.concatenate([c, c_ctx[None, :], jnp.zeros((8 - b - 1, d), F32)], axis=0)
    mod = _mod_call(c8, w_mod[0], b_mod[0])
    sh1, sc1, g1, sh2, sc2, g2 = [mod[:b, j * d:(j + 1) * d].reshape(b, 1, d) for j in range(6)]
    csh1 = jnp.broadcast_to(mod[b:b + 1, 0:d].reshape(1, 1, d), (b, 1, d))
    csc1 = jnp.broadcast_to(mod[b:b + 1, d:2 * d].reshape(1, 1, d), (b, 1, d))

    wi = w_in[0]
    off_gate = QKV_DIM
    off_z = off_gate + 4 * NV_HEADS
    off_f = off_z + V_DIM
    off_ga = off_f + F_DIM
    gate_cols = jnp.pad(wi[:, off_gate:off_z], ((0, 0), (0, LANES - 4 * NV_HEADS)))
    w_lat = jnp.concatenate([wi[:, :QKV_DIM], wi[:, off_z:off_f], wi[:, off_f:off_ga], wi[:, off_ga:], gate_cols],
                            axis=1).astype(BF16)
    w_ctx = jnp.concatenate([wi[:, :QKV_DIM], gate_cols], axis=1).astype(BF16)
    par = jnp.pad(jnp.stack([a_log[0].reshape(-1), dt_bias[0].reshape(-1)]),
                  ((0, 6), (GATE_LANE0, LANES - 2 * GATE_LANE0)))
    n1 = norm1_g[0].reshape(1, d)
    cw = conv_w[0].reshape(9, QKV_DIM)

    zero_state = jnp.zeros((b, 2, NV_HEADS, HEAD_DIM, HEAD_DIM), F32)
    _, _, _, s_ctx = _gdn_branch(ctx.reshape(b * n_ctx, d), csh1, csc1, n1, w_ctx, (QKV_DIM, LANES), (BF16, F32),
                                 cw, par, n_ctx, n_ctx, False, n_ctx, n_ctx, zero_state, prec)

    x2 = x.reshape(n, d)
    outs, o_f, o_b, _ = _gdn_branch(x2, sh1, sc1, n1, w_lat, (QKV_DIM, V_DIM, F_DIM, 2 * d, LANES),
                                    (BF16, BF16, F32, BF16, F32), cw, par, l, GRID_W, True, 512, 512, s_ctx, prec)
    _, z, f, gab, _ = outs
    fmix = _fnet_call(f.reshape(b, l, F_DIM)).reshape(n, F_DIM)

    wr = jnp.pad(w_router[0], ((0, 0), (0, LANES - N_EXPERTS)))
    br = jnp.pad(b_router[0], (0, LANES - N_EXPERTS), constant_values=NEG_BIG).reshape(1, LANES)
    x1, h2, logits = _merge_call(
        o_f.reshape(n, V_DIM), o_b.reshape(n, V_DIM), z, fmix, gab, x2, g1, sh2, sc2,
        gdn_norm_g[0].reshape(1, HEAD_DIM), norm2_g[0].reshape(1, d),
        w_gdn_out[0].astype(BF16), w_fourier_out[0].astype(BF16), w_merge_out[0].astype(BF16), wr, br, l, 256)

    top_idx, top_w, rank, counts = _route_call(logits, 512)
    block_e, n_valid, slot_src, slot_dst = _routing_tables(top_idx[:, :TOP_K], rank[:, :TOP_K],
                                                           counts[0, :N_EXPERTS], n)
    y4 = _expert_call(block_e, n_valid, slot_src, slot_dst, h2, w_gate[0], w_up[0], w_down[0],
                      b_gate[0], b_up[0], b_down[0], n * TOP_K + EXPERT_BLOCK)
    out = _combine_call(y4, top_w, x1, g2, final_norm_g.reshape(1, d), l, 256)
    return out.reshape(b, l, d)
```

```python
import functools
import math

import jax
import jax.numpy as jnp
import numpy as np
from jax import lax
from jax.experimental import pallas as pl
from jax.experimental.pallas import tpu as pltpu

F32 = jnp.float32
BF16 = jnp.bfloat16
I32 = jnp.int32
HIGHEST = lax.Precision.HIGHEST

GRID_W = 64
NQK_HEADS = 4
NV_HEADS = 8
HEAD_DIM = 128
QK_DIM = NQK_HEADS * HEAD_DIM
V_DIM = NV_HEADS * HEAD_DIM
QKV_DIM = 2 * QK_DIM + V_DIM
F_GROUPS = 4
F_DIM = F_GROUPS * HEAD_DIM
N_EXPERTS = 32
TOP_K = 4
SWIGLU_ALPHA = 1.702
SWIGLU_LIMIT = 7.0
EPS = 1e-6

LANES = 128
GATE_LANE0 = 16
GDN_CHUNK = 128
EXPERT_BLOCK = 256
NEG_BIG = -1e30
MIB = 2 ** 20


def _cparams(sem, vmem_mib):
    return pltpu.CompilerParams(dimension_semantics=sem, vmem_limit_bytes=vmem_mib * MIB)


def _mm(a, b, prec=None, dims=(((1,), (0,)), ((), ()))):
    if prec is None:
        return lax.dot_general(a.astype(BF16), b.astype(BF16), dims, preferred_element_type=F32)
    return lax.dot_general(a.astype(F32), b.astype(F32), dims, precision=prec, preferred_element_type=F32)


def _sigmoid(x):
    return 1.0 / (1.0 + jnp.exp(-x))


def _rmsnorm(x, g):
    return x * lax.rsqrt(jnp.mean(x * x, axis=-1, keepdims=True) + EPS) * g


def _mod_kernel(c_ref, w_ref, b_ref, o_ref):
    c = c_ref[...]
    o_ref[...] = _mm(c * _sigmoid(c), w_ref[...], HIGHEST) + b_ref[...]


def _mod_call(c8, w_mod, b_mod):
    d, n = w_mod.shape
    tn = 1536
    return pl.pallas_call(
        _mod_kernel,
        grid=(n // tn,),
        in_specs=[pl.BlockSpec((8, d), lambda j: (0, 0)),
                  pl.BlockSpec((d, tn), lambda j: (0, j)),
                  pl.BlockSpec((1, tn), lambda j: (0, j))],
        out_specs=pl.BlockSpec((8, tn), lambda j: (0, j)),
        out_shape=jax.ShapeDtypeStruct((8, n), F32),
        compiler_params=_cparams(("parallel",), 32),
        name="mod",
    )(c8, w_mod, b_mod.reshape(1, n))


def _inproj_kernel(x_ref, sh_ref, sc_ref, g_ref, w_ref, *o_refs, widths):
    u = (_rmsnorm(x_ref[...], g_ref[...]) * (1.0 + sc_ref[0]) + sh_ref[0]).astype(BF16)
    start = 0
    for o_ref, width in zip(o_refs, widths):
        step = min(width, 512)
        for c0 in range(0, width, step):
            o_ref[:, c0:c0 + step] = _mm(u, w_ref[:, start + c0:start + c0 + step]).astype(o_ref.dtype)
        start += width


def _inproj_call(x2, sh, sc, g, w, widths, dtypes, tokens_per_batch, tm):
    n, d = x2.shape
    per = tokens_per_batch // tm
    out_shape = [jax.ShapeDtypeStruct((n, wd), dt) for wd, dt in zip(widths, dtypes)]
    return pl.pallas_call(
        functools.partial(_inproj_kernel, widths=widths),
        grid=(n // tm,),
        in_specs=[pl.BlockSpec((tm, d), lambda i: (i, 0)),
                  pl.BlockSpec((1, 1, d), lambda i: (i // per, 0, 0)),
                  pl.BlockSpec((1, 1, d), lambda i: (i // per, 0, 0)),
                  pl.BlockSpec((1, d), lambda i: (0, 0)),
                  pl.BlockSpec(w.shape, lambda i: (0, 0))],
        out_specs=[pl.BlockSpec((tm, wd), lambda i: (i, 0)) for wd in widths],
        out_shape=out_shape,
        compiler_params=_cparams(("parallel",), 56),
        name="inproj",
    )(x2, sh, sc, g, w)


def _conv_kernel(*refs, grid_w, use_rows, tm, cw):
    if use_rows:
        prev_ref, main_ref, next_ref, w_ref, gates_ref, par_ref, q_ref, k_ref, v_ref, go_ref = refs
    else:
        main_ref, w_ref, gates_ref, par_ref, q_ref, k_ref, v_ref, go_ref = refs
    r = pl.program_id(1)
    nr = pl.num_programs(1)
    t = lax.broadcasted_iota(I32, (tm, 1), 0)
    col = jnp.bitwise_and(t, grid_w - 1)
    m_left = (col != 0).astype(F32)
    m_right = (col != grid_w - 1).astype(F32)
    has_prev = (r > 0).astype(F32)
    has_next = (r < nr - 1).astype(F32)
    for c0 in range(0, QKV_DIM, cw):
        xm = main_ref[0, :, c0:c0 + cw].astype(F32)
        if use_rows:
            xp = prev_ref[0, :, c0:c0 + cw].astype(F32) * has_prev
            xn = next_ref[0, :, c0:c0 + cw].astype(F32) * has_next
            up = jnp.concatenate([xp, xm[:tm - grid_w]], axis=0)
            dn = jnp.concatenate([xm[grid_w:], xn], axis=0)

        def colsum(kc):
            y = xm * w_ref[3 + kc:4 + kc, c0:c0 + cw]
            if use_rows:
                y = y + up * w_ref[kc:kc + 1, c0:c0 + cw] + dn * w_ref[6 + kc:7 + kc, c0:c0 + cw]
            return y

        acc = (colsum(1) + pltpu.roll(colsum(0), 1, axis=0) * m_left
               + pltpu.roll(colsum(2), tm - 1, axis=0) * m_right)
        s = acc * _sigmoid(acc)
        for h0 in range(0, cw, HEAD_DIM):
            c = c0 + h0
            seg = s[:, h0:h0 + HEAD_DIM]
            if c < 2 * QK_DIM:
                seg = seg * lax.rsqrt(jnp.sum(seg * seg, axis=-1, keepdims=True) + EPS)
            if c < QK_DIM:
                q_ref[0, :, c:c + HEAD_DIM] = (seg * HEAD_DIM ** -0.5).astype(q_ref.dtype)
            elif c < 2 * QK_DIM:
                k_ref[0, :, c - QK_DIM:c - QK_DIM + HEAD_DIM] = seg.astype(k_ref.dtype)
            else:
                v_ref[0, :, c - 2 * QK_DIM:c - 2 * QK_DIM + HEAD_DIM] = seg.astype(v_ref.dtype)
    g = gates_ref[0]
    a = g + par_ref[1:2, :]
    softplus = jnp.maximum(a, 0.0) + jnp.log1p(jnp.exp(-jnp.abs(a)))
    log_g = -jnp.exp(par_ref[0:1, :]) * softplus
    lane = lax.broadcasted_iota(I32, g.shape, 1)
    go_ref[0] = jnp.where(lane < GATE_LANE0, _sigmoid(g), log_g)


def _conv_call(qkv, conv_w, gates, par, grid_w, use_rows, tm):
    b, t, c = qkv.shape
    kern = functools.partial(_conv_kernel, grid_w=grid_w, use_rows=use_rows, tm=tm, cw=512)
    per = tm // grid_w
    nrow = t // grid_w
    in_specs = []
    args = []
    if use_rows:
        in_specs.append(pl.BlockSpec((1, grid_w, c), lambda i, r: (i, jnp.maximum(r * per - 1, 0), 0)))
        args.append(qkv)
    in_specs.append(pl.BlockSpec((1, tm, c), lambda i, r: (i, r, 0)))
    args.append(qkv)
    if use_rows:
        in_specs.append(pl.BlockSpec((1, grid_w, c), lambda i, r: (i, jnp.minimum((r + 1) * per, nrow - 1), 0)))
        args.append(qkv)
    in_specs += [pl.BlockSpec(conv_w.shape, lambda i, r: (0, 0)),
                 pl.BlockSpec((1, tm, LANES), lambda i, r: (i, r, 0)),
                 pl.BlockSpec(par.shape, lambda i, r: (0, 0))]
    args += [conv_w, gates, par]
    out_shape = [jax.ShapeDtypeStruct((b, t, QK_DIM), BF16), jax.ShapeDtypeStruct((b, t, QK_DIM), BF16),
                 jax.ShapeDtypeStruct((b, t, V_DIM), BF16), jax.ShapeDtypeStruct((b, t, LANES), F32)]
    out_specs = [pl.BlockSpec((1, tm, QK_DIM), lambda i, r: (i, r, 0)),
                 pl.BlockSpec((1, tm, QK_DIM), lambda i, r: (i, r, 0)),
                 pl.BlockSpec((1, tm, V_DIM), lambda i, r: (i, r, 0)),
                 pl.BlockSpec((1, tm, LANES), lambda i, r: (i, r, 0))]
    return pl.pallas_call(
        kern, grid=(b, t // tm), in_specs=in_specs, out_specs=out_specs, out_shape=out_shape,
        compiler_params=_cparams(("parallel", "parallel"), 48),
        name="conv_rows" if use_rows else "conv_seq",
    )(*args)


def _gdn_kernel(qf, kf, vf, gf, rf, qb, kb, vb, gb, rb, s0_ref, of, ob, sfin_ref, s_ref, *, prec):
    i = pl.program_id(1)
    nc = pl.num_programs(1)

    @pl.when(i == 0)
    def _():
        s_ref[...] = s0_ref[0]

    c = qf.shape[1]
    per = NV_HEADS // NQK_HEADS
    row = lax.broadcasted_iota(I32, (c, c), 0)
    colj = lax.broadcasted_iota(I32, (c, c), 1)
    eye = jnp.where(row == colj, 1.0, 0.0)
    nt_dims = (((1,), (1,)), ((), ()))
    tn_dims = (((0,), (0,)), ((), ()))

    seqs = []
    for d, (q_r, k_r, v_r, g_r, r_r, o_r) in enumerate(((qf, kf, vf, gf, rf, of), (qb, kb, vb, gb, rb, ob))):
        rev = d == 1
        incl = (colj >= row) if rev else (colj <= row)
        strict = (colj > row) if rev else (colj < row)
        gates = g_r[0]
        tri_c = jnp.where(incl, 1.0, 0.0)
        gcm = _mm(tri_c, gates, HIGHEST)
        gcr = _mm(r_r[0, 0], tri_c, HIGHEST, dims=nt_dims)
        for hq in range(NQK_HEADS):
            q = q_r[0, :, hq * HEAD_DIM:(hq + 1) * HEAD_DIM]
            k = k_r[0, :, hq * HEAD_DIM:(hq + 1) * HEAD_DIM]
            kq = lax.dot_general(jnp.concatenate([q, k], axis=0), k, nt_dims, preferred_element_type=F32)
            for j in range(per):
                h = hq * per + j
                idx = d * NV_HEADS + h
                gc_c = gcm[:, GATE_LANE0 + idx:GATE_LANE0 + idx + 1]
                seqs.append(dict(d=d, h=h, o_r=o_r, v_r=v_r, q=q, k=k, qk=kq[:c], kk=kq[c:], incl=incl, strict=strict,
                                 beta=gates[:, idx:idx + 1], gc_c=gc_c, gc_r=gcr[idx:idx + 1, :],
                                 ge=gc_c[0:1] if rev else gc_c[c - 1:c]))

    def same_block(m):
        sh = int(math.log2(m))
        return jnp.right_shift(row, sh) == jnp.right_shift(colj, sh)

    for s in seqs:
        s['decay'] = jnp.where(s['incl'], jnp.exp(jnp.where(s['incl'], s['gc_c'] - s['gc_r'], 0.0)), 0.0)
        s['a'] = jnp.where(s['strict'], s['beta'] * s['kk'] * s['decay'], 0.0)
        s['t'] = eye - jnp.where(same_block(2), s['a'], 0.0)
    m = 4
    while m <= c:
        between = jnp.logical_and(same_block(m), jnp.logical_not(same_block(m // 2)))
        for s in seqs:
            s['te'] = _mm(s['t'], jnp.where(between, s['a'], 0.0), prec)
        for s in seqs:
            s['t'] = s['t'] - _mm(s['te'], s['t'], prec)
        m *= 2
    for s in seqs:
        h = s['h']
        egc = jnp.exp(s['gc_c'])
        kf32 = s['k'].astype(F32)
        v = s['v_r'][0, :, h * HEAD_DIM:(h + 1) * HEAD_DIM].astype(F32)
        rhs = jnp.concatenate([s['beta'] * v, (s['beta'] * egc) * kf32], axis=1)
        s['sol'] = _mm(s['t'], rhs, prec)
        s['q_dec'] = s['q'].astype(F32) * egc
        s['k_dec'] = kf32 * jnp.exp(s['ge'] - s['gc_c'])
    for s in seqs:
        s['ws'] = _mm(jnp.concatenate([s['sol'][:, HEAD_DIM:], s['q_dec']], axis=0), s_ref[s['d'], s['h']])
    for s in seqs:
        s['u'] = s['sol'][:, :HEAD_DIM] - s['ws'][:c]
        s_ref[s['d'], s['h']] = (jnp.exp(s['ge']) * s_ref[s['d'], s['h']]
                                 + _mm(s['k_dec'], s['u'], dims=tn_dims))
    for s in seqs:
        h = s['h']
        o = s['ws'][c:] + _mm(s['qk'] * s['decay'], s['u'])
        s['o_r'][0, :, h * HEAD_DIM:(h + 1) * HEAD_DIM] = o.astype(s['o_r'].dtype)

    @pl.when(i == nc - 1)
    def _():
        sfin_ref[0] = s_ref[...]


def _gdn_call(q, k, v, go, rows, s0, prec):
    b, t, _ = q.shape
    c = GDN_CHUNK
    nc = t // c
    fwd = lambda i, n: (i, n, 0)
    bwd = lambda i, n: (i, nc - 1 - n, 0)
    rfwd = lambda i, n: (i, n, 0, 0)
    rbwd = lambda i, n: (i, nc - 1 - n, 0, 0)
    state_spec = pl.BlockSpec((1,) + s0.shape[1:], lambda i, n: (i, 0, 0, 0, 0))

    def specs(m3, mr):
        return [pl.BlockSpec((1, c, QK_DIM), m3), pl.BlockSpec((1, c, QK_DIM), m3),
                pl.BlockSpec((1, c, V_DIM), m3), pl.BlockSpec((1, c, LANES), m3),
                pl.BlockSpec((1, 1, 2 * NV_HEADS, c), mr)]

    return pl.pallas_call(
        functools.partial(_gdn_kernel, prec=prec),
        grid=(b, nc),
        in_specs=specs(fwd, rfwd) + specs(bwd, rbwd) + [state_spec],
        out_specs=[pl.BlockSpec((1, c, V_DIM), fwd), pl.BlockSpec((1, c, V_DIM), bwd), state_spec],
        out_shape=[jax.ShapeDtypeStruct((b, t, V_DIM), BF16), jax.ShapeDtypeStruct((b, t, V_DIM), BF16),
                   jax.ShapeDtypeStruct(s0.shape, F32)],
        scratch_shapes=[pltpu.VMEM(s0.shape[1:], F32)],
        compiler_params=_cparams(("parallel", "arbitrary"), 48),
        name="gdn",
    )(q, k, v, go, rows, q, k, v, go, rows, s0)


def _gdn_decay_rows(go):
    b, t, _ = go.shape
    c = GDN_CHUNK
    lg = go[..., GATE_LANE0:GATE_LANE0 + 2 * NV_HEADS]
    return jnp.transpose(lg.reshape(b, t // c, c, 2 * NV_HEADS), (0, 1, 3, 2))


def _fnet1_kernel(x_ref, f_ref, ar_ref, ai_ref):
    a = _mm(f_ref[...], x_ref[0], HIGHEST)
    n = ar_ref.shape[1]
    ar_ref[0] = a[:n]
    ai_ref[0] = a[n:]


def _fnet2_kernel(ar_ref, ai_ref, gc_ref, gs_ref, cc_ref, sc_ref, o_ref, *, mb, scale):
    n = gc_ref.shape[1]
    cdim = ar_ref.shape[2]
    for m in range(mb):
        ar = ar_ref[0, m * n:(m + 1) * n, :]
        ai = ai_ref[0, m * n:(m + 1) * n, :]
        gc = gc_ref[m]
        gs = gs_ref[m]
        zr = _mm(gc, ar, HIGHEST) + _mm(gs, ai, HIGHEST)
        zi = _mm(gc, ai, HIGHEST) - _mm(gs, ar, HIGHEST)
        for g0 in range(0, cdim, HEAD_DIM):
            y = (_mm(zr[:, g0:g0 + HEAD_DIM], cc_ref[...], HIGHEST)
                 + _mm(zi[:, g0:g0 + HEAD_DIM], sc_ref[...], HIGHEST))
            o_ref[0, :, m * cdim + g0:m * cdim + g0 + HEAD_DIM] = (y * scale).astype(o_ref.dtype)


def _fnet_tables(n):
    a = np.arange(n)
    ang1 = 2.0 * np.pi * np.outer(a, a) / n
    f1 = np.concatenate([np.cos(ang1), -np.sin(ang1)], axis=0)
    m = a[:, None] + n * a[None, :]
    ang2 = 2.0 * np.pi * ((m[:, :, None] * a[None, None, :]) % (n * n)) / (n * n)
    angc = 2.0 * np.pi * np.outer(np.arange(HEAD_DIM), np.arange(HEAD_DIM)) / HEAD_DIM
    f = lambda x: jnp.asarray(x, F32)
    return f(f1), f(np.cos(ang2)), f(np.sin(ang2)), f(np.cos(angc)), f(np.sin(angc))


def _fnet_call(f):
    b, l, c = f.shape
    n = GRID_W
    assert l == n * n
    f1, gcos, gsin, ccos, csin = _fnet_tables(n)
    tn = 4096
    ar, ai = pl.pallas_call(
        _fnet1_kernel,
        grid=(b, n * c // tn),
        in_specs=[pl.BlockSpec((1, n, tn), lambda i, j: (i, 0, j)),
                  pl.BlockSpec((2 * n, n), lambda i, j: (0, 0))],
        out_specs=[pl.BlockSpec((1, n, tn), lambda i, j: (i, 0, j))] * 2,
        out_shape=[jax.ShapeDtypeStruct((b, n, n * c), F32)] * 2,
        compiler_params=_cparams(("parallel", "parallel"), 32),
        name="fnet1",
    )(f.reshape(b, n, n * c), f1)
    mb = 8
    out = pl.pallas_call(
        functools.partial(_fnet2_kernel, mb=mb, scale=1.0 / math.sqrt(l * HEAD_DIM)),
        grid=(b, n // mb),
        in_specs=[pl.BlockSpec((1, n * mb, c), lambda i, j: (i, j, 0)),
                  pl.BlockSpec((1, n * mb, c), lambda i, j: (i, j, 0)),
                  pl.BlockSpec((mb, n, n), lambda i, j: (j, 0, 0)),
                  pl.BlockSpec((mb, n, n), lambda i, j: (j, 0, 0)),
                  pl.BlockSpec((HEAD_DIM, HEAD_DIM), lambda i, j: (0, 0)),
                  pl.BlockSpec((HEAD_DIM, HEAD_DIM), lambda i, j: (0, 0))],
        out_specs=pl.BlockSpec((1, n, c * mb), lambda i, j: (i, 0, j)),
        out_shape=jax.ShapeDtypeStruct((b, n, n * c), BF16),
        compiler_params=_cparams(("parallel", "parallel"), 32),
        name="fnet2",
    )(ar.reshape(b, l, c), ai.reshape(b, l, c), gcos, gsin, ccos, csin)
    return out.reshape(b, l, c)


def _merge_kernel(of_ref, ob_ref, z_ref, fm_ref, gab_ref, x_ref, g1_ref, sh2_ref, sc2_ref, gn_ref, n2_ref,
                  wg_ref, wf_ref, wm_ref, wr_ref, br_ref, x1_ref, h2_ref, lg_ref):
    d = x_ref.shape[1]
    o = of_ref[...].astype(F32) + ob_ref[...].astype(F32)
    z = z_ref[...].astype(F32)
    parts = []
    for h0 in range(0, V_DIM, HEAD_DIM):
        oh = o[:, h0:h0 + HEAD_DIM]
        parts.append(oh * lax.rsqrt(jnp.mean(oh * oh, axis=-1, keepdims=True) + EPS) * gn_ref[...])
    yb_in = jnp.concatenate(parts, axis=1) * (z * _sigmoid(z))
    yb = _mm(yb_in, wg_ref[...])
    ya = _mm(fm_ref[...], wf_ref[...])
    ga = _sigmoid(gab_ref[:, :d].astype(F32))
    gb = _sigmoid(gab_ref[:, d:].astype(F32))
    mm = _mm(ga * ya + gb * yb, wm_ref[...])
    x1 = x_ref[...] + g1_ref[0] * mm
    x1_ref[...] = x1
    h2 = _rmsnorm(x1, n2_ref[...]) * (1.0 + sc2_ref[0]) + sh2_ref[0]
    h2_ref[...] = h2
    lg_ref[...] = _mm(h2, wr_ref[...], HIGHEST) + br_ref[...]


def _merge_call(of, ob, z, fm, gab, x2, g1, sh2, sc2, gn, n2, wg, wf, wm, wr, br, tokens_per_batch, tm):
    n, d = x2.shape
    per = tokens_per_batch // tm
    tok = lambda wd: pl.BlockSpec((tm, wd), lambda i: (i, 0))
    vec = pl.BlockSpec((1, 1, d), lambda i: (i // per, 0, 0))
    full = lambda a: pl.BlockSpec(a.shape, lambda i: (0,) * a.ndim)
    return pl.pallas_call(
        _merge_kernel,
        grid=(n // tm,),
        in_specs=[tok(V_DIM), tok(V_DIM), tok(V_DIM), tok(F_DIM), tok(2 * d), tok(d), vec, vec, vec,
                  full(gn), full(n2), full(wg), full(wf), full(wm), full(wr), full(br)],
        out_specs=[tok(d), tok(d), tok(LANES)],
        out_shape=[jax.ShapeDtypeStruct((n, d), F32), jax.ShapeDtypeStruct((n, d), F32),
                   jax.ShapeDtypeStruct((n, LANES), F32)],
        compiler_params=_cparams(("parallel",), 56),
        name="merge",
    )(of, ob, z, fm, gab, x2, g1, sh2, sc2, gn, n2, wg, wf, wm, wr, br)


def _route_kernel(lg_ref, idx_ref, w_ref, rank_ref, cnt_ref, run_ref):
    i = pl.program_id(0)

    @pl.when(i == 0)
    def _():
        run_ref[...] = jnp.zeros_like(run_ref)

    l = lg_ref[...]
    tm = l.shape[0]
    lane = lax.broadcasted_iota(I32, l.shape, 1)
    vals, idxs = [], []
    for _ in range(TOP_K):
        m = jnp.max(l, axis=-1, keepdims=True)
        idx = jnp.min(jnp.where(l == m, lane, LANES), axis=-1, keepdims=True)
        vals.append(m)
        idxs.append(idx)
        l = jnp.where(lane == idx, NEG_BIG * 2.0, l)
    es = [jnp.exp(v - vals[0]) for v in vals]
    inv = 1.0 / (es[0] + es[1] + es[2] + es[3])
    picked = jnp.zeros(l.shape, F32)
    for idx in idxs:
        picked = picked + (lane == idx).astype(F32)
    r = lax.broadcasted_iota(I32, (tm, tm), 0)
    cidx = lax.broadcasted_iota(I32, (tm, tm), 1)
    before = _mm(jnp.where(cidx < r, 1.0, 0.0), picked) + run_ref[...]
    idx_out = jnp.zeros(l.shape, I32)
    w_out = jnp.zeros(l.shape, F32)
    rank_out = jnp.zeros(l.shape, F32)
    for k in range(TOP_K):
        rk = jnp.sum(jnp.where(lane == idxs[k], before, 0.0), axis=-1, keepdims=True)
        idx_out = jnp.where(lane == k, idxs[k], idx_out)
        w_out = jnp.where(lane == k, es[k] * inv, w_out)
        rank_out = jnp.where(lane == k, rk, rank_out)
    idx_ref[...] = idx_out
    w_ref[...] = w_out
    rank_ref[...] = rank_out.astype(I32)
    run_ref[...] = run_ref[...] + jnp.sum(picked, axis=0, keepdims=True)
    cnt_ref[...] = run_ref[...]


def _route_call(logits, tm):
    n = logits.shape[0]
    tok = pl.BlockSpec((tm, LANES), lambda i: (i, 0))
    return pl.pallas_call(
        _route_kernel,
        grid=(n // tm,),
        in_specs=[tok],
        out_specs=[tok, tok, tok, pl.BlockSpec((1, LANES), lambda i: (0, 0))],
        out_shape=[jax.ShapeDtypeStruct((n, LANES), I32), jax.ShapeDtypeStruct((n, LANES), F32),
                   jax.ShapeDtypeStruct((n, LANES), I32), jax.ShapeDtypeStruct((1, LANES), F32)],
        scratch_shapes=[pltpu.VMEM((1, LANES), F32)],
        compiler_params=_cparams(("arbitrary",), 32),
        name="route",
    )(logits)


def _expert_kernel(be_ref, nv_ref, src_ref, nxt_ref, dst_ref, h_hbm, wg_ref, wu_ref, wd_ref, bg_ref, bu_ref, bd_ref,
                   y_hbm, xbuf, ybuf, wgb, wub, wdb, sem_in, sem_out):
    i = pl.program_id(0)
    nb = pl.num_programs(0)
    nv = nv_ref[0]
    tmb = xbuf.shape[1]
    slot = lax.rem(i, 2)

    def gather(idx_ref, s):
        def body(r, carry):
            pltpu.make_async_copy(h_hbm.at[pl.ds(idx_ref[0, 0, r], 1)], xbuf.at[s, pl.ds(r, 1)], sem_in.at[s]).start()
            return carry
        lax.fori_loop(0, tmb, body, 0)

    def wait_scatter():
        pltpu.make_async_copy(ybuf, ybuf, sem_out.at[0]).wait()

    @pl.when(i == 0)
    def _():
        gather(src_ref, 0)
        ybuf[...] = jnp.zeros_like(ybuf)
        spare = pltpu.make_async_copy(ybuf, y_hbm.at[pl.ds(y_hbm.shape[0] - tmb, tmb)], sem_out.at[0])
        spare.start()
        spare.wait()

    @pl.when(i + 1 < nv)
    def _():
        gather(nxt_ref, 1 - slot)

    @pl.when((i >= 1) & (i <= nv))
    def _():
        wait_scatter()

    @pl.when(i < nv)
    def _():
        changed = jnp.logical_or(i == 0, be_ref[i] != be_ref[jnp.maximum(i - 1, 0)])

        @pl.when(changed)
        def _():
            wgb[...] = wg_ref[0].astype(BF16)
            wub[...] = wu_ref[0].astype(BF16)
            wdb[...] = wd_ref[0].astype(BF16)

        pltpu.make_async_copy(xbuf.at[slot], xbuf.at[slot], sem_in.at[slot]).wait()
        x = xbuf[slot].astype(BF16)
        gate = jnp.minimum(_mm(x, wgb[...]) + bg_ref[0], SWIGLU_LIMIT)
        up = jnp.clip(_mm(x, wub[...]) + bu_ref[0], -SWIGLU_LIMIT, SWIGLU_LIMIT)
        act = (up + 1.0) * gate * _sigmoid(SWIGLU_ALPHA * gate)
        ybuf[...] = _mm(act, wdb[...]) + bd_ref[0]

        def body(r, carry):
            pltpu.make_async_copy(ybuf.at[pl.ds(r, 1)], y_hbm.at[pl.ds(dst_ref[0, 0, r], 1)], sem_out.at[0]).start()
            return carry
        lax.fori_loop(0, tmb, body, 0)

        @pl.when(i == nb - 1)
        def _():
            wait_scatter()


def _expert_call(block_e, n_valid, slot_src, slot_dst, h2, w_gate, w_up, w_down, b_gate, b_up, b_down, n_rows_out):
    nb = block_e.shape[0]
    tmb = EXPERT_BLOCK
    n, d = h2.shape
    de = w_gate.shape[2]
    src3 = slot_src.reshape(nb, 1, tmb)
    dst3 = slot_dst.reshape(nb, 1, tmb)
    smem_blk = lambda m: pl.BlockSpec((1, 1, tmb), m, memory_space=pltpu.SMEM)
    wspec = lambda s: pl.BlockSpec((1,) + s, lambda i, be, nv: (be[i], 0, 0))
    grid_spec = pltpu.PrefetchScalarGridSpec(
        num_scalar_prefetch=2,
        grid=(nb,),
        in_specs=[smem_blk(lambda i, be, nv: (i, 0, 0)),
                  smem_blk(lambda i, be, nv: (jnp.minimum(i + 1, nb - 1), 0, 0)),
                  smem_blk(lambda i, be, nv: (i, 0, 0)),
                  pl.BlockSpec(memory_space=pl.ANY),
                  wspec((d, de)), wspec((d, de)), wspec((de, d)),
                  wspec((1, de)), wspec((1, de)), wspec((1, d))],
        out_specs=pl.BlockSpec(memory_space=pl.ANY),
        scratch_shapes=[pltpu.VMEM((2, tmb, d), F32), pltpu.VMEM((tmb, d), F32),
                        pltpu.VMEM((d, de), BF16), pltpu.VMEM((d, de), BF16), pltpu.VMEM((de, d), BF16),
                        pltpu.SemaphoreType.DMA((2,)), pltpu.SemaphoreType.DMA((1,))],
    )
    ne = w_gate.shape[0]
    return pl.pallas_call(
        _expert_kernel,
        grid_spec=grid_spec,
        out_shape=jax.ShapeDtypeStruct((n_rows_out, d), F32),
        compiler_params=_cparams(("arbitrary",), 56),
        name="expert",
    )(block_e, n_valid, src3, src3, dst3, h2, w_gate, w_up, w_down,
      b_gate.reshape(ne, 1, de), b_up.reshape(ne, 1, de), b_down.reshape(ne, 1, d))


def _combine_kernel(y0, y1, y2, y3, w_ref, x1_ref, g2_ref, fg_ref, o_ref):
    w = w_ref[...]
    moe = (w[:, 0:1] * y0[...] + w[:, 1:2] * y1[...]) + (w[:, 2:3] * y2[...] + w[:, 3:4] * y3[...])
    o_ref[...] = _rmsnorm(x1_ref[...] + g2_ref[0] * moe, fg_ref[...])


def _combine_call(y4, top_w, x1, g2, fg, tokens_per_batch, tm):
    n, d = x1.shape
    per = tokens_per_batch // tm
    nt = n // tm
    yspec = lambda k: pl.BlockSpec((tm, d), lambda i: (k * nt + i, 0))
    return pl.pallas_call(
        _combine_kernel,
        grid=(nt,),
        in_specs=[yspec(0), yspec(1), yspec(2), yspec(3),
                  pl.BlockSpec((tm, LANES), lambda i: (i, 0)),
                  pl.BlockSpec((tm, d), lambda i: (i, 0)),
                  pl.BlockSpec((1, 1, d), lambda i: (i // per, 0, 0)),
                  pl.BlockSpec((1, d), lambda i: (0, 0))],
        out_specs=pl.BlockSpec((tm, d), lambda i: (i, 0)),
        out_shape=jax.ShapeDtypeStruct((n, d), F32),
        compiler_params=_cparams(("parallel",), 48),
        name="combine",
    )(y4, y4, y4, y4, top_w, x1, g2, fg)


def _routing_tables(top_idx, rank, counts, n):
    tmb = EXPERT_BLOCK
    counts = counts.astype(I32)
    padded = (counts + tmb - 1) // tmb * tmb
    pad_end = jnp.cumsum(padded)
    pad_start = pad_end - padded
    n_blocks = -(-(n * TOP_K + N_EXPERTS * (tmb - 1)) // tmb)
    n_slots = n_blocks * tmb
    onehot = top_idx[:, :, None] == jnp.arange(N_EXPERTS, dtype=I32)[None, None, :]
    dest = jnp.sum(jnp.where(onehot, pad_start[None, None, :], 0), axis=-1) + rank
    flat = jnp.arange(TOP_K, dtype=I32)[None, :] * n + jnp.arange(n, dtype=I32)[:, None]
    slot_flat = jnp.full((n_slots,), -1, I32).at[dest.reshape(-1)].set(flat.reshape(-1))
    used = slot_flat >= 0
    slot_src = jnp.where(used, slot_flat % n, 0)
    slot_dst = jnp.where(used, slot_flat, n * TOP_K + jnp.arange(n_slots, dtype=I32) % tmb)
    block_start = jnp.arange(n_blocks, dtype=I32) * tmb
    block_e = jnp.minimum(jnp.sum((pad_end[None, :] <= block_start[:, None]).astype(I32), axis=1), N_EXPERTS - 1)
    n_valid = (pad_end[-1:] // tmb).astype(I32)
    return block_e, n_valid, slot_src, slot_dst


def _gdn_branch(xtok, sh, sc, norm_g, w_cols, widths, dtypes, conv_w, par, tokens_per_batch, grid_w, use_rows,
                tm_proj, tm_conv, s0, prec):
    b = xtok.shape[0] // tokens_per_batch
    outs = _inproj_call(xtok, sh, sc, norm_g, w_cols, widths, dtypes, tokens_per_batch, tm_proj)
    qkv, gates = outs[0], outs[-1]
    q, k, v, go = _conv_call(qkv.reshape(b, tokens_per_batch, QKV_DIM), conv_w,
                             gates.reshape(b, tokens_per_batch, LANES), par, grid_w, use_rows, tm_conv)
    o_f, o_b, s_fin = _gdn_call(q, k, v, go, _gdn_decay_rows(go), s0, prec)
    return outs, o_f, o_b, s_fin


def kernel(x, c, ctx, c_ctx, w_mod, b_mod, norm1_g, norm2_g, w_in, conv_w, a_log, dt_bias, gdn_norm_g,
           w_fourier_out, w_gdn_out, w_merge_out, w_router, b_router, w_gate, b_gate, w_up, b_up,
           w_down, b_down, final_norm_g):
    b, l, d = x.shape
    n = b * l
    n_ctx = ctx.shape[1]
    assert w_mod.shape[0] == 1 and l == GRID_W * GRID_W and d == V_DIM
    prec = None

    c8 = jnp.concatenate([c, c_ctx[None, :], jnp.zeros((8 - b - 1, d), F32)], axis=0)
    mod = _mod_call(c8, w_mod[0], b_mod[0])
    sh1, sc1, g1, sh2, sc2, g2 = [mod[:b, j * d:(j + 1) * d].reshape(b, 1, d) for j in range(6)]
    csh1 = jnp.broadcast_to(mod[b:b + 1, 0:d].reshape(1, 1, d), (b, 1, d))
    csc1 = jnp.broadcast_to(mod[b:b + 1, d:2 * d].reshape(1, 1, d), (b, 1, d))

    wi = w_in[0]
    off_gate = QKV_DIM
    off_z = off_gate + 4 * NV_HEADS
    off_f = off_z + V_DIM
    off_ga = off_f + F_DIM
    gate_cols = jnp.pad(wi[:, off_gate:off_z], ((0, 0), (0, LANES - 4 * NV_HEADS)))
    w_lat = jnp.concatenate([wi[:, :QKV_DIM], wi[:, off_z:off_f], wi[:, off_f:off_ga], wi[:, off_ga:], gate_cols],
                            axis=1).astype(BF16)
    w_ctx = jnp.concatenate([wi[:, :QKV_DIM], gate_cols], axis=1).astype(BF16)
    par = jnp.pad(jnp.stack([a_log[0].reshape(-1), dt_bias[0].reshape(-1)]),
                  ((0, 6), (GATE_LANE0, LANES - 2 * GATE_LANE0)))
    n1 = norm1_g[0].reshape(1, d)
    cw = conv_w[0].reshape(9, QKV_DIM)

    zero_state = jnp.zeros((b, 2, NV_HEADS, HEAD_DIM, HEAD_DIM), F32)
    _, _, _, s_ctx = _gdn_branch(ctx.reshape(b * n_ctx, d), csh1, csc1, n1, w_ctx, (QKV_DIM, LANES), (BF16, F32),
                                 cw, par, n_ctx, n_ctx, False, n_ctx, n_ctx, zero_state, prec)

    x2 = x.reshape(n, d)
    outs, o_f, o_b, _ = _gdn_branch(x2, sh1, sc1, n1, w_lat, (QKV_DIM, V_DIM, F_DIM, 2 * d, LANES),
                                    (BF16, BF16, F32, BF16, F32), cw, par, l, GRID_W, True, 512, 512, s_ctx, prec)
    _, z, f, gab, _ = outs
    fmix = _fnet_call(f.reshape(b, l, F_DIM)).reshape(n, F_DIM)

    wr = jnp.pad(w_router[0], ((0, 0), (0, LANES - N_EXPERTS)))
    br = jnp.pad(b_router[0], (0, LANES - N_EXPERTS), constant_values=NEG_BIG).reshape(1, LANES)
    x1, h2, logits = _merge_call(
        o_f.reshape(n, V_DIM), o_b.reshape(n, V_DIM), z, fmix, gab, x2, g1, sh2, sc2,
        gdn_norm_g[0].reshape(1, HEAD_DIM), norm2_g[0].reshape(1, d),
        w_gdn_out[0].astype(BF16), w_fourier_out[0].astype(BF16), w_merge_out[0].astype(BF16), wr, br, l, 256)

    top_idx, top_w, rank, counts = _route_call(logits, 512)
    block_e, n_valid, slot_src, slot_dst = _routing_tables(top_idx[:, :TOP_K], rank[:, :TOP_K],
                                                           counts[0, :N_EXPERTS], n)
    y4 = _expert_call(block_e, n_valid, slot_src, slot_dst, h2, w_gate[0], w_up[0], w_down[0],
                      b_gate[0], b_up[0], b_down[0], n * TOP_K + EXPERT_BLOCK)
    out = _combine_call(y4, top_w, x1, g2, final_norm_g.reshape(1, d), l, 256)
    return out.reshape(b, l, d)
```

```python
import functools
import math

import jax
import jax.numpy as jnp
import numpy as np
from jax import lax
from jax.experimental import pallas as pl
from jax.experimental.pallas import tpu as pltpu

F32 = jnp.float32
BF16 = jnp.bfloat16
I32 = jnp.int32
HIGHEST = lax.Precision.HIGHEST

GRID_W = 64
NQK_HEADS = 4
NV_HEADS = 8
HEAD_DIM = 128
QK_DIM = NQK_HEADS * HEAD_DIM
V_DIM = NV_HEADS * HEAD_DIM
QKV_DIM = 2 * QK_DIM + V_DIM
F_GROUPS = 4
F_DIM = F_GROUPS * HEAD_DIM
N_EXPERTS = 32
TOP_K = 4
SWIGLU_ALPHA = 1.702
SWIGLU_LIMIT = 7.0
EPS = 1e-6

LANES = 128
GATE_LANE0 = 16
GDN_CHUNK = 128
EXPERT_BLOCK = 256
EXPERT_COLS = 256
NEG_BIG = -1e30
MIB = 2 ** 20


def _cparams(sem, vmem_mib):
    return pltpu.CompilerParams(dimension_semantics=sem, vmem_limit_bytes=vmem_mib * MIB)


def _mm(a, b, prec=None, dims=(((1,), (0,)), ((), ()))):
    if prec is None:
        return lax.dot_general(a.astype(BF16), b.astype(BF16), dims, preferred_element_type=F32)
    return lax.dot_general(a.astype(F32), b.astype(F32), dims, precision=prec, preferred_element_type=F32)


def _sigmoid(x):
    return 1.0 / (1.0 + jnp.exp(-x))


def _rmsnorm(x, g):
    return x * lax.rsqrt(jnp.mean(x * x, axis=-1, keepdims=True) + EPS) * g


def _mod_kernel(c_ref, w_ref, b_ref, o_ref):
    c = c_ref[...]
    o_ref[...] = _mm(c * _sigmoid(c), w_ref[...], HIGHEST) + b_ref[...]


def _mod_call(c8, w_mod, b_mod):
    d, n = w_mod.shape
    tn = 1536
    return pl.pallas_call(
        _mod_kernel,
        grid=(n // tn,),
        in_specs=[pl.BlockSpec((8, d), lambda j: (0, 0)),
                  pl.BlockSpec((d, tn), lambda j: (0, j)),
                  pl.BlockSpec((1, tn), lambda j: (0, j))],
        out_specs=pl.BlockSpec((8, tn), lambda j: (0, j)),
        out_shape=jax.ShapeDtypeStruct((8, n), F32),
        compiler_params=_cparams(("parallel",), 32),
        name="mod",
    )(c8, w_mod, b_mod.reshape(1, n))


def _inproj_kernel(x_ref, sh_ref, sc_ref, g_ref, w_ref, *o_refs, widths):
    u = (_rmsnorm(x_ref[...], g_ref[...]) * (1.0 + sc_ref[0]) + sh_ref[0]).astype(BF16)
    start = 0
    for o_ref, width in zip(o_refs, widths):
        step = min(width, 512)
        for c0 in range(0, width, step):
            o_ref[:, c0:c0 + step] = _mm(u, w_ref[:, start + c0:start + c0 + step]).astype(o_ref.dtype)
        start += width


def _inproj_call(x2, sh, sc, g, w, widths, dtypes, tokens_per_batch, tm):
    n, d = x2.shape
    per = tokens_per_batch // tm
    out_shape = [jax.ShapeDtypeStruct((n, wd), dt) for wd, dt in zip(widths, dtypes)]
    return pl.pallas_call(
        functools.partial(_inproj_kernel, widths=widths),
        grid=(n // tm,),
        in_specs=[pl.BlockSpec((tm, d), lambda i: (i, 0)),
                  pl.BlockSpec((1, 1, d), lambda i: (i // per, 0, 0)),
                  pl.BlockSpec((1, 1, d), lambda i: (i // per, 0, 0)),
                  pl.BlockSpec((1, d), lambda i: (0, 0)),
                  pl.BlockSpec(w.shape, lambda i: (0, 0))],
        out_specs=[pl.BlockSpec((tm, wd), lambda i: (i, 0)) for wd in widths],
        out_shape=out_shape,
        compiler_params=_cparams(("parallel",), 56),
        name="inproj",
    )(x2, sh, sc, g, w)


def _conv_kernel(*refs, grid_w, use_rows, tm, cw):
    if use_rows:
        prev_ref, main_ref, next_ref, w_ref, gates_ref, par_ref, q_ref, k_ref, v_ref, go_ref = refs
    else:
        main_ref, w_ref, gates_ref, par_ref, q_ref, k_ref, v_ref, go_ref = refs
    r = pl.program_id(1)
    nr = pl.num_programs(1)
    t = lax.broadcasted_iota(I32, (tm, 1), 0)
    col = jnp.bitwise_and(t, grid_w - 1)
    m_left = (col != 0).astype(F32)
    m_right = (col != grid_w - 1).astype(F32)
    has_prev = (r > 0).astype(F32)
    has_next = (r < nr - 1).astype(F32)
    for c0 in range(0, QKV_DIM, cw):
        xm = main_ref[0, :, c0:c0 + cw].astype(F32)
        if use_rows:
            xp = prev_ref[0, :, c0:c0 + cw].astype(F32) * has_prev
            xn = next_ref[0, :, c0:c0 + cw].astype(F32) * has_next
            up = jnp.concatenate([xp, xm[:tm - grid_w]], axis=0)
            dn = jnp.concatenate([xm[grid_w:], xn], axis=0)

        def colsum(kc):
            y = xm * w_ref[3 + kc:4 + kc, c0:c0 + cw]
            if use_rows:
                y = y + up * w_ref[kc:kc + 1, c0:c0 + cw] + dn * w_ref[6 + kc:7 + kc, c0:c0 + cw]
            return y

        acc = (colsum(1) + pltpu.roll(colsum(0), 1, axis=0) * m_left
               + pltpu.roll(colsum(2), tm - 1, axis=0) * m_right)
        s = acc * _sigmoid(acc)
        for h0 in range(0, cw, HEAD_DIM):
            c = c0 + h0
            seg = s[:, h0:h0 + HEAD_DIM]
            if c < 2 * QK_DIM:
                seg = seg * lax.rsqrt(jnp.sum(seg * seg, axis=-1, keepdims=True) + EPS)
            if c < QK_DIM:
                q_ref[0, :, c:c + HEAD_DIM] = (seg * HEAD_DIM ** -0.5).astype(q_ref.dtype)
            elif c < 2 * QK_DIM:
                k_ref[0, :, c - QK_DIM:c - QK_DIM + HEAD_DIM] = seg.astype(k_ref.dtype)
            else:
                v_ref[0, :, c - 2 * QK_DIM:c - 2 * QK_DIM + HEAD_DIM] = seg.astype(v_ref.dtype)
    g = gates_ref[0]
    a = g + par_ref[1:2, :]
    softplus = jnp.maximum(a, 0.0) + jnp.log1p(jnp.exp(-jnp.abs(a)))
    log_g = -jnp.exp(par_ref[0:1, :]) * softplus
    lane = lax.broadcasted_iota(I32, g.shape, 1)
    go_ref[0] = jnp.where(lane < GATE_LANE0, _sigmoid(g), log_g)


def _conv_call(qkv, conv_w, gates, par, grid_w, use_rows, tm):
    b, t, c = qkv.shape
    kern = functools.partial(_conv_kernel, grid_w=grid_w, use_rows=use_rows, tm=tm, cw=512)
    per = tm // grid_w
    nrow = t // grid_w
    in_specs = []
    args = []
    if use_rows:
        in_specs.append(pl.BlockSpec((1, grid_w, c), lambda i, r: (i, jnp.maximum(r * per - 1, 0), 0)))
        args.append(qkv)
    in_specs.append(pl.BlockSpec((1, tm, c), lambda i, r: (i, r, 0)))
    args.append(qkv)
    if use_rows:
        in_specs.append(pl.BlockSpec((1, grid_w, c), lambda i, r: (i, jnp.minimum((r + 1) * per, nrow - 1), 0)))
        args.append(qkv)
    in_specs += [pl.BlockSpec(conv_w.shape, lambda i, r: (0, 0)),
                 pl.BlockSpec((1, tm, LANES), lambda i, r: (i, r, 0)),
                 pl.BlockSpec(par.shape, lambda i, r: (0, 0))]
    args += [conv_w, gates, par]
    out_shape = [jax.ShapeDtypeStruct((b, t, QK_DIM), BF16), jax.ShapeDtypeStruct((b, t, QK_DIM), BF16),
                 jax.ShapeDtypeStruct((b, t, V_DIM), BF16), jax.ShapeDtypeStruct((b, t, LANES), F32)]
    out_specs = [pl.BlockSpec((1, tm, QK_DIM), lambda i, r: (i, r, 0)),
                 pl.BlockSpec((1, tm, QK_DIM), lambda i, r: (i, r, 0)),
                 pl.BlockSpec((1, tm, V_DIM), lambda i, r: (i, r, 0)),
                 pl.BlockSpec((1, tm, LANES), lambda i, r: (i, r, 0))]
    return pl.pallas_call(
        kern, grid=(b, t // tm), in_specs=in_specs, out_specs=out_specs, out_shape=out_shape,
        compiler_params=_cparams(("parallel", "parallel"), 48),
        name="conv_rows" if use_rows else "conv_seq",
    )(*args)


def _gdn_kernel(qf, kf, vf, gf, rf, qb, kb, vb, gb, rb, s0_ref, of, ob, sfin_ref, s_ref, *, prec):
    i = pl.program_id(1)
    nc = pl.num_programs(1)

    @pl.when(i == 0)
    def _():
        s_ref[...] = s0_ref[0]

    c = qf.shape[1]
    per = NV_HEADS // NQK_HEADS
    row = lax.broadcasted_iota(I32, (c, c), 0)
    colj = lax.broadcasted_iota(I32, (c, c), 1)
    eye = jnp.where(row == colj, 1.0, 0.0)
    nt_dims = (((1,), (1,)), ((), ()))
    tn_dims = (((0,), (0,)), ((), ()))

    seqs = []
    for d, (q_r, k_r, v_r, g_r, r_r, o_r) in enumerate(((qf, kf, vf, gf, rf, of), (qb, kb, vb, gb, rb, ob))):
        rev = d == 1
        incl = (colj >= row) if rev else (colj <= row)
        strict = (colj > row) if rev else (colj < row)
        gates = g_r[0]
        tri_c = jnp.where(incl, 1.0, 0.0)
        gcm = _mm(tri_c, gates, HIGHEST)
        gcr = _mm(r_r[0, 0], tri_c, HIGHEST, dims=nt_dims)
        for hq in range(NQK_HEADS):
            q = q_r[0, :, hq * HEAD_DIM:(hq + 1) * HEAD_DIM]
            k = k_r[0, :, hq * HEAD_DIM:(hq + 1) * HEAD_DIM]
            kq = lax.dot_general(jnp.concatenate([q, k], axis=0), k, nt_dims, preferred_element_type=F32)
            for j in range(per):
                h = hq * per + j
                idx = d * NV_HEADS + h
                gc_c = gcm[:, GATE_LANE0 + idx:GATE_LANE0 + idx + 1]
                seqs.append(dict(d=d, h=h, o_r=o_r, v_r=v_r, q=q, k=k, qk=kq[:c], kk=kq[c:], incl=incl, strict=strict,
                                 beta=gates[:, idx:idx + 1], gc_c=gc_c, gc_r=gcr[idx:idx + 1, :],
                                 ge=gc_c[0:1] if rev else gc_c[c - 1:c]))

    def same_block(m):
        sh = int(math.log2(m))
        return jnp.right_shift(row, sh) == jnp.right_shift(colj, sh)

    for s in seqs:
        s['decay'] = jnp.where(s['incl'], jnp.exp(jnp.where(s['incl'], s['gc_c'] - s['gc_r'], 0.0)), 0.0)
        s['a'] = jnp.where(s['strict'], s['beta'] * s['kk'] * s['decay'], 0.0)
        s['t'] = eye - jnp.where(same_block(2), s['a'], 0.0)
    m = 4
    while m <= c:
        between = jnp.logical_and(same_block(m), jnp.logical_not(same_block(m // 2)))
        for s in seqs:
            s['te'] = _mm(s['t'], jnp.where(between, s['a'], 0.0), prec)
        for s in seqs:
            s['t'] = s['t'] - _mm(s['te'], s['t'], prec)
        m *= 2
    for s in seqs:
        h = s['h']
        egc = jnp.exp(s['gc_c'])
        kf32 = s['k'].astype(F32)
        v = s['v_r'][0, :, h * HEAD_DIM:(h + 1) * HEAD_DIM].astype(F32)
        rhs = jnp.concatenate([s['beta'] * v, (s['beta'] * egc) * kf32], axis=1)
        s['sol'] = _mm(s['t'], rhs, prec)
        s['q_dec'] = s['q'].astype(F32) * egc
        s['k_dec'] = kf32 * jnp.exp(s['ge'] - s['gc_c'])
    for s in seqs:
        s['ws'] = _mm(jnp.concatenate([s['sol'][:, HEAD_DIM:], s['q_dec']], axis=0), s_ref[s['d'], s['h']])
    for s in seqs:
        s['u'] = s['sol'][:, :HEAD_DIM] - s['ws'][:c]
        s_ref[s['d'], s['h']] = (jnp.exp(s['ge']) * s_ref[s['d'], s['h']]
                                 + _mm(s['k_dec'], s['u'], dims=tn_dims))
    for s in seqs:
        h = s['h']
        o = s['ws'][c:] + _mm(s['qk'] * s['decay'], s['u'])
        s['o_r'][0, :, h * HEAD_DIM:(h + 1) * HEAD_DIM] = o.astype(s['o_r'].dtype)

    @pl.when(i == nc - 1)
    def _():
        sfin_ref[0] = s_ref[...]


def _gdn_call(q, k, v, go, rows, s0, prec):
    b, t, _ = q.shape
    c = GDN_CHUNK
    nc = t // c
    fwd = lambda i, n: (i, n, 0)
    bwd = lambda i, n: (i, nc - 1 - n, 0)
    rfwd = lambda i, n: (i, n, 0, 0)
    rbwd = lambda i, n: (i, nc - 1 - n, 0, 0)
    state_spec = pl.BlockSpec((1,) + s0.shape[1:], lambda i, n: (i, 0, 0, 0, 0))

    def specs(m3, mr):
        return [pl.BlockSpec((1, c, QK_DIM), m3), pl.BlockSpec((1, c, QK_DIM), m3),
                pl.BlockSpec((1, c, V_DIM), m3), pl.BlockSpec((1, c, LANES), m3),
                pl.BlockSpec((1, 1, 2 * NV_HEADS, c), mr)]

    return pl.pallas_call(
        functools.partial(_gdn_kernel, prec=prec),
        grid=(b, nc),
        in_specs=specs(fwd, rfwd) + specs(bwd, rbwd) + [state_spec],
        out_specs=[pl.BlockSpec((1, c, V_DIM), fwd), pl.BlockSpec((1, c, V_DIM), bwd), state_spec],
        out_shape=[jax.ShapeDtypeStruct((b, t, V_DIM), BF16), jax.ShapeDtypeStruct((b, t, V_DIM), BF16),
                   jax.ShapeDtypeStruct(s0.shape, F32)],
        scratch_shapes=[pltpu.VMEM(s0.shape[1:], F32)],
        compiler_params=_cparams(("parallel", "arbitrary"), 48),
        name="gdn",
    )(q, k, v, go, rows, q, k, v, go, rows, s0)


def _gdn_decay_rows(go):
    b, t, _ = go.shape
    c = GDN_CHUNK
    lg = go[..., GATE_LANE0:GATE_LANE0 + 2 * NV_HEADS]
    return jnp.transpose(lg.reshape(b, t // c, c, 2 * NV_HEADS), (0, 1, 3, 2))


def _fnet1_kernel(x_ref, f_ref, ar_ref, ai_ref):
    a = _mm(f_ref[...], x_ref[0], HIGHEST)
    n = ar_ref.shape[1]
    ar_ref[0] = a[:n]
    ai_ref[0] = a[n:]


def _fnet2_kernel(ar_ref, ai_ref, gc_ref, gs_ref, cc_ref, sc_ref, o_ref, *, mb, scale):
    n = gc_ref.shape[1]
    cdim = ar_ref.shape[2]
    for m in range(mb):
        ar = ar_ref[0, m * n:(m + 1) * n, :]
        ai = ai_ref[0, m * n:(m + 1) * n, :]
        gc = gc_ref[m]
        gs = gs_ref[m]
        zr = _mm(gc, ar, HIGHEST) + _mm(gs, ai, HIGHEST)
        zi = _mm(gc, ai, HIGHEST) - _mm(gs, ar, HIGHEST)
        for g0 in range(0, cdim, HEAD_DIM):
            y = (_mm(zr[:, g0:g0 + HEAD_DIM], cc_ref[...], HIGHEST)
                 + _mm(zi[:, g0:g0 + HEAD_DIM], sc_ref[...], HIGHEST))
            o_ref[0, :, m * cdim + g0:m * cdim + g0 + HEAD_DIM] = (y * scale).astype(o_ref.dtype)


def _fnet_tables(n):
    a = np.arange(n)
    ang1 = 2.0 * np.pi * np.outer(a, a) / n
    f1 = np.concatenate([np.cos(ang1), -np.sin(ang1)], axis=0)
    m = a[:, None] + n * a[None, :]
    ang2 = 2.0 * np.pi * ((m[:, :, None] * a[None, None, :]) % (n * n)) / (n * n)
    angc = 2.0 * np.pi * np.outer(np.arange(HEAD_DIM), np.arange(HEAD_DIM)) / HEAD_DIM
    f = lambda x: jnp.asarray(x, F32)
    return f(f1), f(np.cos(ang2)), f(np.sin(ang2)), f(np.cos(angc)), f(np.sin(angc))


def _fnet_call(f):
    b, l, c = f.shape
    n = GRID_W
    assert l == n * n
    f1, gcos, gsin, ccos, csin = _fnet_tables(n)
    tn = 4096
    ar, ai = pl.pallas_call(
        _fnet1_kernel,
        grid=(b, n * c // tn),
        in_specs=[pl.BlockSpec((1, n, tn), lambda i, j: (i, 0, j)),
                  pl.BlockSpec((2 * n, n), lambda i, j: (0, 0))],
        out_specs=[pl.BlockSpec((1, n, tn), lambda i, j: (i, 0, j))] * 2,
        out_shape=[jax.ShapeDtypeStruct((b, n, n * c), F32)] * 2,
        compiler_params=_cparams(("parallel", "parallel"), 32),
        name="fnet1",
    )(f.reshape(b, n, n * c), f1)
    mb = 8
    out = pl.pallas_call(
        functools.partial(_fnet2_kernel, mb=mb, scale=1.0 / math.sqrt(l * HEAD_DIM)),
        grid=(b, n // mb),
        in_specs=[pl.BlockSpec((1, n * mb, c), lambda i, j: (i, j, 0)),
                  pl.BlockSpec((1, n * mb, c), lambda i, j: (i, j, 0)),
                  pl.BlockSpec((mb, n, n), lambda i, j: (j, 0, 0)),
                  pl.BlockSpec((mb, n, n), lambda i, j: (j, 0, 0)),
                  pl.BlockSpec((HEAD_DIM, HEAD_DIM), lambda i, j: (0, 0)),
                  pl.BlockSpec((HEAD_DIM, HEAD_DIM), lambda i, j: (0, 0))],
        out_specs=pl.BlockSpec((1, n, c * mb), lambda i, j: (i, 0, j)),
        out_shape=jax.ShapeDtypeStruct((b, n, n * c), BF16),
        compiler_params=_cparams(("parallel", "parallel"), 32),
        name="fnet2",
    )(ar.reshape(b, l, c), ai.reshape(b, l, c), gcos, gsin, ccos, csin)
    return out.reshape(b, l, c)


def _merge_kernel(of_ref, ob_ref, z_ref, fm_ref, gab_ref, x_ref, g1_ref, sh2_ref, sc2_ref, gn_ref, n2_ref,
                  wg_ref, wf_ref, wm_ref, wr_ref, br_ref, x1_ref, h2_ref, lg_ref):
    d = x_ref.shape[1]
    o = of_ref[...].astype(F32) + ob_ref[...].astype(F32)
    z = z_ref[...].astype(F32)
    parts = []
    for h0 in range(0, V_DIM, HEAD_DIM):
        oh = o[:, h0:h0 + HEAD_DIM]
        parts.append(oh * lax.rsqrt(jnp.mean(oh * oh, axis=-1, keepdims=True) + EPS) * gn_ref[...])
    yb_in = jnp.concatenate(parts, axis=1) * (z * _sigmoid(z))
    yb = _mm(yb_in, wg_ref[...])
    ya = _mm(fm_ref[...], wf_ref[...])
    ga = _sigmoid(gab_ref[:, :d].astype(F32))
    gb = _sigmoid(gab_ref[:, d:].astype(F32))
    mm = _mm(ga * ya + gb * yb, wm_ref[...])
    x1 = x_ref[...] + g1_ref[0] * mm
    x1_ref[...] = x1
    h2 = _rmsnorm(x1, n2_ref[...]) * (1.0 + sc2_ref[0]) + sh2_ref[0]
    h2_ref[...] = h2
    lg_ref[...] = _mm(h2, wr_ref[...], HIGHEST) + br_ref[...]


def _merge_call(of, ob, z, fm, gab, x2, g1, sh2, sc2, gn, n2, wg, wf, wm, wr, br, tokens_per_batch, tm):
    n, d = x2.shape
    per = tokens_per_batch // tm
    tok = lambda wd: pl.BlockSpec((tm, wd), lambda i: (i, 0))
    vec = pl.BlockSpec((1, 1, d), lambda i: (i // per, 0, 0))
    full = lambda a: pl.BlockSpec(a.shape, lambda i: (0,) * a.ndim)
    return pl.pallas_call(
        _merge_kernel,
        grid=(n // tm,),
        in_specs=[tok(V_DIM), tok(V_DIM), tok(V_DIM), tok(F_DIM), tok(2 * d), tok(d), vec, vec, vec,
                  full(gn), full(n2), full(wg), full(wf), full(wm), full(wr), full(br)],
        out_specs=[tok(d), tok(d), tok(LANES)],
        out_shape=[jax.ShapeDtypeStruct((n, d), F32), jax.ShapeDtypeStruct((n, d), F32),
                   jax.ShapeDtypeStruct((n, LANES), F32)],
        compiler_params=_cparams(("parallel",), 56),
        name="merge",
    )(of, ob, z, fm, gab, x2, g1, sh2, sc2, gn, n2, wg, wf, wm, wr, br)


def _route_kernel(lg_ref, idx_ref, w_ref, rank_ref, cnt_ref, run_ref):
    i = pl.program_id(0)

    @pl.when(i == 0)
    def _():
        run_ref[...] = jnp.zeros_like(run_ref)

    l = lg_ref[...]
    tm = l.shape[0]
    lane = lax.broadcasted_iota(I32, l.shape, 1)
    vals, idxs = [], []
    for _ in range(TOP_K):
        m = jnp.max(l, axis=-1, keepdims=True)
        idx = jnp.min(jnp.where(l == m, lane, LANES), axis=-1, keepdims=True)
        vals.append(m)
        idxs.append(idx)
        l = jnp.where(lane == idx, NEG_BIG * 2.0, l)
    es = [jnp.exp(v - vals[0]) for v in vals]
    inv = 1.0 / (es[0] + es[1] + es[2] + es[3])
    picked = jnp.zeros(l.shape, F32)
    for idx in idxs:
        picked = picked + (lane == idx).astype(F32)
    r = lax.broadcasted_iota(I32, (tm, tm), 0)
    cidx = lax.broadcasted_iota(I32, (tm, tm), 1)
    before = _mm(jnp.where(cidx < r, 1.0, 0.0), picked) + run_ref[...]
    idx_out = jnp.zeros(l.shape, I32)
    w_out = jnp.zeros(l.shape, F32)
    rank_out = jnp.zeros(l.shape, F32)
    for k in range(TOP_K):
        rk = jnp.sum(jnp.where(lane == idxs[k], before, 0.0), axis=-1, keepdims=True)
        idx_out = jnp.where(lane == k, idxs[k], idx_out)
        w_out = jnp.where(lane == k, es[k] * inv, w_out)
        rank_out = jnp.where(lane == k, rk, rank_out)
    idx_ref[...] = idx_out
    w_ref[...] = w_out
    rank_ref[...] = rank_out.astype(I32)
    run_ref[...] = run_ref[...] + jnp.sum(picked, axis=0, keepdims=True)
    cnt_ref[...] = run_ref[...]


def _route_call(logits, tm):
    n = logits.shape[0]
    tok = pl.BlockSpec((tm, LANES), lambda i: (i, 0))
    return pl.pallas_call(
        _route_kernel,
        grid=(n // tm,),
        in_specs=[tok],
        out_specs=[tok, tok, tok, pl.BlockSpec((1, LANES), lambda i: (0, 0))],
        out_shape=[jax.ShapeDtypeStruct((n, LANES), I32), jax.ShapeDtypeStruct((n, LANES), F32),
                   jax.ShapeDtypeStruct((n, LANES), I32), jax.ShapeDtypeStruct((1, LANES), F32)],
        scratch_shapes=[pltpu.VMEM((1, LANES), F32)],
        compiler_params=_cparams(("arbitrary",), 32),
        name="route",
    )(logits)


def _expert_kernel(be_ref, nv_ref, src_ref, nxt_ref, dst_ref, h_hbm, wg_ref, wu_ref, wd_ref, bg_ref, bu_ref, bd_ref,
                   y_hbm, xbuf, ybuf, wgb, wub, wdb, sem_in, sem_out):
    i = pl.program_id(0)
    nv = nv_ref[0]
    tmb = xbuf.shape[1]
    de = wgb.shape[1]
    slot = lax.rem(i, 2)

    def gather_row(idx_ref, s, r):
        pltpu.make_async_copy(h_hbm.at[pl.ds(idx_ref[0, 0, r], 1)], xbuf.at[s, pl.ds(r, 1)],
                              sem_in.at[s]).start(priority=r % 2)

    def wait_gather(s):
        pltpu.make_async_copy(xbuf.at[s], xbuf.at[s], sem_in.at[s]).wait()

    def wait_scatter():
        pltpu.make_async_copy(ybuf, ybuf, sem_out.at[0]).wait()

    @pl.when(i == 0)
    def _():
        for r in range(tmb):
            gather_row(src_ref, 0, r)
        ybuf[...] = jnp.zeros_like(ybuf)
        pltpu.make_async_copy(ybuf, y_hbm.at[pl.ds(y_hbm.shape[0] - tmb, tmb)], sem_out.at[0]).start()

    @pl.when(i == nv)
    def _():
        wait_gather(slot)
        wait_scatter()

    @pl.when(i < nv)
    def _():
        changed = jnp.logical_or(i == 0, be_ref[i] != be_ref[jnp.maximum(i - 1, 0)])

        @pl.when(changed)
        def _():
            wgb[...] = wg_ref[0].astype(BF16)
            wub[...] = wu_ref[0].astype(BF16)
            wdb[...] = wd_ref[0].astype(BF16)

        wait_gather(slot)
        x = xbuf[slot].astype(BF16)
        pieces = 2 * (de // EXPERT_COLS)
        per_piece = tmb // pieces
        issued = 0
        acts = []
        for c0 in range(0, de, EXPERT_COLS):
            cs = slice(c0, c0 + EXPERT_COLS)
            gate = jnp.minimum(_mm(x, wgb[:, cs]) + bg_ref[0, :, cs], SWIGLU_LIMIT)
            for r in range(issued, issued + per_piece):
                gather_row(nxt_ref, 1 - slot, r)
            issued += per_piece
            up = jnp.clip(_mm(x, wub[:, cs]) + bu_ref[0, :, cs], -SWIGLU_LIMIT, SWIGLU_LIMIT)
            for r in range(issued, issued + per_piece):
                gather_row(nxt_ref, 1 - slot, r)
            issued += per_piece
            acts.append(((up + 1.0) * gate * _sigmoid(SWIGLU_ALPHA * gate)).astype(BF16))
        act = jnp.concatenate(acts, axis=1)
        y = _mm(act, wdb[...]) + bd_ref[0]
        wait_scatter()
        ybuf[...] = y
        for r in range(tmb):
            pltpu.make_async_copy(ybuf.at[pl.ds(r, 1)], y_hbm.at[pl.ds(dst_ref[0, 0, r], 1)],
                                  sem_out.at[0]).start(priority=r % 2)


def _expert_call(block_e, n_valid, slot_src, slot_dst, h2, w_gate, w_up, w_down, b_gate, b_up, b_down, n_rows_out):
    nb = block_e.shape[0]
    tmb = EXPERT_BLOCK
    n, d = h2.shape
    de = w_gate.shape[2]
    block_e = jnp.concatenate([block_e, block_e[-1:]])
    src3 = jnp.concatenate([slot_src, jnp.zeros((tmb,), I32)]).reshape(nb + 1, 1, tmb)
    dst3 = jnp.concatenate([slot_dst, slot_dst[-tmb:]]).reshape(nb + 1, 1, tmb)
    smem_blk = lambda m: pl.BlockSpec((1, 1, tmb), m, memory_space=pltpu.SMEM)
    wspec = lambda s: pl.BlockSpec((1,) + s, lambda i, be, nv: (be[i], 0, 0))
    grid_spec = pltpu.PrefetchScalarGridSpec(
        num_scalar_prefetch=2,
        grid=(nb + 1,),
        in_specs=[smem_blk(lambda i, be, nv: (i, 0, 0)),
                  smem_blk(lambda i, be, nv: (jnp.minimum(i + 1, nb), 0, 0)),
                  smem_blk(lambda i, be, nv: (i, 0, 0)),
                  pl.BlockSpec(memory_space=pl.ANY),
                  wspec((d, de)), wspec((d, de)), wspec((de, d)),
                  wspec((1, de)), wspec((1, de)), wspec((1, d))],
        out_specs=pl.BlockSpec(memory_space=pl.ANY),
        scratch_shapes=[pltpu.VMEM((2, tmb, d), F32), pltpu.VMEM((tmb, d), F32),
                        pltpu.VMEM((d, de), BF16), pltpu.VMEM((d, de), BF16), pltpu.VMEM((de, d), BF16),
                        pltpu.SemaphoreType.DMA((2,)), pltpu.SemaphoreType.DMA((1,))],
    )
    ne = w_gate.shape[0]
    return pl.pallas_call(
        _expert_kernel,
        grid_spec=grid_spec,
        out_shape=jax.ShapeDtypeStruct((n_rows_out, d), F32),
        compiler_params=_cparams(("arbitrary",), 56),
        name="expert",
    )(block_e, n_valid, src3, src3, dst3, h2, w_gate, w_up, w_down,
      b_gate.reshape(ne, 1, de), b_up.reshape(ne, 1, de), b_down.reshape(ne, 1, d))


def _combine_kernel(y0, y1, y2, y3, w_ref, x1_ref, g2_ref, fg_ref, o_ref):
    w = w_ref[...]
    moe = (w[:, 0:1] * y0[...] + w[:, 1:2] * y1[...]) + (w[:, 2:3] * y2[...] + w[:, 3:4] * y3[...])
    o_ref[...] = _rmsnorm(x1_ref[...] + g2_ref[0] * moe, fg_ref[...])


def _combine_call(y4, top_w, x1, g2, fg, tokens_per_batch, tm):
    n, d = x1.shape
    per = tokens_per_batch // tm
    nt = n // tm
    yspec = lambda k: pl.BlockSpec((tm, d), lambda i: (k * nt + i, 0))
    return pl.pallas_call(
        _combine_kernel,
        grid=(nt,),
        in_specs=[yspec(0), yspec(1), yspec(2), yspec(3),
                  pl.BlockSpec((tm, LANES), lambda i: (i, 0)),
                  pl.BlockSpec((tm, d), lambda i: (i, 0)),
                  pl.BlockSpec((1, 1, d), lambda i: (i // per, 0, 0)),
                  pl.BlockSpec((1, d), lambda i: (0, 0))],
        out_specs=pl.BlockSpec((tm, d), lambda i: (i, 0)),
        out_shape=jax.ShapeDtypeStruct((n, d), F32),
        compiler_params=_cparams(("parallel",), 48),
        name="combine",
    )(y4, y4, y4, y4, top_w, x1, g2, fg)


def _routing_tables(top_idx, rank, counts, n):
    tmb = EXPERT_BLOCK
    counts = counts.astype(I32)
    padded = (counts + tmb - 1) // tmb * tmb
    pad_end = jnp.cumsum(padded)
    pad_start = pad_end - padded
    n_blocks = -(-(n * TOP_K + N_EXPERTS * (tmb - 1)) // tmb)
    n_slots = n_blocks * tmb
    onehot = top_idx[:, :, None] == jnp.arange(N_EXPERTS, dtype=I32)[None, None, :]
    dest = jnp.sum(jnp.where(onehot, pad_start[None, None, :], 0), axis=-1) + rank
    flat = jnp.arange(TOP_K, dtype=I32)[None, :] * n + jnp.arange(n, dtype=I32)[:, None]
    slot_flat = jnp.full((n_slots,), -1, I32).at[dest.reshape(-1)].set(flat.reshape(-1))
    used = slot_flat >= 0
    slot_src = jnp.where(used, slot_flat % n, 0)
    slot_dst = jnp.where(used, slot_flat, n * TOP_K + jnp.arange(n_slots, dtype=I32) % tmb)
    block_start = jnp.arange(n_blocks, dtype=I32) * tmb
    block_e = jnp.minimum(jnp.sum((pad_end[None, :] <= block_start[:, None]).astype(I32), axis=1), N_EXPERTS - 1)
    n_valid = (pad_end[-1:] // tmb).astype(I32)
    return block_e, n_valid, slot_src, slot_dst


def _gdn_branch(xtok, sh, sc, norm_g, w_cols, widths, dtypes, conv_w, par, tokens_per_batch, grid_w, use_rows,
                tm_proj, tm_conv, s0, prec):
    b = xtok.shape[0] // tokens_per_batch
    outs = _inproj_call(xtok, sh, sc, norm_g, w_cols, widths, dtypes, tokens_per_batch, tm_proj)
    qkv, gates = outs[0], outs[-1]
    q, k, v, go = _conv_call(qkv.reshape(b, tokens_per_batch, QKV_DIM), conv_w,
                             gates.reshape(b, tokens_per_batch, LANES), par, grid_w, use_rows, tm_conv)
    o_f, o_b, s_fin = _gdn_call(q, k, v, go, _gdn_decay_rows(go), s0, prec)
    return outs, o_f, o_b, s_fin


def kernel(x, c, ctx, c_ctx, w_mod, b_mod, norm1_g, norm2_g, w_in, conv_w, a_log, dt_bias, gdn_norm_g,
           w_fourier_out, w_gdn_out, w_merge_out, w_router, b_router, w_gate, b_gate, w_up, b_up,
           w_down, b_down, final_norm_g):
    b, l, d = x.shape
    n = b * l
    n_ctx = ctx.shape[1]
    assert w_mod.shape[0] == 1 and l == GRID_W * GRID_W and d == V_DIM
    prec = None

    c8 = jnp.concatenate([c, c_ctx[None, :], jnp.zeros((8 - b - 1, d), F32)], axis=0)
    mod = _mod_call(c8, w_mod[0], b_mod[0])
    sh1, sc1, g1, sh2, sc2, g2 = [mod[:b, j * d:(j + 1) * d].reshape(b, 1, d) for j in range(6)]
    csh1 = jnp.broadcast_to(mod[b:b + 1, 0:d].reshape(1, 1, d), (b, 1, d))
    csc1 = jnp.broadcast_to(mod[b:b + 1, d:2 * d].reshape(1, 1, d), (b, 1, d))

    wi = w_in[0]
    off_gate = QKV_DIM
    off_z = off_gate + 4 * NV_HEADS
    off_f = off_z + V_DIM
    off_ga = off_f + F_DIM
    gate_cols = jnp.pad(wi[:, off_gate:off_z], ((0, 0), (0, LANES - 4 * NV_HEADS)))
    w_lat = jnp.concatenate([wi[:, :QKV_DIM], wi[:, off_z:off_f], wi[:, off_f:off_ga], wi[:, off_ga:], gate_cols],
                            axis=1).astype(BF16)
    w_ctx = jnp.concatenate([wi[:, :QKV_DIM], gate_cols], axis=1).astype(BF16)
    par = jnp.pad(jnp.stack([a_log[0].reshape(-1), dt_bias[0].reshape(-1)]),
                  ((0, 6), (GATE_LANE0, LANES - 2 * GATE_LANE0)))
    n1 = norm1_g[0].reshape(1, d)
    cw = conv_w[0].reshape(9, QKV_DIM)

    zero_state = jnp.zeros((b, 2, NV_HEADS, HEAD_DIM, HEAD_DIM), F32)
    _, _, _, s_ctx = _gdn_branch(ctx.reshape(b * n_ctx, d), csh1, csc1, n1, w_ctx, (QKV_DIM, LANES), (BF16, F32),
                                 cw, par, n_ctx, n_ctx, False, n_ctx, n_ctx, zero_state, prec)

    x2 = x.reshape(n, d)
    outs, o_f, o_b, _ = _gdn_branch(x2, sh1, sc1, n1, w_lat, (QKV_DIM, V_DIM, F_DIM, 2 * d, LANES),
                                    (BF16, BF16, F32, BF16, F32), cw, par, l, GRID_W, True, 512, 512, s_ctx, prec)
    _, z, f, gab, _ = outs
    fmix = _fnet_call(f.reshape(b, l, F_DIM)).reshape(n, F_DIM)

    wr = jnp.pad(w_router[0], ((0, 0), (0, LANES - N_EXPERTS)))
    br = jnp.pad(b_router[0], (0, LANES - N_EXPERTS), constant_values=NEG_BIG).reshape(1, LANES)
    x1, h2, logits = _merge_call(
        o_f.reshape(n, V_DIM), o_b.reshape(n, V_DIM), z, fmix, gab, x2, g1, sh2, sc2,
        gdn_norm_g[0].reshape(1, HEAD_DIM), norm2_g[0].reshape(1, d),
        w_gdn_out[0].astype(BF16), w_fourier_out[0].astype(BF16), w_merge_out[0].astype(BF16), wr, br, l, 256)

    top_idx, top_w, rank, counts = _route_call(logits, 512)
    block_e, n_valid, slot_src, slot_dst = _routing_tables(top_idx[:, :TOP_K], rank[:, :TOP_K],
                                                           counts[0, :N_EXPERTS], n)
    y4 = _expert_call(block_e, n_valid, slot_src, slot_dst, h2, w_gate[0], w_up[0], w_down[0],
                      b_gate[0], b_up[0], b_down[0], n * TOP_K + EXPERT_BLOCK)
    out = _combine_call(y4, top_w, x1, g2, final_norm_g.reshape(1, d), l, 256)
    return out.reshape(b, l, d)
```

```python
import functools
import math

import jax
import jax.numpy as jnp
import numpy as np
from jax import lax
from jax.experimental import pallas as pl
from jax.experimental.pallas import tpu as pltpu

F32 = jnp.float32
BF16 = jnp.bfloat16
I32 = jnp.int32
HIGHEST = lax.Precision.HIGHEST

GRID_W = 64
NQK_HEADS = 4
NV_HEADS = 8
HEAD_DIM = 128
QK_DIM = NQK_HEADS * HEAD_DIM
V_DIM = NV_HEADS * HEAD_DIM
QKV_DIM = 2 * QK_DIM + V_DIM
F_GROUPS = 4
F_DIM = F_GROUPS * HEAD_DIM
N_EXPERTS = 32
TOP_K = 4
SWIGLU_ALPHA = 1.702
SWIGLU_LIMIT = 7.0
EPS = 1e-6

LANES = 128
SUBLANES = 8
GATE_LANE0 = 16
GDN_CHUNK = 128
EXPERT_BLOCK = 256
EXPERT_COLS = 256
NEG_BIG = -1e30
MIB = 2 ** 20


def _cparams(sem, vmem_mib):
    return pltpu.CompilerParams(dimension_semantics=sem, vmem_limit_bytes=vmem_mib * MIB)


def _mm(a, b, prec=None, dims=(((1,), (0,)), ((), ()))):
    if prec is None:
        return lax.dot_general(a.astype(BF16), b.astype(BF16), dims, preferred_element_type=F32)
    return lax.dot_general(a.astype(F32), b.astype(F32), dims, precision=prec, preferred_element_type=F32)


def _sigmoid(x):
    return 1.0 / (1.0 + jnp.exp(-x))


def _rmsnorm(x, g):
    return x * lax.rsqrt(jnp.mean(x * x, axis=-1, keepdims=True) + EPS) * g


def _store_row_tiles(ref, x):
    w = ref.shape[-1]
    for s in range(ref.shape[-2]):
        ref[:, s, :] = x[:, s * w:(s + 1) * w]


def _load_row_tiles(ref):
    return jnp.concatenate([ref[:, s, :] for s in range(ref.shape[-2])], axis=-1)


def _mod_kernel(c_ref, w_ref, b_ref, o_ref):
    c = c_ref[...]
    o_ref[...] = _mm(c * _sigmoid(c), w_ref[...], HIGHEST) + b_ref[...]


def _mod_call(c8, w_mod, b_mod):
    d, n = w_mod.shape
    tn = 1536
    return pl.pallas_call(
        _mod_kernel,
        grid=(n // tn,),
        in_specs=[pl.BlockSpec((8, d), lambda j: (0, 0)),
                  pl.BlockSpec((d, tn), lambda j: (0, j)),
                  pl.BlockSpec((1, tn), lambda j: (0, j))],
        out_specs=pl.BlockSpec((8, tn), lambda j: (0, j)),
        out_shape=jax.ShapeDtypeStruct((8, n), F32),
        compiler_params=_cparams(("parallel",), 32),
        name="mod",
    )(c8, w_mod, b_mod.reshape(1, n))


def _inproj_kernel(x_ref, sh_ref, sc_ref, g_ref, w_ref, *o_refs, widths):
    u = (_rmsnorm(x_ref[...], g_ref[...]) * (1.0 + sc_ref[0]) + sh_ref[0]).astype(BF16)
    start = 0
    for o_ref, width in zip(o_refs, widths):
        step = min(width, 512)
        for c0 in range(0, width, step):
            o_ref[:, c0:c0 + step] = _mm(u, w_ref[:, start + c0:start + c0 + step]).astype(o_ref.dtype)
        start += width


def _inproj_call(x2, sh, sc, g, w, widths, dtypes, tokens_per_batch, tm):
    n, d = x2.shape
    per = tokens_per_batch // tm
    out_shape = [jax.ShapeDtypeStruct((n, wd), dt) for wd, dt in zip(widths, dtypes)]
    return pl.pallas_call(
        functools.partial(_inproj_kernel, widths=widths),
        grid=(n // tm,),
        in_specs=[pl.BlockSpec((tm, d), lambda i: (i, 0)),
                  pl.BlockSpec((1, 1, d), lambda i: (i // per, 0, 0)),
                  pl.BlockSpec((1, 1, d), lambda i: (i // per, 0, 0)),
                  pl.BlockSpec((1, d), lambda i: (0, 0)),
                  pl.BlockSpec(w.shape, lambda i: (0, 0))],
        out_specs=[pl.BlockSpec((tm, wd), lambda i: (i, 0)) for wd in widths],
        out_shape=out_shape,
        compiler_params=_cparams(("parallel",), 56),
        name="inproj",
    )(x2, sh, sc, g, w)


def _conv_kernel(*refs, grid_w, use_rows, tm, cw):
    if use_rows:
        prev_ref, main_ref, next_ref, w_ref, gates_ref, par_ref, q_ref, k_ref, v_ref, go_ref = refs
    else:
        main_ref, w_ref, gates_ref, par_ref, q_ref, k_ref, v_ref, go_ref = refs
    r = pl.program_id(1)
    nr = pl.num_programs(1)
    t = lax.broadcasted_iota(I32, (tm, 1), 0)
    col = jnp.bitwise_and(t, grid_w - 1)
    m_left = (col != 0).astype(F32)
    m_right = (col != grid_w - 1).astype(F32)
    has_prev = (r > 0).astype(F32)
    has_next = (r < nr - 1).astype(F32)
    for c0 in range(0, QKV_DIM, cw):
        xm = main_ref[0, :, c0:c0 + cw].astype(F32)
        if use_rows:
            xp = prev_ref[0, :, c0:c0 + cw].astype(F32) * has_prev
            xn = next_ref[0, :, c0:c0 + cw].astype(F32) * has_next
            up = jnp.concatenate([xp, xm[:tm - grid_w]], axis=0)
            dn = jnp.concatenate([xm[grid_w:], xn], axis=0)

        def colsum(kc):
            y = xm * w_ref[3 + kc:4 + kc, c0:c0 + cw]
            if use_rows:
                y = y + up * w_ref[kc:kc + 1, c0:c0 + cw] + dn * w_ref[6 + kc:7 + kc, c0:c0 + cw]
            return y

        acc = (colsum(1) + pltpu.roll(colsum(0), 1, axis=0) * m_left
               + pltpu.roll(colsum(2), tm - 1, axis=0) * m_right)
        s = acc * _sigmoid(acc)
        for h0 in range(0, cw, HEAD_DIM):
            c = c0 + h0
            seg = s[:, h0:h0 + HEAD_DIM]
            if c < 2 * QK_DIM:
                seg = seg * lax.rsqrt(jnp.sum(seg * seg, axis=-1, keepdims=True) + EPS)
            if c < QK_DIM:
                q_ref[0, :, c:c + HEAD_DIM] = (seg * HEAD_DIM ** -0.5).astype(q_ref.dtype)
            elif c < 2 * QK_DIM:
                k_ref[0, :, c - QK_DIM:c - QK_DIM + HEAD_DIM] = seg.astype(k_ref.dtype)
            else:
                v_ref[0, :, c - 2 * QK_DIM:c - 2 * QK_DIM + HEAD_DIM] = seg.astype(v_ref.dtype)
    g = gates_ref[0]
    a = g + par_ref[1:2, :]
    softplus = jnp.maximum(a, 0.0) + jnp.log1p(jnp.exp(-jnp.abs(a)))
    log_g = -jnp.exp(par_ref[0:1, :]) * softplus
    lane = lax.broadcasted_iota(I32, g.shape, 1)
    go_ref[0] = jnp.where(lane < GATE_LANE0, _sigmoid(g), log_g)


def _conv_call(qkv, conv_w, gates, par, grid_w, use_rows, tm):
    b, t, c = qkv.shape
    kern = functools.partial(_conv_kernel, grid_w=grid_w, use_rows=use_rows, tm=tm, cw=512)
    per = tm // grid_w
    nrow = t // grid_w
    in_specs = []
    args = []
    if use_rows:
        in_specs.append(pl.BlockSpec((1, grid_w, c), lambda i, r: (i, jnp.maximum(r * per - 1, 0), 0)))
        args.append(qkv)
    in_specs.append(pl.BlockSpec((1, tm, c), lambda i, r: (i, r, 0)))
    args.append(qkv)
    if use_rows:
        in_specs.append(pl.BlockSpec((1, grid_w, c), lambda i, r: (i, jnp.minimum((r + 1) * per, nrow - 1), 0)))
        args.append(qkv)
    in_specs += [pl.BlockSpec(conv_w.shape, lambda i, r: (0, 0)),
                 pl.BlockSpec((1, tm, LANES), lambda i, r: (i, r, 0)),
                 pl.BlockSpec(par.shape, lambda i, r: (0, 0))]
    args += [conv_w, gates, par]
    out_shape = [jax.ShapeDtypeStruct((b, t, QK_DIM), BF16), jax.ShapeDtypeStruct((b, t, QK_DIM), BF16),
                 jax.ShapeDtypeStruct((b, t, V_DIM), BF16), jax.ShapeDtypeStruct((b, t, LANES), F32)]
    out_specs = [pl.BlockSpec((1, tm, QK_DIM), lambda i, r: (i, r, 0)),
                 pl.BlockSpec((1, tm, QK_DIM), lambda i, r: (i, r, 0)),
                 pl.BlockSpec((1, tm, V_DIM), lambda i, r: (i, r, 0)),
                 pl.BlockSpec((1, tm, LANES), lambda i, r: (i, r, 0))]
    return pl.pallas_call(
        kern, grid=(b, t // tm), in_specs=in_specs, out_specs=out_specs, out_shape=out_shape,
        compiler_params=_cparams(("parallel", "parallel"), 48),
        name="conv_rows" if use_rows else "conv_seq",
    )(*args)


def _gdn_kernel(qf, kf, vf, gf, rf, qb, kb, vb, gb, rb, s0_ref, of, ob, sfin_ref, s_ref, *, prec):
    i = pl.program_id(1)
    nc = pl.num_programs(1)

    @pl.when(i == 0)
    def _():
        s_ref[...] = s0_ref[0]

    c = qf.shape[1]
    per = NV_HEADS // NQK_HEADS
    row = lax.broadcasted_iota(I32, (c, c), 0)
    colj = lax.broadcasted_iota(I32, (c, c), 1)
    eye = jnp.where(row == colj, 1.0, 0.0)
    nt_dims = (((1,), (1,)), ((), ()))
    tn_dims = (((0,), (0,)), ((), ()))

    seqs = []
    for d, (q_r, k_r, v_r, g_r, r_r, o_r) in enumerate(((qf, kf, vf, gf, rf, of), (qb, kb, vb, gb, rb, ob))):
        rev = d == 1
        incl = (colj >= row) if rev else (colj <= row)
        strict = (colj > row) if rev else (colj < row)
        gates = g_r[0]
        tri_c = jnp.where(incl, 1.0, 0.0)
        gcm = _mm(tri_c, gates, HIGHEST)
        gcr = _mm(r_r[0, 0], tri_c, HIGHEST, dims=nt_dims)
        for hq in range(NQK_HEADS):
            q = q_r[0, :, hq * HEAD_DIM:(hq + 1) * HEAD_DIM]
            k = k_r[0, :, hq * HEAD_DIM:(hq + 1) * HEAD_DIM]
            kq = lax.dot_general(jnp.concatenate([q, k], axis=0), k, nt_dims, preferred_element_type=F32)
            for j in range(per):
                h = hq * per + j
                idx = d * NV_HEADS + h
                gc_c = gcm[:, GATE_LANE0 + idx:GATE_LANE0 + idx + 1]
                seqs.append(dict(d=d, h=h, o_r=o_r, v_r=v_r, q=q, k=k, qk=kq[:c], kk=kq[c:], incl=incl, strict=strict,
                                 beta=gates[:, idx:idx + 1], gc_c=gc_c, gc_r=gcr[idx:idx + 1, :],
                                 ge=gc_c[0:1] if rev else gc_c[c - 1:c]))

    def same_block(m):
        sh = int(math.log2(m))
        return jnp.right_shift(row, sh) == jnp.right_shift(colj, sh)

    for s in seqs:
        s['decay'] = jnp.where(s['incl'], jnp.exp(jnp.where(s['incl'], s['gc_c'] - s['gc_r'], 0.0)), 0.0)
        s['a'] = jnp.where(s['strict'], s['beta'] * s['kk'] * s['decay'], 0.0)
        s['t'] = eye - jnp.where(same_block(2), s['a'], 0.0)
    m = 4
    while m <= c:
        between = jnp.logical_and(same_block(m), jnp.logical_not(same_block(m // 2)))
        for s in seqs:
            s['te'] = _mm(s['t'], jnp.where(between, s['a'], 0.0), prec)
        for s in seqs:
            s['t'] = s['t'] - _mm(s['te'], s['t'], prec)
        m *= 2
    for s in seqs:
        h = s['h']
        egc = jnp.exp(s['gc_c'])
        kf32 = s['k'].astype(F32)
        v = s['v_r'][0, :, h * HEAD_DIM:(h + 1) * HEAD_DIM].astype(F32)
        rhs = jnp.concatenate([s['beta'] * v, (s['beta'] * egc) * kf32], axis=1)
        s['sol'] = _mm(s['t'], rhs, prec)
        s['q_dec'] = s['q'].astype(F32) * egc
        s['k_dec'] = kf32 * jnp.exp(s['ge'] - s['gc_c'])
    for s in seqs:
        s['ws'] = _mm(jnp.concatenate([s['sol'][:, HEAD_DIM:], s['q_dec']], axis=0), s_ref[s['d'], s['h']])
    for s in seqs:
        s['u'] = s['sol'][:, :HEAD_DIM] - s['ws'][:c]
        s_ref[s['d'], s['h']] = (jnp.exp(s['ge']) * s_ref[s['d'], s['h']]
                                 + _mm(s['k_dec'], s['u'], dims=tn_dims))
    for s in seqs:
        h = s['h']
        o = s['ws'][c:] + _mm(s['qk'] * s['decay'], s['u'])
        s['o_r'][0, :, h * HEAD_DIM:(h + 1) * HEAD_DIM] = o.astype(s['o_r'].dtype)

    @pl.when(i == nc - 1)
    def _():
        sfin_ref[0] = s_ref[...]


def _gdn_call(q, k, v, go, rows, s0, prec):
    b, t, _ = q.shape
    c = GDN_CHUNK
    nc = t // c
    fwd = lambda i, n: (i, n, 0)
    bwd = lambda i, n: (i, nc - 1 - n, 0)
    rfwd = lambda i, n: (i, n, 0, 0)
    rbwd = lambda i, n: (i, nc - 1 - n, 0, 0)
    state_spec = pl.BlockSpec((1,) + s0.shape[1:], lambda i, n: (i, 0, 0, 0, 0))

    def specs(m3, mr):
        return [pl.BlockSpec((1, c, QK_DIM), m3), pl.BlockSpec((1, c, QK_DIM), m3),
                pl.BlockSpec((1, c, V_DIM), m3), pl.BlockSpec((1, c, LANES), m3),
                pl.BlockSpec((1, 1, 2 * NV_HEADS, c), mr)]

    return pl.pallas_call(
        functools.partial(_gdn_kernel, prec=prec),
        grid=(b, nc),
        in_specs=specs(fwd, rfwd) + specs(bwd, rbwd) + [state_spec],
        out_specs=[pl.BlockSpec((1, c, V_DIM), fwd), pl.BlockSpec((1, c, V_DIM), bwd), state_spec],
        out_shape=[jax.ShapeDtypeStruct((b, t, V_DIM), BF16), jax.ShapeDtypeStruct((b, t, V_DIM), BF16),
                   jax.ShapeDtypeStruct(s0.shape, F32)],
        scratch_shapes=[pltpu.VMEM(s0.shape[1:], F32)],
        compiler_params=_cparams(("parallel", "arbitrary"), 48),
        name="gdn",
    )(q, k, v, go, rows, q, k, v, go, rows, s0)


def _gdn_decay_rows(go):
    b, t, _ = go.shape
    c = GDN_CHUNK
    lg = go[..., GATE_LANE0:GATE_LANE0 + 2 * NV_HEADS]
    return jnp.transpose(lg.reshape(b, t // c, c, 2 * NV_HEADS), (0, 1, 3, 2))


def _fnet1_kernel(x_ref, f_ref, ar_ref, ai_ref):
    n = x_ref.shape[1]
    for j in range(x_ref.shape[2]):
        a = _mm(f_ref[...], x_ref[0, :, j, :])
        ar_ref[0, j] = a[:n]
        ai_ref[0, j] = a[n:]


def _fnet2_kernel(ar_ref, ai_ref, g_ref, wc_ref, o_ref, *, scale):
    n = ar_ref.shape[1]
    zs = []
    for m in range(ar_ref.shape[2]):
        a2 = jnp.concatenate([ar_ref[0, :, m, :], ai_ref[0, :, m, :]], axis=0)
        zs.append(_mm(g_ref[m], a2))
    for m, z in enumerate(zs):
        y = _mm(jnp.concatenate([z[:n], z[n:]], axis=1), wc_ref[...])
        o_ref[0, :, m, :] = y * scale


def _fnet_tables(n, groups):
    a = np.arange(n)
    ang1 = 2.0 * np.pi * np.outer(a, a) / n
    f1 = np.concatenate([np.cos(ang1), -np.sin(ang1)], axis=0)
    m = a[:, None] + n * a[None, :]
    ang2 = 2.0 * np.pi * ((m[:, :, None] * a[None, None, :]) % (n * n)) / (n * n)
    gc, gs = np.cos(ang2), np.sin(ang2)
    g2 = np.concatenate([np.concatenate([gc, gs], axis=2), np.concatenate([-gs, gc], axis=2)], axis=1)
    angc = 2.0 * np.pi * np.outer(np.arange(HEAD_DIM), np.arange(HEAD_DIM)) / HEAD_DIM
    eye = np.eye(groups)
    wc = np.concatenate([np.kron(eye, np.cos(angc)), np.kron(eye, np.sin(angc))], axis=0)
    f = lambda x: jnp.asarray(x, F32).astype(BF16)
    return f(f1), f(g2), f(wc)


def _fnet_call(f):
    b, l, c = f.shape
    n = GRID_W
    assert l == n * n
    f1, g2, wc = _fnet_tables(n, c // HEAD_DIM)
    cols = SUBLANES
    ar, ai = pl.pallas_call(
        _fnet1_kernel,
        grid=(b, n // cols),
        in_specs=[pl.BlockSpec((1, n, cols, c), lambda i, j: (i, 0, j, 0)),
                  pl.BlockSpec((2 * n, n), lambda i, j: (0, 0))],
        out_specs=[pl.BlockSpec((1, cols, n, c), lambda i, j: (i, j, 0, 0))] * 2,
        out_shape=[jax.ShapeDtypeStruct((b, n, n, c), F32)] * 2,
        compiler_params=_cparams(("parallel", "parallel"), 32),
        name="fnet1",
    )(f.reshape(b, n, n, c), f1)
    out = pl.pallas_call(
        functools.partial(_fnet2_kernel, scale=1.0 / math.sqrt(l * HEAD_DIM)),
        grid=(b, n // cols),
        in_specs=[pl.BlockSpec((1, n, cols, c), lambda i, j: (i, 0, j, 0)),
                  pl.BlockSpec((1, n, cols, c), lambda i, j: (i, 0, j, 0)),
                  pl.BlockSpec((cols, 2 * n, 2 * n), lambda i, j: (j, 0, 0)),
                  pl.BlockSpec((2 * c, c), lambda i, j: (0, 0))],
        out_specs=pl.BlockSpec((1, n, cols, c), lambda i, j: (i, 0, j, 0)),
        out_shape=jax.ShapeDtypeStruct((b, n, n, c), F32),
        compiler_params=_cparams(("parallel", "parallel"), 32),
        name="fnet2",
    )(ar, ai, g2, wc)
    return out.reshape(b, l, c)


def _merge_kernel(of_ref, ob_ref, z_ref, fm_ref, gab_ref, x_ref, g1_ref, sh2_ref, sc2_ref, gn_ref, n2_ref,
                  wg_ref, wf_ref, wm_ref, wr_ref, br_ref, x1_ref, h2_ref, lg_ref):
    d = x_ref.shape[1]
    o = of_ref[...].astype(F32) + ob_ref[...].astype(F32)
    z = z_ref[...].astype(F32)
    parts = []
    for h0 in range(0, V_DIM, HEAD_DIM):
        oh = o[:, h0:h0 + HEAD_DIM]
        parts.append(oh * lax.rsqrt(jnp.mean(oh * oh, axis=-1, keepdims=True) + EPS) * gn_ref[...])
    yb_in = jnp.concatenate(parts, axis=1) * (z * _sigmoid(z))
    yb = _mm(yb_in, wg_ref[...])
    ya = _mm(fm_ref[...], wf_ref[...])
    ga = _sigmoid(gab_ref[:, :d].astype(F32))
    gb = _sigmoid(gab_ref[:, d:].astype(F32))
    mm = _mm(ga * ya + gb * yb, wm_ref[...])
    x1 = x_ref[...] + g1_ref[0] * mm
    x1_ref[...] = x1
    h2 = _rmsnorm(x1, n2_ref[...]) * (1.0 + sc2_ref[0]) + sh2_ref[0]
    _store_row_tiles(h2_ref, h2)
    lg_ref[...] = _mm(h2, wr_ref[...], HIGHEST) + br_ref[...]


def _merge_call(of, ob, z, fm, gab, x2, g1, sh2, sc2, gn, n2, wg, wf, wm, wr, br, tokens_per_batch, tm):
    n, d = x2.shape
    per = tokens_per_batch // tm
    tok = lambda wd: pl.BlockSpec((tm, wd), lambda i: (i, 0))
    vec = pl.BlockSpec((1, 1, d), lambda i: (i // per, 0, 0))
    full = lambda a: pl.BlockSpec(a.shape, lambda i: (0,) * a.ndim)
    return pl.pallas_call(
        _merge_kernel,
        grid=(n // tm,),
        in_specs=[tok(V_DIM), tok(V_DIM), tok(V_DIM), tok(F_DIM), tok(2 * d), tok(d), vec, vec, vec,
                  full(gn), full(n2), full(wg), full(wf), full(wm), full(wr), full(br)],
        out_specs=[tok(d), pl.BlockSpec((tm, SUBLANES, d // SUBLANES), lambda i: (i, 0, 0)), tok(LANES)],
        out_shape=[jax.ShapeDtypeStruct((n, d), F32), jax.ShapeDtypeStruct((n, SUBLANES, d // SUBLANES), F32),
                   jax.ShapeDtypeStruct((n, LANES), F32)],
        compiler_params=_cparams(("parallel",), 56),
        name="merge",
    )(of, ob, z, fm, gab, x2, g1, sh2, sc2, gn, n2, wg, wf, wm, wr, br)


def _route_kernel(lg_ref, idx_ref, w_ref, rank_ref, cnt_ref, run_ref):
    i = pl.program_id(0)

    @pl.when(i == 0)
    def _():
        run_ref[...] = jnp.zeros_like(run_ref)

    l = lg_ref[...]
    tm = l.shape[0]
    lane = lax.broadcasted_iota(I32, l.shape, 1)
    vals, idxs = [], []
    for _ in range(TOP_K):
        m = jnp.max(l, axis=-1, keepdims=True)
        idx = jnp.min(jnp.where(l == m, lane, LANES), axis=-1, keepdims=True)
        vals.append(m)
        idxs.append(idx)
        l = jnp.where(lane == idx, NEG_BIG * 2.0, l)
    es = [jnp.exp(v - vals[0]) for v in vals]
    inv = 1.0 / (es[0] + es[1] + es[2] + es[3])
    picked = jnp.zeros(l.shape, F32)
    for idx in idxs:
        picked = picked + (lane == idx).astype(F32)
    r = lax.broadcasted_iota(I32, (tm, tm), 0)
    cidx = lax.broadcasted_iota(I32, (tm, tm), 1)
    before = _mm(jnp.where(cidx < r, 1.0, 0.0), picked) + run_ref[...]
    idx_out = jnp.zeros(l.shape, I32)
    w_out = jnp.zeros(l.shape, F32)
    rank_out = jnp.zeros(l.shape, F32)
    for k in range(TOP_K):
        rk = jnp.sum(jnp.where(lane == idxs[k], before, 0.0), axis=-1, keepdims=True)
        idx_out = jnp.where(lane == k, idxs[k], idx_out)
        w_out = jnp.where(lane == k, es[k] * inv, w_out)
        rank_out = jnp.where(lane == k, rk, rank_out)
    idx_ref[...] = idx_out
    w_ref[...] = w_out
    rank_ref[...] = rank_out.astype(I32)
    run_ref[...] = run_ref[...] + jnp.sum(picked, axis=0, keepdims=True)
    cnt_ref[...] = run_ref[...]


def _route_call(logits, tm):
    n = logits.shape[0]
    tok = pl.BlockSpec((tm, LANES), lambda i: (i, 0))
    return pl.pallas_call(
        _route_kernel,
        grid=(n // tm,),
        in_specs=[tok],
        out_specs=[tok, tok, tok, pl.BlockSpec((1, LANES), lambda i: (0, 0))],
        out_shape=[jax.ShapeDtypeStruct((n, LANES), I32), jax.ShapeDtypeStruct((n, LANES), F32),
                   jax.ShapeDtypeStruct((n, LANES), I32), jax.ShapeDtypeStruct((1, LANES), F32)],
        scratch_shapes=[pltpu.VMEM((1, LANES), F32)],
        compiler_params=_cparams(("arbitrary",), 32),
        name="route",
    )(logits)


def _expert_kernel(be_ref, nv_ref, src_ref, nxt_ref, dst_ref, h_hbm, wg_ref, wu_ref, wd_ref, bg_ref, bu_ref, bd_ref,
                   y_hbm, xbuf, ybuf, wgb, wub, wdb, sem_in, sem_out):
    i = pl.program_id(0)
    nv = nv_ref[0]
    tmb = xbuf.shape[1]
    de = wgb.shape[1]
    slot = lax.rem(i, 2)

    def gather_row(idx_ref, s, r):
        pltpu.make_async_copy(h_hbm.at[pl.ds(idx_ref[0, 0, r], 1)], xbuf.at[s, pl.ds(r, 1)],
                              sem_in.at[s]).start(priority=r % 2)

    def wait_gather(s):
        pltpu.make_async_copy(xbuf.at[s], xbuf.at[s], sem_in.at[s]).wait()

    def wait_scatter():
        pltpu.make_async_copy(ybuf, ybuf, sem_out.at[0]).wait()

    @pl.when(i == 0)
    def _():
        for r in range(tmb):
            gather_row(src_ref, 0, r)
        ybuf[...] = jnp.zeros_like(ybuf)
        pltpu.make_async_copy(ybuf, y_hbm.at[pl.ds(y_hbm.shape[0] - tmb, tmb)], sem_out.at[0]).start()

    @pl.when(i == nv)
    def _():
        wait_gather(slot)
        wait_scatter()

    @pl.when(i < nv)
    def _():
        changed = jnp.logical_or(i == 0, be_ref[i] != be_ref[jnp.maximum(i - 1, 0)])

        @pl.when(changed)
        def _():
            wgb[...] = wg_ref[0].astype(BF16)
            wub[...] = wu_ref[0].astype(BF16)
            wdb[...] = wd_ref[0].astype(BF16)

        wait_gather(slot)
        x = _load_row_tiles(xbuf.at[slot]).astype(BF16)
        pieces = 2 * (de // EXPERT_COLS)
        per_piece = tmb // pieces
        issued = 0
        acts = []
        for c0 in range(0, de, EXPERT_COLS):
            cs = slice(c0, c0 + EXPERT_COLS)
            gate = jnp.minimum(_mm(x, wgb[:, cs]) + bg_ref[0, :, cs], SWIGLU_LIMIT)
            for r in range(issued, issued + per_piece):
                gather_row(nxt_ref, 1 - slot, r)
            issued += per_piece
            up = jnp.clip(_mm(x, wub[:, cs]) + bu_ref[0, :, cs], -SWIGLU_LIMIT, SWIGLU_LIMIT)
            for r in range(issued, issued + per_piece):
                gather_row(nxt_ref, 1 - slot, r)
            issued += per_piece
            acts.append(((up + 1.0) * gate * _sigmoid(SWIGLU_ALPHA * gate)).astype(BF16))
        act = jnp.concatenate(acts, axis=1)
        y = _mm(act, wdb[...]) + bd_ref[0]
        wait_scatter()
        _store_row_tiles(ybuf, y)
        for r in range(tmb):
            pltpu.make_async_copy(ybuf.at[pl.ds(r, 1)], y_hbm.at[pl.ds(dst_ref[0, 0, r], 1)],
                                  sem_out.at[0]).start(priority=r % 2)


def _expert_call(block_e, n_valid, slot_src, slot_dst, h2, w_gate, w_up, w_down, b_gate, b_up, b_down, n_rows_out):
    nb = block_e.shape[0]
    tmb = EXPERT_BLOCK
    d, de = w_gate.shape[1:]
    block_e = jnp.concatenate([block_e, block_e[-1:]])
    src3 = jnp.concatenate([slot_src, jnp.zeros((tmb,), I32)]).reshape(nb + 1, 1, tmb)
    dst3 = jnp.concatenate([slot_dst, slot_dst[-tmb:]]).reshape(nb + 1, 1, tmb)
    smem_blk = lambda m: pl.BlockSpec((1, 1, tmb), m, memory_space=pltpu.SMEM)
    wspec = lambda s: pl.BlockSpec((1,) + s, lambda i, be, nv: (be[i], 0, 0))
    grid_spec = pltpu.PrefetchScalarGridSpec(
        num_scalar_prefetch=2,
        grid=(nb + 1,),
        in_specs=[smem_blk(lambda i, be, nv: (i, 0, 0)),
                  smem_blk(lambda i, be, nv: (jnp.minimum(i + 1, nb), 0, 0)),
                  smem_blk(lambda i, be, nv: (i, 0, 0)),
                  pl.BlockSpec(memory_space=pl.ANY),
                  wspec((d, de)), wspec((d, de)), wspec((de, d)),
                  wspec((1, de)), wspec((1, de)), wspec((1, d))],
        out_specs=pl.BlockSpec(memory_space=pl.ANY),
        scratch_shapes=[pltpu.VMEM((2, tmb) + h2.shape[1:], F32), pltpu.VMEM((tmb,) + h2.shape[1:], F32),
                        pltpu.VMEM((d, de), BF16), pltpu.VMEM((d, de), BF16), pltpu.VMEM((de, d), BF16),
                        pltpu.SemaphoreType.DMA((2,)), pltpu.SemaphoreType.DMA((1,))],
    )
    ne = w_gate.shape[0]
    return pl.pallas_call(
        _expert_kernel,
        grid_spec=grid_spec,
        out_shape=jax.ShapeDtypeStruct((n_rows_out,) + h2.shape[1:], F32),
        compiler_params=_cparams(("arbitrary",), 56),
        name="expert",
    )(block_e, n_valid, src3, src3, dst3, h2, w_gate, w_up, w_down,
      b_gate.reshape(ne, 1, de), b_up.reshape(ne, 1, de), b_down.reshape(ne, 1, d))


def _combine_kernel(y0, y1, y2, y3, w_ref, x1_ref, g2_ref, fg_ref, o_ref):
    w = w_ref[...]
    ys = [_load_row_tiles(y) for y in (y0, y1, y2, y3)]
    moe = (w[:, 0:1] * ys[0] + w[:, 1:2] * ys[1]) + (w[:, 2:3] * ys[2] + w[:, 3:4] * ys[3])
    o_ref[...] = _rmsnorm(x1_ref[...] + g2_ref[0] * moe, fg_ref[...])


def _combine_call(y4, top_w, x1, g2, fg, tokens_per_batch, tm):
    n, d = x1.shape
    per = tokens_per_batch // tm
    nt = n // tm
    yspec = lambda k: pl.BlockSpec((tm,) + y4.shape[1:], lambda i: (k * nt + i, 0, 0))
    return pl.pallas_call(
        _combine_kernel,
        grid=(nt,),
        in_specs=[yspec(0), yspec(1), yspec(2), yspec(3),
                  pl.BlockSpec((tm, LANES), lambda i: (i, 0)),
                  pl.BlockSpec((tm, d), lambda i: (i, 0)),
                  pl.BlockSpec((1, 1, d), lambda i: (i // per, 0, 0)),
                  pl.BlockSpec((1, d), lambda i: (0, 0))],
        out_specs=pl.BlockSpec((tm, d), lambda i: (i, 0)),
        out_shape=jax.ShapeDtypeStruct((n, d), F32),
        compiler_params=_cparams(("parallel",), 48),
        name="combine",
    )(y4, y4, y4, y4, top_w, x1, g2, fg)


def _routing_tables(top_idx, rank, counts, n):
    tmb = EXPERT_BLOCK
    counts = counts.astype(I32)
    padded = (counts + tmb - 1) // tmb * tmb
    pad_end = jnp.cumsum(padded)
    pad_start = pad_end - padded
    n_blocks = -(-(n * TOP_K + N_EXPERTS * (tmb - 1)) // tmb)
    n_slots = n_blocks * tmb
    onehot = top_idx[:, :, None] == jnp.arange(N_EXPERTS, dtype=I32)[None, None, :]
    dest = jnp.sum(jnp.where(onehot, pad_start[None, None, :], 0), axis=-1) + rank
    flat = jnp.arange(TOP_K, dtype=I32)[None, :] * n + jnp.arange(n, dtype=I32)[:, None]
    slot_flat = jnp.full((n_slots,), -1, I32).at[dest.reshape(-1)].set(
        flat.reshape(-1), unique_indices=True, mode='promise_in_bounds')
    used = slot_flat >= 0
    slot_src = jnp.where(used, slot_flat % n, 0)
    slot_dst = jnp.where(used, slot_flat, n * TOP_K + jnp.arange(n_slots, dtype=I32) % tmb)
    block_start = jnp.arange(n_blocks, dtype=I32) * tmb
    block_e = jnp.minimum(jnp.sum((pad_end[None, :] <= block_start[:, None]).astype(I32), axis=1), N_EXPERTS - 1)
    n_valid = (pad_end[-1:] // tmb).astype(I32)
    return block_e, n_valid, slot_src, slot_dst


def _gdn_branch(xtok, sh, sc, norm_g, w_cols, widths, dtypes, conv_w, par, tokens_per_batch, grid_w, use_rows,
                tm_proj, tm_conv, s0, prec):
    b = xtok.shape[0] // tokens_per_batch
    outs = _inproj_call(xtok, sh, sc, norm_g, w_cols, widths, dtypes, tokens_per_batch, tm_proj)
    qkv, gates = outs[0], outs[-1]
    q, k, v, go = _conv_call(qkv.reshape(b, tokens_per_batch, QKV_DIM), conv_w,
                             gates.reshape(b, tokens_per_batch, LANES), par, grid_w, use_rows, tm_conv)
    o_f, o_b, s_fin = _gdn_call(q, k, v, go, _gdn_decay_rows(go), s0, prec)
    return outs, o_f, o_b, s_fin


def kernel(x, c, ctx, c_ctx, w_mod, b_mod, norm1_g, norm2_g, w_in, conv_w, a_log, dt_bias, gdn_norm_g,
           w_fourier_out, w_gdn_out, w_merge_out, w_router, b_router, w_gate, b_gate, w_up, b_up,
           w_down, b_down, final_norm_g):
    b, l, d = x.shape
    n = b * l
    n_ctx = ctx.shape[1]
    assert w_mod.shape[0] == 1 and l == GRID_W * GRID_W and d == V_DIM
    prec = None

    c8 = jnp.concatenate([c, c_ctx[None, :], jnp.zeros((8 - b - 1, d), F32)], axis=0)
    mod = _mod_call(c8, w_mod[0], b_mod[0])
    sh1, sc1, g1, sh2, sc2, g2 = [mod[:b, j * d:(j + 1) * d].reshape(b, 1, d) for j in range(6)]
    csh1 = jnp.broadcast_to(mod[b:b + 1, 0:d].reshape(1, 1, d), (b, 1, d))
    csc1 = jnp.broadcast_to(mod[b:b + 1, d:2 * d].reshape(1, 1, d), (b, 1, d))

    wi = w_in[0]
    off_gate = QKV_DIM
    off_z = off_gate + 4 * NV_HEADS
    off_f = off_z + V_DIM
    off_ga = off_f + F_DIM
    gate_cols = jnp.pad(wi[:, off_gate:off_z], ((0, 0), (0, LANES - 4 * NV_HEADS)))
    w_lat = jnp.concatenate([wi[:, :QKV_DIM], wi[:, off_z:off_f], wi[:, off_f:off_ga], wi[:, off_ga:], gate_cols],
                            axis=1).astype(BF16)
    w_ctx = jnp.concatenate([wi[:, :QKV_DIM], gate_cols], axis=1).astype(BF16)
    par = jnp.pad(jnp.stack([a_log[0].reshape(-1), dt_bias[0].reshape(-1)]),
                  ((0, 6), (GATE_LANE0, LANES - 2 * GATE_LANE0)))
    n1 = norm1_g[0].reshape(1, d)
    cw = conv_w[0].reshape(9, QKV_DIM)

    zero_state = jnp.zeros((b, 2, NV_HEADS, HEAD_DIM, HEAD_DIM), F32)
    _, _, _, s_ctx = _gdn_branch(ctx.reshape(b * n_ctx, d), csh1, csc1, n1, w_ctx, (QKV_DIM, LANES), (BF16, F32),
                                 cw, par, n_ctx, n_ctx, False, n_ctx, n_ctx, zero_state, prec)

    x2 = x.reshape(n, d)
    outs, o_f, o_b, _ = _gdn_branch(x2, sh1, sc1, n1, w_lat, (QKV_DIM, V_DIM, F_DIM, 2 * d, LANES),
                                    (BF16, BF16, F32, BF16, F32), cw, par, l, GRID_W, True, 512, 512, s_ctx, prec)
    _, z, f, gab, _ = outs
    fmix = _fnet_call(f.reshape(b, l, F_DIM)).reshape(n, F_DIM)

    wr = jnp.pad(w_router[0], ((0, 0), (0, LANES - N_EXPERTS)))
    br = jnp.pad(b_router[0], (0, LANES - N_EXPERTS), constant_values=NEG_BIG).reshape(1, LANES)
    x1, h2, logits = _merge_call(
        o_f.reshape(n, V_DIM), o_b.reshape(n, V_DIM), z, fmix, gab, x2, g1, sh2, sc2,
        gdn_norm_g[0].reshape(1, HEAD_DIM), norm2_g[0].reshape(1, d),
        w_gdn_out[0].astype(BF16), w_fourier_out[0].astype(BF16), w_merge_out[0].astype(BF16), wr, br, l, 256)

    top_idx, top_w, rank, counts = _route_call(logits, 512)
    block_e, n_valid, slot_src, slot_dst = _routing_tables(top_idx[:, :TOP_K], rank[:, :TOP_K],
                                                           counts[0, :N_EXPERTS], n)
    y4 = _expert_call(block_e, n_valid, slot_src, slot_dst, h2, w_gate[0], w_up[0], w_down[0],
                      b_gate[0], b_up[0], b_down[0], n * TOP_K + EXPERT_BLOCK)
    out = _combine_call(y4, top_w, x1, g2, final_norm_g.reshape(1, d), l, 256)
    return out.reshape(b, l, d)
```

```python
import functools
import math

import jax
import jax.numpy as jnp
import numpy as np
from jax import lax
from jax.experimental import pallas as pl
from jax.experimental.pallas import tpu as pltpu

F32 = jnp.float32
BF16 = jnp.bfloat16
I32 = jnp.int32
HIGHEST = lax.Precision.HIGHEST

GRID_W = 64
NQK_HEADS = 4
NV_HEADS = 8
HEAD_DIM = 128
QK_DIM = NQK_HEADS * HEAD_DIM
V_DIM = NV_HEADS * HEAD_DIM
QKV_DIM = 2 * QK_DIM + V_DIM
F_GROUPS = 4
F_DIM = F_GROUPS * HEAD_DIM
N_EXPERTS = 32
TOP_K = 4
SWIGLU_ALPHA = 1.702
SWIGLU_LIMIT = 7.0
EPS = 1e-6

LANES = 128
SUBLANES = 8
GATE_LANE0 = 16
GDN_CHUNK = 128
EXPERT_BLOCK = 256
NEG_BIG = -1e30
MIB = 2 ** 20


def _cparams(sem, vmem_mib):
    return pltpu.CompilerParams(dimension_semantics=sem, vmem_limit_bytes=vmem_mib * MIB)


def _mm(a, b, prec=None, dims=(((1,), (0,)), ((), ()))):
    if prec is None:
        return lax.dot_general(a.astype(BF16), b.astype(BF16), dims, preferred_element_type=F32)
    return lax.dot_general(a.astype(F32), b.astype(F32), dims, precision=prec, preferred_element_type=F32)


def _sigmoid(x):
    return 1.0 / (1.0 + jnp.exp(-x))


def _rmsnorm(x, g):
    return x * lax.rsqrt(jnp.mean(x * x, axis=-1, keepdims=True) + EPS) * g


def _mod_kernel(c_ref, w_ref, b_ref, o_ref):
    c = c_ref[...]
    o_ref[...] = _mm(c * _sigmoid(c), w_ref[...], HIGHEST) + b_ref[...]


def _mod_call(c8, w_mod, b_mod):
    d, n = w_mod.shape
    tn = 1536
    return pl.pallas_call(
        _mod_kernel,
        grid=(n // tn,),
        in_specs=[pl.BlockSpec((8, d), lambda j: (0, 0)),
                  pl.BlockSpec((d, tn), lambda j: (0, j)),
                  pl.BlockSpec((1, tn), lambda j: (0, j))],
        out_specs=pl.BlockSpec((8, tn), lambda j: (0, j)),
        out_shape=jax.ShapeDtypeStruct((8, n), F32),
        compiler_params=_cparams(("parallel",), 32),
        name="mod",
    )(c8, w_mod, b_mod.reshape(1, n))


def _inproj_kernel(x_ref, sh_ref, sc_ref, g_ref, w_ref, *o_refs, widths):
    u = (_rmsnorm(x_ref[...], g_ref[...]) * (1.0 + sc_ref[0]) + sh_ref[0]).astype(BF16)
    start = 0
    for o_ref, width in zip(o_refs, widths):
        step = min(width, 512)
        for c0 in range(0, width, step):
            o_ref[:, c0:c0 + step] = _mm(u, w_ref[:, start + c0:start + c0 + step]).astype(o_ref.dtype)
        start += width


def _inproj_call(x2, sh, sc, g, w, widths, dtypes, tokens_per_batch, tm):
    n, d = x2.shape
    per = tokens_per_batch // tm
    out_shape = [jax.ShapeDtypeStruct((n, wd), dt) for wd, dt in zip(widths, dtypes)]
    return pl.pallas_call(
        functools.partial(_inproj_kernel, widths=widths),
        grid=(n // tm,),
        in_specs=[pl.BlockSpec((tm, d), lambda i: (i, 0)),
                  pl.BlockSpec((1, 1, d), lambda i: (i // per, 0, 0)),
                  pl.BlockSpec((1, 1, d), lambda i: (i // per, 0, 0)),
                  pl.BlockSpec((1, d), lambda i: (0, 0)),
                  pl.BlockSpec(w.shape, lambda i: (0, 0))],
        out_specs=[pl.BlockSpec((tm, wd), lambda i: (i, 0)) for wd in widths],
        out_shape=out_shape,
        compiler_params=_cparams(("parallel",), 56),
        name="inproj",
    )(x2, sh, sc, g, w)


def _conv_kernel(*refs, grid_w, use_rows, tm, cw):
    if use_rows:
        prev_ref, main_ref, next_ref, w_ref, gates_ref, par_ref, q_ref, k_ref, v_ref, go_ref = refs
    else:
        main_ref, w_ref, gates_ref, par_ref, q_ref, k_ref, v_ref, go_ref = refs
    r = pl.program_id(1)
    nr = pl.num_programs(1)
    t = lax.broadcasted_iota(I32, (tm, 1), 0)
    col = jnp.bitwise_and(t, grid_w - 1)
    m_left = (col != 0).astype(F32)
    m_right = (col != grid_w - 1).astype(F32)
    has_prev = (r > 0).astype(F32)
    has_next = (r < nr - 1).astype(F32)
    for c0 in range(0, QKV_DIM, cw):
        xm = main_ref[0, :, c0:c0 + cw].astype(F32)
        if use_rows:
            xp = prev_ref[0, :, c0:c0 + cw].astype(F32) * has_prev
            xn = next_ref[0, :, c0:c0 + cw].astype(F32) * has_next
            up = jnp.concatenate([xp, xm[:tm - grid_w]], axis=0)
            dn = jnp.concatenate([xm[grid_w:], xn], axis=0)

        def colsum(kc):
            y = xm * w_ref[3 + kc:4 + kc, c0:c0 + cw]
            if use_rows:
                y = y + up * w_ref[kc:kc + 1, c0:c0 + cw] + dn * w_ref[6 + kc:7 + kc, c0:c0 + cw]
            return y

        acc = (colsum(1) + pltpu.roll(colsum(0), 1, axis=0) * m_left
               + pltpu.roll(colsum(2), tm - 1, axis=0) * m_right)
        s = acc * _sigmoid(acc)
        for h0 in range(0, cw, HEAD_DIM):
            c = c0 + h0
            seg = s[:, h0:h0 + HEAD_DIM]
            if c < 2 * QK_DIM:
                seg = seg * lax.rsqrt(jnp.sum(seg * seg, axis=-1, keepdims=True) + EPS)
            if c < QK_DIM:
                q_ref[0, :, c:c + HEAD_DIM] = (seg * HEAD_DIM ** -0.5).astype(q_ref.dtype)
            elif c < 2 * QK_DIM:
                k_ref[0, :, c - QK_DIM:c - QK_DIM + HEAD_DIM] = seg.astype(k_ref.dtype)
            else:
                v_ref[0, :, c - 2 * QK_DIM:c - 2 * QK_DIM + HEAD_DIM] = seg.astype(v_ref.dtype)
    g = gates_ref[0]
    a = g + par_ref[1:2, :]
    softplus = jnp.maximum(a, 0.0) + jnp.log1p(jnp.exp(-jnp.abs(a)))
    log_g = -jnp.exp(par_ref[0:1, :]) * softplus
    lane = lax.broadcasted_iota(I32, g.shape, 1)
    go_ref[0] = jnp.where(lane < GATE_LANE0, _sigmoid(g), log_g)


def _conv_call(qkv, conv_w, gates, par, grid_w, use_rows, tm):
    b, t, c = qkv.shape
    kern = functools.partial(_conv_kernel, grid_w=grid_w, use_rows=use_rows, tm=tm, cw=512)
    per = tm // grid_w
    nrow = t // grid_w
    in_specs = []
    args = []
    if use_rows:
        in_specs.append(pl.BlockSpec((1, grid_w, c), lambda i, r: (i, jnp.maximum(r * per - 1, 0), 0)))
        args.append(qkv)
    in_specs.append(pl.BlockSpec((1, tm, c), lambda i, r: (i, r, 0)))
    args.append(qkv)
    if use_rows:
        in_specs.append(pl.BlockSpec((1, grid_w, c), lambda i, r: (i, jnp.minimum((r + 1) * per, nrow - 1), 0)))
        args.append(qkv)
    in_specs += [pl.BlockSpec(conv_w.shape, lambda i, r: (0, 0)),
                 pl.BlockSpec((1, tm, LANES), lambda i, r: (i, r, 0)),
                 pl.BlockSpec(par.shape, lambda i, r: (0, 0))]
    args += [conv_w, gates, par]
    out_shape = [jax.ShapeDtypeStruct((b, t, QK_DIM), BF16), jax.ShapeDtypeStruct((b, t, QK_DIM), BF16),
                 jax.ShapeDtypeStruct((b, t, V_DIM), BF16), jax.ShapeDtypeStruct((b, t, LANES), F32)]
    out_specs = [pl.BlockSpec((1, tm, QK_DIM), lambda i, r: (i, r, 0)),
                 pl.BlockSpec((1, tm, QK_DIM), lambda i, r: (i, r, 0)),
                 pl.BlockSpec((1, tm, V_DIM), lambda i, r: (i, r, 0)),
                 pl.BlockSpec((1, tm, LANES), lambda i, r: (i, r, 0))]
    return pl.pallas_call(
        kern, grid=(b, t // tm), in_specs=in_specs, out_specs=out_specs, out_shape=out_shape,
        compiler_params=_cparams(("parallel", "parallel"), 48),
        name="conv_rows" if use_rows else "conv_seq",
    )(*args)


def _gdn_kernel(qf, kf, vf, gf, rf, qb, kb, vb, gb, rb, s0_ref, of, ob, sfin_ref, s_ref, *, prec):
    i = pl.program_id(1)
    nc = pl.num_programs(1)

    @pl.when(i == 0)
    def _():
        s_ref[...] = s0_ref[0]

    c = qf.shape[1]
    per = NV_HEADS // NQK_HEADS
    row = lax.broadcasted_iota(I32, (c, c), 0)
    colj = lax.broadcasted_iota(I32, (c, c), 1)
    eye = jnp.where(row == colj, 1.0, 0.0)
    nt_dims = (((1,), (1,)), ((), ()))
    tn_dims = (((0,), (0,)), ((), ()))

    seqs = []
    for d, (q_r, k_r, v_r, g_r, r_r, o_r) in enumerate(((qf, kf, vf, gf, rf, of), (qb, kb, vb, gb, rb, ob))):
        rev = d == 1
        incl = (colj >= row) if rev else (colj <= row)
        strict = (colj > row) if rev else (colj < row)
        gates = g_r[0]
        tri_c = jnp.where(incl, 1.0, 0.0)
        gcm = _mm(tri_c, gates, HIGHEST)
        gcr = _mm(r_r[0, 0], tri_c, HIGHEST, dims=nt_dims)
        for hq in range(NQK_HEADS):
            q = q_r[0, :, hq * HEAD_DIM:(hq + 1) * HEAD_DIM]
            k = k_r[0, :, hq * HEAD_DIM:(hq + 1) * HEAD_DIM]
            kq = lax.dot_general(jnp.concatenate([q, k], axis=0), k, nt_dims, preferred_element_type=F32)
            for j in range(per):
                h = hq * per + j
                idx = d * NV_HEADS + h
                gc_c = gcm[:, GATE_LANE0 + idx:GATE_LANE0 + idx + 1]
                seqs.append(dict(d=d, h=h, o_r=o_r, v_r=v_r, q=q, k=k, qk=kq[:c], kk=kq[c:], incl=incl, strict=strict,
                                 beta=gates[:, idx:idx + 1], gc_c=gc_c, gc_r=gcr[idx:idx + 1, :],
                                 ge=gc_c[0:1] if rev else gc_c[c - 1:c]))

    def same_block(m):
        sh = int(math.log2(m))
        return jnp.right_shift(row, sh) == jnp.right_shift(colj, sh)

    for s in seqs:
        s['decay'] = jnp.where(s['incl'], jnp.exp(jnp.where(s['incl'], s['gc_c'] - s['gc_r'], 0.0)), 0.0)
        s['a'] = jnp.where(s['strict'], s['beta'] * s['kk'] * s['decay'], 0.0)
        s['t'] = eye - jnp.where(same_block(2), s['a'], 0.0)
    m = 4
    while m <= c:
        between = jnp.logical_and(same_block(m), jnp.logical_not(same_block(m // 2)))
        for s in seqs:
            s['te'] = _mm(s['t'], jnp.where(between, s['a'], 0.0), prec)
        for s in seqs:
            s['t'] = s['t'] - _mm(s['te'], s['t'], prec)
        m *= 2
    for s in seqs:
        h = s['h']
        egc = jnp.exp(s['gc_c'])
        kf32 = s['k'].astype(F32)
        v = s['v_r'][0, :, h * HEAD_DIM:(h + 1) * HEAD_DIM].astype(F32)
        rhs = jnp.concatenate([s['beta'] * v, (s['beta'] * egc) * kf32], axis=1)
        s['sol'] = _mm(s['t'], rhs, prec)
        s['q_dec'] = s['q'].astype(F32) * egc
        s['k_dec'] = kf32 * jnp.exp(s['ge'] - s['gc_c'])
    for s in seqs:
        s['ws'] = _mm(jnp.concatenate([s['sol'][:, HEAD_DIM:], s['q_dec']], axis=0), s_ref[s['d'], s['h']])
    for s in seqs:
        s['u'] = s['sol'][:, :HEAD_DIM] - s['ws'][:c]
        s_ref[s['d'], s['h']] = (jnp.exp(s['ge']) * s_ref[s['d'], s['h']]
                                 + _mm(s['k_dec'], s['u'], dims=tn_dims))
    for s in seqs:
        h = s['h']
        o = s['ws'][c:] + _mm(s['qk'] * s['decay'], s['u'])
        s['o_r'][0, :, h * HEAD_DIM:(h + 1) * HEAD_DIM] = o.astype(s['o_r'].dtype)

    @pl.when(i == nc - 1)
    def _():
        sfin_ref[0] = s_ref[...]


def _gdn_call(q, k, v, go, rows, s0, prec):
    b, t, _ = q.shape
    c = GDN_CHUNK
    nc = t // c
    fwd = lambda i, n: (i, n, 0)
    bwd = lambda i, n: (i, nc - 1 - n, 0)
    rfwd = lambda i, n: (i, n, 0, 0)
    rbwd = lambda i, n: (i, nc - 1 - n, 0, 0)
    state_spec = pl.BlockSpec((1,) + s0.shape[1:], lambda i, n: (i, 0, 0, 0, 0))

    def specs(m3, mr):
        return [pl.BlockSpec((1, c, QK_DIM), m3), pl.BlockSpec((1, c, QK_DIM), m3),
                pl.BlockSpec((1, c, V_DIM), m3), pl.BlockSpec((1, c, LANES), m3),
                pl.BlockSpec((1, 1, 2 * NV_HEADS, c), mr)]

    return pl.pallas_call(
        functools.partial(_gdn_kernel, prec=prec),
        grid=(b, nc),
        in_specs=specs(fwd, rfwd) + specs(bwd, rbwd) + [state_spec],
        out_specs=[pl.BlockSpec((1, c, V_DIM), fwd), pl.BlockSpec((1, c, V_DIM), bwd), state_spec],
        out_shape=[jax.ShapeDtypeStruct((b, t, V_DIM), BF16), jax.ShapeDtypeStruct((b, t, V_DIM), BF16),
                   jax.ShapeDtypeStruct(s0.shape, F32)],
        scratch_shapes=[pltpu.VMEM(s0.shape[1:], F32)],
        compiler_params=_cparams(("parallel", "arbitrary"), 48),
        name="gdn",
    )(q, k, v, go, rows, q, k, v, go, rows, s0)


def _gdn_decay_rows(go):
    b, t, _ = go.shape
    c = GDN_CHUNK
    lg = go[..., GATE_LANE0:GATE_LANE0 + 2 * NV_HEADS]
    return jnp.transpose(lg.reshape(b, t // c, c, 2 * NV_HEADS), (0, 1, 3, 2))


def _fnet1_kernel(x_ref, f_ref, ar_ref, ai_ref):
    n = x_ref.shape[1]
    for j in range(x_ref.shape[2]):
        a = _mm(f_ref[...], x_ref[0, :, j, :])
        ar_ref[0, j] = a[:n]
        ai_ref[0, j] = a[n:]


def _fnet2_kernel(ar_ref, ai_ref, g_ref, wc_ref, o_ref, *, scale):
    n = ar_ref.shape[1]
    zs = []
    for m in range(ar_ref.shape[2]):
        a2 = jnp.concatenate([ar_ref[0, :, m, :], ai_ref[0, :, m, :]], axis=0)
        zs.append(_mm(g_ref[m], a2))
    for m, z in enumerate(zs):
        y = _mm(jnp.concatenate([z[:n], z[n:]], axis=1), wc_ref[...])
        o_ref[0, :, m, :] = y * scale


def _fnet_tables(n, groups):
    a = np.arange(n)
    ang1 = 2.0 * np.pi * np.outer(a, a) / n
    f1 = np.concatenate([np.cos(ang1), -np.sin(ang1)], axis=0)
    m = a[:, None] + n * a[None, :]
    ang2 = 2.0 * np.pi * ((m[:, :, None] * a[None, None, :]) % (n * n)) / (n * n)
    gc, gs = np.cos(ang2), np.sin(ang2)
    g2 = np.concatenate([np.concatenate([gc, gs], axis=2), np.concatenate([-gs, gc], axis=2)], axis=1)
    angc = 2.0 * np.pi * np.outer(np.arange(HEAD_DIM), np.arange(HEAD_DIM)) / HEAD_DIM
    eye = np.eye(groups)
    wc = np.concatenate([np.kron(eye, np.cos(angc)), np.kron(eye, np.sin(angc))], axis=0)
    f = lambda x: jnp.asarray(x, F32).astype(BF16)
    return f(f1), f(g2), f(wc)


def _fnet_call(f):
    b, l, c = f.shape
    n = GRID_W
    assert l == n * n
    f1, g2, wc = _fnet_tables(n, c // HEAD_DIM)
    cols = SUBLANES
    ar, ai = pl.pallas_call(
        _fnet1_kernel,
        grid=(b, n // cols),
        in_specs=[pl.BlockSpec((1, n, cols, c), lambda i, j: (i, 0, j, 0)),
                  pl.BlockSpec((2 * n, n), lambda i, j: (0, 0))],
        out_specs=[pl.BlockSpec((1, cols, n, c), lambda i, j: (i, j, 0, 0))] * 2,
        out_shape=[jax.ShapeDtypeStruct((b, n, n, c), F32)] * 2,
        compiler_params=_cparams(("parallel", "parallel"), 32),
        name="fnet1",
    )(f.reshape(b, n, n, c), f1)
    out = pl.pallas_call(
        functools.partial(_fnet2_kernel, scale=1.0 / math.sqrt(l * HEAD_DIM)),
        grid=(b, n // cols),
        in_specs=[pl.BlockSpec((1, n, cols, c), lambda i, j: (i, 0, j, 0)),
                  pl.BlockSpec((1, n, cols, c), lambda i, j: (i, 0, j, 0)),
                  pl.BlockSpec((cols, 2 * n, 2 * n), lambda i, j: (j, 0, 0)),
                  pl.BlockSpec((2 * c, c), lambda i, j: (0, 0))],
        out_specs=pl.BlockSpec((1, n, cols, c), lambda i, j: (i, 0, j, 0)),
        out_shape=jax.ShapeDtypeStruct((b, n, n, c), F32),
        compiler_params=_cparams(("parallel", "parallel"), 32),
        name="fnet2",
    )(ar, ai, g2, wc)
    return out.reshape(b, l, c)


def _merge_kernel(of_ref, ob_ref, z_ref, fm_ref, gab_ref, x_ref, g1_ref, sh2_ref, sc2_ref, gn_ref, n2_ref,
                  wg_ref, wf_ref, wm_ref, wr_ref, br_ref, x1_ref, h2_ref, lg_ref):
    d = x_ref.shape[1]
    o = of_ref[...].astype(F32) + ob_ref[...].astype(F32)
    z = z_ref[...].astype(F32)
    parts = []
    for h0 in range(0, V_DIM, HEAD_DIM):
        oh = o[:, h0:h0 + HEAD_DIM]
        parts.append(oh * lax.rsqrt(jnp.mean(oh * oh, axis=-1, keepdims=True) + EPS) * gn_ref[...])
    yb_in = jnp.concatenate(parts, axis=1) * (z * _sigmoid(z))
    yb = _mm(yb_in, wg_ref[...])
    ya = _mm(fm_ref[...], wf_ref[...])
    ga = _sigmoid(gab_ref[:, :d].astype(F32))
    gb = _sigmoid(gab_ref[:, d:].astype(F32))
    mm = _mm(ga * ya + gb * yb, wm_ref[...])
    x1 = x_ref[...] + g1_ref[0] * mm
    x1_ref[...] = x1
    h2 = _rmsnorm(x1, n2_ref[...]) * (1.0 + sc2_ref[0]) + sh2_ref[0]
    h2_ref[...] = h2
    h_hi = h2.astype(BF16)
    h_lo = (h2 - h_hi.astype(F32)).astype(BF16)
    part = _mm(h_hi, wr_ref[...])
    lg_ref[...] = (part[:, :LANES] + part[:, LANES:]) + _mm(h_lo, wr_ref[:, :LANES]) + br_ref[...]


def _merge_call(of, ob, z, fm, gab, x2, g1, sh2, sc2, gn, n2, wg, wf, wm, wr, br, tokens_per_batch, tm):
    n, d = x2.shape
    per = tokens_per_batch // tm
    tok = lambda wd: pl.BlockSpec((tm, wd), lambda i: (i, 0))
    vec = pl.BlockSpec((1, 1, d), lambda i: (i // per, 0, 0))
    full = lambda a: pl.BlockSpec(a.shape, lambda i: (0,) * a.ndim)
    return pl.pallas_call(
        _merge_kernel,
        grid=(n // tm,),
        in_specs=[tok(V_DIM), tok(V_DIM), tok(V_DIM), tok(F_DIM), tok(2 * d), tok(d), vec, vec, vec,
                  full(gn), full(n2), full(wg), full(wf), full(wm), full(wr), full(br)],
        out_specs=[tok(d), tok(d), tok(LANES)],
        out_shape=[jax.ShapeDtypeStruct((n, d), F32), jax.ShapeDtypeStruct((n, d), F32),
                   jax.ShapeDtypeStruct((n, LANES), F32)],
        compiler_params=_cparams(("parallel",), 56),
        name="merge",
    )(of, ob, z, fm, gab, x2, g1, sh2, sc2, gn, n2, wg, wf, wm, wr, br)


def _route_kernel(lg_ref, idx_ref, w_ref, rank_ref, cnt_ref, run_ref):
    i = pl.program_id(0)

    @pl.when(i == 0)
    def _():
        run_ref[...] = jnp.zeros_like(run_ref)

    l = lg_ref[...]
    tm = l.shape[0]
    lane = lax.broadcasted_iota(I32, l.shape, 1)
    vals, idxs = [], []
    for _ in range(TOP_K):
        m = jnp.max(l, axis=-1, keepdims=True)
        idx = jnp.min(jnp.where(l == m, lane, LANES), axis=-1, keepdims=True)
        vals.append(m)
        idxs.append(idx)
        l = jnp.where(lane == idx, NEG_BIG * 2.0, l)
    es = [jnp.exp(v - vals[0]) for v in vals]
    inv = 1.0 / (es[0] + es[1] + es[2] + es[3])
    picked = jnp.zeros(l.shape, F32)
    for idx in idxs:
        picked = picked + (lane == idx).astype(F32)
    r = lax.broadcasted_iota(I32, (tm, tm), 0)
    cidx = lax.broadcasted_iota(I32, (tm, tm), 1)
    before = _mm(jnp.where(cidx < r, 1.0, 0.0), picked) + run_ref[...]
    idx_out = jnp.zeros(l.shape, I32)
    w_out = jnp.zeros(l.shape, F32)
    rank_out = jnp.zeros(l.shape, F32)
    for k in range(TOP_K):
        rk = jnp.sum(jnp.where(lane == idxs[k], before, 0.0), axis=-1, keepdims=True)
        idx_out = jnp.where(lane == k, idxs[k], idx_out)
        w_out = jnp.where(lane == k, es[k] * inv, w_out)
        rank_out = jnp.where(lane == k, rk, rank_out)
    idx_ref[...] = idx_out
    w_ref[...] = w_out
    rank_ref[...] = rank_out.astype(I32)
    run_ref[...] = run_ref[...] + jnp.sum(picked, axis=0, keepdims=True)
    cnt_ref[...] = run_ref[...]


def _route_call(logits, tm):
    n = logits.shape[0]
    tok = pl.BlockSpec((tm, LANES), lambda i: (i, 0))
    return pl.pallas_call(
        _route_kernel,
        grid=(n // tm,),
        in_specs=[tok],
        out_specs=[tok, tok, tok, pl.BlockSpec((1, LANES), lambda i: (0, 0))],
        out_shape=[jax.ShapeDtypeStruct((n, LANES), I32), jax.ShapeDtypeStruct((n, LANES), F32),
                   jax.ShapeDtypeStruct((n, LANES), I32), jax.ShapeDtypeStruct((1, LANES), F32)],
        scratch_shapes=[pltpu.VMEM((1, LANES), F32)],
        compiler_params=_cparams(("arbitrary",), 32),
        name="route",
    )(logits)


def _expert_kernel(be_ref, nv_ref, src_ref, nxt_ref, dst_ref, h_hbm, wg_ref, wu_ref, wd_ref, bg_ref, bu_ref, bd_ref,
                   y_hbm, xbuf, ybuf, wgb, wub, wdb, sem_in, sem_out):
    i = pl.program_id(0)
    nv = nv_ref[0]
    tmb = xbuf.shape[1]
    slot = lax.rem(i, 2)

    def gather_row(idx_ref, s, r):
        pltpu.make_async_copy(h_hbm.at[pl.ds(idx_ref[0, 0, r], 1)], xbuf.at[s, pl.ds(r, 1)],
                              sem_in.at[s]).start(priority=r % 2)

    def wait_gather(s):
        pltpu.make_async_copy(xbuf.at[s], xbuf.at[s], sem_in.at[s]).wait()

    def wait_scatter():
        pltpu.make_async_copy(ybuf, ybuf, sem_out.at[0]).wait()

    @pl.when(i == 0)
    def _():
        for r in range(tmb):
            gather_row(src_ref, 0, r)
        ybuf[...] = jnp.zeros_like(ybuf)
        pltpu.make_async_copy(ybuf, y_hbm.at[pl.ds(y_hbm.shape[0] - tmb, tmb)], sem_out.at[0]).start()

    @pl.when(i == nv)
    def _():
        wait_gather(slot)
        wait_scatter()

    @pl.when(i < nv)
    def _():
        for r in range(tmb):
            gather_row(nxt_ref, 1 - slot, r)

    @pl.when(i < nv)
    def _():
        changed = jnp.logical_or(i == 0, be_ref[i] != be_ref[jnp.maximum(i - 1, 0)])

        @pl.when(changed)
        def _():
            wgb[...] = wg_ref[0].astype(BF16)
            wub[...] = wu_ref[0].astype(BF16)
            wdb[...] = wd_ref[0].astype(BF16)

        wait_gather(slot)
        x = xbuf[slot].astype(BF16)
        gate = jnp.minimum(_mm(x, wgb[...]) + bg_ref[0], SWIGLU_LIMIT)
        up = jnp.clip(_mm(x, wub[...]) + bu_ref[0], -SWIGLU_LIMIT, SWIGLU_LIMIT)
        act = (up + 1.0) * gate * _sigmoid(SWIGLU_ALPHA * gate)
        y = _mm(act, wdb[...]) + bd_ref[0]
        wait_scatter()
        ybuf[...] = y
        for r in range(tmb):
            pltpu.make_async_copy(ybuf.at[pl.ds(r, 1)], y_hbm.at[pl.ds(dst_ref[0, 0, r], 1)],
                                  sem_out.at[0]).start(priority=r % 2)


def _expert_call(block_e, n_valid, slot_src, slot_dst, h2, w_gate, w_up, w_down, b_gate, b_up, b_down, n_rows_out):
    nb = block_e.shape[0]
    tmb = EXPERT_BLOCK
    d, de = w_gate.shape[1:]
    block_e = jnp.concatenate([block_e, block_e[-1:]])
    src3 = jnp.concatenate([slot_src, jnp.zeros((tmb,), I32)]).reshape(nb + 1, 1, tmb)
    dst3 = jnp.concatenate([slot_dst, slot_dst[-tmb:]]).reshape(nb + 1, 1, tmb)
    smem_blk = lambda m: pl.BlockSpec((1, 1, tmb), m, memory_space=pltpu.SMEM)
    wspec = lambda s: pl.BlockSpec((1,) + s, lambda i, be, nv: (be[i], 0, 0))
    grid_spec = pltpu.PrefetchScalarGridSpec(
        num_scalar_prefetch=2,
        grid=(nb + 1,),
        in_specs=[smem_blk(lambda i, be, nv: (i, 0, 0)),
                  smem_blk(lambda i, be, nv: (jnp.minimum(i + 1, nb), 0, 0)),
                  smem_blk(lambda i, be, nv: (i, 0, 0)),
                  pl.BlockSpec(memory_space=pl.ANY),
                  wspec((d, de)), wspec((d, de)), wspec((de, d)),
                  wspec((1, de)), wspec((1, de)), wspec((1, d))],
        out_specs=pl.BlockSpec(memory_space=pl.ANY),
        scratch_shapes=[pltpu.VMEM((2, tmb) + h2.shape[1:], F32), pltpu.VMEM((tmb,) + h2.shape[1:], F32),
                        pltpu.VMEM((d, de), BF16), pltpu.VMEM((d, de), BF16), pltpu.VMEM((de, d), BF16),
                        pltpu.SemaphoreType.DMA((2,)), pltpu.SemaphoreType.DMA((1,))],
    )
    ne = w_gate.shape[0]
    return pl.pallas_call(
        _expert_kernel,
        grid_spec=grid_spec,
        out_shape=jax.ShapeDtypeStruct((n_rows_out,) + h2.shape[1:], F32),
        compiler_params=_cparams(("arbitrary",), 56),
        name="expert",
    )(block_e, n_valid, src3, src3, dst3, h2, w_gate, w_up, w_down,
      b_gate.reshape(ne, 1, de), b_up.reshape(ne, 1, de), b_down.reshape(ne, 1, d))


def _combine_kernel(y0, y1, y2, y3, w_ref, x1_ref, g2_ref, fg_ref, o_ref):
    w = w_ref[...]
    moe = (w[:, 0:1] * y0[...] + w[:, 1:2] * y1[...]) + (w[:, 2:3] * y2[...] + w[:, 3:4] * y3[...])
    o_ref[...] = _rmsnorm(x1_ref[...] + g2_ref[0] * moe, fg_ref[...])


def _combine_call(y4, top_w, x1, g2, fg, tokens_per_batch, tm):
    n, d = x1.shape
    per = tokens_per_batch // tm
    nt = n // tm
    yspec = lambda k: pl.BlockSpec((tm, d), lambda i: (k * nt + i, 0))
    return pl.pallas_call(
        _combine_kernel,
        grid=(nt,),
        in_specs=[yspec(0), yspec(1), yspec(2), yspec(3),
                  pl.BlockSpec((tm, LANES), lambda i: (i, 0)),
                  pl.BlockSpec((tm, d), lambda i: (i, 0)),
                  pl.BlockSpec((1, 1, d), lambda i: (i // per, 0, 0)),
                  pl.BlockSpec((1, d), lambda i: (0, 0))],
        out_specs=pl.BlockSpec((tm, d), lambda i: (i, 0)),
        out_shape=jax.ShapeDtypeStruct((n, d), F32),
        compiler_params=_cparams(("parallel",), 48),
        name="combine",
    )(y4, y4, y4, y4, top_w, x1, g2, fg)


def _routing_tables(top_idx, rank, counts, n):
    tmb = EXPERT_BLOCK
    counts = counts.astype(I32)
    padded = (counts + tmb - 1) // tmb * tmb
    pad_end = jnp.cumsum(padded)
    pad_start = pad_end - padded
    n_blocks = -(-(n * TOP_K + N_EXPERTS * (tmb - 1)) // tmb)
    n_slots = n_blocks * tmb
    onehot = top_idx[:, :, None] == jnp.arange(N_EXPERTS, dtype=I32)[None, None, :]
    dest = jnp.sum(jnp.where(onehot, pad_start[None, None, :], 0), axis=-1) + rank
    flat = jnp.arange(TOP_K, dtype=I32)[None, :] * n + jnp.arange(n, dtype=I32)[:, None]
    slot_flat = jnp.full((n_slots,), -1, I32).at[dest.reshape(-1)].set(
        flat.reshape(-1), unique_indices=True, mode='promise_in_bounds')
    used = slot_flat >= 0
    slot_src = jnp.where(used, slot_flat % n, 0)
    slot_dst = jnp.where(used, slot_flat, n * TOP_K + jnp.arange(n_slots, dtype=I32) % tmb)
    block_start = jnp.arange(n_blocks, dtype=I32) * tmb
    block_e = jnp.minimum(jnp.sum((pad_end[None, :] <= block_start[:, None]).astype(I32), axis=1), N_EXPERTS - 1)
    n_valid = (pad_end[-1:] // tmb).astype(I32)
    return block_e, n_valid, slot_src, slot_dst


def _gdn_branch(xtok, sh, sc, norm_g, w_cols, widths, dtypes, conv_w, par, tokens_per_batch, grid_w, use_rows,
                tm_proj, tm_conv, s0, prec):
    b = xtok.shape[0] // tokens_per_batch
    outs = _inproj_call(xtok, sh, sc, norm_g, w_cols, widths, dtypes, tokens_per_batch, tm_proj)
    qkv, gates = outs[0], outs[-1]
    q, k, v, go = _conv_call(qkv.reshape(b, tokens_per_batch, QKV_DIM), conv_w,
                             gates.reshape(b, tokens_per_batch, LANES), par, grid_w, use_rows, tm_conv)
    o_f, o_b, s_fin = _gdn_call(q, k, v, go, _gdn_decay_rows(go), s0, prec)
    return outs, o_f, o_b, s_fin


def kernel(x, c, ctx, c_ctx, w_mod, b_mod, norm1_g, norm2_g, w_in, conv_w, a_log, dt_bias, gdn_norm_g,
           w_fourier_out, w_gdn_out, w_merge_out, w_router, b_router, w_gate, b_gate, w_up, b_up,
           w_down, b_down, final_norm_g):
    b, l, d = x.shape
    n = b * l
    n_ctx = ctx.shape[1]
    assert w_mod.shape[0] == 1 and l == GRID_W * GRID_W and d == V_DIM
    prec = None

    c8 = jnp.concatenate([c, c_ctx[None, :], jnp.zeros((8 - b - 1, d), F32)], axis=0)
    mod = _mod_call(c8, w_mod[0], b_mod[0])
    sh1, sc1, g1, sh2, sc2, g2 = [mod[:b, j * d:(j + 1) * d].reshape(b, 1, d) for j in range(6)]
    csh1 = jnp.broadcast_to(mod[b:b + 1, 0:d].reshape(1, 1, d), (b, 1, d))
    csc1 = jnp.broadcast_to(mod[b:b + 1, d:2 * d].reshape(1, 1, d), (b, 1, d))

    wi = w_in[0]
    off_gate = QKV_DIM
    off_z = off_gate + 4 * NV_HEADS
    off_f = off_z + V_DIM
    off_ga = off_f + F_DIM
    gate_cols = jnp.pad(wi[:, off_gate:off_z], ((0, 0), (0, LANES - 4 * NV_HEADS)))
    w_lat = jnp.concatenate([wi[:, :QKV_DIM], wi[:, off_z:off_f], wi[:, off_f:off_ga], wi[:, off_ga:], gate_cols],
                            axis=1).astype(BF16)
    w_ctx = jnp.concatenate([wi[:, :QKV_DIM], gate_cols], axis=1).astype(BF16)
    par = jnp.pad(jnp.stack([a_log[0].reshape(-1), dt_bias[0].reshape(-1)]),
                  ((0, 6), (GATE_LANE0, LANES - 2 * GATE_LANE0)))
    n1 = norm1_g[0].reshape(1, d)
    cw = conv_w[0].reshape(9, QKV_DIM)

    zero_state = jnp.zeros((b, 2, NV_HEADS, HEAD_DIM, HEAD_DIM), F32)
    _, _, _, s_ctx = _gdn_branch(ctx.reshape(b * n_ctx, d), csh1, csc1, n1, w_ctx, (QKV_DIM, LANES), (BF16, F32),
                                 cw, par, n_ctx, n_ctx, False, n_ctx, n_ctx, zero_state, prec)

    x2 = x.reshape(n, d)
    outs, o_f, o_b, _ = _gdn_branch(x2, sh1, sc1, n1, w_lat, (QKV_DIM, V_DIM, F_DIM, 2 * d, LANES),
                                    (BF16, BF16, F32, BF16, F32), cw, par, l, GRID_W, True, 512, 512, s_ctx, prec)
    _, z, f, gab, _ = outs
    fmix = _fnet_call(f.reshape(b, l, F_DIM)).reshape(n, F_DIM)

    wr = jnp.pad(w_router[0], ((0, 0), (0, LANES - N_EXPERTS)))
    wr_hi = lax.reduce_precision(wr, exponent_bits=8, mantissa_bits=7)
    wr = jnp.concatenate([wr_hi, wr - wr_hi], axis=1).astype(BF16)
    br = jnp.pad(b_router[0], (0, LANES - N_EXPERTS), constant_values=NEG_BIG).reshape(1, LANES)
    x1, h2, logits = _merge_call(
        o_f.reshape(n, V_DIM), o_b.reshape(n, V_DIM), z, fmix, gab, x2, g1, sh2, sc2,
        gdn_norm_g[0].reshape(1, HEAD_DIM), norm2_g[0].reshape(1, d),
        w_gdn_out[0].astype(BF16), w_fourier_out[0].astype(BF16), w_merge_out[0].astype(BF16), wr, br, l, 256)

    top_idx, top_w, rank, counts = _route_call(logits, 512)
    block_e, n_valid, slot_src, slot_dst = _routing_tables(top_idx[:, :TOP_K], rank[:, :TOP_K],
                                                           counts[0, :N_EXPERTS], n)
    y4 = _expert_call(block_e, n_valid, slot_src, slot_dst, h2, w_gate[0], w_up[0], w_down[0],
                      b_gate[0], b_up[0], b_down[0], n * TOP_K + EXPERT_BLOCK)
    out = _combine_call(y4, top_w, x1, g2, final_norm_g.reshape(1, d), l, 256)
    return out.reshape(b, l, d)
```

```python
import functools
import math

import jax
import jax.numpy as jnp
import numpy as np
from jax import lax
from jax.experimental import pallas as pl
from jax.experimental.pallas import tpu as pltpu
from jax.experimental.pallas import tpu_sc as plsc

F32 = jnp.float32
BF16 = jnp.bfloat16
I32 = jnp.int32
HIGHEST = lax.Precision.HIGHEST

GRID_W = 64
NQK_HEADS = 4
NV_HEADS = 8
HEAD_DIM = 128
QK_DIM = NQK_HEADS * HEAD_DIM
V_DIM = NV_HEADS * HEAD_DIM
QKV_DIM = 2 * QK_DIM + V_DIM
F_GROUPS = 4
F_DIM = F_GROUPS * HEAD_DIM
N_EXPERTS = 32
TOP_K = 4
SWIGLU_ALPHA = 1.702
SWIGLU_LIMIT = 7.0
EPS = 1e-6

LANES = 128
SUBLANES = 8
GATE_LANE0 = 16
GDN_CHUNK = 128
EXPERT_BLOCK = 256
SC_CORES = 2
SC_SUBCORES = 16
SC_WORKERS = SC_CORES * SC_SUBCORES
SC_WINDOW = 32
NEG_BIG = -1e30
MIB = 2 ** 20


def _cparams(sem, vmem_mib):
    return pltpu.CompilerParams(dimension_semantics=sem, vmem_limit_bytes=vmem_mib * MIB)


def _mm(a, b, prec=None, dims=(((1,), (0,)), ((), ()))):
    if prec is None:
        return lax.dot_general(a.astype(BF16), b.astype(BF16), dims, preferred_element_type=F32)
    return lax.dot_general(a.astype(F32), b.astype(F32), dims, precision=prec, preferred_element_type=F32)


def _sigmoid(x):
    return 1.0 / (1.0 + jnp.exp(-x))


def _rmsnorm(x, g):
    return x * lax.rsqrt(jnp.mean(x * x, axis=-1, keepdims=True) + EPS) * g


def _mod_kernel(c_ref, w_ref, b_ref, o_ref):
    c = c_ref[...]
    o_ref[...] = _mm(c * _sigmoid(c), w_ref[...], HIGHEST) + b_ref[...]


def _mod_call(c8, w_mod, b_mod):
    d, n = w_mod.shape
    tn = 1536
    return pl.pallas_call(
        _mod_kernel,
        grid=(n // tn,),
        in_specs=[pl.BlockSpec((8, d), lambda j: (0, 0)),
                  pl.BlockSpec((d, tn), lambda j: (0, j)),
                  pl.BlockSpec((1, tn), lambda j: (0, j))],
        out_specs=pl.BlockSpec((8, tn), lambda j: (0, j)),
        out_shape=jax.ShapeDtypeStruct((8, n), F32),
        compiler_params=_cparams(("parallel",), 32),
        name="mod",
    )(c8, w_mod, b_mod.reshape(1, n))


def _inproj_kernel(x_ref, sh_ref, sc_ref, g_ref, w_ref, *o_refs, widths):
    u = (_rmsnorm(x_ref[...], g_ref[...]) * (1.0 + sc_ref[0]) + sh_ref[0]).astype(BF16)
    start = 0
    for o_ref, width in zip(o_refs, widths):
        step = min(width, 512)
        for c0 in range(0, width, step):
            o_ref[:, c0:c0 + step] = _mm(u, w_ref[:, start + c0:start + c0 + step]).astype(o_ref.dtype)
        start += width


def _inproj_call(x2, sh, sc, g, w, widths, dtypes, tokens_per_batch, tm):
    n, d = x2.shape
    per = tokens_per_batch // tm
    out_shape = [jax.ShapeDtypeStruct((n, wd), dt) for wd, dt in zip(widths, dtypes)]
    return pl.pallas_call(
        functools.partial(_inproj_kernel, widths=widths),
        grid=(n // tm,),
        in_specs=[pl.BlockSpec((tm, d), lambda i: (i, 0)),
                  pl.BlockSpec((1, 1, d), lambda i: (i // per, 0, 0)),
                  pl.BlockSpec((1, 1, d), lambda i: (i // per, 0, 0)),
                  pl.BlockSpec((1, d), lambda i: (0, 0)),
                  pl.BlockSpec(w.shape, lambda i: (0, 0))],
        out_specs=[pl.BlockSpec((tm, wd), lambda i: (i, 0)) for wd in widths],
        out_shape=out_shape,
        compiler_params=_cparams(("parallel",), 56),
        name="inproj",
    )(x2, sh, sc, g, w)


def _conv_kernel(*refs, grid_w, use_rows, tm, cw):
    if use_rows:
        prev_ref, main_ref, next_ref, w_ref, gates_ref, par_ref, q_ref, k_ref, v_ref, go_ref = refs
    else:
        main_ref, w_ref, gates_ref, par_ref, q_ref, k_ref, v_ref, go_ref = refs
    r = pl.program_id(1)
    nr = pl.num_programs(1)
    t = lax.broadcasted_iota(I32, (tm, 1), 0)
    col = jnp.bitwise_and(t, grid_w - 1)
    m_left = (col != 0).astype(F32)
    m_right = (col != grid_w - 1).astype(F32)
    has_prev = (r > 0).astype(F32)
    has_next = (r < nr - 1).astype(F32)
    for c0 in range(0, QKV_DIM, cw):
        xm = main_ref[0, :, c0:c0 + cw].astype(F32)
        if use_rows:
            xp = prev_ref[0, :, c0:c0 + cw].astype(F32) * has_prev
            xn = next_ref[0, :, c0:c0 + cw].astype(F32) * has_next
            up = jnp.concatenate([xp, xm[:tm - grid_w]], axis=0)
            dn = jnp.concatenate([xm[grid_w:], xn], axis=0)

        def colsum(kc):
            y = xm * w_ref[3 + kc:4 + kc, c0:c0 + cw]
            if use_rows:
                y = y + up * w_ref[kc:kc + 1, c0:c0 + cw] + dn * w_ref[6 + kc:7 + kc, c0:c0 + cw]
            return y

        acc = (colsum(1) + pltpu.roll(colsum(0), 1, axis=0) * m_left
               + pltpu.roll(colsum(2), tm - 1, axis=0) * m_right)
        s = acc * _sigmoid(acc)
        for h0 in range(0, cw, HEAD_DIM):
            c = c0 + h0
            seg = s[:, h0:h0 + HEAD_DIM]
            if c < 2 * QK_DIM:
                seg = seg * lax.rsqrt(jnp.sum(seg * seg, axis=-1, keepdims=True) + EPS)
            if c < QK_DIM:
                q_ref[0, :, c:c + HEAD_DIM] = (seg * HEAD_DIM ** -0.5).astype(q_ref.dtype)
            elif c < 2 * QK_DIM:
                k_ref[0, :, c - QK_DIM:c - QK_DIM + HEAD_DIM] = seg.astype(k_ref.dtype)
            else:
                v_ref[0, :, c - 2 * QK_DIM:c - 2 * QK_DIM + HEAD_DIM] = seg.astype(v_ref.dtype)
    g = gates_ref[0]
    a = g + par_ref[1:2, :]
    softplus = jnp.maximum(a, 0.0) + jnp.log1p(jnp.exp(-jnp.abs(a)))
    log_g = -jnp.exp(par_ref[0:1, :]) * softplus
    lane = lax.broadcasted_iota(I32, g.shape, 1)
    go_ref[0] = jnp.where(lane < GATE_LANE0, _sigmoid(g), log_g)


def _conv_call(qkv, conv_w, gates, par, grid_w, use_rows, tm):
    b, t, c = qkv.shape
    kern = functools.partial(_conv_kernel, grid_w=grid_w, use_rows=use_rows, tm=tm, cw=512)
    per = tm // grid_w
    nrow = t // grid_w
    in_specs = []
    args = []
    if use_rows:
        in_specs.append(pl.BlockSpec((1, grid_w, c), lambda i, r: (i, jnp.maximum(r * per - 1, 0), 0)))
        args.append(qkv)
    in_specs.append(pl.BlockSpec((1, tm, c), lambda i, r: (i, r, 0)))
    args.append(qkv)
    if use_rows:
        in_specs.append(pl.BlockSpec((1, grid_w, c), lambda i, r: (i, jnp.minimum((r + 1) * per, nrow - 1), 0)))
        args.append(qkv)
    in_specs += [pl.BlockSpec(conv_w.shape, lambda i, r: (0, 0)),
                 pl.BlockSpec((1, tm, LANES), lambda i, r: (i, r, 0)),
                 pl.BlockSpec(par.shape, lambda i, r: (0, 0))]
    args += [conv_w, gates, par]
    out_shape = [jax.ShapeDtypeStruct((b, t, QK_DIM), BF16), jax.ShapeDtypeStruct((b, t, QK_DIM), BF16),
                 jax.ShapeDtypeStruct((b, t, V_DIM), BF16), jax.ShapeDtypeStruct((b, t, LANES), F32)]
    out_specs = [pl.BlockSpec((1, tm, QK_DIM), lambda i, r: (i, r, 0)),
                 pl.BlockSpec((1, tm, QK_DIM), lambda i, r: (i, r, 0)),
                 pl.BlockSpec((1, tm, V_DIM), lambda i, r: (i, r, 0)),
                 pl.BlockSpec((1, tm, LANES), lambda i, r: (i, r, 0))]
    return pl.pallas_call(
        kern, grid=(b, t // tm), in_specs=in_specs, out_specs=out_specs, out_shape=out_shape,
        compiler_params=_cparams(("parallel", "parallel"), 48),
        name="conv_rows" if use_rows else "conv_seq",
    )(*args)


def _gdn_kernel(qf, kf, vf, gf, rf, qb, kb, vb, gb, rb, s0_ref, of, ob, sfin_ref, s_ref, *, prec):
    i = pl.program_id(1)
    nc = pl.num_programs(1)

    @pl.when(i == 0)
    def _():
        s_ref[...] = s0_ref[0]

    c = qf.shape[1]
    per = NV_HEADS // NQK_HEADS
    row = lax.broadcasted_iota(I32, (c, c), 0)
    colj = lax.broadcasted_iota(I32, (c, c), 1)
    eye = jnp.where(row == colj, 1.0, 0.0)
    nt_dims = (((1,), (1,)), ((), ()))
    tn_dims = (((0,), (0,)), ((), ()))

    seqs = []
    for d, (q_r, k_r, v_r, g_r, r_r, o_r) in enumerate(((qf, kf, vf, gf, rf, of), (qb, kb, vb, gb, rb, ob))):
        rev = d == 1
        incl = (colj >= row) if rev else (colj <= row)
        strict = (colj > row) if rev else (colj < row)
        gates = g_r[0]
        tri_c = jnp.where(incl, 1.0, 0.0)
        gcm = _mm(tri_c, gates, HIGHEST)
        gcr = _mm(r_r[0, 0], tri_c, HIGHEST, dims=nt_dims)
        for hq in range(NQK_HEADS):
            q = q_r[0, :, hq * HEAD_DIM:(hq + 1) * HEAD_DIM]
            k = k_r[0, :, hq * HEAD_DIM:(hq + 1) * HEAD_DIM]
            kq = lax.dot_general(jnp.concatenate([q, k], axis=0), k, nt_dims, preferred_element_type=F32)
            for j in range(per):
                h = hq * per + j
                idx = d * NV_HEADS + h
                gc_c = gcm[:, GATE_LANE0 + idx:GATE_LANE0 + idx + 1]
                seqs.append(dict(d=d, h=h, o_r=o_r, v_r=v_r, q=q, k=k, qk=kq[:c], kk=kq[c:], incl=incl, strict=strict,
                                 beta=gates[:, idx:idx + 1], gc_c=gc_c, gc_r=gcr[idx:idx + 1, :],
                                 ge=gc_c[0:1] if rev else gc_c[c - 1:c]))

    def same_block(m):
        sh = int(math.log2(m))
        return jnp.right_shift(row, sh) == jnp.right_shift(colj, sh)

    for s in seqs:
        s['decay'] = jnp.where(s['incl'], jnp.exp(jnp.where(s['incl'], s['gc_c'] - s['gc_r'], 0.0)), 0.0)
        s['a'] = jnp.where(s['strict'], s['beta'] * s['kk'] * s['decay'], 0.0)
        s['t'] = eye - jnp.where(same_block(2), s['a'], 0.0)
    m = 4
    while m <= c:
        between = jnp.logical_and(same_block(m), jnp.logical_not(same_block(m // 2)))
        for s in seqs:
            s['te'] = _mm(s['t'], jnp.where(between, s['a'], 0.0), prec)
        for s in seqs:
            s['t'] = s['t'] - _mm(s['te'], s['t'], prec)
        m *= 2
    for s in seqs:
        h = s['h']
        egc = jnp.exp(s['gc_c'])
        kf32 = s['k'].astype(F32)
        v = s['v_r'][0, :, h * HEAD_DIM:(h + 1) * HEAD_DIM].astype(F32)
        rhs = jnp.concatenate([s['beta'] * v, (s['beta'] * egc) * kf32], axis=1)
        s['sol'] = _mm(s['t'], rhs, prec)
        s['q_dec'] = s['q'].astype(F32) * egc
        s['k_dec'] = kf32 * jnp.exp(s['ge'] - s['gc_c'])
    for s in seqs:
        s['ws'] = _mm(jnp.concatenate([s['sol'][:, HEAD_DIM:], s['q_dec']], axis=0), s_ref[s['d'], s['h']])
    for s in seqs:
        s['u'] = s['sol'][:, :HEAD_DIM] - s['ws'][:c]
        s_ref[s['d'], s['h']] = (jnp.exp(s['ge']) * s_ref[s['d'], s['h']]
                                 + _mm(s['k_dec'], s['u'], dims=tn_dims))
    for s in seqs:
        h = s['h']
        o = s['ws'][c:] + _mm(s['qk'] * s['decay'], s['u'])
        s['o_r'][0, :, h * HEAD_DIM:(h + 1) * HEAD_DIM] = o.astype(s['o_r'].dtype)

    @pl.when(i == nc - 1)
    def _():
        sfin_ref[0] = s_ref[...]


def _gdn_call(q, k, v, go, rows, s0, prec):
    b, t, _ = q.shape
    c = GDN_CHUNK
    nc = t // c
    fwd = lambda i, n: (i, n, 0)
    bwd = lambda i, n: (i, nc - 1 - n, 0)
    rfwd = lambda i, n: (i, n, 0, 0)
    rbwd = lambda i, n: (i, nc - 1 - n, 0, 0)
    state_spec = pl.BlockSpec((1,) + s0.shape[1:], lambda i, n: (i, 0, 0, 0, 0))

    def specs(m3, mr):
        return [pl.BlockSpec((1, c, QK_DIM), m3), pl.BlockSpec((1, c, QK_DIM), m3),
                pl.BlockSpec((1, c, V_DIM), m3), pl.BlockSpec((1, c, LANES), m3),
                pl.BlockSpec((1, 1, 2 * NV_HEADS, c), mr)]

    return pl.pallas_call(
        functools.partial(_gdn_kernel, prec=prec),
        grid=(b, nc),
        in_specs=specs(fwd, rfwd) + specs(bwd, rbwd) + [state_spec],
        out_specs=[pl.BlockSpec((1, c, V_DIM), fwd), pl.BlockSpec((1, c, V_DIM), bwd), state_spec],
        out_shape=[jax.ShapeDtypeStruct((b, t, V_DIM), BF16), jax.ShapeDtypeStruct((b, t, V_DIM), BF16),
                   jax.ShapeDtypeStruct(s0.shape, F32)],
        scratch_shapes=[pltpu.VMEM(s0.shape[1:], F32)],
        compiler_params=_cparams(("parallel", "arbitrary"), 48),
        name="gdn",
    )(q, k, v, go, rows, q, k, v, go, rows, s0)


def _gdn_decay_rows(go):
    b, t, _ = go.shape
    c = GDN_CHUNK
    lg = go[..., GATE_LANE0:GATE_LANE0 + 2 * NV_HEADS]
    return jnp.transpose(lg.reshape(b, t // c, c, 2 * NV_HEADS), (0, 1, 3, 2))


def _fnet1_kernel(x_ref, f_ref, ar_ref, ai_ref):
    n = x_ref.shape[1]
    for j in range(x_ref.shape[2]):
        a = _mm(f_ref[...], x_ref[0, :, j, :])
        ar_ref[0, j] = a[:n]
        ai_ref[0, j] = a[n:]


def _fnet2_kernel(ar_ref, ai_ref, g_ref, wc_ref, o_ref, *, scale):
    n = ar_ref.shape[1]
    zs = []
    for m in range(ar_ref.shape[2]):
        a2 = jnp.concatenate([ar_ref[0, :, m, :], ai_ref[0, :, m, :]], axis=0)
        zs.append(_mm(g_ref[m], a2))
    for m, z in enumerate(zs):
        y = _mm(jnp.concatenate([z[:n], z[n:]], axis=1), wc_ref[...])
        o_ref[0, :, m, :] = y * scale


def _fnet_tables(n, groups):
    a = np.arange(n)
    ang1 = 2.0 * np.pi * np.outer(a, a) / n
    f1 = np.concatenate([np.cos(ang1), -np.sin(ang1)], axis=0)
    m = a[:, None] + n * a[None, :]
    ang2 = 2.0 * np.pi * ((m[:, :, None] * a[None, None, :]) % (n * n)) / (n * n)
    gc, gs = np.cos(ang2), np.sin(ang2)
    g2 = np.concatenate([np.concatenate([gc, gs], axis=2), np.concatenate([-gs, gc], axis=2)], axis=1)
    angc = 2.0 * np.pi * np.outer(np.arange(HEAD_DIM), np.arange(HEAD_DIM)) / HEAD_DIM
    eye = np.eye(groups)
    wc = np.concatenate([np.kron(eye, np.cos(angc)), np.kron(eye, np.sin(angc))], axis=0)
    f = lambda x: jnp.asarray(x, F32).astype(BF16)
    return f(f1), f(g2), f(wc)


def _fnet_call(f):
    b, l, c = f.shape
    n = GRID_W
    assert l == n * n
    f1, g2, wc = _fnet_tables(n, c // HEAD_DIM)
    cols = SUBLANES
    ar, ai = pl.pallas_call(
        _fnet1_kernel,
        grid=(b, n // cols),
        in_specs=[pl.BlockSpec((1, n, cols, c), lambda i, j: (i, 0, j, 0)),
                  pl.BlockSpec((2 * n, n), lambda i, j: (0, 0))],
        out_specs=[pl.BlockSpec((1, cols, n, c), lambda i, j: (i, j, 0, 0))] * 2,
        out_shape=[jax.ShapeDtypeStruct((b, n, n, c), F32)] * 2,
        compiler_params=_cparams(("parallel", "parallel"), 32),
        name="fnet1",
    )(f.reshape(b, n, n, c), f1)
    out = pl.pallas_call(
        functools.partial(_fnet2_kernel, scale=1.0 / math.sqrt(l * HEAD_DIM)),
        grid=(b, n // cols),
        in_specs=[pl.BlockSpec((1, n, cols, c), lambda i, j: (i, 0, j, 0)),
                  pl.BlockSpec((1, n, cols, c), lambda i, j: (i, 0, j, 0)),
                  pl.BlockSpec((cols, 2 * n, 2 * n), lambda i, j: (j, 0, 0)),
                  pl.BlockSpec((2 * c, c), lambda i, j: (0, 0))],
        out_specs=pl.BlockSpec((1, n, cols, c), lambda i, j: (i, 0, j, 0)),
        out_shape=jax.ShapeDtypeStruct((b, n, n, c), F32),
        compiler_params=_cparams(("parallel", "parallel"), 32),
        name="fnet2",
    )(ar, ai, g2, wc)
    return out.reshape(b, l, c)


def _merge_kernel(of_ref, ob_ref, z_ref, fm_ref, gab_ref, x_ref, g1_ref, sh2_ref, sc2_ref, gn_ref, n2_ref,
                  wg_ref, wf_ref, wm_ref, wr_ref, br_ref, x1_ref, h2_ref, lg_ref):
    d = x_ref.shape[1]
    o = of_ref[...].astype(F32) + ob_ref[...].astype(F32)
    z = z_ref[...].astype(F32)
    parts = []
    for h0 in range(0, V_DIM, HEAD_DIM):
        oh = o[:, h0:h0 + HEAD_DIM]
        parts.append(oh * lax.rsqrt(jnp.mean(oh * oh, axis=-1, keepdims=True) + EPS) * gn_ref[...])
    yb_in = jnp.concatenate(parts, axis=1) * (z * _sigmoid(z))
    yb = _mm(yb_in, wg_ref[...])
    ya = _mm(fm_ref[...], wf_ref[...])
    ga = _sigmoid(gab_ref[:, :d].astype(F32))
    gb = _sigmoid(gab_ref[:, d:].astype(F32))
    mm = _mm(ga * ya + gb * yb, wm_ref[...])
    x1 = x_ref[...] + g1_ref[0] * mm
    x1_ref[...] = x1
    h2 = _rmsnorm(x1, n2_ref[...]) * (1.0 + sc2_ref[0]) + sh2_ref[0]
    h2_ref[...] = h2
    h_hi = h2.astype(BF16)
    h_lo = (h2 - h_hi.astype(F32)).astype(BF16)
    w = wr_ref[...]
    w_hi = w.astype(BF16)
    w_lo = (w - w_hi.astype(F32)).astype(BF16)
    part = _mm(h_hi, jnp.concatenate([w_hi, w_lo], axis=1))
    lg_ref[...] = (part[:, :LANES] + part[:, LANES:]) + _mm(h_lo, w_hi) + br_ref[...]


def _merge_call(of, ob, z, fm, gab, x2, g1, sh2, sc2, gn, n2, wg, wf, wm, wr, br, tokens_per_batch, tm):
    n, d = x2.shape
    per = tokens_per_batch // tm
    tok = lambda wd: pl.BlockSpec((tm, wd), lambda i: (i, 0))
    vec = pl.BlockSpec((1, 1, d), lambda i: (i // per, 0, 0))
    full = lambda a: pl.BlockSpec(a.shape, lambda i: (0,) * a.ndim)
    return pl.pallas_call(
        _merge_kernel,
        grid=(n // tm,),
        in_specs=[tok(V_DIM), tok(V_DIM), tok(V_DIM), tok(F_DIM), tok(2 * d), tok(d), vec, vec, vec,
                  full(gn), full(n2), full(wg), full(wf), full(wm), full(wr), full(br)],
        out_specs=[tok(d), tok(d), tok(LANES)],
        out_shape=[jax.ShapeDtypeStruct((n, d), F32), jax.ShapeDtypeStruct((n, d), F32),
                   jax.ShapeDtypeStruct((n, LANES), F32)],
        compiler_params=_cparams(("parallel",), 56),
        name="merge",
    )(of, ob, z, fm, gab, x2, g1, sh2, sc2, gn, n2, wg, wf, wm, wr, br)


def _route_kernel(lg_ref, idx_ref, w_ref, rank_ref, cnt_ref, run_ref):
    i = pl.program_id(0)

    @pl.when(i == 0)
    def _():
        run_ref[...] = jnp.zeros_like(run_ref)

    l = lg_ref[...]
    tm = l.shape[0]
    lane = lax.broadcasted_iota(I32, l.shape, 1)
    vals, idxs = [], []
    for _ in range(TOP_K):
        m = jnp.max(l, axis=-1, keepdims=True)
        idx = jnp.min(jnp.where(l == m, lane, LANES), axis=-1, keepdims=True)
        vals.append(m)
        idxs.append(idx)
        l = jnp.where(lane == idx, NEG_BIG * 2.0, l)
    es = [jnp.exp(v - vals[0]) for v in vals]
    inv = 1.0 / (es[0] + es[1] + es[2] + es[3])
    picked = jnp.zeros(l.shape, F32)
    for idx in idxs:
        picked = picked + (lane == idx).astype(F32)
    r = lax.broadcasted_iota(I32, (tm, tm), 0)
    cidx = lax.broadcasted_iota(I32, (tm, tm), 1)
    before = _mm(jnp.where(cidx < r, 1.0, 0.0), picked) + run_ref[...]
    idx_out = jnp.zeros(l.shape, I32)
    w_out = jnp.zeros(l.shape, F32)
    rank_out = jnp.zeros(l.shape, F32)
    for k in range(TOP_K):
        rk = jnp.sum(jnp.where(lane == idxs[k], before, 0.0), axis=-1, keepdims=True)
        idx_out = jnp.where(lane == k, idxs[k], idx_out)
        w_out = jnp.where(lane == k, es[k] * inv, w_out)
        rank_out = jnp.where(lane == k, rk, rank_out)
    idx_ref[...] = idx_out
    w_ref[...] = w_out
    rank_ref[...] = rank_out.astype(I32)
    run_ref[...] = run_ref[...] + jnp.sum(picked, axis=0, keepdims=True)
    cnt_ref[...] = run_ref[...]


def _route_call(logits, tm):
    n = logits.shape[0]
    tok = pl.BlockSpec((tm, LANES), lambda i: (i, 0))
    return pl.pallas_call(
        _route_kernel,
        grid=(n // tm,),
        in_specs=[tok],
        out_specs=[tok, tok, tok, pl.BlockSpec((1, LANES), lambda i: (0, 0))],
        out_shape=[jax.ShapeDtypeStruct((n, LANES), I32), jax.ShapeDtypeStruct((n, LANES), F32),
                   jax.ShapeDtypeStruct((n, LANES), I32), jax.ShapeDtypeStruct((1, LANES), F32)],
        scratch_shapes=[pltpu.VMEM((1, LANES), F32)],
        compiler_params=_cparams(("arbitrary",), 32),
        name="route",
    )(logits)


def _sc_mesh():
    return plsc.VectorSubcoreMesh(core_axis_name="c", subcore_axis_name="s",
                                  num_cores=SC_CORES, num_subcores=SC_SUBCORES)


def _sc_worker_id():
    return lax.axis_index("s") * SC_CORES + lax.axis_index("c")


def _sc_scatter_rows(x, idx, n_out):
    n, d = x.shape
    per_w = idx.shape[0] // SC_WORKERS
    nwin = per_w // SC_WINDOW
    assert per_w * SC_WORKERS == idx.shape[0] and nwin * SC_WINDOW == per_w and nwin % 2 == 0 and n % per_w == 0

    def body(x_hbm, idx_hbm, out_hbm, idx_v, rows_v, sem_r, sem_w):
        wid = _sc_worker_id()
        row0 = lax.rem(wid * per_w, n)
        pltpu.sync_copy(idx_hbm.at[wid], idx_v)

        def read(j, b):
            return pltpu.make_async_copy(x_hbm.at[pl.ds(row0 + j * SC_WINDOW, SC_WINDOW)], rows_v.at[b], sem_r.at[b])

        def write(j, b):
            return pltpu.make_async_copy(rows_v.at[b], out_hbm.at[idx_v.at[j]], sem_w.at[b])

        @pl.loop(0, nwin, step=2)
        def _(j):
            read(j, 0).start()
            read(j + 1, 1).start()
            read(j, 0).wait()
            write(j, 0).start()
            read(j + 1, 1).wait()
            write(j + 1, 1).start()
            write(j, 0).wait()
            write(j + 1, 1).wait()

    return pl.kernel(
        body, out_type=jax.ShapeDtypeStruct((n_out, d), x.dtype), mesh=_sc_mesh(),
        scratch_types=[pltpu.VMEM((nwin, SC_WINDOW), I32), pltpu.VMEM((2, SC_WINDOW, d), x.dtype),
                       pltpu.SemaphoreType.DMA((2,)), pltpu.SemaphoreType.DMA((2,))],
        name="sc_scatter_rows",
    )(x, idx.reshape(SC_WORKERS, nwin, SC_WINDOW))


def _sc_gather_rows(y, idx):
    d = y.shape[1]
    total = idx.shape[0]
    per_w = total // SC_WORKERS
    nwin = per_w // SC_WINDOW
    assert per_w * SC_WORKERS == total and nwin * SC_WINDOW == per_w and nwin % 2 == 0

    def body(y_hbm, idx_hbm, out_hbm, idx_v, rows_v, sem_r, sem_w):
        wid = _sc_worker_id()
        row0 = wid * per_w
        pltpu.sync_copy(idx_hbm.at[wid], idx_v)

        def read(j, b):
            return pltpu.make_async_copy(y_hbm.at[idx_v.at[j]], rows_v.at[b], sem_r.at[b])

        def write(j, b):
            return pltpu.make_async_copy(rows_v.at[b], out_hbm.at[pl.ds(row0 + j * SC_WINDOW, SC_WINDOW)], sem_w.at[b])

        @pl.loop(0, nwin, step=2)
        def _(j):
            read(j, 0).start()
            read(j + 1, 1).start()
            read(j, 0).wait()
            write(j, 0).start()
            read(j + 1, 1).wait()
            write(j + 1, 1).start()
            write(j, 0).wait()
            write(j + 1, 1).wait()

    return pl.kernel(
        body, out_type=jax.ShapeDtypeStruct((total, d), y.dtype), mesh=_sc_mesh(),
        scratch_types=[pltpu.VMEM((nwin, SC_WINDOW), I32), pltpu.VMEM((2, SC_WINDOW, d), y.dtype),
                       pltpu.SemaphoreType.DMA((2,)), pltpu.SemaphoreType.DMA((2,))],
        name="sc_gather_rows",
    )(y, idx.reshape(SC_WORKERS, nwin, SC_WINDOW))


def _expert_kernel(be_ref, nv_ref, x_ref, wg_ref, wu_ref, wd_ref, bg_ref, bu_ref, bd_ref, y_ref, wgb, wub, wdb):
    i = pl.program_id(0)

    @pl.when(i < nv_ref[0])
    def _():
        changed = jnp.logical_or(i == 0, be_ref[i] != be_ref[jnp.maximum(i - 1, 0)])

        @pl.when(changed)
        def _():
            wgb[...] = wg_ref[0].astype(BF16)
            wub[...] = wu_ref[0].astype(BF16)
            wdb[...] = wd_ref[0].astype(BF16)

        x = x_ref[...].astype(BF16)
        gate = jnp.minimum(_mm(x, wgb[...]) + bg_ref[0], SWIGLU_LIMIT)
        up = jnp.clip(_mm(x, wub[...]) + bu_ref[0], -SWIGLU_LIMIT, SWIGLU_LIMIT)
        act = (up + 1.0) * gate * _sigmoid(SWIGLU_ALPHA * gate)
        y_ref[...] = _mm(act, wdb[...]) + bd_ref[0]


def _expert_call(block_e, n_valid, xs, w_gate, w_up, w_down, b_gate, b_up, b_down):
    nb = block_e.shape[0]
    tmb = EXPERT_BLOCK
    ne, d, de = w_gate.shape
    wspec = lambda s: pl.BlockSpec((1,) + s, lambda i, be, nv: (be[i], 0, 0))
    grid_spec = pltpu.PrefetchScalarGridSpec(
        num_scalar_prefetch=2,
        grid=(nb,),
        in_specs=[pl.BlockSpec((tmb, d), lambda i, be, nv: (i, 0)),
                  wspec((d, de)), wspec((d, de)), wspec((de, d)),
                  wspec((1, de)), wspec((1, de)), wspec((1, d))],
        out_specs=pl.BlockSpec((tmb, d), lambda i, be, nv: (i, 0)),
        scratch_shapes=[pltpu.VMEM((d, de), BF16), pltpu.VMEM((d, de), BF16), pltpu.VMEM((de, d), BF16)],
    )
    return pl.pallas_call(
        _expert_kernel,
        grid_spec=grid_spec,
        out_shape=jax.ShapeDtypeStruct(xs.shape, F32),
        compiler_params=_cparams(("arbitrary",), 56),
        name="expert",
    )(block_e, n_valid, xs, w_gate, w_up, w_down,
      b_gate.reshape(ne, 1, de), b_up.reshape(ne, 1, de), b_down.reshape(ne, 1, d))


def _combine_kernel(y0, y1, y2, y3, w_ref, x1_ref, g2_ref, fg_ref, o_ref):
    w = w_ref[...]
    moe = (w[:, 0:1] * y0[...] + w[:, 1:2] * y1[...]) + (w[:, 2:3] * y2[...] + w[:, 3:4] * y3[...])
    o_ref[...] = _rmsnorm(x1_ref[...] + g2_ref[0] * moe, fg_ref[...])


def _combine_call(y4, top_w, x1, g2, fg, tokens_per_batch, tm):
    n, d = x1.shape
    per = tokens_per_batch // tm
    nt = n // tm
    yspec = lambda k: pl.BlockSpec((tm, d), lambda i: (k * nt + i, 0))
    return pl.pallas_call(
        _combine_kernel,
        grid=(nt,),
        in_specs=[yspec(0), yspec(1), yspec(2), yspec(3),
                  pl.BlockSpec((tm, LANES), lambda i: (i, 0)),
                  pl.BlockSpec((tm, d), lambda i: (i, 0)),
                  pl.BlockSpec((1, 1, d), lambda i: (i // per, 0, 0)),
                  pl.BlockSpec((1, d), lambda i: (0, 0))],
        out_specs=pl.BlockSpec((tm, d), lambda i: (i, 0)),
        out_shape=jax.ShapeDtypeStruct((n, d), F32),
        compiler_params=_cparams(("parallel",), 48),
        name="combine",
    )(y4, y4, y4, y4, top_w, x1, g2, fg)


def _routing_tables(top_idx, rank, counts, n):
    tmb = EXPERT_BLOCK
    counts = counts.astype(I32)
    padded = (counts + tmb - 1) // tmb * tmb
    pad_end = jnp.cumsum(padded)
    pad_start = pad_end - padded
    n_blocks = -(-(n * TOP_K + N_EXPERTS * (tmb - 1)) // tmb)
    n_slots = n_blocks * tmb
    onehot = top_idx[:, :, None] == jnp.arange(N_EXPERTS, dtype=I32)[None, None, :]
    dest = jnp.sum(jnp.where(onehot, pad_start[None, None, :], 0), axis=-1) + rank
    dest_flat = jnp.transpose(dest).reshape(-1)
    block_start = jnp.arange(n_blocks, dtype=I32) * tmb
    block_e = jnp.minimum(jnp.sum((pad_end[None, :] <= block_start[:, None]).astype(I32), axis=1), N_EXPERTS - 1)
    n_valid = (pad_end[-1:] // tmb).astype(I32)
    return block_e, n_valid, dest_flat, n_slots


def _gdn_branch(xtok, sh, sc, norm_g, w_cols, widths, dtypes, conv_w, par, tokens_per_batch, grid_w, use_rows,
                tm_proj, tm_conv, s0, prec):
    b = xtok.shape[0] // tokens_per_batch
    outs = _inproj_call(xtok, sh, sc, norm_g, w_cols, widths, dtypes, tokens_per_batch, tm_proj)
    qkv, gates = outs[0], outs[-1]
    q, k, v, go = _conv_call(qkv.reshape(b, tokens_per_batch, QKV_DIM), conv_w,
                             gates.reshape(b, tokens_per_batch, LANES), par, grid_w, use_rows, tm_conv)
    o_f, o_b, s_fin = _gdn_call(q, k, v, go, _gdn_decay_rows(go), s0, prec)
    return outs, o_f, o_b, s_fin


def kernel(x, c, ctx, c_ctx, w_mod, b_mod, norm1_g, norm2_g, w_in, conv_w, a_log, dt_bias, gdn_norm_g,
           w_fourier_out, w_gdn_out, w_merge_out, w_router, b_router, w_gate, b_gate, w_up, b_up,
           w_down, b_down, final_norm_g):
    b, l, d = x.shape
    n = b * l
    n_ctx = ctx.shape[1]
    assert w_mod.shape[0] == 1 and l == GRID_W * GRID_W and d == V_DIM
    prec = None

    c8 = jnp.concatenate([c, c_ctx[None, :], jnp.zeros((8 - b - 1, d), F32)], axis=0)
    mod = _mod_call(c8, w_mod[0], b_mod[0])
    sh1, sc1, g1, sh2, sc2, g2 = [mod[:b, j * d:(j + 1) * d].reshape(b, 1, d) for j in range(6)]
    csh1 = jnp.broadcast_to(mod[b:b + 1, 0:d].reshape(1, 1, d), (b, 1, d))
    csc1 = jnp.broadcast_to(mod[b:b + 1, d:2 * d].reshape(1, 1, d), (b, 1, d))

    wi = w_in[0]
    off_gate = QKV_DIM
    off_z = off_gate + 4 * NV_HEADS
    off_f = off_z + V_DIM
    off_ga = off_f + F_DIM
    gate_cols = jnp.pad(wi[:, off_gate:off_z], ((0, 0), (0, LANES - 4 * NV_HEADS)))
    w_lat = jnp.concatenate([wi[:, :QKV_DIM], wi[:, off_z:off_f], wi[:, off_f:off_ga], wi[:, off_ga:], gate_cols],
                            axis=1).astype(BF16)
    w_ctx = jnp.concatenate([wi[:, :QKV_DIM], gate_cols], axis=1).astype(BF16)
    par = jnp.pad(jnp.stack([a_log[0].reshape(-1), dt_bias[0].reshape(-1)]),
                  ((0, 6), (GATE_LANE0, LANES - 2 * GATE_LANE0)))
    n1 = norm1_g[0].reshape(1, d)
    cw = conv_w[0].reshape(9, QKV_DIM)

    zero_state = jnp.zeros((b, 2, NV_HEADS, HEAD_DIM, HEAD_DIM), F32)
    _, _, _, s_ctx = _gdn_branch(ctx.reshape(b * n_ctx, d), csh1, csc1, n1, w_ctx, (QKV_DIM, LANES), (BF16, F32),
                                 cw, par, n_ctx, n_ctx, False, n_ctx, n_ctx, zero_state, prec)

    x2 = x.reshape(n, d)
    outs, o_f, o_b, _ = _gdn_branch(x2, sh1, sc1, n1, w_lat, (QKV_DIM, V_DIM, F_DIM, 2 * d, LANES),
                                    (BF16, BF16, F32, BF16, F32), cw, par, l, GRID_W, True, 512, 512, s_ctx, prec)
    _, z, f, gab, _ = outs
    fmix = _fnet_call(f.reshape(b, l, F_DIM)).reshape(n, F_DIM)

    wr = jnp.pad(w_router[0], ((0, 0), (0, LANES - N_EXPERTS)))
    br = jnp.pad(b_router[0], (0, LANES - N_EXPERTS), constant_values=NEG_BIG).reshape(1, LANES)
    x1, h2, logits = _merge_call(
        o_f.reshape(n, V_DIM), o_b.reshape(n, V_DIM), z, fmix, gab, x2, g1, sh2, sc2,
        gdn_norm_g[0].reshape(1, HEAD_DIM), norm2_g[0].reshape(1, d),
        w_gdn_out[0].astype(BF16), w_fourier_out[0].astype(BF16), w_merge_out[0].astype(BF16), wr, br, l, 256)

    top_idx, top_w, rank, counts = _route_call(logits, 512)
    block_e, n_valid, dest_flat, n_slots = _routing_tables(top_idx[:, :TOP_K], rank[:, :TOP_K],
                                                           counts[0, :N_EXPERTS], n)
    xs = _sc_scatter_rows(h2, dest_flat, n_slots)
    ys = _expert_call(block_e, n_valid, xs, w_gate[0], w_up[0], w_down[0], b_gate[0], b_up[0], b_down[0])
    y4 = _sc_gather_rows(ys, dest_flat)
    out = _combine_call(y4, top_w, x1, g2, final_norm_g.reshape(1, d), l, 256)
    return out.reshape(b, l, d)
```

```python
import functools
import math

import jax
import jax.numpy as jnp
import numpy as np
from jax import lax
from jax.experimental import pallas as pl
from jax.experimental.pallas import tpu as pltpu
from jax.experimental.pallas import tpu_sc as plsc

F32 = jnp.float32
BF16 = jnp.bfloat16
I32 = jnp.int32
HIGHEST = lax.Precision.HIGHEST

GRID_W = 64
NQK_HEADS = 4
NV_HEADS = 8
HEAD_DIM = 128
QK_DIM = NQK_HEADS * HEAD_DIM
V_DIM = NV_HEADS * HEAD_DIM
QKV_DIM = 2 * QK_DIM + V_DIM
F_GROUPS = 4
F_DIM = F_GROUPS * HEAD_DIM
N_EXPERTS = 32
TOP_K = 4
SWIGLU_ALPHA = 1.702
SWIGLU_LIMIT = 7.0
EPS = 1e-6

LANES = 128
SUBLANES = 8
GATE_LANE0 = 16
GDN_CHUNK = 128
EXPERT_BLOCK = 512
SC_CORES = 2
SC_SUBCORES = 16
SC_WORKERS = SC_CORES * SC_SUBCORES
SC_WINDOW = 64
NEG_BIG = -1e30
MIB = 2 ** 20


def _cparams(sem, vmem_mib):
    return pltpu.CompilerParams(dimension_semantics=sem, vmem_limit_bytes=vmem_mib * MIB)


def _mm(a, b, prec=None, dims=(((1,), (0,)), ((), ()))):
    if prec is None:
        return lax.dot_general(a.astype(BF16), b.astype(BF16), dims, preferred_element_type=F32)
    return lax.dot_general(a.astype(F32), b.astype(F32), dims, precision=prec, preferred_element_type=F32)


def _sigmoid(x):
    return 1.0 / (1.0 + jnp.exp(-x))


def _rmsnorm(x, g):
    return x * lax.rsqrt(jnp.mean(x * x, axis=-1, keepdims=True) + EPS) * g


def _pack_bf16_pairs(x):
    half = x.shape[1] // 2
    bits = lax.bitcast_convert_type(x.astype(BF16).astype(F32), jnp.uint32)
    packed = jnp.bitwise_or(jnp.right_shift(bits[:, :half], jnp.uint32(16)),
                            jnp.bitwise_and(bits[:, half:], jnp.uint32(0xFFFF0000)))
    return lax.bitcast_convert_type(packed, I32)


def _unpack_bf16_pairs(p):
    bits = lax.bitcast_convert_type(p, jnp.uint32)
    lo = lax.bitcast_convert_type(jnp.left_shift(bits, jnp.uint32(16)), F32)
    hi = lax.bitcast_convert_type(jnp.bitwise_and(bits, jnp.uint32(0xFFFF0000)), F32)
    return jnp.concatenate([lo, hi], axis=1)


def _mod_kernel(c_ref, w_ref, b_ref, o_ref):
    c = c_ref[...]
    o_ref[...] = _mm(c * _sigmoid(c), w_ref[...], HIGHEST) + b_ref[...]


def _mod_call(c8, w_mod, b_mod):
    d, n = w_mod.shape
    tn = 1536
    return pl.pallas_call(
        _mod_kernel,
        grid=(n // tn,),
        in_specs=[pl.BlockSpec((8, d), lambda j: (0, 0)),
                  pl.BlockSpec((d, tn), lambda j: (0, j)),
                  pl.BlockSpec((1, tn), lambda j: (0, j))],
        out_specs=pl.BlockSpec((8, tn), lambda j: (0, j)),
        out_shape=jax.ShapeDtypeStruct((8, n), F32),
        compiler_params=_cparams(("parallel",), 32),
        name="mod",
    )(c8, w_mod, b_mod.reshape(1, n))


def _inproj_kernel(x_ref, sh_ref, sc_ref, g_ref, w_ref, *o_refs, widths):
    u = (_rmsnorm(x_ref[...], g_ref[...]) * (1.0 + sc_ref[0]) + sh_ref[0]).astype(BF16)
    start = 0
    for o_ref, width in zip(o_refs, widths):
        step = min(width, 512)
        for c0 in range(0, width, step):
            o_ref[:, c0:c0 + step] = _mm(u, w_ref[:, start + c0:start + c0 + step]).astype(o_ref.dtype)
        start += width


def _inproj_call(x2, sh, sc, g, w, widths, dtypes, tokens_per_batch, tm):
    n, d = x2.shape
    per = tokens_per_batch // tm
    out_shape = [jax.ShapeDtypeStruct((n, wd), dt) for wd, dt in zip(widths, dtypes)]
    return pl.pallas_call(
        functools.partial(_inproj_kernel, widths=widths),
        grid=(n // tm,),
        in_specs=[pl.BlockSpec((tm, d), lambda i: (i, 0)),
                  pl.BlockSpec((1, 1, d), lambda i: (i // per, 0, 0)),
                  pl.BlockSpec((1, 1, d), lambda i: (i // per, 0, 0)),
                  pl.BlockSpec((1, d), lambda i: (0, 0)),
                  pl.BlockSpec(w.shape, lambda i: (0, 0))],
        out_specs=[pl.BlockSpec((tm, wd), lambda i: (i, 0)) for wd in widths],
        out_shape=out_shape,
        compiler_params=_cparams(("parallel",), 56),
        name="inproj",
    )(x2, sh, sc, g, w)


def _conv_kernel(*refs, grid_w, use_rows, tm, cw):
    if use_rows:
        prev_ref, main_ref, next_ref, w_ref, gates_ref, par_ref, q_ref, k_ref, v_ref, go_ref = refs
    else:
        main_ref, w_ref, gates_ref, par_ref, q_ref, k_ref, v_ref, go_ref = refs
    r = pl.program_id(1)
    nr = pl.num_programs(1)
    t = lax.broadcasted_iota(I32, (tm, 1), 0)
    col = jnp.bitwise_and(t, grid_w - 1)
    m_left = (col != 0).astype(F32)
    m_right = (col != grid_w - 1).astype(F32)
    has_prev = (r > 0).astype(F32)
    has_next = (r < nr - 1).astype(F32)
    for c0 in range(0, QKV_DIM, cw):
        xm = main_ref[0, :, c0:c0 + cw].astype(F32)
        if use_rows:
            xp = prev_ref[0, :, c0:c0 + cw].astype(F32) * has_prev
            xn = next_ref[0, :, c0:c0 + cw].astype(F32) * has_next
            up = jnp.concatenate([xp, xm[:tm - grid_w]], axis=0)
            dn = jnp.concatenate([xm[grid_w:], xn], axis=0)

        def colsum(kc):
            y = xm * w_ref[3 + kc:4 + kc, c0:c0 + cw]
            if use_rows:
                y = y + up * w_ref[kc:kc + 1, c0:c0 + cw] + dn * w_ref[6 + kc:7 + kc, c0:c0 + cw]
            return y

        acc = (colsum(1) + pltpu.roll(colsum(0), 1, axis=0) * m_left
               + pltpu.roll(colsum(2), tm - 1, axis=0) * m_right)
        s = acc * _sigmoid(acc)
        for h0 in range(0, cw, HEAD_DIM):
            c = c0 + h0
            seg = s[:, h0:h0 + HEAD_DIM]
            if c < 2 * QK_DIM:
                seg = seg * lax.rsqrt(jnp.sum(seg * seg, axis=-1, keepdims=True) + EPS)
            if c < QK_DIM:
                q_ref[0, :, c:c + HEAD_DIM] = (seg * HEAD_DIM ** -0.5).astype(q_ref.dtype)
            elif c < 2 * QK_DIM:
                k_ref[0, :, c - QK_DIM:c - QK_DIM + HEAD_DIM] = seg.astype(k_ref.dtype)
            else:
                v_ref[0, :, c - 2 * QK_DIM:c - 2 * QK_DIM + HEAD_DIM] = seg.astype(v_ref.dtype)
    g = gates_ref[0]
    a = g + par_ref[1:2, :]
    softplus = jnp.maximum(a, 0.0) + jnp.log1p(jnp.exp(-jnp.abs(a)))
    log_g = -jnp.exp(par_ref[0:1, :]) * softplus
    lane = lax.broadcasted_iota(I32, g.shape, 1)
    go_ref[0] = jnp.where(lane < GATE_LANE0, _sigmoid(g), log_g)


def _conv_call(qkv, conv_w, gates, par, grid_w, use_rows, tm):
    b, t, c = qkv.shape
    kern = functools.partial(_conv_kernel, grid_w=grid_w, use_rows=use_rows, tm=tm, cw=512)
    per = tm // grid_w
    nrow = t // grid_w
    in_specs = []
    args = []
    if use_rows:
        in_specs.append(pl.BlockSpec((1, grid_w, c), lambda i, r: (i, jnp.maximum(r * per - 1, 0), 0)))
        args.append(qkv)
    in_specs.append(pl.BlockSpec((1, tm, c), lambda i, r: (i, r, 0)))
    args.append(qkv)
    if use_rows:
        in_specs.append(pl.BlockSpec((1, grid_w, c), lambda i, r: (i, jnp.minimum((r + 1) * per, nrow - 1), 0)))
        args.append(qkv)
    in_specs += [pl.BlockSpec(conv_w.shape, lambda i, r: (0, 0)),
                 pl.BlockSpec((1, tm, LANES), lambda i, r: (i, r, 0)),
                 pl.BlockSpec(par.shape, lambda i, r: (0, 0))]
    args += [conv_w, gates, par]
    out_shape = [jax.ShapeDtypeStruct((b, t, QK_DIM), BF16), jax.ShapeDtypeStruct((b, t, QK_DIM), BF16),
                 jax.ShapeDtypeStruct((b, t, V_DIM), BF16), jax.ShapeDtypeStruct((b, t, LANES), F32)]
    out_specs = [pl.BlockSpec((1, tm, QK_DIM), lambda i, r: (i, r, 0)),
                 pl.BlockSpec((1, tm, QK_DIM), lambda i, r: (i, r, 0)),
                 pl.BlockSpec((1, tm, V_DIM), lambda i, r: (i, r, 0)),
                 pl.BlockSpec((1, tm, LANES), lambda i, r: (i, r, 0))]
    return pl.pallas_call(
        kern, grid=(b, t // tm), in_specs=in_specs, out_specs=out_specs, out_shape=out_shape,
        compiler_params=_cparams(("parallel", "parallel"), 48),
        name="conv_rows" if use_rows else "conv_seq",
    )(*args)


def _gdn_kernel(qf, kf, vf, gf, rf, qb, kb, vb, gb, rb, s0_ref, of, ob, sfin_ref, s_ref, *, prec):
    i = pl.program_id(1)
    nc = pl.num_programs(1)

    @pl.when(i == 0)
    def _():
        s_ref[...] = s0_ref[0]

    c = qf.shape[1]
    per = NV_HEADS // NQK_HEADS
    row = lax.broadcasted_iota(I32, (c, c), 0)
    colj = lax.broadcasted_iota(I32, (c, c), 1)
    eye = jnp.where(row == colj, 1.0, 0.0)
    nt_dims = (((1,), (1,)), ((), ()))
    tn_dims = (((0,), (0,)), ((), ()))

    seqs = []
    for d, (q_r, k_r, v_r, g_r, r_r, o_r) in enumerate(((qf, kf, vf, gf, rf, of), (qb, kb, vb, gb, rb, ob))):
        rev = d == 1
        incl = (colj >= row) if rev else (colj <= row)
        strict = (colj > row) if rev else (colj < row)
        gates = g_r[0]
        tri_c = jnp.where(incl, 1.0, 0.0)
        gcm = _mm(tri_c, gates, HIGHEST)
        gcr = _mm(r_r[0, 0], tri_c, HIGHEST, dims=nt_dims)
        for hq in range(NQK_HEADS):
            q = q_r[0, :, hq * HEAD_DIM:(hq + 1) * HEAD_DIM]
            k = k_r[0, :, hq * HEAD_DIM:(hq + 1) * HEAD_DIM]
            kq = lax.dot_general(jnp.concatenate([q, k], axis=0), k, nt_dims, preferred_element_type=F32)
            for j in range(per):
                h = hq * per + j
                idx = d * NV_HEADS + h
                gc_c = gcm[:, GATE_LANE0 + idx:GATE_LANE0 + idx + 1]
                seqs.append(dict(d=d, h=h, o_r=o_r, v_r=v_r, q=q, k=k, qk=kq[:c], kk=kq[c:], incl=incl, strict=strict,
                                 beta=gates[:, idx:idx + 1], gc_c=gc_c, gc_r=gcr[idx:idx + 1, :],
                                 ge=gc_c[0:1] if rev else gc_c[c - 1:c]))

    def same_block(m):
        sh = int(math.log2(m))
        return jnp.right_shift(row, sh) == jnp.right_shift(colj, sh)

    for s in seqs:
        s['decay'] = jnp.where(s['incl'], jnp.exp(jnp.where(s['incl'], s['gc_c'] - s['gc_r'], 0.0)), 0.0)
        s['a'] = jnp.where(s['strict'], s['beta'] * s['kk'] * s['decay'], 0.0)
        s['t'] = eye - jnp.where(same_block(2), s['a'], 0.0)
    m = 4
    while m <= c:
        between = jnp.logical_and(same_block(m), jnp.logical_not(same_block(m // 2)))
        for s in seqs:
            s['te'] = _mm(s['t'], jnp.where(between, s['a'], 0.0), prec)
        for s in seqs:
            s['t'] = s['t'] - _mm(s['te'], s['t'], prec)
        m *= 2
    for s in seqs:
        h = s['h']
        egc = jnp.exp(s['gc_c'])
        kf32 = s['k'].astype(F32)
        v = s['v_r'][0, :, h * HEAD_DIM:(h + 1) * HEAD_DIM].astype(F32)
        rhs = jnp.concatenate([s['beta'] * v, (s['beta'] * egc) * kf32], axis=1)
        s['sol'] = _mm(s['t'], rhs, prec)
        s['q_dec'] = s['q'].astype(F32) * egc
        s['k_dec'] = kf32 * jnp.exp(s['ge'] - s['gc_c'])
    for s in seqs:
        s['ws'] = _mm(jnp.concatenate([s['sol'][:, HEAD_DIM:], s['q_dec']], axis=0), s_ref[s['d'], s['h']])
    for s in seqs:
        s['u'] = s['sol'][:, :HEAD_DIM] - s['ws'][:c]
        s_ref[s['d'], s['h']] = (jnp.exp(s['ge']) * s_ref[s['d'], s['h']]
                                 + _mm(s['k_dec'], s['u'], dims=tn_dims))
    for s in seqs:
        h = s['h']
        o = s['ws'][c:] + _mm(s['qk'] * s['decay'], s['u'])
        s['o_r'][0, :, h * HEAD_DIM:(h + 1) * HEAD_DIM] = o.astype(s['o_r'].dtype)

    @pl.when(i == nc - 1)
    def _():
        sfin_ref[0] = s_ref[...]


def _gdn_call(q, k, v, go, rows, s0, prec):
    b, t, _ = q.shape
    c = GDN_CHUNK
    nc = t // c
    fwd = lambda i, n: (i, n, 0)
    bwd = lambda i, n: (i, nc - 1 - n, 0)
    rfwd = lambda i, n: (i, n, 0, 0)
    rbwd = lambda i, n: (i, nc - 1 - n, 0, 0)
    state_spec = pl.BlockSpec((1,) + s0.shape[1:], lambda i, n: (i, 0, 0, 0, 0))

    def specs(m3, mr):
        return [pl.BlockSpec((1, c, QK_DIM), m3), pl.BlockSpec((1, c, QK_DIM), m3),
                pl.BlockSpec((1, c, V_DIM), m3), pl.BlockSpec((1, c, LANES), m3),
                pl.BlockSpec((1, 1, 2 * NV_HEADS, c), mr)]

    return pl.pallas_call(
        functools.partial(_gdn_kernel, prec=prec),
        grid=(b, nc),
        in_specs=specs(fwd, rfwd) + specs(bwd, rbwd) + [state_spec],
        out_specs=[pl.BlockSpec((1, c, V_DIM), fwd), pl.BlockSpec((1, c, V_DIM), bwd), state_spec],
        out_shape=[jax.ShapeDtypeStruct((b, t, V_DIM), BF16), jax.ShapeDtypeStruct((b, t, V_DIM), BF16),
                   jax.ShapeDtypeStruct(s0.shape, F32)],
        scratch_shapes=[pltpu.VMEM(s0.shape[1:], F32)],
        compiler_params=_cparams(("parallel", "arbitrary"), 48),
        name="gdn",
    )(q, k, v, go, rows, q, k, v, go, rows, s0)


def _gdn_decay_rows(go):
    b, t, _ = go.shape
    c = GDN_CHUNK
    lg = go[..., GATE_LANE0:GATE_LANE0 + 2 * NV_HEADS]
    return jnp.transpose(lg.reshape(b, t // c, c, 2 * NV_HEADS), (0, 1, 3, 2))


def _fnet1_kernel(x_ref, f_ref, ar_ref, ai_ref):
    n = x_ref.shape[1]
    for j in range(x_ref.shape[2]):
        a = _mm(f_ref[...], x_ref[0, :, j, :])
        ar_ref[0, j] = a[:n]
        ai_ref[0, j] = a[n:]


def _fnet2_kernel(ar_ref, ai_ref, g_ref, wc_ref, o_ref, *, scale):
    n = ar_ref.shape[1]
    zs = []
    for m in range(ar_ref.shape[2]):
        a2 = jnp.concatenate([ar_ref[0, :, m, :], ai_ref[0, :, m, :]], axis=0)
        zs.append(_mm(g_ref[m], a2))
    for m, z in enumerate(zs):
        y = _mm(jnp.concatenate([z[:n], z[n:]], axis=1), wc_ref[...])
        o_ref[0, :, m, :] = y * scale


def _fnet_tables(n, groups):
    a = np.arange(n)
    ang1 = 2.0 * np.pi * np.outer(a, a) / n
    f1 = np.concatenate([np.cos(ang1), -np.sin(ang1)], axis=0)
    m = a[:, None] + n * a[None, :]
    ang2 = 2.0 * np.pi * ((m[:, :, None] * a[None, None, :]) % (n * n)) / (n * n)
    gc, gs = np.cos(ang2), np.sin(ang2)
    g2 = np.concatenate([np.concatenate([gc, gs], axis=2), np.concatenate([-gs, gc], axis=2)], axis=1)
    angc = 2.0 * np.pi * np.outer(np.arange(HEAD_DIM), np.arange(HEAD_DIM)) / HEAD_DIM
    eye = np.eye(groups)
    wc = np.concatenate([np.kron(eye, np.cos(angc)), np.kron(eye, np.sin(angc))], axis=0)
    f = lambda x: jnp.asarray(x, F32).astype(BF16)
    return f(f1), f(g2), f(wc)


def _fnet_call(f):
    b, l, c = f.shape
    n = GRID_W
    assert l == n * n
    f1, g2, wc = _fnet_tables(n, c // HEAD_DIM)
    cols = SUBLANES
    ar, ai = pl.pallas_call(
        _fnet1_kernel,
        grid=(b, n // cols),
        in_specs=[pl.BlockSpec((1, n, cols, c), lambda i, j: (i, 0, j, 0)),
                  pl.BlockSpec((2 * n, n), lambda i, j: (0, 0))],
        out_specs=[pl.BlockSpec((1, cols, n, c), lambda i, j: (i, j, 0, 0))] * 2,
        out_shape=[jax.ShapeDtypeStruct((b, n, n, c), F32)] * 2,
        compiler_params=_cparams(("parallel", "parallel"), 32),
        name="fnet1",
    )(f.reshape(b, n, n, c), f1)
    out = pl.pallas_call(
        functools.partial(_fnet2_kernel, scale=1.0 / math.sqrt(l * HEAD_DIM)),
        grid=(b, n // cols),
        in_specs=[pl.BlockSpec((1, n, cols, c), lambda i, j: (i, 0, j, 0)),
                  pl.BlockSpec((1, n, cols, c), lambda i, j: (i, 0, j, 0)),
                  pl.BlockSpec((cols, 2 * n, 2 * n), lambda i, j: (j, 0, 0)),
                  pl.BlockSpec((2 * c, c), lambda i, j: (0, 0))],
        out_specs=pl.BlockSpec((1, n, cols, c), lambda i, j: (i, 0, j, 0)),
        out_shape=jax.ShapeDtypeStruct((b, n, n, c), F32),
        compiler_params=_cparams(("parallel", "parallel"), 32),
        name="fnet2",
    )(ar, ai, g2, wc)
    return out.reshape(b, l, c)


def _merge_kernel(of_ref, ob_ref, z_ref, fm_ref, gab_ref, x_ref, g1_ref, sh2_ref, sc2_ref, gn_ref, n2_ref,
                  wg_ref, wf_ref, wm_ref, wr_ref, br_ref, x1_ref, h2_ref, lg_ref):
    d = x_ref.shape[1]
    o = of_ref[...].astype(F32) + ob_ref[...].astype(F32)
    z = z_ref[...].astype(F32)
    parts = []
    for h0 in range(0, V_DIM, HEAD_DIM):
        oh = o[:, h0:h0 + HEAD_DIM]
        parts.append(oh * lax.rsqrt(jnp.mean(oh * oh, axis=-1, keepdims=True) + EPS) * gn_ref[...])
    yb_in = jnp.concatenate(parts, axis=1) * (z * _sigmoid(z))
    yb = _mm(yb_in, wg_ref[...])
    ya = _mm(fm_ref[...], wf_ref[...])
    ga = _sigmoid(gab_ref[:, :d].astype(F32))
    gb = _sigmoid(gab_ref[:, d:].astype(F32))
    mm = _mm(ga * ya + gb * yb, wm_ref[...])
    x1 = x_ref[...] + g1_ref[0] * mm
    x1_ref[...] = x1
    h2 = _rmsnorm(x1, n2_ref[...]) * (1.0 + sc2_ref[0]) + sh2_ref[0]
    h2_ref[...] = _pack_bf16_pairs(h2)
    h_hi = h2.astype(BF16)
    h_lo = (h2 - h_hi.astype(F32)).astype(BF16)
    w = wr_ref[...]
    w_hi = w.astype(BF16)
    w_lo = (w - w_hi.astype(F32)).astype(BF16)
    part = _mm(h_hi, jnp.concatenate([w_hi, w_lo], axis=1))
    lg_ref[...] = (part[:, :LANES] + part[:, LANES:]) + _mm(h_lo, w_hi) + br_ref[...]


def _merge_call(of, ob, z, fm, gab, x2, g1, sh2, sc2, gn, n2, wg, wf, wm, wr, br, tokens_per_batch, tm):
    n, d = x2.shape
    per = tokens_per_batch // tm
    tok = lambda wd: pl.BlockSpec((tm, wd), lambda i: (i, 0))
    vec = pl.BlockSpec((1, 1, d), lambda i: (i // per, 0, 0))
    full = lambda a: pl.BlockSpec(a.shape, lambda i: (0,) * a.ndim)
    return pl.pallas_call(
        _merge_kernel,
        grid=(n // tm,),
        in_specs=[tok(V_DIM), tok(V_DIM), tok(V_DIM), tok(F_DIM), tok(2 * d), tok(d), vec, vec, vec,
                  full(gn), full(n2), full(wg), full(wf), full(wm), full(wr), full(br)],
        out_specs=[tok(d), tok(d // 2), tok(LANES)],
        out_shape=[jax.ShapeDtypeStruct((n, d), F32), jax.ShapeDtypeStruct((n, d // 2), I32),
                   jax.ShapeDtypeStruct((n, LANES), F32)],
        compiler_params=_cparams(("parallel",), 56),
        name="merge",
    )(of, ob, z, fm, gab, x2, g1, sh2, sc2, gn, n2, wg, wf, wm, wr, br)


def _route_kernel(lg_ref, idx_ref, w_ref, rank_ref, cnt_ref, run_ref):
    i = pl.program_id(0)

    @pl.when(i == 0)
    def _():
        run_ref[...] = jnp.zeros_like(run_ref)

    l = lg_ref[...]
    tm = l.shape[0]
    lane = lax.broadcasted_iota(I32, l.shape, 1)
    vals, idxs = [], []
    for _ in range(TOP_K):
        m = jnp.max(l, axis=-1, keepdims=True)
        idx = jnp.min(jnp.where(l == m, lane, LANES), axis=-1, keepdims=True)
        vals.append(m)
        idxs.append(idx)
        l = jnp.where(lane == idx, NEG_BIG * 2.0, l)
    es = [jnp.exp(v - vals[0]) for v in vals]
    inv = 1.0 / (es[0] + es[1] + es[2] + es[3])
    picked = jnp.zeros(l.shape, F32)
    for idx in idxs:
        picked = picked + (lane == idx).astype(F32)
    r = lax.broadcasted_iota(I32, (tm, tm), 0)
    cidx = lax.broadcasted_iota(I32, (tm, tm), 1)
    before = _mm(jnp.where(cidx < r, 1.0, 0.0), picked) + run_ref[...]
    idx_out = jnp.zeros(l.shape, I32)
    w_out = jnp.zeros(l.shape, F32)
    rank_out = jnp.zeros(l.shape, F32)
    for k in range(TOP_K):
        rk = jnp.sum(jnp.where(lane == idxs[k], before, 0.0), axis=-1, keepdims=True)
        idx_out = jnp.where(lane == k, idxs[k], idx_out)
        w_out = jnp.where(lane == k, es[k] * inv, w_out)
        rank_out = jnp.where(lane == k, rk, rank_out)
    idx_ref[...] = idx_out
    w_ref[...] = w_out
    rank_ref[...] = rank_out.astype(I32)
    run_ref[...] = run_ref[...] + jnp.sum(picked, axis=0, keepdims=True)
    cnt_ref[...] = run_ref[...]


def _route_call(logits, tm):
    n = logits.shape[0]
    tok = pl.BlockSpec((tm, LANES), lambda i: (i, 0))
    return pl.pallas_call(
        _route_kernel,
        grid=(n // tm,),
        in_specs=[tok],
        out_specs=[tok, tok, tok, pl.BlockSpec((1, LANES), lambda i: (0, 0))],
        out_shape=[jax.ShapeDtypeStruct((n, LANES), I32), jax.ShapeDtypeStruct((n, LANES), F32),
                   jax.ShapeDtypeStruct((n, LANES), I32), jax.ShapeDtypeStruct((1, LANES), F32)],
        scratch_shapes=[pltpu.VMEM((1, LANES), F32)],
        compiler_params=_cparams(("arbitrary",), 32),
        name="route",
    )(logits)


def _sc_mesh():
    return plsc.VectorSubcoreMesh(core_axis_name="c", subcore_axis_name="s",
                                  num_cores=SC_CORES, num_subcores=SC_SUBCORES)


def _sc_worker_id():
    return lax.axis_index("s") * SC_CORES + lax.axis_index("c")


def _sc_scatter_rows(x, idx, n_out):
    n, d = x.shape
    per_w = idx.shape[0] // SC_WORKERS
    nwin = per_w // SC_WINDOW
    assert per_w * SC_WORKERS == idx.shape[0] and nwin * SC_WINDOW == per_w and nwin % 2 == 0 and n % per_w == 0

    def body(x_hbm, idx_hbm, out_hbm, idx_v, rows_v, sem_r, sem_w):
        wid = _sc_worker_id()
        row0 = lax.rem(wid * per_w, n)
        pltpu.sync_copy(idx_hbm.at[wid], idx_v)

        def read(j, b):
            return pltpu.make_async_copy(x_hbm.at[pl.ds(row0 + j * SC_WINDOW, SC_WINDOW)], rows_v.at[b], sem_r.at[b])

        def write(j, b):
            return pltpu.make_async_copy(rows_v.at[b], out_hbm.at[idx_v.at[j]], sem_w.at[b])

        @pl.loop(0, nwin, step=2)
        def _(j):
            read(j, 0).start()
            read(j + 1, 1).start()
            read(j, 0).wait()
            write(j, 0).start()
            read(j + 1, 1).wait()
            write(j + 1, 1).start()
            write(j, 0).wait()
            write(j + 1, 1).wait()

    return pl.kernel(
        body, out_type=jax.ShapeDtypeStruct((n_out, d), x.dtype), mesh=_sc_mesh(),
        scratch_types=[pltpu.VMEM((nwin, SC_WINDOW), I32), pltpu.VMEM((2, SC_WINDOW, d), x.dtype),
                       pltpu.SemaphoreType.DMA((2,)), pltpu.SemaphoreType.DMA((2,))],
        name="sc_scatter_rows",
    )(x, idx.reshape(SC_WORKERS, nwin, SC_WINDOW))


def _sc_gather_rows(y, idx):
    d = y.shape[1]
    total = idx.shape[0]
    per_w = total // SC_WORKERS
    nwin = per_w // SC_WINDOW
    assert per_w * SC_WORKERS == total and nwin * SC_WINDOW == per_w and nwin % 2 == 0

    def body(y_hbm, idx_hbm, out_hbm, idx_v, rows_v, sem_r, sem_w):
        wid = _sc_worker_id()
        row0 = wid * per_w
        pltpu.sync_copy(idx_hbm.at[wid], idx_v)

        def read(j, b):
            return pltpu.make_async_copy(y_hbm.at[idx_v.at[j]], rows_v.at[b], sem_r.at[b])

        def write(j, b):
            return pltpu.make_async_copy(rows_v.at[b], out_hbm.at[pl.ds(row0 + j * SC_WINDOW, SC_WINDOW)], sem_w.at[b])

        @pl.loop(0, nwin, step=2)
        def _(j):
            read(j, 0).start()
            read(j + 1, 1).start()
            read(j, 0).wait()
            write(j, 0).start()
            read(j + 1, 1).wait()
            write(j + 1, 1).start()
            write(j, 0).wait()
            write(j + 1, 1).wait()

    return pl.kernel(
        body, out_type=jax.ShapeDtypeStruct((total, d), y.dtype), mesh=_sc_mesh(),
        scratch_types=[pltpu.VMEM((nwin, SC_WINDOW), I32), pltpu.VMEM((2, SC_WINDOW, d), y.dtype),
                       pltpu.SemaphoreType.DMA((2,)), pltpu.SemaphoreType.DMA((2,))],
        name="sc_gather_rows",
    )(y, idx.reshape(SC_WORKERS, nwin, SC_WINDOW))


def _expert_kernel(be_ref, nv_ref, x_ref, wg_ref, wu_ref, wd_ref, bg_ref, bu_ref, bd_ref, y_ref, wgb, wub, wdb):
    i = pl.program_id(0)

    @pl.when(i < nv_ref[0])
    def _():
        changed = jnp.logical_or(i == 0, be_ref[i] != be_ref[jnp.maximum(i - 1, 0)])

        @pl.when(changed)
        def _():
            wgb[...] = wg_ref[0].astype(BF16)
            wub[...] = wu_ref[0].astype(BF16)
            wdb[...] = wd_ref[0].astype(BF16)

        x = _unpack_bf16_pairs(x_ref[...]).astype(BF16)
        gate = jnp.minimum(_mm(x, wgb[...]) + bg_ref[0], SWIGLU_LIMIT)
        up = jnp.clip(_mm(x, wub[...]) + bu_ref[0], -SWIGLU_LIMIT, SWIGLU_LIMIT)
        act = (up + 1.0) * gate * _sigmoid(SWIGLU_ALPHA * gate)
        y_ref[...] = _pack_bf16_pairs(_mm(act, wdb[...]) + bd_ref[0])


def _expert_call(block_e, n_valid, xs, w_gate, w_up, w_down, b_gate, b_up, b_down):
    nb = block_e.shape[0]
    tmb = EXPERT_BLOCK
    ne, d, de = w_gate.shape
    wspec = lambda s: pl.BlockSpec((1,) + s, lambda i, be, nv: (be[i], 0, 0))
    grid_spec = pltpu.PrefetchScalarGridSpec(
        num_scalar_prefetch=2,
        grid=(nb,),
        in_specs=[pl.BlockSpec((tmb, d // 2), lambda i, be, nv: (i, 0)),
                  wspec((d, de)), wspec((d, de)), wspec((de, d)),
                  wspec((1, de)), wspec((1, de)), wspec((1, d))],
        out_specs=pl.BlockSpec((tmb, d // 2), lambda i, be, nv: (i, 0)),
        scratch_shapes=[pltpu.VMEM((d, de), BF16), pltpu.VMEM((d, de), BF16), pltpu.VMEM((de, d), BF16)],
    )
    return pl.pallas_call(
        _expert_kernel,
        grid_spec=grid_spec,
        out_shape=jax.ShapeDtypeStruct(xs.shape, xs.dtype),
        compiler_params=_cparams(("arbitrary",), 56),
        name="expert",
    )(block_e, n_valid, xs, w_gate, w_up, w_down,
      b_gate.reshape(ne, 1, de), b_up.reshape(ne, 1, de), b_down.reshape(ne, 1, d))


def _combine_kernel(y0, y1, y2, y3, w_ref, x1_ref, g2_ref, fg_ref, o_ref):
    w = w_ref[...]
    ys = [_unpack_bf16_pairs(y[...]) for y in (y0, y1, y2, y3)]
    moe = (w[:, 0:1] * ys[0] + w[:, 1:2] * ys[1]) + (w[:, 2:3] * ys[2] + w[:, 3:4] * ys[3])
    o_ref[...] = _rmsnorm(x1_ref[...] + g2_ref[0] * moe, fg_ref[...])


def _combine_call(y4, top_w, x1, g2, fg, tokens_per_batch, tm):
    n, d = x1.shape
    per = tokens_per_batch // tm
    nt = n // tm
    yspec = lambda k: pl.BlockSpec((tm, d // 2), lambda i: (k * nt + i, 0))
    return pl.pallas_call(
        _combine_kernel,
        grid=(nt,),
        in_specs=[yspec(0), yspec(1), yspec(2), yspec(3),
                  pl.BlockSpec((tm, LANES), lambda i: (i, 0)),
                  pl.BlockSpec((tm, d), lambda i: (i, 0)),
                  pl.BlockSpec((1, 1, d), lambda i: (i // per, 0, 0)),
                  pl.BlockSpec((1, d), lambda i: (0, 0))],
        out_specs=pl.BlockSpec((tm, d), lambda i: (i, 0)),
        out_shape=jax.ShapeDtypeStruct((n, d), F32),
        compiler_params=_cparams(("parallel",), 48),
        name="combine",
    )(y4, y4, y4, y4, top_w, x1, g2, fg)


def _routing_tables(top_idx, rank, counts, n):
    tmb = EXPERT_BLOCK
    counts = counts.astype(I32)
    padded = (counts + tmb - 1) // tmb * tmb
    pad_end = jnp.cumsum(padded)
    pad_start = pad_end - padded
    n_blocks = -(-(n * TOP_K + N_EXPERTS * (tmb - 1)) // tmb)
    n_slots = n_blocks * tmb
    onehot = top_idx[:, :, None] == jnp.arange(N_EXPERTS, dtype=I32)[None, None, :]
    dest = jnp.sum(jnp.where(onehot, pad_start[None, None, :], 0), axis=-1) + rank
    dest_flat = jnp.transpose(dest).reshape(-1)
    block_start = jnp.arange(n_blocks, dtype=I32) * tmb
    block_e = jnp.minimum(jnp.sum((pad_end[None, :] <= block_start[:, None]).astype(I32), axis=1), N_EXPERTS - 1)
    n_valid = (pad_end[-1:] // tmb).astype(I32)
    return block_e, n_valid, dest_flat, n_slots


def _gdn_branch(xtok, sh, sc, norm_g, w_cols, widths, dtypes, conv_w, par, tokens_per_batch, grid_w, use_rows,
                tm_proj, tm_conv, s0, prec):
    b = xtok.shape[0] // tokens_per_batch
    outs = _inproj_call(xtok, sh, sc, norm_g, w_cols, widths, dtypes, tokens_per_batch, tm_proj)
    qkv, gates = outs[0], outs[-1]
    q, k, v, go = _conv_call(qkv.reshape(b, tokens_per_batch, QKV_DIM), conv_w,
                             gates.reshape(b, tokens_per_batch, LANES), par, grid_w, use_rows, tm_conv)
    o_f, o_b, s_fin = _gdn_call(q, k, v, go, _gdn_decay_rows(go), s0, prec)
    return outs, o_f, o_b, s_fin


def kernel(x, c, ctx, c_ctx, w_mod, b_mod, norm1_g, norm2_g, w_in, conv_w, a_log, dt_bias, gdn_norm_g,
           w_fourier_out, w_gdn_out, w_merge_out, w_router, b_router, w_gate, b_gate, w_up, b_up,
           w_down, b_down, final_norm_g):
    b, l, d = x.shape
    n = b * l
    n_ctx = ctx.shape[1]
    assert w_mod.shape[0] == 1 and l == GRID_W * GRID_W and d == V_DIM
    prec = None

    c8 = jnp.concatenate([c, c_ctx[None, :], jnp.zeros((8 - b - 1, d), F32)], axis=0)
    mod = _mod_call(c8, w_mod[0], b_mod[0])
    sh1, sc1, g1, sh2, sc2, g2 = [mod[:b, j * d:(j + 1) * d].reshape(b, 1, d) for j in range(6)]
    csh1 = jnp.broadcast_to(mod[b:b + 1, 0:d].reshape(1, 1, d), (b, 1, d))
    csc1 = jnp.broadcast_to(mod[b:b + 1, d:2 * d].reshape(1, 1, d), (b, 1, d))

    wi = w_in[0]
    off_gate = QKV_DIM
    off_z = off_gate + 4 * NV_HEADS
    off_f = off_z + V_DIM
    off_ga = off_f + F_DIM
    gate_cols = jnp.pad(wi[:, off_gate:off_z], ((0, 0), (0, LANES - 4 * NV_HEADS)))
    w_lat = jnp.concatenate([wi[:, :QKV_DIM], wi[:, off_z:off_f], wi[:, off_f:off_ga], wi[:, off_ga:], gate_cols],
                            axis=1).astype(BF16)
    w_ctx = jnp.concatenate([wi[:, :QKV_DIM], gate_cols], axis=1).astype(BF16)
    par = jnp.pad(jnp.stack([a_log[0].reshape(-1), dt_bias[0].reshape(-1)]),
                  ((0, 6), (GATE_LANE0, LANES - 2 * GATE_LANE0)))
    n1 = norm1_g[0].reshape(1, d)
    cw = conv_w[0].reshape(9, QKV_DIM)

    zero_state = jnp.zeros((b, 2, NV_HEADS, HEAD_DIM, HEAD_DIM), F32)
    _, _, _, s_ctx = _gdn_branch(ctx.reshape(b * n_ctx, d), csh1, csc1, n1, w_ctx, (QKV_DIM, LANES), (BF16, F32),
                                 cw, par, n_ctx, n_ctx, False, n_ctx, n_ctx, zero_state, prec)

    x2 = x.reshape(n, d)
    outs, o_f, o_b, _ = _gdn_branch(x2, sh1, sc1, n1, w_lat, (QKV_DIM, V_DIM, F_DIM, 2 * d, LANES),
                                    (BF16, BF16, F32, BF16, F32), cw, par, l, GRID_W, True, 512, 512, s_ctx, prec)
    _, z, f, gab, _ = outs
    fmix = _fnet_call(f.reshape(b, l, F_DIM)).reshape(n, F_DIM)

    wr = jnp.pad(w_router[0], ((0, 0), (0, LANES - N_EXPERTS)))
    br = jnp.pad(b_router[0], (0, LANES - N_EXPERTS), constant_values=NEG_BIG).reshape(1, LANES)
    x1, h2, logits = _merge_call(
        o_f.reshape(n, V_DIM), o_b.reshape(n, V_DIM), z, fmix, gab, x2, g1, sh2, sc2,
        gdn_norm_g[0].reshape(1, HEAD_DIM), norm2_g[0].reshape(1, d),
        w_gdn_out[0].astype(BF16), w_fourier_out[0].astype(BF16), w_merge_out[0].astype(BF16), wr, br, l, 256)

    top_idx, top_w, rank, counts = _route_call(logits, 512)
    block_e, n_valid, dest_flat, n_slots = _routing_tables(top_idx[:, :TOP_K], rank[:, :TOP_K],
                                                           counts[0, :N_EXPERTS], n)
    xs = _sc_scatter_rows(h2, dest_flat, n_slots)
    ys = _expert_call(block_e, n_valid, xs, w_gate[0], w_up[0], w_down[0], b_gate[0], b_up[0], b_down[0])
    y4 = _sc_gather_rows(ys, dest_flat)
    out = _combine_call(y4, top_w, x1, g2, final_norm_g.reshape(1, d), l, 256)
    return out.reshape(b, l, d)
```

```python
import functools
import math

import jax
import jax.numpy as jnp
import numpy as np
from jax import lax
from jax.experimental import pallas as pl
from jax.experimental.pallas import tpu as pltpu
from jax.experimental.pallas import tpu_sc as plsc

F32 = jnp.float32
BF16 = jnp.bfloat16
I32 = jnp.int32
HIGHEST = lax.Precision.HIGHEST

GRID_W = 64
NQK_HEADS = 4
NV_HEADS = 8
HEAD_DIM = 128
QK_DIM = NQK_HEADS * HEAD_DIM
V_DIM = NV_HEADS * HEAD_DIM
QKV_DIM = 2 * QK_DIM + V_DIM
F_GROUPS = 4
F_DIM = F_GROUPS * HEAD_DIM
N_EXPERTS = 32
TOP_K = 4
SWIGLU_ALPHA = 1.702
SWIGLU_LIMIT = 7.0
EPS = 1e-6

LANES = 128
SUBLANES = 8
GATE_LANE0 = 16
GDN_CHUNK = 128
EXPERT_BLOCK = 512
MOE_PARTS = 2
SC_CORES = 2
SC_SUBCORES = 16
SC_WORKERS = SC_CORES * SC_SUBCORES
SC_WINDOW = 64
NEG_BIG = -1e30
MIB = 2 ** 20


def _cparams(sem, vmem_mib):
    return pltpu.CompilerParams(dimension_semantics=sem, vmem_limit_bytes=vmem_mib * MIB)


def _mm(a, b, prec=None, dims=(((1,), (0,)), ((), ()))):
    if prec is None:
        return lax.dot_general(a.astype(BF16), b.astype(BF16), dims, preferred_element_type=F32)
    return lax.dot_general(a.astype(F32), b.astype(F32), dims, precision=prec, preferred_element_type=F32)


def _sigmoid(x):
    return 1.0 / (1.0 + jnp.exp(-x))


def _rmsnorm(x, g):
    return x * lax.rsqrt(jnp.mean(x * x, axis=-1, keepdims=True) + EPS) * g


def _pack_bf16_pairs(x):
    half = x.shape[1] // 2
    bits = lax.bitcast_convert_type(x.astype(BF16).astype(F32), jnp.uint32)
    packed = jnp.bitwise_or(jnp.right_shift(bits[:, :half], jnp.uint32(16)),
                            jnp.bitwise_and(bits[:, half:], jnp.uint32(0xFFFF0000)))
    return lax.bitcast_convert_type(packed, I32)


def _unpack_bf16_pairs(p):
    bits = lax.bitcast_convert_type(p, jnp.uint32)
    lo = lax.bitcast_convert_type(jnp.left_shift(bits, jnp.uint32(16)), F32)
    hi = lax.bitcast_convert_type(jnp.bitwise_and(bits, jnp.uint32(0xFFFF0000)), F32)
    return jnp.concatenate([lo, hi], axis=1)


def _mod_kernel(c_ref, w_ref, b_ref, o_ref):
    c = c_ref[...]
    o_ref[...] = _mm(c * _sigmoid(c), w_ref[...], HIGHEST) + b_ref[...]


def _mod_call(c8, w_mod, b_mod):
    d, n = w_mod.shape
    tn = 1536
    return pl.pallas_call(
        _mod_kernel,
        grid=(n // tn,),
        in_specs=[pl.BlockSpec((8, d), lambda j: (0, 0)),
                  pl.BlockSpec((d, tn), lambda j: (0, j)),
                  pl.BlockSpec((1, tn), lambda j: (0, j))],
        out_specs=pl.BlockSpec((8, tn), lambda j: (0, j)),
        out_shape=jax.ShapeDtypeStruct((8, n), F32),
        compiler_params=_cparams(("parallel",), 32),
        name="mod",
    )(c8, w_mod, b_mod.reshape(1, n))


def _inproj_kernel(x_ref, sh_ref, sc_ref, g_ref, w_ref, *o_refs, widths):
    u = (_rmsnorm(x_ref[...], g_ref[...]) * (1.0 + sc_ref[0]) + sh_ref[0]).astype(BF16)
    start = 0
    for o_ref, width in zip(o_refs, widths):
        step = min(width, 512)
        for c0 in range(0, width, step):
            o_ref[:, c0:c0 + step] = _mm(u, w_ref[:, start + c0:start + c0 + step]).astype(o_ref.dtype)
        start += width


def _inproj_call(x2, sh, sc, g, w, widths, dtypes, tokens_per_batch, tm):
    n, d = x2.shape
    per = tokens_per_batch // tm
    out_shape = [jax.ShapeDtypeStruct((n, wd), dt) for wd, dt in zip(widths, dtypes)]
    return pl.pallas_call(
        functools.partial(_inproj_kernel, widths=widths),
        grid=(n // tm,),
        in_specs=[pl.BlockSpec((tm, d), lambda i: (i, 0)),
                  pl.BlockSpec((1, 1, d), lambda i: (i // per, 0, 0)),
                  pl.BlockSpec((1, 1, d), lambda i: (i // per, 0, 0)),
                  pl.BlockSpec((1, d), lambda i: (0, 0)),
                  pl.BlockSpec(w.shape, lambda i: (0, 0))],
        out_specs=[pl.BlockSpec((tm, wd), lambda i: (i, 0)) for wd in widths],
        out_shape=out_shape,
        compiler_params=_cparams(("parallel",), 56),
        name="inproj",
    )(x2, sh, sc, g, w)


def _conv_kernel(*refs, grid_w, use_rows, tm, cw):
    if use_rows:
        prev_ref, main_ref, next_ref, w_ref, gates_ref, par_ref, q_ref, k_ref, v_ref, go_ref = refs
    else:
        main_ref, w_ref, gates_ref, par_ref, q_ref, k_ref, v_ref, go_ref = refs
    r = pl.program_id(1)
    nr = pl.num_programs(1)
    t = lax.broadcasted_iota(I32, (tm, 1), 0)
    col = jnp.bitwise_and(t, grid_w - 1)
    m_left = (col != 0).astype(F32)
    m_right = (col != grid_w - 1).astype(F32)
    has_prev = (r > 0).astype(F32)
    has_next = (r < nr - 1).astype(F32)
    for c0 in range(0, QKV_DIM, cw):
        xm = main_ref[0, :, c0:c0 + cw].astype(F32)
        if use_rows:
            xp = prev_ref[0, :, c0:c0 + cw].astype(F32) * has_prev
            xn = next_ref[0, :, c0:c0 + cw].astype(F32) * has_next
            up = jnp.concatenate([xp, xm[:tm - grid_w]], axis=0)
            dn = jnp.concatenate([xm[grid_w:], xn], axis=0)

        def colsum(kc):
            y = xm * w_ref[3 + kc:4 + kc, c0:c0 + cw]
            if use_rows:
                y = y + up * w_ref[kc:kc + 1, c0:c0 + cw] + dn * w_ref[6 + kc:7 + kc, c0:c0 + cw]
            return y

        acc = (colsum(1) + pltpu.roll(colsum(0), 1, axis=0) * m_left
               + pltpu.roll(colsum(2), tm - 1, axis=0) * m_right)
        s = acc * _sigmoid(acc)
        for h0 in range(0, cw, HEAD_DIM):
            c = c0 + h0
            seg = s[:, h0:h0 + HEAD_DIM]
            if c < 2 * QK_DIM:
                seg = seg * lax.rsqrt(jnp.sum(seg * seg, axis=-1, keepdims=True) + EPS)
            if c < QK_DIM:
                q_ref[0, :, c:c + HEAD_DIM] = (seg * HEAD_DIM ** -0.5).astype(q_ref.dtype)
            elif c < 2 * QK_DIM:
                k_ref[0, :, c - QK_DIM:c - QK_DIM + HEAD_DIM] = seg.astype(k_ref.dtype)
            else:
                v_ref[0, :, c - 2 * QK_DIM:c - 2 * QK_DIM + HEAD_DIM] = seg.astype(v_ref.dtype)
    g = gates_ref[0]
    a = g + par_ref[1:2, :]
    softplus = jnp.maximum(a, 0.0) + jnp.log1p(jnp.exp(-jnp.abs(a)))
    log_g = -jnp.exp(par_ref[0:1, :]) * softplus
    lane = lax.broadcasted_iota(I32, g.shape, 1)
    go_ref[0] = jnp.where(lane < GATE_LANE0, _sigmoid(g), log_g)


def _conv_call(qkv, conv_w, gates, par, grid_w, use_rows, tm):
    b, t, c = qkv.shape
    kern = functools.partial(_conv_kernel, grid_w=grid_w, use_rows=use_rows, tm=tm, cw=512)
    per = tm // grid_w
    nrow = t // grid_w
    in_specs = []
    args = []
    if use_rows:
        in_specs.append(pl.BlockSpec((1, grid_w, c), lambda i, r: (i, jnp.maximum(r * per - 1, 0), 0)))
        args.append(qkv)
    in_specs.append(pl.BlockSpec((1, tm, c), lambda i, r: (i, r, 0)))
    args.append(qkv)
    if use_rows:
        in_specs.append(pl.BlockSpec((1, grid_w, c), lambda i, r: (i, jnp.minimum((r + 1) * per, nrow - 1), 0)))
        args.append(qkv)
    in_specs += [pl.BlockSpec(conv_w.shape, lambda i, r: (0, 0)),
                 pl.BlockSpec((1, tm, LANES), lambda i, r: (i, r, 0)),
                 pl.BlockSpec(par.shape, lambda i, r: (0, 0))]
    args += [conv_w, gates, par]
    out_shape = [jax.ShapeDtypeStruct((b, t, QK_DIM), BF16), jax.ShapeDtypeStruct((b, t, QK_DIM), BF16),
                 jax.ShapeDtypeStruct((b, t, V_DIM), BF16), jax.ShapeDtypeStruct((b, t, LANES), F32)]
    out_specs = [pl.BlockSpec((1, tm, QK_DIM), lambda i, r: (i, r, 0)),
                 pl.BlockSpec((1, tm, QK_DIM), lambda i, r: (i, r, 0)),
                 pl.BlockSpec((1, tm, V_DIM), lambda i, r: (i, r, 0)),
                 pl.BlockSpec((1, tm, LANES), lambda i, r: (i, r, 0))]
    return pl.pallas_call(
        kern, grid=(b, t // tm), in_specs=in_specs, out_specs=out_specs, out_shape=out_shape,
        compiler_params=_cparams(("parallel", "parallel"), 48),
        name="conv_rows" if use_rows else "conv_seq",
    )(*args)


def _gdn_kernel(qf, kf, vf, gf, rf, qb, kb, vb, gb, rb, s0_ref, of, ob, sfin_ref, s_ref, *, prec):
    i = pl.program_id(1)
    nc = pl.num_programs(1)

    @pl.when(i == 0)
    def _():
        s_ref[...] = s0_ref[0]

    c = qf.shape[1]
    per = NV_HEADS // NQK_HEADS
    row = lax.broadcasted_iota(I32, (c, c), 0)
    colj = lax.broadcasted_iota(I32, (c, c), 1)
    eye = jnp.where(row == colj, 1.0, 0.0)
    nt_dims = (((1,), (1,)), ((), ()))
    tn_dims = (((0,), (0,)), ((), ()))

    seqs = []
    for d, (q_r, k_r, v_r, g_r, r_r, o_r) in enumerate(((qf, kf, vf, gf, rf, of), (qb, kb, vb, gb, rb, ob))):
        rev = d == 1
        incl = (colj >= row) if rev else (colj <= row)
        strict = (colj > row) if rev else (colj < row)
        gates = g_r[0]
        tri_c = jnp.where(incl, 1.0, 0.0)
        gcm = _mm(tri_c, gates, HIGHEST)
        gcr = _mm(r_r[0, 0], tri_c, HIGHEST, dims=nt_dims)
        for hq in range(NQK_HEADS):
            q = q_r[0, :, hq * HEAD_DIM:(hq + 1) * HEAD_DIM]
            k = k_r[0, :, hq * HEAD_DIM:(hq + 1) * HEAD_DIM]
            kq = lax.dot_general(jnp.concatenate([q, k], axis=0), k, nt_dims, preferred_element_type=F32)
            for j in range(per):
                h = hq * per + j
                idx = d * NV_HEADS + h
                gc_c = gcm[:, GATE_LANE0 + idx:GATE_LANE0 + idx + 1]
                seqs.append(dict(d=d, h=h, o_r=o_r, v_r=v_r, q=q, k=k, qk=kq[:c], kk=kq[c:], incl=incl, strict=strict,
                                 beta=gates[:, idx:idx + 1], gc_c=gc_c, gc_r=gcr[idx:idx + 1, :],
                                 ge=gc_c[0:1] if rev else gc_c[c - 1:c]))

    def same_block(m):
        sh = int(math.log2(m))
        return jnp.right_shift(row, sh) == jnp.right_shift(colj, sh)

    for s in seqs:
        s['decay'] = jnp.where(s['incl'], jnp.exp(jnp.where(s['incl'], s['gc_c'] - s['gc_r'], 0.0)), 0.0)
        s['a'] = jnp.where(s['strict'], s['beta'] * s['kk'] * s['decay'], 0.0)
        s['t'] = eye - jnp.where(same_block(2), s['a'], 0.0)
    m = 4
    while m <= c:
        between = jnp.logical_and(same_block(m), jnp.logical_not(same_block(m // 2)))
        for s in seqs:
            s['te'] = _mm(s['t'], jnp.where(between, s['a'], 0.0), prec)
        for s in seqs:
            s['t'] = s['t'] - _mm(s['te'], s['t'], prec)
        m *= 2
    for s in seqs:
        h = s['h']
        egc = jnp.exp(s['gc_c'])
        kf32 = s['k'].astype(F32)
        v = s['v_r'][0, :, h * HEAD_DIM:(h + 1) * HEAD_DIM].astype(F32)
        rhs = jnp.concatenate([s['beta'] * v, (s['beta'] * egc) * kf32], axis=1)
        s['sol'] = _mm(s['t'], rhs, prec)
        s['q_dec'] = s['q'].astype(F32) * egc
        s['k_dec'] = kf32 * jnp.exp(s['ge'] - s['gc_c'])
    for s in seqs:
        s['ws'] = _mm(jnp.concatenate([s['sol'][:, HEAD_DIM:], s['q_dec']], axis=0), s_ref[s['d'], s['h']])
    for s in seqs:
        s['u'] = s['sol'][:, :HEAD_DIM] - s['ws'][:c]
        s_ref[s['d'], s['h']] = (jnp.exp(s['ge']) * s_ref[s['d'], s['h']]
                                 + _mm(s['k_dec'], s['u'], dims=tn_dims))
    for s in seqs:
        h = s['h']
        o = s['ws'][c:] + _mm(s['qk'] * s['decay'], s['u'])
        s['o_r'][0, :, h * HEAD_DIM:(h + 1) * HEAD_DIM] = o.astype(s['o_r'].dtype)

    @pl.when(i == nc - 1)
    def _():
        sfin_ref[0] = s_ref[...]


def _gdn_call(q, k, v, go, rows, s0, prec):
    b, t, _ = q.shape
    c = GDN_CHUNK
    nc = t // c
    fwd = lambda i, n: (i, n, 0)
    bwd = lambda i, n: (i, nc - 1 - n, 0)
    rfwd = lambda i, n: (i, n, 0, 0)
    rbwd = lambda i, n: (i, nc - 1 - n, 0, 0)
    state_spec = pl.BlockSpec((1,) + s0.shape[1:], lambda i, n: (i, 0, 0, 0, 0))

    def specs(m3, mr):
        return [pl.BlockSpec((1, c, QK_DIM), m3), pl.BlockSpec((1, c, QK_DIM), m3),
                pl.BlockSpec((1, c, V_DIM), m3), pl.BlockSpec((1, c, LANES), m3),
                pl.BlockSpec((1, 1, 2 * NV_HEADS, c), mr)]

    return pl.pallas_call(
        functools.partial(_gdn_kernel, prec=prec),
        grid=(b, nc),
        in_specs=specs(fwd, rfwd) + specs(bwd, rbwd) + [state_spec],
        out_specs=[pl.BlockSpec((1, c, V_DIM), fwd), pl.BlockSpec((1, c, V_DIM), bwd), state_spec],
        out_shape=[jax.ShapeDtypeStruct((b, t, V_DIM), BF16), jax.ShapeDtypeStruct((b, t, V_DIM), BF16),
                   jax.ShapeDtypeStruct(s0.shape, F32)],
        scratch_shapes=[pltpu.VMEM(s0.shape[1:], F32)],
        compiler_params=_cparams(("parallel", "arbitrary"), 48),
        name="gdn",
    )(q, k, v, go, rows, q, k, v, go, rows, s0)


def _gdn_decay_rows(go):
    b, t, _ = go.shape
    c = GDN_CHUNK
    lg = go[..., GATE_LANE0:GATE_LANE0 + 2 * NV_HEADS]
    return jnp.transpose(lg.reshape(b, t // c, c, 2 * NV_HEADS), (0, 1, 3, 2))


def _fnet1_kernel(x_ref, f_ref, ar_ref, ai_ref):
    n = x_ref.shape[1]
    for j in range(x_ref.shape[2]):
        a = _mm(f_ref[...], x_ref[0, :, j, :])
        ar_ref[0, j] = a[:n]
        ai_ref[0, j] = a[n:]


def _fnet2_kernel(ar_ref, ai_ref, g_ref, wc_ref, o_ref, *, scale):
    n = ar_ref.shape[1]
    zs = []
    for m in range(ar_ref.shape[2]):
        a2 = jnp.concatenate([ar_ref[0, :, m, :], ai_ref[0, :, m, :]], axis=0)
        zs.append(_mm(g_ref[m], a2))
    for m, z in enumerate(zs):
        y = _mm(jnp.concatenate([z[:n], z[n:]], axis=1), wc_ref[...])
        o_ref[0, :, m, :] = y * scale


def _fnet_tables(n, groups):
    a = np.arange(n)
    ang1 = 2.0 * np.pi * np.outer(a, a) / n
    f1 = np.concatenate([np.cos(ang1), -np.sin(ang1)], axis=0)
    m = a[:, None] + n * a[None, :]
    ang2 = 2.0 * np.pi * ((m[:, :, None] * a[None, None, :]) % (n * n)) / (n * n)
    gc, gs = np.cos(ang2), np.sin(ang2)
    g2 = np.concatenate([np.concatenate([gc, gs], axis=2), np.concatenate([-gs, gc], axis=2)], axis=1)
    angc = 2.0 * np.pi * np.outer(np.arange(HEAD_DIM), np.arange(HEAD_DIM)) / HEAD_DIM
    eye = np.eye(groups)
    wc = np.concatenate([np.kron(eye, np.cos(angc)), np.kron(eye, np.sin(angc))], axis=0)
    f = lambda x: jnp.asarray(x, F32).astype(BF16)
    return f(f1), f(g2), f(wc)


def _fnet_call(f):
    b, l, c = f.shape
    n = GRID_W
    assert l == n * n
    f1, g2, wc = _fnet_tables(n, c // HEAD_DIM)
    cols = SUBLANES
    ar, ai = pl.pallas_call(
        _fnet1_kernel,
        grid=(b, n // cols),
        in_specs=[pl.BlockSpec((1, n, cols, c), lambda i, j: (i, 0, j, 0)),
                  pl.BlockSpec((2 * n, n), lambda i, j: (0, 0))],
        out_specs=[pl.BlockSpec((1, cols, n, c), lambda i, j: (i, j, 0, 0))] * 2,
        out_shape=[jax.ShapeDtypeStruct((b, n, n, c), F32)] * 2,
        compiler_params=_cparams(("parallel", "parallel"), 32),
        name="fnet1",
    )(f.reshape(b, n, n, c), f1)
    out = pl.pallas_call(
        functools.partial(_fnet2_kernel, scale=1.0 / math.sqrt(l * HEAD_DIM)),
        grid=(b, n // cols),
        in_specs=[pl.BlockSpec((1, n, cols, c), lambda i, j: (i, 0, j, 0)),
                  pl.BlockSpec((1, n, cols, c), lambda i, j: (i, 0, j, 0)),
                  pl.BlockSpec((cols, 2 * n, 2 * n), lambda i, j: (j, 0, 0)),
                  pl.BlockSpec((2 * c, c), lambda i, j: (0, 0))],
        out_specs=pl.BlockSpec((1, n, cols, c), lambda i, j: (i, 0, j, 0)),
        out_shape=jax.ShapeDtypeStruct((b, n, n, c), F32),
        compiler_params=_cparams(("parallel", "parallel"), 32),
        name="fnet2",
    )(ar, ai, g2, wc)
    return out.reshape(b, l, c)


def _merge_kernel(of_ref, ob_ref, z_ref, fm_ref, gab_ref, x_ref, g1_ref, sh2_ref, sc2_ref, gn_ref, n2_ref,
                  wg_ref, wf_ref, wm_ref, wr_ref, br_ref, x1_ref, h2_ref, lg_ref):
    d = x_ref.shape[1]
    o = of_ref[...].astype(F32) + ob_ref[...].astype(F32)
    z = z_ref[...].astype(F32)
    parts = []
    for h0 in range(0, V_DIM, HEAD_DIM):
        oh = o[:, h0:h0 + HEAD_DIM]
        parts.append(oh * lax.rsqrt(jnp.mean(oh * oh, axis=-1, keepdims=True) + EPS) * gn_ref[...])
    yb_in = jnp.concatenate(parts, axis=1) * (z * _sigmoid(z))
    yb = _mm(yb_in, wg_ref[...])
    ya = _mm(fm_ref[...], wf_ref[...])
    ga = _sigmoid(gab_ref[:, :d].astype(F32))
    gb = _sigmoid(gab_ref[:, d:].astype(F32))
    mm = _mm(ga * ya + gb * yb, wm_ref[...])
    x1 = x_ref[...] + g1_ref[0] * mm
    x1_ref[...] = x1
    h2 = _rmsnorm(x1, n2_ref[...]) * (1.0 + sc2_ref[0]) + sh2_ref[0]
    h2_ref[...] = _pack_bf16_pairs(h2)
    h_hi = h2.astype(BF16)
    h_lo = (h2 - h_hi.astype(F32)).astype(BF16)
    w = wr_ref[...]
    w_hi = w.astype(BF16)
    w_lo = (w - w_hi.astype(F32)).astype(BF16)
    part = _mm(h_hi, jnp.concatenate([w_hi, w_lo], axis=1))
    lg_ref[...] = (part[:, :LANES] + part[:, LANES:]) + _mm(h_lo, w_hi) + br_ref[...]


def _merge_call(of, ob, z, fm, gab, x2, g1, sh2, sc2, gn, n2, wg, wf, wm, wr, br, tokens_per_batch, tm):
    n, d = x2.shape
    per = tokens_per_batch // tm
    tok = lambda wd: pl.BlockSpec((tm, wd), lambda i: (i, 0))
    vec = pl.BlockSpec((1, 1, d), lambda i: (i // per, 0, 0))
    full = lambda a: pl.BlockSpec(a.shape, lambda i: (0,) * a.ndim)
    return pl.pallas_call(
        _merge_kernel,
        grid=(n // tm,),
        in_specs=[tok(V_DIM), tok(V_DIM), tok(V_DIM), tok(F_DIM), tok(2 * d), tok(d), vec, vec, vec,
                  full(gn), full(n2), full(wg), full(wf), full(wm), full(wr), full(br)],
        out_specs=[tok(d), tok(d // 2), tok(LANES)],
        out_shape=[jax.ShapeDtypeStruct((n, d), F32), jax.ShapeDtypeStruct((n, d // 2), I32),
                   jax.ShapeDtypeStruct((n, LANES), F32)],
        compiler_params=_cparams(("parallel",), 56),
        name="merge",
    )(of, ob, z, fm, gab, x2, g1, sh2, sc2, gn, n2, wg, wf, wm, wr, br)


def _route_kernel(lg_ref, idx_ref, w_ref, rank_ref, cnt_ref, run_ref):
    i = pl.program_id(0)

    @pl.when(i == 0)
    def _():
        run_ref[...] = jnp.zeros_like(run_ref)

    l = lg_ref[...]
    tm = l.shape[0]
    lane = lax.broadcasted_iota(I32, l.shape, 1)
    vals, idxs = [], []
    for _ in range(TOP_K):
        m = jnp.max(l, axis=-1, keepdims=True)
        idx = jnp.min(jnp.where(l == m, lane, LANES), axis=-1, keepdims=True)
        vals.append(m)
        idxs.append(idx)
        l = jnp.where(lane == idx, NEG_BIG * 2.0, l)
    es = [jnp.exp(v - vals[0]) for v in vals]
    inv = 1.0 / (es[0] + es[1] + es[2] + es[3])
    picked = jnp.zeros(l.shape, F32)
    for idx in idxs:
        picked = picked + (lane == idx).astype(F32)
    r = lax.broadcasted_iota(I32, (tm, tm), 0)
    cidx = lax.broadcasted_iota(I32, (tm, tm), 1)
    before = _mm(jnp.where(cidx < r, 1.0, 0.0), picked) + run_ref[...]
    idx_out = jnp.zeros(l.shape, I32)
    w_out = jnp.zeros(l.shape, F32)
    rank_out = jnp.zeros(l.shape, F32)
    for k in range(TOP_K):
        rk = jnp.sum(jnp.where(lane == idxs[k], before, 0.0), axis=-1, keepdims=True)
        idx_out = jnp.where(lane == k, idxs[k], idx_out)
        w_out = jnp.where(lane == k, es[k] * inv, w_out)
        rank_out = jnp.where(lane == k, rk, rank_out)
    idx_ref[...] = idx_out
    w_ref[...] = w_out
    rank_ref[...] = rank_out.astype(I32)
    run_ref[...] = run_ref[...] + jnp.sum(picked, axis=0, keepdims=True)
    cnt_ref[...] = run_ref[...]


def _route_call(logits, tm, start, n):
    first = start // tm
    tok = pl.BlockSpec((tm, LANES), lambda i: (i, 0))
    return pl.pallas_call(
        _route_kernel,
        grid=(n // tm,),
        in_specs=[pl.BlockSpec((tm, LANES), lambda i: (first + i, 0))],
        out_specs=[tok, tok, tok, pl.BlockSpec((1, LANES), lambda i: (0, 0))],
        out_shape=[jax.ShapeDtypeStruct((n, LANES), I32), jax.ShapeDtypeStruct((n, LANES), F32),
                   jax.ShapeDtypeStruct((n, LANES), I32), jax.ShapeDtypeStruct((1, LANES), F32)],
        scratch_shapes=[pltpu.VMEM((1, LANES), F32)],
        compiler_params=_cparams(("arbitrary",), 32),
        name="route",
    )(logits)


def _sc_mesh():
    return plsc.VectorSubcoreMesh(core_axis_name="c", subcore_axis_name="s",
                                  num_cores=SC_CORES, num_subcores=SC_SUBCORES)


def _sc_worker_id():
    return lax.axis_index("s") * SC_CORES + lax.axis_index("c")


def _sc_scatter_rows(x, idx, n_out, start, n):
    d = x.shape[1]
    per_w = idx.shape[0] // SC_WORKERS
    nwin = per_w // SC_WINDOW
    assert per_w * SC_WORKERS == idx.shape[0] and nwin * SC_WINDOW == per_w and nwin % 2 == 0 and n % per_w == 0

    def body(x_hbm, idx_hbm, out_hbm, idx_v, rows_v, sem_r, sem_w):
        wid = _sc_worker_id()
        row0 = start + lax.rem(wid * per_w, n)
        pltpu.sync_copy(idx_hbm.at[wid], idx_v)

        def read(j, b):
            return pltpu.make_async_copy(x_hbm.at[pl.ds(row0 + j * SC_WINDOW, SC_WINDOW)], rows_v.at[b], sem_r.at[b])

        def write(j, b):
            return pltpu.make_async_copy(rows_v.at[b], out_hbm.at[idx_v.at[j]], sem_w.at[b])

        @pl.loop(0, nwin, step=2)
        def _(j):
            read(j, 0).start()
            read(j + 1, 1).start()
            read(j, 0).wait()
            write(j, 0).start()
            read(j + 1, 1).wait()
            write(j + 1, 1).start()
            write(j, 0).wait()
            write(j + 1, 1).wait()

    return pl.kernel(
        body, out_type=jax.ShapeDtypeStruct((n_out, d), x.dtype), mesh=_sc_mesh(),
        scratch_types=[pltpu.VMEM((nwin, SC_WINDOW), I32), pltpu.VMEM((2, SC_WINDOW, d), x.dtype),
                       pltpu.SemaphoreType.DMA((2,)), pltpu.SemaphoreType.DMA((2,))],
        name="sc_scatter_rows",
    )(x, idx.reshape(SC_WORKERS, nwin, SC_WINDOW))


def _sc_gather_rows(y, idx):
    d = y.shape[1]
    total = idx.shape[0]
    per_w = total // SC_WORKERS
    nwin = per_w // SC_WINDOW
    assert per_w * SC_WORKERS == total and nwin * SC_WINDOW == per_w and nwin % 2 == 0

    def body(y_hbm, idx_hbm, out_hbm, idx_v, rows_v, sem_r, sem_w):
        wid = _sc_worker_id()
        row0 = wid * per_w
        pltpu.sync_copy(idx_hbm.at[wid], idx_v)

        def read(j, b):
            return pltpu.make_async_copy(y_hbm.at[idx_v.at[j]], rows_v.at[b], sem_r.at[b])

        def write(j, b):
            return pltpu.make_async_copy(rows_v.at[b], out_hbm.at[pl.ds(row0 + j * SC_WINDOW, SC_WINDOW)], sem_w.at[b])

        @pl.loop(0, nwin, step=2)
        def _(j):
            read(j, 0).start()
            read(j + 1, 1).start()
            read(j, 0).wait()
            write(j, 0).start()
            read(j + 1, 1).wait()
            write(j + 1, 1).start()
            write(j, 0).wait()
            write(j + 1, 1).wait()

    return pl.kernel(
        body, out_type=jax.ShapeDtypeStruct((total, d), y.dtype), mesh=_sc_mesh(),
        scratch_types=[pltpu.VMEM((nwin, SC_WINDOW), I32), pltpu.VMEM((2, SC_WINDOW, d), y.dtype),
                       pltpu.SemaphoreType.DMA((2,)), pltpu.SemaphoreType.DMA((2,))],
        name="sc_gather_rows",
    )(y, idx.reshape(SC_WORKERS, nwin, SC_WINDOW))


def _expert_kernel(be_ref, nv_ref, x_ref, wg_ref, wu_ref, wd_ref, bg_ref, bu_ref, bd_ref, y_ref, wgb, wub, wdb):
    i = pl.program_id(0)

    @pl.when(i < nv_ref[0])
    def _():
        changed = jnp.logical_or(i == 0, be_ref[i] != be_ref[jnp.maximum(i - 1, 0)])

        @pl.when(changed)
        def _():
            wgb[...] = wg_ref[0].astype(BF16)
            wub[...] = wu_ref[0].astype(BF16)
            wdb[...] = wd_ref[0].astype(BF16)

        x = _unpack_bf16_pairs(x_ref[...]).astype(BF16)
        gate = jnp.minimum(_mm(x, wgb[...]) + bg_ref[0], SWIGLU_LIMIT)
        up = jnp.clip(_mm(x, wub[...]) + bu_ref[0], -SWIGLU_LIMIT, SWIGLU_LIMIT)
        act = (up + 1.0) * gate * _sigmoid(SWIGLU_ALPHA * gate)
        y_ref[...] = _pack_bf16_pairs(_mm(act, wdb[...]) + bd_ref[0])


def _expert_call(block_e, n_valid, xs, w_gate, w_up, w_down, b_gate, b_up, b_down):
    nb = block_e.shape[0]
    tmb = EXPERT_BLOCK
    ne, d, de = w_gate.shape
    wspec = lambda s: pl.BlockSpec((1,) + s, lambda i, be, nv: (be[i], 0, 0))
    grid_spec = pltpu.PrefetchScalarGridSpec(
        num_scalar_prefetch=2,
        grid=(nb,),
        in_specs=[pl.BlockSpec((tmb, d // 2), lambda i, be, nv: (i, 0)),
                  wspec((d, de)), wspec((d, de)), wspec((de, d)),
                  wspec((1, de)), wspec((1, de)), wspec((1, d))],
        out_specs=pl.BlockSpec((tmb, d // 2), lambda i, be, nv: (i, 0)),
        scratch_shapes=[pltpu.VMEM((d, de), BF16), pltpu.VMEM((d, de), BF16), pltpu.VMEM((de, d), BF16)],
    )
    return pl.pallas_call(
        _expert_kernel,
        grid_spec=grid_spec,
        out_shape=jax.ShapeDtypeStruct(xs.shape, xs.dtype),
        compiler_params=_cparams(("arbitrary",), 56),
        name="expert",
    )(block_e, n_valid, xs, w_gate, w_up, w_down,
      b_gate.reshape(ne, 1, de), b_up.reshape(ne, 1, de), b_down.reshape(ne, 1, d))


def _combine_kernel(y0, y1, y2, y3, w_ref, x1_ref, g2_ref, fg_ref, o_ref):
    w = w_ref[...]
    ys = [_unpack_bf16_pairs(y[...]) for y in (y0, y1, y2, y3)]
    moe = (w[:, 0:1] * ys[0] + w[:, 1:2] * ys[1]) + (w[:, 2:3] * ys[2] + w[:, 3:4] * ys[3])
    o_ref[...] = _rmsnorm(x1_ref[...] + g2_ref[0] * moe, fg_ref[...])


def _combine_call(prev, y4, top_w, x1, g2, fg, tokens_per_batch, tm, start):
    n, d = x1.shape
    per = tokens_per_batch // tm
    nt = top_w.shape[0] // tm
    first = start // tm
    yspec = lambda k: pl.BlockSpec((tm, d // 2), lambda i: (k * nt + i, 0))
    in_specs = [yspec(0), yspec(1), yspec(2), yspec(3),
                pl.BlockSpec((tm, LANES), lambda i: (i, 0)),
                pl.BlockSpec((tm, d), lambda i: (first + i, 0)),
                pl.BlockSpec((1, 1, d), lambda i: ((first + i) // per, 0, 0)),
                pl.BlockSpec((1, d), lambda i: (0, 0))]
    args = [y4, y4, y4, y4, top_w, x1, g2, fg]
    kern, aliases = _combine_kernel, {}
    if prev is not None:
        in_specs.append(pl.BlockSpec(memory_space=pl.ANY))
        args.append(prev)
        kern = lambda *refs: _combine_kernel(*refs[:8], refs[9])
        aliases = {8: 0}
    return pl.pallas_call(
        kern,
        grid=(nt,),
        in_specs=in_specs,
        out_specs=pl.BlockSpec((tm, d), lambda i: (first + i, 0)),
        out_shape=jax.ShapeDtypeStruct((n, d), F32),
        input_output_aliases=aliases,
        compiler_params=_cparams(("parallel",), 48),
        name="combine",
    )(*args)


def _routing_tables(top_idx, rank, counts, n):
    tmb = EXPERT_BLOCK
    counts = counts.astype(I32)
    padded = (counts + tmb - 1) // tmb * tmb
    pad_end = jnp.cumsum(padded)
    pad_start = pad_end - padded
    n_blocks = -(-(n * TOP_K + N_EXPERTS * (tmb - 1)) // tmb)
    n_slots = n_blocks * tmb
    onehot = top_idx[:, :, None] == jnp.arange(N_EXPERTS, dtype=I32)[None, None, :]
    dest = jnp.sum(jnp.where(onehot, pad_start[None, None, :], 0), axis=-1) + rank
    dest_flat = jnp.transpose(dest).reshape(-1)
    block_start = jnp.arange(n_blocks, dtype=I32) * tmb
    block_e = jnp.minimum(jnp.sum((pad_end[None, :] <= block_start[:, None]).astype(I32), axis=1), N_EXPERTS - 1)
    n_valid = (pad_end[-1:] // tmb).astype(I32)
    return block_e, n_valid, dest_flat, n_slots


def _gdn_branch(xtok, sh, sc, norm_g, w_cols, widths, dtypes, conv_w, par, tokens_per_batch, grid_w, use_rows,
                tm_proj, tm_conv, s0, prec):
    b = xtok.shape[0] // tokens_per_batch
    outs = _inproj_call(xtok, sh, sc, norm_g, w_cols, widths, dtypes, tokens_per_batch, tm_proj)
    qkv, gates = outs[0], outs[-1]
    q, k, v, go = _conv_call(qkv.reshape(b, tokens_per_batch, QKV_DIM), conv_w,
                             gates.reshape(b, tokens_per_batch, LANES), par, grid_w, use_rows, tm_conv)
    o_f, o_b, s_fin = _gdn_call(q, k, v, go, _gdn_decay_rows(go), s0, prec)
    return outs, o_f, o_b, s_fin


def kernel(x, c, ctx, c_ctx, w_mod, b_mod, norm1_g, norm2_g, w_in, conv_w, a_log, dt_bias, gdn_norm_g,
           w_fourier_out, w_gdn_out, w_merge_out, w_router, b_router, w_gate, b_gate, w_up, b_up,
           w_down, b_down, final_norm_g):
    b, l, d = x.shape
    n = b * l
    n_ctx = ctx.shape[1]
    assert w_mod.shape[0] == 1 and l == GRID_W * GRID_W and d == V_DIM
    prec = None

    c8 = jnp.concatenate([c, c_ctx[None, :], jnp.zeros((8 - b - 1, d), F32)], axis=0)
    mod = _mod_call(c8, w_mod[0], b_mod[0])
    sh1, sc1, g1, sh2, sc2, g2 = [mod[:b, j * d:(j + 1) * d].reshape(b, 1, d) for j in range(6)]
    csh1 = jnp.broadcast_to(mod[b:b + 1, 0:d].reshape(1, 1, d), (b, 1, d))
    csc1 = jnp.broadcast_to(mod[b:b + 1, d:2 * d].reshape(1, 1, d), (b, 1, d))

    wi = w_in[0]
    off_gate = QKV_DIM
    off_z = off_gate + 4 * NV_HEADS
    off_f = off_z + V_DIM
    off_ga = off_f + F_DIM
    gate_cols = jnp.pad(wi[:, off_gate:off_z], ((0, 0), (0, LANES - 4 * NV_HEADS)))
    w_lat = jnp.concatenate([wi[:, :QKV_DIM], wi[:, off_z:off_f], wi[:, off_f:off_ga], wi[:, off_ga:], gate_cols],
                            axis=1).astype(BF16)
    w_ctx = jnp.concatenate([wi[:, :QKV_DIM], gate_cols], axis=1).astype(BF16)
    par = jnp.pad(jnp.stack([a_log[0].reshape(-1), dt_bias[0].reshape(-1)]),
                  ((0, 6), (GATE_LANE0, LANES - 2 * GATE_LANE0)))
    n1 = norm1_g[0].reshape(1, d)
    cw = conv_w[0].reshape(9, QKV_DIM)

    zero_state = jnp.zeros((b, 2, NV_HEADS, HEAD_DIM, HEAD_DIM), F32)
    _, _, _, s_ctx = _gdn_branch(ctx.reshape(b * n_ctx, d), csh1, csc1, n1, w_ctx, (QKV_DIM, LANES), (BF16, F32),
                                 cw, par, n_ctx, n_ctx, False, n_ctx, n_ctx, zero_state, prec)

    x2 = x.reshape(n, d)
    outs, o_f, o_b, _ = _gdn_branch(x2, sh1, sc1, n1, w_lat, (QKV_DIM, V_DIM, F_DIM, 2 * d, LANES),
                                    (BF16, BF16, F32, BF16, F32), cw, par, l, GRID_W, True, 512, 512, s_ctx, prec)
    _, z, f, gab, _ = outs
    fmix = _fnet_call(f.reshape(b, l, F_DIM)).reshape(n, F_DIM)

    wr = jnp.pad(w_router[0], ((0, 0), (0, LANES - N_EXPERTS)))
    br = jnp.pad(b_router[0], (0, LANES - N_EXPERTS), constant_values=NEG_BIG).reshape(1, LANES)
    x1, h2, logits = _merge_call(
        o_f.reshape(n, V_DIM), o_b.reshape(n, V_DIM), z, fmix, gab, x2, g1, sh2, sc2,
        gdn_norm_g[0].reshape(1, HEAD_DIM), norm2_g[0].reshape(1, d),
        w_gdn_out[0].astype(BF16), w_fourier_out[0].astype(BF16), w_merge_out[0].astype(BF16), wr, br, l, 256)

    out = None
    part = n // MOE_PARTS
    for start in range(0, n, part):
        top_idx, top_w, rank, counts = _route_call(logits, 512, start, part)
        block_e, n_valid, dest_flat, n_slots = _routing_tables(top_idx[:, :TOP_K], rank[:, :TOP_K],
                                                               counts[0, :N_EXPERTS], part)
        xs = _sc_scatter_rows(h2, dest_flat, n_slots, start, part)
        ys = _expert_call(block_e, n_valid, xs, w_gate[0], w_up[0], w_down[0], b_gate[0], b_up[0], b_down[0])
        y4 = _sc_gather_rows(ys, dest_flat)
        out = _combine_call(out, y4, top_w, x1, g2, final_norm_g.reshape(1, d), l, 256, start)
    return out.reshape(b, l, d)
```

```python
import functools
import math

import jax
import jax.numpy as jnp
import numpy as np
from jax import lax
from jax.experimental import pallas as pl
from jax.experimental.pallas import tpu as pltpu
from jax.experimental.pallas import tpu_sc as plsc

F32 = jnp.float32
BF16 = jnp.bfloat16
I32 = jnp.int32
HIGHEST = lax.Precision.HIGHEST

GRID_W = 64
NQK_HEADS = 4
NV_HEADS = 8
HEAD_DIM = 128
QK_DIM = NQK_HEADS * HEAD_DIM
V_DIM = NV_HEADS * HEAD_DIM
QKV_DIM = 2 * QK_DIM + V_DIM
F_GROUPS = 4
F_DIM = F_GROUPS * HEAD_DIM
N_EXPERTS = 32
TOP_K = 4
SWIGLU_ALPHA = 1.702
SWIGLU_LIMIT = 7.0
EPS = 1e-6

LANES = 128
SUBLANES = 8
GATE_LANE0 = 16
GDN_CHUNK = 128
EXPERT_BLOCK = 512
MOE_PARTS = 1
SC_CORES = 2
SC_SUBCORES = 16
SC_WORKERS = SC_CORES * SC_SUBCORES
SC_WINDOW = 64
NEG_BIG = -1e30
MIB = 2 ** 20


def _cparams(sem, vmem_mib):
    return pltpu.CompilerParams(dimension_semantics=sem, vmem_limit_bytes=vmem_mib * MIB)


def _mm(a, b, prec=None, dims=(((1,), (0,)), ((), ()))):
    if prec is None:
        return lax.dot_general(a.astype(BF16), b.astype(BF16), dims, preferred_element_type=F32)
    return lax.dot_general(a.astype(F32), b.astype(F32), dims, precision=prec, preferred_element_type=F32)


def _sigmoid(x):
    return 1.0 / (1.0 + jnp.exp(-x))


def _rmsnorm(x, g):
    return x * lax.rsqrt(jnp.mean(x * x, axis=-1, keepdims=True) + EPS) * g


def _split3(x):
    hi = x.astype(BF16)
    rest = x - hi.astype(F32)
    mid = rest.astype(BF16)
    return hi, mid, (rest - mid.astype(F32)).astype(BF16)


def _pack_bf16_pairs(x):
    half = x.shape[1] // 2
    bits = lax.bitcast_convert_type(x.astype(BF16).astype(F32), jnp.uint32)
    packed = jnp.bitwise_or(jnp.right_shift(bits[:, :half], jnp.uint32(16)),
                            jnp.bitwise_and(bits[:, half:], jnp.uint32(0xFFFF0000)))
    return lax.bitcast_convert_type(packed, I32)


def _unpack_bf16_pairs(p):
    bits = lax.bitcast_convert_type(p, jnp.uint32)
    lo = lax.bitcast_convert_type(jnp.left_shift(bits, jnp.uint32(16)), F32)
    hi = lax.bitcast_convert_type(jnp.bitwise_and(bits, jnp.uint32(0xFFFF0000)), F32)
    return jnp.concatenate([lo, hi], axis=1)


def _mod_kernel(c_ref, w_ref, b_ref, o_ref):
    c = c_ref[...]
    o_ref[...] = _mm(c * _sigmoid(c), w_ref[...], HIGHEST) + b_ref[...]


def _mod_call(c8, w_mod, b_mod):
    d, n = w_mod.shape
    tn = 1536
    return pl.pallas_call(
        _mod_kernel,
        grid=(n // tn,),
        in_specs=[pl.BlockSpec((8, d), lambda j: (0, 0)),
                  pl.BlockSpec((d, tn), lambda j: (0, j)),
                  pl.BlockSpec((1, tn), lambda j: (0, j))],
        out_specs=pl.BlockSpec((8, tn), lambda j: (0, j)),
        out_shape=jax.ShapeDtypeStruct((8, n), F32),
        compiler_params=_cparams(("parallel",), 32),
        name="mod",
    )(c8, w_mod, b_mod.reshape(1, n))


def _inproj_kernel(x_ref, sh_ref, sc_ref, g_ref, *refs):
    w_refs, o_refs = refs[:len(refs) // 2], refs[len(refs) // 2:]
    u = (_rmsnorm(x_ref[...], g_ref[...]) * (1.0 + sc_ref[0]) + sh_ref[0]).astype(BF16)
    for w_ref, o_ref in zip(w_refs, o_refs):
        width = w_ref.shape[1]
        step = min(width, 512)
        for c0 in range(0, width, step):
            o_ref[:, c0:c0 + step] = _mm(u, w_ref[:, c0:c0 + step]).astype(o_ref.dtype)


def _inproj_call(x2, sh, sc, g, ws, dtypes, tokens_per_batch, tm):
    n, d = x2.shape
    per = tokens_per_batch // tm
    widths = [w.shape[1] for w in ws]
    out_shape = [jax.ShapeDtypeStruct((n, wd), dt) for wd, dt in zip(widths, dtypes)]
    return pl.pallas_call(
        _inproj_kernel,
        grid=(n // tm,),
        in_specs=[pl.BlockSpec((tm, d), lambda i: (i, 0)),
                  pl.BlockSpec((1, 1, d), lambda i: (i // per, 0, 0)),
                  pl.BlockSpec((1, 1, d), lambda i: (i // per, 0, 0)),
                  pl.BlockSpec((1, d), lambda i: (0, 0))] + [pl.BlockSpec(w.shape, lambda i: (0, 0)) for w in ws],
        out_specs=[pl.BlockSpec((tm, wd), lambda i: (i, 0)) for wd in widths],
        out_shape=out_shape,
        compiler_params=_cparams(("parallel",), 56),
        name="inproj",
    )(x2, sh, sc, g, *ws)


def _conv_kernel(*refs, grid_w, use_rows, tm, cw):
    if use_rows:
        prev_ref, main_ref, next_ref, w_ref, gates_ref, par_ref, q_ref, k_ref, v_ref, go_ref = refs
    else:
        main_ref, w_ref, gates_ref, par_ref, q_ref, k_ref, v_ref, go_ref = refs
    r = pl.program_id(1)
    nr = pl.num_programs(1)
    t = lax.broadcasted_iota(I32, (tm, 1), 0)
    col = jnp.bitwise_and(t, grid_w - 1)
    m_left = (col != 0).astype(F32)
    m_right = (col != grid_w - 1).astype(F32)
    has_prev = (r > 0).astype(F32)
    has_next = (r < nr - 1).astype(F32)
    for c0 in range(0, QKV_DIM, cw):
        xm = main_ref[0, :, c0:c0 + cw].astype(F32)
        if use_rows:
            xp = prev_ref[0, :, c0:c0 + cw].astype(F32) * has_prev
            xn = next_ref[0, :, c0:c0 + cw].astype(F32) * has_next
            up = jnp.concatenate([xp, xm[:tm - grid_w]], axis=0)
            dn = jnp.concatenate([xm[grid_w:], xn], axis=0)

        def colsum(kc):
            y = xm * w_ref[3 + kc:4 + kc, c0:c0 + cw]
            if use_rows:
                y = y + up * w_ref[kc:kc + 1, c0:c0 + cw] + dn * w_ref[6 + kc:7 + kc, c0:c0 + cw]
            return y

        acc = (colsum(1) + pltpu.roll(colsum(0), 1, axis=0) * m_left
               + pltpu.roll(colsum(2), tm - 1, axis=0) * m_right)
        s = acc * _sigmoid(acc)
        for h0 in range(0, cw, HEAD_DIM):
            c = c0 + h0
            seg = s[:, h0:h0 + HEAD_DIM]
            if c < 2 * QK_DIM:
                seg = seg * lax.rsqrt(jnp.sum(seg * seg, axis=-1, keepdims=True) + EPS)
            if c < QK_DIM:
                q_ref[0, :, c:c + HEAD_DIM] = (seg * HEAD_DIM ** -0.5).astype(q_ref.dtype)
            elif c < 2 * QK_DIM:
                k_ref[0, :, c - QK_DIM:c - QK_DIM + HEAD_DIM] = seg.astype(k_ref.dtype)
            else:
                v_ref[0, :, c - 2 * QK_DIM:c - 2 * QK_DIM + HEAD_DIM] = seg.astype(v_ref.dtype)
    g = gates_ref[0]
    a = g + par_ref[1:2, :]
    softplus = jnp.maximum(a, 0.0) + jnp.log1p(jnp.exp(-jnp.abs(a)))
    log_g = -jnp.exp(par_ref[0:1, :]) * softplus
    lane = lax.broadcasted_iota(I32, g.shape, 1)
    go_ref[0] = jnp.where(lane < GATE_LANE0, _sigmoid(g), log_g)


def _conv_call(qkv, conv_w, gates, par, grid_w, use_rows, tm):
    b, t, c = qkv.shape
    kern = functools.partial(_conv_kernel, grid_w=grid_w, use_rows=use_rows, tm=tm, cw=512)
    per = tm // grid_w
    nrow = t // grid_w
    in_specs = []
    args = []
    if use_rows:
        in_specs.append(pl.BlockSpec((1, grid_w, c), lambda i, r: (i, jnp.maximum(r * per - 1, 0), 0)))
        args.append(qkv)
    in_specs.append(pl.BlockSpec((1, tm, c), lambda i, r: (i, r, 0)))
    args.append(qkv)
    if use_rows:
        in_specs.append(pl.BlockSpec((1, grid_w, c), lambda i, r: (i, jnp.minimum((r + 1) * per, nrow - 1), 0)))
        args.append(qkv)
    in_specs += [pl.BlockSpec(conv_w.shape, lambda i, r: (0, 0)),
                 pl.BlockSpec((1, tm, LANES), lambda i, r: (i, r, 0)),
                 pl.BlockSpec(par.shape, lambda i, r: (0, 0))]
    args += [conv_w, gates, par]
    out_shape = [jax.ShapeDtypeStruct((b, t, QK_DIM), BF16), jax.ShapeDtypeStruct((b, t, QK_DIM), BF16),
                 jax.ShapeDtypeStruct((b, t, V_DIM), BF16), jax.ShapeDtypeStruct((b, t, LANES), F32)]
    out_specs = [pl.BlockSpec((1, tm, QK_DIM), lambda i, r: (i, r, 0)),
                 pl.BlockSpec((1, tm, QK_DIM), lambda i, r: (i, r, 0)),
                 pl.BlockSpec((1, tm, V_DIM), lambda i, r: (i, r, 0)),
                 pl.BlockSpec((1, tm, LANES), lambda i, r: (i, r, 0))]
    return pl.pallas_call(
        kern, grid=(b, t // tm), in_specs=in_specs, out_specs=out_specs, out_shape=out_shape,
        compiler_params=_cparams(("parallel", "parallel"), 48),
        name="conv_rows" if use_rows else "conv_seq",
    )(*args)


def _gdn_kernel(qf, kf, vf, gf, rf, qb, kb, vb, gb, rb, s0_ref, of, ob, sfin_ref, s_ref, *, prec):
    i = pl.program_id(1)
    nc = pl.num_programs(1)

    @pl.when(i == 0)
    def _():
        s_ref[...] = s0_ref[0]

    c = qf.shape[1]
    per = NV_HEADS // NQK_HEADS
    row = lax.broadcasted_iota(I32, (c, c), 0)
    colj = lax.broadcasted_iota(I32, (c, c), 1)
    eye = jnp.where(row == colj, 1.0, 0.0)
    nt_dims = (((1,), (1,)), ((), ()))
    tn_dims = (((0,), (0,)), ((), ()))

    seqs = []
    for d, (q_r, k_r, v_r, g_r, r_r, o_r) in enumerate(((qf, kf, vf, gf, rf, of), (qb, kb, vb, gb, rb, ob))):
        rev = d == 1
        incl = (colj >= row) if rev else (colj <= row)
        strict = (colj > row) if rev else (colj < row)
        gates = g_r[0]
        tri_c = jnp.where(incl, 1.0, 0.0).astype(BF16)
        g3 = jnp.concatenate(_split3(gates), axis=1)
        gcm3 = lax.dot_general(tri_c, g3, (((1,), (0,)), ((), ())), preferred_element_type=F32)
        gcm = gcm3[:, :LANES] + gcm3[:, LANES:2 * LANES] + gcm3[:, 2 * LANES:]
        nrow = r_r.shape[2]
        r3 = jnp.concatenate(_split3(r_r[0, 0]), axis=0)
        gcr3 = lax.dot_general(r3, tri_c, nt_dims, preferred_element_type=F32)
        gcr = gcr3[:nrow] + gcr3[nrow:2 * nrow] + gcr3[2 * nrow:]
        for hq in range(NQK_HEADS):
            q = q_r[0, :, hq * HEAD_DIM:(hq + 1) * HEAD_DIM]
            k = k_r[0, :, hq * HEAD_DIM:(hq + 1) * HEAD_DIM]
            kq = lax.dot_general(jnp.concatenate([q, k], axis=0), k, nt_dims, preferred_element_type=F32)
            for j in range(per):
                h = hq * per + j
                idx = d * NV_HEADS + h
                gc_c = gcm[:, GATE_LANE0 + idx:GATE_LANE0 + idx + 1]
                seqs.append(dict(d=d, h=h, o_r=o_r, v_r=v_r, q=q, k=k, qk=kq[:c], kk=kq[c:], incl=incl, strict=strict,
                                 beta=gates[:, idx:idx + 1], gc_c=gc_c, gc_r=gcr[idx:idx + 1, :],
                                 ge=gc_c[0:1] if rev else gc_c[c - 1:c]))

    def same_block(m):
        sh = int(math.log2(m))
        return jnp.right_shift(row, sh) == jnp.right_shift(colj, sh)

    for s in seqs:
        s['decay'] = jnp.where(s['incl'], jnp.exp(jnp.where(s['incl'], s['gc_c'] - s['gc_r'], 0.0)), 0.0)
        s['a'] = jnp.where(s['strict'], s['beta'] * s['kk'] * s['decay'], 0.0)
        s['t'] = eye - jnp.where(same_block(2), s['a'], 0.0)
    m = 4
    while m <= c:
        between = jnp.logical_and(same_block(m), jnp.logical_not(same_block(m // 2)))
        for s in seqs:
            s['te'] = _mm(s['t'], jnp.where(between, s['a'], 0.0), prec)
        for s in seqs:
            s['t'] = s['t'] - _mm(s['te'], s['t'], prec)
        m *= 2
    for s in seqs:
        h = s['h']
        egc = jnp.exp(s['gc_c'])
        kf32 = s['k'].astype(F32)
        v = s['v_r'][0, :, h * HEAD_DIM:(h + 1) * HEAD_DIM].astype(F32)
        rhs = jnp.concatenate([s['beta'] * v, (s['beta'] * egc) * kf32], axis=1)
        s['sol'] = _mm(s['t'], rhs, prec)
        s['q_dec'] = s['q'].astype(F32) * egc
        s['k_dec'] = kf32 * jnp.exp(s['ge'] - s['gc_c'])
    for s in seqs:
        s['ws'] = _mm(jnp.concatenate([s['sol'][:, HEAD_DIM:], s['q_dec']], axis=0), s_ref[s['d'], s['h']])
    for s in seqs:
        s['u'] = s['sol'][:, :HEAD_DIM] - s['ws'][:c]
        s_ref[s['d'], s['h']] = (jnp.exp(s['ge']) * s_ref[s['d'], s['h']]
                                 + _mm(s['k_dec'], s['u'], dims=tn_dims))
    for s in seqs:
        h = s['h']
        o = s['ws'][c:] + _mm(s['qk'] * s['decay'], s['u'])
        s['o_r'][0, :, h * HEAD_DIM:(h + 1) * HEAD_DIM] = o.astype(s['o_r'].dtype)

    @pl.when(i == nc - 1)
    def _():
        sfin_ref[0] = s_ref[...]


def _gdn_call(q, k, v, go, rows, s0, prec):
    b, t, _ = q.shape
    c = GDN_CHUNK
    nc = t // c
    fwd = lambda i, n: (i, n, 0)
    bwd = lambda i, n: (i, nc - 1 - n, 0)
    rfwd = lambda i, n: (i, n, 0, 0)
    rbwd = lambda i, n: (i, nc - 1 - n, 0, 0)
    state_spec = pl.BlockSpec((1,) + s0.shape[1:], lambda i, n: (i, 0, 0, 0, 0))

    def specs(m3, mr):
        return [pl.BlockSpec((1, c, QK_DIM), m3), pl.BlockSpec((1, c, QK_DIM), m3),
                pl.BlockSpec((1, c, V_DIM), m3), pl.BlockSpec((1, c, LANES), m3),
                pl.BlockSpec((1, 1, 2 * NV_HEADS, c), mr)]

    return pl.pallas_call(
        functools.partial(_gdn_kernel, prec=prec),
        grid=(b, nc),
        in_specs=specs(fwd, rfwd) + specs(bwd, rbwd) + [state_spec],
        out_specs=[pl.BlockSpec((1, c, V_DIM), fwd), pl.BlockSpec((1, c, V_DIM), bwd), state_spec],
        out_shape=[jax.ShapeDtypeStruct((b, t, V_DIM), BF16), jax.ShapeDtypeStruct((b, t, V_DIM), BF16),
                   jax.ShapeDtypeStruct(s0.shape, F32)],
        scratch_shapes=[pltpu.VMEM(s0.shape[1:], F32)],
        compiler_params=_cparams(("parallel", "arbitrary"), 48),
        name="gdn",
    )(q, k, v, go, rows, q, k, v, go, rows, s0)


def _gdn_decay_rows(go):
    b, t, _ = go.shape
    c = GDN_CHUNK
    lg = go[..., GATE_LANE0:GATE_LANE0 + 2 * NV_HEADS]
    return jnp.transpose(lg.reshape(b, t // c, c, 2 * NV_HEADS), (0, 1, 3, 2))


def _fnet1_kernel(x_ref, f_ref, ar_ref, ai_ref):
    n = x_ref.shape[1]
    for j in range(x_ref.shape[2]):
        a = _mm(f_ref[...], x_ref[0, :, j, :])
        ar_ref[0, j] = a[:n]
        ai_ref[0, j] = a[n:]


def _fnet2_kernel(ar_ref, ai_ref, g_ref, wc_ref, o_ref, *, scale):
    n = ar_ref.shape[1]
    zs = []
    for m in range(ar_ref.shape[2]):
        a2 = jnp.concatenate([ar_ref[0, :, m, :], ai_ref[0, :, m, :]], axis=0)
        zs.append(_mm(g_ref[m], a2))
    for m, z in enumerate(zs):
        y = _mm(jnp.concatenate([z[:n], z[n:]], axis=1), wc_ref[...])
        o_ref[0, :, m, :] = y * scale


def _fnet_tables(n, groups):
    a = np.arange(n)
    ang1 = 2.0 * np.pi * np.outer(a, a) / n
    f1 = np.concatenate([np.cos(ang1), -np.sin(ang1)], axis=0)
    m = a[:, None] + n * a[None, :]
    ang2 = 2.0 * np.pi * ((m[:, :, None] * a[None, None, :]) % (n * n)) / (n * n)
    gc, gs = np.cos(ang2), np.sin(ang2)
    g2 = np.concatenate([np.concatenate([gc, gs], axis=2), np.concatenate([-gs, gc], axis=2)], axis=1)
    angc = 2.0 * np.pi * np.outer(np.arange(HEAD_DIM), np.arange(HEAD_DIM)) / HEAD_DIM
    eye = np.eye(groups)
    wc = np.concatenate([np.kron(eye, np.cos(angc)), np.kron(eye, np.sin(angc))], axis=0)
    f = lambda x: jnp.asarray(x, F32).astype(BF16)
    return f(f1), f(g2), f(wc)


def _fnet_call(f):
    b, l, c = f.shape
    n = GRID_W
    assert l == n * n
    f1, g2, wc = _fnet_tables(n, c // HEAD_DIM)
    cols = SUBLANES
    ar, ai = pl.pallas_call(
        _fnet1_kernel,
        grid=(b, n // cols),
        in_specs=[pl.BlockSpec((1, n, cols, c), lambda i, j: (i, 0, j, 0)),
                  pl.BlockSpec((2 * n, n), lambda i, j: (0, 0))],
        out_specs=[pl.BlockSpec((1, cols, n, c), lambda i, j: (i, j, 0, 0))] * 2,
        out_shape=[jax.ShapeDtypeStruct((b, n, n, c), F32)] * 2,
        compiler_params=_cparams(("parallel", "parallel"), 32),
        name="fnet1",
    )(f.reshape(b, n, n, c), f1)
    out = pl.pallas_call(
        functools.partial(_fnet2_kernel, scale=1.0 / math.sqrt(l * HEAD_DIM)),
        grid=(b, n // cols),
        in_specs=[pl.BlockSpec((1, n, cols, c), lambda i, j: (i, 0, j, 0)),
                  pl.BlockSpec((1, n, cols, c), lambda i, j: (i, 0, j, 0)),
                  pl.BlockSpec((cols, 2 * n, 2 * n), lambda i, j: (j, 0, 0)),
                  pl.BlockSpec((2 * c, c), lambda i, j: (0, 0))],
        out_specs=pl.BlockSpec((1, n, cols, c), lambda i, j: (i, 0, j, 0)),
        out_shape=jax.ShapeDtypeStruct((b, n, n, c), F32),
        compiler_params=_cparams(("parallel", "parallel"), 32),
        name="fnet2",
    )(ar, ai, g2, wc)
    return out.reshape(b, l, c)


def _merge_kernel(of_ref, ob_ref, z_ref, fm_ref, gab_ref, x_ref, g1_ref, sh2_ref, sc2_ref, gn_ref, n2_ref,
                  wg_ref, wf_ref, wm_ref, wr_ref, br_ref, x1_ref, h2_ref, lg_ref):
    d = x_ref.shape[1]
    tm = x_ref.shape[0]
    halves = [slice(0, tm // 2), slice(tm // 2, tm)]
    yb_in = []
    for r in halves:
        o = of_ref[r, :].astype(F32) + ob_ref[r, :].astype(F32)
        z = z_ref[r, :].astype(F32)
        parts = []
        for h0 in range(0, V_DIM, HEAD_DIM):
            oh = o[:, h0:h0 + HEAD_DIM]
            parts.append(oh * lax.rsqrt(jnp.mean(oh * oh, axis=-1, keepdims=True) + EPS) * gn_ref[...])
        yb_in.append(jnp.concatenate(parts, axis=1) * (z * _sigmoid(z)))
    yb = [_mm(v, wg_ref[...]) for v in yb_in]
    ya = [_mm(fm_ref[r, :], wf_ref[...]) for r in halves]
    mixed = [_sigmoid(gab_ref[r, :d].astype(F32)) * a + _sigmoid(gab_ref[r, d:].astype(F32)) * b_
             for r, a, b_ in zip(halves, ya, yb)]
    mm = [_mm(v, wm_ref[...]) for v in mixed]
    w = wr_ref[...]
    w_hi = w.astype(BF16)
    w2 = jnp.concatenate([w_hi, (w - w_hi.astype(F32)).astype(BF16)], axis=1)
    for r, v in zip(halves, mm):
        x1 = x_ref[r, :] + g1_ref[0] * v
        x1_ref[r, :] = x1
        h2 = _rmsnorm(x1, n2_ref[...]) * (1.0 + sc2_ref[0]) + sh2_ref[0]
        h2_ref[r, :] = _pack_bf16_pairs(h2)
        h_hi = h2.astype(BF16)
        h_lo = (h2 - h_hi.astype(F32)).astype(BF16)
        part = _mm(h_hi, w2)
        lg_ref[r, :] = (part[:, :LANES] + part[:, LANES:]) + _mm(h_lo, w_hi) + br_ref[...]


def _merge_call(of, ob, z, fm, gab, x2, g1, sh2, sc2, gn, n2, wg, wf, wm, wr, br, tokens_per_batch, tm):
    n, d = x2.shape
    per = tokens_per_batch // tm
    tok = lambda wd: pl.BlockSpec((tm, wd), lambda i: (i, 0))
    vec = pl.BlockSpec((1, 1, d), lambda i: (i // per, 0, 0))
    full = lambda a: pl.BlockSpec(a.shape, lambda i: (0,) * a.ndim)
    return pl.pallas_call(
        _merge_kernel,
        grid=(n // tm,),
        in_specs=[tok(V_DIM), tok(V_DIM), tok(V_DIM), tok(F_DIM), tok(2 * d), tok(d), vec, vec, vec,
                  full(gn), full(n2), full(wg), full(wf), full(wm), full(wr), full(br)],
        out_specs=[tok(d), tok(d // 2), tok(LANES)],
        out_shape=[jax.ShapeDtypeStruct((n, d), F32), jax.ShapeDtypeStruct((n, d // 2), I32),
                   jax.ShapeDtypeStruct((n, LANES), F32)],
        compiler_params=_cparams(("parallel",), 56),
        name="merge",
    )(of, ob, z, fm, gab, x2, g1, sh2, sc2, gn, n2, wg, wf, wm, wr, br)


def _route_kernel(lg_ref, idx_ref, w_ref, rank_ref, cnt_ref, run_ref):
    i = pl.program_id(0)

    @pl.when(i == 0)
    def _():
        run_ref[...] = jnp.zeros_like(run_ref)

    l = lg_ref[...]
    tm = l.shape[0]
    lane = lax.broadcasted_iota(I32, l.shape, 1)
    vals, idxs = [], []
    for _ in range(TOP_K):
        m = jnp.max(l, axis=-1, keepdims=True)
        idx = jnp.min(jnp.where(l == m, lane, LANES), axis=-1, keepdims=True)
        vals.append(m)
        idxs.append(idx)
        l = jnp.where(lane == idx, NEG_BIG * 2.0, l)
    es = [jnp.exp(v - vals[0]) for v in vals]
    inv = 1.0 / (es[0] + es[1] + es[2] + es[3])
    picked = jnp.zeros(l.shape, F32)
    for idx in idxs:
        picked = picked + (lane == idx).astype(F32)
    r = lax.broadcasted_iota(I32, (tm, tm), 0)
    cidx = lax.broadcasted_iota(I32, (tm, tm), 1)
    before = _mm(jnp.where(cidx < r, 1.0, 0.0), picked) + run_ref[...]
    idx_out = jnp.zeros(l.shape, I32)
    w_out = jnp.zeros(l.shape, F32)
    rank_out = jnp.zeros(l.shape, F32)
    for k in range(TOP_K):
        rk = jnp.sum(jnp.where(lane == idxs[k], before, 0.0), axis=-1, keepdims=True)
        idx_out = jnp.where(lane == k, idxs[k], idx_out)
        w_out = jnp.where(lane == k, es[k] * inv, w_out)
        rank_out = jnp.where(lane == k, rk, rank_out)
    idx_ref[...] = jnp.transpose(idx_out.astype(F32))[:SUBLANES].astype(I32)
    w_ref[...] = w_out
    rank_ref[...] = jnp.transpose(rank_out)[:SUBLANES].astype(I32)
    run_ref[...] = run_ref[...] + jnp.sum(picked, axis=0, keepdims=True)
    cnt_ref[...] = run_ref[...]


def _route_call(logits, tm, start, n):
    first = start // tm
    tok = pl.BlockSpec((tm, LANES), lambda i: (i, 0))
    tok_t = pl.BlockSpec((SUBLANES, tm), lambda i: (0, i))
    return pl.pallas_call(
        _route_kernel,
        grid=(n // tm,),
        in_specs=[pl.BlockSpec((tm, LANES), lambda i: (first + i, 0))],
        out_specs=[tok_t, tok, tok_t, pl.BlockSpec((1, LANES), lambda i: (0, 0))],
        out_shape=[jax.ShapeDtypeStruct((SUBLANES, n), I32), jax.ShapeDtypeStruct((n, LANES), F32),
                   jax.ShapeDtypeStruct((SUBLANES, n), I32), jax.ShapeDtypeStruct((1, LANES), F32)],
        scratch_shapes=[pltpu.VMEM((1, LANES), F32)],
        compiler_params=_cparams(("arbitrary",), 32),
        name="route",
    )(logits)


def _sc_mesh():
    return plsc.VectorSubcoreMesh(core_axis_name="c", subcore_axis_name="s",
                                  num_cores=SC_CORES, num_subcores=SC_SUBCORES)


def _sc_worker_id():
    return lax.axis_index("s") * SC_CORES + lax.axis_index("c")


def _sc_scatter_rows(x, idx, n_out, start, n):
    d = x.shape[1]
    per_w = idx.shape[0] // SC_WORKERS
    nwin = per_w // SC_WINDOW
    assert per_w * SC_WORKERS == idx.shape[0] and nwin * SC_WINDOW == per_w and nwin % 2 == 0 and n % per_w == 0

    def body(x_hbm, idx_hbm, out_hbm, idx_v, rows_v, sem_r, sem_w):
        wid = _sc_worker_id()
        row0 = start + lax.rem(wid * per_w, n)
        pltpu.sync_copy(idx_hbm.at[wid], idx_v)

        def read(j, b):
            return pltpu.make_async_copy(x_hbm.at[pl.ds(row0 + j * SC_WINDOW, SC_WINDOW)], rows_v.at[b], sem_r.at[b])

        def write(j, b):
            return pltpu.make_async_copy(rows_v.at[b], out_hbm.at[idx_v.at[j]], sem_w.at[b])

        @pl.loop(0, nwin, step=2)
        def _(j):
            read(j, 0).start()
            read(j + 1, 1).start()
            read(j, 0).wait()
            write(j, 0).start()
            read(j + 1, 1).wait()
            write(j + 1, 1).start()
            write(j, 0).wait()
            write(j + 1, 1).wait()

    return pl.kernel(
        body, out_type=jax.ShapeDtypeStruct((n_out, d), x.dtype), mesh=_sc_mesh(),
        scratch_types=[pltpu.VMEM((nwin, SC_WINDOW), I32), pltpu.VMEM((2, SC_WINDOW, d), x.dtype),
                       pltpu.SemaphoreType.DMA((2,)), pltpu.SemaphoreType.DMA((2,))],
        name="sc_scatter_rows",
    )(x, idx.reshape(SC_WORKERS, nwin, SC_WINDOW))


def _sc_gather_rows(y, idx):
    d = y.shape[1]
    total = idx.shape[0]
    per_w = total // SC_WORKERS
    nwin = per_w // SC_WINDOW
    assert per_w * SC_WORKERS == total and nwin * SC_WINDOW == per_w and nwin % 2 == 0

    def body(y_hbm, idx_hbm, out_hbm, idx_v, rows_v, sem_r, sem_w):
        wid = _sc_worker_id()
        row0 = wid * per_w
        pltpu.sync_copy(idx_hbm.at[wid], idx_v)

        def read(j, b):
            return pltpu.make_async_copy(y_hbm.at[idx_v.at[j]], rows_v.at[b], sem_r.at[b])

        def write(j, b):
            return pltpu.make_async_copy(rows_v.at[b], out_hbm.at[pl.ds(row0 + j * SC_WINDOW, SC_WINDOW)], sem_w.at[b])

        @pl.loop(0, nwin, step=2)
        def _(j):
            read(j, 0).start()
            read(j + 1, 1).start()
            read(j, 0).wait()
            write(j, 0).start()
            read(j + 1, 1).wait()
            write(j + 1, 1).start()
            write(j, 0).wait()
            write(j + 1, 1).wait()

    return pl.kernel(
        body, out_type=jax.ShapeDtypeStruct((total, d), y.dtype), mesh=_sc_mesh(),
        scratch_types=[pltpu.VMEM((nwin, SC_WINDOW), I32), pltpu.VMEM((2, SC_WINDOW, d), y.dtype),
                       pltpu.SemaphoreType.DMA((2,)), pltpu.SemaphoreType.DMA((2,))],
        name="sc_gather_rows",
    )(y, idx.reshape(SC_WORKERS, nwin, SC_WINDOW))


def _expert_kernel(be_ref, nv_ref, x_ref, wg_ref, wu_ref, wd_ref, bg_ref, bu_ref, bd_ref, y_ref, wgb, wub, wdb):
    i = pl.program_id(0)

    @pl.when(i < nv_ref[0])
    def _():
        changed = jnp.logical_or(i == 0, be_ref[i] != be_ref[jnp.maximum(i - 1, 0)])

        @pl.when(changed)
        def _():
            wgb[...] = wg_ref[0].astype(BF16)
            wub[...] = wu_ref[0].astype(BF16)
            wdb[...] = wd_ref[0].astype(BF16)

        tmb = x_ref.shape[0]
        halves = [slice(0, tmb // 2), slice(tmb // 2, tmb)]
        xs = [_unpack_bf16_pairs(x_ref[r, :]).astype(BF16) for r in halves]
        gate = [jnp.minimum(_mm(x, wgb[...]) + bg_ref[0], SWIGLU_LIMIT) for x in xs]
        up = [jnp.clip(_mm(x, wub[...]) + bu_ref[0], -SWIGLU_LIMIT, SWIGLU_LIMIT) for x in xs]
        for r, g, u in zip(halves, gate, up):
            act = (u + 1.0) * g * _sigmoid(SWIGLU_ALPHA * g)
            y_ref[r, :] = _pack_bf16_pairs(_mm(act, wdb[...]) + bd_ref[0])


def _expert_call(block_e, n_valid, xs, w_gate, w_up, w_down, b_gate, b_up, b_down):
    nb = block_e.shape[0]
    tmb = EXPERT_BLOCK
    ne, d, de = w_gate.shape
    wspec = lambda s: pl.BlockSpec((1,) + s, lambda i, be, nv: (be[i], 0, 0))
    grid_spec = pltpu.PrefetchScalarGridSpec(
        num_scalar_prefetch=2,
        grid=(nb,),
        in_specs=[pl.BlockSpec((tmb, d // 2), lambda i, be, nv: (i, 0)),
                  wspec((d, de)), wspec((d, de)), wspec((de, d)),
                  wspec((1, de)), wspec((1, de)), wspec((1, d))],
        out_specs=pl.BlockSpec((tmb, d // 2), lambda i, be, nv: (i, 0)),
        scratch_shapes=[pltpu.VMEM((d, de), BF16), pltpu.VMEM((d, de), BF16), pltpu.VMEM((de, d), BF16)],
    )
    return pl.pallas_call(
        _expert_kernel,
        grid_spec=grid_spec,
        out_shape=jax.ShapeDtypeStruct(xs.shape, xs.dtype),
        compiler_params=_cparams(("arbitrary",), 56),
        name="expert",
    )(block_e, n_valid, xs, w_gate, w_up, w_down,
      b_gate.reshape(ne, 1, de), b_up.reshape(ne, 1, de), b_down.reshape(ne, 1, d))


def _combine_kernel(y0, y1, y2, y3, w_ref, x1_ref, g2_ref, fg_ref, o_ref):
    w = w_ref[...]
    ys = [_unpack_bf16_pairs(y[...]) for y in (y0, y1, y2, y3)]
    moe = (w[:, 0:1] * ys[0] + w[:, 1:2] * ys[1]) + (w[:, 2:3] * ys[2] + w[:, 3:4] * ys[3])
    o_ref[...] = _rmsnorm(x1_ref[...] + g2_ref[0] * moe, fg_ref[...])


def _combine_call(prev, y4, top_w, x1, g2, fg, tokens_per_batch, tm, start):
    n, d = x1.shape
    per = tokens_per_batch // tm
    nt = top_w.shape[0] // tm
    first = start // tm
    yspec = lambda k: pl.BlockSpec((tm, d // 2), lambda i: (k * nt + i, 0))
    in_specs = [yspec(0), yspec(1), yspec(2), yspec(3),
                pl.BlockSpec((tm, LANES), lambda i: (i, 0)),
                pl.BlockSpec((tm, d), lambda i: (first + i, 0)),
                pl.BlockSpec((1, 1, d), lambda i: ((first + i) // per, 0, 0)),
                pl.BlockSpec((1, d), lambda i: (0, 0))]
    args = [y4, y4, y4, y4, top_w, x1, g2, fg]
    kern, aliases = _combine_kernel, {}
    if prev is not None:
        in_specs.append(pl.BlockSpec(memory_space=pl.ANY))
        args.append(prev)
        kern = lambda *refs: _combine_kernel(*refs[:8], refs[9])
        aliases = {8: 0}
    return pl.pallas_call(
        kern,
        grid=(nt,),
        in_specs=in_specs,
        out_specs=pl.BlockSpec((tm, d), lambda i: (first + i, 0)),
        out_shape=jax.ShapeDtypeStruct((n, d), F32),
        input_output_aliases=aliases,
        compiler_params=_cparams(("parallel",), 48),
        name="combine",
    )(*args)


def _routing_tables(top_idx, rank, counts, n):
    tmb = EXPERT_BLOCK
    counts = counts.astype(I32)
    padded = (counts + tmb - 1) // tmb * tmb
    pad_end = jnp.cumsum(padded)
    pad_start = pad_end - padded
    n_blocks = -(-(n * TOP_K + N_EXPERTS * (tmb - 1)) // tmb)
    n_slots = n_blocks * tmb
    dest = rank
    for e in range(N_EXPERTS):
        dest = dest + jnp.where(top_idx == e, pad_start[e], 0)
    dest_flat = dest.reshape(-1)
    block_start = jnp.arange(n_blocks, dtype=I32) * tmb
    block_e = jnp.minimum(jnp.sum((pad_end[None, :] <= block_start[:, None]).astype(I32), axis=1), N_EXPERTS - 1)
    n_valid = (pad_end[-1:] // tmb).astype(I32)
    return block_e, n_valid, dest_flat, n_slots


def _gdn_branch(xtok, sh, sc, norm_g, ws, dtypes, conv_w, par, tokens_per_batch, grid_w, use_rows,
                tm_proj, tm_conv, s0, prec):
    b = xtok.shape[0] // tokens_per_batch
    outs = _inproj_call(xtok, sh, sc, norm_g, ws, dtypes, tokens_per_batch, tm_proj)
    qkv, gates = outs[0], outs[-1]
    q, k, v, go = _conv_call(qkv.reshape(b, tokens_per_batch, QKV_DIM), conv_w,
                             gates.reshape(b, tokens_per_batch, LANES), par, grid_w, use_rows, tm_conv)
    o_f, o_b, s_fin = _gdn_call(q, k, v, go, _gdn_decay_rows(go), s0, prec)
    return outs, o_f, o_b, s_fin


def kernel(x, c, ctx, c_ctx, w_mod, b_mod, norm1_g, norm2_g, w_in, conv_w, a_log, dt_bias, gdn_norm_g,
           w_fourier_out, w_gdn_out, w_merge_out, w_router, b_router, w_gate, b_gate, w_up, b_up,
           w_down, b_down, final_norm_g):
    b, l, d = x.shape
    n = b * l
    n_ctx = ctx.shape[1]
    assert w_mod.shape[0] == 1 and l == GRID_W * GRID_W and d == V_DIM
    prec = None

    c8 = jnp.concatenate([c, c_ctx[None, :], jnp.zeros((8 - b - 1, d), F32)], axis=0)
    mod = _mod_call(c8, w_mod[0], b_mod[0])
    sh1, sc1, g1, sh2, sc2, g2 = [mod[:b, j * d:(j + 1) * d].reshape(b, 1, d) for j in range(6)]
    csh1 = jnp.broadcast_to(mod[b:b + 1, 0:d].reshape(1, 1, d), (b, 1, d))
    csc1 = jnp.broadcast_to(mod[b:b + 1, d:2 * d].reshape(1, 1, d), (b, 1, d))

    wi = w_in[0]
    off_gate = QKV_DIM
    off_z = off_gate + 4 * NV_HEADS
    off_f = off_z + V_DIM
    off_ga = off_f + F_DIM
    w_qkv = wi[:, :QKV_DIM].astype(BF16)
    w_gates = jnp.pad(wi[:, off_gate:off_z], ((0, 0), (0, LANES - 4 * NV_HEADS))).astype(BF16)
    w_lat = [w_qkv, wi[:, off_z:off_f].astype(BF16), wi[:, off_f:off_ga].astype(BF16), wi[:, off_ga:].astype(BF16),
             w_gates]
    w_ctx = [w_qkv, w_gates]
    par = jnp.pad(jnp.stack([a_log[0].reshape(-1), dt_bias[0].reshape(-1)]),
                  ((0, 6), (GATE_LANE0, LANES - 2 * GATE_LANE0)))
    n1 = norm1_g[0].reshape(1, d)
    cw = conv_w[0].reshape(9, QKV_DIM)

    zero_state = jnp.zeros((b, 2, NV_HEADS, HEAD_DIM, HEAD_DIM), F32)
    _, _, _, s_ctx = _gdn_branch(ctx.reshape(b * n_ctx, d), csh1, csc1, n1, w_ctx, (BF16, F32),
                                 cw, par, n_ctx, n_ctx, False, n_ctx, n_ctx, zero_state, prec)

    x2 = x.reshape(n, d)
    outs, o_f, o_b, _ = _gdn_branch(x2, sh1, sc1, n1, w_lat, (BF16, BF16, F32, BF16, F32),
                                    cw, par, l, GRID_W, True, 512, 512, s_ctx, prec)
    _, z, f, gab, _ = outs
    fmix = _fnet_call(f.reshape(b, l, F_DIM)).reshape(n, F_DIM)

    wr = jnp.pad(w_router[0], ((0, 0), (0, LANES - N_EXPERTS)))
    br = jnp.pad(b_router[0], (0, LANES - N_EXPERTS), constant_values=NEG_BIG).reshape(1, LANES)
    x1, h2, logits = _merge_call(
        o_f.reshape(n, V_DIM), o_b.reshape(n, V_DIM), z, fmix, gab, x2, g1, sh2, sc2,
        gdn_norm_g[0].reshape(1, HEAD_DIM), norm2_g[0].reshape(1, d),
        w_gdn_out[0].astype(BF16), w_fourier_out[0].astype(BF16), w_merge_out[0].astype(BF16), wr, br, l, 256)

    out = None
    part = n // MOE_PARTS
    for start in range(0, n, part):
        top_idx, top_w, rank, counts = _route_call(logits, 512, start, part)
        block_e, n_valid, dest_flat, n_slots = _routing_tables(top_idx[:TOP_K], rank[:TOP_K],
                                                               counts[0, :N_EXPERTS], part)
        xs = _sc_scatter_rows(h2, dest_flat, n_slots, start, part)
        ys = _expert_call(block_e, n_valid, xs, w_gate[0], w_up[0], w_down[0], b_gate[0], b_up[0], b_down[0])
        y4 = _sc_gather_rows(ys, dest_flat)
        out = _combine_call(out, y4, top_w, x1, g2, final_norm_g.reshape(1, d), l, 256, start)
    return out.reshape(b, l, d)
```

```python
import functools
import math

import jax
import jax.numpy as jnp
import numpy as np
from jax import lax
from jax.experimental import pallas as pl
from jax.experimental.pallas import tpu as pltpu
from jax.experimental.pallas import tpu_sc as plsc

F32 = jnp.float32
BF16 = jnp.bfloat16
I32 = jnp.int32
HIGHEST = lax.Precision.HIGHEST

GRID_W = 64
NQK_HEADS = 4
NV_HEADS = 8
HEAD_DIM = 128
QK_DIM = NQK_HEADS * HEAD_DIM
V_DIM = NV_HEADS * HEAD_DIM
QKV_DIM = 2 * QK_DIM + V_DIM
F_GROUPS = 4
F_DIM = F_GROUPS * HEAD_DIM
N_EXPERTS = 32
TOP_K = 4
SWIGLU_ALPHA = 1.702
SWIGLU_LIMIT = 7.0
EPS = 1e-6

LANES = 128
SUBLANES = 8
GATE_LANE0 = 16
GDN_CHUNK = 128
EXPERT_BLOCK = 512
MOE_PARTS = 1
SC_CORES = 2
SC_SUBCORES = 16
SC_WORKERS = SC_CORES * SC_SUBCORES
SC_WINDOW = 64
NEG_BIG = -1e30
MIB = 2 ** 20


def _cparams(sem, vmem_mib):
    return pltpu.CompilerParams(dimension_semantics=sem, vmem_limit_bytes=vmem_mib * MIB)


def _mm(a, b, prec=None, dims=(((1,), (0,)), ((), ()))):
    if prec is None:
        return lax.dot_general(a.astype(BF16), b.astype(BF16), dims, preferred_element_type=F32)
    return lax.dot_general(a.astype(F32), b.astype(F32), dims, precision=prec, preferred_element_type=F32)


def _sigmoid(x):
    return 1.0 / (1.0 + jnp.exp(-x))


def _rmsnorm(x, g):
    return x * lax.rsqrt(jnp.mean(x * x, axis=-1, keepdims=True) + EPS) * g


def _split3(x):
    hi = x.astype(BF16)
    rest = x - hi.astype(F32)
    mid = rest.astype(BF16)
    return hi, mid, (rest - mid.astype(F32)).astype(BF16)


def _pack_bf16_pairs(x):
    half = x.shape[1] // 2
    bits = lax.bitcast_convert_type(x.astype(BF16).astype(F32), jnp.uint32)
    packed = jnp.bitwise_or(jnp.right_shift(bits[:, :half], jnp.uint32(16)),
                            jnp.bitwise_and(bits[:, half:], jnp.uint32(0xFFFF0000)))
    return lax.bitcast_convert_type(packed, I32)


def _unpack_bf16_pairs(p):
    bits = lax.bitcast_convert_type(p, jnp.uint32)
    lo = lax.bitcast_convert_type(jnp.left_shift(bits, jnp.uint32(16)), F32)
    hi = lax.bitcast_convert_type(jnp.bitwise_and(bits, jnp.uint32(0xFFFF0000)), F32)
    return jnp.concatenate([lo, hi], axis=1)


def _mod_kernel(c_ref, w_ref, b_ref, o_ref):
    c = c_ref[...]
    o_ref[...] = _mm(c * _sigmoid(c), w_ref[...], HIGHEST) + b_ref[...]


def _mod_call(c8, w_mod, b_mod):
    d, n = w_mod.shape
    tn = 1536
    return pl.pallas_call(
        _mod_kernel,
        grid=(n // tn,),
        in_specs=[pl.BlockSpec((8, d), lambda j: (0, 0)),
                  pl.BlockSpec((d, tn), lambda j: (0, j)),
                  pl.BlockSpec((1, tn), lambda j: (0, j))],
        out_specs=pl.BlockSpec((8, tn), lambda j: (0, j)),
        out_shape=jax.ShapeDtypeStruct((8, n), F32),
        compiler_params=_cparams(("parallel",), 32),
        name="mod",
    )(c8, w_mod, b_mod.reshape(1, n))


def _inproj_kernel(x_ref, sh_ref, sc_ref, g_ref, *refs):
    w_refs, o_refs = refs[:len(refs) // 2], refs[len(refs) // 2:]
    u = (_rmsnorm(x_ref[...], g_ref[...]) * (1.0 + sc_ref[0]) + sh_ref[0]).astype(BF16)
    for w_ref, o_ref in zip(w_refs, o_refs):
        width = w_ref.shape[1]
        step = min(width, 512)
        for c0 in range(0, width, step):
            o_ref[:, c0:c0 + step] = _mm(u, w_ref[:, c0:c0 + step]).astype(o_ref.dtype)


def _inproj_call(x2, sh, sc, g, ws, dtypes, tokens_per_batch, tm):
    n, d = x2.shape
    per = tokens_per_batch // tm
    widths = [w.shape[1] for w in ws]
    out_shape = [jax.ShapeDtypeStruct((n, wd), dt) for wd, dt in zip(widths, dtypes)]
    return pl.pallas_call(
        _inproj_kernel,
        grid=(n // tm,),
        in_specs=[pl.BlockSpec((tm, d), lambda i: (i, 0)),
                  pl.BlockSpec((1, 1, d), lambda i: (i // per, 0, 0)),
                  pl.BlockSpec((1, 1, d), lambda i: (i // per, 0, 0)),
                  pl.BlockSpec((1, d), lambda i: (0, 0))] + [pl.BlockSpec(w.shape, lambda i: (0, 0)) for w in ws],
        out_specs=[pl.BlockSpec((tm, wd), lambda i: (i, 0)) for wd in widths],
        out_shape=out_shape,
        compiler_params=_cparams(("parallel",), 56),
        name="inproj",
    )(x2, sh, sc, g, *ws)


def _conv_kernel(*refs, grid_w, use_rows, tm, cw):
    if use_rows:
        prev_ref, main_ref, next_ref, w_ref, gates_ref, par_ref, q_ref, k_ref, v_ref, go_ref, gr_ref = refs
    else:
        main_ref, w_ref, gates_ref, par_ref, q_ref, k_ref, v_ref, go_ref, gr_ref = refs
    r = pl.program_id(1)
    nr = pl.num_programs(1)
    t = lax.broadcasted_iota(I32, (tm, 1), 0)
    col = jnp.bitwise_and(t, grid_w - 1)
    m_left = (col != 0).astype(F32)
    m_right = (col != grid_w - 1).astype(F32)
    has_prev = (r > 0).astype(F32)
    has_next = (r < nr - 1).astype(F32)
    for c0 in range(0, QKV_DIM, cw):
        xm = main_ref[0, :, c0:c0 + cw].astype(F32)
        if use_rows:
            xp = prev_ref[0, :, c0:c0 + cw].astype(F32) * has_prev
            xn = next_ref[0, :, c0:c0 + cw].astype(F32) * has_next
            up = jnp.concatenate([xp, xm[:tm - grid_w]], axis=0)
            dn = jnp.concatenate([xm[grid_w:], xn], axis=0)

        def colsum(kc):
            y = xm * w_ref[3 + kc:4 + kc, c0:c0 + cw]
            if use_rows:
                y = y + up * w_ref[kc:kc + 1, c0:c0 + cw] + dn * w_ref[6 + kc:7 + kc, c0:c0 + cw]
            return y

        acc = (colsum(1) + pltpu.roll(colsum(0), 1, axis=0) * m_left
               + pltpu.roll(colsum(2), tm - 1, axis=0) * m_right)
        s = acc * _sigmoid(acc)
        for h0 in range(0, cw, HEAD_DIM):
            c = c0 + h0
            seg = s[:, h0:h0 + HEAD_DIM]
            if c < 2 * QK_DIM:
                seg = seg * lax.rsqrt(jnp.sum(seg * seg, axis=-1, keepdims=True) + EPS)
            if c < QK_DIM:
                q_ref[0, :, c:c + HEAD_DIM] = (seg * HEAD_DIM ** -0.5).astype(q_ref.dtype)
            elif c < 2 * QK_DIM:
                k_ref[0, :, c - QK_DIM:c - QK_DIM + HEAD_DIM] = seg.astype(k_ref.dtype)
            else:
                v_ref[0, :, c - 2 * QK_DIM:c - 2 * QK_DIM + HEAD_DIM] = seg.astype(v_ref.dtype)
    g = gates_ref[0]
    a = g + par_ref[1:2, :]
    softplus = jnp.maximum(a, 0.0) + jnp.log1p(jnp.exp(-jnp.abs(a)))
    log_g = -jnp.exp(par_ref[0:1, :]) * softplus
    lane = lax.broadcasted_iota(I32, g.shape, 1)
    go_ref[0] = jnp.where(lane < GATE_LANE0, _sigmoid(g), log_g)
    gr_ref[0] = jnp.transpose(log_g)[GATE_LANE0:GATE_LANE0 + 2 * NV_HEADS]


def _conv_call(qkv, conv_w, gates, par, grid_w, use_rows, tm):
    b, t, c = qkv.shape
    kern = functools.partial(_conv_kernel, grid_w=grid_w, use_rows=use_rows, tm=tm, cw=512)
    per = tm // grid_w
    nrow = t // grid_w
    in_specs = []
    args = []
    if use_rows:
        in_specs.append(pl.BlockSpec((1, grid_w, c), lambda i, r: (i, jnp.maximum(r * per - 1, 0), 0)))
        args.append(qkv)
    in_specs.append(pl.BlockSpec((1, tm, c), lambda i, r: (i, r, 0)))
    args.append(qkv)
    if use_rows:
        in_specs.append(pl.BlockSpec((1, grid_w, c), lambda i, r: (i, jnp.minimum((r + 1) * per, nrow - 1), 0)))
        args.append(qkv)
    in_specs += [pl.BlockSpec(conv_w.shape, lambda i, r: (0, 0)),
                 pl.BlockSpec((1, tm, LANES), lambda i, r: (i, r, 0)),
                 pl.BlockSpec(par.shape, lambda i, r: (0, 0))]
    args += [conv_w, gates, par]
    out_shape = [jax.ShapeDtypeStruct((b, t, QK_DIM), BF16), jax.ShapeDtypeStruct((b, t, QK_DIM), BF16),
                 jax.ShapeDtypeStruct((b, t, V_DIM), BF16), jax.ShapeDtypeStruct((b, t, LANES), F32),
                 jax.ShapeDtypeStruct((b, 2 * NV_HEADS, t), F32)]
    out_specs = [pl.BlockSpec((1, tm, QK_DIM), lambda i, r: (i, r, 0)),
                 pl.BlockSpec((1, tm, QK_DIM), lambda i, r: (i, r, 0)),
                 pl.BlockSpec((1, tm, V_DIM), lambda i, r: (i, r, 0)),
                 pl.BlockSpec((1, tm, LANES), lambda i, r: (i, r, 0)),
                 pl.BlockSpec((1, 2 * NV_HEADS, tm), lambda i, r: (i, 0, r))]
    return pl.pallas_call(
        kern, grid=(b, t // tm), in_specs=in_specs, out_specs=out_specs, out_shape=out_shape,
        compiler_params=_cparams(("parallel", "parallel"), 48),
        name="conv_rows" if use_rows else "conv_seq",
    )(*args)


def _gdn_kernel(qf, kf, vf, gf, rf, qb, kb, vb, gb, rb, s0_ref, of, ob, sfin_ref, s_ref, *, prec):
    i = pl.program_id(1)
    nc = pl.num_programs(1)

    @pl.when(i == 0)
    def _():
        s_ref[...] = s0_ref[0]

    c = qf.shape[1]
    per = NV_HEADS // NQK_HEADS
    row = lax.broadcasted_iota(I32, (c, c), 0)
    colj = lax.broadcasted_iota(I32, (c, c), 1)
    eye = jnp.where(row == colj, 1.0, 0.0)
    nt_dims = (((1,), (1,)), ((), ()))
    tn_dims = (((0,), (0,)), ((), ()))

    seqs = []
    for d, (q_r, k_r, v_r, g_r, r_r, o_r) in enumerate(((qf, kf, vf, gf, rf, of), (qb, kb, vb, gb, rb, ob))):
        rev = d == 1
        incl = (colj >= row) if rev else (colj <= row)
        strict = (colj > row) if rev else (colj < row)
        gates = g_r[0]
        tri_c = jnp.where(incl, 1.0, 0.0).astype(BF16)
        g3 = jnp.concatenate(_split3(gates), axis=1)
        gcm3 = lax.dot_general(tri_c, g3, (((1,), (0,)), ((), ())), preferred_element_type=F32)
        gcm = gcm3[:, :LANES] + gcm3[:, LANES:2 * LANES] + gcm3[:, 2 * LANES:]
        nrow = r_r.shape[1]
        r3 = jnp.concatenate(_split3(r_r[0]), axis=0)
        gcr3 = lax.dot_general(r3, tri_c, nt_dims, preferred_element_type=F32)
        gcr = gcr3[:nrow] + gcr3[nrow:2 * nrow] + gcr3[2 * nrow:]
        for hq in range(NQK_HEADS):
            q = q_r[0, :, hq * HEAD_DIM:(hq + 1) * HEAD_DIM]
            k = k_r[0, :, hq * HEAD_DIM:(hq + 1) * HEAD_DIM]
            kq = lax.dot_general(jnp.concatenate([q, k], axis=0), k, nt_dims, preferred_element_type=F32)
            for j in range(per):
                h = hq * per + j
                idx = d * NV_HEADS + h
                gc_c = gcm[:, GATE_LANE0 + idx:GATE_LANE0 + idx + 1]
                seqs.append(dict(d=d, h=h, o_r=o_r, v_r=v_r, q=q, k=k, qk=kq[:c], kk=kq[c:], incl=incl, strict=strict,
                                 beta=gates[:, idx:idx + 1], gc_c=gc_c, gc_r=gcr[idx:idx + 1, :],
                                 ge=gc_c[0:1] if rev else gc_c[c - 1:c]))

    def same_block(m):
        sh = int(math.log2(m))
        return jnp.right_shift(row, sh) == jnp.right_shift(colj, sh)

    for s in seqs:
        s['decay'] = jnp.where(s['incl'], jnp.exp(jnp.where(s['incl'], s['gc_c'] - s['gc_r'], 0.0)), 0.0)
        s['a'] = jnp.where(s['strict'], s['beta'] * s['kk'] * s['decay'], 0.0)
        s['t'] = eye - jnp.where(same_block(2), s['a'], 0.0)
    m = 4
    while m <= c:
        between = jnp.logical_and(same_block(m), jnp.logical_not(same_block(m // 2)))
        for s in seqs:
            s['te'] = _mm(s['t'], jnp.where(between, s['a'], 0.0), prec)
        for s in seqs:
            s['t'] = s['t'] - _mm(s['te'], s['t'], prec)
        m *= 2
    for s in seqs:
        h = s['h']
        egc = jnp.exp(s['gc_c'])
        kf32 = s['k'].astype(F32)
        v = s['v_r'][0, :, h * HEAD_DIM:(h + 1) * HEAD_DIM].astype(F32)
        rhs = jnp.concatenate([s['beta'] * v, (s['beta'] * egc) * kf32], axis=1)
        s['sol'] = _mm(s['t'], rhs, prec)
        s['q_dec'] = s['q'].astype(F32) * egc
        s['k_dec'] = kf32 * jnp.exp(s['ge'] - s['gc_c'])
    for s in seqs:
        s['ws'] = _mm(jnp.concatenate([s['sol'][:, HEAD_DIM:], s['q_dec']], axis=0), s_ref[s['d'], s['h']])
    for s in seqs:
        s['u'] = s['sol'][:, :HEAD_DIM] - s['ws'][:c]
        s_ref[s['d'], s['h']] = (jnp.exp(s['ge']) * s_ref[s['d'], s['h']]
                                 + _mm(s['k_dec'], s['u'], dims=tn_dims))
    for s in seqs:
        h = s['h']
        o = s['ws'][c:] + _mm(s['qk'] * s['decay'], s['u'])
        s['o_r'][0, :, h * HEAD_DIM:(h + 1) * HEAD_DIM] = o.astype(s['o_r'].dtype)

    @pl.when(i == nc - 1)
    def _():
        sfin_ref[0] = s_ref[...]


def _gdn_call(q, k, v, go, rows, s0, prec):
    b, t, _ = q.shape
    c = GDN_CHUNK
    nc = t // c
    fwd = lambda i, n: (i, n, 0)
    bwd = lambda i, n: (i, nc - 1 - n, 0)
    rfwd = lambda i, n: (i, 0, n)
    rbwd = lambda i, n: (i, 0, nc - 1 - n)
    state_spec = pl.BlockSpec((1,) + s0.shape[1:], lambda i, n: (i, 0, 0, 0, 0))

    def specs(m3, mr):
        return [pl.BlockSpec((1, c, QK_DIM), m3), pl.BlockSpec((1, c, QK_DIM), m3),
                pl.BlockSpec((1, c, V_DIM), m3), pl.BlockSpec((1, c, LANES), m3),
                pl.BlockSpec((1, 2 * NV_HEADS, c), mr)]

    return pl.pallas_call(
        functools.partial(_gdn_kernel, prec=prec),
        grid=(b, nc),
        in_specs=specs(fwd, rfwd) + specs(bwd, rbwd) + [state_spec],
        out_specs=[pl.BlockSpec((1, c, V_DIM), fwd), pl.BlockSpec((1, c, V_DIM), bwd), state_spec],
        out_shape=[jax.ShapeDtypeStruct((b, t, V_DIM), BF16), jax.ShapeDtypeStruct((b, t, V_DIM), BF16),
                   jax.ShapeDtypeStruct(s0.shape, F32)],
        scratch_shapes=[pltpu.VMEM(s0.shape[1:], F32)],
        compiler_params=_cparams(("parallel", "arbitrary"), 48),
        name="gdn",
    )(q, k, v, go, rows, q, k, v, go, rows, s0)


def _fnet1_kernel(x_ref, f_ref, ar_ref, ai_ref):
    n = x_ref.shape[1]
    for j in range(x_ref.shape[2]):
        a = _mm(f_ref[...], x_ref[0, :, j, :])
        ar_ref[0, j] = a[:n]
        ai_ref[0, j] = a[n:]


def _fnet2_kernel(ar_ref, ai_ref, g_ref, wc_ref, o_ref, *, scale):
    n = ar_ref.shape[1]
    zs = []
    for m in range(ar_ref.shape[2]):
        a2 = jnp.concatenate([ar_ref[0, :, m, :], ai_ref[0, :, m, :]], axis=0)
        zs.append(_mm(g_ref[m], a2))
    for m, z in enumerate(zs):
        y = _mm(jnp.concatenate([z[:n], z[n:]], axis=1), wc_ref[...])
        o_ref[0, :, m, :] = y * scale


def _fnet_tables(n, groups):
    a = np.arange(n)
    ang1 = 2.0 * np.pi * np.outer(a, a) / n
    f1 = np.concatenate([np.cos(ang1), -np.sin(ang1)], axis=0)
    m = a[:, None] + n * a[None, :]
    ang2 = 2.0 * np.pi * ((m[:, :, None] * a[None, None, :]) % (n * n)) / (n * n)
    gc, gs = np.cos(ang2), np.sin(ang2)
    g2 = np.concatenate([np.concatenate([gc, gs], axis=2), np.concatenate([-gs, gc], axis=2)], axis=1)
    angc = 2.0 * np.pi * np.outer(np.arange(HEAD_DIM), np.arange(HEAD_DIM)) / HEAD_DIM
    eye = np.eye(groups)
    wc = np.concatenate([np.kron(eye, np.cos(angc)), np.kron(eye, np.sin(angc))], axis=0)
    f = lambda x: jnp.asarray(x, F32).astype(BF16)
    return f(f1), f(g2), f(wc)


def _fnet_call(f):
    b, l, c = f.shape
    n = GRID_W
    assert l == n * n
    f1, g2, wc = _fnet_tables(n, c // HEAD_DIM)
    cols = SUBLANES
    ar, ai = pl.pallas_call(
        _fnet1_kernel,
        grid=(b, n // cols),
        in_specs=[pl.BlockSpec((1, n, cols, c), lambda i, j: (i, 0, j, 0)),
                  pl.BlockSpec((2 * n, n), lambda i, j: (0, 0))],
        out_specs=[pl.BlockSpec((1, cols, n, c), lambda i, j: (i, j, 0, 0))] * 2,
        out_shape=[jax.ShapeDtypeStruct((b, n, n, c), F32)] * 2,
        compiler_params=_cparams(("parallel", "parallel"), 32),
        name="fnet1",
    )(f.reshape(b, n, n, c), f1)
    out = pl.pallas_call(
        functools.partial(_fnet2_kernel, scale=1.0 / math.sqrt(l * HEAD_DIM)),
        grid=(b, n // cols),
        in_specs=[pl.BlockSpec((1, n, cols, c), lambda i, j: (i, 0, j, 0)),
                  pl.BlockSpec((1, n, cols, c), lambda i, j: (i, 0, j, 0)),
                  pl.BlockSpec((cols, 2 * n, 2 * n), lambda i, j: (j, 0, 0)),
                  pl.BlockSpec((2 * c, c), lambda i, j: (0, 0))],
        out_specs=pl.BlockSpec((1, n, cols, c), lambda i, j: (i, 0, j, 0)),
        out_shape=jax.ShapeDtypeStruct((b, n, n, c), F32),
        compiler_params=_cparams(("parallel", "parallel"), 32),
        name="fnet2",
    )(ar, ai, g2, wc)
    return out.reshape(b, l, c)


def _merge_kernel(of_ref, ob_ref, z_ref, fm_ref, gab_ref, x_ref, g1_ref, sh2_ref, sc2_ref, gn_ref, n2_ref,
                  wg_ref, wf_ref, wm_ref, wr_ref, br_ref, x1_ref, h2_ref, lg_ref):
    d = x_ref.shape[1]
    tm = x_ref.shape[0]
    halves = [slice(0, tm // 2), slice(tm // 2, tm)]
    yb_in = []
    for r in halves:
        o = of_ref[r, :].astype(F32) + ob_ref[r, :].astype(F32)
        z = z_ref[r, :].astype(F32)
        parts = []
        for h0 in range(0, V_DIM, HEAD_DIM):
            oh = o[:, h0:h0 + HEAD_DIM]
            parts.append(oh * lax.rsqrt(jnp.mean(oh * oh, axis=-1, keepdims=True) + EPS) * gn_ref[...])
        yb_in.append(jnp.concatenate(parts, axis=1) * (z * _sigmoid(z)))
    yb = [_mm(v, wg_ref[...]) for v in yb_in]
    ya = [_mm(fm_ref[r, :], wf_ref[...]) for r in halves]
    mixed = [_sigmoid(gab_ref[r, :d].astype(F32)) * a + _sigmoid(gab_ref[r, d:].astype(F32)) * b_
             for r, a, b_ in zip(halves, ya, yb)]
    mm = [_mm(v, wm_ref[...]) for v in mixed]
    w = wr_ref[...]
    w_hi = w.astype(BF16)
    w2 = jnp.concatenate([w_hi, (w - w_hi.astype(F32)).astype(BF16)], axis=1)
    for r, v in zip(halves, mm):
        x1 = x_ref[r, :] + g1_ref[0] * v
        x1_ref[r, :] = x1
        h2 = _rmsnorm(x1, n2_ref[...]) * (1.0 + sc2_ref[0]) + sh2_ref[0]
        h2_ref[r, :] = _pack_bf16_pairs(h2)
        h_hi = h2.astype(BF16)
        h_lo = (h2 - h_hi.astype(F32)).astype(BF16)
        part = _mm(h_hi, w2)
        lg_ref[r, :] = (part[:, :LANES] + part[:, LANES:]) + _mm(h_lo, w_hi) + br_ref[...]


def _merge_call(of, ob, z, fm, gab, x2, g1, sh2, sc2, gn, n2, wg, wf, wm, wr, br, tokens_per_batch, tm):
    n, d = x2.shape
    per = tokens_per_batch // tm
    tok = lambda wd: pl.BlockSpec((tm, wd), lambda i: (i, 0))
    vec = pl.BlockSpec((1, 1, d), lambda i: (i // per, 0, 0))
    full = lambda a: pl.BlockSpec(a.shape, lambda i: (0,) * a.ndim)
    return pl.pallas_call(
        _merge_kernel,
        grid=(n // tm,),
        in_specs=[tok(V_DIM), tok(V_DIM), tok(V_DIM), tok(F_DIM), tok(2 * d), tok(d), vec, vec, vec,
                  full(gn), full(n2), full(wg), full(wf), full(wm), full(wr), full(br)],
        out_specs=[tok(d), tok(d // 2), tok(LANES)],
        out_shape=[jax.ShapeDtypeStruct((n, d), F32), jax.ShapeDtypeStruct((n, d // 2), I32),
                   jax.ShapeDtypeStruct((n, LANES), F32)],
        compiler_params=_cparams(("parallel",), 56),
        name="merge",
    )(of, ob, z, fm, gab, x2, g1, sh2, sc2, gn, n2, wg, wf, wm, wr, br)


def _route_kernel(lg_ref, idx_ref, w_ref, rank_ref, cnt_ref, run_ref):
    i = pl.program_id(0)

    @pl.when(i == 0)
    def _():
        run_ref[...] = jnp.zeros_like(run_ref)

    l = lg_ref[...]
    tm = l.shape[0]
    lane = lax.broadcasted_iota(I32, l.shape, 1)
    vals, idxs = [], []
    for _ in range(TOP_K):
        m = jnp.max(l, axis=-1, keepdims=True)
        idx = jnp.min(jnp.where(l == m, lane, LANES), axis=-1, keepdims=True)
        vals.append(m)
        idxs.append(idx)
        l = jnp.where(lane == idx, NEG_BIG * 2.0, l)
    es = [jnp.exp(v - vals[0]) for v in vals]
    inv = 1.0 / (es[0] + es[1] + es[2] + es[3])
    picked = jnp.zeros(l.shape, F32)
    for idx in idxs:
        picked = picked + (lane == idx).astype(F32)
    r = lax.broadcasted_iota(I32, (tm, tm), 0)
    cidx = lax.broadcasted_iota(I32, (tm, tm), 1)
    before = _mm(jnp.where(cidx < r, 1.0, 0.0), picked) + run_ref[...]
    idx_out = jnp.zeros(l.shape, I32)
    w_out = jnp.zeros(l.shape, F32)
    rank_out = jnp.zeros(l.shape, F32)
    for k in range(TOP_K):
        rk = jnp.sum(jnp.where(lane == idxs[k], before, 0.0), axis=-1, keepdims=True)
        idx_out = jnp.where(lane == k, idxs[k], idx_out)
        w_out = jnp.where(lane == k, es[k] * inv, w_out)
        rank_out = jnp.where(lane == k, rk, rank_out)
    idx_ref[...] = jnp.transpose(idx_out.astype(F32))[:SUBLANES].astype(I32)
    w_ref[...] = w_out
    rank_ref[...] = jnp.transpose(rank_out)[:SUBLANES].astype(I32)
    run_ref[...] = run_ref[...] + jnp.sum(picked, axis=0, keepdims=True)
    cnt_ref[...] = run_ref[...]


def _route_call(logits, tm, start, n):
    first = start // tm
    tok = pl.BlockSpec((tm, LANES), lambda i: (i, 0))
    tok_t = pl.BlockSpec((SUBLANES, tm), lambda i: (0, i))
    return pl.pallas_call(
        _route_kernel,
        grid=(n // tm,),
        in_specs=[pl.BlockSpec((tm, LANES), lambda i: (first + i, 0))],
        out_specs=[tok_t, tok, tok_t, pl.BlockSpec((1, LANES), lambda i: (0, 0))],
        out_shape=[jax.ShapeDtypeStruct((SUBLANES, n), I32), jax.ShapeDtypeStruct((n, LANES), F32),
                   jax.ShapeDtypeStruct((SUBLANES, n), I32), jax.ShapeDtypeStruct((1, LANES), F32)],
        scratch_shapes=[pltpu.VMEM((1, LANES), F32)],
        compiler_params=_cparams(("arbitrary",), 32),
        name="route",
    )(logits)


def _sc_mesh():
    return plsc.VectorSubcoreMesh(core_axis_name="c", subcore_axis_name="s",
                                  num_cores=SC_CORES, num_subcores=SC_SUBCORES)


def _sc_worker_id():
    return lax.axis_index("s") * SC_CORES + lax.axis_index("c")


def _sc_scatter_rows(x, idx, n_out, start, n):
    d = x.shape[1]
    per_w = idx.shape[0] // SC_WORKERS
    nwin = per_w // SC_WINDOW
    assert per_w * SC_WORKERS == idx.shape[0] and nwin * SC_WINDOW == per_w and nwin % 2 == 0 and n % per_w == 0

    def body(x_hbm, idx_hbm, out_hbm, idx_v, rows_v, sem_r, sem_w):
        wid = _sc_worker_id()
        row0 = start + lax.rem(wid * per_w, n)
        pltpu.sync_copy(idx_hbm.at[wid], idx_v)

        def read(j, b):
            return pltpu.make_async_copy(x_hbm.at[pl.ds(row0 + j * SC_WINDOW, SC_WINDOW)], rows_v.at[b], sem_r.at[b])

        def write(j, b):
            return pltpu.make_async_copy(rows_v.at[b], out_hbm.at[idx_v.at[j]], sem_w.at[b])

        @pl.loop(0, nwin, step=2)
        def _(j):
            read(j, 0).start()
            read(j + 1, 1).start()
            read(j, 0).wait()
            write(j, 0).start()
            read(j + 1, 1).wait()
            write(j + 1, 1).start()
            write(j, 0).wait()
            write(j + 1, 1).wait()

    return pl.kernel(
        body, out_type=jax.ShapeDtypeStruct((n_out, d), x.dtype), mesh=_sc_mesh(),
        scratch_types=[pltpu.VMEM((nwin, SC_WINDOW), I32), pltpu.VMEM((2, SC_WINDOW, d), x.dtype),
                       pltpu.SemaphoreType.DMA((2,)), pltpu.SemaphoreType.DMA((2,))],
        name="sc_scatter_rows",
    )(x, idx.reshape(SC_WORKERS, nwin, SC_WINDOW))


def _sc_gather_rows(y, idx):
    d = y.shape[1]
    total = idx.shape[0]
    per_w = total // SC_WORKERS
    nwin = per_w // SC_WINDOW
    assert per_w * SC_WORKERS == total and nwin * SC_WINDOW == per_w and nwin % 2 == 0

    def body(y_hbm, idx_hbm, out_hbm, idx_v, rows_v, sem_r, sem_w):
        wid = _sc_worker_id()
        row0 = wid * per_w
        pltpu.sync_copy(idx_hbm.at[wid], idx_v)

        def read(j, b):
            return pltpu.make_async_copy(y_hbm.at[idx_v.at[j]], rows_v.at[b], sem_r.at[b])

        def write(j, b):
            return pltpu.make_async_copy(rows_v.at[b], out_hbm.at[pl.ds(row0 + j * SC_WINDOW, SC_WINDOW)], sem_w.at[b])

        @pl.loop(0, nwin, step=2)
        def _(j):
            read(j, 0).start()
            read(j + 1, 1).start()
            read(j, 0).wait()
            write(j, 0).start()
            read(j + 1, 1).wait()
            write(j + 1, 1).start()
            write(j, 0).wait()
            write(j + 1, 1).wait()

    return pl.kernel(
        body, out_type=jax.ShapeDtypeStruct((total, d), y.dtype), mesh=_sc_mesh(),
        scratch_types=[pltpu.VMEM((nwin, SC_WINDOW), I32), pltpu.VMEM((2, SC_WINDOW, d), y.dtype),
                       pltpu.SemaphoreType.DMA((2,)), pltpu.SemaphoreType.DMA((2,))],
        name="sc_gather_rows",
    )(y, idx.reshape(SC_WORKERS, nwin, SC_WINDOW))


def _expert_kernel(be_ref, nv_ref, nxt_ref, par_ref, x_ref, wg_hbm, wu_hbm, wd_hbm, bg_ref, bu_ref, bd_ref, y_ref,
                   wgf, wuf, wdf, wgb, wub, wdb, sem):
    i = pl.program_id(0)

    def fetch(e, s):
        return [pltpu.make_async_copy(w_hbm.at[e], w_f.at[s], sem.at[s])
                for w_hbm, w_f in ((wg_hbm, wgf), (wu_hbm, wuf), (wd_hbm, wdf))]

    @pl.when(i < nv_ref[0])
    def _():
        e = be_ref[i]
        s = par_ref[e]

        @pl.when(i == 0)
        def _():
            for cp in fetch(e, s):
                cp.start()

        @pl.when(jnp.logical_or(i == 0, be_ref[jnp.maximum(i - 1, 0)] != e))
        def _():
            for cp in fetch(e, s):
                cp.wait()
            wgb[...] = wgf[s].astype(BF16)
            wub[...] = wuf[s].astype(BF16)
            wdb[...] = wdf[s].astype(BF16)

            @pl.when(nxt_ref[e] >= 0)
            def _():
                for cp in fetch(nxt_ref[e], 1 - s):
                    cp.start()

        x = _unpack_bf16_pairs(x_ref[...]).astype(BF16)
        gate = jnp.minimum(_mm(x, wgb[...]) + bg_ref[0], SWIGLU_LIMIT)
        up = jnp.clip(_mm(x, wub[...]) + bu_ref[0], -SWIGLU_LIMIT, SWIGLU_LIMIT)
        act = (up + 1.0) * gate * _sigmoid(SWIGLU_ALPHA * gate)
        y_ref[...] = _pack_bf16_pairs(_mm(act, wdb[...]) + bd_ref[0])


def _expert_call(block_e, n_valid, next_e, parity, xs, w_gate, w_up, w_down, b_gate, b_up, b_down):
    nb = block_e.shape[0]
    tmb = EXPERT_BLOCK
    ne, d, de = w_gate.shape
    bspec = lambda s: pl.BlockSpec((1,) + s, lambda i, be, nv, nx, pa: (be[i], 0, 0))
    anyspec = pl.BlockSpec(memory_space=pl.ANY)
    grid_spec = pltpu.PrefetchScalarGridSpec(
        num_scalar_prefetch=4,
        grid=(nb,),
        in_specs=[pl.BlockSpec((tmb, d // 2), lambda i, be, nv, nx, pa: (i, 0)),
                  anyspec, anyspec, anyspec, bspec((1, de)), bspec((1, de)), bspec((1, d))],
        out_specs=pl.BlockSpec((tmb, d // 2), lambda i, be, nv, nx, pa: (i, 0)),
        scratch_shapes=[pltpu.VMEM((2, d, de), F32), pltpu.VMEM((2, d, de), F32), pltpu.VMEM((2, de, d), F32),
                        pltpu.VMEM((d, de), BF16), pltpu.VMEM((d, de), BF16), pltpu.VMEM((de, d), BF16),
                        pltpu.SemaphoreType.DMA((2,))],
    )
    return pl.pallas_call(
        _expert_kernel,
        grid_spec=grid_spec,
        out_shape=jax.ShapeDtypeStruct(xs.shape, xs.dtype),
        compiler_params=_cparams(("arbitrary",), 56),
        name="expert",
    )(block_e, n_valid, next_e, parity, xs, w_gate, w_up, w_down,
      b_gate.reshape(ne, 1, de), b_up.reshape(ne, 1, de), b_down.reshape(ne, 1, d))


def _combine_kernel(y0, y1, y2, y3, w_ref, x1_ref, g2_ref, fg_ref, o_ref):
    w = w_ref[...]
    ys = [_unpack_bf16_pairs(y[...]) for y in (y0, y1, y2, y3)]
    moe = (w[:, 0:1] * ys[0] + w[:, 1:2] * ys[1]) + (w[:, 2:3] * ys[2] + w[:, 3:4] * ys[3])
    o_ref[...] = _rmsnorm(x1_ref[...] + g2_ref[0] * moe, fg_ref[...])


def _combine_call(prev, y4, top_w, x1, g2, fg, tokens_per_batch, tm, start):
    n, d = x1.shape
    per = tokens_per_batch // tm
    nt = top_w.shape[0] // tm
    first = start // tm
    yspec = lambda k: pl.BlockSpec((tm, d // 2), lambda i: (k * nt + i, 0))
    in_specs = [yspec(0), yspec(1), yspec(2), yspec(3),
                pl.BlockSpec((tm, LANES), lambda i: (i, 0)),
                pl.BlockSpec((tm, d), lambda i: (first + i, 0)),
                pl.BlockSpec((1, 1, d), lambda i: ((first + i) // per, 0, 0)),
                pl.BlockSpec((1, d), lambda i: (0, 0))]
    args = [y4, y4, y4, y4, top_w, x1, g2, fg]
    kern, aliases = _combine_kernel, {}
    if prev is not None:
        in_specs.append(pl.BlockSpec(memory_space=pl.ANY))
        args.append(prev)
        kern = lambda *refs: _combine_kernel(*refs[:8], refs[9])
        aliases = {8: 0}
    return pl.pallas_call(
        kern,
        grid=(nt,),
        in_specs=in_specs,
        out_specs=pl.BlockSpec((tm, d), lambda i: (first + i, 0)),
        out_shape=jax.ShapeDtypeStruct((n, d), F32),
        input_output_aliases=aliases,
        compiler_params=_cparams(("parallel",), 48),
        name="combine",
    )(*args)


def _routing_tables(top_idx, rank, counts, n):
    tmb = EXPERT_BLOCK
    counts = counts.astype(I32)
    padded = (counts + tmb - 1) // tmb * tmb
    pad_end = jnp.cumsum(padded)
    pad_start = pad_end - padded
    n_blocks = -(-(n * TOP_K + N_EXPERTS * (tmb - 1)) // tmb)
    n_slots = n_blocks * tmb
    dest = rank
    for e in range(N_EXPERTS):
        dest = dest + jnp.where(top_idx == e, pad_start[e], 0)
    dest_flat = dest.reshape(-1)
    block_start = jnp.arange(n_blocks, dtype=I32) * tmb
    block_e = jnp.minimum(jnp.sum((pad_end[None, :] <= block_start[:, None]).astype(I32), axis=1), N_EXPERTS - 1)
    n_valid = (pad_end[-1:] // tmb).astype(I32)
    experts = jnp.arange(N_EXPERTS, dtype=I32)
    used = counts > 0
    later = jnp.where(jnp.logical_and(used[None, :], experts[None, :] > experts[:, None]), experts[None, :], N_EXPERTS)
    next_e = jnp.min(later, axis=1)
    next_e = jnp.where(next_e == N_EXPERTS, -1, next_e).astype(I32)
    parity = ((jnp.cumsum(used.astype(I32)) - used.astype(I32)) % 2).astype(I32)
    return block_e, n_valid, next_e, parity, dest_flat, n_slots


def _gdn_branch(xtok, sh, sc, norm_g, ws, dtypes, conv_w, par, tokens_per_batch, grid_w, use_rows,
                tm_proj, tm_conv, s0, prec):
    b = xtok.shape[0] // tokens_per_batch
    outs = _inproj_call(xtok, sh, sc, norm_g, ws, dtypes, tokens_per_batch, tm_proj)
    qkv, gates = outs[0], outs[-1]
    q, k, v, go, rows = _conv_call(qkv.reshape(b, tokens_per_batch, QKV_DIM), conv_w,
                                   gates.reshape(b, tokens_per_batch, LANES), par, grid_w, use_rows, tm_conv)
    o_f, o_b, s_fin = _gdn_call(q, k, v, go, rows, s0, prec)
    return outs, o_f, o_b, s_fin


def kernel(x, c, ctx, c_ctx, w_mod, b_mod, norm1_g, norm2_g, w_in, conv_w, a_log, dt_bias, gdn_norm_g,
           w_fourier_out, w_gdn_out, w_merge_out, w_router, b_router, w_gate, b_gate, w_up, b_up,
           w_down, b_down, final_norm_g):
    b, l, d = x.shape
    n = b * l
    n_ctx = ctx.shape[1]
    assert w_mod.shape[0] == 1 and l == GRID_W * GRID_W and d == V_DIM
    prec = None

    c8 = jnp.concatenate([c, c_ctx[None, :], jnp.zeros((8 - b - 1, d), F32)], axis=0)
    mod = _mod_call(c8, w_mod[0], b_mod[0])
    sh1, sc1, g1, sh2, sc2, g2 = [mod[:b, j * d:(j + 1) * d].reshape(b, 1, d) for j in range(6)]
    csh1 = jnp.broadcast_to(mod[b:b + 1, 0:d].reshape(1, 1, d), (b, 1, d))
    csc1 = jnp.broadcast_to(mod[b:b + 1, d:2 * d].reshape(1, 1, d), (b, 1, d))

    wi = w_in[0]
    off_gate = QKV_DIM
    off_z = off_gate + 4 * NV_HEADS
    off_f = off_z + V_DIM
    off_ga = off_f + F_DIM
    w_qkv = wi[:, :QKV_DIM].astype(BF16)
    w_gates = jnp.pad(wi[:, off_gate:off_z], ((0, 0), (0, LANES - 4 * NV_HEADS))).astype(BF16)
    w_lat = [w_qkv, wi[:, off_z:off_f].astype(BF16), wi[:, off_f:off_ga].astype(BF16), wi[:, off_ga:].astype(BF16),
             w_gates]
    w_ctx = [w_qkv, w_gates]
    par = jnp.pad(jnp.stack([a_log[0].reshape(-1), dt_bias[0].reshape(-1)]),
                  ((0, 6), (GATE_LANE0, LANES - 2 * GATE_LANE0)))
    n1 = norm1_g[0].reshape(1, d)
    cw = conv_w[0].reshape(9, QKV_DIM)

    zero_state = jnp.zeros((b, 2, NV_HEADS, HEAD_DIM, HEAD_DIM), F32)
    _, _, _, s_ctx = _gdn_branch(ctx.reshape(b * n_ctx, d), csh1, csc1, n1, w_ctx, (BF16, F32),
                                 cw, par, n_ctx, n_ctx, False, n_ctx, n_ctx, zero_state, prec)

    x2 = x.reshape(n, d)
    outs, o_f, o_b, _ = _gdn_branch(x2, sh1, sc1, n1, w_lat, (BF16, BF16, F32, BF16, F32),
                                    cw, par, l, GRID_W, True, 512, 512, s_ctx, prec)
    _, z, f, gab, _ = outs
    fmix = _fnet_call(f.reshape(b, l, F_DIM)).reshape(n, F_DIM)

    wr = jnp.pad(w_router[0], ((0, 0), (0, LANES - N_EXPERTS)))
    br = jnp.pad(b_router[0], (0, LANES - N_EXPERTS), constant_values=NEG_BIG).reshape(1, LANES)
    x1, h2, logits = _merge_call(
        o_f.reshape(n, V_DIM), o_b.reshape(n, V_DIM), z, fmix, gab, x2, g1, sh2, sc2,
        gdn_norm_g[0].reshape(1, HEAD_DIM), norm2_g[0].reshape(1, d),
        w_gdn_out[0].astype(BF16), w_fourier_out[0].astype(BF16), w_merge_out[0].astype(BF16), wr, br, l, 256)

    out = None
    part = n // MOE_PARTS
    for start in range(0, n, part):
        top_idx, top_w, rank, counts = _route_call(logits, 512, start, part)
        block_e, n_valid, next_e, parity, dest_flat, n_slots = _routing_tables(
            top_idx[:TOP_K], rank[:TOP_K], counts[0, :N_EXPERTS], part)
        xs = _sc_scatter_rows(h2, dest_flat, n_slots, start, part)
        ys = _expert_call(block_e, n_valid, next_e, parity, xs, w_gate[0], w_up[0], w_down[0],
                          b_gate[0], b_up[0], b_down[0])
        y4 = _sc_gather_rows(ys, dest_flat)
        out = _combine_call(out, y4, top_w, x1, g2, final_norm_g.reshape(1, d), l, 256, start)
    return out.reshape(b, l, d)
```

```python
import functools
import math

import jax
import jax.numpy as jnp
import numpy as np
from jax import lax
from jax.experimental import pallas as pl
from jax.experimental.pallas import tpu as pltpu
from jax.experimental.pallas import tpu_sc as plsc

F32 = jnp.float32
BF16 = jnp.bfloat16
I32 = jnp.int32
HIGHEST = lax.Precision.HIGHEST

GRID_W = 64
NQK_HEADS = 4
NV_HEADS = 8
HEAD_DIM = 128
QK_DIM = NQK_HEADS * HEAD_DIM
V_DIM = NV_HEADS * HEAD_DIM
QKV_DIM = 2 * QK_DIM + V_DIM
F_GROUPS = 4
F_DIM = F_GROUPS * HEAD_DIM
N_EXPERTS = 32
TOP_K = 4
SWIGLU_ALPHA = 1.702
SWIGLU_LIMIT = 7.0
EPS = 1e-6

LANES = 128
SUBLANES = 8
GATE_LANE0 = 16
GDN_CHUNK = 128
EXPERT_BLOCK = 512
MOE_PARTS = 1
SC_CORES = 2
SC_SUBCORES = 16
SC_WORKERS = SC_CORES * SC_SUBCORES
SC_WINDOW = 64
NEG_BIG = -1e30
MIB = 2 ** 20


def _cparams(sem, vmem_mib):
    return pltpu.CompilerParams(dimension_semantics=sem, vmem_limit_bytes=vmem_mib * MIB)


def _mm(a, b, prec=None, dims=(((1,), (0,)), ((), ()))):
    if prec is None:
        return lax.dot_general(a.astype(BF16), b.astype(BF16), dims, preferred_element_type=F32)
    return lax.dot_general(a.astype(F32), b.astype(F32), dims, precision=prec, preferred_element_type=F32)


def _sigmoid(x):
    return 1.0 / (1.0 + jnp.exp(-x))


def _rmsnorm(x, g):
    return x * lax.rsqrt(jnp.mean(x * x, axis=-1, keepdims=True) + EPS) * g


def _split3(x):
    hi = x.astype(BF16)
    rest = x - hi.astype(F32)
    mid = rest.astype(BF16)
    return hi, mid, (rest - mid.astype(F32)).astype(BF16)


def _pack_bf16_pairs(x):
    half = x.shape[1] // 2
    bits = lax.bitcast_convert_type(x.astype(BF16).astype(F32), jnp.uint32)
    packed = jnp.bitwise_or(jnp.right_shift(bits[:, :half], jnp.uint32(16)),
                            jnp.bitwise_and(bits[:, half:], jnp.uint32(0xFFFF0000)))
    return lax.bitcast_convert_type(packed, I32)


def _unpack_bf16_pairs(p):
    bits = lax.bitcast_convert_type(p, jnp.uint32)
    lo = lax.bitcast_convert_type(jnp.left_shift(bits, jnp.uint32(16)), F32)
    hi = lax.bitcast_convert_type(jnp.bitwise_and(bits, jnp.uint32(0xFFFF0000)), F32)
    return jnp.concatenate([lo, hi], axis=1)


def _mod_kernel(c_ref, w_ref, b_ref, o_ref):
    c = c_ref[...]
    o_ref[...] = _mm(c * _sigmoid(c), w_ref[...], HIGHEST) + b_ref[...]


def _mod_call(c8, w_mod, b_mod):
    d, n = w_mod.shape
    tn = 1536
    return pl.pallas_call(
        _mod_kernel,
        grid=(n // tn,),
        in_specs=[pl.BlockSpec((8, d), lambda j: (0, 0)),
                  pl.BlockSpec((d, tn), lambda j: (0, j)),
                  pl.BlockSpec((1, tn), lambda j: (0, j))],
        out_specs=pl.BlockSpec((8, tn), lambda j: (0, j)),
        out_shape=jax.ShapeDtypeStruct((8, n), F32),
        compiler_params=_cparams(("parallel",), 32),
        name="mod",
    )(c8, w_mod, b_mod.reshape(1, n))


def _inconv_kernel(*refs, grid_w, use_rows, tm, cw, n_extra):
    refs = list(refs)
    prev_ref = refs.pop(0) if use_rows else None
    x_ref = refs.pop(0)
    next_ref = refs.pop(0) if use_rows else None
    sh_ref, sc_ref, g_ref, wq_ref, wgate_ref = refs[:5]
    wx_refs = refs[5:5 + n_extra]
    w_ref, par_ref, q_ref, k_ref, v_ref, go_ref, gr_ref = refs[5 + n_extra:12 + n_extra]
    ox_refs = refs[12 + n_extra:]

    def modulated(x):
        return (_rmsnorm(x, g_ref[...]) * (1.0 + sc_ref[0]) + sh_ref[0]).astype(BF16)

    r = pl.program_id(1)
    nr = pl.num_programs(1)
    u = modulated(x_ref[0])
    for wx_ref, o_ref in zip(wx_refs, ox_refs):
        width = wx_ref.shape[1]
        step = min(width, 512)
        for c0 in range(0, width, step):
            o_ref[0, :, c0:c0 + step] = _mm(u, wx_ref[:, c0:c0 + step]).astype(o_ref.dtype)

    t = lax.broadcasted_iota(I32, (tm, 1), 0)
    col = jnp.bitwise_and(t, grid_w - 1)
    m_left = (col != 0).astype(F32)
    m_right = (col != grid_w - 1).astype(F32)
    if use_rows:
        u_prev = modulated(prev_ref[0])
        u_next = modulated(next_ref[0])
        has_prev = (r > 0).astype(F32)
        has_next = (r < nr - 1).astype(F32)
    for c0 in range(0, QKV_DIM, cw):
        wq = wq_ref[:, c0:c0 + cw]
        xm = _mm(u, wq)
        if use_rows:
            up = jnp.concatenate([_mm(u_prev, wq) * has_prev, xm[:tm - grid_w]], axis=0)
            dn = jnp.concatenate([xm[grid_w:], _mm(u_next, wq) * has_next], axis=0)

        def colsum(kc):
            y = xm * w_ref[3 + kc:4 + kc, c0:c0 + cw]
            if use_rows:
                y = y + up * w_ref[kc:kc + 1, c0:c0 + cw] + dn * w_ref[6 + kc:7 + kc, c0:c0 + cw]
            return y

        acc = (colsum(1) + pltpu.roll(colsum(0), 1, axis=0) * m_left
               + pltpu.roll(colsum(2), tm - 1, axis=0) * m_right)
        s = acc * _sigmoid(acc)
        for h0 in range(0, cw, HEAD_DIM):
            c = c0 + h0
            seg = s[:, h0:h0 + HEAD_DIM]
            if c < 2 * QK_DIM:
                seg = seg * lax.rsqrt(jnp.sum(seg * seg, axis=-1, keepdims=True) + EPS)
            if c < QK_DIM:
                q_ref[0, :, c:c + HEAD_DIM] = (seg * HEAD_DIM ** -0.5).astype(q_ref.dtype)
            elif c < 2 * QK_DIM:
                k_ref[0, :, c - QK_DIM:c - QK_DIM + HEAD_DIM] = seg.astype(k_ref.dtype)
            else:
                v_ref[0, :, c - 2 * QK_DIM:c - 2 * QK_DIM + HEAD_DIM] = seg.astype(v_ref.dtype)
    g = _mm(u, wgate_ref[...])
    a = g + par_ref[1:2, :]
    softplus = jnp.maximum(a, 0.0) + jnp.log1p(jnp.exp(-jnp.abs(a)))
    log_g = -jnp.exp(par_ref[0:1, :]) * softplus
    lane = lax.broadcasted_iota(I32, g.shape, 1)
    go_ref[0] = jnp.where(lane < GATE_LANE0, _sigmoid(g), log_g)
    gr_ref[0] = jnp.transpose(log_g)[GATE_LANE0:GATE_LANE0 + 2 * NV_HEADS]


def _inconv_call(x3, sh, sc, g, w_qkv, w_gates, w_extra, extra_dtypes, conv_w, par, grid_w, use_rows, tm):
    b, t, d = x3.shape
    kern = functools.partial(_inconv_kernel, grid_w=grid_w, use_rows=use_rows, tm=tm, cw=512, n_extra=len(w_extra))
    per = tm // grid_w
    nrow = t // grid_w
    tile = lambda wd: pl.BlockSpec((1, tm, wd), lambda i, r: (i, r, 0))
    vec = pl.BlockSpec((1, 1, d), lambda i, r: (i, 0, 0))
    const = lambda a: pl.BlockSpec(a.shape, lambda i, r: (0,) * a.ndim, pipeline_mode=pl.Buffered(1))
    in_specs, args = [], []
    if use_rows:
        in_specs.append(pl.BlockSpec((1, grid_w, d), lambda i, r: (i, jnp.maximum(r * per - 1, 0), 0)))
        args.append(x3)
    in_specs.append(tile(d))
    args.append(x3)
    if use_rows:
        in_specs.append(pl.BlockSpec((1, grid_w, d), lambda i, r: (i, jnp.minimum((r + 1) * per, nrow - 1), 0)))
        args.append(x3)
    consts = [g, w_qkv, w_gates, *w_extra, conv_w, par]
    in_specs += [vec, vec] + [const(a) for a in consts]
    args += [sh, sc] + consts
    widths = [QK_DIM, QK_DIM, V_DIM, LANES] + [w.shape[1] for w in w_extra]
    dtypes = [BF16, BF16, BF16, F32] + list(extra_dtypes)
    out_shape = [jax.ShapeDtypeStruct((b, t, wd), dt) for wd, dt in zip(widths, dtypes)]
    out_specs = [tile(wd) for wd in widths]
    out_shape.insert(4, jax.ShapeDtypeStruct((b, 2 * NV_HEADS, t), F32))
    out_specs.insert(4, pl.BlockSpec((1, 2 * NV_HEADS, tm), lambda i, r: (i, 0, r)))
    return pl.pallas_call(
        kern, grid=(b, t // tm), in_specs=in_specs, out_specs=out_specs, out_shape=out_shape,
        compiler_params=_cparams(("parallel", "parallel"), 56),
        name="inconv_rows" if use_rows else "inconv_seq",
    )(*args)


def _gdn_kernel(qf, kf, vf, gf, rf, qb, kb, vb, gb, rb, s0_ref, of, ob, sfin_ref, s_ref, *, prec):
    i = pl.program_id(1)
    nc = pl.num_programs(1)

    @pl.when(i == 0)
    def _():
        s_ref[...] = s0_ref[0]

    c = qf.shape[1]
    per = NV_HEADS // NQK_HEADS
    row = lax.broadcasted_iota(I32, (c, c), 0)
    colj = lax.broadcasted_iota(I32, (c, c), 1)
    eye = jnp.where(row == colj, 1.0, 0.0)
    nt_dims = (((1,), (1,)), ((), ()))
    tn_dims = (((0,), (0,)), ((), ()))

    seqs = []
    for d, (q_r, k_r, v_r, g_r, r_r, o_r) in enumerate(((qf, kf, vf, gf, rf, of), (qb, kb, vb, gb, rb, ob))):
        rev = d == 1
        incl = (colj >= row) if rev else (colj <= row)
        strict = (colj > row) if rev else (colj < row)
        gates = g_r[0]
        tri_c = jnp.where(incl, 1.0, 0.0).astype(BF16)
        g3 = jnp.concatenate(_split3(gates), axis=1)
        gcm3 = lax.dot_general(tri_c, g3, (((1,), (0,)), ((), ())), preferred_element_type=F32)
        gcm = gcm3[:, :LANES] + gcm3[:, LANES:2 * LANES] + gcm3[:, 2 * LANES:]
        nrow = r_r.shape[1]
        r3 = jnp.concatenate(_split3(r_r[0]), axis=0)
        gcr3 = lax.dot_general(r3, tri_c, nt_dims, preferred_element_type=F32)
        gcr = gcr3[:nrow] + gcr3[nrow:2 * nrow] + gcr3[2 * nrow:]
        for hq in range(NQK_HEADS):
            q = q_r[0, :, hq * HEAD_DIM:(hq + 1) * HEAD_DIM]
            k = k_r[0, :, hq * HEAD_DIM:(hq + 1) * HEAD_DIM]
            kq = lax.dot_general(jnp.concatenate([q, k], axis=0), k, nt_dims, preferred_element_type=F32)
            for j in range(per):
                h = hq * per + j
                idx = d * NV_HEADS + h
                gc_c = gcm[:, GATE_LANE0 + idx:GATE_LANE0 + idx + 1]
                seqs.append(dict(d=d, h=h, o_r=o_r, v_r=v_r, q=q, k=k, qk=kq[:c], kk=kq[c:], incl=incl, strict=strict,
                                 beta=gates[:, idx:idx + 1], gc_c=gc_c, gc_r=gcr[idx:idx + 1, :],
                                 ge=gc_c[0:1] if rev else gc_c[c - 1:c]))

    def same_block(m):
        sh = int(math.log2(m))
        return jnp.right_shift(row, sh) == jnp.right_shift(colj, sh)

    for s in seqs:
        s['decay'] = jnp.where(s['incl'], jnp.exp(jnp.where(s['incl'], s['gc_c'] - s['gc_r'], 0.0)), 0.0)
        s['a'] = jnp.where(s['strict'], s['beta'] * s['kk'] * s['decay'], 0.0)
        s['t'] = eye - jnp.where(same_block(2), s['a'], 0.0)
    m = 4
    while m <= c:
        between = jnp.logical_and(same_block(m), jnp.logical_not(same_block(m // 2)))
        for s in seqs:
            s['te'] = _mm(s['t'], jnp.where(between, s['a'], 0.0), prec)
        for s in seqs:
            s['t'] = s['t'] - _mm(s['te'], s['t'], prec)
        m *= 2
    for s in seqs:
        h = s['h']
        egc = jnp.exp(s['gc_c'])
        kf32 = s['k'].astype(F32)
        v = s['v_r'][0, :, h * HEAD_DIM:(h + 1) * HEAD_DIM].astype(F32)
        rhs = jnp.concatenate([s['beta'] * v, (s['beta'] * egc) * kf32], axis=1)
        s['sol'] = _mm(s['t'], rhs, prec)
        s['q_dec'] = s['q'].astype(F32) * egc
        s['k_dec'] = kf32 * jnp.exp(s['ge'] - s['gc_c'])
    for s in seqs:
        s['ws'] = _mm(jnp.concatenate([s['sol'][:, HEAD_DIM:], s['q_dec']], axis=0), s_ref[s['d'], s['h']])
    for s in seqs:
        s['u'] = s['sol'][:, :HEAD_DIM] - s['ws'][:c]
        s_ref[s['d'], s['h']] = (jnp.exp(s['ge']) * s_ref[s['d'], s['h']]
                                 + _mm(s['k_dec'], s['u'], dims=tn_dims))
    for s in seqs:
        h = s['h']
        o = s['ws'][c:] + _mm(s['qk'] * s['decay'], s['u'])
        s['o_r'][0, :, h * HEAD_DIM:(h + 1) * HEAD_DIM] = o.astype(s['o_r'].dtype)

    @pl.when(i == nc - 1)
    def _():
        sfin_ref[0] = s_ref[...]


def _gdn_call(q, k, v, go, rows, s0, prec):
    b, t, _ = q.shape
    c = GDN_CHUNK
    nc = t // c
    fwd = lambda i, n: (i, n, 0)
    bwd = lambda i, n: (i, nc - 1 - n, 0)
    rfwd = lambda i, n: (i, 0, n)
    rbwd = lambda i, n: (i, 0, nc - 1 - n)
    state_spec = pl.BlockSpec((1,) + s0.shape[1:], lambda i, n: (i, 0, 0, 0, 0))

    def specs(m3, mr):
        return [pl.BlockSpec((1, c, QK_DIM), m3), pl.BlockSpec((1, c, QK_DIM), m3),
                pl.BlockSpec((1, c, V_DIM), m3), pl.BlockSpec((1, c, LANES), m3),
                pl.BlockSpec((1, 2 * NV_HEADS, c), mr)]

    return pl.pallas_call(
        functools.partial(_gdn_kernel, prec=prec),
        grid=(b, nc),
        in_specs=specs(fwd, rfwd) + specs(bwd, rbwd) + [state_spec],
        out_specs=[pl.BlockSpec((1, c, V_DIM), fwd), pl.BlockSpec((1, c, V_DIM), bwd), state_spec],
        out_shape=[jax.ShapeDtypeStruct((b, t, V_DIM), BF16), jax.ShapeDtypeStruct((b, t, V_DIM), BF16),
                   jax.ShapeDtypeStruct(s0.shape, F32)],
        scratch_shapes=[pltpu.VMEM(s0.shape[1:], F32)],
        compiler_params=_cparams(("parallel", "arbitrary"), 48),
        name="gdn",
    )(q, k, v, go, rows, q, k, v, go, rows, s0)


def _fnet1_kernel(x_ref, f_ref, ar_ref, ai_ref):
    n = x_ref.shape[1]
    for j in range(x_ref.shape[2]):
        a = _mm(f_ref[...], x_ref[0, :, j, :])
        ar_ref[0, j] = a[:n]
        ai_ref[0, j] = a[n:]


def _fnet2_kernel(ar_ref, ai_ref, g_ref, wc_ref, o_ref, *, scale):
    n = ar_ref.shape[1]
    zs = []
    for m in range(ar_ref.shape[2]):
        a2 = jnp.concatenate([ar_ref[0, :, m, :], ai_ref[0, :, m, :]], axis=0)
        zs.append(_mm(g_ref[m], a2))
    for m, z in enumerate(zs):
        y = _mm(jnp.concatenate([z[:n], z[n:]], axis=1), wc_ref[...])
        o_ref[0, :, m, :] = y * scale


def _fnet_tables(n, groups):
    a = np.arange(n)
    ang1 = 2.0 * np.pi * np.outer(a, a) / n
    f1 = np.concatenate([np.cos(ang1), -np.sin(ang1)], axis=0)
    m = a[:, None] + n * a[None, :]
    ang2 = 2.0 * np.pi * ((m[:, :, None] * a[None, None, :]) % (n * n)) / (n * n)
    gc, gs = np.cos(ang2), np.sin(ang2)
    g2 = np.concatenate([np.concatenate([gc, gs], axis=2), np.concatenate([-gs, gc], axis=2)], axis=1)
    angc = 2.0 * np.pi * np.outer(np.arange(HEAD_DIM), np.arange(HEAD_DIM)) / HEAD_DIM
    eye = np.eye(groups)
    wc = np.concatenate([np.kron(eye, np.cos(angc)), np.kron(eye, np.sin(angc))], axis=0)
    f = lambda x: jnp.asarray(x, F32).astype(BF16)
    return f(f1), f(g2), f(wc)


def _fnet_call(f):
    b, l, c = f.shape
    n = GRID_W
    assert l == n * n
    f1, g2, wc = _fnet_tables(n, c // HEAD_DIM)
    cols = SUBLANES
    ar, ai = pl.pallas_call(
        _fnet1_kernel,
        grid=(b, n // cols),
        in_specs=[pl.BlockSpec((1, n, cols, c), lambda i, j: (i, 0, j, 0)),
                  pl.BlockSpec((2 * n, n), lambda i, j: (0, 0))],
        out_specs=[pl.BlockSpec((1, cols, n, c), lambda i, j: (i, j, 0, 0))] * 2,
        out_shape=[jax.ShapeDtypeStruct((b, n, n, c), F32)] * 2,
        compiler_params=_cparams(("parallel", "parallel"), 32),
        name="fnet1",
    )(f.reshape(b, n, n, c), f1)
    out = pl.pallas_call(
        functools.partial(_fnet2_kernel, scale=1.0 / math.sqrt(l * HEAD_DIM)),
        grid=(b, n // cols),
        in_specs=[pl.BlockSpec((1, n, cols, c), lambda i, j: (i, 0, j, 0)),
                  pl.BlockSpec((1, n, cols, c), lambda i, j: (i, 0, j, 0)),
                  pl.BlockSpec((cols, 2 * n, 2 * n), lambda i, j: (j, 0, 0)),
                  pl.BlockSpec((2 * c, c), lambda i, j: (0, 0))],
        out_specs=pl.BlockSpec((1, n, cols, c), lambda i, j: (i, 0, j, 0)),
        out_shape=jax.ShapeDtypeStruct((b, n, n, c), F32),
        compiler_params=_cparams(("parallel", "parallel"), 32),
        name="fnet2",
    )(ar, ai, g2, wc)
    return out.reshape(b, l, c)


def _merge_kernel(of_ref, ob_ref, z_ref, fm_ref, gab_ref, x_ref, g1_ref, sh2_ref, sc2_ref, gn_ref, n2_ref,
                  wg_ref, wf_ref, wm_ref, wr_ref, br_ref, x1_ref, h2_ref, lg_ref):
    d = x_ref.shape[1]
    tm = x_ref.shape[0]
    halves = [slice(0, tm // 2), slice(tm // 2, tm)]
    yb_in = []
    for r in halves:
        o = of_ref[r, :].astype(F32) + ob_ref[r, :].astype(F32)
        z = z_ref[r, :].astype(F32)
        parts = []
        for h0 in range(0, V_DIM, HEAD_DIM):
            oh = o[:, h0:h0 + HEAD_DIM]
            parts.append(oh * lax.rsqrt(jnp.mean(oh * oh, axis=-1, keepdims=True) + EPS) * gn_ref[...])
        yb_in.append(jnp.concatenate(parts, axis=1) * (z * _sigmoid(z)))
    yb = [_mm(v, wg_ref[...]) for v in yb_in]
    ya = [_mm(fm_ref[r, :], wf_ref[...]) for r in halves]
    mixed = [_sigmoid(gab_ref[r, :d].astype(F32)) * a + _sigmoid(gab_ref[r, d:].astype(F32)) * b_
             for r, a, b_ in zip(halves, ya, yb)]
    mm = [_mm(v, wm_ref[...]) for v in mixed]
    w = wr_ref[...]
    w_hi = w.astype(BF16)
    w2 = jnp.concatenate([w_hi, (w - w_hi.astype(F32)).astype(BF16)], axis=1)
    for r, v in zip(halves, mm):
        x1 = x_ref[r, :] + g1_ref[0] * v
        x1_ref[r, :] = x1
        h2 = _rmsnorm(x1, n2_ref[...]) * (1.0 + sc2_ref[0]) + sh2_ref[0]
        h2_ref[r, :] = _pack_bf16_pairs(h2)
        h_hi = h2.astype(BF16)
        h_lo = (h2 - h_hi.astype(F32)).astype(BF16)
        part = _mm(h_hi, w2)
        lg_ref[r, :] = (part[:, :LANES] + part[:, LANES:]) + _mm(h_lo, w_hi) + br_ref[...]


def _merge_call(of, ob, z, fm, gab, x2, g1, sh2, sc2, gn, n2, wg, wf, wm, wr, br, tokens_per_batch, tm):
    n, d = x2.shape
    per = tokens_per_batch // tm
    tok = lambda wd: pl.BlockSpec((tm, wd), lambda i: (i, 0))
    vec = pl.BlockSpec((1, 1, d), lambda i: (i // per, 0, 0))
    full = lambda a: pl.BlockSpec(a.shape, lambda i: (0,) * a.ndim)
    return pl.pallas_call(
        _merge_kernel,
        grid=(n // tm,),
        in_specs=[tok(V_DIM), tok(V_DIM), tok(V_DIM), tok(F_DIM), tok(2 * d), tok(d), vec, vec, vec,
                  full(gn), full(n2), full(wg), full(wf), full(wm), full(wr), full(br)],
        out_specs=[tok(d), tok(d // 2), tok(LANES)],
        out_shape=[jax.ShapeDtypeStruct((n, d), F32), jax.ShapeDtypeStruct((n, d // 2), I32),
                   jax.ShapeDtypeStruct((n, LANES), F32)],
        compiler_params=_cparams(("parallel",), 56),
        name="merge",
    )(of, ob, z, fm, gab, x2, g1, sh2, sc2, gn, n2, wg, wf, wm, wr, br)


def _route_kernel(lg_ref, idx_ref, w_ref, rank_ref, cnt_ref, run_ref):
    i = pl.program_id(0)

    @pl.when(i == 0)
    def _():
        run_ref[...] = jnp.zeros_like(run_ref)

    l = lg_ref[...]
    tm = l.shape[0]
    lane = lax.broadcasted_iota(I32, l.shape, 1)
    vals, idxs = [], []
    for _ in range(TOP_K):
        m = jnp.max(l, axis=-1, keepdims=True)
        idx = jnp.min(jnp.where(l == m, lane, LANES), axis=-1, keepdims=True)
        vals.append(m)
        idxs.append(idx)
        l = jnp.where(lane == idx, NEG_BIG * 2.0, l)
    es = [jnp.exp(v - vals[0]) for v in vals]
    inv = 1.0 / (es[0] + es[1] + es[2] + es[3])
    picked = jnp.zeros(l.shape, F32)
    for idx in idxs:
        picked = picked + (lane == idx).astype(F32)
    r = lax.broadcasted_iota(I32, (tm, tm), 0)
    cidx = lax.broadcasted_iota(I32, (tm, tm), 1)
    before = _mm(jnp.where(cidx < r, 1.0, 0.0), picked) + run_ref[...]
    idx_out = jnp.zeros(l.shape, I32)
    w_out = jnp.zeros(l.shape, F32)
    rank_out = jnp.zeros(l.shape, F32)
    for k in range(TOP_K):
        rk = jnp.sum(jnp.where(lane == idxs[k], before, 0.0), axis=-1, keepdims=True)
        idx_out = jnp.where(lane == k, idxs[k], idx_out)
        w_out = jnp.where(lane == k, es[k] * inv, w_out)
        rank_out = jnp.where(lane == k, rk, rank_out)
    idx_ref[...] = jnp.transpose(idx_out.astype(F32))[:SUBLANES].astype(I32)
    w_ref[...] = w_out
    rank_ref[...] = jnp.transpose(rank_out)[:SUBLANES].astype(I32)
    run_ref[...] = run_ref[...] + jnp.sum(picked, axis=0, keepdims=True)
    cnt_ref[...] = run_ref[...]


def _route_call(logits, tm, start, n):
    first = start // tm
    tok = pl.BlockSpec((tm, LANES), lambda i: (i, 0))
    tok_t = pl.BlockSpec((SUBLANES, tm), lambda i: (0, i))
    return pl.pallas_call(
        _route_kernel,
        grid=(n // tm,),
        in_specs=[pl.BlockSpec((tm, LANES), lambda i: (first + i, 0))],
        out_specs=[tok_t, tok, tok_t, pl.BlockSpec((1, LANES), lambda i: (0, 0))],
        out_shape=[jax.ShapeDtypeStruct((SUBLANES, n), I32), jax.ShapeDtypeStruct((n, LANES), F32),
                   jax.ShapeDtypeStruct((SUBLANES, n), I32), jax.ShapeDtypeStruct((1, LANES), F32)],
        scratch_shapes=[pltpu.VMEM((1, LANES), F32)],
        compiler_params=_cparams(("arbitrary",), 32),
        name="route",
    )(logits)


def _sc_mesh():
    return plsc.VectorSubcoreMesh(core_axis_name="c", subcore_axis_name="s",
                                  num_cores=SC_CORES, num_subcores=SC_SUBCORES)


def _sc_worker_id():
    return lax.axis_index("s") * SC_CORES + lax.axis_index("c")


def _sc_scatter_rows(x, idx, n_out, start, n):
    d = x.shape[1]
    per_w = idx.shape[0] // SC_WORKERS
    nwin = per_w // SC_WINDOW
    assert per_w * SC_WORKERS == idx.shape[0] and nwin * SC_WINDOW == per_w and nwin % 2 == 0 and n % per_w == 0

    def body(x_hbm, idx_hbm, out_hbm, idx_v, rows_v, sem_r, sem_w):
        wid = _sc_worker_id()
        row0 = start + lax.rem(wid * per_w, n)
        pltpu.sync_copy(idx_hbm.at[wid], idx_v)

        def read(j, b):
            return pltpu.make_async_copy(x_hbm.at[pl.ds(row0 + j * SC_WINDOW, SC_WINDOW)], rows_v.at[b], sem_r.at[b])

        def write(j, b):
            return pltpu.make_async_copy(rows_v.at[b], out_hbm.at[idx_v.at[j]], sem_w.at[b])

        @pl.loop(0, nwin, step=2)
        def _(j):
            read(j, 0).start()
            read(j + 1, 1).start()
            read(j, 0).wait()
            write(j, 0).start()
            read(j + 1, 1).wait()
            write(j + 1, 1).start()
            write(j, 0).wait()
            write(j + 1, 1).wait()

    return pl.kernel(
        body, out_type=jax.ShapeDtypeStruct((n_out, d), x.dtype), mesh=_sc_mesh(),
        scratch_types=[pltpu.VMEM((nwin, SC_WINDOW), I32), pltpu.VMEM((2, SC_WINDOW, d), x.dtype),
                       pltpu.SemaphoreType.DMA((2,)), pltpu.SemaphoreType.DMA((2,))],
        name="sc_scatter_rows",
    )(x, idx.reshape(SC_WORKERS, nwin, SC_WINDOW))


def _sc_gather_rows(y, idx):
    d = y.shape[1]
    total = idx.shape[0]
    per_w = total // SC_WORKERS
    nwin = per_w // SC_WINDOW
    assert per_w * SC_WORKERS == total and nwin * SC_WINDOW == per_w and nwin % 2 == 0

    def body(y_hbm, idx_hbm, out_hbm, idx_v, rows_v, sem_r, sem_w):
        wid = _sc_worker_id()
        row0 = wid * per_w
        pltpu.sync_copy(idx_hbm.at[wid], idx_v)

        def read(j, b):
            return pltpu.make_async_copy(y_hbm.at[idx_v.at[j]], rows_v.at[b], sem_r.at[b])

        def write(j, b):
            return pltpu.make_async_copy(rows_v.at[b], out_hbm.at[pl.ds(row0 + j * SC_WINDOW, SC_WINDOW)], sem_w.at[b])

        @pl.loop(0, nwin, step=2)
        def _(j):
            read(j, 0).start()
            read(j + 1, 1).start()
            read(j, 0).wait()
            write(j, 0).start()
            read(j + 1, 1).wait()
            write(j + 1, 1).start()
            write(j, 0).wait()
            write(j + 1, 1).wait()

    return pl.kernel(
        body, out_type=jax.ShapeDtypeStruct((total, d), y.dtype), mesh=_sc_mesh(),
        scratch_types=[pltpu.VMEM((nwin, SC_WINDOW), I32), pltpu.VMEM((2, SC_WINDOW, d), y.dtype),
                       pltpu.SemaphoreType.DMA((2,)), pltpu.SemaphoreType.DMA((2,))],
        name="sc_gather_rows",
    )(y, idx.reshape(SC_WORKERS, nwin, SC_WINDOW))


def _expert_kernel(be_ref, nv_ref, nxt_ref, par_ref, x_ref, wg_hbm, wu_hbm, wd_hbm, bg_ref, bu_ref, bd_ref, y_ref,
                   wgf, wuf, wdf, wgb, wub, wdb, sem):
    i = pl.program_id(0)

    def fetch(e, s):
        return [pltpu.make_async_copy(w_hbm.at[e], w_f.at[s], sem.at[s])
                for w_hbm, w_f in ((wg_hbm, wgf), (wu_hbm, wuf), (wd_hbm, wdf))]

    @pl.when(i < nv_ref[0])
    def _():
        e = be_ref[i]
        s = par_ref[e]

        @pl.when(i == 0)
        def _():
            for cp in fetch(e, s):
                cp.start()

        @pl.when(jnp.logical_or(i == 0, be_ref[jnp.maximum(i - 1, 0)] != e))
        def _():
            for cp in fetch(e, s):
                cp.wait()
            wgb[...] = wgf[s].astype(BF16)
            wub[...] = wuf[s].astype(BF16)
            wdb[...] = wdf[s].astype(BF16)

            @pl.when(nxt_ref[e] >= 0)
            def _():
                for cp in fetch(nxt_ref[e], 1 - s):
                    cp.start()

        x = _unpack_bf16_pairs(x_ref[...]).astype(BF16)
        gate = jnp.minimum(_mm(x, wgb[...]) + bg_ref[0], SWIGLU_LIMIT)
        up = jnp.clip(_mm(x, wub[...]) + bu_ref[0], -SWIGLU_LIMIT, SWIGLU_LIMIT)
        act = (up + 1.0) * gate * _sigmoid(SWIGLU_ALPHA * gate)
        y_ref[...] = _pack_bf16_pairs(_mm(act, wdb[...]) + bd_ref[0])


def _expert_call(block_e, n_valid, next_e, parity, xs, w_gate, w_up, w_down, b_gate, b_up, b_down):
    nb = block_e.shape[0]
    tmb = EXPERT_BLOCK
    ne, d, de = w_gate.shape
    bspec = lambda s: pl.BlockSpec((1,) + s, lambda i, be, nv, nx, pa: (be[i], 0, 0))
    anyspec = pl.BlockSpec(memory_space=pl.ANY)
    grid_spec = pltpu.PrefetchScalarGridSpec(
        num_scalar_prefetch=4,
        grid=(nb,),
        in_specs=[pl.BlockSpec((tmb, d // 2), lambda i, be, nv, nx, pa: (i, 0)),
                  anyspec, anyspec, anyspec, bspec((1, de)), bspec((1, de)), bspec((1, d))],
        out_specs=pl.BlockSpec((tmb, d // 2), lambda i, be, nv, nx, pa: (i, 0)),
        scratch_shapes=[pltpu.VMEM((2, d, de), F32), pltpu.VMEM((2, d, de), F32), pltpu.VMEM((2, de, d), F32),
                        pltpu.VMEM((d, de), BF16), pltpu.VMEM((d, de), BF16), pltpu.VMEM((de, d), BF16),
                        pltpu.SemaphoreType.DMA((2,))],
    )
    return pl.pallas_call(
        _expert_kernel,
        grid_spec=grid_spec,
        out_shape=jax.ShapeDtypeStruct(xs.shape, xs.dtype),
        compiler_params=_cparams(("arbitrary",), 56),
        name="expert",
    )(block_e, n_valid, next_e, parity, xs, w_gate, w_up, w_down,
      b_gate.reshape(ne, 1, de), b_up.reshape(ne, 1, de), b_down.reshape(ne, 1, d))


def _combine_kernel(y0, y1, y2, y3, w_ref, x1_ref, g2_ref, fg_ref, o_ref):
    w = w_ref[...]
    ys = [_unpack_bf16_pairs(y[...]) for y in (y0, y1, y2, y3)]
    moe = (w[:, 0:1] * ys[0] + w[:, 1:2] * ys[1]) + (w[:, 2:3] * ys[2] + w[:, 3:4] * ys[3])
    o_ref[...] = _rmsnorm(x1_ref[...] + g2_ref[0] * moe, fg_ref[...])


def _combine_call(prev, y4, top_w, x1, g2, fg, tokens_per_batch, tm, start):
    n, d = x1.shape
    per = tokens_per_batch // tm
    nt = top_w.shape[0] // tm
    first = start // tm
    yspec = lambda k: pl.BlockSpec((tm, d // 2), lambda i: (k * nt + i, 0))
    in_specs = [yspec(0), yspec(1), yspec(2), yspec(3),
                pl.BlockSpec((tm, LANES), lambda i: (i, 0)),
                pl.BlockSpec((tm, d), lambda i: (first + i, 0)),
                pl.BlockSpec((1, 1, d), lambda i: ((first + i) // per, 0, 0)),
                pl.BlockSpec((1, d), lambda i: (0, 0))]
    args = [y4, y4, y4, y4, top_w, x1, g2, fg]
    kern, aliases = _combine_kernel, {}
    if prev is not None:
        in_specs.append(pl.BlockSpec(memory_space=pl.ANY))
        args.append(prev)
        kern = lambda *refs: _combine_kernel(*refs[:8], refs[9])
        aliases = {8: 0}
    return pl.pallas_call(
        kern,
        grid=(nt,),
        in_specs=in_specs,
        out_specs=pl.BlockSpec((tm, d), lambda i: (first + i, 0)),
        out_shape=jax.ShapeDtypeStruct((n, d), F32),
        input_output_aliases=aliases,
        compiler_params=_cparams(("parallel",), 48),
        name="combine",
    )(*args)


def _routing_tables(top_idx, rank, counts, n):
    tmb = EXPERT_BLOCK
    counts = counts.astype(I32)
    padded = (counts + tmb - 1) // tmb * tmb
    pad_end = jnp.cumsum(padded)
    pad_start = pad_end - padded
    n_blocks = -(-(n * TOP_K + N_EXPERTS * (tmb - 1)) // tmb)
    n_slots = n_blocks * tmb
    dest = rank
    for e in range(N_EXPERTS):
        dest = dest + jnp.where(top_idx == e, pad_start[e], 0)
    dest_flat = dest.reshape(-1)
    block_start = jnp.arange(n_blocks, dtype=I32) * tmb
    block_e = jnp.minimum(jnp.sum((pad_end[None, :] <= block_start[:, None]).astype(I32), axis=1), N_EXPERTS - 1)
    n_valid = (pad_end[-1:] // tmb).astype(I32)
    experts = jnp.arange(N_EXPERTS, dtype=I32)
    used = counts > 0
    later = jnp.where(jnp.logical_and(used[None, :], experts[None, :] > experts[:, None]), experts[None, :], N_EXPERTS)
    next_e = jnp.min(later, axis=1)
    next_e = jnp.where(next_e == N_EXPERTS, -1, next_e).astype(I32)
    parity = ((jnp.cumsum(used.astype(I32)) - used.astype(I32)) % 2).astype(I32)
    return block_e, n_valid, next_e, parity, dest_flat, n_slots


def _gdn_branch(x3, sh, sc, norm_g, w_qkv, w_gates, w_extra, extra_dtypes, conv_w, par, grid_w, use_rows, tm,
                s0, prec):
    q, k, v, go, rows, *extra = _inconv_call(x3, sh, sc, norm_g, w_qkv, w_gates, w_extra, extra_dtypes, conv_w, par,
                                             grid_w, use_rows, tm)
    o_f, o_b, s_fin = _gdn_call(q, k, v, go, rows, s0, prec)
    return extra, o_f, o_b, s_fin


def kernel(x, c, ctx, c_ctx, w_mod, b_mod, norm1_g, norm2_g, w_in, conv_w, a_log, dt_bias, gdn_norm_g,
           w_fourier_out, w_gdn_out, w_merge_out, w_router, b_router, w_gate, b_gate, w_up, b_up,
           w_down, b_down, final_norm_g):
    b, l, d = x.shape
    n = b * l
    n_ctx = ctx.shape[1]
    assert w_mod.shape[0] == 1 and l == GRID_W * GRID_W and d == V_DIM
    prec = None

    c8 = jnp.concatenate([c, c_ctx[None, :], jnp.zeros((8 - b - 1, d), F32)], axis=0)
    mod = _mod_call(c8, w_mod[0], b_mod[0])
    sh1, sc1, g1, sh2, sc2, g2 = [mod[:b, j * d:(j + 1) * d].reshape(b, 1, d) for j in range(6)]
    csh1 = jnp.broadcast_to(mod[b:b + 1, 0:d].reshape(1, 1, d), (b, 1, d))
    csc1 = jnp.broadcast_to(mod[b:b + 1, d:2 * d].reshape(1, 1, d), (b, 1, d))

    wi = w_in[0]
    off_gate = QKV_DIM
    off_z = off_gate + 4 * NV_HEADS
    off_f = off_z + V_DIM
    off_ga = off_f + F_DIM
    w_qkv = wi[:, :QKV_DIM].astype(BF16)
    w_gates = jnp.pad(wi[:, off_gate:off_z], ((0, 0), (0, LANES - 4 * NV_HEADS))).astype(BF16)
    w_extra = [wi[:, off_z:off_f].astype(BF16), wi[:, off_f:off_ga].astype(BF16), wi[:, off_ga:].astype(BF16)]
    par = jnp.pad(jnp.stack([a_log[0].reshape(-1), dt_bias[0].reshape(-1)]),
                  ((0, 6), (GATE_LANE0, LANES - 2 * GATE_LANE0)))
    n1 = norm1_g[0].reshape(1, d)
    cw = conv_w[0].reshape(9, QKV_DIM)

    zero_state = jnp.zeros((b, 2, NV_HEADS, HEAD_DIM, HEAD_DIM), F32)
    _, _, _, s_ctx = _gdn_branch(ctx, csh1, csc1, n1, w_qkv, w_gates, [], [], cw, par, n_ctx, False, n_ctx,
                                 zero_state, prec)

    x2 = x.reshape(n, d)
    (z, f, gab), o_f, o_b, _ = _gdn_branch(x, sh1, sc1, n1, w_qkv, w_gates, w_extra, (BF16, F32, BF16),
                                           cw, par, GRID_W, True, 512, s_ctx, prec)
    z, gab = z.reshape(n, V_DIM), gab.reshape(n, 2 * d)
    fmix = _fnet_call(f).reshape(n, F_DIM)

    wr = jnp.pad(w_router[0], ((0, 0), (0, LANES - N_EXPERTS)))
    br = jnp.pad(b_router[0], (0, LANES - N_EXPERTS), constant_values=NEG_BIG).reshape(1, LANES)
    x1, h2, logits = _merge_call(
        o_f.reshape(n, V_DIM), o_b.reshape(n, V_DIM), z, fmix, gab, x2, g1, sh2, sc2,
        gdn_norm_g[0].reshape(1, HEAD_DIM), norm2_g[0].reshape(1, d),
        w_gdn_out[0].astype(BF16), w_fourier_out[0].astype(BF16), w_merge_out[0].astype(BF16), wr, br, l, 256)

    out = None
    part = n // MOE_PARTS
    for start in range(0, n, part):
        top_idx, top_w, rank, counts = _route_call(logits, 512, start, part)
        block_e, n_valid, next_e, parity, dest_flat, n_slots = _routing_tables(
            top_idx[:TOP_K], rank[:TOP_K], counts[0, :N_EXPERTS], part)
        xs = _sc_scatter_rows(h2, dest_flat, n_slots, start, part)
        ys = _expert_call(block_e, n_valid, next_e, parity, xs, w_gate[0], w_up[0], w_down[0],
                          b_gate[0], b_up[0], b_down[0])
        y4 = _sc_gather_rows(ys, dest_flat)
        out = _combine_call(out, y4, top_w, x1, g2, final_norm_g.reshape(1, d), l, 256, start)
    return out.reshape(b, l, d)
```

```python
import functools
import math

import jax
import jax.numpy as jnp
import numpy as np
from jax import lax
from jax.experimental import pallas as pl
from jax.experimental.pallas import tpu as pltpu
from jax.experimental.pallas import tpu_sc as plsc

F32 = jnp.float32
BF16 = jnp.bfloat16
I32 = jnp.int32
HIGHEST = lax.Precision.HIGHEST

GRID_W = 64
NQK_HEADS = 4
NV_HEADS = 8
HEAD_DIM = 128
QK_DIM = NQK_HEADS * HEAD_DIM
V_DIM = NV_HEADS * HEAD_DIM
QKV_DIM = 2 * QK_DIM + V_DIM
F_GROUPS = 4
F_DIM = F_GROUPS * HEAD_DIM
N_EXPERTS = 32
TOP_K = 4
SWIGLU_ALPHA = 1.702
SWIGLU_LIMIT = 7.0
EPS = 1e-6

LANES = 128
SUBLANES = 8
GATE_LANE0 = 16
GDN_CHUNK = 128
EXPERT_BLOCK = 512
MOE_PARTS = 1
SC_CORES = 2
SC_SUBCORES = 16
SC_WORKERS = SC_CORES * SC_SUBCORES
SC_WINDOW = 64
NEG_BIG = -1e30
MIB = 2 ** 20


def _cparams(sem, vmem_mib):
    return pltpu.CompilerParams(dimension_semantics=sem, vmem_limit_bytes=vmem_mib * MIB)


def _mm(a, b, prec=None, dims=(((1,), (0,)), ((), ()))):
    if prec is None:
        return lax.dot_general(a.astype(BF16), b.astype(BF16), dims, preferred_element_type=F32)
    return lax.dot_general(a.astype(F32), b.astype(F32), dims, precision=prec, preferred_element_type=F32)


def _sigmoid(x):
    return 1.0 / (1.0 + jnp.exp(-x))


def _rmsnorm(x, g):
    return x * lax.rsqrt(jnp.mean(x * x, axis=-1, keepdims=True) + EPS) * g


def _pack_bf16_pairs(x):
    half = x.shape[1] // 2
    bits = lax.bitcast_convert_type(x.astype(BF16).astype(F32), jnp.uint32)
    packed = jnp.bitwise_or(jnp.right_shift(bits[:, :half], jnp.uint32(16)),
                            jnp.bitwise_and(bits[:, half:], jnp.uint32(0xFFFF0000)))
    return lax.bitcast_convert_type(packed, I32)


def _unpack_bf16_pairs(p):
    bits = lax.bitcast_convert_type(p, jnp.uint32)
    lo = lax.bitcast_convert_type(jnp.left_shift(bits, jnp.uint32(16)), F32)
    hi = lax.bitcast_convert_type(jnp.bitwise_and(bits, jnp.uint32(0xFFFF0000)), F32)
    return jnp.concatenate([lo, hi], axis=1)


def _mod_kernel(c_ref, w_ref, b_ref, o_ref):
    c = c_ref[...]
    o_ref[...] = _mm(c * _sigmoid(c), w_ref[...], HIGHEST) + b_ref[...]


def _mod_call(c8, w_mod, b_mod):
    d, n = w_mod.shape
    tn = 1536
    return pl.pallas_call(
        _mod_kernel,
        grid=(n // tn,),
        in_specs=[pl.BlockSpec((8, d), lambda j: (0, 0)),
                  pl.BlockSpec((d, tn), lambda j: (0, j)),
                  pl.BlockSpec((1, tn), lambda j: (0, j))],
        out_specs=pl.BlockSpec((8, tn), lambda j: (0, j)),
        out_shape=jax.ShapeDtypeStruct((8, n), F32),
        compiler_params=_cparams(("parallel",), 32),
        name="mod",
    )(c8, w_mod, b_mod.reshape(1, n))


def _inconv_kernel(*refs, grid_w, use_rows, tm, cw, n_extra):
    refs = list(refs)
    prev_ref = refs.pop(0) if use_rows else None
    x_ref = refs.pop(0)
    next_ref = refs.pop(0) if use_rows else None
    sh_ref, sc_ref, g_ref, wq_ref, wgate_ref = refs[:5]
    wx_refs = refs[5:5 + n_extra]
    w_ref, par_ref, q_ref, k_ref, v_ref, go_ref, gr_ref = refs[5 + n_extra:12 + n_extra]
    ox_refs = refs[12 + n_extra:]

    def modulated(x):
        return (_rmsnorm(x, g_ref[...]) * (1.0 + sc_ref[0]) + sh_ref[0]).astype(BF16)

    r = pl.program_id(1)
    nr = pl.num_programs(1)
    u = modulated(x_ref[0])

    def project(wx_ref, o_ref, c0, step):
        o_ref[0, :, c0:c0 + step] = _mm(u, wx_ref[:, c0:c0 + step]).astype(o_ref.dtype)

    plain = [functools.partial(project, wx_ref, o_ref, c0, min(wx_ref.shape[1], 512))
             for wx_ref, o_ref in zip(wx_refs, ox_refs) for c0 in range(0, wx_ref.shape[1], min(wx_ref.shape[1], 512))]
    n_conv_chunks = QKV_DIM // cw

    t = lax.broadcasted_iota(I32, (tm, 1), 0)
    col = jnp.bitwise_and(t, grid_w - 1)
    m_left = (col != 0).astype(F32)
    m_right = (col != grid_w - 1).astype(F32)
    if use_rows:
        u_prev = modulated(prev_ref[0])
        u_next = modulated(next_ref[0])
        has_prev = (r > 0).astype(F32)
        has_next = (r < nr - 1).astype(F32)
    for c0 in range(0, QKV_DIM, cw):
        wq = wq_ref[:, c0:c0 + cw]
        xm = _mm(u, wq)
        if use_rows:
            up = jnp.concatenate([_mm(u_prev, wq) * has_prev, xm[:tm - grid_w]], axis=0)
            dn = jnp.concatenate([xm[grid_w:], _mm(u_next, wq) * has_next], axis=0)

        def colsum(kc):
            y = xm * w_ref[3 + kc:4 + kc, c0:c0 + cw]
            if use_rows:
                y = y + up * w_ref[kc:kc + 1, c0:c0 + cw] + dn * w_ref[6 + kc:7 + kc, c0:c0 + cw]
            return y

        acc = (colsum(1) + pltpu.roll(colsum(0), 1, axis=0) * m_left
               + pltpu.roll(colsum(2), tm - 1, axis=0) * m_right)
        s = acc * _sigmoid(acc)
        for h0 in range(0, cw, HEAD_DIM):
            c = c0 + h0
            seg = s[:, h0:h0 + HEAD_DIM]
            if c < 2 * QK_DIM:
                seg = seg * lax.rsqrt(jnp.sum(seg * seg, axis=-1, keepdims=True) + EPS)
            if c < QK_DIM:
                q_ref[0, :, c:c + HEAD_DIM] = (seg * HEAD_DIM ** -0.5).astype(q_ref.dtype)
            elif c < 2 * QK_DIM:
                k_ref[0, :, c - QK_DIM:c - QK_DIM + HEAD_DIM] = seg.astype(k_ref.dtype)
            else:
                v_ref[0, :, c - 2 * QK_DIM:c - 2 * QK_DIM + HEAD_DIM] = seg.astype(v_ref.dtype)
        ci = c0 // cw
        for job in plain[ci * len(plain) // n_conv_chunks:(ci + 1) * len(plain) // n_conv_chunks]:
            job()
    g = _mm(u, wgate_ref[...])
    a = g + par_ref[1:2, :]
    softplus = jnp.maximum(a, 0.0) + jnp.log1p(jnp.exp(-jnp.abs(a)))
    log_g = -jnp.exp(par_ref[0:1, :]) * softplus
    lane = lax.broadcasted_iota(I32, g.shape, 1)
    go_ref[0] = jnp.where(lane < GATE_LANE0, _sigmoid(g), log_g)
    gr_ref[0] = jnp.transpose(log_g)[GATE_LANE0:GATE_LANE0 + 2 * NV_HEADS]


def _inconv_call(x3, sh, sc, g, w_qkv, w_gates, w_extra, extra_dtypes, conv_w, par, grid_w, use_rows, tm):
    b, t, d = x3.shape
    kern = functools.partial(_inconv_kernel, grid_w=grid_w, use_rows=use_rows, tm=tm, cw=512, n_extra=len(w_extra))
    per = tm // grid_w
    nrow = t // grid_w
    tile = lambda wd: pl.BlockSpec((1, tm, wd), lambda i, r: (i, r, 0))
    vec = pl.BlockSpec((1, 1, d), lambda i, r: (i, 0, 0))
    const = lambda a: pl.BlockSpec(a.shape, lambda i, r: (0,) * a.ndim, pipeline_mode=pl.Buffered(1))
    in_specs, args = [], []
    if use_rows:
        in_specs.append(pl.BlockSpec((1, grid_w, d), lambda i, r: (i, jnp.maximum(r * per - 1, 0), 0)))
        args.append(x3)
    in_specs.append(tile(d))
    args.append(x3)
    if use_rows:
        in_specs.append(pl.BlockSpec((1, grid_w, d), lambda i, r: (i, jnp.minimum((r + 1) * per, nrow - 1), 0)))
        args.append(x3)
    consts = [g, w_qkv, w_gates, *w_extra, conv_w, par]
    in_specs += [vec, vec] + [const(a) for a in consts]
    args += [sh, sc] + consts
    widths = [QK_DIM, QK_DIM, V_DIM, LANES] + [w.shape[1] for w in w_extra]
    dtypes = [BF16, BF16, BF16, F32] + list(extra_dtypes)
    out_shape = [jax.ShapeDtypeStruct((b, t, wd), dt) for wd, dt in zip(widths, dtypes)]
    out_specs = [tile(wd) for wd in widths]
    out_shape.insert(4, jax.ShapeDtypeStruct((b, 2 * NV_HEADS, t), F32))
    out_specs.insert(4, pl.BlockSpec((1, 2 * NV_HEADS, tm), lambda i, r: (i, 0, r)))
    return pl.pallas_call(
        kern, grid=(b, t // tm), in_specs=in_specs, out_specs=out_specs, out_shape=out_shape,
        compiler_params=_cparams(("parallel", "parallel"), 56),
        name="inconv_rows" if use_rows else "inconv_seq",
    )(*args)


def _gdn_kernel(qf, kf, vf, gf, rf, qb, kb, vb, gb, rb, s0_ref, of, ob, sfin_ref, s_ref, *, prec):
    i = pl.program_id(1)
    nc = pl.num_programs(1)

    @pl.when(i == 0)
    def _():
        s_ref[...] = s0_ref[0]

    c = qf.shape[1]
    per = NV_HEADS // NQK_HEADS
    row = lax.broadcasted_iota(I32, (c, c), 0)
    colj = lax.broadcasted_iota(I32, (c, c), 1)
    eye = jnp.where(row == colj, 1.0, 0.0)
    nt_dims = (((1,), (1,)), ((), ()))
    tn_dims = (((0,), (0,)), ((), ()))

    seqs = []
    for d, (q_r, k_r, v_r, g_r, r_r, o_r) in enumerate(((qf, kf, vf, gf, rf, of), (qb, kb, vb, gb, rb, ob))):
        rev = d == 1
        incl = (colj >= row) if rev else (colj <= row)
        strict = (colj > row) if rev else (colj < row)
        gates = g_r[0]
        tri_c = jnp.where(incl, 1.0, 0.0)
        gcm = _mm(tri_c, gates, HIGHEST)
        gcr = _mm(r_r[0], tri_c, HIGHEST, dims=nt_dims)
        for hq in range(NQK_HEADS):
            q = q_r[0, :, hq * HEAD_DIM:(hq + 1) * HEAD_DIM]
            k = k_r[0, :, hq * HEAD_DIM:(hq + 1) * HEAD_DIM]
            kq = lax.dot_general(jnp.concatenate([q, k], axis=0), k, nt_dims, preferred_element_type=F32)
            for j in range(per):
                h = hq * per + j
                idx = d * NV_HEADS + h
                gc_c = gcm[:, GATE_LANE0 + idx:GATE_LANE0 + idx + 1]
                seqs.append(dict(d=d, h=h, o_r=o_r, v_r=v_r, q=q, k=k, qk=kq[:c], kk=kq[c:], incl=incl, strict=strict,
                                 beta=gates[:, idx:idx + 1], gc_c=gc_c, gc_r=gcr[idx:idx + 1, :],
                                 ge=gc_c[0:1] if rev else gc_c[c - 1:c]))

    def same_block(m):
        sh = int(math.log2(m))
        return jnp.right_shift(row, sh) == jnp.right_shift(colj, sh)

    for s in seqs:
        s['decay'] = jnp.where(s['incl'], jnp.exp(jnp.where(s['incl'], s['gc_c'] - s['gc_r'], 0.0)), 0.0)
        s['a'] = jnp.where(s['strict'], s['beta'] * s['kk'] * s['decay'], 0.0)
        s['t'] = eye - jnp.where(same_block(2), s['a'], 0.0)
    m = 4
    while m <= c:
        between = jnp.logical_and(same_block(m), jnp.logical_not(same_block(m // 2)))
        for s in seqs:
            s['te'] = _mm(s['t'], jnp.where(between, s['a'], 0.0), prec)
        for s in seqs:
            s['t'] = s['t'] - _mm(s['te'], s['t'], prec)
        m *= 2
    for s in seqs:
        h = s['h']
        egc = jnp.exp(s['gc_c'])
        kf32 = s['k'].astype(F32)
        v = s['v_r'][0, :, h * HEAD_DIM:(h + 1) * HEAD_DIM].astype(F32)
        rhs = jnp.concatenate([s['beta'] * v, (s['beta'] * egc) * kf32], axis=1)
        s['sol'] = _mm(s['t'], rhs, prec)
        s['q_dec'] = s['q'].astype(F32) * egc
        s['k_dec'] = kf32 * jnp.exp(s['ge'] - s['gc_c'])
    for s in seqs:
        s['ws'] = _mm(jnp.concatenate([s['sol'][:, HEAD_DIM:], s['q_dec']], axis=0), s_ref[s['d'], s['h']])
    for s in seqs:
        s['u'] = s['sol'][:, :HEAD_DIM] - s['ws'][:c]
        s_ref[s['d'], s['h']] = (jnp.exp(s['ge']) * s_ref[s['d'], s['h']]
                                 + _mm(s['k_dec'], s['u'], dims=tn_dims))
    for s in seqs:
        h = s['h']
        o = s['ws'][c:] + _mm(s['qk'] * s['decay'], s['u'])
        s['o_r'][0, :, h * HEAD_DIM:(h + 1) * HEAD_DIM] = o.astype(s['o_r'].dtype)

    @pl.when(i == nc - 1)
    def _():
        sfin_ref[0] = s_ref[...]


def _gdn_call(q, k, v, go, rows, s0, prec):
    b, t, _ = q.shape
    c = GDN_CHUNK
    nc = t // c
    fwd = lambda i, n: (i, n, 0)
    bwd = lambda i, n: (i, nc - 1 - n, 0)
    rfwd = lambda i, n: (i, 0, n)
    rbwd = lambda i, n: (i, 0, nc - 1 - n)
    state_spec = pl.BlockSpec((1,) + s0.shape[1:], lambda i, n: (i, 0, 0, 0, 0))

    def specs(m3, mr):
        return [pl.BlockSpec((1, c, QK_DIM), m3), pl.BlockSpec((1, c, QK_DIM), m3),
                pl.BlockSpec((1, c, V_DIM), m3), pl.BlockSpec((1, c, LANES), m3),
                pl.BlockSpec((1, 2 * NV_HEADS, c), mr)]

    return pl.pallas_call(
        functools.partial(_gdn_kernel, prec=prec),
        grid=(b, nc),
        in_specs=specs(fwd, rfwd) + specs(bwd, rbwd) + [state_spec],
        out_specs=[pl.BlockSpec((1, c, V_DIM), fwd), pl.BlockSpec((1, c, V_DIM), bwd), state_spec],
        out_shape=[jax.ShapeDtypeStruct((b, t, V_DIM), BF16), jax.ShapeDtypeStruct((b, t, V_DIM), BF16),
                   jax.ShapeDtypeStruct(s0.shape, F32)],
        scratch_shapes=[pltpu.VMEM(s0.shape[1:], F32)],
        compiler_params=_cparams(("parallel", "arbitrary"), 48),
        name="gdn",
    )(q, k, v, go, rows, q, k, v, go, rows, s0)


def _fnet1_kernel(x_ref, f_ref, ar_ref, ai_ref):
    n = x_ref.shape[1]
    for j in range(x_ref.shape[2]):
        a = _mm(f_ref[...], x_ref[0, :, j, :])
        ar_ref[0, j] = a[:n]
        ai_ref[0, j] = a[n:]


def _fnet2_kernel(ar_ref, ai_ref, g_ref, wc_ref, o_ref, *, scale):
    n = ar_ref.shape[1]
    cols = ar_ref.shape[2]
    zs = []
    for m in range(cols):
        a2 = jnp.concatenate([ar_ref[0, :, m, :], ai_ref[0, :, m, :]], axis=0)
        zs.append(_mm(g_ref[m], a2))
    zr = jnp.concatenate([z[:n] for z in zs], axis=0)
    zi = jnp.concatenate([z[n:] for z in zs], axis=0)
    for g0 in range(0, zr.shape[1], HEAD_DIM):
        y = _mm(jnp.concatenate([zr[:, g0:g0 + HEAD_DIM], zi[:, g0:g0 + HEAD_DIM]], axis=1), wc_ref[...])
        for m in range(cols):
            o_ref[0, :, m, g0:g0 + HEAD_DIM] = y[m * n:(m + 1) * n] * scale


def _fnet_tables(n, groups):
    a = np.arange(n)
    ang1 = 2.0 * np.pi * np.outer(a, a) / n
    f1 = np.concatenate([np.cos(ang1), -np.sin(ang1)], axis=0)
    m = a[:, None] + n * a[None, :]
    ang2 = 2.0 * np.pi * ((m[:, :, None] * a[None, None, :]) % (n * n)) / (n * n)
    gc, gs = np.cos(ang2), np.sin(ang2)
    g2 = np.concatenate([np.concatenate([gc, gs], axis=2), np.concatenate([-gs, gc], axis=2)], axis=1)
    angc = 2.0 * np.pi * np.outer(np.arange(HEAD_DIM), np.arange(HEAD_DIM)) / HEAD_DIM
    wc = np.concatenate([np.cos(angc), np.sin(angc)], axis=0)
    f = lambda x: jnp.asarray(x, F32).astype(BF16)
    return f(f1), f(g2), f(wc)


def _fnet_call(f):
    b, l, c = f.shape
    n = GRID_W
    assert l == n * n
    f1, g2, wc = _fnet_tables(n, c // HEAD_DIM)
    cols = SUBLANES
    ar, ai = pl.pallas_call(
        _fnet1_kernel,
        grid=(b, n // cols),
        in_specs=[pl.BlockSpec((1, n, cols, c), lambda i, j: (i, 0, j, 0)),
                  pl.BlockSpec((2 * n, n), lambda i, j: (0, 0))],
        out_specs=[pl.BlockSpec((1, cols, n, c), lambda i, j: (i, j, 0, 0))] * 2,
        out_shape=[jax.ShapeDtypeStruct((b, n, n, c), F32)] * 2,
        compiler_params=_cparams(("parallel", "parallel"), 32),
        name="fnet1",
    )(f.reshape(b, n, n, c), f1)
    out = pl.pallas_call(
        functools.partial(_fnet2_kernel, scale=1.0 / math.sqrt(l * HEAD_DIM)),
        grid=(b, n // cols),
        in_specs=[pl.BlockSpec((1, n, cols, c), lambda i, j: (i, 0, j, 0)),
                  pl.BlockSpec((1, n, cols, c), lambda i, j: (i, 0, j, 0)),
                  pl.BlockSpec((cols, 2 * n, 2 * n), lambda i, j: (j, 0, 0)),
                  pl.BlockSpec((2 * HEAD_DIM, HEAD_DIM), lambda i, j: (0, 0))],
        out_specs=pl.BlockSpec((1, n, cols, c), lambda i, j: (i, 0, j, 0)),
        out_shape=jax.ShapeDtypeStruct((b, n, n, c), F32),
        compiler_params=_cparams(("parallel", "parallel"), 32),
        name="fnet2",
    )(ar, ai, g2, wc)
    return out.reshape(b, l, c)


def _merge_kernel(of_ref, ob_ref, z_ref, fm_ref, gab_ref, x_ref, g1_ref, sh2_ref, sc2_ref, gn_ref, n2_ref,
                  wg_ref, wf_ref, wm_ref, wr_ref, br_ref, x1_ref, h2_ref, lg_ref):
    d = x_ref.shape[1]
    tm = x_ref.shape[0]
    halves = [slice(0, tm // 2), slice(tm // 2, tm)]
    yb_in = []
    for r in halves:
        o = of_ref[r, :].astype(F32) + ob_ref[r, :].astype(F32)
        z = z_ref[r, :].astype(F32)
        parts = []
        for h0 in range(0, V_DIM, HEAD_DIM):
            oh = o[:, h0:h0 + HEAD_DIM]
            parts.append(oh * lax.rsqrt(jnp.mean(oh * oh, axis=-1, keepdims=True) + EPS) * gn_ref[...])
        yb_in.append(jnp.concatenate(parts, axis=1) * (z * _sigmoid(z)))
    yb = [_mm(v, wg_ref[...]) for v in yb_in]
    ya = [_mm(fm_ref[r, :], wf_ref[...]) for r in halves]
    mixed = [_sigmoid(gab_ref[r, :d].astype(F32)) * a + _sigmoid(gab_ref[r, d:].astype(F32)) * b_
             for r, a, b_ in zip(halves, ya, yb)]
    mm = [_mm(v, wm_ref[...]) for v in mixed]
    w = wr_ref[...]
    w_hi = w.astype(BF16)
    w2 = jnp.concatenate([w_hi, (w - w_hi.astype(F32)).astype(BF16)], axis=1)
    for r, v in zip(halves, mm):
        x1 = x_ref[r, :] + g1_ref[0] * v
        x1_ref[r, :] = x1
        h2 = _rmsnorm(x1, n2_ref[...]) * (1.0 + sc2_ref[0]) + sh2_ref[0]
        h2_ref[r, :] = _pack_bf16_pairs(h2)
        h_hi = h2.astype(BF16)
        h_lo = (h2 - h_hi.astype(F32)).astype(BF16)
        part = _mm(h_hi, w2)
        lg_ref[r, :] = (part[:, :LANES] + part[:, LANES:]) + _mm(h_lo, w_hi) + br_ref[...]


def _merge_call(of, ob, z, fm, gab, x2, g1, sh2, sc2, gn, n2, wg, wf, wm, wr, br, tokens_per_batch, tm):
    n, d = x2.shape
    per = tokens_per_batch // tm
    tok = lambda wd: pl.BlockSpec((tm, wd), lambda i: (i, 0))
    vec = pl.BlockSpec((1, 1, d), lambda i: (i // per, 0, 0))
    full = lambda a: pl.BlockSpec(a.shape, lambda i: (0,) * a.ndim)
    return pl.pallas_call(
        _merge_kernel,
        grid=(n // tm,),
        in_specs=[tok(V_DIM), tok(V_DIM), tok(V_DIM), tok(F_DIM), tok(2 * d), tok(d), vec, vec, vec,
                  full(gn), full(n2), full(wg), full(wf), full(wm), full(wr), full(br)],
        out_specs=[tok(d), tok(d // 2), tok(LANES)],
        out_shape=[jax.ShapeDtypeStruct((n, d), F32), jax.ShapeDtypeStruct((n, d // 2), I32),
                   jax.ShapeDtypeStruct((n, LANES), F32)],
        compiler_params=_cparams(("parallel",), 56),
        name="merge",
    )(of, ob, z, fm, gab, x2, g1, sh2, sc2, gn, n2, wg, wf, wm, wr, br)


def _route_kernel(lg_ref, idx_ref, w_ref, rank_ref, cnt_ref, run_ref):
    i = pl.program_id(0)

    @pl.when(i == 0)
    def _():
        run_ref[...] = jnp.zeros_like(run_ref)

    l = lg_ref[...]
    tm = l.shape[0]
    lane = lax.broadcasted_iota(I32, l.shape, 1)
    vals, idxs = [], []
    for _ in range(TOP_K):
        m = jnp.max(l, axis=-1, keepdims=True)
        idx = jnp.min(jnp.where(l == m, lane, LANES), axis=-1, keepdims=True)
        vals.append(m)
        idxs.append(idx)
        l = jnp.where(lane == idx, NEG_BIG * 2.0, l)
    es = [jnp.exp(v - vals[0]) for v in vals]
    inv = 1.0 / (es[0] + es[1] + es[2] + es[3])
    picked = jnp.zeros(l.shape, F32)
    for idx in idxs:
        picked = picked + (lane == idx).astype(F32)
    r = lax.broadcasted_iota(I32, (tm, tm), 0)
    cidx = lax.broadcasted_iota(I32, (tm, tm), 1)
    before = _mm(jnp.where(cidx < r, 1.0, 0.0), picked) + run_ref[...]
    idx_out = jnp.zeros(l.shape, I32)
    w_out = jnp.zeros(l.shape, F32)
    rank_out = jnp.zeros(l.shape, F32)
    for k in range(TOP_K):
        rk = jnp.sum(jnp.where(lane == idxs[k], before, 0.0), axis=-1, keepdims=True)
        idx_out = jnp.where(lane == k, idxs[k], idx_out)
        w_out = jnp.where(lane == k, es[k] * inv, w_out)
        rank_out = jnp.where(lane == k, rk, rank_out)
    idx_ref[...] = jnp.transpose(idx_out.astype(F32))[:SUBLANES].astype(I32)
    w_ref[...] = w_out
    rank_ref[...] = jnp.transpose(rank_out)[:SUBLANES].astype(I32)
    run_ref[...] = run_ref[...] + jnp.sum(picked, axis=0, keepdims=True)
    cnt_ref[...] = run_ref[...]


def _route_call(logits, tm, start, n):
    first = start // tm
    tok = pl.BlockSpec((tm, LANES), lambda i: (i, 0))
    tok_t = pl.BlockSpec((SUBLANES, tm), lambda i: (0, i))
    return pl.pallas_call(
        _route_kernel,
        grid=(n // tm,),
        in_specs=[pl.BlockSpec((tm, LANES), lambda i: (first + i, 0))],
        out_specs=[tok_t, tok, tok_t, pl.BlockSpec((1, LANES), lambda i: (0, 0))],
        out_shape=[jax.ShapeDtypeStruct((SUBLANES, n), I32), jax.ShapeDtypeStruct((n, LANES), F32),
                   jax.ShapeDtypeStruct((SUBLANES, n), I32), jax.ShapeDtypeStruct((1, LANES), F32)],
        scratch_shapes=[pltpu.VMEM((1, LANES), F32)],
        compiler_params=_cparams(("arbitrary",), 32),
        name="route",
    )(logits)


def _sc_mesh():
    return plsc.VectorSubcoreMesh(core_axis_name="c", subcore_axis_name="s",
                                  num_cores=SC_CORES, num_subcores=SC_SUBCORES)


def _sc_worker_id():
    return lax.axis_index("s") * SC_CORES + lax.axis_index("c")


def _sc_scatter_rows(x, idx, n_out, start, n):
    d = x.shape[1]
    per_w = idx.shape[0] // SC_WORKERS
    nwin = per_w // SC_WINDOW
    assert per_w * SC_WORKERS == idx.shape[0] and nwin * SC_WINDOW == per_w and nwin % 2 == 0 and n % per_w == 0

    def body(x_hbm, idx_hbm, out_hbm, idx_v, rows_v, sem_r, sem_w):
        wid = _sc_worker_id()
        row0 = start + lax.rem(wid * per_w, n)
        pltpu.sync_copy(idx_hbm.at[wid], idx_v)

        def read(j, b):
            return pltpu.make_async_copy(x_hbm.at[pl.ds(row0 + j * SC_WINDOW, SC_WINDOW)], rows_v.at[b], sem_r.at[b])

        def write(j, b):
            return pltpu.make_async_copy(rows_v.at[b], out_hbm.at[idx_v.at[j]], sem_w.at[b])

        @pl.loop(0, nwin, step=2)
        def _(j):
            read(j, 0).start()
            read(j + 1, 1).start()
            read(j, 0).wait()
            write(j, 0).start()
            read(j + 1, 1).wait()
            write(j + 1, 1).start()
            write(j, 0).wait()
            write(j + 1, 1).wait()

    return pl.kernel(
        body, out_type=jax.ShapeDtypeStruct((n_out, d), x.dtype), mesh=_sc_mesh(),
        scratch_types=[pltpu.VMEM((nwin, SC_WINDOW), I32), pltpu.VMEM((2, SC_WINDOW, d), x.dtype),
                       pltpu.SemaphoreType.DMA((2,)), pltpu.SemaphoreType.DMA((2,))],
        name="sc_scatter_rows",
    )(x, idx.reshape(SC_WORKERS, nwin, SC_WINDOW))


def _sc_gather_rows(y, idx):
    d = y.shape[1]
    total = idx.shape[0]
    per_w = total // SC_WORKERS
    nwin = per_w // SC_WINDOW
    assert per_w * SC_WORKERS == total and nwin * SC_WINDOW == per_w and nwin % 2 == 0

    def body(y_hbm, idx_hbm, out_hbm, idx_v, rows_v, sem_r, sem_w):
        wid = _sc_worker_id()
        row0 = wid * per_w
        pltpu.sync_copy(idx_hbm.at[wid], idx_v)

        def read(j, b):
            return pltpu.make_async_copy(y_hbm.at[idx_v.at[j]], rows_v.at[b], sem_r.at[b])

        def write(j, b):
            return pltpu.make_async_copy(rows_v.at[b], out_hbm.at[pl.ds(row0 + j * SC_WINDOW, SC_WINDOW)], sem_w.at[b])

        @pl.loop(0, nwin, step=2)
        def _(j):
            read(j, 0).start()
            read(j + 1, 1).start()
            read(j, 0).wait()
            write(j, 0).start()
            read(j + 1, 1).wait()
            write(j + 1, 1).start()
            write(j, 0).wait()
            write(j + 1, 1).wait()

    return pl.kernel(
        body, out_type=jax.ShapeDtypeStruct((total, d), y.dtype), mesh=_sc_mesh(),
        scratch_types=[pltpu.VMEM((nwin, SC_WINDOW), I32), pltpu.VMEM((2, SC_WINDOW, d), y.dtype),
                       pltpu.SemaphoreType.DMA((2,)), pltpu.SemaphoreType.DMA((2,))],
        name="sc_gather_rows",
    )(y, idx.reshape(SC_WORKERS, nwin, SC_WINDOW))


def _expert_kernel(be_ref, nv_ref, nxt_ref, par_ref, x_ref, wg_hbm, wu_hbm, wd_hbm, bg_ref, bu_ref, bd_ref, y_ref,
                   wgf, wuf, wdf, wgb, wub, wdb, sem):
    i = pl.program_id(0)

    def fetch(e, s):
        return [pltpu.make_async_copy(w_hbm.at[e], w_f.at[s], sem.at[s])
                for w_hbm, w_f in ((wg_hbm, wgf), (wu_hbm, wuf), (wd_hbm, wdf))]

    @pl.when(i < nv_ref[0])
    def _():
        e = be_ref[i]
        s = par_ref[e]

        @pl.when(i == 0)
        def _():
            for cp in fetch(e, s):
                cp.start()

        @pl.when(jnp.logical_or(i == 0, be_ref[jnp.maximum(i - 1, 0)] != e))
        def _():
            for cp in fetch(e, s):
                cp.wait()
            wgb[...] = wgf[s].astype(BF16)
            wub[...] = wuf[s].astype(BF16)
            wdb[...] = wdf[s].astype(BF16)

            @pl.when(nxt_ref[e] >= 0)
            def _():
                for cp in fetch(nxt_ref[e], 1 - s):
                    cp.start()

        x = _unpack_bf16_pairs(x_ref[...]).astype(BF16)
        gate = jnp.minimum(_mm(x, wgb[...]) + bg_ref[0], SWIGLU_LIMIT)
        up = jnp.clip(_mm(x, wub[...]) + bu_ref[0], -SWIGLU_LIMIT, SWIGLU_LIMIT)
        act = (up + 1.0) * gate * _sigmoid(SWIGLU_ALPHA * gate)
        y_ref[...] = _pack_bf16_pairs(_mm(act, wdb[...]) + bd_ref[0])


def _expert_call(block_e, n_valid, next_e, parity, xs, w_gate, w_up, w_down, b_gate, b_up, b_down):
    nb = block_e.shape[0]
    tmb = EXPERT_BLOCK
    ne, d, de = w_gate.shape
    bspec = lambda s: pl.BlockSpec((1,) + s, lambda i, be, nv, nx, pa: (be[i], 0, 0))
    anyspec = pl.BlockSpec(memory_space=pl.ANY)
    grid_spec = pltpu.PrefetchScalarGridSpec(
        num_scalar_prefetch=4,
        grid=(nb,),
        in_specs=[pl.BlockSpec((tmb, d // 2), lambda i, be, nv, nx, pa: (i, 0)),
                  anyspec, anyspec, anyspec, bspec((1, de)), bspec((1, de)), bspec((1, d))],
        out_specs=pl.BlockSpec((tmb, d // 2), lambda i, be, nv, nx, pa: (i, 0)),
        scratch_shapes=[pltpu.VMEM((2, d, de), F32), pltpu.VMEM((2, d, de), F32), pltpu.VMEM((2, de, d), F32),
                        pltpu.VMEM((d, de), BF16), pltpu.VMEM((d, de), BF16), pltpu.VMEM((de, d), BF16),
                        pltpu.SemaphoreType.DMA((2,))],
    )
    return pl.pallas_call(
        _expert_kernel,
        grid_spec=grid_spec,
        out_shape=jax.ShapeDtypeStruct(xs.shape, xs.dtype),
        compiler_params=_cparams(("arbitrary",), 56),
        name="expert",
    )(block_e, n_valid, next_e, parity, xs, w_gate, w_up, w_down,
      b_gate.reshape(ne, 1, de), b_up.reshape(ne, 1, de), b_down.reshape(ne, 1, d))


def _combine_kernel(y0, y1, y2, y3, w_ref, x1_ref, g2_ref, fg_ref, o_ref):
    w = w_ref[...]
    ys = [_unpack_bf16_pairs(y[...]) for y in (y0, y1, y2, y3)]
    moe = (w[:, 0:1] * ys[0] + w[:, 1:2] * ys[1]) + (w[:, 2:3] * ys[2] + w[:, 3:4] * ys[3])
    o_ref[...] = _rmsnorm(x1_ref[...] + g2_ref[0] * moe, fg_ref[...])


def _combine_call(prev, y4, top_w, x1, g2, fg, tokens_per_batch, tm, start):
    n, d = x1.shape
    per = tokens_per_batch // tm
    nt = top_w.shape[0] // tm
    first = start // tm
    yspec = lambda k: pl.BlockSpec((tm, d // 2), lambda i: (k * nt + i, 0))
    in_specs = [yspec(0), yspec(1), yspec(2), yspec(3),
                pl.BlockSpec((tm, LANES), lambda i: (i, 0)),
                pl.BlockSpec((tm, d), lambda i: (first + i, 0)),
                pl.BlockSpec((1, 1, d), lambda i: ((first + i) // per, 0, 0)),
                pl.BlockSpec((1, d), lambda i: (0, 0))]
    args = [y4, y4, y4, y4, top_w, x1, g2, fg]
    kern, aliases = _combine_kernel, {}
    if prev is not None:
        in_specs.append(pl.BlockSpec(memory_space=pl.ANY))
        args.append(prev)
        kern = lambda *refs: _combine_kernel(*refs[:8], refs[9])
        aliases = {8: 0}
    return pl.pallas_call(
        kern,
        grid=(nt,),
        in_specs=in_specs,
        out_specs=pl.BlockSpec((tm, d), lambda i: (first + i, 0)),
        out_shape=jax.ShapeDtypeStruct((n, d), F32),
        input_output_aliases=aliases,
        compiler_params=_cparams(("parallel",), 48),
        name="combine",
    )(*args)


def _slot_kernel(start_ref, idx_ref, rank_ref, o_ref):
    idx = idx_ref[...]
    acc = rank_ref[...]
    for e in range(N_EXPERTS):
        acc = acc + jnp.where(idx == e, start_ref[e], 0)
    o_ref[...] = acc


def _slot_call(pad_start, top_idx, rank):
    full = pl.BlockSpec(top_idx.shape, lambda i, s: (0, 0))
    return pl.pallas_call(
        _slot_kernel,
        grid_spec=pltpu.PrefetchScalarGridSpec(num_scalar_prefetch=1, grid=(1,), in_specs=[full, full],
                                               out_specs=full),
        out_shape=jax.ShapeDtypeStruct(top_idx.shape, I32),
        compiler_params=_cparams(("arbitrary",), 32),
        name="slots",
    )(pad_start, top_idx, rank)


def _routing_tables(top_idx, rank, counts, n):
    tmb = EXPERT_BLOCK
    counts = counts.astype(I32)
    padded = (counts + tmb - 1) // tmb * tmb
    pad_end = jnp.cumsum(padded)
    pad_start = pad_end - padded
    n_blocks = -(-(n * TOP_K + N_EXPERTS * (tmb - 1)) // tmb)
    n_slots = n_blocks * tmb
    dest_flat = _slot_call(pad_start, top_idx, rank)[:TOP_K].reshape(-1)
    block_start = jnp.arange(n_blocks, dtype=I32) * tmb
    block_e = jnp.minimum(jnp.sum((pad_end[None, :] <= block_start[:, None]).astype(I32), axis=1), N_EXPERTS - 1)
    n_valid = (pad_end[-1:] // tmb).astype(I32)
    experts = jnp.arange(N_EXPERTS, dtype=I32)
    used = counts > 0
    later = jnp.where(jnp.logical_and(used[None, :], experts[None, :] > experts[:, None]), experts[None, :], N_EXPERTS)
    next_e = jnp.min(later, axis=1)
    next_e = jnp.where(next_e == N_EXPERTS, -1, next_e).astype(I32)
    parity = ((jnp.cumsum(used.astype(I32)) - used.astype(I32)) % 2).astype(I32)
    return block_e, n_valid, next_e, parity, dest_flat, n_slots


def _gdn_branch(x3, sh, sc, norm_g, w_qkv, w_gates, w_extra, extra_dtypes, conv_w, par, grid_w, use_rows, tm,
                s0, prec):
    q, k, v, go, rows, *extra = _inconv_call(x3, sh, sc, norm_g, w_qkv, w_gates, w_extra, extra_dtypes, conv_w, par,
                                             grid_w, use_rows, tm)
    o_f, o_b, s_fin = _gdn_call(q, k, v, go, rows, s0, prec)
    return extra, o_f, o_b, s_fin


def kernel(x, c, ctx, c_ctx, w_mod, b_mod, norm1_g, norm2_g, w_in, conv_w, a_log, dt_bias, gdn_norm_g,
           w_fourier_out, w_gdn_out, w_merge_out, w_router, b_router, w_gate, b_gate, w_up, b_up,
           w_down, b_down, final_norm_g):
    b, l, d = x.shape
    n = b * l
    n_ctx = ctx.shape[1]
    assert w_mod.shape[0] == 1 and l == GRID_W * GRID_W and d == V_DIM
    prec = None

    c8 = jnp.concatenate([c, c_ctx[None, :], jnp.zeros((8 - b - 1, d), F32)], axis=0)
    mod = _mod_call(c8, w_mod[0], b_mod[0])
    sh1, sc1, g1, sh2, sc2, g2 = [mod[:b, j * d:(j + 1) * d].reshape(b, 1, d) for j in range(6)]
    csh1 = jnp.broadcast_to(mod[b:b + 1, 0:d].reshape(1, 1, d), (b, 1, d))
    csc1 = jnp.broadcast_to(mod[b:b + 1, d:2 * d].reshape(1, 1, d), (b, 1, d))

    wi = w_in[0]
    off_gate = QKV_DIM
    off_z = off_gate + 4 * NV_HEADS
    off_f = off_z + V_DIM
    off_ga = off_f + F_DIM
    w_qkv = wi[:, :QKV_DIM].astype(BF16)
    w_gates = jnp.pad(wi[:, off_gate:off_z], ((0, 0), (0, LANES - 4 * NV_HEADS))).astype(BF16)
    w_extra = [wi[:, off_z:off_f].astype(BF16), wi[:, off_f:off_ga].astype(BF16), wi[:, off_ga:].astype(BF16)]
    par = jnp.pad(jnp.stack([a_log[0].reshape(-1), dt_bias[0].reshape(-1)]),
                  ((0, 6), (GATE_LANE0, LANES - 2 * GATE_LANE0)))
    n1 = norm1_g[0].reshape(1, d)
    cw = conv_w[0].reshape(9, QKV_DIM)

    zero_state = jnp.zeros((b, 2, NV_HEADS, HEAD_DIM, HEAD_DIM), F32)
    _, _, _, s_ctx = _gdn_branch(ctx, csh1, csc1, n1, w_qkv, w_gates, [], [], cw, par, n_ctx, False, n_ctx,
                                 zero_state, prec)

    x2 = x.reshape(n, d)
    (z, f, gab), o_f, o_b, _ = _gdn_branch(x, sh1, sc1, n1, w_qkv, w_gates, w_extra, (BF16, F32, BF16),
                                           cw, par, GRID_W, True, 512, s_ctx, prec)
    z, gab = z.reshape(n, V_DIM), gab.reshape(n, 2 * d)
    fmix = _fnet_call(f).reshape(n, F_DIM)

    wr = jnp.pad(w_router[0], ((0, 0), (0, LANES - N_EXPERTS)))
    br = jnp.pad(b_router[0], (0, LANES - N_EXPERTS), constant_values=NEG_BIG).reshape(1, LANES)
    x1, h2, logits = _merge_call(
        o_f.reshape(n, V_DIM), o_b.reshape(n, V_DIM), z, fmix, gab, x2, g1, sh2, sc2,
        gdn_norm_g[0].reshape(1, HEAD_DIM), norm2_g[0].reshape(1, d),
        w_gdn_out[0].astype(BF16), w_fourier_out[0].astype(BF16), w_merge_out[0].astype(BF16), wr, br, l, 256)

    out = None
    part = n // MOE_PARTS
    for start in range(0, n, part):
        top_idx, top_w, rank, counts = _route_call(logits, 512, start, part)
        block_e, n_valid, next_e, parity, dest_flat, n_slots = _routing_tables(
            top_idx, rank, counts[0, :N_EXPERTS], part)
        xs = _sc_scatter_rows(h2, dest_flat, n_slots, start, part)
        ys = _expert_call(block_e, n_valid, next_e, parity, xs, w_gate[0], w_up[0], w_down[0],
                          b_gate[0], b_up[0], b_down[0])
        y4 = _sc_gather_rows(ys, dest_flat)
        out = _combine_call(out, y4, top_w, x1, g2, final_norm_g.reshape(1, d), l, 256, start)
    return out.reshape(b, l, d)
```

```python
import functools
import math

import jax
import jax.numpy as jnp
import numpy as np
from jax import lax
from jax.experimental import pallas as pl
from jax.experimental.pallas import tpu as pltpu
from jax.experimental.pallas import tpu_sc as plsc

F32 = jnp.float32
BF16 = jnp.bfloat16
I32 = jnp.int32
HIGHEST = lax.Precision.HIGHEST

GRID_W = 64
NQK_HEADS = 4
NV_HEADS = 8
HEAD_DIM = 128
QK_DIM = NQK_HEADS * HEAD_DIM
V_DIM = NV_HEADS * HEAD_DIM
QKV_DIM = 2 * QK_DIM + V_DIM
F_GROUPS = 4
F_DIM = F_GROUPS * HEAD_DIM
N_EXPERTS = 32
TOP_K = 4
SWIGLU_ALPHA = 1.702
SWIGLU_LIMIT = 7.0
EPS = 1e-6

LANES = 128
SUBLANES = 8
GATE_LANE0 = 16
GDN_CHUNK = 128
EXPERT_BLOCK = 512
MOE_PARTS = 1
SC_CORES = 2
SC_SUBCORES = 16
SC_WORKERS = SC_CORES * SC_SUBCORES
SC_WINDOW = 64
NEG_BIG = -1e30
MIB = 2 ** 20


def _cparams(sem, vmem_mib):
    return pltpu.CompilerParams(dimension_semantics=sem, vmem_limit_bytes=vmem_mib * MIB)


def _mm(a, b, prec=None, dims=(((1,), (0,)), ((), ()))):
    if prec is None:
        return lax.dot_general(a.astype(BF16), b.astype(BF16), dims, preferred_element_type=F32)
    return lax.dot_general(a.astype(F32), b.astype(F32), dims, precision=prec, preferred_element_type=F32)


def _sigmoid(x):
    return 1.0 / (1.0 + jnp.exp(-x))


def _rmsnorm(x, g):
    return x * lax.rsqrt(jnp.mean(x * x, axis=-1, keepdims=True) + EPS) * g


def _pack_bf16_pairs(x):
    half = x.shape[1] // 2
    bits = lax.bitcast_convert_type(x.astype(BF16).astype(F32), jnp.uint32)
    packed = jnp.bitwise_or(jnp.right_shift(bits[:, :half], jnp.uint32(16)),
                            jnp.bitwise_and(bits[:, half:], jnp.uint32(0xFFFF0000)))
    return lax.bitcast_convert_type(packed, I32)


def _unpack_bf16_pairs(p):
    bits = lax.bitcast_convert_type(p, jnp.uint32)
    lo = lax.bitcast_convert_type(jnp.left_shift(bits, jnp.uint32(16)), F32)
    hi = lax.bitcast_convert_type(jnp.bitwise_and(bits, jnp.uint32(0xFFFF0000)), F32)
    return jnp.concatenate([lo, hi], axis=1)


def _mod_kernel(c_ref, w_ref, b_ref, o_ref):
    c = c_ref[...]
    o_ref[...] = _mm(c * _sigmoid(c), w_ref[...], HIGHEST) + b_ref[...]


def _mod_call(c8, w_mod, b_mod):
    d, n = w_mod.shape
    tn = 1536
    return pl.pallas_call(
        _mod_kernel,
        grid=(n // tn,),
        in_specs=[pl.BlockSpec((8, d), lambda j: (0, 0)),
                  pl.BlockSpec((d, tn), lambda j: (0, j)),
                  pl.BlockSpec((1, tn), lambda j: (0, j))],
        out_specs=pl.BlockSpec((8, tn), lambda j: (0, j)),
        out_shape=jax.ShapeDtypeStruct((8, n), F32),
        compiler_params=_cparams(("parallel",), 32),
        name="mod",
    )(c8, w_mod, b_mod.reshape(1, n))


def _inconv_kernel(*refs, grid_w, use_rows, tm, cw, n_extra):
    refs = list(refs)
    prev_ref = refs.pop(0) if use_rows else None
    x_ref = refs.pop(0)
    next_ref = refs.pop(0) if use_rows else None
    sh_ref, sc_ref, g_ref, wq_ref, wgate_ref = refs[:5]
    wx_refs = refs[5:5 + n_extra]
    w_ref, par_ref, q_ref, k_ref, v_ref, go_ref, gr_ref = refs[5 + n_extra:12 + n_extra]
    ox_refs = refs[12 + n_extra:]

    def modulated(x):
        return (_rmsnorm(x, g_ref[...]) * (1.0 + sc_ref[0]) + sh_ref[0]).astype(BF16)

    r = pl.program_id(1)
    nr = pl.num_programs(1)
    u = modulated(x_ref[0])

    def project(wx_ref, o_ref, c0, step):
        o_ref[0, :, c0:c0 + step] = _mm(u, wx_ref[:, c0:c0 + step]).astype(o_ref.dtype)

    plain = [functools.partial(project, wx_ref, o_ref, c0, min(wx_ref.shape[1], 512))
             for wx_ref, o_ref in zip(wx_refs, ox_refs) for c0 in range(0, wx_ref.shape[1], min(wx_ref.shape[1], 512))]
    n_conv_chunks = QKV_DIM // cw

    t = lax.broadcasted_iota(I32, (tm, 1), 0)
    col = jnp.bitwise_and(t, grid_w - 1)
    m_left = (col != 0).astype(F32)
    m_right = (col != grid_w - 1).astype(F32)
    if use_rows:
        u_prev = modulated(prev_ref[0])
        u_next = modulated(next_ref[0])
        has_prev = (r > 0).astype(F32)
        has_next = (r < nr - 1).astype(F32)
    for c0 in range(0, QKV_DIM, cw):
        wq = wq_ref[:, c0:c0 + cw]
        xm = _mm(u, wq)
        if use_rows:
            up = jnp.concatenate([_mm(u_prev, wq) * has_prev, xm[:tm - grid_w]], axis=0)
            dn = jnp.concatenate([xm[grid_w:], _mm(u_next, wq) * has_next], axis=0)

        def colsum(kc):
            y = xm * w_ref[3 + kc:4 + kc, c0:c0 + cw]
            if use_rows:
                y = y + up * w_ref[kc:kc + 1, c0:c0 + cw] + dn * w_ref[6 + kc:7 + kc, c0:c0 + cw]
            return y

        acc = (colsum(1) + pltpu.roll(colsum(0), 1, axis=0) * m_left
               + pltpu.roll(colsum(2), tm - 1, axis=0) * m_right)
        s = acc * _sigmoid(acc)
        for h0 in range(0, cw, HEAD_DIM):
            c = c0 + h0
            seg = s[:, h0:h0 + HEAD_DIM]
            if c < 2 * QK_DIM:
                seg = seg * lax.rsqrt(jnp.sum(seg * seg, axis=-1, keepdims=True) + EPS)
            if c < QK_DIM:
                q_ref[0, :, c:c + HEAD_DIM] = (seg * HEAD_DIM ** -0.5).astype(q_ref.dtype)
            elif c < 2 * QK_DIM:
                k_ref[0, :, c - QK_DIM:c - QK_DIM + HEAD_DIM] = seg.astype(k_ref.dtype)
            else:
                v_ref[0, :, c - 2 * QK_DIM:c - 2 * QK_DIM + HEAD_DIM] = seg.astype(v_ref.dtype)
        ci = c0 // cw
        for job in plain[ci * len(plain) // n_conv_chunks:(ci + 1) * len(plain) // n_conv_chunks]:
            job()
    g = _mm(u, wgate_ref[...])
    a = g + par_ref[1:2, :]
    softplus = jnp.maximum(a, 0.0) + jnp.log1p(jnp.exp(-jnp.abs(a)))
    log_g = -jnp.exp(par_ref[0:1, :]) * softplus
    lane = lax.broadcasted_iota(I32, g.shape, 1)
    go_ref[0] = jnp.where(lane < GATE_LANE0, _sigmoid(g), log_g)
    gr_ref[0] = jnp.transpose(log_g)[GATE_LANE0:GATE_LANE0 + 2 * NV_HEADS]


def _inconv_call(x3, sh, sc, g, w_qkv, w_gates, w_extra, extra_dtypes, conv_w, par, grid_w, use_rows, tm):
    b, t, d = x3.shape
    kern = functools.partial(_inconv_kernel, grid_w=grid_w, use_rows=use_rows, tm=tm, cw=512, n_extra=len(w_extra))
    per = tm // grid_w
    nrow = t // grid_w
    tile = lambda wd: pl.BlockSpec((1, tm, wd), lambda i, r: (i, r, 0))
    vec = pl.BlockSpec((1, 1, d), lambda i, r: (i, 0, 0))
    const = lambda a: pl.BlockSpec(a.shape, lambda i, r: (0,) * a.ndim, pipeline_mode=pl.Buffered(1))
    in_specs, args = [], []
    if use_rows:
        in_specs.append(pl.BlockSpec((1, grid_w, d), lambda i, r: (i, jnp.maximum(r * per - 1, 0), 0)))
        args.append(x3)
    in_specs.append(tile(d))
    args.append(x3)
    if use_rows:
        in_specs.append(pl.BlockSpec((1, grid_w, d), lambda i, r: (i, jnp.minimum((r + 1) * per, nrow - 1), 0)))
        args.append(x3)
    consts = [g, w_qkv, w_gates, *w_extra, conv_w, par]
    in_specs += [vec, vec] + [const(a) for a in consts]
    args += [sh, sc] + consts
    widths = [QK_DIM, QK_DIM, V_DIM, LANES] + [w.shape[1] for w in w_extra]
    dtypes = [BF16, BF16, BF16, F32] + list(extra_dtypes)
    out_shape = [jax.ShapeDtypeStruct((b, t, wd), dt) for wd, dt in zip(widths, dtypes)]
    out_specs = [tile(wd) for wd in widths]
    out_shape.insert(4, jax.ShapeDtypeStruct((b, 2 * NV_HEADS, t), F32))
    out_specs.insert(4, pl.BlockSpec((1, 2 * NV_HEADS, tm), lambda i, r: (i, 0, r)))
    return pl.pallas_call(
        kern, grid=(b, t // tm), in_specs=in_specs, out_specs=out_specs, out_shape=out_shape,
        compiler_params=_cparams(("parallel", "parallel"), 56),
        name="inconv_rows" if use_rows else "inconv_seq",
    )(*args)


def _gdn_kernel(qf, kf, vf, gf, rf, qb, kb, vb, gb, rb, s0_ref, of, ob, sfin_ref, s_ref, *, prec):
    i = pl.program_id(1)
    nc = pl.num_programs(1)

    @pl.when(i == 0)
    def _():
        s_ref[...] = s0_ref[0]

    c = qf.shape[1]
    per = NV_HEADS // NQK_HEADS
    row = lax.broadcasted_iota(I32, (c, c), 0)
    colj = lax.broadcasted_iota(I32, (c, c), 1)
    eye = jnp.where(row == colj, 1.0, 0.0)
    nt_dims = (((1,), (1,)), ((), ()))
    tn_dims = (((0,), (0,)), ((), ()))

    seqs = []
    for d, (q_r, k_r, v_r, g_r, r_r, o_r) in enumerate(((qf, kf, vf, gf, rf, of), (qb, kb, vb, gb, rb, ob))):
        rev = d == 1
        incl = (colj >= row) if rev else (colj <= row)
        strict = (colj > row) if rev else (colj < row)
        gates = g_r[0]
        tri_c = jnp.where(incl, 1.0, 0.0)
        gcm = _mm(tri_c, gates, HIGHEST)
        gcr = _mm(r_r[0], tri_c, HIGHEST, dims=nt_dims)
        for hq in range(NQK_HEADS):
            q = q_r[0, :, hq * HEAD_DIM:(hq + 1) * HEAD_DIM]
            k = k_r[0, :, hq * HEAD_DIM:(hq + 1) * HEAD_DIM]
            kq = lax.dot_general(jnp.concatenate([q, k], axis=0), k, nt_dims, preferred_element_type=F32)
            for j in range(per):
                h = hq * per + j
                idx = d * NV_HEADS + h
                gc_c = gcm[:, GATE_LANE0 + idx:GATE_LANE0 + idx + 1]
                seqs.append(dict(d=d, h=h, o_r=o_r, v_r=v_r, q=q, k=k, qk=kq[:c], kk=kq[c:], incl=incl, strict=strict,
                                 beta=gates[:, idx:idx + 1], gc_c=gc_c, gc_r=gcr[idx:idx + 1, :],
                                 ge=gc_c[0:1] if rev else gc_c[c - 1:c]))

    def same_block(m):
        sh = int(math.log2(m))
        return jnp.right_shift(row, sh) == jnp.right_shift(colj, sh)

    for s in seqs:
        s['decay'] = jnp.where(s['incl'], jnp.exp(jnp.where(s['incl'], s['gc_c'] - s['gc_r'], 0.0)), 0.0)
        s['a'] = jnp.where(s['strict'], s['beta'] * s['kk'] * s['decay'], 0.0)
        s['t'] = eye - jnp.where(same_block(2), s['a'], 0.0)
    m = 4
    while m <= c:
        between = jnp.logical_and(same_block(m), jnp.logical_not(same_block(m // 2)))
        for s in seqs:
            s['te'] = _mm(s['t'], jnp.where(between, s['a'], 0.0), prec)
        for s in seqs:
            s['t'] = s['t'] - _mm(s['te'], s['t'], prec)
        m *= 2
    for s in seqs:
        h = s['h']
        egc = jnp.exp(s['gc_c'])
        kf32 = s['k'].astype(F32)
        v = s['v_r'][0, :, h * HEAD_DIM:(h + 1) * HEAD_DIM].astype(F32)
        rhs = jnp.concatenate([s['beta'] * v, (s['beta'] * egc) * kf32], axis=1)
        s['sol'] = _mm(s['t'], rhs, prec)
        s['q_dec'] = s['q'].astype(F32) * egc
        s['k_dec'] = kf32 * jnp.exp(s['ge'] - s['gc_c'])
    for s in seqs:
        s['ws'] = _mm(jnp.concatenate([s['sol'][:, HEAD_DIM:], s['q_dec']], axis=0), s_ref[s['d'], s['h']])
    for s in seqs:
        s['u'] = s['sol'][:, :HEAD_DIM] - s['ws'][:c]
        s_ref[s['d'], s['h']] = (jnp.exp(s['ge']) * s_ref[s['d'], s['h']]
                                 + _mm(s['k_dec'], s['u'], dims=tn_dims))
    for s in seqs:
        h = s['h']
        o = s['ws'][c:] + _mm(s['qk'] * s['decay'], s['u'])
        s['o_r'][0, :, h * HEAD_DIM:(h + 1) * HEAD_DIM] = o.astype(s['o_r'].dtype)

    @pl.when(i == nc - 1)
    def _():
        sfin_ref[0] = s_ref[...]


def _gdn_call(q, k, v, go, rows, s0, prec):
    b, t, _ = q.shape
    c = GDN_CHUNK
    nc = t // c
    fwd = lambda i, n: (i, n, 0)
    bwd = lambda i, n: (i, nc - 1 - n, 0)
    rfwd = lambda i, n: (i, 0, n)
    rbwd = lambda i, n: (i, 0, nc - 1 - n)
    state_spec = pl.BlockSpec((1,) + s0.shape[1:], lambda i, n: (i, 0, 0, 0, 0))

    def specs(m3, mr):
        return [pl.BlockSpec((1, c, QK_DIM), m3), pl.BlockSpec((1, c, QK_DIM), m3),
                pl.BlockSpec((1, c, V_DIM), m3), pl.BlockSpec((1, c, LANES), m3),
                pl.BlockSpec((1, 2 * NV_HEADS, c), mr)]

    return pl.pallas_call(
        functools.partial(_gdn_kernel, prec=prec),
        grid=(b, nc),
        in_specs=specs(fwd, rfwd) + specs(bwd, rbwd) + [state_spec],
        out_specs=[pl.BlockSpec((1, c, V_DIM), fwd), pl.BlockSpec((1, c, V_DIM), bwd), state_spec],
        out_shape=[jax.ShapeDtypeStruct((b, t, V_DIM), BF16), jax.ShapeDtypeStruct((b, t, V_DIM), BF16),
                   jax.ShapeDtypeStruct(s0.shape, F32)],
        scratch_shapes=[pltpu.VMEM(s0.shape[1:], F32)],
        compiler_params=_cparams(("parallel", "arbitrary"), 48),
        name="gdn",
    )(q, k, v, go, rows, q, k, v, go, rows, s0)


def _fnet1_kernel(x_ref, f_ref, ar_ref, ai_ref):
    n = x_ref.shape[1]
    for j in range(x_ref.shape[2]):
        a = _mm(f_ref[...], x_ref[0, :, j, :])
        ar_ref[0, j] = a[:n]
        ai_ref[0, j] = a[n:]


def _fnet2_kernel(ar_ref, ai_ref, g_ref, wc_ref, o_ref, *, scale):
    n = ar_ref.shape[1]
    cols = ar_ref.shape[2]
    zs = []
    for m in range(cols):
        a2 = jnp.concatenate([ar_ref[0, :, m, :], ai_ref[0, :, m, :]], axis=0)
        zs.append(_mm(g_ref[m], a2))
    zr = jnp.concatenate([z[:n] for z in zs], axis=0)
    zi = jnp.concatenate([z[n:] for z in zs], axis=0)
    for g0 in range(0, zr.shape[1], HEAD_DIM):
        y = _mm(jnp.concatenate([zr[:, g0:g0 + HEAD_DIM], zi[:, g0:g0 + HEAD_DIM]], axis=1), wc_ref[...])
        for m in range(cols):
            o_ref[0, :, m, g0:g0 + HEAD_DIM] = y[m * n:(m + 1) * n] * scale


def _fnet_tables(n, groups):
    a = np.arange(n)
    ang1 = 2.0 * np.pi * np.outer(a, a) / n
    f1 = np.concatenate([np.cos(ang1), -np.sin(ang1)], axis=0)
    m = a[:, None] + n * a[None, :]
    ang2 = 2.0 * np.pi * ((m[:, :, None] * a[None, None, :]) % (n * n)) / (n * n)
    gc, gs = np.cos(ang2), np.sin(ang2)
    g2 = np.concatenate([np.concatenate([gc, gs], axis=2), np.concatenate([-gs, gc], axis=2)], axis=1)
    angc = 2.0 * np.pi * np.outer(np.arange(HEAD_DIM), np.arange(HEAD_DIM)) / HEAD_DIM
    wc = np.concatenate([np.cos(angc), np.sin(angc)], axis=0)
    f = lambda x: jnp.asarray(x, F32).astype(BF16)
    return f(f1), f(g2), f(wc)


def _fnet_call(f):
    b, l, c = f.shape
    n = GRID_W
    assert l == n * n
    f1, g2, wc = _fnet_tables(n, c // HEAD_DIM)
    cols = 2 * SUBLANES
    ar, ai = pl.pallas_call(
        _fnet1_kernel,
        grid=(b, n // cols),
        in_specs=[pl.BlockSpec((1, n, cols, c), lambda i, j: (i, 0, j, 0)),
                  pl.BlockSpec((2 * n, n), lambda i, j: (0, 0))],
        out_specs=[pl.BlockSpec((1, cols, n, c), lambda i, j: (i, j, 0, 0))] * 2,
        out_shape=[jax.ShapeDtypeStruct((b, n, n, c), F32)] * 2,
        compiler_params=_cparams(("parallel", "parallel"), 32),
        name="fnet1",
    )(f.reshape(b, n, n, c), f1)
    out = pl.pallas_call(
        functools.partial(_fnet2_kernel, scale=1.0 / math.sqrt(l * HEAD_DIM)),
        grid=(b, n // cols),
        in_specs=[pl.BlockSpec((1, n, cols, c), lambda i, j: (i, 0, j, 0)),
                  pl.BlockSpec((1, n, cols, c), lambda i, j: (i, 0, j, 0)),
                  pl.BlockSpec((cols, 2 * n, 2 * n), lambda i, j: (j, 0, 0)),
                  pl.BlockSpec((2 * HEAD_DIM, HEAD_DIM), lambda i, j: (0, 0))],
        out_specs=pl.BlockSpec((1, n, cols, c), lambda i, j: (i, 0, j, 0)),
        out_shape=jax.ShapeDtypeStruct((b, n, n, c), F32),
        compiler_params=_cparams(("parallel", "parallel"), 32),
        name="fnet2",
    )(ar, ai, g2, wc)
    return out.reshape(b, l, c)


def _merge_kernel(of_ref, ob_ref, z_ref, fm_ref, gab_ref, x_ref, g1_ref, sh2_ref, sc2_ref, gn_ref, n2_ref,
                  wg_ref, wf_ref, wm_ref, wr_ref, br_ref, x1_ref, h2_ref, lg_ref):
    d = x_ref.shape[1]
    tm = x_ref.shape[0]
    halves = [slice(0, tm // 2), slice(tm // 2, tm)]
    yb_in = []
    for r in halves:
        o = of_ref[r, :].astype(F32) + ob_ref[r, :].astype(F32)
        z = z_ref[r, :].astype(F32)
        parts = []
        for h0 in range(0, V_DIM, HEAD_DIM):
            oh = o[:, h0:h0 + HEAD_DIM]
            parts.append(oh * lax.rsqrt(jnp.mean(oh * oh, axis=-1, keepdims=True) + EPS) * gn_ref[...])
        yb_in.append(jnp.concatenate(parts, axis=1) * (z * _sigmoid(z)))
    yb = [_mm(v, wg_ref[...]) for v in yb_in]
    ya = [_mm(fm_ref[r, :], wf_ref[...]) for r in halves]
    mixed = [_sigmoid(gab_ref[r, :d].astype(F32)) * a + _sigmoid(gab_ref[r, d:].astype(F32)) * b_
             for r, a, b_ in zip(halves, ya, yb)]
    mm = [_mm(v, wm_ref[...]) for v in mixed]
    w = wr_ref[...]
    w_hi = w.astype(BF16)
    w2 = jnp.concatenate([w_hi, (w - w_hi.astype(F32)).astype(BF16)], axis=1)
    for r, v in zip(halves, mm):
        x1 = x_ref[r, :] + g1_ref[0] * v
        x1_ref[r, :] = x1
        h2 = _rmsnorm(x1, n2_ref[...]) * (1.0 + sc2_ref[0]) + sh2_ref[0]
        h2_ref[r, :] = _pack_bf16_pairs(h2)
        h_hi = h2.astype(BF16)
        h_lo = (h2 - h_hi.astype(F32)).astype(BF16)
        part = _mm(h_hi, w2)
        lg_ref[r, :] = (part[:, :LANES] + part[:, LANES:]) + _mm(h_lo, w_hi) + br_ref[...]


def _merge_call(of, ob, z, fm, gab, x2, g1, sh2, sc2, gn, n2, wg, wf, wm, wr, br, tokens_per_batch, tm):
    n, d = x2.shape
    per = tokens_per_batch // tm
    tok = lambda wd: pl.BlockSpec((tm, wd), lambda i: (i, 0))
    vec = pl.BlockSpec((1, 1, d), lambda i: (i // per, 0, 0))
    full = lambda a: pl.BlockSpec(a.shape, lambda i: (0,) * a.ndim)
    return pl.pallas_call(
        _merge_kernel,
        grid=(n // tm,),
        in_specs=[tok(V_DIM), tok(V_DIM), tok(V_DIM), tok(F_DIM), tok(2 * d), tok(d), vec, vec, vec,
                  full(gn), full(n2), full(wg), full(wf), full(wm), full(wr), full(br)],
        out_specs=[tok(d), tok(d // 2), tok(LANES)],
        out_shape=[jax.ShapeDtypeStruct((n, d), F32), jax.ShapeDtypeStruct((n, d // 2), I32),
                   jax.ShapeDtypeStruct((n, LANES), F32)],
        compiler_params=_cparams(("parallel",), 56),
        name="merge",
    )(of, ob, z, fm, gab, x2, g1, sh2, sc2, gn, n2, wg, wf, wm, wr, br)


def _route_kernel(lg_ref, idx_ref, w_ref, rank_ref, cnt_ref, run_ref):
    i = pl.program_id(0)

    @pl.when(i == 0)
    def _():
        run_ref[...] = jnp.zeros_like(run_ref)

    l = lg_ref[...]
    tm = l.shape[0]
    lane = lax.broadcasted_iota(I32, l.shape, 1)
    vals, idxs = [], []
    for _ in range(TOP_K):
        m = jnp.max(l, axis=-1, keepdims=True)
        idx = jnp.min(jnp.where(l == m, lane, LANES), axis=-1, keepdims=True)
        vals.append(m)
        idxs.append(idx)
        l = jnp.where(lane == idx, NEG_BIG * 2.0, l)
    es = [jnp.exp(v - vals[0]) for v in vals]
    inv = 1.0 / (es[0] + es[1] + es[2] + es[3])
    picked = jnp.zeros(l.shape, F32)
    for idx in idxs:
        picked = picked + (lane == idx).astype(F32)
    r = lax.broadcasted_iota(I32, (tm, tm), 0)
    cidx = lax.broadcasted_iota(I32, (tm, tm), 1)
    before = _mm(jnp.where(cidx < r, 1.0, 0.0), picked) + run_ref[...]
    idx_out = jnp.zeros(l.shape, I32)
    w_out = jnp.zeros(l.shape, F32)
    rank_out = jnp.zeros(l.shape, F32)
    for k in range(TOP_K):
        rk = jnp.sum(jnp.where(lane == idxs[k], before, 0.0), axis=-1, keepdims=True)
        idx_out = jnp.where(lane == k, idxs[k], idx_out)
        w_out = jnp.where(lane == k, es[k] * inv, w_out)
        rank_out = jnp.where(lane == k, rk, rank_out)
    idx_ref[...] = jnp.transpose(idx_out.astype(F32))[:SUBLANES].astype(I32)
    w_ref[...] = w_out
    rank_ref[...] = jnp.transpose(rank_out)[:SUBLANES].astype(I32)
    run_ref[...] = run_ref[...] + jnp.sum(picked, axis=0, keepdims=True)
    cnt_ref[...] = run_ref[...]


def _route_call(logits, tm, start, n):
    first = start // tm
    tok = pl.BlockSpec((tm, LANES), lambda i: (i, 0))
    tok_t = pl.BlockSpec((SUBLANES, tm), lambda i: (0, i))
    return pl.pallas_call(
        _route_kernel,
        grid=(n // tm,),
        in_specs=[pl.BlockSpec((tm, LANES), lambda i: (first + i, 0))],
        out_specs=[tok_t, tok, tok_t, pl.BlockSpec((1, LANES), lambda i: (0, 0))],
        out_shape=[jax.ShapeDtypeStruct((SUBLANES, n), I32), jax.ShapeDtypeStruct((n, LANES), F32),
                   jax.ShapeDtypeStruct((SUBLANES, n), I32), jax.ShapeDtypeStruct((1, LANES), F32)],
        scratch_shapes=[pltpu.VMEM((1, LANES), F32)],
        compiler_params=_cparams(("arbitrary",), 32),
        name="route",
    )(logits)


def _sc_mesh():
    return plsc.VectorSubcoreMesh(core_axis_name="c", subcore_axis_name="s",
                                  num_cores=SC_CORES, num_subcores=SC_SUBCORES)


def _sc_worker_id():
    return lax.axis_index("s") * SC_CORES + lax.axis_index("c")


def _sc_scatter_rows(x, idx, n_out, start, n):
    d = x.shape[1]
    per_w = idx.shape[0] // SC_WORKERS
    nwin = per_w // SC_WINDOW
    assert per_w * SC_WORKERS == idx.shape[0] and nwin * SC_WINDOW == per_w and nwin % 2 == 0 and n % per_w == 0

    def body(x_hbm, idx_hbm, out_hbm, idx_v, rows_v, sem_r, sem_w):
        wid = _sc_worker_id()
        row0 = start + lax.rem(wid * per_w, n)
        pltpu.sync_copy(idx_hbm.at[wid], idx_v)

        def read(j, b):
            return pltpu.make_async_copy(x_hbm.at[pl.ds(row0 + j * SC_WINDOW, SC_WINDOW)], rows_v.at[b], sem_r.at[b])

        def write(j, b):
            return pltpu.make_async_copy(rows_v.at[b], out_hbm.at[idx_v.at[j]], sem_w.at[b])

        @pl.loop(0, nwin, step=2)
        def _(j):
            read(j, 0).start()
            read(j + 1, 1).start()
            read(j, 0).wait()
            write(j, 0).start()
            read(j + 1, 1).wait()
            write(j + 1, 1).start()
            write(j, 0).wait()
            write(j + 1, 1).wait()

    return pl.kernel(
        body, out_type=jax.ShapeDtypeStruct((n_out, d), x.dtype), mesh=_sc_mesh(),
        scratch_types=[pltpu.VMEM((nwin, SC_WINDOW), I32), pltpu.VMEM((2, SC_WINDOW, d), x.dtype),
                       pltpu.SemaphoreType.DMA((2,)), pltpu.SemaphoreType.DMA((2,))],
        name="sc_scatter_rows",
    )(x, idx.reshape(SC_WORKERS, nwin, SC_WINDOW))


def _sc_gather_rows(y, idx):
    d = y.shape[1]
    total = idx.shape[0]
    per_w = total // SC_WORKERS
    nwin = per_w // SC_WINDOW
    assert per_w * SC_WORKERS == total and nwin * SC_WINDOW == per_w and nwin % 2 == 0

    def body(y_hbm, idx_hbm, out_hbm, idx_v, rows_v, sem_r, sem_w):
        wid = _sc_worker_id()
        row0 = wid * per_w
        pltpu.sync_copy(idx_hbm.at[wid], idx_v)

        def read(j, b):
            return pltpu.make_async_copy(y_hbm.at[idx_v.at[j]], rows_v.at[b], sem_r.at[b])

        def write(j, b):
            return pltpu.make_async_copy(rows_v.at[b], out_hbm.at[pl.ds(row0 + j * SC_WINDOW, SC_WINDOW)], sem_w.at[b])

        @pl.loop(0, nwin, step=2)
        def _(j):
            read(j, 0).start()
            read(j + 1, 1).start()
            read(j, 0).wait()
            write(j, 0).start()
            read(j + 1, 1).wait()
            write(j + 1, 1).start()
            write(j, 0).wait()
            write(j + 1, 1).wait()

    return pl.kernel(
        body, out_type=jax.ShapeDtypeStruct((total, d), y.dtype), mesh=_sc_mesh(),
        scratch_types=[pltpu.VMEM((nwin, SC_WINDOW), I32), pltpu.VMEM((2, SC_WINDOW, d), y.dtype),
                       pltpu.SemaphoreType.DMA((2,)), pltpu.SemaphoreType.DMA((2,))],
        name="sc_gather_rows",
    )(y, idx.reshape(SC_WORKERS, nwin, SC_WINDOW))


def _expert_kernel(be_ref, nv_ref, nxt_ref, par_ref, used_ref, x_ref, wg_hbm, wu_hbm, wd_hbm, bg_ref, bu_ref, bd_ref,
                   y_ref, wgf, wuf, wdf, wgb, wub, wdb, sem):
    i = pl.program_id(0)

    def fetch(e, s):
        return [pltpu.make_async_copy(w_hbm.at[e], w_f.at[s], sem.at[s])
                for w_hbm, w_f in ((wg_hbm, wgf), (wu_hbm, wuf), (wd_hbm, wdf))]

    @pl.when(i < nv_ref[0])
    def _():
        e = be_ref[i]
        s = par_ref[e]

        @pl.when(i == 0)
        def _():
            for cp in fetch(e, s):
                cp.start()

        @pl.when(jnp.logical_or(i == 0, be_ref[jnp.maximum(i - 1, 0)] != e))
        def _():
            for cp in fetch(e, s):
                cp.wait()
            wgb[...] = wgf[s].astype(BF16)
            wub[...] = wuf[s].astype(BF16)
            wdb[...] = wdf[s].astype(BF16)

            @pl.when(nxt_ref[e] >= 0)
            def _():
                for cp in fetch(nxt_ref[e], 1 - s):
                    cp.start()

        def ffn(nrows):
            x = _unpack_bf16_pairs(x_ref[:nrows, :]).astype(BF16)
            gate = jnp.minimum(_mm(x, wgb[...]) + bg_ref[0], SWIGLU_LIMIT)
            up = jnp.clip(_mm(x, wub[...]) + bu_ref[0], -SWIGLU_LIMIT, SWIGLU_LIMIT)
            act = (up + 1.0) * gate * _sigmoid(SWIGLU_ALPHA * gate)
            y_ref[:nrows, :] = _pack_bf16_pairs(_mm(act, wdb[...]) + bd_ref[0])

        half = x_ref.shape[0] // 2

        @pl.when(used_ref[i] > half)
        def _():
            ffn(x_ref.shape[0])

        @pl.when(used_ref[i] <= half)
        def _():
            ffn(half)


def _expert_call(block_e, n_valid, next_e, parity, used, xs, w_gate, w_up, w_down, b_gate, b_up, b_down):
    nb = block_e.shape[0]
    tmb = EXPERT_BLOCK
    ne, d, de = w_gate.shape
    bspec = lambda s: pl.BlockSpec((1,) + s, lambda i, be, nv, nx, pa, us: (be[i], 0, 0))
    anyspec = pl.BlockSpec(memory_space=pl.ANY)
    grid_spec = pltpu.PrefetchScalarGridSpec(
        num_scalar_prefetch=5,
        grid=(nb,),
        in_specs=[pl.BlockSpec((tmb, d // 2), lambda i, be, nv, nx, pa, us: (i, 0)),
                  anyspec, anyspec, anyspec, bspec((1, de)), bspec((1, de)), bspec((1, d))],
        out_specs=pl.BlockSpec((tmb, d // 2), lambda i, be, nv, nx, pa, us: (i, 0)),
        scratch_shapes=[pltpu.VMEM((2, d, de), F32), pltpu.VMEM((2, d, de), F32), pltpu.VMEM((2, de, d), F32),
                        pltpu.VMEM((d, de), BF16), pltpu.VMEM((d, de), BF16), pltpu.VMEM((de, d), BF16),
                        pltpu.SemaphoreType.DMA((2,))],
    )
    return pl.pallas_call(
        _expert_kernel,
        grid_spec=grid_spec,
        out_shape=jax.ShapeDtypeStruct(xs.shape, xs.dtype),
        compiler_params=_cparams(("arbitrary",), 56),
        name="expert",
    )(block_e, n_valid, next_e, parity, used, xs, w_gate, w_up, w_down,
      b_gate.reshape(ne, 1, de), b_up.reshape(ne, 1, de), b_down.reshape(ne, 1, d))


def _combine_kernel(y0, y1, y2, y3, w_ref, x1_ref, g2_ref, fg_ref, o_ref):
    w = w_ref[...]
    ys = [_unpack_bf16_pairs(y[...]) for y in (y0, y1, y2, y3)]
    moe = (w[:, 0:1] * ys[0] + w[:, 1:2] * ys[1]) + (w[:, 2:3] * ys[2] + w[:, 3:4] * ys[3])
    o_ref[...] = _rmsnorm(x1_ref[...] + g2_ref[0] * moe, fg_ref[...])


def _combine_call(prev, y4, top_w, x1, g2, fg, tokens_per_batch, tm, start):
    n, d = x1.shape
    per = tokens_per_batch // tm
    nt = top_w.shape[0] // tm
    first = start // tm
    yspec = lambda k: pl.BlockSpec((tm, d // 2), lambda i: (k * nt + i, 0))
    in_specs = [yspec(0), yspec(1), yspec(2), yspec(3),
                pl.BlockSpec((tm, LANES), lambda i: (i, 0)),
                pl.BlockSpec((tm, d), lambda i: (first + i, 0)),
                pl.BlockSpec((1, 1, d), lambda i: ((first + i) // per, 0, 0)),
                pl.BlockSpec((1, d), lambda i: (0, 0))]
    args = [y4, y4, y4, y4, top_w, x1, g2, fg]
    kern, aliases = _combine_kernel, {}
    if prev is not None:
        in_specs.append(pl.BlockSpec(memory_space=pl.ANY))
        args.append(prev)
        kern = lambda *refs: _combine_kernel(*refs[:8], refs[9])
        aliases = {8: 0}
    return pl.pallas_call(
        kern,
        grid=(nt,),
        in_specs=in_specs,
        out_specs=pl.BlockSpec((tm, d), lambda i: (first + i, 0)),
        out_shape=jax.ShapeDtypeStruct((n, d), F32),
        input_output_aliases=aliases,
        compiler_params=_cparams(("parallel",), 48),
        name="combine",
    )(*args)


def _slot_kernel(start_ref, idx_ref, rank_ref, o_ref):
    idx = idx_ref[...]
    acc = rank_ref[...]
    for e in range(N_EXPERTS):
        acc = acc + jnp.where(idx == e, start_ref[e], 0)
    o_ref[...] = acc


def _slot_call(pad_start, top_idx, rank):
    full = pl.BlockSpec(top_idx.shape, lambda i, s: (0, 0))
    return pl.pallas_call(
        _slot_kernel,
        grid_spec=pltpu.PrefetchScalarGridSpec(num_scalar_prefetch=1, grid=(1,), in_specs=[full, full],
                                               out_specs=full),
        out_shape=jax.ShapeDtypeStruct(top_idx.shape, I32),
        compiler_params=_cparams(("arbitrary",), 32),
        name="slots",
    )(pad_start, top_idx, rank)


def _routing_tables(top_idx, rank, counts, n):
    tmb = EXPERT_BLOCK
    counts = counts.astype(I32)
    padded = (counts + tmb - 1) // tmb * tmb
    pad_end = jnp.cumsum(padded)
    pad_start = pad_end - padded
    n_blocks = -(-(n * TOP_K + N_EXPERTS * (tmb - 1)) // tmb)
    n_slots = n_blocks * tmb
    dest_flat = _slot_call(pad_start, top_idx, rank)[:TOP_K].reshape(-1)
    block_start = jnp.arange(n_blocks, dtype=I32) * tmb
    block_e = jnp.minimum(jnp.sum((pad_end[None, :] <= block_start[:, None]).astype(I32), axis=1), N_EXPERTS - 1)
    n_valid = (pad_end[-1:] // tmb).astype(I32)
    experts = jnp.arange(N_EXPERTS, dtype=I32)
    used = counts > 0
    later = jnp.where(jnp.logical_and(used[None, :], experts[None, :] > experts[:, None]), experts[None, :], N_EXPERTS)
    next_e = jnp.min(later, axis=1)
    next_e = jnp.where(next_e == N_EXPERTS, -1, next_e).astype(I32)
    parity = ((jnp.cumsum(used.astype(I32)) - used.astype(I32)) % 2).astype(I32)
    row_end = jnp.sum(jnp.where(block_e[:, None] == experts[None, :], (pad_start + counts)[None, :], 0), axis=1)
    rows_used = jnp.clip(row_end - block_start, 0, tmb).astype(I32)
    return block_e, n_valid, next_e, parity, rows_used, dest_flat, n_slots


def _gdn_branch(x3, sh, sc, norm_g, w_qkv, w_gates, w_extra, extra_dtypes, conv_w, par, grid_w, use_rows, tm,
                s0, prec):
    q, k, v, go, rows, *extra = _inconv_call(x3, sh, sc, norm_g, w_qkv, w_gates, w_extra, extra_dtypes, conv_w, par,
                                             grid_w, use_rows, tm)
    o_f, o_b, s_fin = _gdn_call(q, k, v, go, rows, s0, prec)
    return extra, o_f, o_b, s_fin


def kernel(x, c, ctx, c_ctx, w_mod, b_mod, norm1_g, norm2_g, w_in, conv_w, a_log, dt_bias, gdn_norm_g,
           w_fourier_out, w_gdn_out, w_merge_out, w_router, b_router, w_gate, b_gate, w_up, b_up,
           w_down, b_down, final_norm_g):
    b, l, d = x.shape
    n = b * l
    n_ctx = ctx.shape[1]
    assert w_mod.shape[0] == 1 and l == GRID_W * GRID_W and d == V_DIM
    prec = None

    c8 = jnp.concatenate([c, c_ctx[None, :], jnp.zeros((8 - b - 1, d), F32)], axis=0)
    mod = _mod_call(c8, w_mod[0], b_mod[0])
    sh1, sc1, g1, sh2, sc2, g2 = [mod[:b, j * d:(j + 1) * d].reshape(b, 1, d) for j in range(6)]
    csh1 = jnp.broadcast_to(mod[b:b + 1, 0:d].reshape(1, 1, d), (b, 1, d))
    csc1 = jnp.broadcast_to(mod[b:b + 1, d:2 * d].reshape(1, 1, d), (b, 1, d))

    wi = w_in[0]
    off_gate = QKV_DIM
    off_z = off_gate + 4 * NV_HEADS
    off_f = off_z + V_DIM
    off_ga = off_f + F_DIM
    w_qkv = wi[:, :QKV_DIM].astype(BF16)
    w_gates = jnp.pad(wi[:, off_gate:off_z], ((0, 0), (0, LANES - 4 * NV_HEADS))).astype(BF16)
    w_extra = [wi[:, off_z:off_f].astype(BF16), wi[:, off_f:off_ga].astype(BF16), wi[:, off_ga:].astype(BF16)]
    par = jnp.pad(jnp.stack([a_log[0].reshape(-1), dt_bias[0].reshape(-1)]),
                  ((0, 6), (GATE_LANE0, LANES - 2 * GATE_LANE0)))
    n1 = norm1_g[0].reshape(1, d)
    cw = conv_w[0].reshape(9, QKV_DIM)

    zero_state = jnp.zeros((b, 2, NV_HEADS, HEAD_DIM, HEAD_DIM), F32)
    _, _, _, s_ctx = _gdn_branch(ctx, csh1, csc1, n1, w_qkv, w_gates, [], [], cw, par, n_ctx, False, n_ctx,
                                 zero_state, prec)

    x2 = x.reshape(n, d)
    (z, f, gab), o_f, o_b, _ = _gdn_branch(x, sh1, sc1, n1, w_qkv, w_gates, w_extra, (BF16, F32, BF16),
                                           cw, par, GRID_W, True, 512, s_ctx, prec)
    z, gab = z.reshape(n, V_DIM), gab.reshape(n, 2 * d)
    fmix = _fnet_call(f).reshape(n, F_DIM)

    wr = jnp.pad(w_router[0], ((0, 0), (0, LANES - N_EXPERTS)))
    br = jnp.pad(b_router[0], (0, LANES - N_EXPERTS), constant_values=NEG_BIG).reshape(1, LANES)
    x1, h2, logits = _merge_call(
        o_f.reshape(n, V_DIM), o_b.reshape(n, V_DIM), z, fmix, gab, x2, g1, sh2, sc2,
        gdn_norm_g[0].reshape(1, HEAD_DIM), norm2_g[0].reshape(1, d),
        w_gdn_out[0].astype(BF16), w_fourier_out[0].astype(BF16), w_merge_out[0].astype(BF16), wr, br, l, 512)

    out = None
    part = n // MOE_PARTS
    for start in range(0, n, part):
        top_idx, top_w, rank, counts = _route_call(logits, 512, start, part)
        block_e, n_valid, next_e, parity, rows_used, dest_flat, n_slots = _routing_tables(
            top_idx, rank, counts[0, :N_EXPERTS], part)
        xs = _sc_scatter_rows(h2, dest_flat, n_slots, start, part)
        ys = _expert_call(block_e, n_valid, next_e, parity, rows_used, xs, w_gate[0], w_up[0], w_down[0],
                          b_gate[0], b_up[0], b_down[0])
        y4 = _sc_gather_rows(ys, dest_flat)
        out = _combine_call(out, y4, top_w, x1, g2, final_norm_g.reshape(1, d), l, 512, start)
    return out.reshape(b, l, d)
```

```python
import functools
import math

import jax
import jax.numpy as jnp
import numpy as np
from jax import lax
from jax.experimental import pallas as pl
from jax.experimental.pallas import tpu as pltpu
from jax.experimental.pallas import tpu_sc as plsc

F32 = jnp.float32
BF16 = jnp.bfloat16
I32 = jnp.int32
HIGHEST = lax.Precision.HIGHEST

GRID_W = 64
NQK_HEADS = 4
NV_HEADS = 8
HEAD_DIM = 128
QK_DIM = NQK_HEADS * HEAD_DIM
V_DIM = NV_HEADS * HEAD_DIM
QKV_DIM = 2 * QK_DIM + V_DIM
F_GROUPS = 4
F_DIM = F_GROUPS * HEAD_DIM
N_EXPERTS = 32
TOP_K = 4
SWIGLU_ALPHA = 1.702
SWIGLU_LIMIT = 7.0
EPS = 1e-6

LANES = 128
SUBLANES = 8
GATE_LANE0 = 16
GDN_CHUNK = 128
EXPERT_BLOCK = 512
TOKEN_TILE = 512
SC_CORES = 2
SC_SUBCORES = 16
SC_WORKERS = SC_CORES * SC_SUBCORES
SC_WINDOW = 64
NEG_BIG = -1e30
MIB = 2 ** 20


def _cparams(sem, vmem_mib):
    return pltpu.CompilerParams(dimension_semantics=sem, vmem_limit_bytes=vmem_mib * MIB)


def _mm(a, b, prec=None, dims=(((1,), (0,)), ((), ()))):
    if prec is None:
        return lax.dot_general(a.astype(BF16), b.astype(BF16), dims, preferred_element_type=F32)
    return lax.dot_general(a.astype(F32), b.astype(F32), dims, precision=prec, preferred_element_type=F32)


def _sigmoid(x):
    return 1.0 / (1.0 + jnp.exp(-x))


def _rmsnorm(x, g):
    return x * lax.rsqrt(jnp.mean(x * x, axis=-1, keepdims=True) + EPS) * g


def _pack_bf16_pairs(x):
    half = x.shape[1] // 2
    bits = lax.bitcast_convert_type(x.astype(BF16).astype(F32), jnp.uint32)
    packed = jnp.bitwise_or(jnp.right_shift(bits[:, :half], jnp.uint32(16)),
                            jnp.bitwise_and(bits[:, half:], jnp.uint32(0xFFFF0000)))
    return lax.bitcast_convert_type(packed, I32)


def _unpack_bf16_pairs(p):
    bits = lax.bitcast_convert_type(p, jnp.uint32)
    lo = lax.bitcast_convert_type(jnp.left_shift(bits, jnp.uint32(16)), F32)
    hi = lax.bitcast_convert_type(jnp.bitwise_and(bits, jnp.uint32(0xFFFF0000)), F32)
    return jnp.concatenate([lo, hi], axis=1)


def _mod_kernel(c_ref, w_ref, b_ref, o_ref):
    c = c_ref[...]
    o_ref[...] = _mm(c * _sigmoid(c), w_ref[...], HIGHEST) + b_ref[...]


def _mod_call(c8, w_mod, b_mod):
    d, n = w_mod.shape
    tn = 1536
    return pl.pallas_call(
        _mod_kernel,
        grid=(n // tn,),
        in_specs=[pl.BlockSpec((8, d), lambda j: (0, 0)),
                  pl.BlockSpec((d, tn), lambda j: (0, j)),
                  pl.BlockSpec((1, tn), lambda j: (0, j))],
        out_specs=pl.BlockSpec((8, tn), lambda j: (0, j)),
        out_shape=jax.ShapeDtypeStruct((8, n), F32),
        compiler_params=_cparams(("parallel",), 32),
        name="mod",
    )(c8, w_mod, b_mod.reshape(1, n))


def _inconv_kernel(*refs, grid_w, use_rows, tm, cw, n_extra):
    refs = list(refs)
    prev_ref = refs.pop(0) if use_rows else None
    x_ref = refs.pop(0)
    next_ref = refs.pop(0) if use_rows else None
    sh_ref, sc_ref, g_ref, wq_ref, wgate_ref = refs[:5]
    wx_refs = refs[5:5 + n_extra]
    w_ref, par_ref, q_ref, k_ref, v_ref, go_ref, gr_ref = refs[5 + n_extra:12 + n_extra]
    ox_refs = refs[12 + n_extra:]

    def modulated(x):
        return (_rmsnorm(x, g_ref[...]) * (1.0 + sc_ref[0]) + sh_ref[0]).astype(BF16)

    r = pl.program_id(1)
    nr = pl.num_programs(1)
    u = modulated(x_ref[0])

    def project(wx_ref, o_ref, c0, step):
        o_ref[0, :, c0:c0 + step] = _mm(u, wx_ref[:, c0:c0 + step]).astype(o_ref.dtype)

    plain = [functools.partial(project, wx_ref, o_ref, c0, min(wx_ref.shape[1], 512))
             for wx_ref, o_ref in zip(wx_refs, ox_refs) for c0 in range(0, wx_ref.shape[1], min(wx_ref.shape[1], 512))]
    n_conv_chunks = QKV_DIM // cw

    t = lax.broadcasted_iota(I32, (tm, 1), 0)
    col = jnp.bitwise_and(t, grid_w - 1)
    m_left = (col != 0).astype(F32)
    m_right = (col != grid_w - 1).astype(F32)
    if use_rows:
        u_prev = modulated(prev_ref[0])
        u_next = modulated(next_ref[0])
        has_prev = (r > 0).astype(F32)
        has_next = (r < nr - 1).astype(F32)
    for c0 in range(0, QKV_DIM, cw):
        wq = wq_ref[:, c0:c0 + cw]
        xm = _mm(u, wq)
        if use_rows:
            up = jnp.concatenate([_mm(u_prev, wq) * has_prev, xm[:tm - grid_w]], axis=0)
            dn = jnp.concatenate([xm[grid_w:], _mm(u_next, wq) * has_next], axis=0)

        def colsum(kc):
            y = xm * w_ref[3 + kc:4 + kc, c0:c0 + cw]
            if use_rows:
                y = y + up * w_ref[kc:kc + 1, c0:c0 + cw] + dn * w_ref[6 + kc:7 + kc, c0:c0 + cw]
            return y

        acc = (colsum(1) + pltpu.roll(colsum(0), 1, axis=0) * m_left
               + pltpu.roll(colsum(2), tm - 1, axis=0) * m_right)
        s = acc * _sigmoid(acc)
        for h0 in range(0, cw, HEAD_DIM):
            c = c0 + h0
            seg = s[:, h0:h0 + HEAD_DIM]
            if c < 2 * QK_DIM:
                seg = seg * lax.rsqrt(jnp.sum(seg * seg, axis=-1, keepdims=True) + EPS)
            if c < QK_DIM:
                q_ref[0, :, c:c + HEAD_DIM] = (seg * HEAD_DIM ** -0.5).astype(q_ref.dtype)
            elif c < 2 * QK_DIM:
                k_ref[0, :, c - QK_DIM:c - QK_DIM + HEAD_DIM] = seg.astype(k_ref.dtype)
            else:
                v_ref[0, :, c - 2 * QK_DIM:c - 2 * QK_DIM + HEAD_DIM] = seg.astype(v_ref.dtype)
        ci = c0 // cw
        for job in plain[ci * len(plain) // n_conv_chunks:(ci + 1) * len(plain) // n_conv_chunks]:
            job()
    g = _mm(u, wgate_ref[...])
    a = g + par_ref[1:2, :]
    softplus = jnp.maximum(a, 0.0) + jnp.log1p(jnp.exp(-jnp.abs(a)))
    log_g = -jnp.exp(par_ref[0:1, :]) * softplus
    lane = lax.broadcasted_iota(I32, g.shape, 1)
    go_ref[0] = jnp.where(lane < GATE_LANE0, _sigmoid(g), log_g)
    gr_ref[0] = jnp.transpose(log_g)[GATE_LANE0:GATE_LANE0 + 2 * NV_HEADS]


def _inconv_call(x3, sh, sc, g, w_qkv, w_gates, w_extra, extra_dtypes, conv_w, par, grid_w, use_rows, tm):
    b, t, d = x3.shape
    kern = functools.partial(_inconv_kernel, grid_w=grid_w, use_rows=use_rows, tm=tm, cw=512, n_extra=len(w_extra))
    per = tm // grid_w
    nrow = t // grid_w
    tile = lambda wd: pl.BlockSpec((1, tm, wd), lambda i, r: (i, r, 0))
    vec = pl.BlockSpec((1, 1, d), lambda i, r: (i, 0, 0))
    const = lambda a: pl.BlockSpec(a.shape, lambda i, r: (0,) * a.ndim, pipeline_mode=pl.Buffered(1))
    in_specs, args = [], []
    if use_rows:
        in_specs.append(pl.BlockSpec((1, grid_w, d), lambda i, r: (i, jnp.maximum(r * per - 1, 0), 0)))
        args.append(x3)
    in_specs.append(tile(d))
    args.append(x3)
    if use_rows:
        in_specs.append(pl.BlockSpec((1, grid_w, d), lambda i, r: (i, jnp.minimum((r + 1) * per, nrow - 1), 0)))
        args.append(x3)
    consts = [g, w_qkv, w_gates, *w_extra, conv_w, par]
    in_specs += [vec, vec] + [const(a) for a in consts]
    args += [sh, sc] + consts
    widths = [QK_DIM, QK_DIM, V_DIM, LANES] + [w.shape[1] for w in w_extra]
    dtypes = [BF16, BF16, BF16, F32] + list(extra_dtypes)
    out_shape = [jax.ShapeDtypeStruct((b, t, wd), dt) for wd, dt in zip(widths, dtypes)]
    out_specs = [tile(wd) for wd in widths]
    out_shape.insert(4, jax.ShapeDtypeStruct((b, 2 * NV_HEADS, t), F32))
    out_specs.insert(4, pl.BlockSpec((1, 2 * NV_HEADS, tm), lambda i, r: (i, 0, r)))
    return pl.pallas_call(
        kern, grid=(b, t // tm), in_specs=in_specs, out_specs=out_specs, out_shape=out_shape,
        compiler_params=_cparams(("parallel", "parallel"), 56),
        name="inconv_rows" if use_rows else "inconv_seq",
    )(*args)


def _gdn_kernel(qf, kf, vf, gf, rf, qb, kb, vb, gb, rb, s0_ref, of, ob, sfin_ref, s_ref, *, prec):
    i = pl.program_id(1)
    nc = pl.num_programs(1)

    @pl.when(i == 0)
    def _():
        s_ref[...] = s0_ref[0]

    c = qf.shape[1]
    per = NV_HEADS // NQK_HEADS
    row = lax.broadcasted_iota(I32, (c, c), 0)
    colj = lax.broadcasted_iota(I32, (c, c), 1)
    eye = jnp.where(row == colj, 1.0, 0.0)
    nt_dims = (((1,), (1,)), ((), ()))
    tn_dims = (((0,), (0,)), ((), ()))

    seqs = []
    for d, (q_r, k_r, v_r, g_r, r_r, o_r) in enumerate(((qf, kf, vf, gf, rf, of), (qb, kb, vb, gb, rb, ob))):
        rev = d == 1
        incl = (colj >= row) if rev else (colj <= row)
        strict = (colj > row) if rev else (colj < row)
        gates = g_r[0]
        tri_c = jnp.where(incl, 1.0, 0.0)
        gcm = _mm(tri_c, gates, HIGHEST)
        gcr = _mm(r_r[0], tri_c, HIGHEST, dims=nt_dims)
        for hq in range(NQK_HEADS):
            q = q_r[0, :, hq * HEAD_DIM:(hq + 1) * HEAD_DIM]
            k = k_r[0, :, hq * HEAD_DIM:(hq + 1) * HEAD_DIM]
            kq = lax.dot_general(jnp.concatenate([q, k], axis=0), k, nt_dims, preferred_element_type=F32)
            for j in range(per):
                h = hq * per + j
                idx = d * NV_HEADS + h
                gc_c = gcm[:, GATE_LANE0 + idx:GATE_LANE0 + idx + 1]
                seqs.append(dict(d=d, h=h, o_r=o_r, v_r=v_r, q=q, k=k, qk=kq[:c], kk=kq[c:], incl=incl, strict=strict,
                                 beta=gates[:, idx:idx + 1], gc_c=gc_c, gc_r=gcr[idx:idx + 1, :],
                                 ge=gc_c[0:1] if rev else gc_c[c - 1:c]))

    def same_block(m):
        sh = int(math.log2(m))
        return jnp.right_shift(row, sh) == jnp.right_shift(colj, sh)

    for s in seqs:
        s['decay'] = jnp.where(s['incl'], jnp.exp(jnp.where(s['incl'], s['gc_c'] - s['gc_r'], 0.0)), 0.0)
        s['a'] = jnp.where(s['strict'], s['beta'] * s['kk'] * s['decay'], 0.0)
        s['t'] = eye - jnp.where(same_block(2), s['a'], 0.0)
    m = 4
    while m <= c:
        between = jnp.logical_and(same_block(m), jnp.logical_not(same_block(m // 2)))
        for s in seqs:
            s['te'] = _mm(s['t'], jnp.where(between, s['a'], 0.0), prec)
        for s in seqs:
            s['t'] = s['t'] - _mm(s['te'], s['t'], prec)
        m *= 2
    for s in seqs:
        h = s['h']
        egc = jnp.exp(s['gc_c'])
        kf32 = s['k'].astype(F32)
        v = s['v_r'][0, :, h * HEAD_DIM:(h + 1) * HEAD_DIM].astype(F32)
        rhs = jnp.concatenate([s['beta'] * v, (s['beta'] * egc) * kf32], axis=1)
        s['sol'] = _mm(s['t'], rhs, prec)
        s['q_dec'] = s['q'].astype(F32) * egc
        s['k_dec'] = kf32 * jnp.exp(s['ge'] - s['gc_c'])
    for s in seqs:
        s['ws'] = _mm(jnp.concatenate([s['sol'][:, HEAD_DIM:], s['q_dec']], axis=0), s_ref[s['d'], s['h']])
    for s in seqs:
        s['u'] = s['sol'][:, :HEAD_DIM] - s['ws'][:c]
        s_ref[s['d'], s['h']] = (jnp.exp(s['ge']) * s_ref[s['d'], s['h']]
                                 + _mm(s['k_dec'], s['u'], dims=tn_dims))
    for s in seqs:
        h = s['h']
        o = s['ws'][c:] + _mm(s['qk'] * s['decay'], s['u'])
        s['o_r'][0, :, h * HEAD_DIM:(h + 1) * HEAD_DIM] = o.astype(s['o_r'].dtype)

    @pl.when(i == nc - 1)
    def _():
        sfin_ref[0] = s_ref[...]


def _gdn_call(q, k, v, go, rows, s0, prec):
    b, t, _ = q.shape
    c = GDN_CHUNK
    nc = t // c
    fwd = lambda i, n: (i, n, 0)
    bwd = lambda i, n: (i, nc - 1 - n, 0)
    rfwd = lambda i, n: (i, 0, n)
    rbwd = lambda i, n: (i, 0, nc - 1 - n)
    state_spec = pl.BlockSpec((1,) + s0.shape[1:], lambda i, n: (i, 0, 0, 0, 0))

    def specs(m3, mr):
        return [pl.BlockSpec((1, c, QK_DIM), m3), pl.BlockSpec((1, c, QK_DIM), m3),
                pl.BlockSpec((1, c, V_DIM), m3), pl.BlockSpec((1, c, LANES), m3),
                pl.BlockSpec((1, 2 * NV_HEADS, c), mr)]

    return pl.pallas_call(
        functools.partial(_gdn_kernel, prec=prec),
        grid=(b, nc),
        in_specs=specs(fwd, rfwd) + specs(bwd, rbwd) + [state_spec],
        out_specs=[pl.BlockSpec((1, c, V_DIM), fwd), pl.BlockSpec((1, c, V_DIM), bwd), state_spec],
        out_shape=[jax.ShapeDtypeStruct((b, t, V_DIM), BF16), jax.ShapeDtypeStruct((b, t, V_DIM), BF16),
                   jax.ShapeDtypeStruct(s0.shape, F32)],
        scratch_shapes=[pltpu.VMEM(s0.shape[1:], F32)],
        compiler_params=_cparams(("parallel", "arbitrary"), 48),
        name="gdn",
    )(q, k, v, go, rows, q, k, v, go, rows, s0)


def _fnet1_kernel(x_ref, f_ref, ar_ref, ai_ref):
    n = x_ref.shape[1]
    for j in range(x_ref.shape[2]):
        a = _mm(f_ref[...], x_ref[0, :, j, :])
        ar_ref[0, j] = a[:n]
        ai_ref[0, j] = a[n:]


def _fnet2_kernel(ar_ref, ai_ref, g_ref, wc_ref, o_ref, *, scale):
    n = ar_ref.shape[1]
    cols = ar_ref.shape[2]
    zs = []
    for m in range(cols):
        a2 = jnp.concatenate([ar_ref[0, :, m, :], ai_ref[0, :, m, :]], axis=0)
        zs.append(_mm(g_ref[m], a2))
    zr = jnp.concatenate([z[:n] for z in zs], axis=0)
    zi = jnp.concatenate([z[n:] for z in zs], axis=0)
    for g0 in range(0, zr.shape[1], HEAD_DIM):
        y = _mm(jnp.concatenate([zr[:, g0:g0 + HEAD_DIM], zi[:, g0:g0 + HEAD_DIM]], axis=1), wc_ref[...])
        for m in range(cols):
            o_ref[0, :, m, g0:g0 + HEAD_DIM] = y[m * n:(m + 1) * n] * scale


def _fnet_tables(n, groups):
    a = np.arange(n)
    ang1 = 2.0 * np.pi * np.outer(a, a) / n
    f1 = np.concatenate([np.cos(ang1), -np.sin(ang1)], axis=0)
    m = a[:, None] + n * a[None, :]
    ang2 = 2.0 * np.pi * ((m[:, :, None] * a[None, None, :]) % (n * n)) / (n * n)
    gc, gs = np.cos(ang2), np.sin(ang2)
    g2 = np.concatenate([np.concatenate([gc, gs], axis=2), np.concatenate([-gs, gc], axis=2)], axis=1)
    angc = 2.0 * np.pi * np.outer(np.arange(HEAD_DIM), np.arange(HEAD_DIM)) / HEAD_DIM
    wc = np.concatenate([np.cos(angc), np.sin(angc)], axis=0)
    f = lambda x: jnp.asarray(x, F32).astype(BF16)
    return f(f1), f(g2), f(wc)


def _fnet_call(f):
    b, l, c = f.shape
    n = GRID_W
    assert l == n * n
    f1, g2, wc = _fnet_tables(n, c // HEAD_DIM)
    cols = 2 * SUBLANES
    ar, ai = pl.pallas_call(
        _fnet1_kernel,
        grid=(b, n // cols),
        in_specs=[pl.BlockSpec((1, n, cols, c), lambda i, j: (i, 0, j, 0)),
                  pl.BlockSpec((2 * n, n), lambda i, j: (0, 0))],
        out_specs=[pl.BlockSpec((1, cols, n, c), lambda i, j: (i, j, 0, 0))] * 2,
        out_shape=[jax.ShapeDtypeStruct((b, n, n, c), F32)] * 2,
        compiler_params=_cparams(("parallel", "parallel"), 32),
        name="fnet1",
    )(f.reshape(b, n, n, c), f1)
    out = pl.pallas_call(
        functools.partial(_fnet2_kernel, scale=1.0 / math.sqrt(l * HEAD_DIM)),
        grid=(b, n // cols),
        in_specs=[pl.BlockSpec((1, n, cols, c), lambda i, j: (i, 0, j, 0)),
                  pl.BlockSpec((1, n, cols, c), lambda i, j: (i, 0, j, 0)),
                  pl.BlockSpec((cols, 2 * n, 2 * n), lambda i, j: (j, 0, 0)),
                  pl.BlockSpec((2 * HEAD_DIM, HEAD_DIM), lambda i, j: (0, 0))],
        out_specs=pl.BlockSpec((1, n, cols, c), lambda i, j: (i, 0, j, 0)),
        out_shape=jax.ShapeDtypeStruct((b, n, n, c), F32),
        compiler_params=_cparams(("parallel", "parallel"), 32),
        name="fnet2",
    )(ar, ai, g2, wc)
    return out.reshape(b, l, c)


def _merge_kernel(of_ref, ob_ref, z_ref, fm_ref, gab_ref, x_ref, g1_ref, sh2_ref, sc2_ref, gn_ref, n2_ref,
                  wg_ref, wf_ref, wm_ref, wr_ref, br_ref, x1_ref, h2_ref, lg_ref):
    d = x_ref.shape[1]
    tm = x_ref.shape[0]
    halves = [slice(0, tm // 2), slice(tm // 2, tm)]
    yb_in = []
    for r in halves:
        o = of_ref[r, :].astype(F32) + ob_ref[r, :].astype(F32)
        z = z_ref[r, :].astype(F32)
        parts = []
        for h0 in range(0, V_DIM, HEAD_DIM):
            oh = o[:, h0:h0 + HEAD_DIM]
            parts.append(oh * lax.rsqrt(jnp.mean(oh * oh, axis=-1, keepdims=True) + EPS) * gn_ref[...])
        yb_in.append(jnp.concatenate(parts, axis=1) * (z * _sigmoid(z)))
    yb = [_mm(v, wg_ref[...]) for v in yb_in]
    ya = [_mm(fm_ref[r, :], wf_ref[...]) for r in halves]
    mixed = [_sigmoid(gab_ref[r, :d].astype(F32)) * a + _sigmoid(gab_ref[r, d:].astype(F32)) * b_
             for r, a, b_ in zip(halves, ya, yb)]
    mm = [_mm(v, wm_ref[...]) for v in mixed]
    w = wr_ref[...]
    w_hi = w.astype(BF16)
    w2 = jnp.concatenate([w_hi, (w - w_hi.astype(F32)).astype(BF16)], axis=1)
    for r, v in zip(halves, mm):
        x1 = x_ref[r, :] + g1_ref[0] * v
        x1_ref[r, :] = x1
        h2 = _rmsnorm(x1, n2_ref[...]) * (1.0 + sc2_ref[0]) + sh2_ref[0]
        h2_ref[r, :] = _pack_bf16_pairs(h2)
        h_hi = h2.astype(BF16)
        h_lo = (h2 - h_hi.astype(F32)).astype(BF16)
        part = _mm(h_hi, w2)
        lg_ref[r, :] = (part[:, :LANES] + part[:, LANES:]) + _mm(h_lo, w_hi) + br_ref[...]


def _merge_call(of, ob, z, fm, gab, x2, g1, sh2, sc2, gn, n2, wg, wf, wm, wr, br, tokens_per_batch, tm):
    n, d = x2.shape
    per = tokens_per_batch // tm
    tok = lambda wd: pl.BlockSpec((tm, wd), lambda i: (i, 0))
    vec = pl.BlockSpec((1, 1, d), lambda i: (i // per, 0, 0))
    full = lambda a: pl.BlockSpec(a.shape, lambda i: (0,) * a.ndim)
    return pl.pallas_call(
        _merge_kernel,
        grid=(n // tm,),
        in_specs=[tok(V_DIM), tok(V_DIM), tok(V_DIM), tok(F_DIM), tok(2 * d), tok(d), vec, vec, vec,
                  full(gn), full(n2), full(wg), full(wf), full(wm), full(wr), full(br)],
        out_specs=[tok(d), tok(d // 2), tok(LANES)],
        out_shape=[jax.ShapeDtypeStruct((n, d), F32), jax.ShapeDtypeStruct((n, d // 2), I32),
                   jax.ShapeDtypeStruct((n, LANES), F32)],
        compiler_params=_cparams(("parallel",), 56),
        name="merge",
    )(of, ob, z, fm, gab, x2, g1, sh2, sc2, gn, n2, wg, wf, wm, wr, br)


def _route_kernel(lg_ref, idx_ref, w_ref, rank_ref, cnt_ref, run_ref):
    i = pl.program_id(0)

    @pl.when(i == 0)
    def _():
        run_ref[...] = jnp.zeros_like(run_ref)

    l = lg_ref[...]
    tm = l.shape[0]
    lane = lax.broadcasted_iota(I32, l.shape, 1)
    vals, idxs = [], []
    for _ in range(TOP_K):
        m = jnp.max(l, axis=-1, keepdims=True)
        idx = jnp.min(jnp.where(l == m, lane, LANES), axis=-1, keepdims=True)
        vals.append(m)
        idxs.append(idx)
        l = jnp.where(lane == idx, NEG_BIG * 2.0, l)
    es = [jnp.exp(v - vals[0]) for v in vals]
    inv = 1.0 / (es[0] + es[1] + es[2] + es[3])
    picked = jnp.zeros(l.shape, F32)
    for idx in idxs:
        picked = picked + (lane == idx).astype(F32)
    r = lax.broadcasted_iota(I32, (tm, tm), 0)
    cidx = lax.broadcasted_iota(I32, (tm, tm), 1)
    before = _mm(jnp.where(cidx < r, 1.0, 0.0), picked) + run_ref[...]
    idx_out = jnp.zeros(l.shape, I32)
    w_out = jnp.zeros(l.shape, F32)
    rank_out = jnp.zeros(l.shape, F32)
    for k in range(TOP_K):
        rk = jnp.sum(jnp.where(lane == idxs[k], before, 0.0), axis=-1, keepdims=True)
        idx_out = jnp.where(lane == k, idxs[k], idx_out)
        w_out = jnp.where(lane == k, es[k] * inv, w_out)
        rank_out = jnp.where(lane == k, rk, rank_out)
    idx_ref[...] = jnp.transpose(idx_out.astype(F32))[:SUBLANES].astype(I32)
    w_ref[...] = w_out
    rank_ref[...] = jnp.transpose(rank_out)[:SUBLANES].astype(I32)
    run_ref[...] = run_ref[...] + jnp.sum(picked, axis=0, keepdims=True)
    cnt_ref[...] = run_ref[...]


def _route_call(logits, tm):
    n = logits.shape[0]
    tok = pl.BlockSpec((tm, LANES), lambda i: (i, 0))
    tok_t = pl.BlockSpec((SUBLANES, tm), lambda i: (0, i))
    return pl.pallas_call(
        _route_kernel,
        grid=(n // tm,),
        in_specs=[tok],
        out_specs=[tok_t, tok, tok_t, pl.BlockSpec((1, LANES), lambda i: (0, 0))],
        out_shape=[jax.ShapeDtypeStruct((SUBLANES, n), I32), jax.ShapeDtypeStruct((n, LANES), F32),
                   jax.ShapeDtypeStruct((SUBLANES, n), I32), jax.ShapeDtypeStruct((1, LANES), F32)],
        scratch_shapes=[pltpu.VMEM((1, LANES), F32)],
        compiler_params=_cparams(("arbitrary",), 32),
        name="route",
    )(logits)


def _sc_mesh():
    return plsc.VectorSubcoreMesh(core_axis_name="c", subcore_axis_name="s",
                                  num_cores=SC_CORES, num_subcores=SC_SUBCORES)


def _sc_worker_id():
    return lax.axis_index("s") * SC_CORES + lax.axis_index("c")


def _sc_scatter_rows(x, idx, n_out):
    n, d = x.shape
    per_w = idx.shape[0] // SC_WORKERS
    nwin = per_w // SC_WINDOW
    assert per_w * SC_WORKERS == idx.shape[0] and nwin * SC_WINDOW == per_w and nwin % 2 == 0 and n % per_w == 0

    def body(x_hbm, idx_hbm, out_hbm, idx_v, rows_v, sem_r, sem_w):
        wid = _sc_worker_id()
        row0 = lax.rem(wid * per_w, n)
        pltpu.sync_copy(idx_hbm.at[wid], idx_v)

        def read(j, b):
            return pltpu.make_async_copy(x_hbm.at[pl.ds(row0 + j * SC_WINDOW, SC_WINDOW)], rows_v.at[b], sem_r.at[b])

        def write(j, b):
            return pltpu.make_async_copy(rows_v.at[b], out_hbm.at[idx_v.at[j]], sem_w.at[b])

        @pl.loop(0, nwin, step=2)
        def _(j):
            read(j, 0).start()
            read(j + 1, 1).start()
            read(j, 0).wait()
            write(j, 0).start()
            read(j + 1, 1).wait()
            write(j + 1, 1).start()
            write(j, 0).wait()
            write(j + 1, 1).wait()

    return pl.kernel(
        body, out_type=jax.ShapeDtypeStruct((n_out, d), x.dtype), mesh=_sc_mesh(),
        scratch_types=[pltpu.VMEM((nwin, SC_WINDOW), I32), pltpu.VMEM((2, SC_WINDOW, d), x.dtype),
                       pltpu.SemaphoreType.DMA((2,)), pltpu.SemaphoreType.DMA((2,))],
        name="sc_scatter_rows",
    )(x, idx.reshape(SC_WORKERS, nwin, SC_WINDOW))


def _sc_gather_rows(y, idx):
    d = y.shape[1]
    total = idx.shape[0]
    per_w = total // SC_WORKERS
    nwin = per_w // SC_WINDOW
    assert per_w * SC_WORKERS == total and nwin * SC_WINDOW == per_w and nwin % 2 == 0

    def body(y_hbm, idx_hbm, out_hbm, idx_v, rows_v, sem_r, sem_w):
        wid = _sc_worker_id()
        row0 = wid * per_w
        pltpu.sync_copy(idx_hbm.at[wid], idx_v)

        def read(j, b):
            return pltpu.make_async_copy(y_hbm.at[idx_v.at[j]], rows_v.at[b], sem_r.at[b])

        def write(j, b):
            return pltpu.make_async_copy(rows_v.at[b], out_hbm.at[pl.ds(row0 + j * SC_WINDOW, SC_WINDOW)], sem_w.at[b])

        @pl.loop(0, nwin, step=2)
        def _(j):
            read(j, 0).start()
            read(j + 1, 1).start()
            read(j, 0).wait()
            write(j, 0).start()
            read(j + 1, 1).wait()
            write(j + 1, 1).start()
            write(j, 0).wait()
            write(j + 1, 1).wait()

    return pl.kernel(
        body, out_type=jax.ShapeDtypeStruct((total, d), y.dtype), mesh=_sc_mesh(),
        scratch_types=[pltpu.VMEM((nwin, SC_WINDOW), I32), pltpu.VMEM((2, SC_WINDOW, d), y.dtype),
                       pltpu.SemaphoreType.DMA((2,)), pltpu.SemaphoreType.DMA((2,))],
        name="sc_gather_rows",
    )(y, idx.reshape(SC_WORKERS, nwin, SC_WINDOW))


def _expert_kernel(be_ref, nv_ref, nxt_ref, par_ref, used_ref, x_ref, wg_hbm, wu_hbm, wd_hbm, bg_ref, bu_ref, bd_ref,
                   y_ref, wgf, wuf, wdf, wgb, wub, wdb, sem):
    i = pl.program_id(0)

    def fetch(e, s):
        return [pltpu.make_async_copy(w_hbm.at[e], w_f.at[s], sem.at[s])
                for w_hbm, w_f in ((wg_hbm, wgf), (wu_hbm, wuf), (wd_hbm, wdf))]

    @pl.when(i < nv_ref[0])
    def _():
        e = be_ref[i]
        s = par_ref[e]

        @pl.when(i == 0)
        def _():
            for cp in fetch(e, s):
                cp.start()

        @pl.when(jnp.logical_or(i == 0, be_ref[jnp.maximum(i - 1, 0)] != e))
        def _():
            for cp in fetch(e, s):
                cp.wait()
            wgb[...] = wgf[s].astype(BF16)
            wub[...] = wuf[s].astype(BF16)
            wdb[...] = wdf[s].astype(BF16)

            @pl.when(nxt_ref[e] >= 0)
            def _():
                for cp in fetch(nxt_ref[e], 1 - s):
                    cp.start()

        def ffn(nrows):
            x = _unpack_bf16_pairs(x_ref[:nrows, :]).astype(BF16)
            gate = jnp.minimum(_mm(x, wgb[...]) + bg_ref[0], SWIGLU_LIMIT)
            up = jnp.clip(_mm(x, wub[...]) + bu_ref[0], -SWIGLU_LIMIT, SWIGLU_LIMIT)
            act = (up + 1.0) * gate * _sigmoid(SWIGLU_ALPHA * gate)
            y_ref[:nrows, :] = _pack_bf16_pairs(_mm(act, wdb[...]) + bd_ref[0])

        half = x_ref.shape[0] // 2

        @pl.when(used_ref[i] > half)
        def _():
            ffn(x_ref.shape[0])

        @pl.when(used_ref[i] <= half)
        def _():
            ffn(half)


def _expert_call(block_e, n_valid, next_e, parity, used, xs, w_gate, w_up, w_down, b_gate, b_up, b_down):
    nb = block_e.shape[0]
    tmb = EXPERT_BLOCK
    ne, d, de = w_gate.shape
    bspec = lambda s: pl.BlockSpec((1,) + s, lambda i, be, nv, nx, pa, us: (be[i], 0, 0))
    anyspec = pl.BlockSpec(memory_space=pl.ANY)
    grid_spec = pltpu.PrefetchScalarGridSpec(
        num_scalar_prefetch=5,
        grid=(nb,),
        in_specs=[pl.BlockSpec((tmb, d // 2), lambda i, be, nv, nx, pa, us: (i, 0)),
                  anyspec, anyspec, anyspec, bspec((1, de)), bspec((1, de)), bspec((1, d))],
        out_specs=pl.BlockSpec((tmb, d // 2), lambda i, be, nv, nx, pa, us: (i, 0)),
        scratch_shapes=[pltpu.VMEM((2, d, de), F32), pltpu.VMEM((2, d, de), F32), pltpu.VMEM((2, de, d), F32),
                        pltpu.VMEM((d, de), BF16), pltpu.VMEM((d, de), BF16), pltpu.VMEM((de, d), BF16),
                        pltpu.SemaphoreType.DMA((2,))],
    )
    return pl.pallas_call(
        _expert_kernel,
        grid_spec=grid_spec,
        out_shape=jax.ShapeDtypeStruct(xs.shape, xs.dtype),
        compiler_params=_cparams(("arbitrary",), 56),
        name="expert",
    )(block_e, n_valid, next_e, parity, used, xs, w_gate, w_up, w_down,
      b_gate.reshape(ne, 1, de), b_up.reshape(ne, 1, de), b_down.reshape(ne, 1, d))


def _combine_kernel(y0, y1, y2, y3, w_ref, x1_ref, g2_ref, fg_ref, o_ref):
    w = w_ref[...]
    ys = [_unpack_bf16_pairs(y[...]) for y in (y0, y1, y2, y3)]
    moe = (w[:, 0:1] * ys[0] + w[:, 1:2] * ys[1]) + (w[:, 2:3] * ys[2] + w[:, 3:4] * ys[3])
    o_ref[...] = _rmsnorm(x1_ref[...] + g2_ref[0] * moe, fg_ref[...])


def _combine_call(y4, top_w, x1, g2, fg, tokens_per_batch, tm):
    n, d = x1.shape
    per = tokens_per_batch // tm
    nt = n // tm
    yspec = lambda k: pl.BlockSpec((tm, d // 2), lambda i: (k * nt + i, 0))
    return pl.pallas_call(
        _combine_kernel,
        grid=(nt,),
        in_specs=[yspec(0), yspec(1), yspec(2), yspec(3),
                  pl.BlockSpec((tm, LANES), lambda i: (i, 0)),
                  pl.BlockSpec((tm, d), lambda i: (i, 0)),
                  pl.BlockSpec((1, 1, d), lambda i: (i // per, 0, 0)),
                  pl.BlockSpec((1, d), lambda i: (0, 0))],
        out_specs=pl.BlockSpec((tm, d), lambda i: (i, 0)),
        out_shape=jax.ShapeDtypeStruct((n, d), F32),
        compiler_params=_cparams(("parallel",), 48),
        name="combine",
    )(y4, y4, y4, y4, top_w, x1, g2, fg)


def _slot_kernel(start_ref, idx_ref, rank_ref, o_ref):
    idx = idx_ref[...]
    acc = rank_ref[...]
    for e in range(N_EXPERTS):
        acc = acc + jnp.where(idx == e, start_ref[e], 0)
    o_ref[...] = acc


def _slot_call(pad_start, top_idx, rank):
    full = pl.BlockSpec(top_idx.shape, lambda i, s: (0, 0))
    return pl.pallas_call(
        _slot_kernel,
        grid_spec=pltpu.PrefetchScalarGridSpec(num_scalar_prefetch=1, grid=(1,), in_specs=[full, full],
                                               out_specs=full),
        out_shape=jax.ShapeDtypeStruct(top_idx.shape, I32),
        compiler_params=_cparams(("arbitrary",), 32),
        name="slots",
    )(pad_start, top_idx, rank)


def _routing_tables(top_idx, rank, counts, n):
    tmb = EXPERT_BLOCK
    counts = counts.astype(I32)
    padded = (counts + tmb - 1) // tmb * tmb
    pad_end = jnp.cumsum(padded)
    pad_start = pad_end - padded
    n_blocks = -(-(n * TOP_K + N_EXPERTS * (tmb - 1)) // tmb)
    n_slots = n_blocks * tmb
    dest_flat = _slot_call(pad_start, top_idx, rank)[:TOP_K].reshape(-1)
    block_start = jnp.arange(n_blocks, dtype=I32) * tmb
    block_e = jnp.minimum(jnp.sum((pad_end[None, :] <= block_start[:, None]).astype(I32), axis=1), N_EXPERTS - 1)
    n_valid = (pad_end[-1:] // tmb).astype(I32)
    experts = jnp.arange(N_EXPERTS, dtype=I32)
    used = counts > 0
    later = jnp.where(jnp.logical_and(used[None, :], experts[None, :] > experts[:, None]), experts[None, :], N_EXPERTS)
    next_e = jnp.min(later, axis=1)
    next_e = jnp.where(next_e == N_EXPERTS, -1, next_e).astype(I32)
    parity = ((jnp.cumsum(used.astype(I32)) - used.astype(I32)) % 2).astype(I32)
    row_end = jnp.sum(jnp.where(block_e[:, None] == experts[None, :], (pad_start + counts)[None, :], 0), axis=1)
    rows_used = jnp.clip(row_end - block_start, 0, tmb).astype(I32)
    return block_e, n_valid, next_e, parity, rows_used, dest_flat, n_slots


def _gdn_branch(x3, sh, sc, norm_g, w_qkv, w_gates, w_extra, extra_dtypes, conv_w, par, grid_w, use_rows, tm,
                s0, prec):
    q, k, v, go, rows, *extra = _inconv_call(x3, sh, sc, norm_g, w_qkv, w_gates, w_extra, extra_dtypes, conv_w, par,
                                             grid_w, use_rows, tm)
    o_f, o_b, s_fin = _gdn_call(q, k, v, go, rows, s0, prec)
    return extra, o_f, o_b, s_fin


def kernel(x, c, ctx, c_ctx, w_mod, b_mod, norm1_g, norm2_g, w_in, conv_w, a_log, dt_bias, gdn_norm_g,
           w_fourier_out, w_gdn_out, w_merge_out, w_router, b_router, w_gate, b_gate, w_up, b_up,
           w_down, b_down, final_norm_g):
    b, l, d = x.shape
    n = b * l
    n_ctx = ctx.shape[1]
    assert w_mod.shape[0] == 1 and l == GRID_W * GRID_W and d == V_DIM
    prec = None

    c8 = jnp.concatenate([c, c_ctx[None, :], jnp.zeros((8 - b - 1, d), F32)], axis=0)
    mod = _mod_call(c8, w_mod[0], b_mod[0])
    sh1, sc1, g1, sh2, sc2, g2 = [mod[:b, j * d:(j + 1) * d].reshape(b, 1, d) for j in range(6)]
    csh1 = jnp.broadcast_to(mod[b:b + 1, 0:d].reshape(1, 1, d), (b, 1, d))
    csc1 = jnp.broadcast_to(mod[b:b + 1, d:2 * d].reshape(1, 1, d), (b, 1, d))

    wi = w_in[0]
    off_gate = QKV_DIM
    off_z = off_gate + 4 * NV_HEADS
    off_f = off_z + V_DIM
    off_ga = off_f + F_DIM
    w_qkv = wi[:, :QKV_DIM].astype(BF16)
    w_gates = jnp.pad(wi[:, off_gate:off_z], ((0, 0), (0, LANES - 4 * NV_HEADS))).astype(BF16)
    w_extra = [wi[:, off_z:off_f].astype(BF16), wi[:, off_f:off_ga].astype(BF16), wi[:, off_ga:].astype(BF16)]
    par = jnp.pad(jnp.stack([a_log[0].reshape(-1), dt_bias[0].reshape(-1)]),
                  ((0, 6), (GATE_LANE0, LANES - 2 * GATE_LANE0)))
    n1 = norm1_g[0].reshape(1, d)
    cw = conv_w[0].reshape(9, QKV_DIM)

    zero_state = jnp.zeros((b, 2, NV_HEADS, HEAD_DIM, HEAD_DIM), F32)
    _, _, _, s_ctx = _gdn_branch(ctx, csh1, csc1, n1, w_qkv, w_gates, [], [], cw, par, n_ctx, False, n_ctx,
                                 zero_state, prec)

    x2 = x.reshape(n, d)
    (z, f, gab), o_f, o_b, _ = _gdn_branch(x, sh1, sc1, n1, w_qkv, w_gates, w_extra, (BF16, F32, BF16),
                                           cw, par, GRID_W, True, TOKEN_TILE, s_ctx, prec)
    z, gab = z.reshape(n, V_DIM), gab.reshape(n, 2 * d)
    fmix = _fnet_call(f).reshape(n, F_DIM)

    wr = jnp.pad(w_router[0], ((0, 0), (0, LANES - N_EXPERTS)))
    br = jnp.pad(b_router[0], (0, LANES - N_EXPERTS), constant_values=NEG_BIG).reshape(1, LANES)
    x1, h2, logits = _merge_call(
        o_f.reshape(n, V_DIM), o_b.reshape(n, V_DIM), z, fmix, gab, x2, g1, sh2, sc2,
        gdn_norm_g[0].reshape(1, HEAD_DIM), norm2_g[0].reshape(1, d),
        w_gdn_out[0].astype(BF16), w_fourier_out[0].astype(BF16), w_merge_out[0].astype(BF16), wr, br, l, TOKEN_TILE)

    top_idx, top_w, rank, counts = _route_call(logits, TOKEN_TILE)
    block_e, n_valid, next_e, parity, rows_used, dest_flat, n_slots = _routing_tables(
        top_idx, rank, counts[0, :N_EXPERTS], n)
    xs = _sc_scatter_rows(h2, dest_flat, n_slots)
    ys = _expert_call(block_e, n_valid, next_e, parity, rows_used, xs, w_gate[0], w_up[0], w_down[0],
                      b_gate[0], b_up[0], b_down[0])
    y4 = _sc_gather_rows(ys, dest_flat)
    out = _combine_call(y4, top_w, x1, g2, final_norm_g.reshape(1, d), l, TOKEN_TILE)
    return out.reshape(b, l, d)
```

```python
import functools
import math

import jax
import jax.numpy as jnp
import numpy as np
from jax import lax
from jax.experimental import pallas as pl
from jax.experimental.pallas import tpu as pltpu
from jax.experimental.pallas import tpu_sc as plsc

F32 = jnp.float32
BF16 = jnp.bfloat16
I32 = jnp.int32
HIGHEST = lax.Precision.HIGHEST

GRID_W = 64
NQK_HEADS = 4
NV_HEADS = 8
HEAD_DIM = 128
QK_DIM = NQK_HEADS * HEAD_DIM
V_DIM = NV_HEADS * HEAD_DIM
QKV_DIM = 2 * QK_DIM + V_DIM
F_GROUPS = 4
F_DIM = F_GROUPS * HEAD_DIM
N_EXPERTS = 32
TOP_K = 4
SWIGLU_ALPHA = 1.702
SWIGLU_LIMIT = 7.0
EPS = 1e-6

LANES = 128
SUBLANES = 8
GATE_LANE0 = 16
GDN_CHUNK = 128
EXPERT_BLOCK = 512
TOKEN_TILE = 512
SC_CORES = 2
SC_SUBCORES = 16
SC_WORKERS = SC_CORES * SC_SUBCORES
SC_WINDOW = 64
NEG_BIG = -1e30
MIB = 2 ** 20


def _cparams(sem, vmem_mib):
    return pltpu.CompilerParams(dimension_semantics=sem, vmem_limit_bytes=vmem_mib * MIB)


def _mm(a, b, prec=None, dims=(((1,), (0,)), ((), ()))):
    if prec is None:
        return lax.dot_general(a.astype(BF16), b.astype(BF16), dims, preferred_element_type=F32)
    return lax.dot_general(a.astype(F32), b.astype(F32), dims, precision=prec, preferred_element_type=F32)


def _sigmoid(x):
    return 1.0 / (1.0 + jnp.exp(-x))


def _rmsnorm(x, g):
    return x * lax.rsqrt(jnp.mean(x * x, axis=-1, keepdims=True) + EPS) * g


def _pack_bf16_pairs(x):
    half = x.shape[1] // 2
    bits = lax.bitcast_convert_type(x.astype(BF16).astype(F32), jnp.uint32)
    packed = jnp.bitwise_or(jnp.right_shift(bits[:, :half], jnp.uint32(16)),
                            jnp.bitwise_and(bits[:, half:], jnp.uint32(0xFFFF0000)))
    return lax.bitcast_convert_type(packed, I32)


def _unpack_bf16_pairs(p):
    bits = lax.bitcast_convert_type(p, jnp.uint32)
    lo = lax.bitcast_convert_type(jnp.left_shift(bits, jnp.uint32(16)), F32)
    hi = lax.bitcast_convert_type(jnp.bitwise_and(bits, jnp.uint32(0xFFFF0000)), F32)
    return jnp.concatenate([lo, hi], axis=1)


def _mod_kernel(c_ref, w_ref, b_ref, o_ref):
    c = c_ref[...]
    o_ref[...] = _mm(c * _sigmoid(c), w_ref[...], HIGHEST) + b_ref[...]


def _mod_call(c8, w_mod, b_mod):
    d, n = w_mod.shape
    tn = 1536
    return pl.pallas_call(
        _mod_kernel,
        grid=(n // tn,),
        in_specs=[pl.BlockSpec((8, d), lambda j: (0, 0)),
                  pl.BlockSpec((d, tn), lambda j: (0, j)),
                  pl.BlockSpec((1, tn), lambda j: (0, j))],
        out_specs=pl.BlockSpec((8, tn), lambda j: (0, j)),
        out_shape=jax.ShapeDtypeStruct((8, n), F32),
        compiler_params=_cparams(("parallel",), 32),
        name="mod",
    )(c8, w_mod, b_mod.reshape(1, n))


def _wsplit_kernel(w_ref, *o_refs, bounds):
    for o_ref, lo, hi in zip(o_refs, bounds[:-1], bounds[1:]):
        o_ref[:, :hi - lo] = w_ref[:, lo:hi].astype(o_ref.dtype)
        if o_ref.shape[1] > hi - lo:
            o_ref[:, hi - lo:] = jnp.zeros((o_ref.shape[0], o_ref.shape[1] - (hi - lo)), o_ref.dtype)


def _wsplit_call(w, bounds):
    d = w.shape[0]
    tr = 256
    widths = [-(-(hi - lo) // LANES) * LANES for lo, hi in zip(bounds[:-1], bounds[1:])]
    return pl.pallas_call(
        functools.partial(_wsplit_kernel, bounds=bounds),
        grid=(d // tr,),
        in_specs=[pl.BlockSpec((tr, w.shape[1]), lambda i: (i, 0))],
        out_specs=[pl.BlockSpec((tr, wd), lambda i: (i, 0)) for wd in widths],
        out_shape=[jax.ShapeDtypeStruct((d, wd), BF16) for wd in widths],
        compiler_params=_cparams(("parallel",), 32),
        name="wsplit",
    )(w)


def _inconv_kernel(*refs, grid_w, use_rows, tm, cw, n_extra):
    refs = list(refs)
    prev_ref = refs.pop(0) if use_rows else None
    x_ref = refs.pop(0)
    next_ref = refs.pop(0) if use_rows else None
    sh_ref, sc_ref, g_ref, wq_ref, wgate_ref = refs[:5]
    wx_refs = refs[5:5 + n_extra]
    w_ref, par_ref, q_ref, k_ref, v_ref, go_ref, gr_ref = refs[5 + n_extra:12 + n_extra]
    ox_refs = refs[12 + n_extra:]

    def modulated(x):
        return (_rmsnorm(x, g_ref[...]) * (1.0 + sc_ref[0]) + sh_ref[0]).astype(BF16)

    r = pl.program_id(1)
    nr = pl.num_programs(1)
    u = modulated(x_ref[0])

    def project(wx_ref, o_ref, c0, step):
        o_ref[0, :, c0:c0 + step] = _mm(u, wx_ref[:, c0:c0 + step]).astype(o_ref.dtype)

    plain = [functools.partial(project, wx_ref, o_ref, c0, min(wx_ref.shape[1], 512))
             for wx_ref, o_ref in zip(wx_refs, ox_refs) for c0 in range(0, wx_ref.shape[1], min(wx_ref.shape[1], 512))]
    n_conv_chunks = QKV_DIM // cw

    t = lax.broadcasted_iota(I32, (tm, 1), 0)
    col = jnp.bitwise_and(t, grid_w - 1)
    m_left = (col != 0).astype(F32)
    m_right = (col != grid_w - 1).astype(F32)
    if use_rows:
        u_prev = modulated(prev_ref[0])
        u_next = modulated(next_ref[0])
        has_prev = (r > 0).astype(F32)
        has_next = (r < nr - 1).astype(F32)
    for c0 in range(0, QKV_DIM, cw):
        wq = wq_ref[:, c0:c0 + cw]
        xm = _mm(u, wq)
        if use_rows:
            up = jnp.concatenate([_mm(u_prev, wq) * has_prev, xm[:tm - grid_w]], axis=0)
            dn = jnp.concatenate([xm[grid_w:], _mm(u_next, wq) * has_next], axis=0)

        def colsum(kc):
            y = xm * w_ref[3 + kc:4 + kc, c0:c0 + cw]
            if use_rows:
                y = y + up * w_ref[kc:kc + 1, c0:c0 + cw] + dn * w_ref[6 + kc:7 + kc, c0:c0 + cw]
            return y

        acc = (colsum(1) + pltpu.roll(colsum(0), 1, axis=0) * m_left
               + pltpu.roll(colsum(2), tm - 1, axis=0) * m_right)
        s = acc * _sigmoid(acc)
        for h0 in range(0, cw, HEAD_DIM):
            c = c0 + h0
            seg = s[:, h0:h0 + HEAD_DIM]
            if c < 2 * QK_DIM:
                seg = seg * lax.rsqrt(jnp.sum(seg * seg, axis=-1, keepdims=True) + EPS)
            if c < QK_DIM:
                q_ref[0, :, c:c + HEAD_DIM] = (seg * HEAD_DIM ** -0.5).astype(q_ref.dtype)
            elif c < 2 * QK_DIM:
                k_ref[0, :, c - QK_DIM:c - QK_DIM + HEAD_DIM] = seg.astype(k_ref.dtype)
            else:
                v_ref[0, :, c - 2 * QK_DIM:c - 2 * QK_DIM + HEAD_DIM] = seg.astype(v_ref.dtype)
        ci = c0 // cw
        for job in plain[ci * len(plain) // n_conv_chunks:(ci + 1) * len(plain) // n_conv_chunks]:
            job()
    g = _mm(u, wgate_ref[...])
    a = g + par_ref[1:2, :]
    softplus = jnp.maximum(a, 0.0) + jnp.log1p(jnp.exp(-jnp.abs(a)))
    log_g = -jnp.exp(par_ref[0:1, :]) * softplus
    lane = lax.broadcasted_iota(I32, g.shape, 1)
    go_ref[0] = jnp.where(lane < GATE_LANE0, _sigmoid(g), log_g)
    gr_ref[0] = jnp.transpose(log_g)[GATE_LANE0:GATE_LANE0 + 2 * NV_HEADS]


def _inconv_call(x3, sh, sc, g, w_qkv, w_gates, w_extra, extra_dtypes, conv_w, par, grid_w, use_rows, tm):
    b, t, d = x3.shape
    kern = functools.partial(_inconv_kernel, grid_w=grid_w, use_rows=use_rows, tm=tm, cw=512, n_extra=len(w_extra))
    per = tm // grid_w
    nrow = t // grid_w
    tile = lambda wd: pl.BlockSpec((1, tm, wd), lambda i, r: (i, r, 0))
    vec = pl.BlockSpec((1, 1, d), lambda i, r: (i, 0, 0))
    const = lambda a: pl.BlockSpec(a.shape, lambda i, r: (0,) * a.ndim, pipeline_mode=pl.Buffered(1))
    in_specs, args = [], []
    if use_rows:
        in_specs.append(pl.BlockSpec((1, grid_w, d), lambda i, r: (i, jnp.maximum(r * per - 1, 0), 0)))
        args.append(x3)
    in_specs.append(tile(d))
    args.append(x3)
    if use_rows:
        in_specs.append(pl.BlockSpec((1, grid_w, d), lambda i, r: (i, jnp.minimum((r + 1) * per, nrow - 1), 0)))
        args.append(x3)
    consts = [g, w_qkv, w_gates, *w_extra, conv_w, par]
    in_specs += [vec, vec] + [const(a) for a in consts]
    args += [sh, sc] + consts
    widths = [QK_DIM, QK_DIM, V_DIM, LANES] + [w.shape[1] for w in w_extra]
    dtypes = [BF16, BF16, BF16, F32] + list(extra_dtypes)
    out_shape = [jax.ShapeDtypeStruct((b, t, wd), dt) for wd, dt in zip(widths, dtypes)]
    out_specs = [tile(wd) for wd in widths]
    out_shape.insert(4, jax.ShapeDtypeStruct((b, 2 * NV_HEADS, t), F32))
    out_specs.insert(4, pl.BlockSpec((1, 2 * NV_HEADS, tm), lambda i, r: (i, 0, r)))
    return pl.pallas_call(
        kern, grid=(b, t // tm), in_specs=in_specs, out_specs=out_specs, out_shape=out_shape,
        compiler_params=_cparams(("parallel", "parallel"), 56),
        name="inconv_rows" if use_rows else "inconv_seq",
    )(*args)


def _gdn_kernel(qf, kf, vf, gf, rf, qb, kb, vb, gb, rb, s0_ref, of, ob, sfin_ref, s_ref, *, prec):
    i = pl.program_id(1)
    nc = pl.num_programs(1)

    @pl.when(i == 0)
    def _():
        s_ref[...] = s0_ref[0]

    c = qf.shape[1]
    per = NV_HEADS // NQK_HEADS
    row = lax.broadcasted_iota(I32, (c, c), 0)
    colj = lax.broadcasted_iota(I32, (c, c), 1)
    eye = jnp.where(row == colj, 1.0, 0.0)
    nt_dims = (((1,), (1,)), ((), ()))
    tn_dims = (((0,), (0,)), ((), ()))

    seqs = []
    for d, (q_r, k_r, v_r, g_r, r_r, o_r) in enumerate(((qf, kf, vf, gf, rf, of), (qb, kb, vb, gb, rb, ob))):
        rev = d == 1
        incl = (colj >= row) if rev else (colj <= row)
        strict = (colj > row) if rev else (colj < row)
        gates = g_r[0]
        tri_c = jnp.where(incl, 1.0, 0.0)
        gcm = _mm(tri_c, gates, HIGHEST)
        gcr = _mm(r_r[0], tri_c, HIGHEST, dims=nt_dims)
        for hq in range(NQK_HEADS):
            q = q_r[0, :, hq * HEAD_DIM:(hq + 1) * HEAD_DIM]
            k = k_r[0, :, hq * HEAD_DIM:(hq + 1) * HEAD_DIM]
            kq = lax.dot_general(jnp.concatenate([q, k], axis=0), k, nt_dims, preferred_element_type=F32)
            for j in range(per):
                h = hq * per + j
                idx = d * NV_HEADS + h
                gc_c = gcm[:, GATE_LANE0 + idx:GATE_LANE0 + idx + 1]
                seqs.append(dict(d=d, h=h, o_r=o_r, v_r=v_r, q=q, k=k, qk=kq[:c], kk=kq[c:], incl=incl, strict=strict,
                                 beta=gates[:, idx:idx + 1], gc_c=gc_c, gc_r=gcr[idx:idx + 1, :],
                                 ge=gc_c[0:1] if rev else gc_c[c - 1:c]))

    def same_block(m):
        sh = int(math.log2(m))
        return jnp.right_shift(row, sh) == jnp.right_shift(colj, sh)

    for s in seqs:
        s['decay'] = jnp.where(s['incl'], jnp.exp(jnp.where(s['incl'], s['gc_c'] - s['gc_r'], 0.0)), 0.0)
        s['a'] = jnp.where(s['strict'], s['beta'] * s['kk'] * s['decay'], 0.0)
        s['t'] = eye - jnp.where(same_block(2), s['a'], 0.0)
    m = 4
    while m <= c:
        between = jnp.logical_and(same_block(m), jnp.logical_not(same_block(m // 2)))
        for s in seqs:
            s['te'] = _mm(s['t'], jnp.where(between, s['a'], 0.0), prec)
        for s in seqs:
            s['t'] = s['t'] - _mm(s['te'], s['t'], prec)
        m *= 2
    for s in seqs:
        h = s['h']
        egc = jnp.exp(s['gc_c'])
        kf32 = s['k'].astype(F32)
        v = s['v_r'][0, :, h * HEAD_DIM:(h + 1) * HEAD_DIM].astype(F32)
        rhs = jnp.concatenate([s['beta'] * v, (s['beta'] * egc) * kf32], axis=1)
        s['sol'] = _mm(s['t'], rhs, prec)
        s['q_dec'] = s['q'].astype(F32) * egc
        s['k_dec'] = kf32 * jnp.exp(s['ge'] - s['gc_c'])
    for s in seqs:
        s['ws'] = _mm(jnp.concatenate([s['sol'][:, HEAD_DIM:], s['q_dec']], axis=0), s_ref[s['d'], s['h']])
    for s in seqs:
        s['u'] = s['sol'][:, :HEAD_DIM] - s['ws'][:c]
        s_ref[s['d'], s['h']] = (jnp.exp(s['ge']) * s_ref[s['d'], s['h']]
                                 + _mm(s['k_dec'], s['u'], dims=tn_dims))
    for s in seqs:
        h = s['h']
        o = s['ws'][c:] + _mm(s['qk'] * s['decay'], s['u'])
        s['o_r'][0, :, h * HEAD_DIM:(h + 1) * HEAD_DIM] = o.astype(s['o_r'].dtype)

    @pl.when(i == nc - 1)
    def _():
        sfin_ref[0] = s_ref[...]


def _gdn_call(q, k, v, go, rows, s0, prec):
    b, t, _ = q.shape
    c = GDN_CHUNK
    nc = t // c
    fwd = lambda i, n: (i, n, 0)
    bwd = lambda i, n: (i, nc - 1 - n, 0)
    rfwd = lambda i, n: (i, 0, n)
    rbwd = lambda i, n: (i, 0, nc - 1 - n)
    state_spec = pl.BlockSpec((1,) + s0.shape[1:], lambda i, n: (i, 0, 0, 0, 0))

    def specs(m3, mr):
        return [pl.BlockSpec((1, c, QK_DIM), m3), pl.BlockSpec((1, c, QK_DIM), m3),
                pl.BlockSpec((1, c, V_DIM), m3), pl.BlockSpec((1, c, LANES), m3),
                pl.BlockSpec((1, 2 * NV_HEADS, c), mr)]

    return pl.pallas_call(
        functools.partial(_gdn_kernel, prec=prec),
        grid=(b, nc),
        in_specs=specs(fwd, rfwd) + specs(bwd, rbwd) + [state_spec],
        out_specs=[pl.BlockSpec((1, c, V_DIM), fwd), pl.BlockSpec((1, c, V_DIM), bwd), state_spec],
        out_shape=[jax.ShapeDtypeStruct((b, t, V_DIM), BF16), jax.ShapeDtypeStruct((b, t, V_DIM), BF16),
                   jax.ShapeDtypeStruct(s0.shape, F32)],
        scratch_shapes=[pltpu.VMEM(s0.shape[1:], F32)],
        compiler_params=_cparams(("parallel", "arbitrary"), 48),
        name="gdn",
    )(q, k, v, go, rows, q, k, v, go, rows, s0)


def _fnet1_kernel(x_ref, f_ref, ar_ref, ai_ref):
    n = x_ref.shape[1]
    for j in range(x_ref.shape[2]):
        a = _mm(f_ref[...], x_ref[0, :, j, :])
        ar_ref[0, j] = a[:n]
        ai_ref[0, j] = a[n:]


def _fnet2_kernel(ar_ref, ai_ref, g_ref, wc_ref, o_ref, *, scale):
    n = ar_ref.shape[1]
    cols = ar_ref.shape[2]
    zs = []
    for m in range(cols):
        a2 = jnp.concatenate([ar_ref[0, :, m, :], ai_ref[0, :, m, :]], axis=0)
        zs.append(_mm(g_ref[m], a2))
    zr = jnp.concatenate([z[:n] for z in zs], axis=0)
    zi = jnp.concatenate([z[n:] for z in zs], axis=0)
    for g0 in range(0, zr.shape[1], HEAD_DIM):
        y = _mm(jnp.concatenate([zr[:, g0:g0 + HEAD_DIM], zi[:, g0:g0 + HEAD_DIM]], axis=1), wc_ref[...])
        for m in range(cols):
            o_ref[0, :, m, g0:g0 + HEAD_DIM] = y[m * n:(m + 1) * n] * scale


def _fnet_tables(n, groups):
    a = np.arange(n)
    ang1 = 2.0 * np.pi * np.outer(a, a) / n
    f1 = np.concatenate([np.cos(ang1), -np.sin(ang1)], axis=0)
    m = a[:, None] + n * a[None, :]
    ang2 = 2.0 * np.pi * ((m[:, :, None] * a[None, None, :]) % (n * n)) / (n * n)
    gc, gs = np.cos(ang2), np.sin(ang2)
    g2 = np.concatenate([np.concatenate([gc, gs], axis=2), np.concatenate([-gs, gc], axis=2)], axis=1)
    angc = 2.0 * np.pi * np.outer(np.arange(HEAD_DIM), np.arange(HEAD_DIM)) / HEAD_DIM
    wc = np.concatenate([np.cos(angc), np.sin(angc)], axis=0)
    f = lambda x: jnp.asarray(x, F32).astype(BF16)
    return f(f1), f(g2), f(wc)


def _fnet_call(f):
    b, l, c = f.shape
    n = GRID_W
    assert l == n * n
    f1, g2, wc = _fnet_tables(n, c // HEAD_DIM)
    cols = 2 * SUBLANES
    ar, ai = pl.pallas_call(
        _fnet1_kernel,
        grid=(b, n // cols),
        in_specs=[pl.BlockSpec((1, n, cols, c), lambda i, j: (i, 0, j, 0)),
                  pl.BlockSpec((2 * n, n), lambda i, j: (0, 0))],
        out_specs=[pl.BlockSpec((1, cols, n, c), lambda i, j: (i, j, 0, 0))] * 2,
        out_shape=[jax.ShapeDtypeStruct((b, n, n, c), F32)] * 2,
        compiler_params=_cparams(("parallel", "parallel"), 32),
        name="fnet1",
    )(f.reshape(b, n, n, c), f1)
    out = pl.pallas_call(
        functools.partial(_fnet2_kernel, scale=1.0 / math.sqrt(l * HEAD_DIM)),
        grid=(b, n // cols),
        in_specs=[pl.BlockSpec((1, n, cols, c), lambda i, j: (i, 0, j, 0)),
                  pl.BlockSpec((1, n, cols, c), lambda i, j: (i, 0, j, 0)),
                  pl.BlockSpec((cols, 2 * n, 2 * n), lambda i, j: (j, 0, 0)),
                  pl.BlockSpec((2 * HEAD_DIM, HEAD_DIM), lambda i, j: (0, 0))],
        out_specs=pl.BlockSpec((1, n, cols, c), lambda i, j: (i, 0, j, 0)),
        out_shape=jax.ShapeDtypeStruct((b, n, n, c), F32),
        compiler_params=_cparams(("parallel", "parallel"), 32),
        name="fnet2",
    )(ar, ai, g2, wc)
    return out.reshape(b, l, c)


def _merge_kernel(of_ref, ob_ref, z_ref, fm_ref, gab_ref, x_ref, g1_ref, sh2_ref, sc2_ref, gn_ref, n2_ref,
                  wg_ref, wf_ref, wm_ref, wr_ref, br_ref, x1_ref, h2_ref, lg_ref):
    d = x_ref.shape[1]
    tm = x_ref.shape[0]
    halves = [slice(0, tm // 2), slice(tm // 2, tm)]
    yb_in = []
    for r in halves:
        o = of_ref[r, :].astype(F32) + ob_ref[r, :].astype(F32)
        z = z_ref[r, :].astype(F32)
        parts = []
        for h0 in range(0, V_DIM, HEAD_DIM):
            oh = o[:, h0:h0 + HEAD_DIM]
            parts.append(oh * lax.rsqrt(jnp.mean(oh * oh, axis=-1, keepdims=True) + EPS) * gn_ref[...])
        yb_in.append(jnp.concatenate(parts, axis=1) * (z * _sigmoid(z)))
    yb = [_mm(v, wg_ref[...]) for v in yb_in]
    ya = [_mm(fm_ref[r, :], wf_ref[...]) for r in halves]
    mixed = [_sigmoid(gab_ref[r, :d].astype(F32)) * a + _sigmoid(gab_ref[r, d:].astype(F32)) * b_
             for r, a, b_ in zip(halves, ya, yb)]
    mm = [_mm(v, wm_ref[...]) for v in mixed]
    w = wr_ref[...]
    w_hi = w.astype(BF16)
    w2 = jnp.concatenate([w_hi, (w - w_hi.astype(F32)).astype(BF16)], axis=1)
    for r, v in zip(halves, mm):
        x1 = x_ref[r, :] + g1_ref[0] * v
        x1_ref[r, :] = x1
        h2 = _rmsnorm(x1, n2_ref[...]) * (1.0 + sc2_ref[0]) + sh2_ref[0]
        h2_ref[r, :] = _pack_bf16_pairs(h2)
        h_hi = h2.astype(BF16)
        h_lo = (h2 - h_hi.astype(F32)).astype(BF16)
        part = _mm(h_hi, w2)
        lg_ref[r, :] = (part[:, :LANES] + part[:, LANES:]) + _mm(h_lo, w_hi) + br_ref[...]


def _merge_call(of, ob, z, fm, gab, x2, g1, sh2, sc2, gn, n2, wg, wf, wm, wr, br, tokens_per_batch, tm):
    n, d = x2.shape
    per = tokens_per_batch // tm
    tok = lambda wd: pl.BlockSpec((tm, wd), lambda i: (i, 0))
    vec = pl.BlockSpec((1, 1, d), lambda i: (i // per, 0, 0))
    full = lambda a: pl.BlockSpec(a.shape, lambda i: (0,) * a.ndim)
    return pl.pallas_call(
        _merge_kernel,
        grid=(n // tm,),
        in_specs=[tok(V_DIM), tok(V_DIM), tok(V_DIM), tok(F_DIM), tok(2 * d), tok(d), vec, vec, vec,
                  full(gn), full(n2), full(wg), full(wf), full(wm), full(wr), full(br)],
        out_specs=[tok(d), tok(d // 2), tok(LANES)],
        out_shape=[jax.ShapeDtypeStruct((n, d), F32), jax.ShapeDtypeStruct((n, d // 2), I32),
                   jax.ShapeDtypeStruct((n, LANES), F32)],
        compiler_params=_cparams(("parallel",), 56),
        name="merge",
    )(of, ob, z, fm, gab, x2, g1, sh2, sc2, gn, n2, wg, wf, wm, wr, br)


def _route_kernel(lg_ref, idx_ref, w_ref, rank_ref, cnt_ref, run_ref):
    i = pl.program_id(0)

    @pl.when(i == 0)
    def _():
        run_ref[...] = jnp.zeros_like(run_ref)

    l = lg_ref[...]
    tm = l.shape[0]
    lane = lax.broadcasted_iota(I32, l.shape, 1)
    vals, idxs = [], []
    for _ in range(TOP_K):
        m = jnp.max(l, axis=-1, keepdims=True)
        idx = jnp.min(jnp.where(l == m, lane, LANES), axis=-1, keepdims=True)
        vals.append(m)
        idxs.append(idx)
        l = jnp.where(lane == idx, NEG_BIG * 2.0, l)
    es = [jnp.exp(v - vals[0]) for v in vals]
    inv = 1.0 / (es[0] + es[1] + es[2] + es[3])
    picked = jnp.zeros(l.shape, F32)
    for idx in idxs:
        picked = picked + (lane == idx).astype(F32)
    r = lax.broadcasted_iota(I32, (tm, tm), 0)
    cidx = lax.broadcasted_iota(I32, (tm, tm), 1)
    before = _mm(jnp.where(cidx < r, 1.0, 0.0), picked) + run_ref[...]
    idx_out = jnp.zeros(l.shape, I32)
    w_out = jnp.zeros(l.shape, F32)
    rank_out = jnp.zeros(l.shape, F32)
    for k in range(TOP_K):
        rk = jnp.sum(jnp.where(lane == idxs[k], before, 0.0), axis=-1, keepdims=True)
        idx_out = jnp.where(lane == k, idxs[k], idx_out)
        w_out = jnp.where(lane == k, es[k] * inv, w_out)
        rank_out = jnp.where(lane == k, rk, rank_out)
    idx_ref[...] = jnp.transpose(idx_out.astype(F32))[:SUBLANES].astype(I32)
    w_ref[...] = w_out
    rank_ref[...] = jnp.transpose(rank_out)[:SUBLANES].astype(I32)
    run_ref[...] = run_ref[...] + jnp.sum(picked, axis=0, keepdims=True)
    cnt_ref[...] = run_ref[...]


def _route_call(logits, tm):
    n = logits.shape[0]
    tok = pl.BlockSpec((tm, LANES), lambda i: (i, 0))
    tok_t = pl.BlockSpec((SUBLANES, tm), lambda i: (0, i))
    return pl.pallas_call(
        _route_kernel,
        grid=(n // tm,),
        in_specs=[tok],
        out_specs=[tok_t, tok, tok_t, pl.BlockSpec((1, LANES), lambda i: (0, 0))],
        out_shape=[jax.ShapeDtypeStruct((SUBLANES, n), I32), jax.ShapeDtypeStruct((n, LANES), F32),
                   jax.ShapeDtypeStruct((SUBLANES, n), I32), jax.ShapeDtypeStruct((1, LANES), F32)],
        scratch_shapes=[pltpu.VMEM((1, LANES), F32)],
        compiler_params=_cparams(("arbitrary",), 32),
        name="route",
    )(logits)


def _sc_mesh():
    return plsc.VectorSubcoreMesh(core_axis_name="c", subcore_axis_name="s",
                                  num_cores=SC_CORES, num_subcores=SC_SUBCORES)


def _sc_worker_id():
    return lax.axis_index("s") * SC_CORES + lax.axis_index("c")


def _sc_scatter_rows(x, idx, n_out):
    n, d = x.shape
    per_w = idx.shape[0] // SC_WORKERS
    nwin = per_w // SC_WINDOW
    assert per_w * SC_WORKERS == idx.shape[0] and nwin * SC_WINDOW == per_w and nwin % 2 == 0 and n % per_w == 0

    def body(x_hbm, idx_hbm, out_hbm, idx_v, rows_v, sem_r, sem_w):
        wid = _sc_worker_id()
        row0 = lax.rem(wid * per_w, n)
        pltpu.sync_copy(idx_hbm.at[wid], idx_v)

        def read(j, b):
            return pltpu.make_async_copy(x_hbm.at[pl.ds(row0 + j * SC_WINDOW, SC_WINDOW)], rows_v.at[b], sem_r.at[b])

        def write(j, b):
            return pltpu.make_async_copy(rows_v.at[b], out_hbm.at[idx_v.at[j]], sem_w.at[b])

        @pl.loop(0, nwin, step=2)
        def _(j):
            read(j, 0).start()
            read(j + 1, 1).start()
            read(j, 0).wait()
            write(j, 0).start()
            read(j + 1, 1).wait()
            write(j + 1, 1).start()
            write(j, 0).wait()
            write(j + 1, 1).wait()

    return pl.kernel(
        body, out_type=jax.ShapeDtypeStruct((n_out, d), x.dtype), mesh=_sc_mesh(),
        scratch_types=[pltpu.VMEM((nwin, SC_WINDOW), I32), pltpu.VMEM((2, SC_WINDOW, d), x.dtype),
                       pltpu.SemaphoreType.DMA((2,)), pltpu.SemaphoreType.DMA((2,))],
        name="sc_scatter_rows",
    )(x, idx.reshape(SC_WORKERS, nwin, SC_WINDOW))


def _sc_gather_rows(y, idx):
    d = y.shape[1]
    total = idx.shape[0]
    per_w = total // SC_WORKERS
    nwin = per_w // SC_WINDOW
    assert per_w * SC_WORKERS == total and nwin * SC_WINDOW == per_w and nwin % 2 == 0

    def body(y_hbm, idx_hbm, out_hbm, idx_v, rows_v, sem_r, sem_w):
        wid = _sc_worker_id()
        row0 = wid * per_w
        pltpu.sync_copy(idx_hbm.at[wid], idx_v)

        def read(j, b):
            return pltpu.make_async_copy(y_hbm.at[idx_v.at[j]], rows_v.at[b], sem_r.at[b])

        def write(j, b):
            return pltpu.make_async_copy(rows_v.at[b], out_hbm.at[pl.ds(row0 + j * SC_WINDOW, SC_WINDOW)], sem_w.at[b])

        @pl.loop(0, nwin, step=2)
        def _(j):
            read(j, 0).start()
            read(j + 1, 1).start()
            read(j, 0).wait()
            write(j, 0).start()
            read(j + 1, 1).wait()
            write(j + 1, 1).start()
            write(j, 0).wait()
            write(j + 1, 1).wait()

    return pl.kernel(
        body, out_type=jax.ShapeDtypeStruct((total, d), y.dtype), mesh=_sc_mesh(),
        scratch_types=[pltpu.VMEM((nwin, SC_WINDOW), I32), pltpu.VMEM((2, SC_WINDOW, d), y.dtype),
                       pltpu.SemaphoreType.DMA((2,)), pltpu.SemaphoreType.DMA((2,))],
        name="sc_gather_rows",
    )(y, idx.reshape(SC_WORKERS, nwin, SC_WINDOW))


def _expert_kernel(be_ref, nv_ref, nxt_ref, par_ref, used_ref, x_ref, wg_hbm, wu_hbm, wd_hbm, bg_ref, bu_ref, bd_ref,
                   y_ref, wgf, wuf, wdf, wgb, wub, wdb, sem):
    i = pl.program_id(0)

    def fetch(e, s):
        return [pltpu.make_async_copy(w_hbm.at[e], w_f.at[s], sem.at[s])
                for w_hbm, w_f in ((wg_hbm, wgf), (wu_hbm, wuf), (wd_hbm, wdf))]

    @pl.when(i < nv_ref[0])
    def _():
        e = be_ref[i]
        s = par_ref[e]

        @pl.when(i == 0)
        def _():
            for cp in fetch(e, s):
                cp.start()

        @pl.when(jnp.logical_or(i == 0, be_ref[jnp.maximum(i - 1, 0)] != e))
        def _():
            for cp in fetch(e, s):
                cp.wait()
            wgb[...] = wgf[s].astype(BF16)
            wub[...] = wuf[s].astype(BF16)
            wdb[...] = wdf[s].astype(BF16)

            @pl.when(nxt_ref[e] >= 0)
            def _():
                for cp in fetch(nxt_ref[e], 1 - s):
                    cp.start()

        def ffn(nrows):
            x = _unpack_bf16_pairs(x_ref[:nrows, :]).astype(BF16)
            gate = jnp.minimum(_mm(x, wgb[...]) + bg_ref[0], SWIGLU_LIMIT)
            up = jnp.clip(_mm(x, wub[...]) + bu_ref[0], -SWIGLU_LIMIT, SWIGLU_LIMIT)
            act = (up + 1.0) * gate * _sigmoid(SWIGLU_ALPHA * gate)
            y_ref[:nrows, :] = _pack_bf16_pairs(_mm(act, wdb[...]) + bd_ref[0])

        half = x_ref.shape[0] // 2

        @pl.when(used_ref[i] > half)
        def _():
            ffn(x_ref.shape[0])

        @pl.when(used_ref[i] <= half)
        def _():
            ffn(half)


def _expert_call(block_e, n_valid, next_e, parity, used, xs, w_gate, w_up, w_down, b_gate, b_up, b_down):
    nb = block_e.shape[0]
    tmb = EXPERT_BLOCK
    ne, d, de = w_gate.shape
    bspec = lambda s: pl.BlockSpec((1,) + s, lambda i, be, nv, nx, pa, us: (be[i], 0, 0))
    anyspec = pl.BlockSpec(memory_space=pl.ANY)
    grid_spec = pltpu.PrefetchScalarGridSpec(
        num_scalar_prefetch=5,
        grid=(nb,),
        in_specs=[pl.BlockSpec((tmb, d // 2), lambda i, be, nv, nx, pa, us: (i, 0)),
                  anyspec, anyspec, anyspec, bspec((1, de)), bspec((1, de)), bspec((1, d))],
        out_specs=pl.BlockSpec((tmb, d // 2), lambda i, be, nv, nx, pa, us: (i, 0)),
        scratch_shapes=[pltpu.VMEM((2, d, de), F32), pltpu.VMEM((2, d, de), F32), pltpu.VMEM((2, de, d), F32),
                        pltpu.VMEM((d, de), BF16), pltpu.VMEM((d, de), BF16), pltpu.VMEM((de, d), BF16),
                        pltpu.SemaphoreType.DMA((2,))],
    )
    return pl.pallas_call(
        _expert_kernel,
        grid_spec=grid_spec,
        out_shape=jax.ShapeDtypeStruct(xs.shape, xs.dtype),
        compiler_params=_cparams(("arbitrary",), 56),
        name="expert",
    )(block_e, n_valid, next_e, parity, used, xs, w_gate, w_up, w_down,
      b_gate.reshape(ne, 1, de), b_up.reshape(ne, 1, de), b_down.reshape(ne, 1, d))


def _combine_kernel(y0, y1, y2, y3, w_ref, x1_ref, g2_ref, fg_ref, o_ref):
    w = w_ref[...]
    ys = [_unpack_bf16_pairs(y[...]) for y in (y0, y1, y2, y3)]
    moe = (w[:, 0:1] * ys[0] + w[:, 1:2] * ys[1]) + (w[:, 2:3] * ys[2] + w[:, 3:4] * ys[3])
    o_ref[...] = _rmsnorm(x1_ref[...] + g2_ref[0] * moe, fg_ref[...])


def _combine_call(y4, top_w, x1, g2, fg, tokens_per_batch, tm):
    n, d = x1.shape
    per = tokens_per_batch // tm
    nt = n // tm
    yspec = lambda k: pl.BlockSpec((tm, d // 2), lambda i: (k * nt + i, 0))
    return pl.pallas_call(
        _combine_kernel,
        grid=(nt,),
        in_specs=[yspec(0), yspec(1), yspec(2), yspec(3),
                  pl.BlockSpec((tm, LANES), lambda i: (i, 0)),
                  pl.BlockSpec((tm, d), lambda i: (i, 0)),
                  pl.BlockSpec((1, 1, d), lambda i: (i // per, 0, 0)),
                  pl.BlockSpec((1, d), lambda i: (0, 0))],
        out_specs=pl.BlockSpec((tm, d), lambda i: (i, 0)),
        out_shape=jax.ShapeDtypeStruct((n, d), F32),
        compiler_params=_cparams(("parallel",), 48),
        name="combine",
    )(y4, y4, y4, y4, top_w, x1, g2, fg)


def _slot_kernel(start_ref, idx_ref, rank_ref, o_ref):
    idx = idx_ref[...]
    acc = rank_ref[...]
    for e in range(N_EXPERTS):
        acc = acc + jnp.where(idx == e, start_ref[e], 0)
    o_ref[...] = acc


def _slot_call(pad_start, top_idx, rank):
    full = pl.BlockSpec(top_idx.shape, lambda i, s: (0, 0))
    return pl.pallas_call(
        _slot_kernel,
        grid_spec=pltpu.PrefetchScalarGridSpec(num_scalar_prefetch=1, grid=(1,), in_specs=[full, full],
                                               out_specs=full),
        out_shape=jax.ShapeDtypeStruct(top_idx.shape, I32),
        compiler_params=_cparams(("arbitrary",), 32),
        name="slots",
    )(pad_start, top_idx, rank)


def _routing_tables(top_idx, rank, counts, n):
    tmb = EXPERT_BLOCK
    counts = counts.astype(I32)
    padded = (counts + tmb - 1) // tmb * tmb
    pad_end = jnp.cumsum(padded)
    pad_start = pad_end - padded
    n_blocks = -(-(n * TOP_K + N_EXPERTS * (tmb - 1)) // tmb)
    n_slots = n_blocks * tmb
    dest_flat = _slot_call(pad_start, top_idx, rank)[:TOP_K].reshape(-1)
    block_start = jnp.arange(n_blocks, dtype=I32) * tmb
    block_e = jnp.minimum(jnp.sum((pad_end[None, :] <= block_start[:, None]).astype(I32), axis=1), N_EXPERTS - 1)
    n_valid = (pad_end[-1:] // tmb).astype(I32)
    experts = jnp.arange(N_EXPERTS, dtype=I32)
    used = counts > 0
    later = jnp.where(jnp.logical_and(used[None, :], experts[None, :] > experts[:, None]), experts[None, :], N_EXPERTS)
    next_e = jnp.min(later, axis=1)
    next_e = jnp.where(next_e == N_EXPERTS, -1, next_e).astype(I32)
    parity = ((jnp.cumsum(used.astype(I32)) - used.astype(I32)) % 2).astype(I32)
    row_end = jnp.sum(jnp.where(block_e[:, None] == experts[None, :], (pad_start + counts)[None, :], 0), axis=1)
    rows_used = jnp.clip(row_end - block_start, 0, tmb).astype(I32)
    return block_e, n_valid, next_e, parity, rows_used, dest_flat, n_slots


def _gdn_branch(x3, sh, sc, norm_g, w_qkv, w_gates, w_extra, extra_dtypes, conv_w, par, grid_w, use_rows, tm,
                s0, prec):
    q, k, v, go, rows, *extra = _inconv_call(x3, sh, sc, norm_g, w_qkv, w_gates, w_extra, extra_dtypes, conv_w, par,
                                             grid_w, use_rows, tm)
    o_f, o_b, s_fin = _gdn_call(q, k, v, go, rows, s0, prec)
    return extra, o_f, o_b, s_fin


def kernel(x, c, ctx, c_ctx, w_mod, b_mod, norm1_g, norm2_g, w_in, conv_w, a_log, dt_bias, gdn_norm_g,
           w_fourier_out, w_gdn_out, w_merge_out, w_router, b_router, w_gate, b_gate, w_up, b_up,
           w_down, b_down, final_norm_g):
    b, l, d = x.shape
    n = b * l
    n_ctx = ctx.shape[1]
    assert w_mod.shape[0] == 1 and l == GRID_W * GRID_W and d == V_DIM
    prec = None

    c8 = jnp.concatenate([c, c_ctx[None, :], jnp.zeros((8 - b - 1, d), F32)], axis=0)
    mod = _mod_call(c8, w_mod[0], b_mod[0])
    sh1, sc1, g1, sh2, sc2, g2 = [mod[:b, j * d:(j + 1) * d].reshape(b, 1, d) for j in range(6)]
    csh1 = jnp.broadcast_to(mod[b:b + 1, 0:d].reshape(1, 1, d), (b, 1, d))
    csc1 = jnp.broadcast_to(mod[b:b + 1, d:2 * d].reshape(1, 1, d), (b, 1, d))

    wi = w_in[0]
    off_gate = QKV_DIM
    off_z = off_gate + 4 * NV_HEADS
    off_f = off_z + V_DIM
    off_ga = off_f + F_DIM
    w_qkv, w_gates, *w_extra = _wsplit_call(wi, (0, off_gate, off_z, off_f, off_ga, wi.shape[1]))
    par = jnp.pad(jnp.stack([a_log[0].reshape(-1), dt_bias[0].reshape(-1)]),
                  ((0, 6), (GATE_LANE0, LANES - 2 * GATE_LANE0)))
    n1 = norm1_g[0].reshape(1, d)
    cw = conv_w[0].reshape(9, QKV_DIM)

    zero_state = jnp.zeros((b, 2, NV_HEADS, HEAD_DIM, HEAD_DIM), F32)
    _, _, _, s_ctx = _gdn_branch(ctx, csh1, csc1, n1, w_qkv, w_gates, [], [], cw, par, n_ctx, False, n_ctx,
                                 zero_state, prec)

    x2 = x.reshape(n, d)
    (z, f, gab), o_f, o_b, _ = _gdn_branch(x, sh1, sc1, n1, w_qkv, w_gates, w_extra, (BF16, F32, BF16),
                                           cw, par, GRID_W, True, TOKEN_TILE, s_ctx, prec)
    z, gab = z.reshape(n, V_DIM), gab.reshape(n, 2 * d)
    fmix = _fnet_call(f).reshape(n, F_DIM)

    wr = jnp.pad(w_router[0], ((0, 0), (0, LANES - N_EXPERTS)))
    br = jnp.pad(b_router[0], (0, LANES - N_EXPERTS), constant_values=NEG_BIG).reshape(1, LANES)
    x1, h2, logits = _merge_call(
        o_f.reshape(n, V_DIM), o_b.reshape(n, V_DIM), z, fmix, gab, x2, g1, sh2, sc2,
        gdn_norm_g[0].reshape(1, HEAD_DIM), norm2_g[0].reshape(1, d),
        w_gdn_out[0].astype(BF16), w_fourier_out[0].astype(BF16), w_merge_out[0].astype(BF16), wr, br, l, TOKEN_TILE)

    top_idx, top_w, rank, counts = _route_call(logits, TOKEN_TILE)
    block_e, n_valid, next_e, parity, rows_used, dest_flat, n_slots = _routing_tables(
        top_idx, rank, counts[0, :N_EXPERTS], n)
    xs = _sc_scatter_rows(h2, dest_flat, n_slots)
    ys = _expert_call(block_e, n_valid, next_e, parity, rows_used, xs, w_gate[0], w_up[0], w_down[0],
                      b_gate[0], b_up[0], b_down[0])
    y4 = _sc_gather_rows(ys, dest_flat)
    out = _combine_call(y4, top_w, x1, g2, final_norm_g.reshape(1, d), l, TOKEN_TILE)
    return out.reshape(b, l, d)
```

```python
import functools
import math

import jax
import jax.numpy as jnp
import numpy as np
from jax import lax
from jax.experimental import pallas as pl
from jax.experimental.pallas import tpu as pltpu
from jax.experimental.pallas import tpu_sc as plsc

F32 = jnp.float32
BF16 = jnp.bfloat16
I32 = jnp.int32
HIGHEST = lax.Precision.HIGHEST

GRID_W = 64
NQK_HEADS = 4
NV_HEADS = 8
HEAD_DIM = 128
QK_DIM = NQK_HEADS * HEAD_DIM
V_DIM = NV_HEADS * HEAD_DIM
QKV_DIM = 2 * QK_DIM + V_DIM
F_GROUPS = 4
F_DIM = F_GROUPS * HEAD_DIM
N_EXPERTS = 32
TOP_K = 4
SWIGLU_ALPHA = 1.702
SWIGLU_LIMIT = 7.0
EPS = 1e-6

LANES = 128
SUBLANES = 8
GATE_LANE0 = 16
GDN_CHUNK = 128
EXPERT_BLOCK = 512
TOKEN_TILE = 512
SC_CORES = 2
SC_SUBCORES = 16
SC_WORKERS = SC_CORES * SC_SUBCORES
SC_WINDOW = 64
NEG_BIG = -1e30
MIB = 2 ** 20


def _cparams(sem, vmem_mib):
    return pltpu.CompilerParams(dimension_semantics=sem, vmem_limit_bytes=vmem_mib * MIB)


def _mm(a, b, prec=None, dims=(((1,), (0,)), ((), ()))):
    if prec is None:
        return lax.dot_general(a.astype(BF16), b.astype(BF16), dims, preferred_element_type=F32)
    return lax.dot_general(a.astype(F32), b.astype(F32), dims, precision=prec, preferred_element_type=F32)


def _sigmoid(x):
    return 1.0 / (1.0 + jnp.exp(-x))


def _rmsnorm(x, g):
    return x * lax.rsqrt(jnp.mean(x * x, axis=-1, keepdims=True) + EPS) * g


def _pack_bf16_pairs(x):
    half = x.shape[1] // 2
    bits = lax.bitcast_convert_type(x.astype(BF16).astype(F32), jnp.uint32)
    packed = jnp.bitwise_or(jnp.right_shift(bits[:, :half], jnp.uint32(16)),
                            jnp.bitwise_and(bits[:, half:], jnp.uint32(0xFFFF0000)))
    return lax.bitcast_convert_type(packed, I32)


def _unpack_bf16_pairs(p):
    bits = lax.bitcast_convert_type(p, jnp.uint32)
    lo = lax.bitcast_convert_type(jnp.left_shift(bits, jnp.uint32(16)), F32)
    hi = lax.bitcast_convert_type(jnp.bitwise_and(bits, jnp.uint32(0xFFFF0000)), F32)
    return jnp.concatenate([lo, hi], axis=1)


def _mod_kernel(c_ref, w_ref, b_ref, o_ref):
    c = c_ref[...]
    o_ref[...] = _mm(c * _sigmoid(c), w_ref[...], HIGHEST) + b_ref[...]


def _mod_call(c8, w_mod, b_mod):
    d, n = w_mod.shape
    tn = 1536
    return pl.pallas_call(
        _mod_kernel,
        grid=(n // tn,),
        in_specs=[pl.BlockSpec((8, d), lambda j: (0, 0)),
                  pl.BlockSpec((d, tn), lambda j: (0, j)),
                  pl.BlockSpec((1, tn), lambda j: (0, j))],
        out_specs=pl.BlockSpec((8, tn), lambda j: (0, j)),
        out_shape=jax.ShapeDtypeStruct((8, n), F32),
        compiler_params=_cparams(("parallel",), 32),
        name="mod",
    )(c8, w_mod, b_mod.reshape(1, n))


def _wsplit_kernel(w_ref, *o_refs, bounds):
    for o_ref, lo, hi in zip(o_refs, bounds[:-1], bounds[1:]):
        o_ref[:, :hi - lo] = w_ref[0, :, lo:hi].astype(o_ref.dtype)
        if o_ref.shape[1] > hi - lo:
            o_ref[:, hi - lo:] = jnp.zeros((o_ref.shape[0], o_ref.shape[1] - (hi - lo)), o_ref.dtype)


def _wsplit_call(w, bounds):
    d = w.shape[1]
    tr = 256
    widths = [-(-(hi - lo) // LANES) * LANES for lo, hi in zip(bounds[:-1], bounds[1:])]
    return pl.pallas_call(
        functools.partial(_wsplit_kernel, bounds=bounds),
        grid=(d // tr,),
        in_specs=[pl.BlockSpec((1, tr, w.shape[2]), lambda i: (0, i, 0))],
        out_specs=[pl.BlockSpec((tr, wd), lambda i: (i, 0)) for wd in widths],
        out_shape=[jax.ShapeDtypeStruct((d, wd), BF16) for wd in widths],
        compiler_params=_cparams(("parallel",), 32),
        name="wsplit",
    )(w)


def _inconv_kernel(*refs, grid_w, use_rows, tm, cw, n_extra):
    refs = list(refs)
    prev_ref = refs.pop(0) if use_rows else None
    x_ref = refs.pop(0)
    next_ref = refs.pop(0) if use_rows else None
    sh_ref, sc_ref, g_ref, wq_ref, wgate_ref = refs[:5]
    wx_refs = refs[5:5 + n_extra]
    w_ref, par_ref, q_ref, k_ref, v_ref, go_ref, gr_ref = refs[5 + n_extra:12 + n_extra]
    ox_refs = refs[12 + n_extra:]

    def modulated(x):
        return (_rmsnorm(x, g_ref[...]) * (1.0 + sc_ref[0]) + sh_ref[0]).astype(BF16)

    r = pl.program_id(1)
    nr = pl.num_programs(1)
    u = modulated(x_ref[0])

    def project(wx_ref, o_ref, c0, step):
        o_ref[0, :, c0:c0 + step] = _mm(u, wx_ref[:, c0:c0 + step]).astype(o_ref.dtype)

    plain = [functools.partial(project, wx_ref, o_ref, c0, min(wx_ref.shape[1], 512))
             for wx_ref, o_ref in zip(wx_refs, ox_refs) for c0 in range(0, wx_ref.shape[1], min(wx_ref.shape[1], 512))]
    n_conv_chunks = QKV_DIM // cw

    t = lax.broadcasted_iota(I32, (tm, 1), 0)
    col = jnp.bitwise_and(t, grid_w - 1)
    m_left = (col != 0).astype(F32)
    m_right = (col != grid_w - 1).astype(F32)
    if use_rows:
        u_prev = modulated(prev_ref[0])
        u_next = modulated(next_ref[0])
        has_prev = (r > 0).astype(F32)
        has_next = (r < nr - 1).astype(F32)
    for c0 in range(0, QKV_DIM, cw):
        wq = wq_ref[:, c0:c0 + cw]
        xm = _mm(u, wq)
        if use_rows:
            up = jnp.concatenate([_mm(u_prev, wq) * has_prev, xm[:tm - grid_w]], axis=0)
            dn = jnp.concatenate([xm[grid_w:], _mm(u_next, wq) * has_next], axis=0)

        def colsum(kc):
            y = xm * w_ref[3 + kc:4 + kc, c0:c0 + cw]
            if use_rows:
                y = y + up * w_ref[kc:kc + 1, c0:c0 + cw] + dn * w_ref[6 + kc:7 + kc, c0:c0 + cw]
            return y

        acc = (colsum(1) + pltpu.roll(colsum(0), 1, axis=0) * m_left
               + pltpu.roll(colsum(2), tm - 1, axis=0) * m_right)
        s = acc * _sigmoid(acc)
        for h0 in range(0, cw, HEAD_DIM):
            c = c0 + h0
            seg = s[:, h0:h0 + HEAD_DIM]
            if c < 2 * QK_DIM:
                seg = seg * lax.rsqrt(jnp.sum(seg * seg, axis=-1, keepdims=True) + EPS)
            if c < QK_DIM:
                q_ref[0, :, c:c + HEAD_DIM] = (seg * HEAD_DIM ** -0.5).astype(q_ref.dtype)
            elif c < 2 * QK_DIM:
                k_ref[0, :, c - QK_DIM:c - QK_DIM + HEAD_DIM] = seg.astype(k_ref.dtype)
            else:
                v_ref[0, :, c - 2 * QK_DIM:c - 2 * QK_DIM + HEAD_DIM] = seg.astype(v_ref.dtype)
        ci = c0 // cw
        for job in plain[ci * len(plain) // n_conv_chunks:(ci + 1) * len(plain) // n_conv_chunks]:
            job()
    g = _mm(u, wgate_ref[...])
    a = g + par_ref[1:2, :]
    softplus = jnp.maximum(a, 0.0) + jnp.log1p(jnp.exp(-jnp.abs(a)))
    log_g = -jnp.exp(par_ref[0:1, :]) * softplus
    lane = lax.broadcasted_iota(I32, g.shape, 1)
    go_ref[0] = jnp.where(lane < GATE_LANE0, _sigmoid(g), log_g)
    gr_ref[0] = jnp.transpose(log_g)[GATE_LANE0:GATE_LANE0 + 2 * NV_HEADS]


def _inconv_call(x3, sh, sc, g, w_qkv, w_gates, w_extra, extra_dtypes, conv_w, par, grid_w, use_rows, tm):
    b, t, d = x3.shape
    kern = functools.partial(_inconv_kernel, grid_w=grid_w, use_rows=use_rows, tm=tm, cw=512, n_extra=len(w_extra))
    per = tm // grid_w
    nrow = t // grid_w
    tile = lambda wd: pl.BlockSpec((1, tm, wd), lambda i, r: (i, r, 0))
    vec = pl.BlockSpec((1, 1, d), lambda i, r: (i, 0, 0))
    const = lambda a: pl.BlockSpec(a.shape, lambda i, r: (0,) * a.ndim, pipeline_mode=pl.Buffered(1))
    in_specs, args = [], []
    if use_rows:
        in_specs.append(pl.BlockSpec((1, grid_w, d), lambda i, r: (i, jnp.maximum(r * per - 1, 0), 0)))
        args.append(x3)
    in_specs.append(tile(d))
    args.append(x3)
    if use_rows:
        in_specs.append(pl.BlockSpec((1, grid_w, d), lambda i, r: (i, jnp.minimum((r + 1) * per, nrow - 1), 0)))
        args.append(x3)
    consts = [g, w_qkv, w_gates, *w_extra, conv_w, par]
    in_specs += [vec, vec] + [const(a) for a in consts]
    args += [sh, sc] + consts
    widths = [QK_DIM, QK_DIM, V_DIM, LANES] + [w.shape[1] for w in w_extra]
    dtypes = [BF16, BF16, BF16, F32] + list(extra_dtypes)
    out_shape = [jax.ShapeDtypeStruct((b, t, wd), dt) for wd, dt in zip(widths, dtypes)]
    out_specs = [tile(wd) for wd in widths]
    out_shape.insert(4, jax.ShapeDtypeStruct((b, 2 * NV_HEADS, t), F32))
    out_specs.insert(4, pl.BlockSpec((1, 2 * NV_HEADS, tm), lambda i, r: (i, 0, r)))
    return pl.pallas_call(
        kern, grid=(b, t // tm), in_specs=in_specs, out_specs=out_specs, out_shape=out_shape,
        compiler_params=_cparams(("parallel", "parallel"), 56),
        name="inconv_rows" if use_rows else "inconv_seq",
    )(*args)


def _gdn_kernel(qf, kf, vf, gf, rf, qb, kb, vb, gb, rb, s0_ref, of, ob, sfin_ref, s_ref):
    i = pl.program_id(1)
    nc = pl.num_programs(1)

    @pl.when(i == 0)
    def _():
        s_ref[...] = s0_ref[0]

    c = qf.shape[1]
    per = NV_HEADS // NQK_HEADS
    row = lax.broadcasted_iota(I32, (c, c), 0)
    colj = lax.broadcasted_iota(I32, (c, c), 1)
    eye = jnp.where(row == colj, 1.0, 0.0)
    nt_dims = (((1,), (1,)), ((), ()))
    tn_dims = (((0,), (0,)), ((), ()))

    seqs = []
    for d, (q_r, k_r, v_r, g_r, r_r, o_r) in enumerate(((qf, kf, vf, gf, rf, of), (qb, kb, vb, gb, rb, ob))):
        rev = d == 1
        incl = (colj >= row) if rev else (colj <= row)
        strict = (colj > row) if rev else (colj < row)
        gates = g_r[0]
        tri_c = jnp.where(incl, 1.0, 0.0)
        gcm = _mm(tri_c, gates, HIGHEST)
        gcr = _mm(r_r[0], tri_c, HIGHEST, dims=nt_dims)
        for hq in range(NQK_HEADS):
            q = q_r[0, :, hq * HEAD_DIM:(hq + 1) * HEAD_DIM]
            k = k_r[0, :, hq * HEAD_DIM:(hq + 1) * HEAD_DIM]
            kq = lax.dot_general(jnp.concatenate([q, k], axis=0), k, nt_dims, preferred_element_type=F32)
            for j in range(per):
                h = hq * per + j
                idx = d * NV_HEADS + h
                gc_c = gcm[:, GATE_LANE0 + idx:GATE_LANE0 + idx + 1]
                seqs.append(dict(d=d, h=h, o_r=o_r, v_r=v_r, q=q, k=k, qk=kq[:c], kk=kq[c:], incl=incl, strict=strict,
                                 beta=gates[:, idx:idx + 1], gc_c=gc_c, gc_r=gcr[idx:idx + 1, :],
                                 ge=gc_c[0:1] if rev else gc_c[c - 1:c]))

    def same_block(m):
        sh = int(math.log2(m))
        return jnp.right_shift(row, sh) == jnp.right_shift(colj, sh)

    for s in seqs:
        s['decay'] = jnp.where(s['incl'], jnp.exp(jnp.where(s['incl'], s['gc_c'] - s['gc_r'], 0.0)), 0.0)
        s['a'] = jnp.where(s['strict'], s['beta'] * s['kk'] * s['decay'], 0.0)
        s['t'] = eye - jnp.where(same_block(2), s['a'], 0.0)
    m = 4
    while m <= c:
        between = jnp.logical_and(same_block(m), jnp.logical_not(same_block(m // 2)))
        for s in seqs:
            s['te'] = _mm(s['t'], jnp.where(between, s['a'], 0.0))
        for s in seqs:
            s['t'] = s['t'] - _mm(s['te'], s['t'])
        m *= 2
    for s in seqs:
        h = s['h']
        egc = jnp.exp(s['gc_c'])
        kf32 = s['k'].astype(F32)
        v = s['v_r'][0, :, h * HEAD_DIM:(h + 1) * HEAD_DIM].astype(F32)
        rhs = jnp.concatenate([s['beta'] * v, (s['beta'] * egc) * kf32], axis=1)
        s['sol'] = _mm(s['t'], rhs)
        s['q_dec'] = s['q'].astype(F32) * egc
        s['k_dec'] = kf32 * jnp.exp(s['ge'] - s['gc_c'])
    for s in seqs:
        s['ws'] = _mm(jnp.concatenate([s['sol'][:, HEAD_DIM:], s['q_dec']], axis=0), s_ref[s['d'], s['h']])
    for s in seqs:
        s['u'] = s['sol'][:, :HEAD_DIM] - s['ws'][:c]
        s_ref[s['d'], s['h']] = (jnp.exp(s['ge']) * s_ref[s['d'], s['h']]
                                 + _mm(s['k_dec'], s['u'], dims=tn_dims))
    for s in seqs:
        h = s['h']
        o = s['ws'][c:] + _mm(s['qk'] * s['decay'], s['u'])
        s['o_r'][0, :, h * HEAD_DIM:(h + 1) * HEAD_DIM] = o.astype(s['o_r'].dtype)

    @pl.when(i == nc - 1)
    def _():
        sfin_ref[0] = s_ref[...]


def _gdn_call(q, k, v, go, rows, s0):
    b, t, _ = q.shape
    c = GDN_CHUNK
    nc = t // c
    fwd = lambda i, n: (i, n, 0)
    bwd = lambda i, n: (i, nc - 1 - n, 0)
    rfwd = lambda i, n: (i, 0, n)
    rbwd = lambda i, n: (i, 0, nc - 1 - n)
    state_spec = pl.BlockSpec((1,) + s0.shape[1:], lambda i, n: (i, 0, 0, 0, 0))

    def specs(m3, mr):
        return [pl.BlockSpec((1, c, QK_DIM), m3), pl.BlockSpec((1, c, QK_DIM), m3),
                pl.BlockSpec((1, c, V_DIM), m3), pl.BlockSpec((1, c, LANES), m3),
                pl.BlockSpec((1, 2 * NV_HEADS, c), mr)]

    return pl.pallas_call(
        _gdn_kernel,
        grid=(b, nc),
        in_specs=specs(fwd, rfwd) + specs(bwd, rbwd) + [state_spec],
        out_specs=[pl.BlockSpec((1, c, V_DIM), fwd), pl.BlockSpec((1, c, V_DIM), bwd), state_spec],
        out_shape=[jax.ShapeDtypeStruct((b, t, V_DIM), BF16), jax.ShapeDtypeStruct((b, t, V_DIM), BF16),
                   jax.ShapeDtypeStruct(s0.shape, F32)],
        scratch_shapes=[pltpu.VMEM(s0.shape[1:], F32)],
        compiler_params=_cparams(("parallel", "arbitrary"), 48),
        name="gdn",
    )(q, k, v, go, rows, q, k, v, go, rows, s0)


def _fnet1_kernel(x_ref, f_ref, ar_ref, ai_ref):
    n = x_ref.shape[1]
    for j in range(x_ref.shape[2]):
        a = _mm(f_ref[...], x_ref[0, :, j, :])
        ar_ref[0, j] = a[:n]
        ai_ref[0, j] = a[n:]


def _fnet2_kernel(ar_ref, ai_ref, g_ref, wc_ref, o_ref, *, scale):
    n = ar_ref.shape[1]
    cols = ar_ref.shape[2]
    zs = []
    for m in range(cols):
        a2 = jnp.concatenate([ar_ref[0, :, m, :], ai_ref[0, :, m, :]], axis=0)
        zs.append(_mm(g_ref[m], a2))
    zr = jnp.concatenate([z[:n] for z in zs], axis=0)
    zi = jnp.concatenate([z[n:] for z in zs], axis=0)
    for g0 in range(0, zr.shape[1], HEAD_DIM):
        y = _mm(jnp.concatenate([zr[:, g0:g0 + HEAD_DIM], zi[:, g0:g0 + HEAD_DIM]], axis=1), wc_ref[...])
        for m in range(cols):
            o_ref[0, :, m, g0:g0 + HEAD_DIM] = y[m * n:(m + 1) * n] * scale


def _fnet_tables(n):
    a = np.arange(n)
    ang1 = 2.0 * np.pi * np.outer(a, a) / n
    f1 = np.concatenate([np.cos(ang1), -np.sin(ang1)], axis=0)
    m = a[:, None] + n * a[None, :]
    ang2 = 2.0 * np.pi * ((m[:, :, None] * a[None, None, :]) % (n * n)) / (n * n)
    gc, gs = np.cos(ang2), np.sin(ang2)
    g2 = np.concatenate([np.concatenate([gc, gs], axis=2), np.concatenate([-gs, gc], axis=2)], axis=1)
    angc = 2.0 * np.pi * np.outer(np.arange(HEAD_DIM), np.arange(HEAD_DIM)) / HEAD_DIM
    wc = np.concatenate([np.cos(angc), np.sin(angc)], axis=0)
    f = lambda x: jnp.asarray(x, F32).astype(BF16)
    return f(f1), f(g2), f(wc)


def _fnet_call(f):
    b, l, c = f.shape
    n = GRID_W
    assert l == n * n
    f1, g2, wc = _fnet_tables(n)
    cols = 4 * SUBLANES
    ar, ai = pl.pallas_call(
        _fnet1_kernel,
        grid=(b, n // cols),
        in_specs=[pl.BlockSpec((1, n, cols, c), lambda i, j: (i, 0, j, 0)),
                  pl.BlockSpec((2 * n, n), lambda i, j: (0, 0))],
        out_specs=[pl.BlockSpec((1, cols, n, c), lambda i, j: (i, j, 0, 0))] * 2,
        out_shape=[jax.ShapeDtypeStruct((b, n, n, c), F32)] * 2,
        compiler_params=_cparams(("parallel", "parallel"), 32),
        name="fnet1",
    )(f.reshape(b, n, n, c), f1)
    out = pl.pallas_call(
        functools.partial(_fnet2_kernel, scale=1.0 / math.sqrt(l * HEAD_DIM)),
        grid=(b, n // cols),
        in_specs=[pl.BlockSpec((1, n, cols, c), lambda i, j: (i, 0, j, 0)),
                  pl.BlockSpec((1, n, cols, c), lambda i, j: (i, 0, j, 0)),
                  pl.BlockSpec((cols, 2 * n, 2 * n), lambda i, j: (j, 0, 0)),
                  pl.BlockSpec((2 * HEAD_DIM, HEAD_DIM), lambda i, j: (0, 0))],
        out_specs=pl.BlockSpec((1, n, cols, c), lambda i, j: (i, 0, j, 0)),
        out_shape=jax.ShapeDtypeStruct((b, n, n, c), F32),
        compiler_params=_cparams(("parallel", "parallel"), 32),
        name="fnet2",
    )(ar, ai, g2, wc)
    return out.reshape(b, l, c)


def _merge_kernel(of_ref, ob_ref, z_ref, fm_ref, gab_ref, x_ref, g1_ref, sh2_ref, sc2_ref, gn_ref, n2_ref,
                  wg_ref, wf_ref, wm_ref, wr_ref, br_ref, x1_ref, h2_ref, lg_ref):
    d = x_ref.shape[1]
    tm = x_ref.shape[0]
    halves = [slice(0, tm // 2), slice(tm // 2, tm)]
    yb_in = []
    for r in halves:
        o = of_ref[r, :].astype(F32) + ob_ref[r, :].astype(F32)
        z = z_ref[r, :].astype(F32)
        parts = []
        for h0 in range(0, V_DIM, HEAD_DIM):
            oh = o[:, h0:h0 + HEAD_DIM]
            parts.append(oh * lax.rsqrt(jnp.mean(oh * oh, axis=-1, keepdims=True) + EPS) * gn_ref[...])
        yb_in.append(jnp.concatenate(parts, axis=1) * (z * _sigmoid(z)))
    yb = [_mm(v, wg_ref[...]) for v in yb_in]
    ya = [_mm(fm_ref[r, :], wf_ref[...]) for r in halves]
    mixed = [_sigmoid(gab_ref[r, :d].astype(F32)) * a + _sigmoid(gab_ref[r, d:].astype(F32)) * b_
             for r, a, b_ in zip(halves, ya, yb)]
    mm = [_mm(v, wm_ref[...]) for v in mixed]
    w = wr_ref[...]
    w_hi = w.astype(BF16)
    w2 = jnp.concatenate([w_hi, (w - w_hi.astype(F32)).astype(BF16)], axis=1)
    for r, v in zip(halves, mm):
        x1 = x_ref[r, :] + g1_ref[0] * v
        x1_ref[r, :] = x1
        h2 = _rmsnorm(x1, n2_ref[...]) * (1.0 + sc2_ref[0]) + sh2_ref[0]
        h2_ref[r, :] = _pack_bf16_pairs(h2)
        h_hi = h2.astype(BF16)
        h_lo = (h2 - h_hi.astype(F32)).astype(BF16)
        part = _mm(h_hi, w2)
        lg_ref[r, :] = (part[:, :LANES] + part[:, LANES:]) + _mm(h_lo, w_hi) + br_ref[...]


def _merge_call(of, ob, z, fm, gab, x2, g1, sh2, sc2, gn, n2, wg, wf, wm, wr, br, tokens_per_batch, tm):
    n, d = x2.shape
    per = tokens_per_batch // tm
    tok = lambda wd: pl.BlockSpec((tm, wd), lambda i: (i, 0))
    vec = pl.BlockSpec((1, 1, d), lambda i: (i // per, 0, 0))
    full = lambda a: pl.BlockSpec(a.shape, lambda i: (0,) * a.ndim)
    return pl.pallas_call(
        _merge_kernel,
        grid=(n // tm,),
        in_specs=[tok(V_DIM), tok(V_DIM), tok(V_DIM), tok(F_DIM), tok(2 * d), tok(d), vec, vec, vec,
                  full(gn), full(n2), full(wg), full(wf), full(wm), full(wr), full(br)],
        out_specs=[tok(d), tok(d // 2), tok(LANES)],
        out_shape=[jax.ShapeDtypeStruct((n, d), F32), jax.ShapeDtypeStruct((n, d // 2), I32),
                   jax.ShapeDtypeStruct((n, LANES), F32)],
        compiler_params=_cparams(("parallel",), 56),
        name="merge",
    )(of, ob, z, fm, gab, x2, g1, sh2, sc2, gn, n2, wg, wf, wm, wr, br)


def _route_kernel(lg_ref, idx_ref, w_ref, rank_ref, cnt_ref, run_ref):
    i = pl.program_id(0)

    @pl.when(i == 0)
    def _():
        run_ref[...] = jnp.zeros_like(run_ref)

    l = lg_ref[...]
    tm = l.shape[0]
    lane = lax.broadcasted_iota(I32, l.shape, 1)
    vals, idxs = [], []
    for _ in range(TOP_K):
        m = jnp.max(l, axis=-1, keepdims=True)
        idx = jnp.min(jnp.where(l == m, lane, LANES), axis=-1, keepdims=True)
        vals.append(m)
        idxs.append(idx)
        l = jnp.where(lane == idx, NEG_BIG * 2.0, l)
    es = [jnp.exp(v - vals[0]) for v in vals]
    inv = 1.0 / (es[0] + es[1] + es[2] + es[3])
    picked = jnp.zeros(l.shape, F32)
    for idx in idxs:
        picked = picked + (lane == idx).astype(F32)
    r = lax.broadcasted_iota(I32, (tm, tm), 0)
    cidx = lax.broadcasted_iota(I32, (tm, tm), 1)
    before = _mm(jnp.where(cidx < r, 1.0, 0.0), picked) + run_ref[...]
    idx_out = jnp.zeros(l.shape, I32)
    w_out = jnp.zeros(l.shape, F32)
    rank_out = jnp.zeros(l.shape, F32)
    for k in range(TOP_K):
        rk = jnp.sum(jnp.where(lane == idxs[k], before, 0.0), axis=-1, keepdims=True)
        idx_out = jnp.where(lane == k, idxs[k], idx_out)
        w_out = jnp.where(lane == k, es[k] * inv, w_out)
        rank_out = jnp.where(lane == k, rk, rank_out)
    idx_ref[...] = jnp.transpose(idx_out.astype(F32))[:SUBLANES].astype(I32)
    w_ref[...] = w_out
    rank_ref[...] = jnp.transpose(rank_out)[:SUBLANES].astype(I32)
    run_ref[...] = run_ref[...] + jnp.sum(picked, axis=0, keepdims=True)
    cnt_ref[...] = run_ref[...]


def _route_call(logits, tm):
    n = logits.shape[0]
    tok = pl.BlockSpec((tm, LANES), lambda i: (i, 0))
    tok_t = pl.BlockSpec((SUBLANES, tm), lambda i: (0, i))
    return pl.pallas_call(
        _route_kernel,
        grid=(n // tm,),
        in_specs=[tok],
        out_specs=[tok_t, tok, tok_t, pl.BlockSpec((1, LANES), lambda i: (0, 0))],
        out_shape=[jax.ShapeDtypeStruct((SUBLANES, n), I32), jax.ShapeDtypeStruct((n, LANES), F32),
                   jax.ShapeDtypeStruct((SUBLANES, n), I32), jax.ShapeDtypeStruct((1, LANES), F32)],
        scratch_shapes=[pltpu.VMEM((1, LANES), F32)],
        compiler_params=_cparams(("arbitrary",), 32),
        name="route",
    )(logits)


def _sc_mesh():
    return plsc.VectorSubcoreMesh(core_axis_name="c", subcore_axis_name="s",
                                  num_cores=SC_CORES, num_subcores=SC_SUBCORES)


def _sc_worker_id():
    return lax.axis_index("s") * SC_CORES + lax.axis_index("c")


def _sc_scatter_rows(x, idx, n_out):
    n, d = x.shape
    per_w = idx.shape[0] // SC_WORKERS
    nwin = per_w // SC_WINDOW
    assert per_w * SC_WORKERS == idx.shape[0] and nwin * SC_WINDOW == per_w and nwin % 2 == 0 and n % per_w == 0

    def body(x_hbm, idx_hbm, out_hbm, idx_v, rows_v, sem_r, sem_w):
        wid = _sc_worker_id()
        row0 = lax.rem(wid * per_w, n)
        pltpu.sync_copy(idx_hbm.at[wid], idx_v)

        def read(j, b):
            return pltpu.make_async_copy(x_hbm.at[pl.ds(row0 + j * SC_WINDOW, SC_WINDOW)], rows_v.at[b], sem_r.at[b])

        def write(j, b):
            return pltpu.make_async_copy(rows_v.at[b], out_hbm.at[idx_v.at[j]], sem_w.at[b])

        @pl.loop(0, nwin, step=2)
        def _(j):
            read(j, 0).start()
            read(j + 1, 1).start()
            read(j, 0).wait()
            write(j, 0).start()
            read(j + 1, 1).wait()
            write(j + 1, 1).start()
            write(j, 0).wait()
            write(j + 1, 1).wait()

    return pl.kernel(
        body, out_type=jax.ShapeDtypeStruct((n_out, d), x.dtype), mesh=_sc_mesh(),
        scratch_types=[pltpu.VMEM((nwin, SC_WINDOW), I32), pltpu.VMEM((2, SC_WINDOW, d), x.dtype),
                       pltpu.SemaphoreType.DMA((2,)), pltpu.SemaphoreType.DMA((2,))],
        name="sc_scatter_rows",
    )(x, idx.reshape(SC_WORKERS, nwin, SC_WINDOW))


def _sc_gather_rows(y, idx):
    d = y.shape[1]
    total = idx.shape[0]
    per_w = total // SC_WORKERS
    nwin = per_w // SC_WINDOW
    assert per_w * SC_WORKERS == total and nwin * SC_WINDOW == per_w and nwin % 2 == 0

    def body(y_hbm, idx_hbm, out_hbm, idx_v, rows_v, sem_r, sem_w):
        wid = _sc_worker_id()
        row0 = wid * per_w
        pltpu.sync_copy(idx_hbm.at[wid], idx_v)

        def read(j, b):
            return pltpu.make_async_copy(y_hbm.at[idx_v.at[j]], rows_v.at[b], sem_r.at[b])

        def write(j, b):
            return pltpu.make_async_copy(rows_v.at[b], out_hbm.at[pl.ds(row0 + j * SC_WINDOW, SC_WINDOW)], sem_w.at[b])

        @pl.loop(0, nwin, step=2)
        def _(j):
            read(j, 0).start()
            read(j + 1, 1).start()
            read(j, 0).wait()
            write(j, 0).start()
            read(j + 1, 1).wait()
            write(j + 1, 1).start()
            write(j, 0).wait()
            write(j + 1, 1).wait()

    return pl.kernel(
        body, out_type=jax.ShapeDtypeStruct((total, d), y.dtype), mesh=_sc_mesh(),
        scratch_types=[pltpu.VMEM((nwin, SC_WINDOW), I32), pltpu.VMEM((2, SC_WINDOW, d), y.dtype),
                       pltpu.SemaphoreType.DMA((2,)), pltpu.SemaphoreType.DMA((2,))],
        name="sc_gather_rows",
    )(y, idx.reshape(SC_WORKERS, nwin, SC_WINDOW))


def _expert_kernel(be_ref, nv_ref, nxt_ref, par_ref, used_ref, x_ref, wg_hbm, wu_hbm, wd_hbm, bg_ref, bu_ref, bd_ref,
                   y_ref, wgf, wuf, wdf, wgb, wub, wdb, sem):
    i = pl.program_id(0)

    def fetch(e, s):
        return [pltpu.make_async_copy(w_hbm.at[e], w_f.at[s], sem.at[s])
                for w_hbm, w_f in ((wg_hbm, wgf), (wu_hbm, wuf), (wd_hbm, wdf))]

    @pl.when(i < nv_ref[0])
    def _():
        e = be_ref[i]
        s = par_ref[e]

        @pl.when(i == 0)
        def _():
            for cp in fetch(e, s):
                cp.start()

        @pl.when(jnp.logical_or(i == 0, be_ref[jnp.maximum(i - 1, 0)] != e))
        def _():
            for cp in fetch(e, s):
                cp.wait()
            wgb[...] = wgf[s].astype(BF16)
            wub[...] = wuf[s].astype(BF16)
            wdb[...] = wdf[s].astype(BF16)

            @pl.when(nxt_ref[e] >= 0)
            def _():
                for cp in fetch(nxt_ref[e], 1 - s):
                    cp.start()

        def ffn(nrows):
            x = _unpack_bf16_pairs(x_ref[:nrows, :]).astype(BF16)
            gate = jnp.minimum(_mm(x, wgb[...]) + bg_ref[0], SWIGLU_LIMIT)
            up = jnp.clip(_mm(x, wub[...]) + bu_ref[0], -SWIGLU_LIMIT, SWIGLU_LIMIT)
            act = (up + 1.0) * gate * _sigmoid(SWIGLU_ALPHA * gate)
            y_ref[:nrows, :] = _pack_bf16_pairs(_mm(act, wdb[...]) + bd_ref[0])

        half = x_ref.shape[0] // 2

        @pl.when(used_ref[i] > half)
        def _():
            ffn(x_ref.shape[0])

        @pl.when(used_ref[i] <= half)
        def _():
            ffn(half)


def _expert_call(block_e, n_valid, next_e, parity, used, xs, w_gate, w_up, w_down, b_gate, b_up, b_down):
    nb = block_e.shape[0]
    tmb = EXPERT_BLOCK
    ne, d, de = w_gate.shape
    bspec = lambda s: pl.BlockSpec((1,) + s, lambda i, be, nv, nx, pa, us: (be[i], 0, 0))
    anyspec = pl.BlockSpec(memory_space=pl.ANY)
    grid_spec = pltpu.PrefetchScalarGridSpec(
        num_scalar_prefetch=5,
        grid=(nb,),
        in_specs=[pl.BlockSpec((tmb, d // 2), lambda i, be, nv, nx, pa, us: (i, 0)),
                  anyspec, anyspec, anyspec, bspec((1, de)), bspec((1, de)), bspec((1, d))],
        out_specs=pl.BlockSpec((tmb, d // 2), lambda i, be, nv, nx, pa, us: (i, 0)),
        scratch_shapes=[pltpu.VMEM((2, d, de), F32), pltpu.VMEM((2, d, de), F32), pltpu.VMEM((2, de, d), F32),
                        pltpu.VMEM((d, de), BF16), pltpu.VMEM((d, de), BF16), pltpu.VMEM((de, d), BF16),
                        pltpu.SemaphoreType.DMA((2,))],
    )
    return pl.pallas_call(
        _expert_kernel,
        grid_spec=grid_spec,
        out_shape=jax.ShapeDtypeStruct(xs.shape, xs.dtype),
        compiler_params=_cparams(("arbitrary",), 56),
        name="expert",
    )(block_e, n_valid, next_e, parity, used, xs, w_gate, w_up, w_down,
      b_gate.reshape(ne, 1, de), b_up.reshape(ne, 1, de), b_down.reshape(ne, 1, d))


def _combine_kernel(y0, y1, y2, y3, w_ref, x1_ref, g2_ref, fg_ref, o_ref):
    w = w_ref[...]
    ys = [_unpack_bf16_pairs(y[...]) for y in (y0, y1, y2, y3)]
    moe = (w[:, 0:1] * ys[0] + w[:, 1:2] * ys[1]) + (w[:, 2:3] * ys[2] + w[:, 3:4] * ys[3])
    o_ref[...] = _rmsnorm(x1_ref[...] + g2_ref[0] * moe, fg_ref[...])


def _combine_call(y4, top_w, x1, g2, fg, tokens_per_batch, tm):
    n, d = x1.shape
    per = tokens_per_batch // tm
    nt = n // tm
    yspec = lambda k: pl.BlockSpec((tm, d // 2), lambda i: (k * nt + i, 0))
    return pl.pallas_call(
        _combine_kernel,
        grid=(nt,),
        in_specs=[yspec(0), yspec(1), yspec(2), yspec(3),
                  pl.BlockSpec((tm, LANES), lambda i: (i, 0)),
                  pl.BlockSpec((tm, d), lambda i: (i, 0)),
                  pl.BlockSpec((1, 1, d), lambda i: (i // per, 0, 0)),
                  pl.BlockSpec((1, d), lambda i: (0, 0))],
        out_specs=pl.BlockSpec((tm, d), lambda i: (i, 0)),
        out_shape=jax.ShapeDtypeStruct((n, d), F32),
        compiler_params=_cparams(("parallel",), 48),
        name="combine",
    )(y4, y4, y4, y4, top_w, x1, g2, fg)


def _slot_kernel(start_ref, idx_ref, rank_ref, o_ref):
    idx = idx_ref[...]
    acc = rank_ref[...]
    for e in range(N_EXPERTS):
        acc = acc + jnp.where(idx == e, start_ref[e], 0)
    o_ref[...] = acc


def _slot_call(pad_start, top_idx, rank):
    full = pl.BlockSpec(top_idx.shape, lambda i, s: (0, 0))
    return pl.pallas_call(
        _slot_kernel,
        grid_spec=pltpu.PrefetchScalarGridSpec(num_scalar_prefetch=1, grid=(1,), in_specs=[full, full],
                                               out_specs=full),
        out_shape=jax.ShapeDtypeStruct(top_idx.shape, I32),
        compiler_params=_cparams(("arbitrary",), 32),
        name="slots",
    )(pad_start, top_idx, rank)


def _routing_tables(top_idx, rank, counts, n):
    tmb = EXPERT_BLOCK
    counts = counts.astype(I32)
    padded = (counts + tmb - 1) // tmb * tmb
    pad_end = jnp.cumsum(padded)
    pad_start = pad_end - padded
    n_blocks = -(-(n * TOP_K + N_EXPERTS * (tmb - 1)) // tmb)
    n_slots = n_blocks * tmb
    dest_flat = _slot_call(pad_start, top_idx, rank)[:TOP_K].reshape(-1)
    block_start = jnp.arange(n_blocks, dtype=I32) * tmb
    block_e = jnp.minimum(jnp.sum((pad_end[None, :] <= block_start[:, None]).astype(I32), axis=1), N_EXPERTS - 1)
    n_valid = (pad_end[-1:] // tmb).astype(I32)
    experts = jnp.arange(N_EXPERTS, dtype=I32)
    used = counts > 0
    later = jnp.where(jnp.logical_and(used[None, :], experts[None, :] > experts[:, None]), experts[None, :], N_EXPERTS)
    next_e = jnp.min(later, axis=1)
    next_e = jnp.where(next_e == N_EXPERTS, -1, next_e).astype(I32)
    parity = ((jnp.cumsum(used.astype(I32)) - used.astype(I32)) % 2).astype(I32)
    row_end = jnp.sum(jnp.where(block_e[:, None] == experts[None, :], (pad_start + counts)[None, :], 0), axis=1)
    rows_used = jnp.clip(row_end - block_start, 0, tmb).astype(I32)
    return block_e, n_valid, next_e, parity, rows_used, dest_flat, n_slots


def _gdn_branch(x3, sh, sc, norm_g, w_qkv, w_gates, w_extra, extra_dtypes, conv_w, par, grid_w, use_rows, tm, s0):
    q, k, v, go, rows, *extra = _inconv_call(x3, sh, sc, norm_g, w_qkv, w_gates, w_extra, extra_dtypes, conv_w, par,
                                             grid_w, use_rows, tm)
    o_f, o_b, s_fin = _gdn_call(q, k, v, go, rows, s0)
    return extra, o_f, o_b, s_fin


def kernel(x, c, ctx, c_ctx, w_mod, b_mod, norm1_g, norm2_g, w_in, conv_w, a_log, dt_bias, gdn_norm_g,
           w_fourier_out, w_gdn_out, w_merge_out, w_router, b_router, w_gate, b_gate, w_up, b_up,
           w_down, b_down, final_norm_g):
    b, l, d = x.shape
    n = b * l
    n_ctx = ctx.shape[1]
    assert w_mod.shape[0] == 1 and l == GRID_W * GRID_W and d == V_DIM

    c8 = jnp.concatenate([c, c_ctx[None, :], jnp.zeros((8 - b - 1, d), F32)], axis=0)
    mod = _mod_call(c8, w_mod[0], b_mod[0])
    sh1, sc1, g1, sh2, sc2, g2 = [mod[:b, j * d:(j + 1) * d].reshape(b, 1, d) for j in range(6)]
    csh1 = jnp.broadcast_to(mod[b:b + 1, 0:d].reshape(1, 1, d), (b, 1, d))
    csc1 = jnp.broadcast_to(mod[b:b + 1, d:2 * d].reshape(1, 1, d), (b, 1, d))

    off_gate = QKV_DIM
    off_z = off_gate + 4 * NV_HEADS
    off_f = off_z + V_DIM
    off_ga = off_f + F_DIM
    w_qkv, w_gates, *w_extra = _wsplit_call(w_in, (0, off_gate, off_z, off_f, off_ga, w_in.shape[2]))
    par = jnp.pad(jnp.stack([a_log[0].reshape(-1), dt_bias[0].reshape(-1)]),
                  ((0, 6), (GATE_LANE0, LANES - 2 * GATE_LANE0)))
    n1 = norm1_g[0].reshape(1, d)
    cw = conv_w[0].reshape(9, QKV_DIM)

    zero_state = jnp.zeros((b, 2, NV_HEADS, HEAD_DIM, HEAD_DIM), F32)
    _, _, _, s_ctx = _gdn_branch(ctx, csh1, csc1, n1, w_qkv, w_gates, [], [], cw, par, n_ctx, False, n_ctx,
                                 zero_state)

    x2 = x.reshape(n, d)
    (z, f, gab), o_f, o_b, _ = _gdn_branch(x, sh1, sc1, n1, w_qkv, w_gates, w_extra, (BF16, F32, BF16),
                                           cw, par, GRID_W, True, TOKEN_TILE, s_ctx)
    z, gab = z.reshape(n, V_DIM), gab.reshape(n, 2 * d)
    fmix = _fnet_call(f).reshape(n, F_DIM)

    wr = jnp.pad(w_router[0], ((0, 0), (0, LANES - N_EXPERTS)))
    br = jnp.pad(b_router[0], (0, LANES - N_EXPERTS), constant_values=NEG_BIG).reshape(1, LANES)
    x1, h2, logits = _merge_call(
        o_f.reshape(n, V_DIM), o_b.reshape(n, V_DIM), z, fmix, gab, x2, g1, sh2, sc2,
        gdn_norm_g[0].reshape(1, HEAD_DIM), norm2_g[0].reshape(1, d),
        w_gdn_out[0].astype(BF16), w_fourier_out[0].astype(BF16), w_merge_out[0].astype(BF16), wr, br, l, TOKEN_TILE)

    top_idx, top_w, rank, counts = _route_call(logits, TOKEN_TILE)
    block_e, n_valid, next_e, parity, rows_used, dest_flat, n_slots = _routing_tables(
        top_idx, rank, counts[0, :N_EXPERTS], n)
    xs = _sc_scatter_rows(h2, dest_flat, n_slots)
    ys = _expert_call(block_e, n_valid, next_e, parity, rows_used, xs, w_gate[0], w_up[0], w_down[0],
                      b_gate[0], b_up[0], b_down[0])
    y4 = _sc_gather_rows(ys, dest_flat)
    out = _combine_call(y4, top_w, x1, g2, final_norm_g.reshape(1, d), l, TOKEN_TILE)
    return out.reshape(b, l, d)
```

```python
import functools
import math

import jax
import jax.numpy as jnp
import numpy as np
from jax import lax
from jax.experimental import pallas as pl
from jax.experimental.pallas import tpu as pltpu
from jax.experimental.pallas import tpu_sc as plsc

F32 = jnp.float32
BF16 = jnp.bfloat16
I32 = jnp.int32
HIGHEST = lax.Precision.HIGHEST

GRID_W = 64
NQK_HEADS = 4
NV_HEADS = 8
HEAD_DIM = 128
QK_DIM = NQK_HEADS * HEAD_DIM
V_DIM = NV_HEADS * HEAD_DIM
QKV_DIM = 2 * QK_DIM + V_DIM
F_GROUPS = 4
F_DIM = F_GROUPS * HEAD_DIM
N_EXPERTS = 32
TOP_K = 4
SWIGLU_ALPHA = 1.702
SWIGLU_LIMIT = 7.0
EPS = 1e-6

LANES = 128
SUBLANES = 8
GATE_LANE0 = 16
GDN_CHUNK = 128
EXPERT_BLOCK = 512
TOKEN_TILE = 512
SC_CORES = 2
SC_SUBCORES = 16
SC_WORKERS = SC_CORES * SC_SUBCORES
SC_WINDOW = 64
NEG_BIG = -1e30
MIB = 2 ** 20


def _cparams(sem, vmem_mib):
    return pltpu.CompilerParams(dimension_semantics=sem, vmem_limit_bytes=vmem_mib * MIB)


def _mm(a, b, prec=None, dims=(((1,), (0,)), ((), ()))):
    if prec is None:
        return lax.dot_general(a.astype(BF16), b.astype(BF16), dims, preferred_element_type=F32)
    return lax.dot_general(a.astype(F32), b.astype(F32), dims, precision=prec, preferred_element_type=F32)


def _sigmoid(x):
    return 1.0 / (1.0 + jnp.exp(-x))


def _rmsnorm(x, g):
    return x * lax.rsqrt(jnp.mean(x * x, axis=-1, keepdims=True) + EPS) * g


def _pack_bf16_pairs(x):
    half = x.shape[1] // 2
    bits = lax.bitcast_convert_type(x.astype(BF16).astype(F32), jnp.uint32)
    packed = jnp.bitwise_or(jnp.right_shift(bits[:, :half], jnp.uint32(16)),
                            jnp.bitwise_and(bits[:, half:], jnp.uint32(0xFFFF0000)))
    return lax.bitcast_convert_type(packed, I32)


def _unpack_bf16_pairs(p):
    bits = lax.bitcast_convert_type(p, jnp.uint32)
    lo = lax.bitcast_convert_type(jnp.left_shift(bits, jnp.uint32(16)), F32)
    hi = lax.bitcast_convert_type(jnp.bitwise_and(bits, jnp.uint32(0xFFFF0000)), F32)
    return jnp.concatenate([lo, hi], axis=1)


def _mod_kernel(c_ref, w_ref, b_ref, o_ref):
    c = c_ref[...]
    o_ref[...] = _mm(c * _sigmoid(c), w_ref[...], HIGHEST) + b_ref[...]


def _mod_call(c8, w_mod, b_mod):
    d, n = w_mod.shape
    tn = 1536
    return pl.pallas_call(
        _mod_kernel,
        grid=(n // tn,),
        in_specs=[pl.BlockSpec((8, d), lambda j: (0, 0)),
                  pl.BlockSpec((d, tn), lambda j: (0, j)),
                  pl.BlockSpec((1, tn), lambda j: (0, j))],
        out_specs=pl.BlockSpec((8, tn), lambda j: (0, j)),
        out_shape=jax.ShapeDtypeStruct((8, n), F32),
        compiler_params=_cparams(("parallel",), 32),
        name="mod",
    )(c8, w_mod, b_mod.reshape(1, n))


def _wsplit_kernel(w_ref, *o_refs, bounds):
    for o_ref, lo, hi in zip(o_refs, bounds[:-1], bounds[1:]):
        o_ref[:, :hi - lo] = w_ref[0, :, lo:hi].astype(o_ref.dtype)
        if o_ref.shape[1] > hi - lo:
            o_ref[:, hi - lo:] = jnp.zeros((o_ref.shape[0], o_ref.shape[1] - (hi - lo)), o_ref.dtype)


def _wsplit_call(w, bounds):
    d = w.shape[1]
    tr = 256
    widths = [-(-(hi - lo) // LANES) * LANES for lo, hi in zip(bounds[:-1], bounds[1:])]
    return pl.pallas_call(
        functools.partial(_wsplit_kernel, bounds=bounds),
        grid=(d // tr,),
        in_specs=[pl.BlockSpec((1, tr, w.shape[2]), lambda i: (0, i, 0))],
        out_specs=[pl.BlockSpec((tr, wd), lambda i: (i, 0)) for wd in widths],
        out_shape=[jax.ShapeDtypeStruct((d, wd), BF16) for wd in widths],
        compiler_params=_cparams(("parallel",), 32),
        name="wsplit",
    )(w)


def _inconv_kernel(*refs, grid_w, use_rows, tm, cw, n_extra):
    refs = list(refs)
    prev_ref = refs.pop(0) if use_rows else None
    x_ref = refs.pop(0)
    next_ref = refs.pop(0) if use_rows else None
    sh_ref, sc_ref, g_ref, wq_ref, wgate_ref = refs[:5]
    wx_refs = refs[5:5 + n_extra]
    w_ref, par_ref, q_ref, k_ref, v_ref, go_ref, gr_ref = refs[5 + n_extra:12 + n_extra]
    ox_refs = refs[12 + n_extra:]

    def modulated(x):
        return (_rmsnorm(x, g_ref[...]) * (1.0 + sc_ref[0]) + sh_ref[0]).astype(BF16)

    r = pl.program_id(1)
    nr = pl.num_programs(1)
    u = modulated(x_ref[0])

    def project(wx_ref, o_ref, c0, step):
        o_ref[0, :, c0:c0 + step] = _mm(u, wx_ref[:, c0:c0 + step]).astype(o_ref.dtype)

    plain = [functools.partial(project, wx_ref, o_ref, c0, min(wx_ref.shape[1], 512))
             for wx_ref, o_ref in zip(wx_refs, ox_refs) for c0 in range(0, wx_ref.shape[1], min(wx_ref.shape[1], 512))]
    n_conv_chunks = QKV_DIM // cw

    t = lax.broadcasted_iota(I32, (tm, 1), 0)
    col = jnp.bitwise_and(t, grid_w - 1)
    m_left = (col != 0).astype(F32)
    m_right = (col != grid_w - 1).astype(F32)
    if use_rows:
        u_prev = modulated(prev_ref[0])
        u_next = modulated(next_ref[0])
        has_prev = (r > 0).astype(F32)
        has_next = (r < nr - 1).astype(F32)
    for c0 in range(0, QKV_DIM, cw):
        wq = wq_ref[:, c0:c0 + cw]
        xm = _mm(u, wq)
        if use_rows:
            up = jnp.concatenate([_mm(u_prev, wq) * has_prev, xm[:tm - grid_w]], axis=0)
            dn = jnp.concatenate([xm[grid_w:], _mm(u_next, wq) * has_next], axis=0)

        def colsum(kc):
            y = xm * w_ref[3 + kc:4 + kc, c0:c0 + cw]
            if use_rows:
                y = y + up * w_ref[kc:kc + 1, c0:c0 + cw] + dn * w_ref[6 + kc:7 + kc, c0:c0 + cw]
            return y

        acc = (colsum(1) + pltpu.roll(colsum(0), 1, axis=0) * m_left
               + pltpu.roll(colsum(2), tm - 1, axis=0) * m_right)
        s = acc * _sigmoid(acc)
        for h0 in range(0, cw, HEAD_DIM):
            c = c0 + h0
            seg = s[:, h0:h0 + HEAD_DIM]
            if c < 2 * QK_DIM:
                seg = seg * lax.rsqrt(jnp.sum(seg * seg, axis=-1, keepdims=True) + EPS)
            if c < QK_DIM:
                q_ref[0, :, c:c + HEAD_DIM] = (seg * HEAD_DIM ** -0.5).astype(q_ref.dtype)
            elif c < 2 * QK_DIM:
                k_ref[0, :, c - QK_DIM:c - QK_DIM + HEAD_DIM] = seg.astype(k_ref.dtype)
            else:
                v_ref[0, :, c - 2 * QK_DIM:c - 2 * QK_DIM + HEAD_DIM] = seg.astype(v_ref.dtype)
        ci = c0 // cw
        for job in plain[ci * len(plain) // n_conv_chunks:(ci + 1) * len(plain) // n_conv_chunks]:
            job()
    g = _mm(u, wgate_ref[...])
    a = g + par_ref[1:2, :]
    softplus = jnp.maximum(a, 0.0) + jnp.log1p(jnp.exp(-jnp.abs(a)))
    log_g = -jnp.exp(par_ref[0:1, :]) * softplus
    lane = lax.broadcasted_iota(I32, g.shape, 1)
    go_ref[0] = jnp.where(lane < GATE_LANE0, _sigmoid(g), log_g)
    gr_ref[0] = jnp.transpose(log_g)[GATE_LANE0:GATE_LANE0 + 2 * NV_HEADS]


def _inconv_call(x3, sh, sc, g, w_qkv, w_gates, w_extra, extra_dtypes, conv_w, par, grid_w, use_rows, tm):
    b, t, d = x3.shape
    kern = functools.partial(_inconv_kernel, grid_w=grid_w, use_rows=use_rows, tm=tm, cw=512, n_extra=len(w_extra))
    per = tm // grid_w
    nrow = t // grid_w
    tile = lambda wd: pl.BlockSpec((1, tm, wd), lambda i, r: (i, r, 0))
    vec = pl.BlockSpec((1, 1, d), lambda i, r: (i, 0, 0))
    const = lambda a: pl.BlockSpec(a.shape, lambda i, r: (0,) * a.ndim, pipeline_mode=pl.Buffered(1))
    in_specs, args = [], []
    if use_rows:
        in_specs.append(pl.BlockSpec((1, grid_w, d), lambda i, r: (i, jnp.maximum(r * per - 1, 0), 0)))
        args.append(x3)
    in_specs.append(tile(d))
    args.append(x3)
    if use_rows:
        in_specs.append(pl.BlockSpec((1, grid_w, d), lambda i, r: (i, jnp.minimum((r + 1) * per, nrow - 1), 0)))
        args.append(x3)
    consts = [g, w_qkv, w_gates, *w_extra, conv_w, par]
    in_specs += [vec, vec] + [const(a) for a in consts]
    args += [sh, sc] + consts
    widths = [QK_DIM, QK_DIM, V_DIM, LANES] + [w.shape[1] for w in w_extra]
    dtypes = [BF16, BF16, BF16, F32] + list(extra_dtypes)
    out_shape = [jax.ShapeDtypeStruct((b, t, wd), dt) for wd, dt in zip(widths, dtypes)]
    out_specs = [tile(wd) for wd in widths]
    out_shape.insert(4, jax.ShapeDtypeStruct((b, 2 * NV_HEADS, t), F32))
    out_specs.insert(4, pl.BlockSpec((1, 2 * NV_HEADS, tm), lambda i, r: (i, 0, r)))
    return pl.pallas_call(
        kern, grid=(b, t // tm), in_specs=in_specs, out_specs=out_specs, out_shape=out_shape,
        compiler_params=_cparams(("parallel", "parallel"), 56),
        name="inconv_rows" if use_rows else "inconv_seq",
    )(*args)


def _gdn_kernel(qf, kf, vf, gf, rf, qb, kb, vb, gb, rb, s0_ref, of, ob, sfin_ref, s_ref):
    i = pl.program_id(1)
    nc = pl.num_programs(1)

    @pl.when(i == 0)
    def _():
        s_ref[...] = s0_ref[0]

    c = qf.shape[1]
    per = NV_HEADS // NQK_HEADS
    row = lax.broadcasted_iota(I32, (c, c), 0)
    colj = lax.broadcasted_iota(I32, (c, c), 1)
    eye = jnp.where(row == colj, 1.0, 0.0)
    nt_dims = (((1,), (1,)), ((), ()))
    tn_dims = (((0,), (0,)), ((), ()))

    seqs = []
    for d, (q_r, k_r, v_r, g_r, r_r, o_r) in enumerate(((qf, kf, vf, gf, rf, of), (qb, kb, vb, gb, rb, ob))):
        rev = d == 1
        incl = (colj >= row) if rev else (colj <= row)
        strict = (colj > row) if rev else (colj < row)
        gates = g_r[0]
        tri_c = jnp.where(incl, 1.0, 0.0)
        gcm = _mm(tri_c, gates, HIGHEST)
        gcr = _mm(r_r[0], tri_c, HIGHEST, dims=nt_dims)
        for hq in range(NQK_HEADS):
            q = q_r[0, :, hq * HEAD_DIM:(hq + 1) * HEAD_DIM]
            k = k_r[0, :, hq * HEAD_DIM:(hq + 1) * HEAD_DIM]
            kq = lax.dot_general(jnp.concatenate([q, k], axis=0), k, nt_dims, preferred_element_type=F32)
            for j in range(per):
                h = hq * per + j
                idx = d * NV_HEADS + h
                gc_c = gcm[:, GATE_LANE0 + idx:GATE_LANE0 + idx + 1]
                seqs.append(dict(d=d, h=h, o_r=o_r, v_r=v_r, q=q, k=k, qk=kq[:c], kk=kq[c:], incl=incl, strict=strict,
                                 beta=gates[:, idx:idx + 1], gc_c=gc_c, gc_r=gcr[idx:idx + 1, :],
                                 ge=gc_c[0:1] if rev else gc_c[c - 1:c]))

    def same_block(m):
        sh = int(math.log2(m))
        return jnp.right_shift(row, sh) == jnp.right_shift(colj, sh)

    for s in seqs:
        s['decay'] = jnp.where(s['incl'], jnp.exp(jnp.where(s['incl'], s['gc_c'] - s['gc_r'], 0.0)), 0.0)
        s['a'] = jnp.where(s['strict'], s['beta'] * s['kk'] * s['decay'], 0.0)
        s['t'] = eye - jnp.where(same_block(2), s['a'], 0.0)
    m = 4
    while m <= c:
        between = jnp.logical_and(same_block(m), jnp.logical_not(same_block(m // 2)))
        for s in seqs:
            s['te'] = _mm(s['t'], jnp.where(between, s['a'], 0.0))
        for s in seqs:
            s['t'] = s['t'] - _mm(s['te'], s['t'])
        m *= 2
    for s in seqs:
        h = s['h']
        egc = jnp.exp(s['gc_c'])
        kf32 = s['k'].astype(F32)
        v = s['v_r'][0, :, h * HEAD_DIM:(h + 1) * HEAD_DIM].astype(F32)
        rhs = jnp.concatenate([s['beta'] * v, (s['beta'] * egc) * kf32], axis=1)
        s['sol'] = _mm(s['t'], rhs)
        s['q_dec'] = s['q'].astype(F32) * egc
        s['k_dec'] = kf32 * jnp.exp(s['ge'] - s['gc_c'])
    for s in seqs:
        s['ws'] = _mm(jnp.concatenate([s['sol'][:, HEAD_DIM:], s['q_dec']], axis=0), s_ref[s['d'], s['h']])
    for s in seqs:
        s['u'] = s['sol'][:, :HEAD_DIM] - s['ws'][:c]
        s_ref[s['d'], s['h']] = (jnp.exp(s['ge']) * s_ref[s['d'], s['h']]
                                 + _mm(s['k_dec'], s['u'], dims=tn_dims))
    for s in seqs:
        h = s['h']
        o = s['ws'][c:] + _mm(s['qk'] * s['decay'], s['u'])
        s['o_r'][0, :, h * HEAD_DIM:(h + 1) * HEAD_DIM] = o.astype(s['o_r'].dtype)

    @pl.when(i == nc - 1)
    def _():
        sfin_ref[0] = s_ref[...]


def _gdn_call(q, k, v, go, rows, s0):
    b, t, _ = q.shape
    c = GDN_CHUNK
    nc = t // c
    fwd = lambda i, n: (i, n, 0)
    bwd = lambda i, n: (i, nc - 1 - n, 0)
    rfwd = lambda i, n: (i, 0, n)
    rbwd = lambda i, n: (i, 0, nc - 1 - n)
    state_spec = pl.BlockSpec((1,) + s0.shape[1:], lambda i, n: (i, 0, 0, 0, 0))

    def specs(m3, mr):
        return [pl.BlockSpec((1, c, QK_DIM), m3), pl.BlockSpec((1, c, QK_DIM), m3),
                pl.BlockSpec((1, c, V_DIM), m3), pl.BlockSpec((1, c, LANES), m3),
                pl.BlockSpec((1, 2 * NV_HEADS, c), mr)]

    return pl.pallas_call(
        _gdn_kernel,
        grid=(b, nc),
        in_specs=specs(fwd, rfwd) + specs(bwd, rbwd) + [state_spec],
        out_specs=[pl.BlockSpec((1, c, V_DIM), fwd), pl.BlockSpec((1, c, V_DIM), bwd), state_spec],
        out_shape=[jax.ShapeDtypeStruct((b, t, V_DIM), BF16), jax.ShapeDtypeStruct((b, t, V_DIM), BF16),
                   jax.ShapeDtypeStruct(s0.shape, F32)],
        scratch_shapes=[pltpu.VMEM(s0.shape[1:], F32)],
        compiler_params=_cparams(("parallel", "arbitrary"), 48),
        name="gdn",
    )(q, k, v, go, rows, q, k, v, go, rows, s0)


def _fnet1_kernel(x_ref, f_ref, ar_ref, ai_ref):
    n = x_ref.shape[1]
    for j in range(x_ref.shape[2]):
        a = _mm(f_ref[...], x_ref[0, :, j, :])
        ar_ref[0, j] = a[:n]
        ai_ref[0, j] = a[n:]


def _fnet2_kernel(ar_ref, ai_ref, g_ref, wc_ref, o_ref, *, scale):
    n = ar_ref.shape[1]
    cols = ar_ref.shape[2]
    zs = []
    for m in range(cols):
        a2 = jnp.concatenate([ar_ref[0, :, m, :], ai_ref[0, :, m, :]], axis=0)
        zs.append(_mm(g_ref[m], a2))
    zr = jnp.concatenate([z[:n] for z in zs], axis=0)
    zi = jnp.concatenate([z[n:] for z in zs], axis=0)
    for g0 in range(0, zr.shape[1], HEAD_DIM):
        y = _mm(jnp.concatenate([zr[:, g0:g0 + HEAD_DIM], zi[:, g0:g0 + HEAD_DIM]], axis=1), wc_ref[...])
        for m in range(cols):
            o_ref[0, :, m, g0:g0 + HEAD_DIM] = y[m * n:(m + 1) * n] * scale


def _fnet_tables(n):
    a = np.arange(n)
    ang1 = 2.0 * np.pi * np.outer(a, a) / n
    f1 = np.concatenate([np.cos(ang1), -np.sin(ang1)], axis=0)
    m = a[:, None] + n * a[None, :]
    ang2 = 2.0 * np.pi * ((m[:, :, None] * a[None, None, :]) % (n * n)) / (n * n)
    gc, gs = np.cos(ang2), np.sin(ang2)
    g2 = np.concatenate([np.concatenate([gc, gs], axis=2), np.concatenate([-gs, gc], axis=2)], axis=1)
    angc = 2.0 * np.pi * np.outer(np.arange(HEAD_DIM), np.arange(HEAD_DIM)) / HEAD_DIM
    wc = np.concatenate([np.cos(angc), np.sin(angc)], axis=0)
    f = lambda x: jnp.asarray(x, F32).astype(BF16)
    return f(f1), f(g2), f(wc)


def _fnet_call(f):
    b, l, c = f.shape
    n = GRID_W
    assert l == n * n
    f1, g2, wc = _fnet_tables(n)
    cols = 4 * SUBLANES
    ar, ai = pl.pallas_call(
        _fnet1_kernel,
        grid=(b, n // cols),
        in_specs=[pl.BlockSpec((1, n, cols, c), lambda i, j: (i, 0, j, 0)),
                  pl.BlockSpec((2 * n, n), lambda i, j: (0, 0))],
        out_specs=[pl.BlockSpec((1, cols, n, c), lambda i, j: (i, j, 0, 0))] * 2,
        out_shape=[jax.ShapeDtypeStruct((b, n, n, c), F32)] * 2,
        compiler_params=_cparams(("parallel", "parallel"), 32),
        name="fnet1",
    )(f.reshape(b, n, n, c), f1)
    out = pl.pallas_call(
        functools.partial(_fnet2_kernel, scale=1.0 / math.sqrt(l * HEAD_DIM)),
        grid=(b, n // cols),
        in_specs=[pl.BlockSpec((1, n, cols, c), lambda i, j: (i, 0, j, 0)),
                  pl.BlockSpec((1, n, cols, c), lambda i, j: (i, 0, j, 0)),
                  pl.BlockSpec((cols, 2 * n, 2 * n), lambda i, j: (j, 0, 0)),
                  pl.BlockSpec((2 * HEAD_DIM, HEAD_DIM), lambda i, j: (0, 0))],
        out_specs=pl.BlockSpec((1, n, cols, c), lambda i, j: (i, 0, j, 0)),
        out_shape=jax.ShapeDtypeStruct((b, n, n, c), F32),
        compiler_params=_cparams(("parallel", "parallel"), 32),
        name="fnet2",
    )(ar, ai, g2, wc)
    return out.reshape(b, l, c)


def _merge_kernel(of_ref, ob_ref, z_ref, fm_ref, gab_ref, x_ref, g1_ref, sh2_ref, sc2_ref, gn_ref, n2_ref,
                  wg_ref, wf_ref, wm_ref, wr_ref, br_ref, x1_ref, h2_ref, lg_ref):
    d = x_ref.shape[1]
    tm = x_ref.shape[0]
    halves = [slice(0, tm // 2), slice(tm // 2, tm)]
    yb_in = []
    for r in halves:
        o = of_ref[r, :].astype(F32) + ob_ref[r, :].astype(F32)
        z = z_ref[r, :].astype(F32)
        parts = []
        for h0 in range(0, V_DIM, HEAD_DIM):
            oh = o[:, h0:h0 + HEAD_DIM]
            parts.append(oh * lax.rsqrt(jnp.mean(oh * oh, axis=-1, keepdims=True) + EPS) * gn_ref[...])
        yb_in.append(jnp.concatenate(parts, axis=1) * (z * _sigmoid(z)))
    yb = [_mm(v, wg_ref[...]) for v in yb_in]
    ya = [_mm(fm_ref[r, :], wf_ref[...]) for r in halves]
    mixed = [_sigmoid(gab_ref[r, :d].astype(F32)) * a + _sigmoid(gab_ref[r, d:].astype(F32)) * b_
             for r, a, b_ in zip(halves, ya, yb)]
    mm = [_mm(v, wm_ref[...]) for v in mixed]
    w = wr_ref[...]
    w_hi = w.astype(BF16)
    w2 = jnp.concatenate([w_hi, (w - w_hi.astype(F32)).astype(BF16)], axis=1)
    for r, v in zip(halves, mm):
        x1 = x_ref[r, :] + g1_ref[0] * v
        x1_ref[r, :] = x1
        h2 = _rmsnorm(x1, n2_ref[...]) * (1.0 + sc2_ref[0]) + sh2_ref[0]
        h2_ref[r, :] = _pack_bf16_pairs(h2)
        h_hi = h2.astype(BF16)
        h_lo = (h2 - h_hi.astype(F32)).astype(BF16)
        part = _mm(h_hi, w2)
        lg_ref[r, :] = (part[:, :LANES] + part[:, LANES:]) + _mm(h_lo, w_hi) + br_ref[...]


def _merge_call(of, ob, z, fm, gab, x2, g1, sh2, sc2, gn, n2, wg, wf, wm, wr, br, tokens_per_batch, tm):
    n, d = x2.shape
    per = tokens_per_batch // tm
    tok = lambda wd: pl.BlockSpec((tm, wd), lambda i: (i, 0))
    vec = pl.BlockSpec((1, 1, d), lambda i: (i // per, 0, 0))
    full = lambda a: pl.BlockSpec(a.shape, lambda i: (0,) * a.ndim)
    return pl.pallas_call(
        _merge_kernel,
        grid=(n // tm,),
        in_specs=[tok(V_DIM), tok(V_DIM), tok(V_DIM), tok(F_DIM), tok(2 * d), tok(d), vec, vec, vec,
                  full(gn), full(n2), full(wg), full(wf), full(wm), full(wr), full(br)],
        out_specs=[tok(d), tok(d // 2), tok(LANES)],
        out_shape=[jax.ShapeDtypeStruct((n, d), F32), jax.ShapeDtypeStruct((n, d // 2), I32),
                   jax.ShapeDtypeStruct((n, LANES), F32)],
        compiler_params=_cparams(("parallel",), 56),
        name="merge",
    )(of, ob, z, fm, gab, x2, g1, sh2, sc2, gn, n2, wg, wf, wm, wr, br)


def _route_kernel(lg_ref, idx_ref, w_ref, rank_ref, cnt_ref, run_ref):
    i = pl.program_id(0)

    @pl.when(i == 0)
    def _():
        run_ref[...] = jnp.zeros_like(run_ref)

    l = lg_ref[...]
    tm = l.shape[0]
    lane = lax.broadcasted_iota(I32, l.shape, 1).astype(F32)
    vals, idxs = [], []
    for _ in range(TOP_K):
        m = jnp.max(l, axis=-1, keepdims=True)
        idx = jnp.min(jnp.where(l == m, lane, float(LANES)), axis=-1, keepdims=True)
        vals.append(m)
        idxs.append(idx)
        l = jnp.where(lane == idx, NEG_BIG * 2.0, l)
    es = [jnp.exp(v - vals[0]) for v in vals]
    inv = 1.0 / (es[0] + es[1] + es[2] + es[3])
    picked = jnp.zeros(l.shape, F32)
    for idx in idxs:
        picked = picked + (lane == idx).astype(F32)
    r = lax.broadcasted_iota(I32, (tm, tm), 0)
    cidx = lax.broadcasted_iota(I32, (tm, tm), 1)
    before = _mm(jnp.where(cidx < r, 1.0, 0.0), picked) + run_ref[...]
    idx_out = jnp.zeros(l.shape, F32)
    w_out = jnp.zeros(l.shape, F32)
    rank_out = jnp.zeros(l.shape, F32)
    for k in range(TOP_K):
        rk = jnp.sum(jnp.where(lane == idxs[k], before, 0.0), axis=-1, keepdims=True)
        idx_out = jnp.where(lane == k, idxs[k], idx_out)
        w_out = jnp.where(lane == k, es[k] * inv, w_out)
        rank_out = jnp.where(lane == k, rk, rank_out)
    idx_ref[...] = jnp.transpose(idx_out)[:SUBLANES].astype(I32)
    w_ref[...] = w_out
    rank_ref[...] = jnp.transpose(rank_out)[:SUBLANES].astype(I32)
    run_ref[...] = run_ref[...] + jnp.sum(picked, axis=0, keepdims=True)
    cnt_ref[...] = run_ref[...]


def _route_call(logits, tm):
    n = logits.shape[0]
    tok = pl.BlockSpec((tm, LANES), lambda i: (i, 0))
    tok_t = pl.BlockSpec((SUBLANES, tm), lambda i: (0, i))
    return pl.pallas_call(
        _route_kernel,
        grid=(n // tm,),
        in_specs=[tok],
        out_specs=[tok_t, tok, tok_t, pl.BlockSpec((1, LANES), lambda i: (0, 0))],
        out_shape=[jax.ShapeDtypeStruct((SUBLANES, n), I32), jax.ShapeDtypeStruct((n, LANES), F32),
                   jax.ShapeDtypeStruct((SUBLANES, n), I32), jax.ShapeDtypeStruct((1, LANES), F32)],
        scratch_shapes=[pltpu.VMEM((1, LANES), F32)],
        compiler_params=_cparams(("arbitrary",), 32),
        name="route",
    )(logits)


def _sc_mesh():
    return plsc.VectorSubcoreMesh(core_axis_name="c", subcore_axis_name="s",
                                  num_cores=SC_CORES, num_subcores=SC_SUBCORES)


def _sc_worker_id():
    return lax.axis_index("s") * SC_CORES + lax.axis_index("c")


def _sc_scatter_rows(x, idx, n_out):
    n, d = x.shape
    per_w = idx.shape[0] // SC_WORKERS
    nwin = per_w // SC_WINDOW
    assert per_w * SC_WORKERS == idx.shape[0] and nwin * SC_WINDOW == per_w and nwin % 2 == 0 and n % per_w == 0

    def body(x_hbm, idx_hbm, out_hbm, idx_v, rows_v, sem_r, sem_w):
        wid = _sc_worker_id()
        row0 = lax.rem(wid * per_w, n)
        pltpu.sync_copy(idx_hbm.at[wid], idx_v)

        def read(j, b):
            return pltpu.make_async_copy(x_hbm.at[pl.ds(row0 + j * SC_WINDOW, SC_WINDOW)], rows_v.at[b], sem_r.at[b])

        def write(j, b):
            return pltpu.make_async_copy(rows_v.at[b], out_hbm.at[idx_v.at[j]], sem_w.at[b])

        @pl.loop(0, nwin, step=2)
        def _(j):
            read(j, 0).start()
            read(j + 1, 1).start()
            read(j, 0).wait()
            write(j, 0).start()
            read(j + 1, 1).wait()
            write(j + 1, 1).start()
            write(j, 0).wait()
            write(j + 1, 1).wait()

    return pl.kernel(
        body, out_type=jax.ShapeDtypeStruct((n_out, d), x.dtype), mesh=_sc_mesh(),
        scratch_types=[pltpu.VMEM((nwin, SC_WINDOW), I32), pltpu.VMEM((2, SC_WINDOW, d), x.dtype),
                       pltpu.SemaphoreType.DMA((2,)), pltpu.SemaphoreType.DMA((2,))],
        name="sc_scatter_rows",
    )(x, idx.reshape(SC_WORKERS, nwin, SC_WINDOW))


def _sc_gather_rows(y, idx):
    d = y.shape[1]
    total = idx.shape[0]
    per_w = total // SC_WORKERS
    nwin = per_w // SC_WINDOW
    assert per_w * SC_WORKERS == total and nwin * SC_WINDOW == per_w and nwin % 2 == 0

    def body(y_hbm, idx_hbm, out_hbm, idx_v, rows_v, sem_r, sem_w):
        wid = _sc_worker_id()
        row0 = wid * per_w
        pltpu.sync_copy(idx_hbm.at[wid], idx_v)

        def read(j, b):
            return pltpu.make_async_copy(y_hbm.at[idx_v.at[j]], rows_v.at[b], sem_r.at[b])

        def write(j, b):
            return pltpu.make_async_copy(rows_v.at[b], out_hbm.at[pl.ds(row0 + j * SC_WINDOW, SC_WINDOW)], sem_w.at[b])

        @pl.loop(0, nwin, step=2)
        def _(j):
            read(j, 0).start()
            read(j + 1, 1).start()
            read(j, 0).wait()
            write(j, 0).start()
            read(j + 1, 1).wait()
            write(j + 1, 1).start()
            write(j, 0).wait()
            write(j + 1, 1).wait()

    return pl.kernel(
        body, out_type=jax.ShapeDtypeStruct((total, d), y.dtype), mesh=_sc_mesh(),
        scratch_types=[pltpu.VMEM((nwin, SC_WINDOW), I32), pltpu.VMEM((2, SC_WINDOW, d), y.dtype),
                       pltpu.SemaphoreType.DMA((2,)), pltpu.SemaphoreType.DMA((2,))],
        name="sc_gather_rows",
    )(y, idx.reshape(SC_WORKERS, nwin, SC_WINDOW))


def _expert_kernel(be_ref, nv_ref, nxt_ref, par_ref, used_ref, x_ref, wg_hbm, wu_hbm, wd_hbm, bg_ref, bu_ref, bd_ref,
                   y_ref, wgf, wuf, wdf, wgb, wub, wdb, sem):
    i = pl.program_id(0)

    def fetch(e, s):
        return [pltpu.make_async_copy(w_hbm.at[e], w_f.at[s], sem.at[s])
                for w_hbm, w_f in ((wg_hbm, wgf), (wu_hbm, wuf), (wd_hbm, wdf))]

    @pl.when(i < nv_ref[0])
    def _():
        e = be_ref[i]
        s = par_ref[e]

        @pl.when(i == 0)
        def _():
            for cp in fetch(e, s):
                cp.start()

        @pl.when(jnp.logical_or(i == 0, be_ref[jnp.maximum(i - 1, 0)] != e))
        def _():
            for cp in fetch(e, s):
                cp.wait()
            wgb[...] = wgf[s].astype(BF16)
            wub[...] = wuf[s].astype(BF16)
            wdb[...] = wdf[s].astype(BF16)

            @pl.when(nxt_ref[e] >= 0)
            def _():
                for cp in fetch(nxt_ref[e], 1 - s):
                    cp.start()

        def ffn(nrows):
            x = _unpack_bf16_pairs(x_ref[:nrows, :]).astype(BF16)
            gate = jnp.minimum(_mm(x, wgb[...]) + bg_ref[0], SWIGLU_LIMIT)
            up = jnp.clip(_mm(x, wub[...]) + bu_ref[0], -SWIGLU_LIMIT, SWIGLU_LIMIT)
            act = (up + 1.0) * gate * _sigmoid(SWIGLU_ALPHA * gate)
            y_ref[:nrows, :] = _pack_bf16_pairs(_mm(act, wdb[...]) + bd_ref[0])

        half = x_ref.shape[0] // 2

        @pl.when(used_ref[i] > half)
        def _():
            ffn(x_ref.shape[0])

        @pl.when(used_ref[i] <= half)
        def _():
            ffn(half)


def _expert_call(block_e, n_valid, next_e, parity, used, xs, w_gate, w_up, w_down, b_gate, b_up, b_down):
    nb = block_e.shape[0]
    tmb = EXPERT_BLOCK
    ne, d, de = w_gate.shape
    bspec = lambda s: pl.BlockSpec((1,) + s, lambda i, be, nv, nx, pa, us: (be[i], 0, 0))
    anyspec = pl.BlockSpec(memory_space=pl.ANY)
    grid_spec = pltpu.PrefetchScalarGridSpec(
        num_scalar_prefetch=5,
        grid=(nb,),
        in_specs=[pl.BlockSpec((tmb, d // 2), lambda i, be, nv, nx, pa, us: (i, 0)),
                  anyspec, anyspec, anyspec, bspec((1, de)), bspec((1, de)), bspec((1, d))],
        out_specs=pl.BlockSpec((tmb, d // 2), lambda i, be, nv, nx, pa, us: (i, 0)),
        scratch_shapes=[pltpu.VMEM((2, d, de), F32), pltpu.VMEM((2, d, de), F32), pltpu.VMEM((2, de, d), F32),
                        pltpu.VMEM((d, de), BF16), pltpu.VMEM((d, de), BF16), pltpu.VMEM((de, d), BF16),
                        pltpu.SemaphoreType.DMA((2,))],
    )
    return pl.pallas_call(
        _expert_kernel,
        grid_spec=grid_spec,
        out_shape=jax.ShapeDtypeStruct(xs.shape, xs.dtype),
        compiler_params=_cparams(("arbitrary",), 56),
        name="expert",
    )(block_e, n_valid, next_e, parity, used, xs, w_gate, w_up, w_down,
      b_gate.reshape(ne, 1, de), b_up.reshape(ne, 1, de), b_down.reshape(ne, 1, d))


def _combine_kernel(y0, y1, y2, y3, w_ref, x1_ref, g2_ref, fg_ref, o_ref):
    w = w_ref[...]
    ys = [_unpack_bf16_pairs(y[...]) for y in (y0, y1, y2, y3)]
    moe = (w[:, 0:1] * ys[0] + w[:, 1:2] * ys[1]) + (w[:, 2:3] * ys[2] + w[:, 3:4] * ys[3])
    o_ref[...] = _rmsnorm(x1_ref[...] + g2_ref[0] * moe, fg_ref[...])


def _combine_call(y4, top_w, x1, g2, fg, tokens_per_batch, tm):
    n, d = x1.shape
    per = tokens_per_batch // tm
    nt = n // tm
    yspec = lambda k: pl.BlockSpec((tm, d // 2), lambda i: (k * nt + i, 0))
    return pl.pallas_call(
        _combine_kernel,
        grid=(nt,),
        in_specs=[yspec(0), yspec(1), yspec(2), yspec(3),
                  pl.BlockSpec((tm, LANES), lambda i: (i, 0)),
                  pl.BlockSpec((tm, d), lambda i: (i, 0)),
                  pl.BlockSpec((1, 1, d), lambda i: (i // per, 0, 0)),
                  pl.BlockSpec((1, d), lambda i: (0, 0))],
        out_specs=pl.BlockSpec((tm, d), lambda i: (i, 0)),
        out_shape=jax.ShapeDtypeStruct((n, d), F32),
        compiler_params=_cparams(("parallel",), 48),
        name="combine",
    )(y4, y4, y4, y4, top_w, x1, g2, fg)


def _slot_kernel(start_ref, idx_ref, rank_ref, o_ref):
    idx = idx_ref[...]
    acc = rank_ref[...]
    for e in range(N_EXPERTS):
        acc = acc + jnp.where(idx == e, start_ref[e], 0)
    o_ref[...] = acc


def _slot_call(pad_start, top_idx, rank):
    full = pl.BlockSpec(top_idx.shape, lambda i, s: (0, 0))
    return pl.pallas_call(
        _slot_kernel,
        grid_spec=pltpu.PrefetchScalarGridSpec(num_scalar_prefetch=1, grid=(1,), in_specs=[full, full],
                                               out_specs=full),
        out_shape=jax.ShapeDtypeStruct(top_idx.shape, I32),
        compiler_params=_cparams(("arbitrary",), 32),
        name="slots",
    )(pad_start, top_idx, rank)


def _routing_tables(top_idx, rank, counts, n):
    tmb = EXPERT_BLOCK
    counts = counts.astype(I32)
    padded = (counts + tmb - 1) // tmb * tmb
    pad_end = jnp.cumsum(padded)
    pad_start = pad_end - padded
    n_blocks = -(-(n * TOP_K + N_EXPERTS * (tmb - 1)) // tmb)
    n_slots = n_blocks * tmb
    dest_flat = _slot_call(pad_start, top_idx, rank)[:TOP_K].reshape(-1)
    block_start = jnp.arange(n_blocks, dtype=I32) * tmb
    block_e = jnp.minimum(jnp.sum((pad_end[None, :] <= block_start[:, None]).astype(I32), axis=1), N_EXPERTS - 1)
    n_valid = (pad_end[-1:] // tmb).astype(I32)
    experts = jnp.arange(N_EXPERTS, dtype=I32)
    used = counts > 0
    later = jnp.where(jnp.logical_and(used[None, :], experts[None, :] > experts[:, None]), experts[None, :], N_EXPERTS)
    next_e = jnp.min(later, axis=1)
    next_e = jnp.where(next_e == N_EXPERTS, -1, next_e).astype(I32)
    parity = ((jnp.cumsum(used.astype(I32)) - used.astype(I32)) % 2).astype(I32)
    row_end = jnp.sum(jnp.where(block_e[:, None] == experts[None, :], (pad_start + counts)[None, :], 0), axis=1)
    rows_used = jnp.clip(row_end - block_start, 0, tmb).astype(I32)
    return block_e, n_valid, next_e, parity, rows_used, dest_flat, n_slots


def _gdn_branch(x3, sh, sc, norm_g, w_qkv, w_gates, w_extra, extra_dtypes, conv_w, par, grid_w, use_rows, tm, s0):
    q, k, v, go, rows, *extra = _inconv_call(x3, sh, sc, norm_g, w_qkv, w_gates, w_extra, extra_dtypes, conv_w, par,
                                             grid_w, use_rows, tm)
    o_f, o_b, s_fin = _gdn_call(q, k, v, go, rows, s0)
    return extra, o_f, o_b, s_fin


def kernel(x, c, ctx, c_ctx, w_mod, b_mod, norm1_g, norm2_g, w_in, conv_w, a_log, dt_bias, gdn_norm_g,
           w_fourier_out, w_gdn_out, w_merge_out, w_router, b_router, w_gate, b_gate, w_up, b_up,
           w_down, b_down, final_norm_g):
    b, l, d = x.shape
    n = b * l
    n_ctx = ctx.shape[1]
    assert w_mod.shape[0] == 1 and l == GRID_W * GRID_W and d == V_DIM

    c8 = jnp.concatenate([c, c_ctx[None, :], jnp.zeros((8 - b - 1, d), F32)], axis=0)
    mod = _mod_call(c8, w_mod[0], b_mod[0])
    sh1, sc1, g1, sh2, sc2, g2 = [mod[:b, j * d:(j + 1) * d].reshape(b, 1, d) for j in range(6)]
    csh1 = jnp.broadcast_to(mod[b:b + 1, 0:d].reshape(1, 1, d), (b, 1, d))
    csc1 = jnp.broadcast_to(mod[b:b + 1, d:2 * d].reshape(1, 1, d), (b, 1, d))

    off_gate = QKV_DIM
    off_z = off_gate + 4 * NV_HEADS
    off_f = off_z + V_DIM
    off_ga = off_f + F_DIM
    w_qkv, w_gates, *w_extra = _wsplit_call(w_in, (0, off_gate, off_z, off_f, off_ga, w_in.shape[2]))
    par = jnp.pad(jnp.stack([a_log[0].reshape(-1), dt_bias[0].reshape(-1)]),
                  ((0, 6), (GATE_LANE0, LANES - 2 * GATE_LANE0)))
    n1 = norm1_g[0].reshape(1, d)
    cw = conv_w[0].reshape(9, QKV_DIM)

    zero_state = jnp.zeros((b, 2, NV_HEADS, HEAD_DIM, HEAD_DIM), F32)
    _, _, _, s_ctx = _gdn_branch(ctx, csh1, csc1, n1, w_qkv, w_gates, [], [], cw, par, n_ctx, False, n_ctx,
                                 zero_state)

    x2 = x.reshape(n, d)
    (z, f, gab), o_f, o_b, _ = _gdn_branch(x, sh1, sc1, n1, w_qkv, w_gates, w_extra, (BF16, F32, BF16),
                                           cw, par, GRID_W, True, TOKEN_TILE, s_ctx)
    z, gab = z.reshape(n, V_DIM), gab.reshape(n, 2 * d)
    fmix = _fnet_call(f).reshape(n, F_DIM)

    wr = jnp.pad(w_router[0], ((0, 0), (0, LANES - N_EXPERTS)))
    br = jnp.pad(b_router[0], (0, LANES - N_EXPERTS), constant_values=NEG_BIG).reshape(1, LANES)
    x1, h2, logits = _merge_call(
        o_f.reshape(n, V_DIM), o_b.reshape(n, V_DIM), z, fmix, gab, x2, g1, sh2, sc2,
        gdn_norm_g[0].reshape(1, HEAD_DIM), norm2_g[0].reshape(1, d),
        w_gdn_out[0].astype(BF16), w_fourier_out[0].astype(BF16), w_merge_out[0].astype(BF16), wr, br, l, TOKEN_TILE)

    top_idx, top_w, rank, counts = _route_call(logits, TOKEN_TILE)
    block_e, n_valid, next_e, parity, rows_used, dest_flat, n_slots = _routing_tables(
        top_idx, rank, counts[0, :N_EXPERTS], n)
    xs = _sc_scatter_rows(h2, dest_flat, n_slots)
    ys = _expert_call(block_e, n_valid, next_e, parity, rows_used, xs, w_gate[0], w_up[0], w_down[0],
                      b_gate[0], b_up[0], b_down[0])
    y4 = _sc_gather_rows(ys, dest_flat)
    out = _combine_call(y4, top_w, x1, g2, final_norm_g.reshape(1, d), l, TOKEN_TILE)
    return out.reshape(b, l, d)
```

```python
import functools
import math

import jax
import jax.numpy as jnp
import numpy as np
from jax import lax
from jax.experimental import pallas as pl
from jax.experimental.pallas import tpu as pltpu
from jax.experimental.pallas import tpu_sc as plsc

F32 = jnp.float32
BF16 = jnp.bfloat16
I32 = jnp.int32
HIGHEST = lax.Precision.HIGHEST

GRID_W = 64
NQK_HEADS = 4
NV_HEADS = 8
HEAD_DIM = 128
QK_DIM = NQK_HEADS * HEAD_DIM
V_DIM = NV_HEADS * HEAD_DIM
QKV_DIM = 2 * QK_DIM + V_DIM
F_GROUPS = 4
F_DIM = F_GROUPS * HEAD_DIM
N_EXPERTS = 32
TOP_K = 4
SWIGLU_ALPHA = 1.702
SWIGLU_LIMIT = 7.0
EPS = 1e-6

LANES = 128
SUBLANES = 8
GATE_LANE0 = 16
GDN_CHUNK = 128
EXPERT_BLOCK = 512
TOKEN_TILE = 512
SC_CORES = 2
SC_SUBCORES = 16
SC_WORKERS = SC_CORES * SC_SUBCORES
SC_WINDOW = 64
NEG_BIG = -1e30
MIB = 2 ** 20


def _cparams(sem, vmem_mib):
    return pltpu.CompilerParams(dimension_semantics=sem, vmem_limit_bytes=vmem_mib * MIB)


def _mm(a, b, prec=None, dims=(((1,), (0,)), ((), ()))):
    if prec is None:
        return lax.dot_general(a.astype(BF16), b.astype(BF16), dims, preferred_element_type=F32)
    return lax.dot_general(a.astype(F32), b.astype(F32), dims, precision=prec, preferred_element_type=F32)


def _sigmoid(x):
    return 0.5 * jnp.tanh(0.5 * x) + 0.5


def _rmsnorm(x, g):
    return x * lax.rsqrt(jnp.mean(x * x, axis=-1, keepdims=True) + EPS) * g


def _pack_bf16_pairs(x):
    half = x.shape[1] // 2
    bits = lax.bitcast_convert_type(x.astype(BF16).astype(F32), jnp.uint32)
    packed = jnp.bitwise_or(jnp.right_shift(bits[:, :half], jnp.uint32(16)),
                            jnp.bitwise_and(bits[:, half:], jnp.uint32(0xFFFF0000)))
    return lax.bitcast_convert_type(packed, I32)


def _unpack_bf16_pairs(p):
    bits = lax.bitcast_convert_type(p, jnp.uint32)
    lo = lax.bitcast_convert_type(jnp.left_shift(bits, jnp.uint32(16)), F32)
    hi = lax.bitcast_convert_type(jnp.bitwise_and(bits, jnp.uint32(0xFFFF0000)), F32)
    return jnp.concatenate([lo, hi], axis=1)


def _mod_kernel(c_ref, w_ref, b_ref, o_ref):
    c = c_ref[...]
    o_ref[...] = _mm(c * _sigmoid(c), w_ref[...], HIGHEST) + b_ref[...]


def _mod_call(c8, w_mod, b_mod):
    d, n = w_mod.shape
    tn = 1536
    return pl.pallas_call(
        _mod_kernel,
        grid=(n // tn,),
        in_specs=[pl.BlockSpec((8, d), lambda j: (0, 0)),
                  pl.BlockSpec((d, tn), lambda j: (0, j)),
                  pl.BlockSpec((1, tn), lambda j: (0, j))],
        out_specs=pl.BlockSpec((8, tn), lambda j: (0, j)),
        out_shape=jax.ShapeDtypeStruct((8, n), F32),
        compiler_params=_cparams(("parallel",), 32),
        name="mod",
    )(c8, w_mod, b_mod.reshape(1, n))


def _wsplit_kernel(w_ref, *o_refs, bounds):
    for o_ref, lo, hi in zip(o_refs, bounds[:-1], bounds[1:]):
        o_ref[:, :hi - lo] = w_ref[0, :, lo:hi].astype(o_ref.dtype)
        if o_ref.shape[1] > hi - lo:
            o_ref[:, hi - lo:] = jnp.zeros((o_ref.shape[0], o_ref.shape[1] - (hi - lo)), o_ref.dtype)


def _wsplit_call(w, bounds):
    d = w.shape[1]
    tr = 256
    widths = [-(-(hi - lo) // LANES) * LANES for lo, hi in zip(bounds[:-1], bounds[1:])]
    return pl.pallas_call(
        functools.partial(_wsplit_kernel, bounds=bounds),
        grid=(d // tr,),
        in_specs=[pl.BlockSpec((1, tr, w.shape[2]), lambda i: (0, i, 0))],
        out_specs=[pl.BlockSpec((tr, wd), lambda i: (i, 0)) for wd in widths],
        out_shape=[jax.ShapeDtypeStruct((d, wd), BF16) for wd in widths],
        compiler_params=_cparams(("parallel",), 32),
        name="wsplit",
    )(w)


def _inconv_kernel(*refs, grid_w, use_rows, tm, cw, n_extra):
    refs = list(refs)
    prev_ref = refs.pop(0) if use_rows else None
    x_ref = refs.pop(0)
    next_ref = refs.pop(0) if use_rows else None
    sh_ref, sc_ref, g_ref, wq_ref, wgate_ref = refs[:5]
    wx_refs = refs[5:5 + n_extra]
    w_ref, par_ref, q_ref, k_ref, v_ref, go_ref, gr_ref = refs[5 + n_extra:12 + n_extra]
    ox_refs = refs[12 + n_extra:]

    def modulated(x):
        return (_rmsnorm(x, g_ref[...]) * (1.0 + sc_ref[0]) + sh_ref[0]).astype(BF16)

    r = pl.program_id(1)
    nr = pl.num_programs(1)
    u = modulated(x_ref[0])

    def project(wx_ref, o_ref, c0, step):
        o_ref[0, :, c0:c0 + step] = _mm(u, wx_ref[:, c0:c0 + step]).astype(o_ref.dtype)

    plain = [functools.partial(project, wx_ref, o_ref, c0, min(wx_ref.shape[1], 512))
             for wx_ref, o_ref in zip(wx_refs, ox_refs) for c0 in range(0, wx_ref.shape[1], min(wx_ref.shape[1], 512))]
    n_conv_chunks = QKV_DIM // cw

    t = lax.broadcasted_iota(I32, (tm, 1), 0)
    col = jnp.bitwise_and(t, grid_w - 1)
    m_left = (col != 0).astype(F32)
    m_right = (col != grid_w - 1).astype(F32)
    if use_rows:
        u_prev = modulated(prev_ref[0])
        u_next = modulated(next_ref[0])
        has_prev = (r > 0).astype(F32)
        has_next = (r < nr - 1).astype(F32)
    for c0 in range(0, QKV_DIM, cw):
        wq = wq_ref[:, c0:c0 + cw]
        xm = _mm(u, wq)
        if use_rows:
            up = jnp.concatenate([_mm(u_prev, wq) * has_prev, xm[:tm - grid_w]], axis=0)
            dn = jnp.concatenate([xm[grid_w:], _mm(u_next, wq) * has_next], axis=0)

        def colsum(kc):
            y = xm * w_ref[3 + kc:4 + kc, c0:c0 + cw]
            if use_rows:
                y = y + up * w_ref[kc:kc + 1, c0:c0 + cw] + dn * w_ref[6 + kc:7 + kc, c0:c0 + cw]
            return y

        acc = (colsum(1) + pltpu.roll(colsum(0), 1, axis=0) * m_left
               + pltpu.roll(colsum(2), tm - 1, axis=0) * m_right)
        s = acc * _sigmoid(acc)
        for h0 in range(0, cw, HEAD_DIM):
            c = c0 + h0
            seg = s[:, h0:h0 + HEAD_DIM]
            if c < 2 * QK_DIM:
                seg = seg * lax.rsqrt(jnp.sum(seg * seg, axis=-1, keepdims=True) + EPS)
            if c < QK_DIM:
                q_ref[0, :, c:c + HEAD_DIM] = (seg * HEAD_DIM ** -0.5).astype(q_ref.dtype)
            elif c < 2 * QK_DIM:
                k_ref[0, :, c - QK_DIM:c - QK_DIM + HEAD_DIM] = seg.astype(k_ref.dtype)
            else:
                v_ref[0, :, c - 2 * QK_DIM:c - 2 * QK_DIM + HEAD_DIM] = seg.astype(v_ref.dtype)
        ci = c0 // cw
        for job in plain[ci * len(plain) // n_conv_chunks:(ci + 1) * len(plain) // n_conv_chunks]:
            job()
    g = _mm(u, wgate_ref[...])
    a = g + par_ref[1:2, :]
    softplus = jnp.maximum(a, 0.0) + jnp.log1p(jnp.exp(-jnp.abs(a)))
    log_g = -jnp.exp(par_ref[0:1, :]) * softplus
    lane = lax.broadcasted_iota(I32, g.shape, 1)
    go_ref[0] = jnp.where(lane < GATE_LANE0, _sigmoid(g), log_g)
    gr_ref[0] = jnp.transpose(log_g)[GATE_LANE0:GATE_LANE0 + 2 * NV_HEADS]


def _inconv_call(x3, sh, sc, g, w_qkv, w_gates, w_extra, extra_dtypes, conv_w, par, grid_w, use_rows, tm):
    b, t, d = x3.shape
    kern = functools.partial(_inconv_kernel, grid_w=grid_w, use_rows=use_rows, tm=tm, cw=512, n_extra=len(w_extra))
    per = tm // grid_w
    nrow = t // grid_w
    tile = lambda wd: pl.BlockSpec((1, tm, wd), lambda i, r: (i, r, 0))
    vec = pl.BlockSpec((1, 1, d), lambda i, r: (i, 0, 0))
    const = lambda a: pl.BlockSpec(a.shape, lambda i, r: (0,) * a.ndim, pipeline_mode=pl.Buffered(1))
    in_specs, args = [], []
    if use_rows:
        in_specs.append(pl.BlockSpec((1, grid_w, d), lambda i, r: (i, jnp.maximum(r * per - 1, 0), 0)))
        args.append(x3)
    in_specs.append(tile(d))
    args.append(x3)
    if use_rows:
        in_specs.append(pl.BlockSpec((1, grid_w, d), lambda i, r: (i, jnp.minimum((r + 1) * per, nrow - 1), 0)))
        args.append(x3)
    consts = [g, w_qkv, w_gates, *w_extra, conv_w, par]
    in_specs += [vec, vec] + [const(a) for a in consts]
    args += [sh, sc] + consts
    widths = [QK_DIM, QK_DIM, V_DIM, LANES] + [w.shape[1] for w in w_extra]
    dtypes = [BF16, BF16, BF16, F32] + list(extra_dtypes)
    out_shape = [jax.ShapeDtypeStruct((b, t, wd), dt) for wd, dt in zip(widths, dtypes)]
    out_specs = [tile(wd) for wd in widths]
    out_shape.insert(4, jax.ShapeDtypeStruct((b, 2 * NV_HEADS, t), F32))
    out_specs.insert(4, pl.BlockSpec((1, 2 * NV_HEADS, tm), lambda i, r: (i, 0, r)))
    return pl.pallas_call(
        kern, grid=(b, t // tm), in_specs=in_specs, out_specs=out_specs, out_shape=out_shape,
        compiler_params=_cparams(("parallel", "parallel"), 56),
        name="inconv_rows" if use_rows else "inconv_seq",
    )(*args)


def _gdn_kernel(qf, kf, vf, gf, rf, qb, kb, vb, gb, rb, s0_ref, of, ob, sfin_ref, s_ref):
    i = pl.program_id(1)
    nc = pl.num_programs(1)

    @pl.when(i == 0)
    def _():
        s_ref[...] = s0_ref[0]

    c = qf.shape[1]
    per = NV_HEADS // NQK_HEADS
    row = lax.broadcasted_iota(I32, (c, c), 0)
    colj = lax.broadcasted_iota(I32, (c, c), 1)
    eye = jnp.where(row == colj, 1.0, 0.0)
    nt_dims = (((1,), (1,)), ((), ()))
    tn_dims = (((0,), (0,)), ((), ()))

    seqs = []
    for d, (q_r, k_r, v_r, g_r, r_r, o_r) in enumerate(((qf, kf, vf, gf, rf, of), (qb, kb, vb, gb, rb, ob))):
        rev = d == 1
        incl = (colj >= row) if rev else (colj <= row)
        strict = (colj > row) if rev else (colj < row)
        gates = g_r[0]
        tri_c = jnp.where(incl, 1.0, 0.0)
        gcm = _mm(tri_c, gates, HIGHEST)
        gcr = _mm(r_r[0], tri_c, HIGHEST, dims=nt_dims)
        for hq in range(NQK_HEADS):
            q = q_r[0, :, hq * HEAD_DIM:(hq + 1) * HEAD_DIM]
            k = k_r[0, :, hq * HEAD_DIM:(hq + 1) * HEAD_DIM]
            kq = lax.dot_general(jnp.concatenate([q, k], axis=0), k, nt_dims, preferred_element_type=F32)
            for j in range(per):
                h = hq * per + j
                idx = d * NV_HEADS + h
                gc_c = gcm[:, GATE_LANE0 + idx:GATE_LANE0 + idx + 1]
                seqs.append(dict(d=d, h=h, o_r=o_r, v_r=v_r, q=q, k=k, qk=kq[:c], kk=kq[c:], incl=incl, strict=strict,
                                 beta=gates[:, idx:idx + 1], gc_c=gc_c, gc_r=gcr[idx:idx + 1, :],
                                 ge=gc_c[0:1] if rev else gc_c[c - 1:c]))

    def same_block(m):
        sh = int(math.log2(m))
        return jnp.right_shift(row, sh) == jnp.right_shift(colj, sh)

    for s in seqs:
        s['decay'] = jnp.where(s['incl'], jnp.exp(jnp.where(s['incl'], s['gc_c'] - s['gc_r'], 0.0)), 0.0)
        s['a'] = jnp.where(s['strict'], s['beta'] * s['kk'] * s['decay'], 0.0)
        s['t'] = eye - jnp.where(same_block(2), s['a'], 0.0)
    m = 4
    while m <= c:
        between = jnp.logical_and(same_block(m), jnp.logical_not(same_block(m // 2)))
        for s in seqs:
            s['te'] = _mm(s['t'], jnp.where(between, s['a'], 0.0))
        for s in seqs:
            s['t'] = s['t'] - _mm(s['te'], s['t'])
        m *= 2
    for s in seqs:
        h = s['h']
        egc = jnp.exp(s['gc_c'])
        kf32 = s['k'].astype(F32)
        v = s['v_r'][0, :, h * HEAD_DIM:(h + 1) * HEAD_DIM].astype(F32)
        rhs = jnp.concatenate([s['beta'] * v, (s['beta'] * egc) * kf32], axis=1)
        s['sol'] = _mm(s['t'], rhs)
        s['q_dec'] = s['q'].astype(F32) * egc
        s['k_dec'] = kf32 * jnp.exp(s['ge'] - s['gc_c'])
    for s in seqs:
        s['ws'] = _mm(jnp.concatenate([s['sol'][:, HEAD_DIM:], s['q_dec']], axis=0), s_ref[s['d'], s['h']])
    for s in seqs:
        s['u'] = s['sol'][:, :HEAD_DIM] - s['ws'][:c]
        s_ref[s['d'], s['h']] = (jnp.exp(s['ge']) * s_ref[s['d'], s['h']]
                                 + _mm(s['k_dec'], s['u'], dims=tn_dims))
    for s in seqs:
        h = s['h']
        o = s['ws'][c:] + _mm(s['qk'] * s['decay'], s['u'])
        s['o_r'][0, :, h * HEAD_DIM:(h + 1) * HEAD_DIM] = o.astype(s['o_r'].dtype)

    @pl.when(i == nc - 1)
    def _():
        sfin_ref[0] = s_ref[...]


def _gdn_call(q, k, v, go, rows, s0):
    b, t, _ = q.shape
    c = GDN_CHUNK
    nc = t // c
    fwd = lambda i, n: (i, n, 0)
    bwd = lambda i, n: (i, nc - 1 - n, 0)
    rfwd = lambda i, n: (i, 0, n)
    rbwd = lambda i, n: (i, 0, nc - 1 - n)
    state_spec = pl.BlockSpec((1,) + s0.shape[1:], lambda i, n: (i, 0, 0, 0, 0))

    def specs(m3, mr):
        return [pl.BlockSpec((1, c, QK_DIM), m3), pl.BlockSpec((1, c, QK_DIM), m3),
                pl.BlockSpec((1, c, V_DIM), m3), pl.BlockSpec((1, c, LANES), m3),
                pl.BlockSpec((1, 2 * NV_HEADS, c), mr)]

    return pl.pallas_call(
        _gdn_kernel,
        grid=(b, nc),
        in_specs=specs(fwd, rfwd) + specs(bwd, rbwd) + [state_spec],
        out_specs=[pl.BlockSpec((1, c, V_DIM), fwd), pl.BlockSpec((1, c, V_DIM), bwd), state_spec],
        out_shape=[jax.ShapeDtypeStruct((b, t, V_DIM), BF16), jax.ShapeDtypeStruct((b, t, V_DIM), BF16),
                   jax.ShapeDtypeStruct(s0.shape, F32)],
        scratch_shapes=[pltpu.VMEM(s0.shape[1:], F32)],
        compiler_params=_cparams(("parallel", "arbitrary"), 48),
        name="gdn",
    )(q, k, v, go, rows, q, k, v, go, rows, s0)


def _fnet1_kernel(x_ref, f_ref, ar_ref, ai_ref):
    n = x_ref.shape[1]
    for j in range(x_ref.shape[2]):
        a = _mm(f_ref[...], x_ref[0, :, j, :])
        ar_ref[0, j] = a[:n]
        ai_ref[0, j] = a[n:]


def _fnet2_kernel(ar_ref, ai_ref, g_ref, wc_ref, o_ref, *, scale):
    n = ar_ref.shape[1]
    cols = ar_ref.shape[2]
    zs = []
    for m in range(cols):
        a2 = jnp.concatenate([ar_ref[0, :, m, :], ai_ref[0, :, m, :]], axis=0)
        zs.append(_mm(g_ref[m], a2))
    zr = jnp.concatenate([z[:n] for z in zs], axis=0)
    zi = jnp.concatenate([z[n:] for z in zs], axis=0)
    for g0 in range(0, zr.shape[1], HEAD_DIM):
        y = _mm(jnp.concatenate([zr[:, g0:g0 + HEAD_DIM], zi[:, g0:g0 + HEAD_DIM]], axis=1), wc_ref[...])
        for m in range(cols):
            o_ref[0, :, m, g0:g0 + HEAD_DIM] = y[m * n:(m + 1) * n] * scale


def _fnet_tables(n):
    a = np.arange(n)
    ang1 = 2.0 * np.pi * np.outer(a, a) / n
    f1 = np.concatenate([np.cos(ang1), -np.sin(ang1)], axis=0)
    m = a[:, None] + n * a[None, :]
    ang2 = 2.0 * np.pi * ((m[:, :, None] * a[None, None, :]) % (n * n)) / (n * n)
    gc, gs = np.cos(ang2), np.sin(ang2)
    g2 = np.concatenate([np.concatenate([gc, gs], axis=2), np.concatenate([-gs, gc], axis=2)], axis=1)
    angc = 2.0 * np.pi * np.outer(np.arange(HEAD_DIM), np.arange(HEAD_DIM)) / HEAD_DIM
    wc = np.concatenate([np.cos(angc), np.sin(angc)], axis=0)
    f = lambda x: jnp.asarray(x, F32).astype(BF16)
    return f(f1), f(g2), f(wc)


def _fnet_call(f):
    b, l, c = f.shape
    n = GRID_W
    assert l == n * n
    f1, g2, wc = _fnet_tables(n)
    cols = 4 * SUBLANES
    ar, ai = pl.pallas_call(
        _fnet1_kernel,
        grid=(b, n // cols),
        in_specs=[pl.BlockSpec((1, n, cols, c), lambda i, j: (i, 0, j, 0)),
                  pl.BlockSpec((2 * n, n), lambda i, j: (0, 0))],
        out_specs=[pl.BlockSpec((1, cols, n, c), lambda i, j: (i, j, 0, 0))] * 2,
        out_shape=[jax.ShapeDtypeStruct((b, n, n, c), F32)] * 2,
        compiler_params=_cparams(("parallel", "parallel"), 32),
        name="fnet1",
    )(f.reshape(b, n, n, c), f1)
    out = pl.pallas_call(
        functools.partial(_fnet2_kernel, scale=1.0 / math.sqrt(l * HEAD_DIM)),
        grid=(b, n // cols),
        in_specs=[pl.BlockSpec((1, n, cols, c), lambda i, j: (i, 0, j, 0)),
                  pl.BlockSpec((1, n, cols, c), lambda i, j: (i, 0, j, 0)),
                  pl.BlockSpec((cols, 2 * n, 2 * n), lambda i, j: (j, 0, 0)),
                  pl.BlockSpec((2 * HEAD_DIM, HEAD_DIM), lambda i, j: (0, 0))],
        out_specs=pl.BlockSpec((1, n, cols, c), lambda i, j: (i, 0, j, 0)),
        out_shape=jax.ShapeDtypeStruct((b, n, n, c), F32),
        compiler_params=_cparams(("parallel", "parallel"), 32),
        name="fnet2",
    )(ar, ai, g2, wc)
    return out.reshape(b, l, c)


def _merge_kernel(of_ref, ob_ref, z_ref, fm_ref, gab_ref, x_ref, g1_ref, sh2_ref, sc2_ref, gn_ref, n2_ref,
                  wg_ref, wf_ref, wm_ref, wr_ref, br_ref, x1_ref, h2_ref, lg_ref):
    d = x_ref.shape[1]
    tm = x_ref.shape[0]
    halves = [slice(0, tm // 2), slice(tm // 2, tm)]
    yb_in = []
    for r in halves:
        o = of_ref[r, :].astype(F32) + ob_ref[r, :].astype(F32)
        z = z_ref[r, :].astype(F32)
        parts = []
        for h0 in range(0, V_DIM, HEAD_DIM):
            oh = o[:, h0:h0 + HEAD_DIM]
            parts.append(oh * lax.rsqrt(jnp.mean(oh * oh, axis=-1, keepdims=True) + EPS) * gn_ref[...])
        yb_in.append(jnp.concatenate(parts, axis=1) * (z * _sigmoid(z)))
    yb = [_mm(v, wg_ref[...]) for v in yb_in]
    ya = [_mm(fm_ref[r, :], wf_ref[...]) for r in halves]
    mixed = [_sigmoid(gab_ref[r, :d].astype(F32)) * a + _sigmoid(gab_ref[r, d:].astype(F32)) * b_
             for r, a, b_ in zip(halves, ya, yb)]
    mm = [_mm(v, wm_ref[...]) for v in mixed]
    w = wr_ref[...]
    w_hi = w.astype(BF16)
    w2 = jnp.concatenate([w_hi, (w - w_hi.astype(F32)).astype(BF16)], axis=1)
    for r, v in zip(halves, mm):
        x1 = x_ref[r, :] + g1_ref[0] * v
        x1_ref[r, :] = x1
        h2 = _rmsnorm(x1, n2_ref[...]) * (1.0 + sc2_ref[0]) + sh2_ref[0]
        h2_ref[r, :] = _pack_bf16_pairs(h2)
        h_hi = h2.astype(BF16)
        h_lo = (h2 - h_hi.astype(F32)).astype(BF16)
        part = _mm(h_hi, w2)
        lg_ref[r, :] = (part[:, :LANES] + part[:, LANES:]) + _mm(h_lo, w_hi) + br_ref[...]


def _merge_call(of, ob, z, fm, gab, x2, g1, sh2, sc2, gn, n2, wg, wf, wm, wr, br, tokens_per_batch, tm):
    n, d = x2.shape
    per = tokens_per_batch // tm
    tok = lambda wd: pl.BlockSpec((tm, wd), lambda i: (i, 0))
    vec = pl.BlockSpec((1, 1, d), lambda i: (i // per, 0, 0))
    full = lambda a: pl.BlockSpec(a.shape, lambda i: (0,) * a.ndim)
    return pl.pallas_call(
        _merge_kernel,
        grid=(n // tm,),
        in_specs=[tok(V_DIM), tok(V_DIM), tok(V_DIM), tok(F_DIM), tok(2 * d), tok(d), vec, vec, vec,
                  full(gn), full(n2), full(wg), full(wf), full(wm), full(wr), full(br)],
        out_specs=[tok(d), tok(d // 2), tok(LANES)],
        out_shape=[jax.ShapeDtypeStruct((n, d), F32), jax.ShapeDtypeStruct((n, d // 2), I32),
                   jax.ShapeDtypeStruct((n, LANES), F32)],
        compiler_params=_cparams(("parallel",), 56),
        name="merge",
    )(of, ob, z, fm, gab, x2, g1, sh2, sc2, gn, n2, wg, wf, wm, wr, br)


def _route_kernel(lg_ref, idx_ref, w_ref, rank_ref, cnt_ref, run_ref):
    i = pl.program_id(0)

    @pl.when(i == 0)
    def _():
        run_ref[...] = jnp.zeros_like(run_ref)

    l = lg_ref[...]
    tm = l.shape[0]
    lane = lax.broadcasted_iota(I32, l.shape, 1).astype(F32)
    vals, idxs = [], []
    for _ in range(TOP_K):
        m = jnp.max(l, axis=-1, keepdims=True)
        idx = jnp.min(jnp.where(l == m, lane, float(LANES)), axis=-1, keepdims=True)
        vals.append(m)
        idxs.append(idx)
        l = jnp.where(lane == idx, NEG_BIG * 2.0, l)
    es = [jnp.exp(v - vals[0]) for v in vals]
    inv = 1.0 / (es[0] + es[1] + es[2] + es[3])
    picked = jnp.zeros(l.shape, F32)
    for idx in idxs:
        picked = picked + (lane == idx).astype(F32)
    r = lax.broadcasted_iota(I32, (tm, tm), 0)
    cidx = lax.broadcasted_iota(I32, (tm, tm), 1)
    before = _mm(jnp.where(cidx < r, 1.0, 0.0), picked) + run_ref[...]
    idx_out = jnp.zeros(l.shape, F32)
    w_out = jnp.zeros(l.shape, F32)
    rank_out = jnp.zeros(l.shape, F32)
    for k in range(TOP_K):
        rk = jnp.sum(jnp.where(lane == idxs[k], before, 0.0), axis=-1, keepdims=True)
        idx_out = jnp.where(lane == k, idxs[k], idx_out)
        w_out = jnp.where(lane == k, es[k] * inv, w_out)
        rank_out = jnp.where(lane == k, rk, rank_out)
    idx_ref[...] = jnp.transpose(idx_out)[:SUBLANES].astype(I32)
    w_ref[...] = w_out
    rank_ref[...] = jnp.transpose(rank_out)[:SUBLANES].astype(I32)
    run_ref[...] = run_ref[...] + jnp.sum(picked, axis=0, keepdims=True)
    cnt_ref[...] = run_ref[...]


def _route_call(logits, tm):
    n = logits.shape[0]
    tok = pl.BlockSpec((tm, LANES), lambda i: (i, 0))
    tok_t = pl.BlockSpec((SUBLANES, tm), lambda i: (0, i))
    return pl.pallas_call(
        _route_kernel,
        grid=(n // tm,),
        in_specs=[tok],
        out_specs=[tok_t, tok, tok_t, pl.BlockSpec((1, LANES), lambda i: (0, 0))],
        out_shape=[jax.ShapeDtypeStruct((SUBLANES, n), I32), jax.ShapeDtypeStruct((n, LANES), F32),
                   jax.ShapeDtypeStruct((SUBLANES, n), I32), jax.ShapeDtypeStruct((1, LANES), F32)],
        scratch_shapes=[pltpu.VMEM((1, LANES), F32)],
        compiler_params=_cparams(("arbitrary",), 32),
        name="route",
    )(logits)


def _sc_mesh():
    return plsc.VectorSubcoreMesh(core_axis_name="c", subcore_axis_name="s",
                                  num_cores=SC_CORES, num_subcores=SC_SUBCORES)


def _sc_worker_id():
    return lax.axis_index("s") * SC_CORES + lax.axis_index("c")


def _sc_scatter_rows(x, idx, n_out):
    n, d = x.shape
    per_w = idx.shape[0] // SC_WORKERS
    nwin = per_w // SC_WINDOW
    assert per_w * SC_WORKERS == idx.shape[0] and nwin * SC_WINDOW == per_w and nwin % 2 == 0 and n % per_w == 0

    def body(x_hbm, idx_hbm, out_hbm, idx_v, rows_v, sem_r, sem_w):
        wid = _sc_worker_id()
        row0 = lax.rem(wid * per_w, n)
        pltpu.sync_copy(idx_hbm.at[wid], idx_v)

        def read(j, b):
            return pltpu.make_async_copy(x_hbm.at[pl.ds(row0 + j * SC_WINDOW, SC_WINDOW)], rows_v.at[b], sem_r.at[b])

        def write(j, b):
            return pltpu.make_async_copy(rows_v.at[b], out_hbm.at[idx_v.at[j]], sem_w.at[b])

        @pl.loop(0, nwin, step=2)
        def _(j):
            read(j, 0).start()
            read(j + 1, 1).start()
            read(j, 0).wait()
            write(j, 0).start()
            read(j + 1, 1).wait()
            write(j + 1, 1).start()
            write(j, 0).wait()
            write(j + 1, 1).wait()

    return pl.kernel(
        body, out_type=jax.ShapeDtypeStruct((n_out, d), x.dtype), mesh=_sc_mesh(),
        scratch_types=[pltpu.VMEM((nwin, SC_WINDOW), I32), pltpu.VMEM((2, SC_WINDOW, d), x.dtype),
                       pltpu.SemaphoreType.DMA((2,)), pltpu.SemaphoreType.DMA((2,))],
        name="sc_scatter_rows",
    )(x, idx.reshape(SC_WORKERS, nwin, SC_WINDOW))


def _sc_gather_rows(y, idx):
    d = y.shape[1]
    total = idx.shape[0]
    per_w = total // SC_WORKERS
    nwin = per_w // SC_WINDOW
    assert per_w * SC_WORKERS == total and nwin * SC_WINDOW == per_w and nwin % 2 == 0

    def body(y_hbm, idx_hbm, out_hbm, idx_v, rows_v, sem_r, sem_w):
        wid = _sc_worker_id()
        row0 = wid * per_w
        pltpu.sync_copy(idx_hbm.at[wid], idx_v)

        def read(j, b):
            return pltpu.make_async_copy(y_hbm.at[idx_v.at[j]], rows_v.at[b], sem_r.at[b])

        def write(j, b):
            return pltpu.make_async_copy(rows_v.at[b], out_hbm.at[pl.ds(row0 + j * SC_WINDOW, SC_WINDOW)], sem_w.at[b])

        @pl.loop(0, nwin, step=2)
        def _(j):
            read(j, 0).start()
            read(j + 1, 1).start()
            read(j, 0).wait()
            write(j, 0).start()
            read(j + 1, 1).wait()
            write(j + 1, 1).start()
            write(j, 0).wait()
            write(j + 1, 1).wait()

    return pl.kernel(
        body, out_type=jax.ShapeDtypeStruct((total, d), y.dtype), mesh=_sc_mesh(),
        scratch_types=[pltpu.VMEM((nwin, SC_WINDOW), I32), pltpu.VMEM((2, SC_WINDOW, d), y.dtype),
                       pltpu.SemaphoreType.DMA((2,)), pltpu.SemaphoreType.DMA((2,))],
        name="sc_gather_rows",
    )(y, idx.reshape(SC_WORKERS, nwin, SC_WINDOW))


def _expert_kernel(be_ref, nv_ref, nxt_ref, par_ref, used_ref, x_ref, wg_hbm, wu_hbm, wd_hbm, bg_ref, bu_ref, bd_ref,
                   y_ref, wgf, wuf, wdf, wgb, wub, wdb, sem):
    i = pl.program_id(0)

    def fetch(e, s):
        return [pltpu.make_async_copy(w_hbm.at[e], w_f.at[s], sem.at[s])
                for w_hbm, w_f in ((wg_hbm, wgf), (wu_hbm, wuf), (wd_hbm, wdf))]

    @pl.when(i < nv_ref[0])
    def _():
        e = be_ref[i]
        s = par_ref[e]

        @pl.when(i == 0)
        def _():
            for cp in fetch(e, s):
                cp.start()

        @pl.when(jnp.logical_or(i == 0, be_ref[jnp.maximum(i - 1, 0)] != e))
        def _():
            for cp in fetch(e, s):
                cp.wait()
            wgb[...] = wgf[s].astype(BF16)
            wub[...] = wuf[s].astype(BF16)
            wdb[...] = wdf[s].astype(BF16)

            @pl.when(nxt_ref[e] >= 0)
            def _():
                for cp in fetch(nxt_ref[e], 1 - s):
                    cp.start()

        def ffn(nrows):
            x = _unpack_bf16_pairs(x_ref[:nrows, :]).astype(BF16)
            gate = jnp.minimum(_mm(x, wgb[...]) + bg_ref[0], SWIGLU_LIMIT)
            up = jnp.clip(_mm(x, wub[...]) + bu_ref[0], -SWIGLU_LIMIT, SWIGLU_LIMIT)
            act = (up + 1.0) * gate * _sigmoid(SWIGLU_ALPHA * gate)
            y_ref[:nrows, :] = _pack_bf16_pairs(_mm(act, wdb[...]) + bd_ref[0])

        half = x_ref.shape[0] // 2

        @pl.when(used_ref[i] > half)
        def _():
            ffn(x_ref.shape[0])

        @pl.when(used_ref[i] <= half)
        def _():
            ffn(half)


def _expert_call(block_e, n_valid, next_e, parity, used, xs, w_gate, w_up, w_down, b_gate, b_up, b_down):
    nb = block_e.shape[0]
    tmb = EXPERT_BLOCK
    ne, d, de = w_gate.shape
    bspec = lambda s: pl.BlockSpec((1,) + s, lambda i, be, nv, nx, pa, us: (be[i], 0, 0))
    anyspec = pl.BlockSpec(memory_space=pl.ANY)
    grid_spec = pltpu.PrefetchScalarGridSpec(
        num_scalar_prefetch=5,
        grid=(nb,),
        in_specs=[pl.BlockSpec((tmb, d // 2), lambda i, be, nv, nx, pa, us: (i, 0)),
                  anyspec, anyspec, anyspec, bspec((1, de)), bspec((1, de)), bspec((1, d))],
        out_specs=pl.BlockSpec((tmb, d // 2), lambda i, be, nv, nx, pa, us: (i, 0)),
        scratch_shapes=[pltpu.VMEM((2, d, de), F32), pltpu.VMEM((2, d, de), F32), pltpu.VMEM((2, de, d), F32),
                        pltpu.VMEM((d, de), BF16), pltpu.VMEM((d, de), BF16), pltpu.VMEM((de, d), BF16),
                        pltpu.SemaphoreType.DMA((2,))],
    )
    return pl.pallas_call(
        _expert_kernel,
        grid_spec=grid_spec,
        out_shape=jax.ShapeDtypeStruct(xs.shape, xs.dtype),
        compiler_params=_cparams(("arbitrary",), 56),
        name="expert",
    )(block_e, n_valid, next_e, parity, used, xs, w_gate, w_up, w_down,
      b_gate.reshape(ne, 1, de), b_up.reshape(ne, 1, de), b_down.reshape(ne, 1, d))


def _combine_kernel(y0, y1, y2, y3, w_ref, x1_ref, g2_ref, fg_ref, o_ref):
    w = w_ref[...]
    ys = [_unpack_bf16_pairs(y[...]) for y in (y0, y1, y2, y3)]
    moe = (w[:, 0:1] * ys[0] + w[:, 1:2] * ys[1]) + (w[:, 2:3] * ys[2] + w[:, 3:4] * ys[3])
    o_ref[...] = _rmsnorm(x1_ref[...] + g2_ref[0] * moe, fg_ref[...])


def _combine_call(y4, top_w, x1, g2, fg, tokens_per_batch, tm):
    n, d = x1.shape
    per = tokens_per_batch // tm
    nt = n // tm
    yspec = lambda k: pl.BlockSpec((tm, d // 2), lambda i: (k * nt + i, 0))
    return pl.pallas_call(
        _combine_kernel,
        grid=(nt,),
        in_specs=[yspec(0), yspec(1), yspec(2), yspec(3),
                  pl.BlockSpec((tm, LANES), lambda i: (i, 0)),
                  pl.BlockSpec((tm, d), lambda i: (i, 0)),
                  pl.BlockSpec((1, 1, d), lambda i: (i // per, 0, 0)),
                  pl.BlockSpec((1, d), lambda i: (0, 0))],
        out_specs=pl.BlockSpec((tm, d), lambda i: (i, 0)),
        out_shape=jax.ShapeDtypeStruct((n, d), F32),
        compiler_params=_cparams(("parallel",), 48),
        name="combine",
    )(y4, y4, y4, y4, top_w, x1, g2, fg)


def _slot_kernel(start_ref, idx_ref, rank_ref, o_ref):
    idx = idx_ref[...]
    acc = rank_ref[...]
    for e in range(N_EXPERTS):
        acc = acc + jnp.where(idx == e, start_ref[e], 0)
    o_ref[...] = acc


def _slot_call(pad_start, top_idx, rank):
    full = pl.BlockSpec(top_idx.shape, lambda i, s: (0, 0))
    return pl.pallas_call(
        _slot_kernel,
        grid_spec=pltpu.PrefetchScalarGridSpec(num_scalar_prefetch=1, grid=(1,), in_specs=[full, full],
                                               out_specs=full),
        out_shape=jax.ShapeDtypeStruct(top_idx.shape, I32),
        compiler_params=_cparams(("arbitrary",), 32),
        name="slots",
    )(pad_start, top_idx, rank)


def _routing_tables(top_idx, rank, counts, n):
    tmb = EXPERT_BLOCK
    counts = counts.astype(I32)
    padded = (counts + tmb - 1) // tmb * tmb
    pad_end = jnp.cumsum(padded)
    pad_start = pad_end - padded
    n_blocks = -(-(n * TOP_K + N_EXPERTS * (tmb - 1)) // tmb)
    n_slots = n_blocks * tmb
    dest_flat = _slot_call(pad_start, top_idx, rank)[:TOP_K].reshape(-1)
    block_start = jnp.arange(n_blocks, dtype=I32) * tmb
    block_e = jnp.minimum(jnp.sum((pad_end[None, :] <= block_start[:, None]).astype(I32), axis=1), N_EXPERTS - 1)
    n_valid = (pad_end[-1:] // tmb).astype(I32)
    experts = jnp.arange(N_EXPERTS, dtype=I32)
    used = counts > 0
    later = jnp.where(jnp.logical_and(used[None, :], experts[None, :] > experts[:, None]), experts[None, :], N_EXPERTS)
    next_e = jnp.min(later, axis=1)
    next_e = jnp.where(next_e == N_EXPERTS, -1, next_e).astype(I32)
    parity = ((jnp.cumsum(used.astype(I32)) - used.astype(I32)) % 2).astype(I32)
    row_end = jnp.sum(jnp.where(block_e[:, None] == experts[None, :], (pad_start + counts)[None, :], 0), axis=1)
    rows_used = jnp.clip(row_end - block_start, 0, tmb).astype(I32)
    return block_e, n_valid, next_e, parity, rows_used, dest_flat, n_slots


def _gdn_branch(x3, sh, sc, norm_g, w_qkv, w_gates, w_extra, extra_dtypes, conv_w, par, grid_w, use_rows, tm, s0):
    q, k, v, go, rows, *extra = _inconv_call(x3, sh, sc, norm_g, w_qkv, w_gates, w_extra, extra_dtypes, conv_w, par,
                                             grid_w, use_rows, tm)
    o_f, o_b, s_fin = _gdn_call(q, k, v, go, rows, s0)
    return extra, o_f, o_b, s_fin


def kernel(x, c, ctx, c_ctx, w_mod, b_mod, norm1_g, norm2_g, w_in, conv_w, a_log, dt_bias, gdn_norm_g,
           w_fourier_out, w_gdn_out, w_merge_out, w_router, b_router, w_gate, b_gate, w_up, b_up,
           w_down, b_down, final_norm_g):
    b, l, d = x.shape
    n = b * l
    n_ctx = ctx.shape[1]
    assert w_mod.shape[0] == 1 and l == GRID_W * GRID_W and d == V_DIM

    c8 = jnp.concatenate([c, c_ctx[None, :], jnp.zeros((8 - b - 1, d), F32)], axis=0)
    mod = _mod_call(c8, w_mod[0], b_mod[0])
    sh1, sc1, g1, sh2, sc2, g2 = [mod[:b, j * d:(j + 1) * d].reshape(b, 1, d) for j in range(6)]
    csh1 = jnp.broadcast_to(mod[b:b + 1, 0:d].reshape(1, 1, d), (b, 1, d))
    csc1 = jnp.broadcast_to(mod[b:b + 1, d:2 * d].reshape(1, 1, d), (b, 1, d))

    off_gate = QKV_DIM
    off_z = off_gate + 4 * NV_HEADS
    off_f = off_z + V_DIM
    off_ga = off_f + F_DIM
    w_qkv, w_gates, *w_extra = _wsplit_call(w_in, (0, off_gate, off_z, off_f, off_ga, w_in.shape[2]))
    par = jnp.pad(jnp.stack([a_log[0].reshape(-1), dt_bias[0].reshape(-1)]),
                  ((0, 6), (GATE_LANE0, LANES - 2 * GATE_LANE0)))
    n1 = norm1_g[0].reshape(1, d)
    cw = conv_w[0].reshape(9, QKV_DIM)

    zero_state = jnp.zeros((b, 2, NV_HEADS, HEAD_DIM, HEAD_DIM), F32)
    _, _, _, s_ctx = _gdn_branch(ctx, csh1, csc1, n1, w_qkv, w_gates, [], [], cw, par, n_ctx, False, n_ctx,
                                 zero_state)

    x2 = x.reshape(n, d)
    (z, f, gab), o_f, o_b, _ = _gdn_branch(x, sh1, sc1, n1, w_qkv, w_gates, w_extra, (BF16, F32, BF16),
                                           cw, par, GRID_W, True, TOKEN_TILE, s_ctx)
    z, gab = z.reshape(n, V_DIM), gab.reshape(n, 2 * d)
    fmix = _fnet_call(f).reshape(n, F_DIM)

    wr = jnp.pad(w_router[0], ((0, 0), (0, LANES - N_EXPERTS)))
    br = jnp.pad(b_router[0], (0, LANES - N_EXPERTS), constant_values=NEG_BIG).reshape(1, LANES)
    x1, h2, logits = _merge_call(
        o_f.reshape(n, V_DIM), o_b.reshape(n, V_DIM), z, fmix, gab, x2, g1, sh2, sc2,
        gdn_norm_g[0].reshape(1, HEAD_DIM), norm2_g[0].reshape(1, d),
        w_gdn_out[0].astype(BF16), w_fourier_out[0].astype(BF16), w_merge_out[0].astype(BF16), wr, br, l, TOKEN_TILE)

    top_idx, top_w, rank, counts = _route_call(logits, TOKEN_TILE)
    block_e, n_valid, next_e, parity, rows_used, dest_flat, n_slots = _routing_tables(
        top_idx, rank, counts[0, :N_EXPERTS], n)
    xs = _sc_scatter_rows(h2, dest_flat, n_slots)
    ys = _expert_call(block_e, n_valid, next_e, parity, rows_used, xs, w_gate[0], w_up[0], w_down[0],
                      b_gate[0], b_up[0], b_down[0])
    y4 = _sc_gather_rows(ys, dest_flat)
    out = _combine_call(y4, top_w, x1, g2, final_norm_g.reshape(1, d), l, TOKEN_TILE)
    return out.reshape(b, l, d)
```

```python
import functools
import math

import jax
import jax.numpy as jnp
import numpy as np
from jax import lax
from jax.experimental import pallas as pl
from jax.experimental.pallas import tpu as pltpu
from jax.experimental.pallas import tpu_sc as plsc

F32 = jnp.float32
BF16 = jnp.bfloat16
I32 = jnp.int32
HIGHEST = lax.Precision.HIGHEST

GRID_W = 64
NQK_HEADS = 4
NV_HEADS = 8
HEAD_DIM = 128
QK_DIM = NQK_HEADS * HEAD_DIM
V_DIM = NV_HEADS * HEAD_DIM
QKV_DIM = 2 * QK_DIM + V_DIM
F_GROUPS = 4
F_DIM = F_GROUPS * HEAD_DIM
N_EXPERTS = 32
TOP_K = 4
SWIGLU_ALPHA = 1.702
SWIGLU_LIMIT = 7.0
EPS = 1e-6

LANES = 128
SUBLANES = 8
GATE_LANE0 = 16
GDN_CHUNK = 128
EXPERT_BLOCK = 512
TOKEN_TILE = 512
SC_CORES = 2
SC_SUBCORES = 16
SC_WORKERS = SC_CORES * SC_SUBCORES
SC_WINDOW = 64
NEG_BIG = -1e30
MIB = 2 ** 20


def _cparams(sem, vmem_mib):
    return pltpu.CompilerParams(dimension_semantics=sem, vmem_limit_bytes=vmem_mib * MIB)


def _mm(a, b, prec=None, dims=(((1,), (0,)), ((), ()))):
    if prec is None:
        return lax.dot_general(a.astype(BF16), b.astype(BF16), dims, preferred_element_type=F32)
    return lax.dot_general(a.astype(F32), b.astype(F32), dims, precision=prec, preferred_element_type=F32)


def _sigmoid(x):
    return 0.5 * jnp.tanh(0.5 * x) + 0.5


def _rmsnorm(x, g):
    return x * lax.rsqrt(jnp.mean(x * x, axis=-1, keepdims=True) + EPS) * g


def _pack_bf16_pairs(x):
    half = x.shape[1] // 2
    bits = lax.bitcast_convert_type(x.astype(BF16).astype(F32), jnp.uint32)
    packed = jnp.bitwise_or(jnp.right_shift(bits[:, :half], jnp.uint32(16)),
                            jnp.bitwise_and(bits[:, half:], jnp.uint32(0xFFFF0000)))
    return lax.bitcast_convert_type(packed, I32)


def _unpack_bf16_pairs(p):
    bits = lax.bitcast_convert_type(p, jnp.uint32)
    lo = lax.bitcast_convert_type(jnp.left_shift(bits, jnp.uint32(16)), F32)
    hi = lax.bitcast_convert_type(jnp.bitwise_and(bits, jnp.uint32(0xFFFF0000)), F32)
    return jnp.concatenate([lo, hi], axis=1)


def _mod_kernel(c_ref, w_ref, b_ref, o_ref):
    c = c_ref[...]
    o_ref[...] = _mm(c * _sigmoid(c), w_ref[...], HIGHEST) + b_ref[...]


def _mod_call(c8, w_mod, b_mod):
    d, n = w_mod.shape
    tn = 1536
    return pl.pallas_call(
        _mod_kernel,
        grid=(n // tn,),
        in_specs=[pl.BlockSpec((8, d), lambda j: (0, 0)),
                  pl.BlockSpec((d, tn), lambda j: (0, j)),
                  pl.BlockSpec((1, tn), lambda j: (0, j))],
        out_specs=pl.BlockSpec((8, tn), lambda j: (0, j)),
        out_shape=jax.ShapeDtypeStruct((8, n), F32),
        compiler_params=_cparams(("parallel",), 32),
        name="mod",
    )(c8, w_mod, b_mod.reshape(1, n))


def _wsplit_kernel(w_ref, *o_refs, bounds):
    for o_ref, lo, hi in zip(o_refs, bounds[:-1], bounds[1:]):
        o_ref[:, :hi - lo] = w_ref[0, :, lo:hi].astype(o_ref.dtype)
        if o_ref.shape[1] > hi - lo:
            o_ref[:, hi - lo:] = jnp.zeros((o_ref.shape[0], o_ref.shape[1] - (hi - lo)), o_ref.dtype)


def _wsplit_call(w, bounds):
    d = w.shape[1]
    tr = 256
    widths = [-(-(hi - lo) // LANES) * LANES for lo, hi in zip(bounds[:-1], bounds[1:])]
    return pl.pallas_call(
        functools.partial(_wsplit_kernel, bounds=bounds),
        grid=(d // tr,),
        in_specs=[pl.BlockSpec((1, tr, w.shape[2]), lambda i: (0, i, 0))],
        out_specs=[pl.BlockSpec((tr, wd), lambda i: (i, 0)) for wd in widths],
        out_shape=[jax.ShapeDtypeStruct((d, wd), BF16) for wd in widths],
        compiler_params=_cparams(("parallel",), 32),
        name="wsplit",
    )(w)


def _inconv_kernel(*refs, grid_w, use_rows, tm, cw, n_extra):
    refs = list(refs)
    prev_ref = refs.pop(0) if use_rows else None
    x_ref = refs.pop(0)
    next_ref = refs.pop(0) if use_rows else None
    sh_ref, sc_ref, g_ref, wq_ref, wgate_ref = refs[:5]
    wx_refs = refs[5:5 + n_extra]
    w_ref, par_ref, q_ref, k_ref, v_ref, go_ref, gr_ref = refs[5 + n_extra:12 + n_extra]
    ox_refs = refs[12 + n_extra:]

    def modulated(x):
        return (_rmsnorm(x, g_ref[...]) * (1.0 + sc_ref[0]) + sh_ref[0]).astype(BF16)

    r = pl.program_id(1)
    nr = pl.num_programs(1)
    u = modulated(x_ref[0])

    def project(wx_ref, o_ref, c0, step):
        o_ref[0, :, c0:c0 + step] = _mm(u, wx_ref[:, c0:c0 + step]).astype(o_ref.dtype)

    plain = [functools.partial(project, wx_ref, o_ref, c0, min(wx_ref.shape[1], 512))
             for wx_ref, o_ref in zip(wx_refs, ox_refs) for c0 in range(0, wx_ref.shape[1], min(wx_ref.shape[1], 512))]
    n_conv_chunks = QKV_DIM // cw

    t = lax.broadcasted_iota(I32, (tm, 1), 0)
    col = jnp.bitwise_and(t, grid_w - 1)
    m_left = (col != 0).astype(F32)
    m_right = (col != grid_w - 1).astype(F32)
    if use_rows:
        u_prev = modulated(prev_ref[0])
        u_next = modulated(next_ref[0])
        has_prev = (r > 0).astype(F32)
        has_next = (r < nr - 1).astype(F32)
    for c0 in range(0, QKV_DIM, cw):
        wq = wq_ref[:, c0:c0 + cw]
        xm = _mm(u, wq)
        if use_rows:
            up = jnp.concatenate([_mm(u_prev, wq) * has_prev, xm[:tm - grid_w]], axis=0)
            dn = jnp.concatenate([xm[grid_w:], _mm(u_next, wq) * has_next], axis=0)

        def colsum(kc):
            y = xm * w_ref[3 + kc:4 + kc, c0:c0 + cw]
            if use_rows:
                y = y + up * w_ref[kc:kc + 1, c0:c0 + cw] + dn * w_ref[6 + kc:7 + kc, c0:c0 + cw]
            return y

        acc = (colsum(1) + pltpu.roll(colsum(0), 1, axis=0) * m_left
               + pltpu.roll(colsum(2), tm - 1, axis=0) * m_right)
        s = acc * _sigmoid(acc)
        for h0 in range(0, cw, HEAD_DIM):
            c = c0 + h0
            seg = s[:, h0:h0 + HEAD_DIM]
            if c < 2 * QK_DIM:
                seg = seg * lax.rsqrt(jnp.sum(seg * seg, axis=-1, keepdims=True) + EPS)
            if c < QK_DIM:
                q_ref[0, :, c:c + HEAD_DIM] = (seg * HEAD_DIM ** -0.5).astype(q_ref.dtype)
            elif c < 2 * QK_DIM:
                k_ref[0, :, c - QK_DIM:c - QK_DIM + HEAD_DIM] = seg.astype(k_ref.dtype)
            else:
                v_ref[0, :, c - 2 * QK_DIM:c - 2 * QK_DIM + HEAD_DIM] = seg.astype(v_ref.dtype)
        ci = c0 // cw
        for job in plain[ci * len(plain) // n_conv_chunks:(ci + 1) * len(plain) // n_conv_chunks]:
            job()
    g = _mm(u, wgate_ref[...])
    a = g + par_ref[1:2, :]
    softplus = jnp.maximum(a, 0.0) + jnp.log1p(jnp.exp(-jnp.abs(a)))
    log_g = -jnp.exp(par_ref[0:1, :]) * softplus
    lane = lax.broadcasted_iota(I32, g.shape, 1)
    go_ref[0] = jnp.where(lane < GATE_LANE0, _sigmoid(g), log_g)
    gr_ref[0] = jnp.transpose(log_g)[GATE_LANE0:GATE_LANE0 + 2 * NV_HEADS]


def _inconv_call(x3, sh, sc, g, w_qkv, w_gates, w_extra, extra_dtypes, conv_w, par, grid_w, use_rows, tm):
    b, t, d = x3.shape
    kern = functools.partial(_inconv_kernel, grid_w=grid_w, use_rows=use_rows, tm=tm, cw=512, n_extra=len(w_extra))
    per = tm // grid_w
    nrow = t // grid_w
    tile = lambda wd: pl.BlockSpec((1, tm, wd), lambda i, r: (i, r, 0))
    vec = pl.BlockSpec((1, 1, d), lambda i, r: (i, 0, 0))
    const = lambda a: pl.BlockSpec(a.shape, lambda i, r: (0,) * a.ndim, pipeline_mode=pl.Buffered(1))
    in_specs, args = [], []
    if use_rows:
        in_specs.append(pl.BlockSpec((1, grid_w, d), lambda i, r: (i, jnp.maximum(r * per - 1, 0), 0)))
        args.append(x3)
    in_specs.append(tile(d))
    args.append(x3)
    if use_rows:
        in_specs.append(pl.BlockSpec((1, grid_w, d), lambda i, r: (i, jnp.minimum((r + 1) * per, nrow - 1), 0)))
        args.append(x3)
    consts = [g, w_qkv, w_gates, *w_extra, conv_w, par]
    in_specs += [vec, vec] + [const(a) for a in consts]
    args += [sh, sc] + consts
    widths = [QK_DIM, QK_DIM, V_DIM, LANES] + [w.shape[1] for w in w_extra]
    dtypes = [BF16, BF16, BF16, F32] + list(extra_dtypes)
    out_shape = [jax.ShapeDtypeStruct((b, t, wd), dt) for wd, dt in zip(widths, dtypes)]
    out_specs = [tile(wd) for wd in widths]
    out_shape.insert(4, jax.ShapeDtypeStruct((b, 2 * NV_HEADS, t), F32))
    out_specs.insert(4, pl.BlockSpec((1, 2 * NV_HEADS, tm), lambda i, r: (i, 0, r)))
    return pl.pallas_call(
        kern, grid=(b, t // tm), in_specs=in_specs, out_specs=out_specs, out_shape=out_shape,
        compiler_params=_cparams(("parallel", "parallel"), 56),
        name="inconv_rows" if use_rows else "inconv_seq",
    )(*args)


def _gdn_kernel(qf, kf, vf, gf, rf, qb, kb, vb, gb, rb, s0_ref, of, ob, sfin_ref, s_ref):
    i = pl.program_id(1)
    nc = pl.num_programs(1)

    @pl.when(i == 0)
    def _():
        s_ref[...] = s0_ref[0]

    c = qf.shape[1]
    per = NV_HEADS // NQK_HEADS
    row = lax.broadcasted_iota(I32, (c, c), 0)
    colj = lax.broadcasted_iota(I32, (c, c), 1)
    eye = jnp.where(row == colj, 1.0, 0.0)
    nt_dims = (((1,), (1,)), ((), ()))
    tn_dims = (((0,), (0,)), ((), ()))

    seqs = []
    for d, (q_r, k_r, v_r, g_r, r_r, o_r) in enumerate(((qf, kf, vf, gf, rf, of), (qb, kb, vb, gb, rb, ob))):
        rev = d == 1
        incl = (colj >= row) if rev else (colj <= row)
        strict = (colj > row) if rev else (colj < row)
        gates = g_r[0]
        tri_c = jnp.where(incl, 1.0, 0.0)
        gcm = _mm(tri_c, gates, HIGHEST)
        gcr = _mm(r_r[0], tri_c, HIGHEST, dims=nt_dims)
        for hq in range(NQK_HEADS):
            q = q_r[0, :, hq * HEAD_DIM:(hq + 1) * HEAD_DIM]
            k = k_r[0, :, hq * HEAD_DIM:(hq + 1) * HEAD_DIM]
            kq = lax.dot_general(jnp.concatenate([q, k], axis=0), k, nt_dims, preferred_element_type=F32)
            for j in range(per):
                h = hq * per + j
                idx = d * NV_HEADS + h
                gc_c = gcm[:, GATE_LANE0 + idx:GATE_LANE0 + idx + 1]
                seqs.append(dict(d=d, h=h, o_r=o_r, v_r=v_r, q=q, k=k, qk=kq[:c], kk=kq[c:], incl=incl, strict=strict,
                                 beta=gates[:, idx:idx + 1], gc_c=gc_c, gc_r=gcr[idx:idx + 1, :],
                                 ge=gc_c[0:1] if rev else gc_c[c - 1:c]))

    def same_block(m):
        sh = int(math.log2(m))
        return jnp.right_shift(row, sh) == jnp.right_shift(colj, sh)

    for s in seqs:
        s['decay'] = jnp.where(s['incl'], jnp.exp(jnp.where(s['incl'], s['gc_c'] - s['gc_r'], 0.0)), 0.0)
        s['a'] = jnp.where(s['strict'], s['beta'] * s['kk'] * s['decay'], 0.0)
        s['t'] = eye - jnp.where(same_block(2), s['a'], 0.0)
    m = 4
    while m <= c:
        between = jnp.logical_and(same_block(m), jnp.logical_not(same_block(m // 2)))
        for s in seqs:
            s['te'] = _mm(s['t'], jnp.where(between, s['a'], 0.0))
        for s in seqs:
            s['t'] = s['t'] - _mm(s['te'], s['t'])
        m *= 2
    for s in seqs:
        h = s['h']
        egc = jnp.exp(s['gc_c'])
        kf32 = s['k'].astype(F32)
        v = s['v_r'][0, :, h * HEAD_DIM:(h + 1) * HEAD_DIM].astype(F32)
        rhs = jnp.concatenate([s['beta'] * v, (s['beta'] * egc) * kf32], axis=1)
        s['sol'] = _mm(s['t'], rhs)
        s['q_dec'] = s['q'].astype(F32) * egc
        s['k_dec'] = kf32 * jnp.exp(s['ge'] - s['gc_c'])
    for s in seqs:
        s['ws'] = _mm(jnp.concatenate([s['sol'][:, HEAD_DIM:], s['q_dec']], axis=0), s_ref[s['d'], s['h']])
    for s in seqs:
        s['u'] = s['sol'][:, :HEAD_DIM] - s['ws'][:c]
        s_ref[s['d'], s['h']] = (jnp.exp(s['ge']) * s_ref[s['d'], s['h']]
                                 + _mm(s['k_dec'], s['u'], dims=tn_dims))
    for s in seqs:
        h = s['h']
        o = s['ws'][c:] + _mm(s['qk'] * s['decay'], s['u'])
        s['o_r'][0, :, h * HEAD_DIM:(h + 1) * HEAD_DIM] = o.astype(s['o_r'].dtype)

    @pl.when(i == nc - 1)
    def _():
        sfin_ref[0] = s_ref[...]


def _gdn_call(q, k, v, go, rows, s0):
    b, t, _ = q.shape
    c = GDN_CHUNK
    nc = t // c
    fwd = lambda i, n: (i, n, 0)
    bwd = lambda i, n: (i, nc - 1 - n, 0)
    rfwd = lambda i, n: (i, 0, n)
    rbwd = lambda i, n: (i, 0, nc - 1 - n)
    state_spec = pl.BlockSpec((1,) + s0.shape[1:], lambda i, n: (i, 0, 0, 0, 0))

    def specs(m3, mr):
        return [pl.BlockSpec((1, c, QK_DIM), m3), pl.BlockSpec((1, c, QK_DIM), m3),
                pl.BlockSpec((1, c, V_DIM), m3), pl.BlockSpec((1, c, LANES), m3),
                pl.BlockSpec((1, 2 * NV_HEADS, c), mr)]

    return pl.pallas_call(
        _gdn_kernel,
        grid=(b, nc),
        in_specs=specs(fwd, rfwd) + specs(bwd, rbwd) + [state_spec],
        out_specs=[pl.BlockSpec((1, c, V_DIM), fwd), pl.BlockSpec((1, c, V_DIM), bwd), state_spec],
        out_shape=[jax.ShapeDtypeStruct((b, t, V_DIM), BF16), jax.ShapeDtypeStruct((b, t, V_DIM), BF16),
                   jax.ShapeDtypeStruct(s0.shape, F32)],
        scratch_shapes=[pltpu.VMEM(s0.shape[1:], F32)],
        compiler_params=_cparams(("parallel", "arbitrary"), 48),
        name="gdn",
    )(q, k, v, go, rows, q, k, v, go, rows, s0)


def _fnet1_kernel(x_ref, f_ref, ar_ref, ai_ref):
    n = x_ref.shape[1]
    for j in range(x_ref.shape[2]):
        a = _mm(f_ref[...], x_ref[0, :, j, :])
        ar_ref[0, j] = a[:n]
        ai_ref[0, j] = a[n:]


def _fnet2_kernel(ar_ref, ai_ref, g_ref, wc_ref, o_ref, *, scale):
    n = ar_ref.shape[1]
    cols = ar_ref.shape[2]
    zs = []
    for m in range(cols):
        a2 = jnp.concatenate([ar_ref[0, :, m, :], ai_ref[0, :, m, :]], axis=0)
        zs.append(_mm(g_ref[m], a2))
    zr = jnp.concatenate([z[:n] for z in zs], axis=0)
    zi = jnp.concatenate([z[n:] for z in zs], axis=0)
    for g0 in range(0, zr.shape[1], HEAD_DIM):
        y = _mm(jnp.concatenate([zr[:, g0:g0 + HEAD_DIM], zi[:, g0:g0 + HEAD_DIM]], axis=1), wc_ref[...])
        for m in range(cols):
            o_ref[0, :, m, g0:g0 + HEAD_DIM] = y[m * n:(m + 1) * n] * scale


def _fnet_tables(n):
    a = np.arange(n)
    ang1 = 2.0 * np.pi * np.outer(a, a) / n
    f1 = np.concatenate([np.cos(ang1), -np.sin(ang1)], axis=0)
    m = a[:, None] + n * a[None, :]
    ang2 = 2.0 * np.pi * ((m[:, :, None] * a[None, None, :]) % (n * n)) / (n * n)
    gc, gs = np.cos(ang2), np.sin(ang2)
    g2 = np.concatenate([np.concatenate([gc, gs], axis=2), np.concatenate([-gs, gc], axis=2)], axis=1)
    angc = 2.0 * np.pi * np.outer(np.arange(HEAD_DIM), np.arange(HEAD_DIM)) / HEAD_DIM
    wc = np.concatenate([np.cos(angc), np.sin(angc)], axis=0)
    f = lambda x: jnp.asarray(x, F32).astype(BF16)
    return f(f1), f(g2), f(wc)


def _fnet_call(f):
    b, l, c = f.shape
    n = GRID_W
    assert l == n * n
    f1, g2, wc = _fnet_tables(n)
    cols = 4 * SUBLANES
    ar, ai = pl.pallas_call(
        _fnet1_kernel,
        grid=(b, n // cols),
        in_specs=[pl.BlockSpec((1, n, cols, c), lambda i, j: (i, 0, j, 0)),
                  pl.BlockSpec((2 * n, n), lambda i, j: (0, 0))],
        out_specs=[pl.BlockSpec((1, cols, n, c), lambda i, j: (i, j, 0, 0))] * 2,
        out_shape=[jax.ShapeDtypeStruct((b, n, n, c), F32)] * 2,
        compiler_params=_cparams(("parallel", "parallel"), 32),
        name="fnet1",
    )(f.reshape(b, n, n, c), f1)
    out = pl.pallas_call(
        functools.partial(_fnet2_kernel, scale=1.0 / math.sqrt(l * HEAD_DIM)),
        grid=(b, n // cols),
        in_specs=[pl.BlockSpec((1, n, cols, c), lambda i, j: (i, 0, j, 0)),
                  pl.BlockSpec((1, n, cols, c), lambda i, j: (i, 0, j, 0)),
                  pl.BlockSpec((cols, 2 * n, 2 * n), lambda i, j: (j, 0, 0)),
                  pl.BlockSpec((2 * HEAD_DIM, HEAD_DIM), lambda i, j: (0, 0))],
        out_specs=pl.BlockSpec((1, n, cols, c), lambda i, j: (i, 0, j, 0)),
        out_shape=jax.ShapeDtypeStruct((b, n, n, c), F32),
        compiler_params=_cparams(("parallel", "parallel"), 32),
        name="fnet2",
    )(ar, ai, g2, wc)
    return out.reshape(b, l, c)


def _merge_kernel(of_ref, ob_ref, z_ref, fm_ref, gab_ref, x_ref, g1_ref, sh2_ref, sc2_ref, gn_ref, n2_ref,
                  wg_ref, wf_ref, wm_ref, wr_ref, br_ref, x1_ref, h2_ref, lg_ref):
    d = x_ref.shape[1]
    tm = x_ref.shape[0]
    halves = [slice(0, tm // 2), slice(tm // 2, tm)]
    yb_in = []
    for r in halves:
        o = of_ref[r, :].astype(F32) + ob_ref[r, :].astype(F32)
        z = z_ref[r, :].astype(F32)
        parts = []
        for h0 in range(0, V_DIM, HEAD_DIM):
            oh = o[:, h0:h0 + HEAD_DIM]
            parts.append(oh * lax.rsqrt(jnp.mean(oh * oh, axis=-1, keepdims=True) + EPS) * gn_ref[...])
        yb_in.append(jnp.concatenate(parts, axis=1) * (z * _sigmoid(z)))
    yb = [_mm(v, wg_ref[...]) for v in yb_in]
    ya = [_mm(fm_ref[r, :], wf_ref[...]) for r in halves]
    mixed = [_sigmoid(gab_ref[r, :d].astype(F32)) * a + _sigmoid(gab_ref[r, d:].astype(F32)) * b_
             for r, a, b_ in zip(halves, ya, yb)]
    mm = [_mm(v, wm_ref[...]) for v in mixed]
    w = wr_ref[...]
    w_hi = w.astype(BF16)
    w2 = jnp.concatenate([w_hi, (w - w_hi.astype(F32)).astype(BF16)], axis=1)
    for r, v in zip(halves, mm):
        x1 = x_ref[r, :] + g1_ref[0] * v
        x1_ref[r, :] = x1
        h2 = _rmsnorm(x1, n2_ref[...]) * (1.0 + sc2_ref[0]) + sh2_ref[0]
        h2_ref[r, :] = _pack_bf16_pairs(h2)
        h_hi = h2.astype(BF16)
        h_lo = (h2 - h_hi.astype(F32)).astype(BF16)
        part = _mm(h_hi, w2)
        lg_ref[r, :] = (part[:, :LANES] + part[:, LANES:]) + _mm(h_lo, w_hi) + br_ref[...]


def _merge_call(of, ob, z, fm, gab, x2, g1, sh2, sc2, gn, n2, wg, wf, wm, wr, br, tokens_per_batch, tm):
    n, d = x2.shape
    per = tokens_per_batch // tm
    tok = lambda wd: pl.BlockSpec((tm, wd), lambda i: (i, 0))
    vec = pl.BlockSpec((1, 1, d), lambda i: (i // per, 0, 0))
    full = lambda a: pl.BlockSpec(a.shape, lambda i: (0,) * a.ndim)
    return pl.pallas_call(
        _merge_kernel,
        grid=(n // tm,),
        in_specs=[tok(V_DIM), tok(V_DIM), tok(V_DIM), tok(F_DIM), tok(2 * d), tok(d), vec, vec, vec,
                  full(gn), full(n2), full(wg), full(wf), full(wm), full(wr), full(br)],
        out_specs=[tok(d), tok(d // 2), tok(LANES)],
        out_shape=[jax.ShapeDtypeStruct((n, d), F32), jax.ShapeDtypeStruct((n, d // 2), I32),
                   jax.ShapeDtypeStruct((n, LANES), F32)],
        compiler_params=_cparams(("parallel",), 56),
        name="merge",
    )(of, ob, z, fm, gab, x2, g1, sh2, sc2, gn, n2, wg, wf, wm, wr, br)


def _route_kernel(lg_ref, idx_ref, w_ref, rank_ref, cnt_ref, run_ref):
    i = pl.program_id(0)

    @pl.when(i == 0)
    def _():
        run_ref[...] = jnp.zeros_like(run_ref)

    l = lg_ref[...]
    tm = l.shape[0]
    lane = lax.broadcasted_iota(I32, l.shape, 1).astype(F32)
    vals, idxs = [], []
    for _ in range(TOP_K):
        m = jnp.max(l, axis=-1, keepdims=True)
        idx = jnp.min(jnp.where(l == m, lane, float(LANES)), axis=-1, keepdims=True)
        vals.append(m)
        idxs.append(idx)
        l = jnp.where(lane == idx, NEG_BIG * 2.0, l)
    es = [jnp.exp(v - vals[0]) for v in vals]
    inv = 1.0 / (es[0] + es[1] + es[2] + es[3])
    picked = jnp.zeros(l.shape, F32)
    for idx in idxs:
        picked = picked + (lane == idx).astype(F32)
    r = lax.broadcasted_iota(I32, (tm, tm), 0)
    cidx = lax.broadcasted_iota(I32, (tm, tm), 1)
    before = _mm(jnp.where(cidx < r, 1.0, 0.0), picked) + run_ref[...]
    idx_out = jnp.zeros(l.shape, F32)
    w_out = jnp.zeros(l.shape, F32)
    rank_out = jnp.zeros(l.shape, F32)
    for k in range(TOP_K):
        rk = jnp.sum(jnp.where(lane == idxs[k], before, 0.0), axis=-1, keepdims=True)
        idx_out = jnp.where(lane == k, idxs[k], idx_out)
        w_out = jnp.where(lane == k, es[k] * inv, w_out)
        rank_out = jnp.where(lane == k, rk, rank_out)
    idx_ref[...] = jnp.transpose(idx_out)[:SUBLANES].astype(I32)
    w_ref[...] = w_out
    rank_ref[...] = jnp.transpose(rank_out)[:SUBLANES].astype(I32)
    run_ref[...] = run_ref[...] + jnp.sum(picked, axis=0, keepdims=True)
    cnt_ref[...] = run_ref[...]


def _route_call(logits, tm):
    n = logits.shape[0]
    tok = pl.BlockSpec((tm, LANES), lambda i: (i, 0))
    tok_t = pl.BlockSpec((SUBLANES, tm), lambda i: (0, i))
    return pl.pallas_call(
        _route_kernel,
        grid=(n // tm,),
        in_specs=[tok],
        out_specs=[tok_t, tok, tok_t, pl.BlockSpec((1, LANES), lambda i: (0, 0))],
        out_shape=[jax.ShapeDtypeStruct((SUBLANES, n), I32), jax.ShapeDtypeStruct((n, LANES), F32),
                   jax.ShapeDtypeStruct((SUBLANES, n), I32), jax.ShapeDtypeStruct((1, LANES), F32)],
        scratch_shapes=[pltpu.VMEM((1, LANES), F32)],
        compiler_params=_cparams(("arbitrary",), 32),
        name="route",
    )(logits)


def _sc_mesh():
    return plsc.VectorSubcoreMesh(core_axis_name="c", subcore_axis_name="s",
                                  num_cores=SC_CORES, num_subcores=SC_SUBCORES)


def _sc_worker_id():
    return lax.axis_index("s") * SC_CORES + lax.axis_index("c")


def _sc_scatter_rows(x, idx, n_out):
    n, d = x.shape
    per_w = idx.shape[0] // SC_WORKERS
    nwin = per_w // SC_WINDOW
    assert per_w * SC_WORKERS == idx.shape[0] and nwin * SC_WINDOW == per_w and nwin % 2 == 0 and n % per_w == 0

    def body(x_hbm, idx_hbm, out_hbm, idx_v, rows_v, sem_r, sem_w):
        wid = _sc_worker_id()
        row0 = lax.rem(wid * per_w, n)
        pltpu.sync_copy(idx_hbm.at[wid], idx_v)

        def read(j, b):
            return pltpu.make_async_copy(x_hbm.at[pl.ds(row0 + j * SC_WINDOW, SC_WINDOW)], rows_v.at[b], sem_r.at[b])

        def write(j, b):
            return pltpu.make_async_copy(rows_v.at[b], out_hbm.at[idx_v.at[j]], sem_w.at[b])

        @pl.loop(0, nwin, step=2)
        def _(j):
            read(j, 0).start()
            read(j + 1, 1).start()
            read(j, 0).wait()
            write(j, 0).start()
            read(j + 1, 1).wait()
            write(j + 1, 1).start()
            write(j, 0).wait()
            write(j + 1, 1).wait()

    return pl.kernel(
        body, out_type=jax.ShapeDtypeStruct((n_out, d), x.dtype), mesh=_sc_mesh(),
        scratch_types=[pltpu.VMEM((nwin, SC_WINDOW), I32), pltpu.VMEM((2, SC_WINDOW, d), x.dtype),
                       pltpu.SemaphoreType.DMA((2,)), pltpu.SemaphoreType.DMA((2,))],
        name="sc_scatter_rows",
    )(x, idx.reshape(SC_WORKERS, nwin, SC_WINDOW))


def _sc_gather_rows(y, idx):
    d = y.shape[1]
    total = idx.shape[0]
    per_w = total // SC_WORKERS
    nwin = per_w // SC_WINDOW
    assert per_w * SC_WORKERS == total and nwin * SC_WINDOW == per_w and nwin % 2 == 0

    def body(y_hbm, idx_hbm, out_hbm, idx_v, rows_v, sem_r, sem_w):
        wid = _sc_worker_id()
        row0 = wid * per_w
        pltpu.sync_copy(idx_hbm.at[wid], idx_v)

        def read(j, b):
            return pltpu.make_async_copy(y_hbm.at[idx_v.at[j]], rows_v.at[b], sem_r.at[b])

        def write(j, b):
            return pltpu.make_async_copy(rows_v.at[b], out_hbm.at[pl.ds(row0 + j * SC_WINDOW, SC_WINDOW)], sem_w.at[b])

        @pl.loop(0, nwin, step=2)
        def _(j):
            read(j, 0).start()
            read(j + 1, 1).start()
            read(j, 0).wait()
            write(j, 0).start()
            read(j + 1, 1).wait()
            write(j + 1, 1).start()
            write(j, 0).wait()
            write(j + 1, 1).wait()

    return pl.kernel(
        body, out_type=jax.ShapeDtypeStruct((total, d), y.dtype), mesh=_sc_mesh(),
        scratch_types=[pltpu.VMEM((nwin, SC_WINDOW), I32), pltpu.VMEM((2, SC_WINDOW, d), y.dtype),
                       pltpu.SemaphoreType.DMA((2,)), pltpu.SemaphoreType.DMA((2,))],
        name="sc_gather_rows",
    )(y, idx.reshape(SC_WORKERS, nwin, SC_WINDOW))


def _expert_kernel(be_ref, nv_ref, nxt_ref, par_ref, used_ref, x_ref, wg_hbm, wu_hbm, wd_hbm, bg_ref, bu_ref, bd_ref,
                   y_ref, wgf, wuf, wdf, wgb, wub, wdb, sem):
    i = pl.program_id(0)

    def fetch(e, s):
        return [pltpu.make_async_copy(w_hbm.at[e], w_f.at[s], sem.at[s])
                for w_hbm, w_f in ((wg_hbm, wgf), (wu_hbm, wuf), (wd_hbm, wdf))]

    @pl.when(i < nv_ref[0])
    def _():
        e = be_ref[i]
        s = par_ref[e]

        @pl.when(i == 0)
        def _():
            for cp in fetch(e, s):
                cp.start()

        @pl.when(jnp.logical_or(i == 0, be_ref[jnp.maximum(i - 1, 0)] != e))
        def _():
            for cp in fetch(e, s):
                cp.wait()
            wgb[...] = wgf[s].astype(BF16)
            wub[...] = wuf[s].astype(BF16)
            wdb[...] = wdf[s].astype(BF16)

            @pl.when(nxt_ref[e] >= 0)
            def _():
                for cp in fetch(nxt_ref[e], 1 - s):
                    cp.start()

        def ffn(nrows):
            x = _unpack_bf16_pairs(x_ref[:nrows, :]).astype(BF16)
            gate = jnp.minimum(_mm(x, wgb[...]) + bg_ref[0], SWIGLU_LIMIT)
            up = jnp.clip(_mm(x, wub[...]) + bu_ref[0], -SWIGLU_LIMIT, SWIGLU_LIMIT)
            act = (up + 1.0) * gate * _sigmoid(SWIGLU_ALPHA * gate)
            y_ref[:nrows, :] = _pack_bf16_pairs(_mm(act, wdb[...]) + bd_ref[0])

        quarter = x_ref.shape[0] // 4
        for nq in range(1, 5):
            @pl.when(jnp.logical_and(used_ref[i] > (nq - 1) * quarter, used_ref[i] <= nq * quarter))
            def _(nq=nq):
                ffn(nq * quarter)


def _expert_call(block_e, n_valid, next_e, parity, used, xs, w_gate, w_up, w_down, b_gate, b_up, b_down):
    nb = block_e.shape[0]
    tmb = EXPERT_BLOCK
    ne, d, de = w_gate.shape
    bspec = lambda s: pl.BlockSpec((1,) + s, lambda i, be, nv, nx, pa, us: (be[i], 0, 0))
    anyspec = pl.BlockSpec(memory_space=pl.ANY)
    grid_spec = pltpu.PrefetchScalarGridSpec(
        num_scalar_prefetch=5,
        grid=(nb,),
        in_specs=[pl.BlockSpec((tmb, d // 2), lambda i, be, nv, nx, pa, us: (i, 0)),
                  anyspec, anyspec, anyspec, bspec((1, de)), bspec((1, de)), bspec((1, d))],
        out_specs=pl.BlockSpec((tmb, d // 2), lambda i, be, nv, nx, pa, us: (i, 0)),
        scratch_shapes=[pltpu.VMEM((2, d, de), F32), pltpu.VMEM((2, d, de), F32), pltpu.VMEM((2, de, d), F32),
                        pltpu.VMEM((d, de), BF16), pltpu.VMEM((d, de), BF16), pltpu.VMEM((de, d), BF16),
                        pltpu.SemaphoreType.DMA((2,))],
    )
    return pl.pallas_call(
        _expert_kernel,
        grid_spec=grid_spec,
        out_shape=jax.ShapeDtypeStruct(xs.shape, xs.dtype),
        compiler_params=_cparams(("arbitrary",), 56),
        name="expert",
    )(block_e, n_valid, next_e, parity, used, xs, w_gate, w_up, w_down,
      b_gate.reshape(ne, 1, de), b_up.reshape(ne, 1, de), b_down.reshape(ne, 1, d))


def _combine_kernel(y0, y1, y2, y3, w_ref, x1_ref, g2_ref, fg_ref, o_ref):
    w = w_ref[...]
    ys = [_unpack_bf16_pairs(y[...]) for y in (y0, y1, y2, y3)]
    moe = (w[:, 0:1] * ys[0] + w[:, 1:2] * ys[1]) + (w[:, 2:3] * ys[2] + w[:, 3:4] * ys[3])
    o_ref[...] = _rmsnorm(x1_ref[...] + g2_ref[0] * moe, fg_ref[...])


def _combine_call(y4, top_w, x1, g2, fg, tokens_per_batch, tm):
    n, d = x1.shape
    per = tokens_per_batch // tm
    nt = n // tm
    yspec = lambda k: pl.BlockSpec((tm, d // 2), lambda i: (k * nt + i, 0))
    return pl.pallas_call(
        _combine_kernel,
        grid=(nt,),
        in_specs=[yspec(0), yspec(1), yspec(2), yspec(3),
                  pl.BlockSpec((tm, LANES), lambda i: (i, 0)),
                  pl.BlockSpec((tm, d), lambda i: (i, 0)),
                  pl.BlockSpec((1, 1, d), lambda i: (i // per, 0, 0)),
                  pl.BlockSpec((1, d), lambda i: (0, 0))],
        out_specs=pl.BlockSpec((tm, d), lambda i: (i, 0)),
        out_shape=jax.ShapeDtypeStruct((n, d), F32),
        compiler_params=_cparams(("parallel",), 48),
        name="combine",
    )(y4, y4, y4, y4, top_w, x1, g2, fg)


def _slot_kernel(start_ref, idx_ref, rank_ref, o_ref):
    idx = idx_ref[...]
    acc = rank_ref[...]
    for e in range(N_EXPERTS):
        acc = acc + jnp.where(idx == e, start_ref[e], 0)
    o_ref[...] = acc


def _slot_call(pad_start, top_idx, rank):
    full = pl.BlockSpec(top_idx.shape, lambda i, s: (0, 0))
    return pl.pallas_call(
        _slot_kernel,
        grid_spec=pltpu.PrefetchScalarGridSpec(num_scalar_prefetch=1, grid=(1,), in_specs=[full, full],
                                               out_specs=full),
        out_shape=jax.ShapeDtypeStruct(top_idx.shape, I32),
        compiler_params=_cparams(("arbitrary",), 32),
        name="slots",
    )(pad_start, top_idx, rank)


def _routing_tables(top_idx, rank, counts, n):
    tmb = EXPERT_BLOCK
    counts = counts.astype(I32)
    padded = (counts + tmb - 1) // tmb * tmb
    pad_end = jnp.cumsum(padded)
    pad_start = pad_end - padded
    n_blocks = -(-(n * TOP_K + N_EXPERTS * (tmb - 1)) // tmb)
    n_slots = n_blocks * tmb
    dest_flat = _slot_call(pad_start, top_idx, rank)[:TOP_K].reshape(-1)
    block_start = jnp.arange(n_blocks, dtype=I32) * tmb
    block_e = jnp.minimum(jnp.sum((pad_end[None, :] <= block_start[:, None]).astype(I32), axis=1), N_EXPERTS - 1)
    n_valid = (pad_end[-1:] // tmb).astype(I32)
    experts = jnp.arange(N_EXPERTS, dtype=I32)
    used = counts > 0
    later = jnp.where(jnp.logical_and(used[None, :], experts[None, :] > experts[:, None]), experts[None, :], N_EXPERTS)
    next_e = jnp.min(later, axis=1)
    next_e = jnp.where(next_e == N_EXPERTS, -1, next_e).astype(I32)
    parity = ((jnp.cumsum(used.astype(I32)) - used.astype(I32)) % 2).astype(I32)
    row_end = jnp.sum(jnp.where(block_e[:, None] == experts[None, :], (pad_start + counts)[None, :], 0), axis=1)
    rows_used = jnp.clip(row_end - block_start, 0, tmb).astype(I32)
    return block_e, n_valid, next_e, parity, rows_used, dest_flat, n_slots


def _gdn_branch(x3, sh, sc, norm_g, w_qkv, w_gates, w_extra, extra_dtypes, conv_w, par, grid_w, use_rows, tm, s0):
    q, k, v, go, rows, *extra = _inconv_call(x3, sh, sc, norm_g, w_qkv, w_gates, w_extra, extra_dtypes, conv_w, par,
                                             grid_w, use_rows, tm)
    o_f, o_b, s_fin = _gdn_call(q, k, v, go, rows, s0)
    return extra, o_f, o_b, s_fin


def kernel(x, c, ctx, c_ctx, w_mod, b_mod, norm1_g, norm2_g, w_in, conv_w, a_log, dt_bias, gdn_norm_g,
           w_fourier_out, w_gdn_out, w_merge_out, w_router, b_router, w_gate, b_gate, w_up, b_up,
           w_down, b_down, final_norm_g):
    b, l, d = x.shape
    n = b * l
    n_ctx = ctx.shape[1]
    assert w_mod.shape[0] == 1 and l == GRID_W * GRID_W and d == V_DIM

    c8 = jnp.concatenate([c, c_ctx[None, :], jnp.zeros((8 - b - 1, d), F32)], axis=0)
    mod = _mod_call(c8, w_mod[0], b_mod[0])
    sh1, sc1, g1, sh2, sc2, g2 = [mod[:b, j * d:(j + 1) * d].reshape(b, 1, d) for j in range(6)]
    csh1 = jnp.broadcast_to(mod[b:b + 1, 0:d].reshape(1, 1, d), (b, 1, d))
    csc1 = jnp.broadcast_to(mod[b:b + 1, d:2 * d].reshape(1, 1, d), (b, 1, d))

    off_gate = QKV_DIM
    off_z = off_gate + 4 * NV_HEADS
    off_f = off_z + V_DIM
    off_ga = off_f + F_DIM
    w_qkv, w_gates, *w_extra = _wsplit_call(w_in, (0, off_gate, off_z, off_f, off_ga, w_in.shape[2]))
    par = jnp.pad(jnp.stack([a_log[0].reshape(-1), dt_bias[0].reshape(-1)]),
                  ((0, 6), (GATE_LANE0, LANES - 2 * GATE_LANE0)))
    n1 = norm1_g[0].reshape(1, d)
    cw = conv_w[0].reshape(9, QKV_DIM)

    zero_state = jnp.zeros((b, 2, NV_HEADS, HEAD_DIM, HEAD_DIM), F32)
    _, _, _, s_ctx = _gdn_branch(ctx, csh1, csc1, n1, w_qkv, w_gates, [], [], cw, par, n_ctx, False, n_ctx,
                                 zero_state)

    x2 = x.reshape(n, d)
    (z, f, gab), o_f, o_b, _ = _gdn_branch(x, sh1, sc1, n1, w_qkv, w_gates, w_extra, (BF16, F32, BF16),
                                           cw, par, GRID_W, True, TOKEN_TILE, s_ctx)
    z, gab = z.reshape(n, V_DIM), gab.reshape(n, 2 * d)
    fmix = _fnet_call(f).reshape(n, F_DIM)

    wr = jnp.pad(w_router[0], ((0, 0), (0, LANES - N_EXPERTS)))
    br = jnp.pad(b_router[0], (0, LANES - N_EXPERTS), constant_values=NEG_BIG).reshape(1, LANES)
    x1, h2, logits = _merge_call(
        o_f.reshape(n, V_DIM), o_b.reshape(n, V_DIM), z, fmix, gab, x2, g1, sh2, sc2,
        gdn_norm_g[0].reshape(1, HEAD_DIM), norm2_g[0].reshape(1, d),
        w_gdn_out[0].astype(BF16), w_fourier_out[0].astype(BF16), w_merge_out[0].astype(BF16), wr, br, l, TOKEN_TILE)

    top_idx, top_w, rank, counts = _route_call(logits, TOKEN_TILE)
    block_e, n_valid, next_e, parity, rows_used, dest_flat, n_slots = _routing_tables(
        top_idx, rank, counts[0, :N_EXPERTS], n)
    xs = _sc_scatter_rows(h2, dest_flat, n_slots)
    ys = _expert_call(block_e, n_valid, next_e, parity, rows_used, xs, w_gate[0], w_up[0], w_down[0],
                      b_gate[0], b_up[0], b_down[0])
    y4 = _sc_gather_rows(ys, dest_flat)
    out = _combine_call(y4, top_w, x1, g2, final_norm_g.reshape(1, d), l, TOKEN_TILE)
    return out.reshape(b, l, d)
```

```python
import functools
import math

import jax
import jax.numpy as jnp
import numpy as np
from jax import lax
from jax.experimental import pallas as pl
from jax.experimental.pallas import tpu as pltpu
from jax.experimental.pallas import tpu_sc as plsc

F32 = jnp.float32
BF16 = jnp.bfloat16
I32 = jnp.int32
HIGHEST = lax.Precision.HIGHEST

GRID_W = 64
NQK_HEADS = 4
NV_HEADS = 8
HEAD_DIM = 128
QK_DIM = NQK_HEADS * HEAD_DIM
V_DIM = NV_HEADS * HEAD_DIM
QKV_DIM = 2 * QK_DIM + V_DIM
F_GROUPS = 4
F_DIM = F_GROUPS * HEAD_DIM
N_EXPERTS = 32
TOP_K = 4
SWIGLU_ALPHA = 1.702
SWIGLU_LIMIT = 7.0
EPS = 1e-6

LANES = 128
SUBLANES = 8
GATE_LANE0 = 16
GDN_CHUNK = 128
EXPERT_BLOCK = 512
TOKEN_TILE = 512
SC_CORES = 2
SC_SUBCORES = 16
SC_WORKERS = SC_CORES * SC_SUBCORES
SC_WINDOW = 64
NEG_BIG = -1e30
MIB = 2 ** 20


def _cparams(sem, vmem_mib):
    return pltpu.CompilerParams(dimension_semantics=sem, vmem_limit_bytes=vmem_mib * MIB)


def _mm(a, b, prec=None, dims=(((1,), (0,)), ((), ()))):
    if prec is None:
        return lax.dot_general(a.astype(BF16), b.astype(BF16), dims, preferred_element_type=F32)
    return lax.dot_general(a.astype(F32), b.astype(F32), dims, precision=prec, preferred_element_type=F32)


def _sigmoid(x):
    return 0.5 * jnp.tanh(0.5 * x) + 0.5


def _rmsnorm(x, g):
    return x * lax.rsqrt(jnp.mean(x * x, axis=-1, keepdims=True) + EPS) * g


def _pack_bf16_pairs(x):
    half = x.shape[1] // 2
    bits = lax.bitcast_convert_type(x.astype(BF16).astype(F32), jnp.uint32)
    packed = jnp.bitwise_or(jnp.right_shift(bits[:, :half], jnp.uint32(16)),
                            jnp.bitwise_and(bits[:, half:], jnp.uint32(0xFFFF0000)))
    return lax.bitcast_convert_type(packed, I32)


def _unpack_bf16_pairs(p):
    bits = lax.bitcast_convert_type(p, jnp.uint32)
    lo = lax.bitcast_convert_type(jnp.left_shift(bits, jnp.uint32(16)), F32)
    hi = lax.bitcast_convert_type(jnp.bitwise_and(bits, jnp.uint32(0xFFFF0000)), F32)
    return jnp.concatenate([lo, hi], axis=1)


def _mod_kernel(c_ref, w_ref, b_ref, o_ref):
    c = c_ref[...]
    o_ref[...] = _mm(c * _sigmoid(c), w_ref[...], HIGHEST) + b_ref[...]


def _mod_call(c8, w_mod, b_mod):
    d, n = w_mod.shape
    tn = 1536
    return pl.pallas_call(
        _mod_kernel,
        grid=(n // tn,),
        in_specs=[pl.BlockSpec((8, d), lambda j: (0, 0)),
                  pl.BlockSpec((d, tn), lambda j: (0, j)),
                  pl.BlockSpec((1, tn), lambda j: (0, j))],
        out_specs=pl.BlockSpec((8, tn), lambda j: (0, j)),
        out_shape=jax.ShapeDtypeStruct((8, n), F32),
        compiler_params=_cparams(("parallel",), 32),
        name="mod",
    )(c8, w_mod, b_mod.reshape(1, n))


def _wsplit_kernel(w_ref, *o_refs, bounds):
    for o_ref, lo, hi in zip(o_refs, bounds[:-1], bounds[1:]):
        o_ref[:, :hi - lo] = w_ref[0, :, lo:hi].astype(o_ref.dtype)
        if o_ref.shape[1] > hi - lo:
            o_ref[:, hi - lo:] = jnp.zeros((o_ref.shape[0], o_ref.shape[1] - (hi - lo)), o_ref.dtype)


def _wsplit_call(w, bounds):
    d = w.shape[1]
    tr = 256
    widths = [-(-(hi - lo) // LANES) * LANES for lo, hi in zip(bounds[:-1], bounds[1:])]
    return pl.pallas_call(
        functools.partial(_wsplit_kernel, bounds=bounds),
        grid=(d // tr,),
        in_specs=[pl.BlockSpec((1, tr, w.shape[2]), lambda i: (0, i, 0))],
        out_specs=[pl.BlockSpec((tr, wd), lambda i: (i, 0)) for wd in widths],
        out_shape=[jax.ShapeDtypeStruct((d, wd), BF16) for wd in widths],
        compiler_params=_cparams(("parallel",), 32),
        name="wsplit",
    )(w)


def _inconv_kernel(*refs, grid_w, use_rows, tm, cw, n_extra):
    refs = list(refs)
    prev_ref = refs.pop(0) if use_rows else None
    x_ref = refs.pop(0)
    next_ref = refs.pop(0) if use_rows else None
    sh_ref, sc_ref, g_ref, wq_ref, wgate_ref = refs[:5]
    wx_refs = refs[5:5 + n_extra]
    w_ref, par_ref, q_ref, k_ref, v_ref, go_ref, gr_ref = refs[5 + n_extra:12 + n_extra]
    ox_refs = refs[12 + n_extra:]

    def modulated(x):
        return (_rmsnorm(x, g_ref[...]) * (1.0 + sc_ref[0]) + sh_ref[0]).astype(BF16)

    r = pl.program_id(1)
    nr = pl.num_programs(1)
    u = modulated(x_ref[0])

    def project(wx_ref, o_ref, c0, step):
        o_ref[0, :, c0:c0 + step] = _mm(u, wx_ref[:, c0:c0 + step]).astype(o_ref.dtype)

    plain = [functools.partial(project, wx_ref, o_ref, c0, min(wx_ref.shape[1], 512))
             for wx_ref, o_ref in zip(wx_refs, ox_refs) for c0 in range(0, wx_ref.shape[1], min(wx_ref.shape[1], 512))]
    n_conv_chunks = QKV_DIM // cw

    t = lax.broadcasted_iota(I32, (tm, 1), 0)
    col = jnp.bitwise_and(t, grid_w - 1)
    m_left = (col != 0).astype(F32)
    m_right = (col != grid_w - 1).astype(F32)
    if use_rows:
        u_prev = modulated(prev_ref[0])
        u_next = modulated(next_ref[0])
        has_prev = (r > 0).astype(F32)
        has_next = (r < nr - 1).astype(F32)
    for c0 in range(0, QKV_DIM, cw):
        wq = wq_ref[:, c0:c0 + cw]
        xm = _mm(u, wq)
        if use_rows:
            up = jnp.concatenate([_mm(u_prev, wq) * has_prev, xm[:tm - grid_w]], axis=0)
            dn = jnp.concatenate([xm[grid_w:], _mm(u_next, wq) * has_next], axis=0)

        def colsum(kc):
            y = xm * w_ref[3 + kc:4 + kc, c0:c0 + cw]
            if use_rows:
                y = y + up * w_ref[kc:kc + 1, c0:c0 + cw] + dn * w_ref[6 + kc:7 + kc, c0:c0 + cw]
            return y

        acc = (colsum(1) + pltpu.roll(colsum(0), 1, axis=0) * m_left
               + pltpu.roll(colsum(2), tm - 1, axis=0) * m_right)
        s = acc * _sigmoid(acc)
        for h0 in range(0, cw, HEAD_DIM):
            c = c0 + h0
            seg = s[:, h0:h0 + HEAD_DIM]
            if c < 2 * QK_DIM:
                seg = seg * lax.rsqrt(jnp.sum(seg * seg, axis=-1, keepdims=True) + EPS)
            if c < QK_DIM:
                q_ref[0, :, c:c + HEAD_DIM] = (seg * HEAD_DIM ** -0.5).astype(q_ref.dtype)
            elif c < 2 * QK_DIM:
                k_ref[0, :, c - QK_DIM:c - QK_DIM + HEAD_DIM] = seg.astype(k_ref.dtype)
            else:
                v_ref[0, :, c - 2 * QK_DIM:c - 2 * QK_DIM + HEAD_DIM] = seg.astype(v_ref.dtype)
        ci = c0 // cw
        for job in plain[ci * len(plain) // n_conv_chunks:(ci + 1) * len(plain) // n_conv_chunks]:
            job()
    g = _mm(u, wgate_ref[...])
    a = g + par_ref[1:2, :]
    softplus = jnp.maximum(a, 0.0) + jnp.log1p(jnp.exp(-jnp.abs(a)))
    log_g = -jnp.exp(par_ref[0:1, :]) * softplus
    lane = lax.broadcasted_iota(I32, g.shape, 1)
    go_ref[0] = jnp.where(lane < GATE_LANE0, _sigmoid(g), log_g)
    gr_ref[0] = jnp.transpose(log_g)[GATE_LANE0:GATE_LANE0 + 2 * NV_HEADS]


def _inconv_call(x3, sh, sc, g, w_qkv, w_gates, w_extra, extra_dtypes, conv_w, par, grid_w, use_rows, tm):
    b, t, d = x3.shape
    kern = functools.partial(_inconv_kernel, grid_w=grid_w, use_rows=use_rows, tm=tm, cw=512, n_extra=len(w_extra))
    per = tm // grid_w
    nrow = t // grid_w
    tile = lambda wd: pl.BlockSpec((1, tm, wd), lambda i, r: (i, r, 0))
    vec = pl.BlockSpec((1, 1, d), lambda i, r: (i, 0, 0))
    const = lambda a: pl.BlockSpec(a.shape, lambda i, r: (0,) * a.ndim, pipeline_mode=pl.Buffered(1))
    in_specs, args = [], []
    if use_rows:
        in_specs.append(pl.BlockSpec((1, grid_w, d), lambda i, r: (i, jnp.maximum(r * per - 1, 0), 0)))
        args.append(x3)
    in_specs.append(tile(d))
    args.append(x3)
    if use_rows:
        in_specs.append(pl.BlockSpec((1, grid_w, d), lambda i, r: (i, jnp.minimum((r + 1) * per, nrow - 1), 0)))
        args.append(x3)
    consts = [g, w_qkv, w_gates, *w_extra, conv_w, par]
    in_specs += [vec, vec] + [const(a) for a in consts]
    args += [sh, sc] + consts
    widths = [QK_DIM, QK_DIM, V_DIM, LANES] + [w.shape[1] for w in w_extra]
    dtypes = [BF16, BF16, BF16, F32] + list(extra_dtypes)
    out_shape = [jax.ShapeDtypeStruct((b, t, wd), dt) for wd, dt in zip(widths, dtypes)]
    out_specs = [tile(wd) for wd in widths]
    out_shape.insert(4, jax.ShapeDtypeStruct((b, 2 * NV_HEADS, t), F32))
    out_specs.insert(4, pl.BlockSpec((1, 2 * NV_HEADS, tm), lambda i, r: (i, 0, r)))
    return pl.pallas_call(
        kern, grid=(b, t // tm), in_specs=in_specs, out_specs=out_specs, out_shape=out_shape,
        compiler_params=_cparams(("parallel", "parallel"), 56),
        name="inconv_rows" if use_rows else "inconv_seq",
    )(*args)


def _gdn_kernel(qf, kf, vf, gf, rf, qb, kb, vb, gb, rb, s0_ref, of, ob, sfin_ref, s_ref):
    i = pl.program_id(1)
    nc = pl.num_programs(1)

    @pl.when(i == 0)
    def _():
        s_ref[...] = s0_ref[0]

    c = qf.shape[1]
    per = NV_HEADS // NQK_HEADS
    row = lax.broadcasted_iota(I32, (c, c), 0)
    colj = lax.broadcasted_iota(I32, (c, c), 1)
    eye = jnp.where(row == colj, 1.0, 0.0)
    nt_dims = (((1,), (1,)), ((), ()))
    tn_dims = (((0,), (0,)), ((), ()))

    seqs = []
    for d, (q_r, k_r, v_r, g_r, r_r, o_r) in enumerate(((qf, kf, vf, gf, rf, of), (qb, kb, vb, gb, rb, ob))):
        rev = d == 1
        incl = (colj >= row) if rev else (colj <= row)
        strict = (colj > row) if rev else (colj < row)
        gates = g_r[0]
        tri_c = jnp.where(incl, 1.0, 0.0)
        gcm = _mm(tri_c, gates, HIGHEST)
        gcr = _mm(r_r[0], tri_c, HIGHEST, dims=nt_dims)
        for hq in range(NQK_HEADS):
            q = q_r[0, :, hq * HEAD_DIM:(hq + 1) * HEAD_DIM]
            k = k_r[0, :, hq * HEAD_DIM:(hq + 1) * HEAD_DIM]
            kq = lax.dot_general(jnp.concatenate([q, k], axis=0), k, nt_dims, preferred_element_type=F32)
            for j in range(per):
                h = hq * per + j
                idx = d * NV_HEADS + h
                gc_c = gcm[:, GATE_LANE0 + idx:GATE_LANE0 + idx + 1]
                seqs.append(dict(d=d, h=h, o_r=o_r, v_r=v_r, q=q, k=k, qk=kq[:c], kk=kq[c:], incl=incl, strict=strict,
                                 beta=gates[:, idx:idx + 1], gc_c=gc_c, gc_r=gcr[idx:idx + 1, :],
                                 ge=gc_c[0:1] if rev else gc_c[c - 1:c]))

    def same_block(m):
        sh = int(math.log2(m))
        return jnp.right_shift(row, sh) == jnp.right_shift(colj, sh)

    for s in seqs:
        s['decay'] = jnp.where(s['incl'], jnp.exp(jnp.where(s['incl'], s['gc_c'] - s['gc_r'], 0.0)), 0.0)
        s['a'] = jnp.where(s['strict'], s['beta'] * s['kk'] * s['decay'], 0.0)
        s['t'] = eye - jnp.where(same_block(2), s['a'], 0.0)
    m = 4
    while m <= c:
        between = jnp.logical_and(same_block(m), jnp.logical_not(same_block(m // 2)))
        for s in seqs:
            s['te'] = _mm(s['t'], jnp.where(between, s['a'], 0.0))
        for s in seqs:
            s['t'] = s['t'] - _mm(s['te'], s['t'])
        m *= 2
    for s in seqs:
        h = s['h']
        egc = jnp.exp(s['gc_c'])
        kf32 = s['k'].astype(F32)
        v = s['v_r'][0, :, h * HEAD_DIM:(h + 1) * HEAD_DIM].astype(F32)
        rhs = jnp.concatenate([s['beta'] * v, (s['beta'] * egc) * kf32], axis=1)
        s['sol'] = _mm(s['t'], rhs)
        s['q_dec'] = s['q'].astype(F32) * egc
        s['k_dec'] = kf32 * jnp.exp(s['ge'] - s['gc_c'])
    for s in seqs:
        s['ws'] = _mm(jnp.concatenate([s['sol'][:, HEAD_DIM:], s['q_dec']], axis=0), s_ref[s['d'], s['h']])
    for s in seqs:
        s['u'] = s['sol'][:, :HEAD_DIM] - s['ws'][:c]
        s_ref[s['d'], s['h']] = (jnp.exp(s['ge']) * s_ref[s['d'], s['h']]
                                 + _mm(s['k_dec'], s['u'], dims=tn_dims))
    for s in seqs:
        h = s['h']
        o = s['ws'][c:] + _mm(s['qk'] * s['decay'], s['u'])
        s['o_r'][0, :, h * HEAD_DIM:(h + 1) * HEAD_DIM] = o.astype(s['o_r'].dtype)

    @pl.when(i == nc - 1)
    def _():
        sfin_ref[0] = s_ref[...]


def _gdn_call(q, k, v, go, rows, s0):
    b, t, _ = q.shape
    c = GDN_CHUNK
    nc = t // c
    fwd = lambda i, n: (i, n, 0)
    bwd = lambda i, n: (i, nc - 1 - n, 0)
    rfwd = lambda i, n: (i, 0, n)
    rbwd = lambda i, n: (i, 0, nc - 1 - n)
    state_spec = pl.BlockSpec((1,) + s0.shape[1:], lambda i, n: (i, 0, 0, 0, 0))

    def specs(m3, mr):
        return [pl.BlockSpec((1, c, QK_DIM), m3), pl.BlockSpec((1, c, QK_DIM), m3),
                pl.BlockSpec((1, c, V_DIM), m3), pl.BlockSpec((1, c, LANES), m3),
                pl.BlockSpec((1, 2 * NV_HEADS, c), mr)]

    return pl.pallas_call(
        _gdn_kernel,
        grid=(b, nc),
        in_specs=specs(fwd, rfwd) + specs(bwd, rbwd) + [state_spec],
        out_specs=[pl.BlockSpec((1, c, V_DIM), fwd), pl.BlockSpec((1, c, V_DIM), bwd), state_spec],
        out_shape=[jax.ShapeDtypeStruct((b, t, V_DIM), BF16), jax.ShapeDtypeStruct((b, t, V_DIM), BF16),
                   jax.ShapeDtypeStruct(s0.shape, F32)],
        scratch_shapes=[pltpu.VMEM(s0.shape[1:], F32)],
        compiler_params=_cparams(("parallel", "arbitrary"), 48),
        name="gdn",
    )(q, k, v, go, rows, q, k, v, go, rows, s0)


def _fnet1_kernel(x_ref, f_ref, ar_ref, ai_ref):
    n = x_ref.shape[1]
    for j in range(x_ref.shape[2]):
        a = _mm(f_ref[...], x_ref[0, :, j, :])
        ar_ref[0, j] = a[:n]
        ai_ref[0, j] = a[n:]


def _fnet2_kernel(ar_ref, ai_ref, g_ref, wc_ref, o_ref, *, scale):
    n = ar_ref.shape[1]
    cols = ar_ref.shape[2]
    zs = []
    for m in range(cols):
        a2 = jnp.concatenate([ar_ref[0, :, m, :], ai_ref[0, :, m, :]], axis=0)
        zs.append(_mm(g_ref[m], a2))
    zr = jnp.concatenate([z[:n] for z in zs], axis=0)
    zi = jnp.concatenate([z[n:] for z in zs], axis=0)
    for g0 in range(0, zr.shape[1], HEAD_DIM):
        y = _mm(jnp.concatenate([zr[:, g0:g0 + HEAD_DIM], zi[:, g0:g0 + HEAD_DIM]], axis=1), wc_ref[...])
        for m in range(cols):
            o_ref[0, :, m, g0:g0 + HEAD_DIM] = y[m * n:(m + 1) * n] * scale


def _fnet_tables(n):
    a = np.arange(n)
    ang1 = 2.0 * np.pi * np.outer(a, a) / n
    f1 = np.concatenate([np.cos(ang1), -np.sin(ang1)], axis=0)
    m = a[:, None] + n * a[None, :]
    ang2 = 2.0 * np.pi * ((m[:, :, None] * a[None, None, :]) % (n * n)) / (n * n)
    gc, gs = np.cos(ang2), np.sin(ang2)
    g2 = np.concatenate([np.concatenate([gc, gs], axis=2), np.concatenate([-gs, gc], axis=2)], axis=1)
    angc = 2.0 * np.pi * np.outer(np.arange(HEAD_DIM), np.arange(HEAD_DIM)) / HEAD_DIM
    wc = np.concatenate([np.cos(angc), np.sin(angc)], axis=0)
    f = lambda x: jnp.asarray(x, F32).astype(BF16)
    return f(f1), f(g2), f(wc)


def _fnet_call(f):
    b, l, c = f.shape
    n = GRID_W
    assert l == n * n
    f1, g2, wc = _fnet_tables(n)
    cols = 4 * SUBLANES
    ar, ai = pl.pallas_call(
        _fnet1_kernel,
        grid=(b, n // cols),
        in_specs=[pl.BlockSpec((1, n, cols, c), lambda i, j: (i, 0, j, 0)),
                  pl.BlockSpec((2 * n, n), lambda i, j: (0, 0))],
        out_specs=[pl.BlockSpec((1, cols, n, c), lambda i, j: (i, j, 0, 0))] * 2,
        out_shape=[jax.ShapeDtypeStruct((b, n, n, c), F32)] * 2,
        compiler_params=_cparams(("parallel", "parallel"), 32),
        name="fnet1",
    )(f.reshape(b, n, n, c), f1)
    out = pl.pallas_call(
        functools.partial(_fnet2_kernel, scale=1.0 / math.sqrt(l * HEAD_DIM)),
        grid=(b, n // cols),
        in_specs=[pl.BlockSpec((1, n, cols, c), lambda i, j: (i, 0, j, 0)),
                  pl.BlockSpec((1, n, cols, c), lambda i, j: (i, 0, j, 0)),
                  pl.BlockSpec((cols, 2 * n, 2 * n), lambda i, j: (j, 0, 0)),
                  pl.BlockSpec((2 * HEAD_DIM, HEAD_DIM), lambda i, j: (0, 0))],
        out_specs=pl.BlockSpec((1, n, cols, c), lambda i, j: (i, 0, j, 0)),
        out_shape=jax.ShapeDtypeStruct((b, n, n, c), F32),
        compiler_params=_cparams(("parallel", "parallel"), 32),
        name="fnet2",
    )(ar, ai, g2, wc)
    return out.reshape(b, l, c)


def _merge_kernel(of_ref, ob_ref, z_ref, fm_ref, gab_ref, x_ref, g1_ref, sh2_ref, sc2_ref, gn_ref, n2_ref,
                  wg_ref, wf_ref, wm_ref, wr_ref, br_ref, x1_ref, h2_ref, lg_ref):
    d = x_ref.shape[1]
    tm = x_ref.shape[0]
    halves = [slice(0, tm // 2), slice(tm // 2, tm)]
    yb_in = []
    for r in halves:
        o = of_ref[r, :].astype(F32) + ob_ref[r, :].astype(F32)
        z = z_ref[r, :].astype(F32)
        parts = []
        for h0 in range(0, V_DIM, HEAD_DIM):
            oh = o[:, h0:h0 + HEAD_DIM]
            parts.append(oh * lax.rsqrt(jnp.mean(oh * oh, axis=-1, keepdims=True) + EPS) * gn_ref[...])
        yb_in.append(jnp.concatenate(parts, axis=1) * (z * _sigmoid(z)))
    yb = [_mm(v, wg_ref[...]) for v in yb_in]
    ya = [_mm(fm_ref[r, :], wf_ref[...]) for r in halves]
    mixed = [_sigmoid(gab_ref[r, :d].astype(F32)) * a + _sigmoid(gab_ref[r, d:].astype(F32)) * b_
             for r, a, b_ in zip(halves, ya, yb)]
    mm = [_mm(v, wm_ref[...]) for v in mixed]
    w = wr_ref[...]
    w_hi = w.astype(BF16)
    w2 = jnp.concatenate([w_hi, (w - w_hi.astype(F32)).astype(BF16)], axis=1)
    for r, v in zip(halves, mm):
        x1 = x_ref[r, :] + g1_ref[0] * v
        x1_ref[r, :] = x1
        h2 = _rmsnorm(x1, n2_ref[...]) * (1.0 + sc2_ref[0]) + sh2_ref[0]
        h2_ref[r, :] = _pack_bf16_pairs(h2)
        h_hi = h2.astype(BF16)
        h_lo = (h2 - h_hi.astype(F32)).astype(BF16)
        part = _mm(h_hi, w2)
        lg_ref[r, :] = (part[:, :LANES] + part[:, LANES:]) + _mm(h_lo, w_hi) + br_ref[...]


def _merge_call(of, ob, z, fm, gab, x2, g1, sh2, sc2, gn, n2, wg, wf, wm, wr, br, tokens_per_batch, tm):
    n, d = x2.shape
    per = tokens_per_batch // tm
    tok = lambda wd: pl.BlockSpec((tm, wd), lambda i: (i, 0))
    vec = pl.BlockSpec((1, 1, d), lambda i: (i // per, 0, 0))
    full = lambda a: pl.BlockSpec(a.shape, lambda i: (0,) * a.ndim)
    return pl.pallas_call(
        _merge_kernel,
        grid=(n // tm,),
        in_specs=[tok(V_DIM), tok(V_DIM), tok(V_DIM), tok(F_DIM), tok(2 * d), tok(d), vec, vec, vec,
                  full(gn), full(n2), full(wg), full(wf), full(wm), full(wr), full(br)],
        out_specs=[tok(d), tok(d // 2), tok(LANES)],
        out_shape=[jax.ShapeDtypeStruct((n, d), F32), jax.ShapeDtypeStruct((n, d // 2), I32),
                   jax.ShapeDtypeStruct((n, LANES), F32)],
        compiler_params=_cparams(("parallel",), 56),
        name="merge",
    )(of, ob, z, fm, gab, x2, g1, sh2, sc2, gn, n2, wg, wf, wm, wr, br)


def _route_kernel(lg_ref, idx_ref, w_ref, rank_ref, cnt_ref, run_ref):
    i = pl.program_id(0)

    @pl.when(i == 0)
    def _():
        run_ref[...] = jnp.zeros_like(run_ref)

    l = lg_ref[...]
    tm = l.shape[0]
    lane = lax.broadcasted_iota(I32, l.shape, 1).astype(F32)
    vals, idxs = [], []
    for _ in range(TOP_K):
        m = jnp.max(l, axis=-1, keepdims=True)
        idx = jnp.min(jnp.where(l == m, lane, float(LANES)), axis=-1, keepdims=True)
        vals.append(m)
        idxs.append(idx)
        l = jnp.where(lane == idx, NEG_BIG * 2.0, l)
    es = [jnp.exp(v - vals[0]) for v in vals]
    inv = 1.0 / (es[0] + es[1] + es[2] + es[3])
    picked = jnp.zeros(l.shape, F32)
    for idx in idxs:
        picked = picked + (lane == idx).astype(F32)
    r = lax.broadcasted_iota(I32, (tm, tm), 0)
    cidx = lax.broadcasted_iota(I32, (tm, tm), 1)
    before = _mm(jnp.where(cidx < r, 1.0, 0.0), picked) + run_ref[...]
    idx_out = jnp.zeros(l.shape, F32)
    w_out = jnp.zeros(l.shape, F32)
    rank_out = jnp.zeros(l.shape, F32)
    for k in range(TOP_K):
        rk = jnp.sum(jnp.where(lane == idxs[k], before, 0.0), axis=-1, keepdims=True)
        idx_out = jnp.where(lane == k, idxs[k], idx_out)
        w_out = jnp.where(lane == k, es[k] * inv, w_out)
        rank_out = jnp.where(lane == k, rk, rank_out)
    idx_ref[...] = jnp.transpose(idx_out)[:SUBLANES].astype(I32)
    w_ref[...] = w_out
    rank_ref[...] = jnp.transpose(rank_out)[:SUBLANES].astype(I32)
    run_ref[...] = run_ref[...] + jnp.sum(picked, axis=0, keepdims=True)
    cnt_ref[...] = run_ref[...]


def _route_call(logits, tm):
    n = logits.shape[0]
    tok = pl.BlockSpec((tm, LANES), lambda i: (i, 0))
    tok_t = pl.BlockSpec((SUBLANES, tm), lambda i: (0, i))
    return pl.pallas_call(
        _route_kernel,
        grid=(n // tm,),
        in_specs=[tok],
        out_specs=[tok_t, tok, tok_t, pl.BlockSpec((1, LANES), lambda i: (0, 0))],
        out_shape=[jax.ShapeDtypeStruct((SUBLANES, n), I32), jax.ShapeDtypeStruct((n, LANES), F32),
                   jax.ShapeDtypeStruct((SUBLANES, n), I32), jax.ShapeDtypeStruct((1, LANES), F32)],
        scratch_shapes=[pltpu.VMEM((1, LANES), F32)],
        compiler_params=_cparams(("arbitrary",), 32),
        name="route",
    )(logits)


def _sc_mesh():
    return plsc.VectorSubcoreMesh(core_axis_name="c", subcore_axis_name="s",
                                  num_cores=SC_CORES, num_subcores=SC_SUBCORES)


def _sc_worker_id():
    return lax.axis_index("s") * SC_CORES + lax.axis_index("c")


def _sc_scatter_rows(x, idx, n_out):
    n, d = x.shape
    per_w = idx.shape[0] // SC_WORKERS
    nwin = per_w // SC_WINDOW
    assert per_w * SC_WORKERS == idx.shape[0] and nwin * SC_WINDOW == per_w and nwin % 2 == 0 and n % per_w == 0

    def body(x_hbm, idx_hbm, out_hbm, idx_v, rows_v, sem_r, sem_w):
        wid = _sc_worker_id()
        row0 = lax.rem(wid * per_w, n)
        pltpu.sync_copy(idx_hbm.at[wid], idx_v)

        def read(j, b):
            return pltpu.make_async_copy(x_hbm.at[pl.ds(row0 + j * SC_WINDOW, SC_WINDOW)], rows_v.at[b], sem_r.at[b])

        def write(j, b):
            return pltpu.make_async_copy(rows_v.at[b], out_hbm.at[idx_v.at[j]], sem_w.at[b])

        @pl.loop(0, nwin, step=2)
        def _(j):
            read(j, 0).start()
            read(j + 1, 1).start()
            read(j, 0).wait()
            write(j, 0).start()
            read(j + 1, 1).wait()
            write(j + 1, 1).start()
            write(j, 0).wait()
            write(j + 1, 1).wait()

    return pl.kernel(
        body, out_type=jax.ShapeDtypeStruct((n_out, d), x.dtype), mesh=_sc_mesh(),
        scratch_types=[pltpu.VMEM((nwin, SC_WINDOW), I32), pltpu.VMEM((2, SC_WINDOW, d), x.dtype),
                       pltpu.SemaphoreType.DMA((2,)), pltpu.SemaphoreType.DMA((2,))],
        name="sc_scatter_rows",
    )(x, idx.reshape(SC_WORKERS, nwin, SC_WINDOW))


def _sc_gather_rows(y, idx):
    d = y.shape[1]
    total = idx.shape[0]
    per_w = total // SC_WORKERS
    nwin = per_w // SC_WINDOW
    assert per_w * SC_WORKERS == total and nwin * SC_WINDOW == per_w and nwin % 2 == 0

    def body(y_hbm, idx_hbm, out_hbm, idx_v, rows_v, sem_r, sem_w):
        wid = _sc_worker_id()
        row0 = wid * per_w
        pltpu.sync_copy(idx_hbm.at[wid], idx_v)

        def read(j, b):
            return pltpu.make_async_copy(y_hbm.at[idx_v.at[j]], rows_v.at[b], sem_r.at[b])

        def write(j, b):
            return pltpu.make_async_copy(rows_v.at[b], out_hbm.at[pl.ds(row0 + j * SC_WINDOW, SC_WINDOW)], sem_w.at[b])

        @pl.loop(0, nwin, step=2)
        def _(j):
            read(j, 0).start()
            read(j + 1, 1).start()
            read(j, 0).wait()
            write(j, 0).start()
            read(j + 1, 1).wait()
            write(j + 1, 1).start()
            write(j, 0).wait()
            write(j + 1, 1).wait()

    return pl.kernel(
        body, out_type=jax.ShapeDtypeStruct((total, d), y.dtype), mesh=_sc_mesh(),
        scratch_types=[pltpu.VMEM((nwin, SC_WINDOW), I32), pltpu.VMEM((2, SC_WINDOW, d), y.dtype),
                       pltpu.SemaphoreType.DMA((2,)), pltpu.SemaphoreType.DMA((2,))],
        name="sc_gather_rows",
    )(y, idx.reshape(SC_WORKERS, nwin, SC_WINDOW))


def _expert_kernel(be_ref, nv_ref, nxt_ref, par_ref, used_ref, x_ref, wg_hbm, wu_hbm, wd_hbm, bg_ref, bu_ref, bd_ref,
                   y_ref, wgf, wuf, wdf, wgb, wub, wdb, sem):
    i = pl.program_id(0)

    def fetch(e, s):
        return [pltpu.make_async_copy(w_hbm.at[e], w_f.at[s], sem.at[s])
                for w_hbm, w_f in ((wg_hbm, wgf), (wu_hbm, wuf), (wd_hbm, wdf))]

    @pl.when(i < nv_ref[0])
    def _():
        e = be_ref[i]
        s = par_ref[e]

        @pl.when(i == 0)
        def _():
            for cp in fetch(e, s):
                cp.start()

        first = jnp.logical_or(i == 0, be_ref[jnp.maximum(i - 1, 0)] != e)
        quarter = x_ref.shape[0] // 4
        full = used_ref[i] > 3 * quarter

        @pl.when(first)
        def _():
            for cp in fetch(e, s):
                cp.wait()

            @pl.when(nxt_ref[e] >= 0)
            def _():
                for cp in fetch(nxt_ref[e], 1 - s):
                    cp.start()

        def ffn(nrows, cast):
            x = _unpack_bf16_pairs(x_ref[:nrows, :]).astype(BF16)
            if cast:
                wgb[...] = wgf[s].astype(BF16)
            gate = jnp.minimum(_mm(x, wgb[...]) + bg_ref[0], SWIGLU_LIMIT)
            if cast:
                wub[...] = wuf[s].astype(BF16)
            up = jnp.clip(_mm(x, wub[...]) + bu_ref[0], -SWIGLU_LIMIT, SWIGLU_LIMIT)
            if cast:
                wdb[...] = wdf[s].astype(BF16)
            act = (up + 1.0) * gate * _sigmoid(SWIGLU_ALPHA * gate)
            y_ref[:nrows, :] = _pack_bf16_pairs(_mm(act, wdb[...]) + bd_ref[0])

        @pl.when(jnp.logical_and(first, full))
        def _():
            ffn(4 * quarter, True)

        @pl.when(jnp.logical_and(first, jnp.logical_not(full)))
        def _():
            wgb[...] = wgf[s].astype(BF16)
            wub[...] = wuf[s].astype(BF16)
            wdb[...] = wdf[s].astype(BF16)

        for nq in range(1, 5):
            in_range = jnp.logical_and(used_ref[i] > (nq - 1) * quarter, used_ref[i] <= nq * quarter)
            if nq == 4:
                in_range = jnp.logical_and(in_range, jnp.logical_not(first))

            @pl.when(in_range)
            def _(nq=nq):
                ffn(nq * quarter, False)


def _expert_call(block_e, n_valid, next_e, parity, used, xs, w_gate, w_up, w_down, b_gate, b_up, b_down):
    nb = block_e.shape[0]
    tmb = EXPERT_BLOCK
    ne, d, de = w_gate.shape
    bspec = lambda s: pl.BlockSpec((1,) + s, lambda i, be, nv, nx, pa, us: (be[i], 0, 0))
    anyspec = pl.BlockSpec(memory_space=pl.ANY)
    grid_spec = pltpu.PrefetchScalarGridSpec(
        num_scalar_prefetch=5,
        grid=(nb,),
        in_specs=[pl.BlockSpec((tmb, d // 2), lambda i, be, nv, nx, pa, us: (i, 0)),
                  anyspec, anyspec, anyspec, bspec((1, de)), bspec((1, de)), bspec((1, d))],
        out_specs=pl.BlockSpec((tmb, d // 2), lambda i, be, nv, nx, pa, us: (i, 0)),
        scratch_shapes=[pltpu.VMEM((2, d, de), F32), pltpu.VMEM((2, d, de), F32), pltpu.VMEM((2, de, d), F32),
                        pltpu.VMEM((d, de), BF16), pltpu.VMEM((d, de), BF16), pltpu.VMEM((de, d), BF16),
                        pltpu.SemaphoreType.DMA((2,))],
    )
    return pl.pallas_call(
        _expert_kernel,
        grid_spec=grid_spec,
        out_shape=jax.ShapeDtypeStruct(xs.shape, xs.dtype),
        compiler_params=_cparams(("arbitrary",), 56),
        name="expert",
    )(block_e, n_valid, next_e, parity, used, xs, w_gate, w_up, w_down,
      b_gate.reshape(ne, 1, de), b_up.reshape(ne, 1, de), b_down.reshape(ne, 1, d))


def _combine_kernel(y0, y1, y2, y3, w_ref, x1_ref, g2_ref, fg_ref, o_ref):
    w = w_ref[...]
    ys = [_unpack_bf16_pairs(y[...]) for y in (y0, y1, y2, y3)]
    moe = (w[:, 0:1] * ys[0] + w[:, 1:2] * ys[1]) + (w[:, 2:3] * ys[2] + w[:, 3:4] * ys[3])
    o_ref[...] = _rmsnorm(x1_ref[...] + g2_ref[0] * moe, fg_ref[...])


def _combine_call(y4, top_w, x1, g2, fg, tokens_per_batch, tm):
    n, d = x1.shape
    per = tokens_per_batch // tm
    nt = n // tm
    yspec = lambda k: pl.BlockSpec((tm, d // 2), lambda i: (k * nt + i, 0))
    return pl.pallas_call(
        _combine_kernel,
        grid=(nt,),
        in_specs=[yspec(0), yspec(1), yspec(2), yspec(3),
                  pl.BlockSpec((tm, LANES), lambda i: (i, 0)),
                  pl.BlockSpec((tm, d), lambda i: (i, 0)),
                  pl.BlockSpec((1, 1, d), lambda i: (i // per, 0, 0)),
                  pl.BlockSpec((1, d), lambda i: (0, 0))],
        out_specs=pl.BlockSpec((tm, d), lambda i: (i, 0)),
        out_shape=jax.ShapeDtypeStruct((n, d), F32),
        compiler_params=_cparams(("parallel",), 48),
        name="combine",
    )(y4, y4, y4, y4, top_w, x1, g2, fg)


def _slot_kernel(start_ref, idx_ref, rank_ref, o_ref):
    idx = idx_ref[...]
    acc = rank_ref[...]
    for e in range(N_EXPERTS):
        acc = acc + jnp.where(idx == e, start_ref[e], 0)
    o_ref[...] = acc


def _slot_call(pad_start, top_idx, rank):
    full = pl.BlockSpec(top_idx.shape, lambda i, s: (0, 0))
    return pl.pallas_call(
        _slot_kernel,
        grid_spec=pltpu.PrefetchScalarGridSpec(num_scalar_prefetch=1, grid=(1,), in_specs=[full, full],
                                               out_specs=full),
        out_shape=jax.ShapeDtypeStruct(top_idx.shape, I32),
        compiler_params=_cparams(("arbitrary",), 32),
        name="slots",
    )(pad_start, top_idx, rank)


def _routing_tables(top_idx, rank, counts, n):
    tmb = EXPERT_BLOCK
    counts = counts.astype(I32)
    padded = (counts + tmb - 1) // tmb * tmb
    pad_end = jnp.cumsum(padded)
    pad_start = pad_end - padded
    n_blocks = -(-(n * TOP_K + N_EXPERTS * (tmb - 1)) // tmb)
    n_slots = n_blocks * tmb
    dest_flat = _slot_call(pad_start, top_idx, rank)[:TOP_K].reshape(-1)
    block_start = jnp.arange(n_blocks, dtype=I32) * tmb
    block_e = jnp.minimum(jnp.sum((pad_end[None, :] <= block_start[:, None]).astype(I32), axis=1), N_EXPERTS - 1)
    n_valid = (pad_end[-1:] // tmb).astype(I32)
    experts = jnp.arange(N_EXPERTS, dtype=I32)
    used = counts > 0
    later = jnp.where(jnp.logical_and(used[None, :], experts[None, :] > experts[:, None]), experts[None, :], N_EXPERTS)
    next_e = jnp.min(later, axis=1)
    next_e = jnp.where(next_e == N_EXPERTS, -1, next_e).astype(I32)
    parity = ((jnp.cumsum(used.astype(I32)) - used.astype(I32)) % 2).astype(I32)
    row_end = jnp.sum(jnp.where(block_e[:, None] == experts[None, :], (pad_start + counts)[None, :], 0), axis=1)
    rows_used = jnp.clip(row_end - block_start, 0, tmb).astype(I32)
    return block_e, n_valid, next_e, parity, rows_used, dest_flat, n_slots


def _gdn_branch(x3, sh, sc, norm_g, w_qkv, w_gates, w_extra, extra_dtypes, conv_w, par, grid_w, use_rows, tm, s0):
    q, k, v, go, rows, *extra = _inconv_call(x3, sh, sc, norm_g, w_qkv, w_gates, w_extra, extra_dtypes, conv_w, par,
                                             grid_w, use_rows, tm)
    o_f, o_b, s_fin = _gdn_call(q, k, v, go, rows, s0)
    return extra, o_f, o_b, s_fin


def kernel(x, c, ctx, c_ctx, w_mod, b_mod, norm1_g, norm2_g, w_in, conv_w, a_log, dt_bias, gdn_norm_g,
           w_fourier_out, w_gdn_out, w_merge_out, w_router, b_router, w_gate, b_gate, w_up, b_up,
           w_down, b_down, final_norm_g):
    b, l, d = x.shape
    n = b * l
    n_ctx = ctx.shape[1]
    assert w_mod.shape[0] == 1 and l == GRID_W * GRID_W and d == V_DIM

    c8 = jnp.concatenate([c, c_ctx[None, :], jnp.zeros((8 - b - 1, d), F32)], axis=0)
    mod = _mod_call(c8, w_mod[0], b_mod[0])
    sh1, sc1, g1, sh2, sc2, g2 = [mod[:b, j * d:(j + 1) * d].reshape(b, 1, d) for j in range(6)]
    csh1 = jnp.broadcast_to(mod[b:b + 1, 0:d].reshape(1, 1, d), (b, 1, d))
    csc1 = jnp.broadcast_to(mod[b:b + 1, d:2 * d].reshape(1, 1, d), (b, 1, d))

    off_gate = QKV_DIM
    off_z = off_gate + 4 * NV_HEADS
    off_f = off_z + V_DIM
    off_ga = off_f + F_DIM
    w_qkv, w_gates, *w_extra = _wsplit_call(w_in, (0, off_gate, off_z, off_f, off_ga, w_in.shape[2]))
    par = jnp.pad(jnp.stack([a_log[0].reshape(-1), dt_bias[0].reshape(-1)]),
                  ((0, 6), (GATE_LANE0, LANES - 2 * GATE_LANE0)))
    n1 = norm1_g[0].reshape(1, d)
    cw = conv_w[0].reshape(9, QKV_DIM)

    zero_state = jnp.zeros((b, 2, NV_HEADS, HEAD_DIM, HEAD_DIM), F32)
    _, _, _, s_ctx = _gdn_branch(ctx, csh1, csc1, n1, w_qkv, w_gates, [], [], cw, par, n_ctx, False, n_ctx,
                                 zero_state)

    x2 = x.reshape(n, d)
    (z, f, gab), o_f, o_b, _ = _gdn_branch(x, sh1, sc1, n1, w_qkv, w_gates, w_extra, (BF16, F32, BF16),
                                           cw, par, GRID_W, True, TOKEN_TILE, s_ctx)
    z, gab = z.reshape(n, V_DIM), gab.reshape(n, 2 * d)
    fmix = _fnet_call(f).reshape(n, F_DIM)

    wr = jnp.pad(w_router[0], ((0, 0), (0, LANES - N_EXPERTS)))
    br = jnp.pad(b_router[0], (0, LANES - N_EXPERTS), constant_values=NEG_BIG).reshape(1, LANES)
    x1, h2, logits = _merge_call(
        o_f.reshape(n, V_DIM), o_b.reshape(n, V_DIM), z, fmix, gab, x2, g1, sh2, sc2,
        gdn_norm_g[0].reshape(1, HEAD_DIM), norm2_g[0].reshape(1, d),
        w_gdn_out[0].astype(BF16), w_fourier_out[0].astype(BF16), w_merge_out[0].astype(BF16), wr, br, l, TOKEN_TILE)

    top_idx, top_w, rank, counts = _route_call(logits, TOKEN_TILE)
    block_e, n_valid, next_e, parity, rows_used, dest_flat, n_slots = _routing_tables(
        top_idx, rank, counts[0, :N_EXPERTS], n)
    xs = _sc_scatter_rows(h2, dest_flat, n_slots)
    ys = _expert_call(block_e, n_valid, next_e, parity, rows_used, xs, w_gate[0], w_up[0], w_down[0],
                      b_gate[0], b_up[0], b_down[0])
    y4 = _sc_gather_rows(ys, dest_flat)
    out = _combine_call(y4, top_w, x1, g2, final_norm_g.reshape(1, d), l, TOKEN_TILE)
    return out.reshape(b, l, d)
```

```python
import functools
import math

import jax
import jax.numpy as jnp
import numpy as np
from jax import lax
from jax.experimental import pallas as pl
from jax.experimental.pallas import tpu as pltpu
from jax.experimental.pallas import tpu_sc as plsc

F32 = jnp.float32
BF16 = jnp.bfloat16
I32 = jnp.int32
HIGHEST = lax.Precision.HIGHEST

GRID_W = 64
NQK_HEADS = 4
NV_HEADS = 8
HEAD_DIM = 128
QK_DIM = NQK_HEADS * HEAD_DIM
V_DIM = NV_HEADS * HEAD_DIM
QKV_DIM = 2 * QK_DIM + V_DIM
F_GROUPS = 4
F_DIM = F_GROUPS * HEAD_DIM
N_EXPERTS = 32
TOP_K = 4
SWIGLU_ALPHA = 1.702
SWIGLU_LIMIT = 7.0
EPS = 1e-6

LANES = 128
SUBLANES = 8
GATE_LANE0 = 16
GDN_CHUNK = 128
EXPERT_BLOCK = 512
TOKEN_TILE = 512
SC_CORES = 2
SC_SUBCORES = 16
SC_WORKERS = SC_CORES * SC_SUBCORES
SC_WINDOW = 64
NEG_BIG = -1e30
MIB = 2 ** 20


def _cparams(sem, vmem_mib):
    return pltpu.CompilerParams(dimension_semantics=sem, vmem_limit_bytes=vmem_mib * MIB)


def _mm(a, b, prec=None, dims=(((1,), (0,)), ((), ()))):
    if prec is None:
        return lax.dot_general(a.astype(BF16), b.astype(BF16), dims, preferred_element_type=F32)
    return lax.dot_general(a.astype(F32), b.astype(F32), dims, precision=prec, preferred_element_type=F32)


def _sigmoid(x):
    return 0.5 * jnp.tanh(0.5 * x) + 0.5


def _rmsnorm(x, g):
    return x * lax.rsqrt(jnp.mean(x * x, axis=-1, keepdims=True) + EPS) * g


def _pack_bf16_pairs(x):
    half = x.shape[1] // 2
    bits = lax.bitcast_convert_type(x.astype(BF16).astype(F32), jnp.uint32)
    packed = jnp.bitwise_or(jnp.right_shift(bits[:, :half], jnp.uint32(16)),
                            jnp.bitwise_and(bits[:, half:], jnp.uint32(0xFFFF0000)))
    return lax.bitcast_convert_type(packed, I32)


def _unpack_bf16_pairs(p):
    bits = lax.bitcast_convert_type(p, jnp.uint32)
    lo = lax.bitcast_convert_type(jnp.left_shift(bits, jnp.uint32(16)), F32)
    hi = lax.bitcast_convert_type(jnp.bitwise_and(bits, jnp.uint32(0xFFFF0000)), F32)
    return jnp.concatenate([lo, hi], axis=1)


def _mod_kernel(c_ref, w_ref, b_ref, o_ref):
    c = c_ref[...]
    o_ref[...] = _mm(c * _sigmoid(c), w_ref[...], HIGHEST) + b_ref[...]


def _mod_call(c8, w_mod, b_mod):
    d, n = w_mod.shape
    tn = 1536
    return pl.pallas_call(
        _mod_kernel,
        grid=(n // tn,),
        in_specs=[pl.BlockSpec((8, d), lambda j: (0, 0)),
                  pl.BlockSpec((d, tn), lambda j: (0, j)),
                  pl.BlockSpec((1, tn), lambda j: (0, j))],
        out_specs=pl.BlockSpec((8, tn), lambda j: (0, j)),
        out_shape=jax.ShapeDtypeStruct((8, n), F32),
        compiler_params=_cparams(("parallel",), 32),
        name="mod",
    )(c8, w_mod, b_mod.reshape(1, n))


def _wsplit_kernel(w_ref, *o_refs, bounds):
    for o_ref, lo, hi in zip(o_refs, bounds[:-1], bounds[1:]):
        o_ref[:, :hi - lo] = w_ref[0, :, lo:hi].astype(o_ref.dtype)
        if o_ref.shape[1] > hi - lo:
            o_ref[:, hi - lo:] = jnp.zeros((o_ref.shape[0], o_ref.shape[1] - (hi - lo)), o_ref.dtype)


def _wsplit_call(w, bounds):
    d = w.shape[1]
    tr = 256
    widths = [-(-(hi - lo) // LANES) * LANES for lo, hi in zip(bounds[:-1], bounds[1:])]
    return pl.pallas_call(
        functools.partial(_wsplit_kernel, bounds=bounds),
        grid=(d // tr,),
        in_specs=[pl.BlockSpec((1, tr, w.shape[2]), lambda i: (0, i, 0))],
        out_specs=[pl.BlockSpec((tr, wd), lambda i: (i, 0)) for wd in widths],
        out_shape=[jax.ShapeDtypeStruct((d, wd), BF16) for wd in widths],
        compiler_params=_cparams(("parallel",), 32),
        name="wsplit",
    )(w)


def _inconv_kernel(*refs, grid_w, use_rows, tm, cw, n_extra):
    refs = list(refs)
    prev_ref = refs.pop(0) if use_rows else None
    x_ref = refs.pop(0)
    next_ref = refs.pop(0) if use_rows else None
    sh_ref, sc_ref, g_ref, wq_ref, wgate_ref = refs[:5]
    wx_refs = refs[5:5 + n_extra]
    w_ref, par_ref, q_ref, k_ref, v_ref, go_ref, gr_ref = refs[5 + n_extra:12 + n_extra]
    ox_refs = refs[12 + n_extra:]

    def modulated(x):
        return (_rmsnorm(x, g_ref[...]) * (1.0 + sc_ref[0]) + sh_ref[0]).astype(BF16)

    r = pl.program_id(1)
    nr = pl.num_programs(1)
    u = modulated(x_ref[0])

    def project(wx_ref, o_ref, c0, step):
        o_ref[0, :, c0:c0 + step] = _mm(u, wx_ref[:, c0:c0 + step]).astype(o_ref.dtype)

    plain = [functools.partial(project, wx_ref, o_ref, c0, min(wx_ref.shape[1], 512))
             for wx_ref, o_ref in zip(wx_refs, ox_refs) for c0 in range(0, wx_ref.shape[1], min(wx_ref.shape[1], 512))]
    n_conv_chunks = QKV_DIM // cw

    t = lax.broadcasted_iota(I32, (tm, 1), 0)
    col = jnp.bitwise_and(t, grid_w - 1)
    m_left = (col != 0).astype(F32)
    m_right = (col != grid_w - 1).astype(F32)
    if use_rows:
        u_prev = modulated(prev_ref[0])
        u_next = modulated(next_ref[0])
        has_prev = (r > 0).astype(F32)
        has_next = (r < nr - 1).astype(F32)
    for c0 in range(0, QKV_DIM, cw):
        wq = wq_ref[:, c0:c0 + cw]
        xm = _mm(u, wq)
        if use_rows:
            up = jnp.concatenate([_mm(u_prev, wq) * has_prev, xm[:tm - grid_w]], axis=0)
            dn = jnp.concatenate([xm[grid_w:], _mm(u_next, wq) * has_next], axis=0)

        def colsum(kc):
            y = xm * w_ref[3 + kc:4 + kc, c0:c0 + cw]
            if use_rows:
                y = y + up * w_ref[kc:kc + 1, c0:c0 + cw] + dn * w_ref[6 + kc:7 + kc, c0:c0 + cw]
            return y

        acc = (colsum(1) + pltpu.roll(colsum(0), 1, axis=0) * m_left
               + pltpu.roll(colsum(2), tm - 1, axis=0) * m_right)
        s = acc * _sigmoid(acc)
        for h0 in range(0, cw, HEAD_DIM):
            c = c0 + h0
            seg = s[:, h0:h0 + HEAD_DIM]
            if c < 2 * QK_DIM:
                seg = seg * lax.rsqrt(jnp.sum(seg * seg, axis=-1, keepdims=True) + EPS)
            if c < QK_DIM:
                q_ref[0, :, c:c + HEAD_DIM] = (seg * HEAD_DIM ** -0.5).astype(q_ref.dtype)
            elif c < 2 * QK_DIM:
                k_ref[0, :, c - QK_DIM:c - QK_DIM + HEAD_DIM] = seg.astype(k_ref.dtype)
            else:
                v_ref[0, :, c - 2 * QK_DIM:c - 2 * QK_DIM + HEAD_DIM] = seg.astype(v_ref.dtype)
        ci = c0 // cw
        for job in plain[ci * len(plain) // n_conv_chunks:(ci + 1) * len(plain) // n_conv_chunks]:
            job()
    g = _mm(u, wgate_ref[...])
    a = g + par_ref[1:2, :]
    softplus = jnp.maximum(a, 0.0) + jnp.log1p(jnp.exp(-jnp.abs(a)))
    log_g = -jnp.exp(par_ref[0:1, :]) * softplus
    lane = lax.broadcasted_iota(I32, g.shape, 1)
    go_ref[0] = jnp.where(lane < GATE_LANE0, _sigmoid(g), log_g)
    gr_ref[0] = jnp.transpose(log_g)[GATE_LANE0:GATE_LANE0 + 2 * NV_HEADS]


def _inconv_call(x3, sh, sc, g, w_qkv, w_gates, w_extra, extra_dtypes, conv_w, par, grid_w, use_rows, tm):
    b, t, d = x3.shape
    kern = functools.partial(_inconv_kernel, grid_w=grid_w, use_rows=use_rows, tm=tm, cw=512, n_extra=len(w_extra))
    per = tm // grid_w
    nrow = t // grid_w
    tile = lambda wd: pl.BlockSpec((1, tm, wd), lambda i, r: (i, r, 0))
    vec = pl.BlockSpec((1, 1, d), lambda i, r: (i, 0, 0))
    const = lambda a: pl.BlockSpec(a.shape, lambda i, r: (0,) * a.ndim, pipeline_mode=pl.Buffered(1))
    in_specs, args = [], []
    if use_rows:
        in_specs.append(pl.BlockSpec((1, grid_w, d), lambda i, r: (i, jnp.maximum(r * per - 1, 0), 0)))
        args.append(x3)
    in_specs.append(tile(d))
    args.append(x3)
    if use_rows:
        in_specs.append(pl.BlockSpec((1, grid_w, d), lambda i, r: (i, jnp.minimum((r + 1) * per, nrow - 1), 0)))
        args.append(x3)
    consts = [g, w_qkv, w_gates, *w_extra, conv_w, par]
    in_specs += [vec, vec] + [const(a) for a in consts]
    args += [sh, sc] + consts
    widths = [QK_DIM, QK_DIM, V_DIM, LANES] + [w.shape[1] for w in w_extra]
    dtypes = [BF16, BF16, BF16, F32] + list(extra_dtypes)
    out_shape = [jax.ShapeDtypeStruct((b, t, wd), dt) for wd, dt in zip(widths, dtypes)]
    out_specs = [tile(wd) for wd in widths]
    out_shape.insert(4, jax.ShapeDtypeStruct((b, 2 * NV_HEADS, t), F32))
    out_specs.insert(4, pl.BlockSpec((1, 2 * NV_HEADS, tm), lambda i, r: (i, 0, r)))
    return pl.pallas_call(
        kern, grid=(b, t // tm), in_specs=in_specs, out_specs=out_specs, out_shape=out_shape,
        compiler_params=_cparams(("parallel", "parallel"), 56),
        name="inconv_rows" if use_rows else "inconv_seq",
    )(*args)


def _gdn_kernel(qf, kf, vf, gf, rf, qb, kb, vb, gb, rb, s0_ref, of, ob, sfin_ref, s_ref):
    i = pl.program_id(1)
    nc = pl.num_programs(1)

    @pl.when(i == 0)
    def _():
        s_ref[...] = s0_ref[0]

    c = qf.shape[1]
    per = NV_HEADS // NQK_HEADS
    row = lax.broadcasted_iota(I32, (c, c), 0)
    colj = lax.broadcasted_iota(I32, (c, c), 1)
    eye = jnp.where(row == colj, 1.0, 0.0)
    nt_dims = (((1,), (1,)), ((), ()))
    tn_dims = (((0,), (0,)), ((), ()))

    seqs = []
    for d, (q_r, k_r, v_r, g_r, r_r, o_r) in enumerate(((qf, kf, vf, gf, rf, of), (qb, kb, vb, gb, rb, ob))):
        rev = d == 1
        incl = (colj >= row) if rev else (colj <= row)
        strict = (colj > row) if rev else (colj < row)
        gates = g_r[0]
        tri_c = jnp.where(incl, 1.0, 0.0)
        gcm = _mm(tri_c, gates, HIGHEST)
        gcr = _mm(r_r[0], tri_c, HIGHEST, dims=nt_dims)
        for hq in range(NQK_HEADS):
            q = q_r[0, :, hq * HEAD_DIM:(hq + 1) * HEAD_DIM]
            k = k_r[0, :, hq * HEAD_DIM:(hq + 1) * HEAD_DIM]
            kq = lax.dot_general(jnp.concatenate([q, k], axis=0), k, nt_dims, preferred_element_type=F32)
            for j in range(per):
                h = hq * per + j
                idx = d * NV_HEADS + h
                gc_c = gcm[:, GATE_LANE0 + idx:GATE_LANE0 + idx + 1]
                seqs.append(dict(d=d, h=h, o_r=o_r, v_r=v_r, q=q, k=k, qk=kq[:c], kk=kq[c:], incl=incl, strict=strict,
                                 beta=gates[:, idx:idx + 1], gc_c=gc_c, gc_r=gcr[idx:idx + 1, :],
                                 ge=gc_c[0:1] if rev else gc_c[c - 1:c]))

    def same_block(m):
        sh = int(math.log2(m))
        return jnp.right_shift(row, sh) == jnp.right_shift(colj, sh)

    for s in seqs:
        s['decay'] = jnp.where(s['incl'], jnp.exp(jnp.where(s['incl'], s['gc_c'] - s['gc_r'], 0.0)), 0.0)
        s['a'] = jnp.where(s['strict'], s['beta'] * s['kk'] * s['decay'], 0.0)
        s['t'] = eye - jnp.where(same_block(2), s['a'], 0.0)
    m = 4
    while m <= c:
        between = jnp.logical_and(same_block(m), jnp.logical_not(same_block(m // 2)))
        for s in seqs:
            s['te'] = _mm(s['t'], jnp.where(between, s['a'], 0.0))
        for s in seqs:
            s['t'] = s['t'] - _mm(s['te'], s['t'])
        m *= 2
    for s in seqs:
        h = s['h']
        egc = jnp.exp(s['gc_c'])
        kf32 = s['k'].astype(F32)
        v = s['v_r'][0, :, h * HEAD_DIM:(h + 1) * HEAD_DIM].astype(F32)
        rhs = jnp.concatenate([s['beta'] * v, (s['beta'] * egc) * kf32], axis=1)
        s['sol'] = _mm(s['t'], rhs)
        s['q_dec'] = s['q'].astype(F32) * egc
        s['k_dec'] = kf32 * jnp.exp(s['ge'] - s['gc_c'])
    for s in seqs:
        s['ws'] = _mm(jnp.concatenate([s['sol'][:, HEAD_DIM:], s['q_dec']], axis=0), s_ref[s['d'], s['h']])
    for s in seqs:
        s['u'] = s['sol'][:, :HEAD_DIM] - s['ws'][:c]
        s_ref[s['d'], s['h']] = (jnp.exp(s['ge']) * s_ref[s['d'], s['h']]
                                 + _mm(s['k_dec'], s['u'], dims=tn_dims))
    for s in seqs:
        h = s['h']
        o = s['ws'][c:] + _mm(s['qk'] * s['decay'], s['u'])
        s['o_r'][0, :, h * HEAD_DIM:(h + 1) * HEAD_DIM] = o.astype(s['o_r'].dtype)

    @pl.when(i == nc - 1)
    def _():
        sfin_ref[0] = s_ref[...]


def _gdn_call(q, k, v, go, rows, s0):
    b, t, _ = q.shape
    c = GDN_CHUNK
    nc = t // c
    fwd = lambda i, n: (i, n, 0)
    bwd = lambda i, n: (i, nc - 1 - n, 0)
    rfwd = lambda i, n: (i, 0, n)
    rbwd = lambda i, n: (i, 0, nc - 1 - n)
    state_spec = pl.BlockSpec((1,) + s0.shape[1:], lambda i, n: (i, 0, 0, 0, 0))

    def specs(m3, mr):
        return [pl.BlockSpec((1, c, QK_DIM), m3), pl.BlockSpec((1, c, QK_DIM), m3),
                pl.BlockSpec((1, c, V_DIM), m3), pl.BlockSpec((1, c, LANES), m3),
                pl.BlockSpec((1, 2 * NV_HEADS, c), mr)]

    return pl.pallas_call(
        _gdn_kernel,
        grid=(b, nc),
        in_specs=specs(fwd, rfwd) + specs(bwd, rbwd) + [state_spec],
        out_specs=[pl.BlockSpec((1, c, V_DIM), fwd), pl.BlockSpec((1, c, V_DIM), bwd), state_spec],
        out_shape=[jax.ShapeDtypeStruct((b, t, V_DIM), BF16), jax.ShapeDtypeStruct((b, t, V_DIM), BF16),
                   jax.ShapeDtypeStruct(s0.shape, F32)],
        scratch_shapes=[pltpu.VMEM(s0.shape[1:], F32)],
        compiler_params=_cparams(("parallel", "arbitrary"), 48),
        name="gdn",
    )(q, k, v, go, rows, q, k, v, go, rows, s0)


def _fnet1_kernel(x_ref, f_ref, ar_ref, ai_ref):
    n = x_ref.shape[1]
    for j in range(x_ref.shape[2]):
        a = _mm(f_ref[...], x_ref[0, :, j, :])
        ar_ref[0, j] = a[:n]
        ai_ref[0, j] = a[n:]


def _fnet2_kernel(ar_ref, ai_ref, g_ref, wc_ref, o_ref, *, scale):
    n = ar_ref.shape[1]
    cols = ar_ref.shape[2]
    zs = []
    for m in range(cols):
        a2 = jnp.concatenate([ar_ref[0, :, m, :], ai_ref[0, :, m, :]], axis=0)
        zs.append(_mm(g_ref[m], a2))
    zr = jnp.concatenate([z[:n] for z in zs], axis=0)
    zi = jnp.concatenate([z[n:] for z in zs], axis=0)
    for g0 in range(0, zr.shape[1], HEAD_DIM):
        y = _mm(jnp.concatenate([zr[:, g0:g0 + HEAD_DIM], zi[:, g0:g0 + HEAD_DIM]], axis=1), wc_ref[...])
        for m in range(cols):
            o_ref[0, :, m, g0:g0 + HEAD_DIM] = y[m * n:(m + 1) * n] * scale


def _fnet_tables(n):
    a = np.arange(n)
    ang1 = 2.0 * np.pi * np.outer(a, a) / n
    f1 = np.concatenate([np.cos(ang1), -np.sin(ang1)], axis=0)
    m = a[:, None] + n * a[None, :]
    ang2 = 2.0 * np.pi * ((m[:, :, None] * a[None, None, :]) % (n * n)) / (n * n)
    gc, gs = np.cos(ang2), np.sin(ang2)
    g2 = np.concatenate([np.concatenate([gc, gs], axis=2), np.concatenate([-gs, gc], axis=2)], axis=1)
    angc = 2.0 * np.pi * np.outer(np.arange(HEAD_DIM), np.arange(HEAD_DIM)) / HEAD_DIM
    wc = np.concatenate([np.cos(angc), np.sin(angc)], axis=0)
    f = lambda x: jnp.asarray(x, F32).astype(BF16)
    return f(f1), f(g2), f(wc)


def _fnet_call(f):
    b, l, c = f.shape
    n = GRID_W
    assert l == n * n
    f1, g2, wc = _fnet_tables(n)
    cols = 4 * SUBLANES
    ar, ai = pl.pallas_call(
        _fnet1_kernel,
        grid=(b, n // cols),
        in_specs=[pl.BlockSpec((1, n, cols, c), lambda i, j: (i, 0, j, 0)),
                  pl.BlockSpec((2 * n, n), lambda i, j: (0, 0))],
        out_specs=[pl.BlockSpec((1, cols, n, c), lambda i, j: (i, j, 0, 0))] * 2,
        out_shape=[jax.ShapeDtypeStruct((b, n, n, c), F32)] * 2,
        compiler_params=_cparams(("parallel", "parallel"), 32),
        name="fnet1",
    )(f.reshape(b, n, n, c), f1)
    out = pl.pallas_call(
        functools.partial(_fnet2_kernel, scale=1.0 / math.sqrt(l * HEAD_DIM)),
        grid=(b, n // cols),
        in_specs=[pl.BlockSpec((1, n, cols, c), lambda i, j: (i, 0, j, 0)),
                  pl.BlockSpec((1, n, cols, c), lambda i, j: (i, 0, j, 0)),
                  pl.BlockSpec((cols, 2 * n, 2 * n), lambda i, j: (j, 0, 0)),
                  pl.BlockSpec((2 * HEAD_DIM, HEAD_DIM), lambda i, j: (0, 0))],
        out_specs=pl.BlockSpec((1, n, cols, c), lambda i, j: (i, 0, j, 0)),
        out_shape=jax.ShapeDtypeStruct((b, n, n, c), F32),
        compiler_params=_cparams(("parallel", "parallel"), 32),
        name="fnet2",
    )(ar, ai, g2, wc)
    return out.reshape(b, l, c)


def _merge_kernel(of_ref, ob_ref, z_ref, fm_ref, gab_ref, x_ref, g1_ref, sh2_ref, sc2_ref, gn_ref, n2_ref,
                  wg_ref, wf_ref, wm_ref, wr_ref, br_ref, x1_ref, h2_ref, lg_ref):
    d = x_ref.shape[1]
    tm = x_ref.shape[0]
    halves = [slice(0, tm // 2), slice(tm // 2, tm)]
    yb_in = []
    for r in halves:
        o = of_ref[r, :].astype(F32) + ob_ref[r, :].astype(F32)
        z = z_ref[r, :].astype(F32)
        parts = []
        for h0 in range(0, V_DIM, HEAD_DIM):
            oh = o[:, h0:h0 + HEAD_DIM]
            parts.append(oh * lax.rsqrt(jnp.mean(oh * oh, axis=-1, keepdims=True) + EPS) * gn_ref[...])
        yb_in.append(jnp.concatenate(parts, axis=1) * (z * _sigmoid(z)))
    yb = [_mm(v, wg_ref[...]) for v in yb_in]
    ya = [_mm(fm_ref[r, :], wf_ref[...]) for r in halves]
    mixed = [_sigmoid(gab_ref[r, :d].astype(F32)) * a + _sigmoid(gab_ref[r, d:].astype(F32)) * b_
             for r, a, b_ in zip(halves, ya, yb)]
    mm = [_mm(v, wm_ref[...]) for v in mixed]
    w = wr_ref[...]
    w_hi = w.astype(BF16)
    w2 = jnp.concatenate([w_hi, (w - w_hi.astype(F32)).astype(BF16)], axis=1)
    for r, v in zip(halves, mm):
        x1 = x_ref[r, :] + g1_ref[0] * v
        x1_ref[r, :] = x1
        h2 = _rmsnorm(x1, n2_ref[...]) * (1.0 + sc2_ref[0]) + sh2_ref[0]
        h2_ref[r, :] = _pack_bf16_pairs(h2)
        h_hi = h2.astype(BF16)
        h_lo = (h2 - h_hi.astype(F32)).astype(BF16)
        part = _mm(h_hi, w2)
        lg_ref[r, :] = (part[:, :LANES] + part[:, LANES:]) + _mm(h_lo, w_hi) + br_ref[...]


def _merge_call(of, ob, z, fm, gab, x2, g1, sh2, sc2, gn, n2, wg, wf, wm, wr, br, tokens_per_batch, tm):
    n, d = x2.shape
    per = tokens_per_batch // tm
    tok = lambda wd: pl.BlockSpec((tm, wd), lambda i: (i, 0))
    vec = pl.BlockSpec((1, 1, d), lambda i: (i // per, 0, 0))
    full = lambda a: pl.BlockSpec(a.shape, lambda i: (0,) * a.ndim)
    return pl.pallas_call(
        _merge_kernel,
        grid=(n // tm,),
        in_specs=[tok(V_DIM), tok(V_DIM), tok(V_DIM), tok(F_DIM), tok(2 * d), tok(d), vec, vec, vec,
                  full(gn), full(n2), full(wg), full(wf), full(wm), full(wr), full(br)],
        out_specs=[tok(d), tok(d // 2), tok(LANES)],
        out_shape=[jax.ShapeDtypeStruct((n, d), F32), jax.ShapeDtypeStruct((n, d // 2), I32),
                   jax.ShapeDtypeStruct((n, LANES), F32)],
        compiler_params=_cparams(("parallel",), 56),
        name="merge",
    )(of, ob, z, fm, gab, x2, g1, sh2, sc2, gn, n2, wg, wf, wm, wr, br)


def _route_kernel(lg_ref, idx_ref, w_ref, rank_ref, cnt_ref, run_ref):
    i = pl.program_id(0)

    @pl.when(i == 0)
    def _():
        run_ref[...] = jnp.zeros_like(run_ref)

    l = lg_ref[...]
    tm = l.shape[0]
    lane = lax.broadcasted_iota(I32, l.shape, 1).astype(F32)
    vals, idxs = [], []
    for _ in range(TOP_K):
        m = jnp.max(l, axis=-1, keepdims=True)
        idx = jnp.min(jnp.where(l == m, lane, float(LANES)), axis=-1, keepdims=True)
        vals.append(m)
        idxs.append(idx)
        l = jnp.where(lane == idx, NEG_BIG * 2.0, l)
    es = [jnp.exp(v - vals[0]) for v in vals]
    inv = 1.0 / (es[0] + es[1] + es[2] + es[3])
    picked = jnp.zeros(l.shape, F32)
    for idx in idxs:
        picked = picked + (lane == idx).astype(F32)
    r = lax.broadcasted_iota(I32, (tm, tm), 0)
    cidx = lax.broadcasted_iota(I32, (tm, tm), 1)
    before = _mm(jnp.where(cidx < r, 1.0, 0.0), picked) + run_ref[...]
    idx_out = jnp.zeros(l.shape, F32)
    w_out = jnp.zeros(l.shape, F32)
    rank_out = jnp.zeros(l.shape, F32)
    for k in range(TOP_K):
        rk = jnp.sum(jnp.where(lane == idxs[k], before, 0.0), axis=-1, keepdims=True)
        idx_out = jnp.where(lane == k, idxs[k], idx_out)
        w_out = jnp.where(lane == k, es[k] * inv, w_out)
        rank_out = jnp.where(lane == k, rk, rank_out)
    idx_ref[...] = jnp.transpose(idx_out)[:SUBLANES].astype(I32)
    w_ref[...] = w_out
    rank_ref[...] = jnp.transpose(rank_out)[:SUBLANES].astype(I32)
    run_ref[...] = run_ref[...] + jnp.sum(picked, axis=0, keepdims=True)
    cnt_ref[...] = run_ref[...]


def _route_call(logits, tm):
    n = logits.shape[0]
    tok = pl.BlockSpec((tm, LANES), lambda i: (i, 0))
    tok_t = pl.BlockSpec((SUBLANES, tm), lambda i: (0, i))
    return pl.pallas_call(
        _route_kernel,
        grid=(n // tm,),
        in_specs=[tok],
        out_specs=[tok_t, tok, tok_t, pl.BlockSpec((1, LANES), lambda i: (0, 0))],
        out_shape=[jax.ShapeDtypeStruct((SUBLANES, n), I32), jax.ShapeDtypeStruct((n, LANES), F32),
                   jax.ShapeDtypeStruct((SUBLANES, n), I32), jax.ShapeDtypeStruct((1, LANES), F32)],
        scratch_shapes=[pltpu.VMEM((1, LANES), F32)],
        compiler_params=_cparams(("arbitrary",), 32),
        name="route",
    )(logits)


def _sc_mesh():
    return plsc.VectorSubcoreMesh(core_axis_name="c", subcore_axis_name="s",
                                  num_cores=SC_CORES, num_subcores=SC_SUBCORES)


def _sc_worker_id():
    return lax.axis_index("s") * SC_CORES + lax.axis_index("c")


def _sc_scatter_rows(x, idx, n_out):
    n, d = x.shape
    per_w = idx.shape[0] // SC_WORKERS
    nwin = per_w // SC_WINDOW
    assert per_w * SC_WORKERS == idx.shape[0] and nwin * SC_WINDOW == per_w and nwin % 2 == 0 and n % per_w == 0

    def body(x_hbm, idx_hbm, out_hbm, idx_v, rows_v, sem_r, sem_w):
        wid = _sc_worker_id()
        row0 = lax.rem(wid * per_w, n)
        pltpu.sync_copy(idx_hbm.at[wid], idx_v)

        def read(j, b):
            return pltpu.make_async_copy(x_hbm.at[pl.ds(row0 + j * SC_WINDOW, SC_WINDOW)], rows_v.at[b], sem_r.at[b])

        def write(j, b):
            return pltpu.make_async_copy(rows_v.at[b], out_hbm.at[idx_v.at[j]], sem_w.at[b])

        read(0, 0).start()
        read(1, 1).start()

        @pl.loop(0, nwin, step=2)
        def _(j):
            read(j, 0).wait()
            write(j, 0).start()
            read(j + 1, 1).wait()
            write(j + 1, 1).start()
            write(j, 0).wait()

            @pl.when(j + 2 < nwin)
            def _():
                read(j + 2, 0).start()

            write(j + 1, 1).wait()

            @pl.when(j + 2 < nwin)
            def _():
                read(j + 3, 1).start()

    return pl.kernel(
        body, out_type=jax.ShapeDtypeStruct((n_out, d), x.dtype), mesh=_sc_mesh(),
        scratch_types=[pltpu.VMEM((nwin, SC_WINDOW), I32), pltpu.VMEM((2, SC_WINDOW, d), x.dtype),
                       pltpu.SemaphoreType.DMA((2,)), pltpu.SemaphoreType.DMA((2,))],
        name="sc_scatter_rows",
    )(x, idx.reshape(SC_WORKERS, nwin, SC_WINDOW))


def _sc_gather_rows(y, idx):
    d = y.shape[1]
    total = idx.shape[0]
    per_w = total // SC_WORKERS
    nwin = per_w // SC_WINDOW
    assert per_w * SC_WORKERS == total and nwin * SC_WINDOW == per_w and nwin % 2 == 0

    def body(y_hbm, idx_hbm, out_hbm, idx_v, rows_v, sem_r, sem_w):
        wid = _sc_worker_id()
        row0 = wid * per_w
        pltpu.sync_copy(idx_hbm.at[wid], idx_v)

        def read(j, b):
            return pltpu.make_async_copy(y_hbm.at[idx_v.at[j]], rows_v.at[b], sem_r.at[b])

        def write(j, b):
            return pltpu.make_async_copy(rows_v.at[b], out_hbm.at[pl.ds(row0 + j * SC_WINDOW, SC_WINDOW)], sem_w.at[b])

        read(0, 0).start()
        read(1, 1).start()

        @pl.loop(0, nwin, step=2)
        def _(j):
            read(j, 0).wait()
            write(j, 0).start()
            read(j + 1, 1).wait()
            write(j + 1, 1).start()
            write(j, 0).wait()

            @pl.when(j + 2 < nwin)
            def _():
                read(j + 2, 0).start()

            write(j + 1, 1).wait()

            @pl.when(j + 2 < nwin)
            def _():
                read(j + 3, 1).start()

    return pl.kernel(
        body, out_type=jax.ShapeDtypeStruct((total, d), y.dtype), mesh=_sc_mesh(),
        scratch_types=[pltpu.VMEM((nwin, SC_WINDOW), I32), pltpu.VMEM((2, SC_WINDOW, d), y.dtype),
                       pltpu.SemaphoreType.DMA((2,)), pltpu.SemaphoreType.DMA((2,))],
        name="sc_gather_rows",
    )(y, idx.reshape(SC_WORKERS, nwin, SC_WINDOW))


def _expert_kernel(be_ref, nv_ref, nxt_ref, par_ref, used_ref, x_ref, wg_hbm, wu_hbm, wd_hbm, bg_ref, bu_ref, bd_ref,
                   y_ref, wgf, wuf, wdf, wgb, wub, wdb, sem):
    i = pl.program_id(0)

    def fetch(e, s):
        return [pltpu.make_async_copy(w_hbm.at[e], w_f.at[s], sem.at[s])
                for w_hbm, w_f in ((wg_hbm, wgf), (wu_hbm, wuf), (wd_hbm, wdf))]

    @pl.when(i < nv_ref[0])
    def _():
        e = be_ref[i]
        s = par_ref[e]

        @pl.when(i == 0)
        def _():
            for cp in fetch(e, s):
                cp.start()

        first = jnp.logical_or(i == 0, be_ref[jnp.maximum(i - 1, 0)] != e)
        quarter = x_ref.shape[0] // 4
        full = used_ref[i] > 3 * quarter

        @pl.when(first)
        def _():
            for cp in fetch(e, s):
                cp.wait()

            @pl.when(nxt_ref[e] >= 0)
            def _():
                for cp in fetch(nxt_ref[e], 1 - s):
                    cp.start()

        def ffn(nrows, cast):
            x = _unpack_bf16_pairs(x_ref[:nrows, :]).astype(BF16)
            if cast:
                wgb[...] = wgf[s].astype(BF16)
            gate = jnp.minimum(_mm(x, wgb[...]) + bg_ref[0], SWIGLU_LIMIT)
            if cast:
                wub[...] = wuf[s].astype(BF16)
            up = jnp.clip(_mm(x, wub[...]) + bu_ref[0], -SWIGLU_LIMIT, SWIGLU_LIMIT)
            if cast:
                wdb[...] = wdf[s].astype(BF16)
            act = (up + 1.0) * gate * _sigmoid(SWIGLU_ALPHA * gate)
            y_ref[:nrows, :] = _pack_bf16_pairs(_mm(act, wdb[...]) + bd_ref[0])

        @pl.when(jnp.logical_and(first, full))
        def _():
            ffn(4 * quarter, True)

        @pl.when(jnp.logical_and(first, jnp.logical_not(full)))
        def _():
            wgb[...] = wgf[s].astype(BF16)
            wub[...] = wuf[s].astype(BF16)
            wdb[...] = wdf[s].astype(BF16)

        for nq in range(1, 5):
            in_range = jnp.logical_and(used_ref[i] > (nq - 1) * quarter, used_ref[i] <= nq * quarter)
            if nq == 4:
                in_range = jnp.logical_and(in_range, jnp.logical_not(first))

            @pl.when(in_range)
            def _(nq=nq):
                ffn(nq * quarter, False)


def _expert_call(block_e, n_valid, next_e, parity, used, xs, w_gate, w_up, w_down, b_gate, b_up, b_down):
    nb = block_e.shape[0]
    tmb = EXPERT_BLOCK
    ne, d, de = w_gate.shape
    bspec = lambda s: pl.BlockSpec((1,) + s, lambda i, be, nv, nx, pa, us: (be[i], 0, 0))
    anyspec = pl.BlockSpec(memory_space=pl.ANY)
    grid_spec = pltpu.PrefetchScalarGridSpec(
        num_scalar_prefetch=5,
        grid=(nb,),
        in_specs=[pl.BlockSpec((tmb, d // 2), lambda i, be, nv, nx, pa, us: (i, 0)),
                  anyspec, anyspec, anyspec, bspec((1, de)), bspec((1, de)), bspec((1, d))],
        out_specs=pl.BlockSpec((tmb, d // 2), lambda i, be, nv, nx, pa, us: (i, 0)),
        scratch_shapes=[pltpu.VMEM((2, d, de), F32), pltpu.VMEM((2, d, de), F32), pltpu.VMEM((2, de, d), F32),
                        pltpu.VMEM((d, de), BF16), pltpu.VMEM((d, de), BF16), pltpu.VMEM((de, d), BF16),
                        pltpu.SemaphoreType.DMA((2,))],
    )
    return pl.pallas_call(
        _expert_kernel,
        grid_spec=grid_spec,
        out_shape=jax.ShapeDtypeStruct(xs.shape, xs.dtype),
        compiler_params=_cparams(("arbitrary",), 56),
        name="expert",
    )(block_e, n_valid, next_e, parity, used, xs, w_gate, w_up, w_down,
      b_gate.reshape(ne, 1, de), b_up.reshape(ne, 1, de), b_down.reshape(ne, 1, d))


def _combine_kernel(y0, y1, y2, y3, w_ref, x1_ref, g2_ref, fg_ref, o_ref):
    w = w_ref[...]
    ys = [_unpack_bf16_pairs(y[...]) for y in (y0, y1, y2, y3)]
    moe = (w[:, 0:1] * ys[0] + w[:, 1:2] * ys[1]) + (w[:, 2:3] * ys[2] + w[:, 3:4] * ys[3])
    o_ref[...] = _rmsnorm(x1_ref[...] + g2_ref[0] * moe, fg_ref[...])


def _combine_call(y4, top_w, x1, g2, fg, tokens_per_batch, tm):
    n, d = x1.shape
    per = tokens_per_batch // tm
    nt = n // tm
    yspec = lambda k: pl.BlockSpec((tm, d // 2), lambda i: (k * nt + i, 0))
    return pl.pallas_call(
        _combine_kernel,
        grid=(nt,),
        in_specs=[yspec(0), yspec(1), yspec(2), yspec(3),
                  pl.BlockSpec((tm, LANES), lambda i: (i, 0)),
                  pl.BlockSpec((tm, d), lambda i: (i, 0)),
                  pl.BlockSpec((1, 1, d), lambda i: (i // per, 0, 0)),
                  pl.BlockSpec((1, d), lambda i: (0, 0))],
        out_specs=pl.BlockSpec((tm, d), lambda i: (i, 0)),
        out_shape=jax.ShapeDtypeStruct((n, d), F32),
        compiler_params=_cparams(("parallel",), 48),
        name="combine",
    )(y4, y4, y4, y4, top_w, x1, g2, fg)


def _slot_kernel(start_ref, idx_ref, rank_ref, o_ref):
    idx = idx_ref[...]
    acc = rank_ref[...]
    for e in range(N_EXPERTS):
        acc = acc + jnp.where(idx == e, start_ref[e], 0)
    o_ref[...] = acc


def _slot_call(pad_start, top_idx, rank):
    full = pl.BlockSpec(top_idx.shape, lambda i, s: (0, 0))
    return pl.pallas_call(
        _slot_kernel,
        grid_spec=pltpu.PrefetchScalarGridSpec(num_scalar_prefetch=1, grid=(1,), in_specs=[full, full],
                                               out_specs=full),
        out_shape=jax.ShapeDtypeStruct(top_idx.shape, I32),
        compiler_params=_cparams(("arbitrary",), 32),
        name="slots",
    )(pad_start, top_idx, rank)


def _routing_tables(top_idx, rank, counts, n):
    tmb = EXPERT_BLOCK
    counts = counts.astype(I32)
    padded = (counts + tmb - 1) // tmb * tmb
    pad_end = jnp.cumsum(padded)
    pad_start = pad_end - padded
    n_blocks = -(-(n * TOP_K + N_EXPERTS * (tmb - 1)) // tmb)
    n_slots = n_blocks * tmb
    dest_flat = _slot_call(pad_start, top_idx, rank)[:TOP_K].reshape(-1)
    block_start = jnp.arange(n_blocks, dtype=I32) * tmb
    block_e = jnp.minimum(jnp.sum((pad_end[None, :] <= block_start[:, None]).astype(I32), axis=1), N_EXPERTS - 1)
    n_valid = (pad_end[-1:] // tmb).astype(I32)
    experts = jnp.arange(N_EXPERTS, dtype=I32)
    used = counts > 0
    later = jnp.where(jnp.logical_and(used[None, :], experts[None, :] > experts[:, None]), experts[None, :], N_EXPERTS)
    next_e = jnp.min(later, axis=1)
    next_e = jnp.where(next_e == N_EXPERTS, -1, next_e).astype(I32)
    parity = ((jnp.cumsum(used.astype(I32)) - used.astype(I32)) % 2).astype(I32)
    row_end = jnp.sum(jnp.where(block_e[:, None] == experts[None, :], (pad_start + counts)[None, :], 0), axis=1)
    rows_used = jnp.clip(row_end - block_start, 0, tmb).astype(I32)
    return block_e, n_valid, next_e, parity, rows_used, dest_flat, n_slots


def _gdn_branch(x3, sh, sc, norm_g, w_qkv, w_gates, w_extra, extra_dtypes, conv_w, par, grid_w, use_rows, tm, s0):
    q, k, v, go, rows, *extra = _inconv_call(x3, sh, sc, norm_g, w_qkv, w_gates, w_extra, extra_dtypes, conv_w, par,
                                             grid_w, use_rows, tm)
    o_f, o_b, s_fin = _gdn_call(q, k, v, go, rows, s0)
    return extra, o_f, o_b, s_fin


def kernel(x, c, ctx, c_ctx, w_mod, b_mod, norm1_g, norm2_g, w_in, conv_w, a_log, dt_bias, gdn_norm_g,
           w_fourier_out, w_gdn_out, w_merge_out, w_router, b_router, w_gate, b_gate, w_up, b_up,
           w_down, b_down, final_norm_g):
    b, l, d = x.shape
    n = b * l
    n_ctx = ctx.shape[1]
    assert w_mod.shape[0] == 1 and l == GRID_W * GRID_W and d == V_DIM

    c8 = jnp.concatenate([c, c_ctx[None, :], jnp.zeros((8 - b - 1, d), F32)], axis=0)
    mod = _mod_call(c8, w_mod[0], b_mod[0])
    sh1, sc1, g1, sh2, sc2, g2 = [mod[:b, j * d:(j + 1) * d].reshape(b, 1, d) for j in range(6)]
    csh1 = jnp.broadcast_to(mod[b:b + 1, 0:d].reshape(1, 1, d), (b, 1, d))
    csc1 = jnp.broadcast_to(mod[b:b + 1, d:2 * d].reshape(1, 1, d), (b, 1, d))

    off_gate = QKV_DIM
    off_z = off_gate + 4 * NV_HEADS
    off_f = off_z + V_DIM
    off_ga = off_f + F_DIM
    w_qkv, w_gates, *w_extra = _wsplit_call(w_in, (0, off_gate, off_z, off_f, off_ga, w_in.shape[2]))
    par = jnp.pad(jnp.stack([a_log[0].reshape(-1), dt_bias[0].reshape(-1)]),
                  ((0, 6), (GATE_LANE0, LANES - 2 * GATE_LANE0)))
    n1 = norm1_g[0].reshape(1, d)
    cw = conv_w[0].reshape(9, QKV_DIM)

    zero_state = jnp.zeros((b, 2, NV_HEADS, HEAD_DIM, HEAD_DIM), F32)
    _, _, _, s_ctx = _gdn_branch(ctx, csh1, csc1, n1, w_qkv, w_gates, [], [], cw, par, n_ctx, False, n_ctx,
                                 zero_state)

    x2 = x.reshape(n, d)
    (z, f, gab), o_f, o_b, _ = _gdn_branch(x, sh1, sc1, n1, w_qkv, w_gates, w_extra, (BF16, F32, BF16),
                                           cw, par, GRID_W, True, TOKEN_TILE, s_ctx)
    z, gab = z.reshape(n, V_DIM), gab.reshape(n, 2 * d)
    fmix = _fnet_call(f).reshape(n, F_DIM)

    wr = jnp.pad(w_router[0], ((0, 0), (0, LANES - N_EXPERTS)))
    br = jnp.pad(b_router[0], (0, LANES - N_EXPERTS), constant_values=NEG_BIG).reshape(1, LANES)
    x1, h2, logits = _merge_call(
        o_f.reshape(n, V_DIM), o_b.reshape(n, V_DIM), z, fmix, gab, x2, g1, sh2, sc2,
        gdn_norm_g[0].reshape(1, HEAD_DIM), norm2_g[0].reshape(1, d),
        w_gdn_out[0].astype(BF16), w_fourier_out[0].astype(BF16), w_merge_out[0].astype(BF16), wr, br, l, TOKEN_TILE)

    top_idx, top_w, rank, counts = _route_call(logits, TOKEN_TILE)
    block_e, n_valid, next_e, parity, rows_used, dest_flat, n_slots = _routing_tables(
        top_idx, rank, counts[0, :N_EXPERTS], n)
    xs = _sc_scatter_rows(h2, dest_flat, n_slots)
    ys = _expert_call(block_e, n_valid, next_e, parity, rows_used, xs, w_gate[0], w_up[0], w_down[0],
                      b_gate[0], b_up[0], b_down[0])
    y4 = _sc_gather_rows(ys, dest_flat)
    out = _combine_call(y4, top_w, x1, g2, final_norm_g.reshape(1, d), l, TOKEN_TILE)
    return out.reshape(b, l, d)
```

```python
import functools
import math

import jax
import jax.numpy as jnp
import numpy as np
from jax import lax
from jax.experimental import pallas as pl
from jax.experimental.pallas import tpu as pltpu
from jax.experimental.pallas import tpu_sc as plsc

F32 = jnp.float32
BF16 = jnp.bfloat16
I32 = jnp.int32
HIGHEST = lax.Precision.HIGHEST

GRID_W = 64
NQK_HEADS = 4
NV_HEADS = 8
HEAD_DIM = 128
QK_DIM = NQK_HEADS * HEAD_DIM
V_DIM = NV_HEADS * HEAD_DIM
QKV_DIM = 2 * QK_DIM + V_DIM
F_GROUPS = 4
F_DIM = F_GROUPS * HEAD_DIM
N_EXPERTS = 32
TOP_K = 4
SWIGLU_ALPHA = 1.702
SWIGLU_LIMIT = 7.0
EPS = 1e-6

LANES = 128
SUBLANES = 8
GATE_LANE0 = 16
GDN_CHUNK = 128
EXPERT_BLOCK = 1024
TOKEN_TILE = 512
SC_CORES = 2
SC_SUBCORES = 16
SC_WORKERS = SC_CORES * SC_SUBCORES
SC_WINDOW = 64
NEG_BIG = -1e30
MIB = 2 ** 20


def _cparams(sem, vmem_mib):
    return pltpu.CompilerParams(dimension_semantics=sem, vmem_limit_bytes=vmem_mib * MIB)


def _mm(a, b, prec=None, dims=(((1,), (0,)), ((), ()))):
    if prec is None:
        return lax.dot_general(a.astype(BF16), b.astype(BF16), dims, preferred_element_type=F32)
    return lax.dot_general(a.astype(F32), b.astype(F32), dims, precision=prec, preferred_element_type=F32)


def _sigmoid(x):
    return 0.5 * jnp.tanh(0.5 * x) + 0.5


def _rmsnorm(x, g):
    return x * lax.rsqrt(jnp.mean(x * x, axis=-1, keepdims=True) + EPS) * g


def _pack_bf16_pairs(x):
    half = x.shape[1] // 2
    bits = lax.bitcast_convert_type(x.astype(BF16).astype(F32), jnp.uint32)
    packed = jnp.bitwise_or(jnp.right_shift(bits[:, :half], jnp.uint32(16)),
                            jnp.bitwise_and(bits[:, half:], jnp.uint32(0xFFFF0000)))
    return lax.bitcast_convert_type(packed, I32)


def _unpack_bf16_pairs(p):
    bits = lax.bitcast_convert_type(p, jnp.uint32)
    lo = lax.bitcast_convert_type(jnp.left_shift(bits, jnp.uint32(16)), F32)
    hi = lax.bitcast_convert_type(jnp.bitwise_and(bits, jnp.uint32(0xFFFF0000)), F32)
    return jnp.concatenate([lo, hi], axis=1)


def _mod_kernel(c_ref, w_ref, b_ref, o_ref):
    c = c_ref[...]
    o_ref[...] = _mm(c * _sigmoid(c), w_ref[...], HIGHEST) + b_ref[...]


def _mod_call(c8, w_mod, b_mod):
    d, n = w_mod.shape
    tn = 1536
    return pl.pallas_call(
        _mod_kernel,
        grid=(n // tn,),
        in_specs=[pl.BlockSpec((8, d), lambda j: (0, 0)),
                  pl.BlockSpec((d, tn), lambda j: (0, j)),
                  pl.BlockSpec((1, tn), lambda j: (0, j))],
        out_specs=pl.BlockSpec((8, tn), lambda j: (0, j)),
        out_shape=jax.ShapeDtypeStruct((8, n), F32),
        compiler_params=_cparams(("parallel",), 32),
        name="mod",
    )(c8, w_mod, b_mod.reshape(1, n))


def _wsplit_kernel(w_ref, *o_refs, bounds):
    for o_ref, lo, hi in zip(o_refs, bounds[:-1], bounds[1:]):
        o_ref[:, :hi - lo] = w_ref[0, :, lo:hi].astype(o_ref.dtype)
        if o_ref.shape[1] > hi - lo:
            o_ref[:, hi - lo:] = jnp.zeros((o_ref.shape[0], o_ref.shape[1] - (hi - lo)), o_ref.dtype)


def _wsplit_call(w, bounds):
    d = w.shape[1]
    tr = 256
    widths = [-(-(hi - lo) // LANES) * LANES for lo, hi in zip(bounds[:-1], bounds[1:])]
    return pl.pallas_call(
        functools.partial(_wsplit_kernel, bounds=bounds),
        grid=(d // tr,),
        in_specs=[pl.BlockSpec((1, tr, w.shape[2]), lambda i: (0, i, 0))],
        out_specs=[pl.BlockSpec((tr, wd), lambda i: (i, 0)) for wd in widths],
        out_shape=[jax.ShapeDtypeStruct((d, wd), BF16) for wd in widths],
        compiler_params=_cparams(("parallel",), 32),
        name="wsplit",
    )(w)


def _inconv_kernel(*refs, grid_w, use_rows, tm, cw, n_extra):
    refs = list(refs)
    prev_ref = refs.pop(0) if use_rows else None
    x_ref = refs.pop(0)
    next_ref = refs.pop(0) if use_rows else None
    sh_ref, sc_ref, g_ref, wq_ref, wgate_ref = refs[:5]
    wx_refs = refs[5:5 + n_extra]
    w_ref, par_ref, q_ref, k_ref, v_ref, go_ref, gr_ref = refs[5 + n_extra:12 + n_extra]
    ox_refs = refs[12 + n_extra:]

    def modulated(x):
        return (_rmsnorm(x, g_ref[...]) * (1.0 + sc_ref[0]) + sh_ref[0]).astype(BF16)

    r = pl.program_id(1)
    nr = pl.num_programs(1)
    u = modulated(x_ref[0])

    def project(wx_ref, o_ref, c0, step):
        o_ref[0, :, c0:c0 + step] = _mm(u, wx_ref[:, c0:c0 + step]).astype(o_ref.dtype)

    plain = [functools.partial(project, wx_ref, o_ref, c0, min(wx_ref.shape[1], 512))
             for wx_ref, o_ref in zip(wx_refs, ox_refs) for c0 in range(0, wx_ref.shape[1], min(wx_ref.shape[1], 512))]
    n_conv_chunks = QKV_DIM // cw

    t = lax.broadcasted_iota(I32, (tm, 1), 0)
    col = jnp.bitwise_and(t, grid_w - 1)
    m_left = (col != 0).astype(F32)
    m_right = (col != grid_w - 1).astype(F32)
    if use_rows:
        u_prev = modulated(prev_ref[0])
        u_next = modulated(next_ref[0])
        has_prev = (r > 0).astype(F32)
        has_next = (r < nr - 1).astype(F32)
    for c0 in range(0, QKV_DIM, cw):
        wq = wq_ref[:, c0:c0 + cw]
        xm = _mm(u, wq)
        if use_rows:
            up = jnp.concatenate([_mm(u_prev, wq) * has_prev, xm[:tm - grid_w]], axis=0)
            dn = jnp.concatenate([xm[grid_w:], _mm(u_next, wq) * has_next], axis=0)

        def colsum(kc):
            y = xm * w_ref[3 + kc:4 + kc, c0:c0 + cw]
            if use_rows:
                y = y + up * w_ref[kc:kc + 1, c0:c0 + cw] + dn * w_ref[6 + kc:7 + kc, c0:c0 + cw]
            return y

        acc = (colsum(1) + pltpu.roll(colsum(0), 1, axis=0) * m_left
               + pltpu.roll(colsum(2), tm - 1, axis=0) * m_right)
        s = acc * _sigmoid(acc)
        for h0 in range(0, cw, HEAD_DIM):
            c = c0 + h0
            seg = s[:, h0:h0 + HEAD_DIM]
            if c < 2 * QK_DIM:
                seg = seg * lax.rsqrt(jnp.sum(seg * seg, axis=-1, keepdims=True) + EPS)
            if c < QK_DIM:
                q_ref[0, :, c:c + HEAD_DIM] = (seg * HEAD_DIM ** -0.5).astype(q_ref.dtype)
            elif c < 2 * QK_DIM:
                k_ref[0, :, c - QK_DIM:c - QK_DIM + HEAD_DIM] = seg.astype(k_ref.dtype)
            else:
                v_ref[0, :, c - 2 * QK_DIM:c - 2 * QK_DIM + HEAD_DIM] = seg.astype(v_ref.dtype)
        ci = c0 // cw
        for job in plain[ci * len(plain) // n_conv_chunks:(ci + 1) * len(plain) // n_conv_chunks]:
            job()
    g = _mm(u, wgate_ref[...])
    a = g + par_ref[1:2, :]
    softplus = jnp.maximum(a, 0.0) + jnp.log1p(jnp.exp(-jnp.abs(a)))
    log_g = -jnp.exp(par_ref[0:1, :]) * softplus
    lane = lax.broadcasted_iota(I32, g.shape, 1)
    go_ref[0] = jnp.where(lane < GATE_LANE0, _sigmoid(g), log_g)
    gr_ref[0] = jnp.transpose(log_g)[GATE_LANE0:GATE_LANE0 + 2 * NV_HEADS]


def _inconv_call(x3, sh, sc, g, w_qkv, w_gates, w_extra, extra_dtypes, conv_w, par, grid_w, use_rows, tm):
    b, t, d = x3.shape
    kern = functools.partial(_inconv_kernel, grid_w=grid_w, use_rows=use_rows, tm=tm, cw=512, n_extra=len(w_extra))
    per = tm // grid_w
    nrow = t // grid_w
    tile = lambda wd: pl.BlockSpec((1, tm, wd), lambda i, r: (i, r, 0))
    vec = pl.BlockSpec((1, 1, d), lambda i, r: (i, 0, 0))
    const = lambda a: pl.BlockSpec(a.shape, lambda i, r: (0,) * a.ndim, pipeline_mode=pl.Buffered(1))
    in_specs, args = [], []
    if use_rows:
        in_specs.append(pl.BlockSpec((1, grid_w, d), lambda i, r: (i, jnp.maximum(r * per - 1, 0), 0)))
        args.append(x3)
    in_specs.append(tile(d))
    args.append(x3)
    if use_rows:
        in_specs.append(pl.BlockSpec((1, grid_w, d), lambda i, r: (i, jnp.minimum((r + 1) * per, nrow - 1), 0)))
        args.append(x3)
    consts = [g, w_qkv, w_gates, *w_extra, conv_w, par]
    in_specs += [vec, vec] + [const(a) for a in consts]
    args += [sh, sc] + consts
    widths = [QK_DIM, QK_DIM, V_DIM, LANES] + [w.shape[1] for w in w_extra]
    dtypes = [BF16, BF16, BF16, F32] + list(extra_dtypes)
    out_shape = [jax.ShapeDtypeStruct((b, t, wd), dt) for wd, dt in zip(widths, dtypes)]
    out_specs = [tile(wd) for wd in widths]
    out_shape.insert(4, jax.ShapeDtypeStruct((b, 2 * NV_HEADS, t), F32))
    out_specs.insert(4, pl.BlockSpec((1, 2 * NV_HEADS, tm), lambda i, r: (i, 0, r)))
    return pl.pallas_call(
        kern, grid=(b, t // tm), in_specs=in_specs, out_specs=out_specs, out_shape=out_shape,
        compiler_params=_cparams(("parallel", "parallel"), 56),
        name="inconv_rows" if use_rows else "inconv_seq",
    )(*args)


def _gdn_kernel(qf, kf, vf, gf, rf, qb, kb, vb, gb, rb, s0_ref, of, ob, sfin_ref, s_ref):
    i = pl.program_id(1)
    nc = pl.num_programs(1)

    @pl.when(i == 0)
    def _():
        s_ref[...] = s0_ref[0]

    c = qf.shape[1]
    per = NV_HEADS // NQK_HEADS
    row = lax.broadcasted_iota(I32, (c, c), 0)
    colj = lax.broadcasted_iota(I32, (c, c), 1)
    eye = jnp.where(row == colj, 1.0, 0.0)
    nt_dims = (((1,), (1,)), ((), ()))
    tn_dims = (((0,), (0,)), ((), ()))

    seqs = []
    for d, (q_r, k_r, v_r, g_r, r_r, o_r) in enumerate(((qf, kf, vf, gf, rf, of), (qb, kb, vb, gb, rb, ob))):
        rev = d == 1
        incl = (colj >= row) if rev else (colj <= row)
        strict = (colj > row) if rev else (colj < row)
        gates = g_r[0]
        tri_c = jnp.where(incl, 1.0, 0.0)
        gcm = _mm(tri_c, gates, HIGHEST)
        gcr = _mm(r_r[0], tri_c, HIGHEST, dims=nt_dims)
        for hq in range(NQK_HEADS):
            q = q_r[0, :, hq * HEAD_DIM:(hq + 1) * HEAD_DIM]
            k = k_r[0, :, hq * HEAD_DIM:(hq + 1) * HEAD_DIM]
            kq = lax.dot_general(jnp.concatenate([q, k], axis=0), k, nt_dims, preferred_element_type=F32)
            for j in range(per):
                h = hq * per + j
                idx = d * NV_HEADS + h
                gc_c = gcm[:, GATE_LANE0 + idx:GATE_LANE0 + idx + 1]
                seqs.append(dict(d=d, h=h, o_r=o_r, v_r=v_r, q=q, k=k, qk=kq[:c], kk=kq[c:], incl=incl, strict=strict,
                                 beta=gates[:, idx:idx + 1], gc_c=gc_c, gc_r=gcr[idx:idx + 1, :],
                                 ge=gc_c[0:1] if rev else gc_c[c - 1:c]))

    def same_block(m):
        sh = int(math.log2(m))
        return jnp.right_shift(row, sh) == jnp.right_shift(colj, sh)

    for s in seqs:
        s['decay'] = jnp.where(s['incl'], jnp.exp(jnp.where(s['incl'], s['gc_c'] - s['gc_r'], 0.0)), 0.0)
        s['a'] = jnp.where(s['strict'], s['beta'] * s['kk'] * s['decay'], 0.0)
        s['t'] = eye - jnp.where(same_block(2), s['a'], 0.0)
    m = 4
    while m <= c:
        between = jnp.logical_and(same_block(m), jnp.logical_not(same_block(m // 2)))
        for s in seqs:
            s['te'] = _mm(s['t'], jnp.where(between, s['a'], 0.0))
        for s in seqs:
            s['t'] = s['t'] - _mm(s['te'], s['t'])
        m *= 2
    for s in seqs:
        h = s['h']
        egc = jnp.exp(s['gc_c'])
        kf32 = s['k'].astype(F32)
        v = s['v_r'][0, :, h * HEAD_DIM:(h + 1) * HEAD_DIM].astype(F32)
        rhs = jnp.concatenate([s['beta'] * v, (s['beta'] * egc) * kf32], axis=1)
        s['sol'] = _mm(s['t'], rhs)
        s['q_dec'] = s['q'].astype(F32) * egc
        s['k_dec'] = kf32 * jnp.exp(s['ge'] - s['gc_c'])
    for s in seqs:
        s['ws'] = _mm(jnp.concatenate([s['sol'][:, HEAD_DIM:], s['q_dec']], axis=0), s_ref[s['d'], s['h']])
    for s in seqs:
        s['u'] = s['sol'][:, :HEAD_DIM] - s['ws'][:c]
        s_ref[s['d'], s['h']] = (jnp.exp(s['ge']) * s_ref[s['d'], s['h']]
                                 + _mm(s['k_dec'], s['u'], dims=tn_dims))
    for s in seqs:
        h = s['h']
        o = s['ws'][c:] + _mm(s['qk'] * s['decay'], s['u'])
        s['o_r'][0, :, h * HEAD_DIM:(h + 1) * HEAD_DIM] = o.astype(s['o_r'].dtype)

    @pl.when(i == nc - 1)
    def _():
        sfin_ref[0] = s_ref[...]


def _gdn_call(q, k, v, go, rows, s0):
    b, t, _ = q.shape
    c = GDN_CHUNK
    nc = t // c
    fwd = lambda i, n: (i, n, 0)
    bwd = lambda i, n: (i, nc - 1 - n, 0)
    rfwd = lambda i, n: (i, 0, n)
    rbwd = lambda i, n: (i, 0, nc - 1 - n)
    state_spec = pl.BlockSpec((1,) + s0.shape[1:], lambda i, n: (i, 0, 0, 0, 0))

    def specs(m3, mr):
        return [pl.BlockSpec((1, c, QK_DIM), m3), pl.BlockSpec((1, c, QK_DIM), m3),
                pl.BlockSpec((1, c, V_DIM), m3), pl.BlockSpec((1, c, LANES), m3),
                pl.BlockSpec((1, 2 * NV_HEADS, c), mr)]

    return pl.pallas_call(
        _gdn_kernel,
        grid=(b, nc),
        in_specs=specs(fwd, rfwd) + specs(bwd, rbwd) + [state_spec],
        out_specs=[pl.BlockSpec((1, c, V_DIM), fwd), pl.BlockSpec((1, c, V_DIM), bwd), state_spec],
        out_shape=[jax.ShapeDtypeStruct((b, t, V_DIM), BF16), jax.ShapeDtypeStruct((b, t, V_DIM), BF16),
                   jax.ShapeDtypeStruct(s0.shape, F32)],
        scratch_shapes=[pltpu.VMEM(s0.shape[1:], F32)],
        compiler_params=_cparams(("parallel", "arbitrary"), 48),
        name="gdn",
    )(q, k, v, go, rows, q, k, v, go, rows, s0)


def _fnet1_kernel(x_ref, f_ref, ar_ref, ai_ref):
    n = x_ref.shape[1]
    for j in range(x_ref.shape[2]):
        a = _mm(f_ref[...], x_ref[0, :, j, :])
        ar_ref[0, j] = a[:n]
        ai_ref[0, j] = a[n:]


def _fnet2_kernel(ar_ref, ai_ref, g_ref, wc_ref, o_ref, *, scale):
    n = ar_ref.shape[1]
    cols = ar_ref.shape[2]
    zs = []
    for m in range(cols):
        a2 = jnp.concatenate([ar_ref[0, :, m, :], ai_ref[0, :, m, :]], axis=0)
        zs.append(_mm(g_ref[m], a2))
    zr = jnp.concatenate([z[:n] for z in zs], axis=0)
    zi = jnp.concatenate([z[n:] for z in zs], axis=0)
    for g0 in range(0, zr.shape[1], HEAD_DIM):
        y = _mm(jnp.concatenate([zr[:, g0:g0 + HEAD_DIM], zi[:, g0:g0 + HEAD_DIM]], axis=1), wc_ref[...])
        for m in range(cols):
            o_ref[0, :, m, g0:g0 + HEAD_DIM] = y[m * n:(m + 1) * n] * scale


def _fnet_tables(n):
    a = np.arange(n)
    ang1 = 2.0 * np.pi * np.outer(a, a) / n
    f1 = np.concatenate([np.cos(ang1), -np.sin(ang1)], axis=0)
    m = a[:, None] + n * a[None, :]
    ang2 = 2.0 * np.pi * ((m[:, :, None] * a[None, None, :]) % (n * n)) / (n * n)
    gc, gs = np.cos(ang2), np.sin(ang2)
    g2 = np.concatenate([np.concatenate([gc, gs], axis=2), np.concatenate([-gs, gc], axis=2)], axis=1)
    angc = 2.0 * np.pi * np.outer(np.arange(HEAD_DIM), np.arange(HEAD_DIM)) / HEAD_DIM
    wc = np.concatenate([np.cos(angc), np.sin(angc)], axis=0)
    f = lambda x: jnp.asarray(x, F32).astype(BF16)
    return f(f1), f(g2), f(wc)


def _fnet_call(f):
    b, l, c = f.shape
    n = GRID_W
    assert l == n * n
    f1, g2, wc = _fnet_tables(n)
    cols = 4 * SUBLANES
    ar, ai = pl.pallas_call(
        _fnet1_kernel,
        grid=(b, n // cols),
        in_specs=[pl.BlockSpec((1, n, cols, c), lambda i, j: (i, 0, j, 0)),
                  pl.BlockSpec((2 * n, n), lambda i, j: (0, 0))],
        out_specs=[pl.BlockSpec((1, cols, n, c), lambda i, j: (i, j, 0, 0))] * 2,
        out_shape=[jax.ShapeDtypeStruct((b, n, n, c), F32)] * 2,
        compiler_params=_cparams(("parallel", "parallel"), 32),
        name="fnet1",
    )(f.reshape(b, n, n, c), f1)
    out = pl.pallas_call(
        functools.partial(_fnet2_kernel, scale=1.0 / math.sqrt(l * HEAD_DIM)),
        grid=(b, n // cols),
        in_specs=[pl.BlockSpec((1, n, cols, c), lambda i, j: (i, 0, j, 0)),
                  pl.BlockSpec((1, n, cols, c), lambda i, j: (i, 0, j, 0)),
                  pl.BlockSpec((cols, 2 * n, 2 * n), lambda i, j: (j, 0, 0)),
                  pl.BlockSpec((2 * HEAD_DIM, HEAD_DIM), lambda i, j: (0, 0))],
        out_specs=pl.BlockSpec((1, n, cols, c), lambda i, j: (i, 0, j, 0)),
        out_shape=jax.ShapeDtypeStruct((b, n, n, c), F32),
        compiler_params=_cparams(("parallel", "parallel"), 32),
        name="fnet2",
    )(ar, ai, g2, wc)
    return out.reshape(b, l, c)


def _merge_kernel(of_ref, ob_ref, z_ref, fm_ref, gab_ref, x_ref, g1_ref, sh2_ref, sc2_ref, gn_ref, n2_ref,
                  wg_ref, wf_ref, wm_ref, wr_ref, br_ref, x1_ref, h2_ref, lg_ref):
    d = x_ref.shape[1]
    tm = x_ref.shape[0]
    halves = [slice(0, tm // 2), slice(tm // 2, tm)]
    yb_in = []
    for r in halves:
        o = of_ref[r, :].astype(F32) + ob_ref[r, :].astype(F32)
        z = z_ref[r, :].astype(F32)
        parts = []
        for h0 in range(0, V_DIM, HEAD_DIM):
            oh = o[:, h0:h0 + HEAD_DIM]
            parts.append(oh * lax.rsqrt(jnp.mean(oh * oh, axis=-1, keepdims=True) + EPS) * gn_ref[...])
        yb_in.append(jnp.concatenate(parts, axis=1) * (z * _sigmoid(z)))
    yb = [_mm(v, wg_ref[...]) for v in yb_in]
    ya = [_mm(fm_ref[r, :], wf_ref[...]) for r in halves]
    mixed = [_sigmoid(gab_ref[r, :d].astype(F32)) * a + _sigmoid(gab_ref[r, d:].astype(F32)) * b_
             for r, a, b_ in zip(halves, ya, yb)]
    mm = [_mm(v, wm_ref[...]) for v in mixed]
    w = wr_ref[...]
    w_hi = w.astype(BF16)
    w2 = jnp.concatenate([w_hi, (w - w_hi.astype(F32)).astype(BF16)], axis=1)
    for r, v in zip(halves, mm):
        x1 = x_ref[r, :] + g1_ref[0] * v
        x1_ref[r, :] = x1
        h2 = _rmsnorm(x1, n2_ref[...]) * (1.0 + sc2_ref[0]) + sh2_ref[0]
        h2_ref[r, :] = _pack_bf16_pairs(h2)
        h_hi = h2.astype(BF16)
        h_lo = (h2 - h_hi.astype(F32)).astype(BF16)
        part = _mm(h_hi, w2)
        lg_ref[r, :] = (part[:, :LANES] + part[:, LANES:]) + _mm(h_lo, w_hi) + br_ref[...]


def _merge_call(of, ob, z, fm, gab, x2, g1, sh2, sc2, gn, n2, wg, wf, wm, wr, br, tokens_per_batch, tm):
    n, d = x2.shape
    per = tokens_per_batch // tm
    tok = lambda wd: pl.BlockSpec((tm, wd), lambda i: (i, 0))
    vec = pl.BlockSpec((1, 1, d), lambda i: (i // per, 0, 0))
    full = lambda a: pl.BlockSpec(a.shape, lambda i: (0,) * a.ndim)
    return pl.pallas_call(
        _merge_kernel,
        grid=(n // tm,),
        in_specs=[tok(V_DIM), tok(V_DIM), tok(V_DIM), tok(F_DIM), tok(2 * d), tok(d), vec, vec, vec,
                  full(gn), full(n2), full(wg), full(wf), full(wm), full(wr), full(br)],
        out_specs=[tok(d), tok(d // 2), tok(LANES)],
        out_shape=[jax.ShapeDtypeStruct((n, d), F32), jax.ShapeDtypeStruct((n, d // 2), I32),
                   jax.ShapeDtypeStruct((n, LANES), F32)],
        compiler_params=_cparams(("parallel",), 56),
        name="merge",
    )(of, ob, z, fm, gab, x2, g1, sh2, sc2, gn, n2, wg, wf, wm, wr, br)


def _route_kernel(lg_ref, idx_ref, w_ref, rank_ref, cnt_ref, run_ref):
    i = pl.program_id(0)

    @pl.when(i == 0)
    def _():
        run_ref[...] = jnp.zeros_like(run_ref)

    l = lg_ref[...]
    tm = l.shape[0]
    lane = lax.broadcasted_iota(I32, l.shape, 1).astype(F32)
    vals, idxs = [], []
    for _ in range(TOP_K):
        m = jnp.max(l, axis=-1, keepdims=True)
        idx = jnp.min(jnp.where(l == m, lane, float(LANES)), axis=-1, keepdims=True)
        vals.append(m)
        idxs.append(idx)
        l = jnp.where(lane == idx, NEG_BIG * 2.0, l)
    es = [jnp.exp(v - vals[0]) for v in vals]
    inv = 1.0 / (es[0] + es[1] + es[2] + es[3])
    picked = jnp.zeros(l.shape, F32)
    for idx in idxs:
        picked = picked + (lane == idx).astype(F32)
    r = lax.broadcasted_iota(I32, (tm, tm), 0)
    cidx = lax.broadcasted_iota(I32, (tm, tm), 1)
    before = _mm(jnp.where(cidx < r, 1.0, 0.0), picked) + run_ref[...]
    idx_out = jnp.zeros(l.shape, F32)
    w_out = jnp.zeros(l.shape, F32)
    rank_out = jnp.zeros(l.shape, F32)
    for k in range(TOP_K):
        rk = jnp.sum(jnp.where(lane == idxs[k], before, 0.0), axis=-1, keepdims=True)
        idx_out = jnp.where(lane == k, idxs[k], idx_out)
        w_out = jnp.where(lane == k, es[k] * inv, w_out)
        rank_out = jnp.where(lane == k, rk, rank_out)
    idx_ref[...] = jnp.transpose(idx_out)[:SUBLANES].astype(I32)
    w_ref[...] = w_out
    rank_ref[...] = jnp.transpose(rank_out)[:SUBLANES].astype(I32)
    run_ref[...] = run_ref[...] + jnp.sum(picked, axis=0, keepdims=True)
    cnt_ref[...] = run_ref[...]


def _route_call(logits, tm):
    n = logits.shape[0]
    tok = pl.BlockSpec((tm, LANES), lambda i: (i, 0))
    tok_t = pl.BlockSpec((SUBLANES, tm), lambda i: (0, i))
    return pl.pallas_call(
        _route_kernel,
        grid=(n // tm,),
        in_specs=[tok],
        out_specs=[tok_t, tok, tok_t, pl.BlockSpec((1, LANES), lambda i: (0, 0))],
        out_shape=[jax.ShapeDtypeStruct((SUBLANES, n), I32), jax.ShapeDtypeStruct((n, LANES), F32),
                   jax.ShapeDtypeStruct((SUBLANES, n), I32), jax.ShapeDtypeStruct((1, LANES), F32)],
        scratch_shapes=[pltpu.VMEM((1, LANES), F32)],
        compiler_params=_cparams(("arbitrary",), 32),
        name="route",
    )(logits)


def _sc_mesh():
    return plsc.VectorSubcoreMesh(core_axis_name="c", subcore_axis_name="s",
                                  num_cores=SC_CORES, num_subcores=SC_SUBCORES)


def _sc_worker_id():
    return lax.axis_index("s") * SC_CORES + lax.axis_index("c")


def _sc_scatter_rows(x, idx, n_out):
    n, d = x.shape
    per_w = idx.shape[0] // SC_WORKERS
    nwin = per_w // SC_WINDOW
    assert per_w * SC_WORKERS == idx.shape[0] and nwin * SC_WINDOW == per_w and nwin % 2 == 0 and n % per_w == 0

    def body(x_hbm, idx_hbm, out_hbm, idx_v, rows_v, sem_r, sem_w):
        wid = _sc_worker_id()
        row0 = lax.rem(wid * per_w, n)
        pltpu.sync_copy(idx_hbm.at[wid], idx_v)

        def read(j, b):
            return pltpu.make_async_copy(x_hbm.at[pl.ds(row0 + j * SC_WINDOW, SC_WINDOW)], rows_v.at[b], sem_r.at[b])

        def write(j, b):
            return pltpu.make_async_copy(rows_v.at[b], out_hbm.at[idx_v.at[j]], sem_w.at[b])

        @pl.loop(0, nwin, step=2)
        def _(j):
            read(j, 0).start()
            read(j + 1, 1).start()
            read(j, 0).wait()
            write(j, 0).start()
            read(j + 1, 1).wait()
            write(j + 1, 1).start()
            write(j, 0).wait()
            write(j + 1, 1).wait()

    return pl.kernel(
        body, out_type=jax.ShapeDtypeStruct((n_out, d), x.dtype), mesh=_sc_mesh(),
        scratch_types=[pltpu.VMEM((nwin, SC_WINDOW), I32), pltpu.VMEM((2, SC_WINDOW, d), x.dtype),
                       pltpu.SemaphoreType.DMA((2,)), pltpu.SemaphoreType.DMA((2,))],
        name="sc_scatter_rows",
    )(x, idx.reshape(SC_WORKERS, nwin, SC_WINDOW))


def _sc_gather_rows(y, idx):
    d = y.shape[1]
    total = idx.shape[0]
    per_w = total // SC_WORKERS
    nwin = per_w // SC_WINDOW
    assert per_w * SC_WORKERS == total and nwin * SC_WINDOW == per_w and nwin % 2 == 0

    def body(y_hbm, idx_hbm, out_hbm, idx_v, rows_v, sem_r, sem_w):
        wid = _sc_worker_id()
        row0 = wid * per_w
        pltpu.sync_copy(idx_hbm.at[wid], idx_v)

        def read(j, b):
            return pltpu.make_async_copy(y_hbm.at[idx_v.at[j]], rows_v.at[b], sem_r.at[b])

        def write(j, b):
            return pltpu.make_async_copy(rows_v.at[b], out_hbm.at[pl.ds(row0 + j * SC_WINDOW, SC_WINDOW)], sem_w.at[b])

        @pl.loop(0, nwin, step=2)
        def _(j):
            read(j, 0).start()
            read(j + 1, 1).start()
            read(j, 0).wait()
            write(j, 0).start()
            read(j + 1, 1).wait()
            write(j + 1, 1).start()
            write(j, 0).wait()
            write(j + 1, 1).wait()

    return pl.kernel(
        body, out_type=jax.ShapeDtypeStruct((total, d), y.dtype), mesh=_sc_mesh(),
        scratch_types=[pltpu.VMEM((nwin, SC_WINDOW), I32), pltpu.VMEM((2, SC_WINDOW, d), y.dtype),
                       pltpu.SemaphoreType.DMA((2,)), pltpu.SemaphoreType.DMA((2,))],
        name="sc_gather_rows",
    )(y, idx.reshape(SC_WORKERS, nwin, SC_WINDOW))


def _expert_kernel(be_ref, nv_ref, nxt_ref, par_ref, used_ref, x_ref, wg_hbm, wu_hbm, wd_hbm, bg_ref, bu_ref, bd_ref,
                   y_ref, wgf, wuf, wdf, wgb, wub, wdb, sem):
    i = pl.program_id(0)

    def fetch(e, s):
        return [pltpu.make_async_copy(w_hbm.at[e], w_f.at[s], sem.at[s])
                for w_hbm, w_f in ((wg_hbm, wgf), (wu_hbm, wuf), (wd_hbm, wdf))]

    @pl.when(i < nv_ref[0])
    def _():
        e = be_ref[i]
        s = par_ref[e]

        @pl.when(i == 0)
        def _():
            for cp in fetch(e, s):
                cp.start()

        first = jnp.logical_or(i == 0, be_ref[jnp.maximum(i - 1, 0)] != e)
        quarter = x_ref.shape[0] // 4
        full = used_ref[i] > 3 * quarter

        @pl.when(first)
        def _():
            for cp in fetch(e, s):
                cp.wait()

            @pl.when(nxt_ref[e] >= 0)
            def _():
                for cp in fetch(nxt_ref[e], 1 - s):
                    cp.start()

        def ffn(nrows, cast):
            x = _unpack_bf16_pairs(x_ref[:nrows, :]).astype(BF16)
            if cast:
                wgb[...] = wgf[s].astype(BF16)
            gate = jnp.minimum(_mm(x, wgb[...]) + bg_ref[0], SWIGLU_LIMIT)
            if cast:
                wub[...] = wuf[s].astype(BF16)
            up = jnp.clip(_mm(x, wub[...]) + bu_ref[0], -SWIGLU_LIMIT, SWIGLU_LIMIT)
            if cast:
                wdb[...] = wdf[s].astype(BF16)
            act = (up + 1.0) * gate * _sigmoid(SWIGLU_ALPHA * gate)
            y_ref[:nrows, :] = _pack_bf16_pairs(_mm(act, wdb[...]) + bd_ref[0])

        @pl.when(jnp.logical_and(first, full))
        def _():
            ffn(4 * quarter, True)

        @pl.when(jnp.logical_and(first, jnp.logical_not(full)))
        def _():
            wgb[...] = wgf[s].astype(BF16)
            wub[...] = wuf[s].astype(BF16)
            wdb[...] = wdf[s].astype(BF16)

        for nq in range(1, 5):
            in_range = jnp.logical_and(used_ref[i] > (nq - 1) * quarter, used_ref[i] <= nq * quarter)
            if nq == 4:
                in_range = jnp.logical_and(in_range, jnp.logical_not(first))

            @pl.when(in_range)
            def _(nq=nq):
                ffn(nq * quarter, False)


def _expert_call(block_e, n_valid, next_e, parity, used, xs, w_gate, w_up, w_down, b_gate, b_up, b_down):
    nb = block_e.shape[0]
    tmb = EXPERT_BLOCK
    ne, d, de = w_gate.shape
    bspec = lambda s: pl.BlockSpec((1,) + s, lambda i, be, nv, nx, pa, us: (be[i], 0, 0))
    anyspec = pl.BlockSpec(memory_space=pl.ANY)
    grid_spec = pltpu.PrefetchScalarGridSpec(
        num_scalar_prefetch=5,
        grid=(nb,),
        in_specs=[pl.BlockSpec((tmb, d // 2), lambda i, be, nv, nx, pa, us: (i, 0)),
                  anyspec, anyspec, anyspec, bspec((1, de)), bspec((1, de)), bspec((1, d))],
        out_specs=pl.BlockSpec((tmb, d // 2), lambda i, be, nv, nx, pa, us: (i, 0)),
        scratch_shapes=[pltpu.VMEM((2, d, de), F32), pltpu.VMEM((2, d, de), F32), pltpu.VMEM((2, de, d), F32),
                        pltpu.VMEM((d, de), BF16), pltpu.VMEM((d, de), BF16), pltpu.VMEM((de, d), BF16),
                        pltpu.SemaphoreType.DMA((2,))],
    )
    return pl.pallas_call(
        _expert_kernel,
        grid_spec=grid_spec,
        out_shape=jax.ShapeDtypeStruct(xs.shape, xs.dtype),
        compiler_params=_cparams(("arbitrary",), 56),
        name="expert",
    )(block_e, n_valid, next_e, parity, used, xs, w_gate, w_up, w_down,
      b_gate.reshape(ne, 1, de), b_up.reshape(ne, 1, de), b_down.reshape(ne, 1, d))


def _combine_kernel(y0, y1, y2, y3, w_ref, x1_ref, g2_ref, fg_ref, o_ref):
    w = w_ref[...]
    ys = [_unpack_bf16_pairs(y[...]) for y in (y0, y1, y2, y3)]
    moe = (w[:, 0:1] * ys[0] + w[:, 1:2] * ys[1]) + (w[:, 2:3] * ys[2] + w[:, 3:4] * ys[3])
    o_ref[...] = _rmsnorm(x1_ref[...] + g2_ref[0] * moe, fg_ref[...])


def _combine_call(y4, top_w, x1, g2, fg, tokens_per_batch, tm):
    n, d = x1.shape
    per = tokens_per_batch // tm
    nt = n // tm
    yspec = lambda k: pl.BlockSpec((tm, d // 2), lambda i: (k * nt + i, 0))
    return pl.pallas_call(
        _combine_kernel,
        grid=(nt,),
        in_specs=[yspec(0), yspec(1), yspec(2), yspec(3),
                  pl.BlockSpec((tm, LANES), lambda i: (i, 0)),
                  pl.BlockSpec((tm, d), lambda i: (i, 0)),
                  pl.BlockSpec((1, 1, d), lambda i: (i // per, 0, 0)),
                  pl.BlockSpec((1, d), lambda i: (0, 0))],
        out_specs=pl.BlockSpec((tm, d), lambda i: (i, 0)),
        out_shape=jax.ShapeDtypeStruct((n, d), F32),
        compiler_params=_cparams(("parallel",), 48),
        name="combine",
    )(y4, y4, y4, y4, top_w, x1, g2, fg)


def _slot_kernel(start_ref, idx_ref, rank_ref, o_ref):
    idx = idx_ref[...]
    acc = rank_ref[...]
    for e in range(N_EXPERTS):
        acc = acc + jnp.where(idx == e, start_ref[e], 0)
    o_ref[...] = acc


def _slot_call(pad_start, top_idx, rank):
    full = pl.BlockSpec(top_idx.shape, lambda i, s: (0, 0))
    return pl.pallas_call(
        _slot_kernel,
        grid_spec=pltpu.PrefetchScalarGridSpec(num_scalar_prefetch=1, grid=(1,), in_specs=[full, full],
                                               out_specs=full),
        out_shape=jax.ShapeDtypeStruct(top_idx.shape, I32),
        compiler_params=_cparams(("arbitrary",), 32),
        name="slots",
    )(pad_start, top_idx, rank)


def _routing_tables(top_idx, rank, counts, n):
    tmb = EXPERT_BLOCK
    counts = counts.astype(I32)
    padded = (counts + tmb - 1) // tmb * tmb
    pad_end = jnp.cumsum(padded)
    pad_start = pad_end - padded
    n_blocks = -(-(n * TOP_K + N_EXPERTS * (tmb - 1)) // tmb)
    n_slots = n_blocks * tmb
    dest_flat = _slot_call(pad_start, top_idx, rank)[:TOP_K].reshape(-1)
    block_start = jnp.arange(n_blocks, dtype=I32) * tmb
    block_e = jnp.minimum(jnp.sum((pad_end[None, :] <= block_start[:, None]).astype(I32), axis=1), N_EXPERTS - 1)
    n_valid = (pad_end[-1:] // tmb).astype(I32)
    experts = jnp.arange(N_EXPERTS, dtype=I32)
    used = counts > 0
    later = jnp.where(jnp.logical_and(used[None, :], experts[None, :] > experts[:, None]), experts[None, :], N_EXPERTS)
    next_e = jnp.min(later, axis=1)
    next_e = jnp.where(next_e == N_EXPERTS, -1, next_e).astype(I32)
    parity = ((jnp.cumsum(used.astype(I32)) - used.astype(I32)) % 2).astype(I32)
    row_end = jnp.sum(jnp.where(block_e[:, None] == experts[None, :], (pad_start + counts)[None, :], 0), axis=1)
    rows_used = jnp.clip(row_end - block_start, 0, tmb).astype(I32)
    return block_e, n_valid, next_e, parity, rows_used, dest_flat, n_slots


def _gdn_branch(x3, sh, sc, norm_g, w_qkv, w_gates, w_extra, extra_dtypes, conv_w, par, grid_w, use_rows, tm, s0):
    q, k, v, go, rows, *extra = _inconv_call(x3, sh, sc, norm_g, w_qkv, w_gates, w_extra, extra_dtypes, conv_w, par,
                                             grid_w, use_rows, tm)
    o_f, o_b, s_fin = _gdn_call(q, k, v, go, rows, s0)
    return extra, o_f, o_b, s_fin


def kernel(x, c, ctx, c_ctx, w_mod, b_mod, norm1_g, norm2_g, w_in, conv_w, a_log, dt_bias, gdn_norm_g,
           w_fourier_out, w_gdn_out, w_merge_out, w_router, b_router, w_gate, b_gate, w_up, b_up,
           w_down, b_down, final_norm_g):
    b, l, d = x.shape
    n = b * l
    n_ctx = ctx.shape[1]
    assert w_mod.shape[0] == 1 and l == GRID_W * GRID_W and d == V_DIM

    c8 = jnp.concatenate([c, c_ctx[None, :], jnp.zeros((8 - b - 1, d), F32)], axis=0)
    mod = _mod_call(c8, w_mod[0], b_mod[0])
    sh1, sc1, g1, sh2, sc2, g2 = [mod[:b, j * d:(j + 1) * d].reshape(b, 1, d) for j in range(6)]
    csh1 = jnp.broadcast_to(mod[b:b + 1, 0:d].reshape(1, 1, d), (b, 1, d))
    csc1 = jnp.broadcast_to(mod[b:b + 1, d:2 * d].reshape(1, 1, d), (b, 1, d))

    off_gate = QKV_DIM
    off_z = off_gate + 4 * NV_HEADS
    off_f = off_z + V_DIM
    off_ga = off_f + F_DIM
    w_qkv, w_gates, *w_extra = _wsplit_call(w_in, (0, off_gate, off_z, off_f, off_ga, w_in.shape[2]))
    par = jnp.pad(jnp.stack([a_log[0].reshape(-1), dt_bias[0].reshape(-1)]),
                  ((0, 6), (GATE_LANE0, LANES - 2 * GATE_LANE0)))
    n1 = norm1_g[0].reshape(1, d)
    cw = conv_w[0].reshape(9, QKV_DIM)

    zero_state = jnp.zeros((b, 2, NV_HEADS, HEAD_DIM, HEAD_DIM), F32)
    _, _, _, s_ctx = _gdn_branch(ctx, csh1, csc1, n1, w_qkv, w_gates, [], [], cw, par, n_ctx, False, n_ctx,
                                 zero_state)

    x2 = x.reshape(n, d)
    (z, f, gab), o_f, o_b, _ = _gdn_branch(x, sh1, sc1, n1, w_qkv, w_gates, w_extra, (BF16, F32, BF16),
                                           cw, par, GRID_W, True, TOKEN_TILE, s_ctx)
    z, gab = z.reshape(n, V_DIM), gab.reshape(n, 2 * d)
    fmix = _fnet_call(f).reshape(n, F_DIM)

    wr = jnp.pad(w_router[0], ((0, 0), (0, LANES - N_EXPERTS)))
    br = jnp.pad(b_router[0], (0, LANES - N_EXPERTS), constant_values=NEG_BIG).reshape(1, LANES)
    x1, h2, logits = _merge_call(
        o_f.reshape(n, V_DIM), o_b.reshape(n, V_DIM), z, fmix, gab, x2, g1, sh2, sc2,
        gdn_norm_g[0].reshape(1, HEAD_DIM), norm2_g[0].reshape(1, d),
        w_gdn_out[0].astype(BF16), w_fourier_out[0].astype(BF16), w_merge_out[0].astype(BF16), wr, br, l, TOKEN_TILE)

    top_idx, top_w, rank, counts = _route_call(logits, TOKEN_TILE)
    block_e, n_valid, next_e, parity, rows_used, dest_flat, n_slots = _routing_tables(
        top_idx, rank, counts[0, :N_EXPERTS], n)
    xs = _sc_scatter_rows(h2, dest_flat, n_slots)
    ys = _expert_call(block_e, n_valid, next_e, parity, rows_used, xs, w_gate[0], w_up[0], w_down[0],
                      b_gate[0], b_up[0], b_down[0])
    y4 = _sc_gather_rows(ys, dest_flat)
    out = _combine_call(y4, top_w, x1, g2, final_norm_g.reshape(1, d), l, 2 * TOKEN_TILE)
    return out.reshape(b, l, d)
```

```python
import functools
import math

import jax
import jax.numpy as jnp
import numpy as np
from jax import lax
from jax.experimental import pallas as pl
from jax.experimental.pallas import tpu as pltpu
from jax.experimental.pallas import tpu_sc as plsc

F32 = jnp.float32
BF16 = jnp.bfloat16
I32 = jnp.int32
HIGHEST = lax.Precision.HIGHEST

GRID_W = 64
NQK_HEADS = 4
NV_HEADS = 8
HEAD_DIM = 128
QK_DIM = NQK_HEADS * HEAD_DIM
V_DIM = NV_HEADS * HEAD_DIM
QKV_DIM = 2 * QK_DIM + V_DIM
F_GROUPS = 4
F_DIM = F_GROUPS * HEAD_DIM
N_EXPERTS = 32
TOP_K = 4
SWIGLU_ALPHA = 1.702
SWIGLU_LIMIT = 7.0
EPS = 1e-6

LANES = 128
SUBLANES = 8
GATE_LANE0 = 16
GDN_CHUNK = 128
EXPERT_BLOCK = 1024
TOKEN_TILE = 512
SC_CORES = 2
SC_SUBCORES = 16
SC_WORKERS = SC_CORES * SC_SUBCORES
SC_WINDOW = 64
NEG_BIG = -1e30
MIB = 2 ** 20


def _cparams(sem, vmem_mib):
    return pltpu.CompilerParams(dimension_semantics=sem, vmem_limit_bytes=vmem_mib * MIB)


def _mm(a, b, prec=None, dims=(((1,), (0,)), ((), ()))):
    if prec is None:
        return lax.dot_general(a.astype(BF16), b.astype(BF16), dims, preferred_element_type=F32)
    return lax.dot_general(a.astype(F32), b.astype(F32), dims, precision=prec, preferred_element_type=F32)


def _sigmoid(x):
    return 0.5 * jnp.tanh(0.5 * x) + 0.5


def _rmsnorm(x, g):
    return x * lax.rsqrt(jnp.mean(x * x, axis=-1, keepdims=True) + EPS) * g


def _pack_bf16_pairs(x):
    half = x.shape[1] // 2
    bits = lax.bitcast_convert_type(x.astype(BF16).astype(F32), jnp.uint32)
    packed = jnp.bitwise_or(jnp.right_shift(bits[:, :half], jnp.uint32(16)),
                            jnp.bitwise_and(bits[:, half:], jnp.uint32(0xFFFF0000)))
    return lax.bitcast_convert_type(packed, I32)


def _unpack_bf16_pairs(p):
    bits = lax.bitcast_convert_type(p, jnp.uint32)
    lo = lax.bitcast_convert_type(jnp.left_shift(bits, jnp.uint32(16)), F32)
    hi = lax.bitcast_convert_type(jnp.bitwise_and(bits, jnp.uint32(0xFFFF0000)), F32)
    return jnp.concatenate([lo, hi], axis=1)


def _mod_kernel(c_ref, w_ref, b_ref, o_ref):
    c = c_ref[...]
    o_ref[...] = _mm(c * _sigmoid(c), w_ref[...], HIGHEST) + b_ref[...]


def _mod_call(c8, w_mod, b_mod):
    d, n = w_mod.shape
    tn = 3072
    return pl.pallas_call(
        _mod_kernel,
        grid=(n // tn,),
        in_specs=[pl.BlockSpec((8, d), lambda j: (0, 0)),
                  pl.BlockSpec((d, tn), lambda j: (0, j)),
                  pl.BlockSpec((1, tn), lambda j: (0, j))],
        out_specs=pl.BlockSpec((8, tn), lambda j: (0, j)),
        out_shape=jax.ShapeDtypeStruct((8, n), F32),
        compiler_params=_cparams(("parallel",), 32),
        name="mod",
    )(c8, w_mod, b_mod.reshape(1, n))


def _wsplit_kernel(w_ref, *o_refs, bounds):
    for o_ref, lo, hi in zip(o_refs, bounds[:-1], bounds[1:]):
        o_ref[:, :hi - lo] = w_ref[0, :, lo:hi].astype(o_ref.dtype)
        if o_ref.shape[1] > hi - lo:
            o_ref[:, hi - lo:] = jnp.zeros((o_ref.shape[0], o_ref.shape[1] - (hi - lo)), o_ref.dtype)


def _wsplit_call(w, bounds):
    d = w.shape[1]
    tr = 256
    widths = [-(-(hi - lo) // LANES) * LANES for lo, hi in zip(bounds[:-1], bounds[1:])]
    return pl.pallas_call(
        functools.partial(_wsplit_kernel, bounds=bounds),
        grid=(d // tr,),
        in_specs=[pl.BlockSpec((1, tr, w.shape[2]), lambda i: (0, i, 0))],
        out_specs=[pl.BlockSpec((tr, wd), lambda i: (i, 0)) for wd in widths],
        out_shape=[jax.ShapeDtypeStruct((d, wd), BF16) for wd in widths],
        compiler_params=_cparams(("parallel",), 32),
        name="wsplit",
    )(w)


def _inconv_kernel(*refs, grid_w, use_rows, tm, cw, n_extra):
    refs = list(refs)
    prev_ref = refs.pop(0) if use_rows else None
    x_ref = refs.pop(0)
    next_ref = refs.pop(0) if use_rows else None
    sh_ref, sc_ref, g_ref, wq_ref, wgate_ref = refs[:5]
    wx_refs = refs[5:5 + n_extra]
    w_ref, par_ref, q_ref, k_ref, v_ref, go_ref, gr_ref = refs[5 + n_extra:12 + n_extra]
    ox_refs = refs[12 + n_extra:]

    def modulated(x):
        return (_rmsnorm(x, g_ref[...]) * (1.0 + sc_ref[0]) + sh_ref[0]).astype(BF16)

    r = pl.program_id(1)
    nr = pl.num_programs(1)
    u = modulated(x_ref[0])

    def project(wx_ref, o_ref, c0, step):
        o_ref[0, :, c0:c0 + step] = _mm(u, wx_ref[:, c0:c0 + step]).astype(o_ref.dtype)

    plain = [functools.partial(project, wx_ref, o_ref, c0, min(wx_ref.shape[1], 512))
             for wx_ref, o_ref in zip(wx_refs, ox_refs) for c0 in range(0, wx_ref.shape[1], min(wx_ref.shape[1], 512))]
    n_conv_chunks = QKV_DIM // cw

    t = lax.broadcasted_iota(I32, (tm, 1), 0)
    col = jnp.bitwise_and(t, grid_w - 1)
    m_left = (col != 0).astype(F32)
    m_right = (col != grid_w - 1).astype(F32)
    if use_rows:
        u_prev = modulated(prev_ref[0])
        u_next = modulated(next_ref[0])
        has_prev = (r > 0).astype(F32)
        has_next = (r < nr - 1).astype(F32)
    for c0 in range(0, QKV_DIM, cw):
        wq = wq_ref[:, c0:c0 + cw]
        xm = _mm(u, wq)
        if use_rows:
            up = jnp.concatenate([_mm(u_prev, wq) * has_prev, xm[:tm - grid_w]], axis=0)
            dn = jnp.concatenate([xm[grid_w:], _mm(u_next, wq) * has_next], axis=0)

        def colsum(kc):
            y = xm * w_ref[3 + kc:4 + kc, c0:c0 + cw]
            if use_rows:
                y = y + up * w_ref[kc:kc + 1, c0:c0 + cw] + dn * w_ref[6 + kc:7 + kc, c0:c0 + cw]
            return y

        acc = (colsum(1) + pltpu.roll(colsum(0), 1, axis=0) * m_left
               + pltpu.roll(colsum(2), tm - 1, axis=0) * m_right)
        s = acc * _sigmoid(acc)
        for h0 in range(0, cw, HEAD_DIM):
            c = c0 + h0
            seg = s[:, h0:h0 + HEAD_DIM]
            if c < 2 * QK_DIM:
                seg = seg * lax.rsqrt(jnp.sum(seg * seg, axis=-1, keepdims=True) + EPS)
            if c < QK_DIM:
                q_ref[0, :, c:c + HEAD_DIM] = (seg * HEAD_DIM ** -0.5).astype(q_ref.dtype)
            elif c < 2 * QK_DIM:
                k_ref[0, :, c - QK_DIM:c - QK_DIM + HEAD_DIM] = seg.astype(k_ref.dtype)
            else:
                v_ref[0, :, c - 2 * QK_DIM:c - 2 * QK_DIM + HEAD_DIM] = seg.astype(v_ref.dtype)
        ci = c0 // cw
        for job in plain[ci * len(plain) // n_conv_chunks:(ci + 1) * len(plain) // n_conv_chunks]:
            job()
    g = _mm(u, wgate_ref[...])
    a = g + par_ref[1:2, :]
    softplus = jnp.maximum(a, 0.0) + jnp.log1p(jnp.exp(-jnp.abs(a)))
    log_g = -jnp.exp(par_ref[0:1, :]) * softplus
    lane = lax.broadcasted_iota(I32, g.shape, 1)
    go_ref[0] = jnp.where(lane < GATE_LANE0, _sigmoid(g), log_g)
    gr_ref[0] = jnp.transpose(log_g)[GATE_LANE0:GATE_LANE0 + 2 * NV_HEADS]


def _inconv_call(x3, sh, sc, g, w_qkv, w_gates, w_extra, extra_dtypes, conv_w, par, grid_w, use_rows, tm):
    b, t, d = x3.shape
    kern = functools.partial(_inconv_kernel, grid_w=grid_w, use_rows=use_rows, tm=tm, cw=512, n_extra=len(w_extra))
    per = tm // grid_w
    nrow = t // grid_w
    tile = lambda wd: pl.BlockSpec((1, tm, wd), lambda i, r: (i, r, 0))
    vec = pl.BlockSpec((1, 1, d), lambda i, r: (i, 0, 0))
    const = lambda a: pl.BlockSpec(a.shape, lambda i, r: (0,) * a.ndim, pipeline_mode=pl.Buffered(1))
    in_specs, args = [], []
    if use_rows:
        in_specs.append(pl.BlockSpec((1, grid_w, d), lambda i, r: (i, jnp.maximum(r * per - 1, 0), 0)))
        args.append(x3)
    in_specs.append(tile(d))
    args.append(x3)
    if use_rows:
        in_specs.append(pl.BlockSpec((1, grid_w, d), lambda i, r: (i, jnp.minimum((r + 1) * per, nrow - 1), 0)))
        args.append(x3)
    consts = [g, w_qkv, w_gates, *w_extra, conv_w, par]
    in_specs += [vec, vec] + [const(a) for a in consts]
    args += [sh, sc] + consts
    widths = [QK_DIM, QK_DIM, V_DIM, LANES] + [w.shape[1] for w in w_extra]
    dtypes = [BF16, BF16, BF16, F32] + list(extra_dtypes)
    out_shape = [jax.ShapeDtypeStruct((b, t, wd), dt) for wd, dt in zip(widths, dtypes)]
    out_specs = [tile(wd) for wd in widths]
    out_shape.insert(4, jax.ShapeDtypeStruct((b, 2 * NV_HEADS, t), F32))
    out_specs.insert(4, pl.BlockSpec((1, 2 * NV_HEADS, tm), lambda i, r: (i, 0, r)))
    return pl.pallas_call(
        kern, grid=(b, t // tm), in_specs=in_specs, out_specs=out_specs, out_shape=out_shape,
        compiler_params=_cparams(("parallel", "parallel"), 56),
        name="inconv_rows" if use_rows else "inconv_seq",
    )(*args)


def _gdn_kernel(qf, kf, vf, gf, rf, qb, kb, vb, gb, rb, s0_ref, of, ob, sfin_ref, s_ref):
    i = pl.program_id(1)
    nc = pl.num_programs(1)

    @pl.when(i == 0)
    def _():
        s_ref[...] = s0_ref[0]

    c = qf.shape[1]
    per = NV_HEADS // NQK_HEADS
    row = lax.broadcasted_iota(I32, (c, c), 0)
    colj = lax.broadcasted_iota(I32, (c, c), 1)
    eye = jnp.where(row == colj, 1.0, 0.0)
    nt_dims = (((1,), (1,)), ((), ()))
    tn_dims = (((0,), (0,)), ((), ()))

    seqs = []
    for d, (q_r, k_r, v_r, g_r, r_r, o_r) in enumerate(((qf, kf, vf, gf, rf, of), (qb, kb, vb, gb, rb, ob))):
        rev = d == 1
        incl = (colj >= row) if rev else (colj <= row)
        strict = (colj > row) if rev else (colj < row)
        gates = g_r[0]
        tri_c = jnp.where(incl, 1.0, 0.0)
        gcm = _mm(tri_c, gates, HIGHEST)
        gcr = _mm(r_r[0], tri_c, HIGHEST, dims=nt_dims)
        for hq in range(NQK_HEADS):
            q = q_r[0, :, hq * HEAD_DIM:(hq + 1) * HEAD_DIM]
            k = k_r[0, :, hq * HEAD_DIM:(hq + 1) * HEAD_DIM]
            kq = lax.dot_general(jnp.concatenate([q, k], axis=0), k, nt_dims, preferred_element_type=F32)
            for j in range(per):
                h = hq * per + j
                idx = d * NV_HEADS + h
                gc_c = gcm[:, GATE_LANE0 + idx:GATE_LANE0 + idx + 1]
                seqs.append(dict(d=d, h=h, o_r=o_r, v_r=v_r, q=q, k=k, qk=kq[:c], kk=kq[c:], incl=incl, strict=strict,
                                 beta=gates[:, idx:idx + 1], gc_c=gc_c, gc_r=gcr[idx:idx + 1, :],
                                 ge=gc_c[0:1] if rev else gc_c[c - 1:c]))

    def same_block(m):
        sh = int(math.log2(m))
        return jnp.right_shift(row, sh) == jnp.right_shift(colj, sh)

    for s in seqs:
        s['decay'] = jnp.where(s['incl'], jnp.exp(jnp.where(s['incl'], s['gc_c'] - s['gc_r'], 0.0)), 0.0)
        s['a'] = jnp.where(s['strict'], s['beta'] * s['kk'] * s['decay'], 0.0)
        s['t'] = eye - jnp.where(same_block(2), s['a'], 0.0)
    m = 4
    while m <= c:
        between = jnp.logical_and(same_block(m), jnp.logical_not(same_block(m // 2)))
        for s in seqs:
            s['te'] = _mm(s['t'], jnp.where(between, s['a'], 0.0))
        for s in seqs:
            s['t'] = s['t'] - _mm(s['te'], s['t'])
        m *= 2
    for s in seqs:
        h = s['h']
        egc = jnp.exp(s['gc_c'])
        kf32 = s['k'].astype(F32)
        v = s['v_r'][0, :, h * HEAD_DIM:(h + 1) * HEAD_DIM].astype(F32)
        rhs = jnp.concatenate([s['beta'] * v, (s['beta'] * egc) * kf32], axis=1)
        s['sol'] = _mm(s['t'], rhs)
        s['q_dec'] = s['q'].astype(F32) * egc
        s['k_dec'] = kf32 * jnp.exp(s['ge'] - s['gc_c'])
    for s in seqs:
        s['ws'] = _mm(jnp.concatenate([s['sol'][:, HEAD_DIM:], s['q_dec']], axis=0), s_ref[s['d'], s['h']])
    for s in seqs:
        s['u'] = s['sol'][:, :HEAD_DIM] - s['ws'][:c]
        s_ref[s['d'], s['h']] = (jnp.exp(s['ge']) * s_ref[s['d'], s['h']]
                                 + _mm(s['k_dec'], s['u'], dims=tn_dims))
    for s in seqs:
        h = s['h']
        o = s['ws'][c:] + _mm(s['qk'] * s['decay'], s['u'])
        s['o_r'][0, :, h * HEAD_DIM:(h + 1) * HEAD_DIM] = o.astype(s['o_r'].dtype)

    @pl.when(i == nc - 1)
    def _():
        sfin_ref[0] = s_ref[...]


def _gdn_call(q, k, v, go, rows, s0):
    b, t, _ = q.shape
    c = GDN_CHUNK
    nc = t // c
    fwd = lambda i, n: (i, n, 0)
    bwd = lambda i, n: (i, nc - 1 - n, 0)
    rfwd = lambda i, n: (i, 0, n)
    rbwd = lambda i, n: (i, 0, nc - 1 - n)
    state_spec = pl.BlockSpec((1,) + s0.shape[1:], lambda i, n: (i, 0, 0, 0, 0))

    def specs(m3, mr):
        return [pl.BlockSpec((1, c, QK_DIM), m3), pl.BlockSpec((1, c, QK_DIM), m3),
                pl.BlockSpec((1, c, V_DIM), m3), pl.BlockSpec((1, c, LANES), m3),
                pl.BlockSpec((1, 2 * NV_HEADS, c), mr)]

    return pl.pallas_call(
        _gdn_kernel,
        grid=(b, nc),
        in_specs=specs(fwd, rfwd) + specs(bwd, rbwd) + [state_spec],
        out_specs=[pl.BlockSpec((1, c, V_DIM), fwd), pl.BlockSpec((1, c, V_DIM), bwd), state_spec],
        out_shape=[jax.ShapeDtypeStruct((b, t, V_DIM), BF16), jax.ShapeDtypeStruct((b, t, V_DIM), BF16),
                   jax.ShapeDtypeStruct(s0.shape, F32)],
        scratch_shapes=[pltpu.VMEM(s0.shape[1:], F32)],
        compiler_params=_cparams(("parallel", "arbitrary"), 48),
        name="gdn",
    )(q, k, v, go, rows, q, k, v, go, rows, s0)


def _fnet1_kernel(x_ref, f_ref, ar_ref, ai_ref):
    n = x_ref.shape[1]
    for j in range(x_ref.shape[2]):
        a = _mm(f_ref[...], x_ref[0, :, j, :])
        ar_ref[0, j] = a[:n]
        ai_ref[0, j] = a[n:]


def _fnet2_kernel(ar_ref, ai_ref, g_ref, wc_ref, o_ref, *, scale):
    n = ar_ref.shape[1]
    cols = ar_ref.shape[2]
    zs = []
    for m in range(cols):
        a2 = jnp.concatenate([ar_ref[0, :, m, :], ai_ref[0, :, m, :]], axis=0)
        zs.append(_mm(g_ref[m], a2))
    zr = jnp.concatenate([z[:n] for z in zs], axis=0)
    zi = jnp.concatenate([z[n:] for z in zs], axis=0)
    for g0 in range(0, zr.shape[1], HEAD_DIM):
        y = _mm(jnp.concatenate([zr[:, g0:g0 + HEAD_DIM], zi[:, g0:g0 + HEAD_DIM]], axis=1), wc_ref[...])
        for m in range(cols):
            o_ref[0, :, m, g0:g0 + HEAD_DIM] = y[m * n:(m + 1) * n] * scale


def _fnet_tables(n):
    a = np.arange(n)
    ang1 = 2.0 * np.pi * np.outer(a, a) / n
    f1 = np.concatenate([np.cos(ang1), -np.sin(ang1)], axis=0)
    m = a[:, None] + n * a[None, :]
    ang2 = 2.0 * np.pi * ((m[:, :, None] * a[None, None, :]) % (n * n)) / (n * n)
    gc, gs = np.cos(ang2), np.sin(ang2)
    g2 = np.concatenate([np.concatenate([gc, gs], axis=2), np.concatenate([-gs, gc], axis=2)], axis=1)
    angc = 2.0 * np.pi * np.outer(np.arange(HEAD_DIM), np.arange(HEAD_DIM)) / HEAD_DIM
    wc = np.concatenate([np.cos(angc), np.sin(angc)], axis=0)
    f = lambda x: jnp.asarray(x, F32).astype(BF16)
    return f(f1), f(g2), f(wc)


def _fnet_call(f):
    b, l, c = f.shape
    n = GRID_W
    assert l == n * n
    f1, g2, wc = _fnet_tables(n)
    cols = 4 * SUBLANES
    ar, ai = pl.pallas_call(
        _fnet1_kernel,
        grid=(b, n // cols),
        in_specs=[pl.BlockSpec((1, n, cols, c), lambda i, j: (i, 0, j, 0)),
                  pl.BlockSpec((2 * n, n), lambda i, j: (0, 0))],
        out_specs=[pl.BlockSpec((1, cols, n, c), lambda i, j: (i, j, 0, 0))] * 2,
        out_shape=[jax.ShapeDtypeStruct((b, n, n, c), F32)] * 2,
        compiler_params=_cparams(("parallel", "parallel"), 32),
        name="fnet1",
    )(f.reshape(b, n, n, c), f1)
    out = pl.pallas_call(
        functools.partial(_fnet2_kernel, scale=1.0 / math.sqrt(l * HEAD_DIM)),
        grid=(b, n // cols),
        in_specs=[pl.BlockSpec((1, n, cols, c), lambda i, j: (i, 0, j, 0)),
                  pl.BlockSpec((1, n, cols, c), lambda i, j: (i, 0, j, 0)),
                  pl.BlockSpec((cols, 2 * n, 2 * n), lambda i, j: (j, 0, 0)),
                  pl.BlockSpec((2 * HEAD_DIM, HEAD_DIM), lambda i, j: (0, 0))],
        out_specs=pl.BlockSpec((1, n, cols, c), lambda i, j: (i, 0, j, 0)),
        out_shape=jax.ShapeDtypeStruct((b, n, n, c), F32),
        compiler_params=_cparams(("parallel", "parallel"), 32),
        name="fnet2",
    )(ar, ai, g2, wc)
    return out.reshape(b, l, c)


def _merge_kernel(of_ref, ob_ref, z_ref, fm_ref, gab_ref, x_ref, g1_ref, sh2_ref, sc2_ref, gn_ref, n2_ref,
                  wg_ref, wf_ref, wm_ref, wr_ref, br_ref, x1_ref, h2_ref, lg_ref):
    d = x_ref.shape[1]
    tm = x_ref.shape[0]
    halves = [slice(0, tm // 2), slice(tm // 2, tm)]
    yb_in = []
    for r in halves:
        o = of_ref[r, :].astype(F32) + ob_ref[r, :].astype(F32)
        z = z_ref[r, :].astype(F32)
        parts = []
        for h0 in range(0, V_DIM, HEAD_DIM):
            oh = o[:, h0:h0 + HEAD_DIM]
            parts.append(oh * lax.rsqrt(jnp.mean(oh * oh, axis=-1, keepdims=True) + EPS) * gn_ref[...])
        yb_in.append(jnp.concatenate(parts, axis=1) * (z * _sigmoid(z)))
    yb = [_mm(v, wg_ref[...]) for v in yb_in]
    ya = [_mm(fm_ref[r, :], wf_ref[...]) for r in halves]
    mixed = [_sigmoid(gab_ref[r, :d].astype(F32)) * a + _sigmoid(gab_ref[r, d:].astype(F32)) * b_
             for r, a, b_ in zip(halves, ya, yb)]
    mm = [_mm(v, wm_ref[...]) for v in mixed]
    w = wr_ref[...]
    w_hi = w.astype(BF16)
    w2 = jnp.concatenate([w_hi, (w - w_hi.astype(F32)).astype(BF16)], axis=1)
    for r, v in zip(halves, mm):
        x1 = x_ref[r, :] + g1_ref[0] * v
        x1_ref[r, :] = x1
        h2 = _rmsnorm(x1, n2_ref[...]) * (1.0 + sc2_ref[0]) + sh2_ref[0]
        h2_ref[r, :] = _pack_bf16_pairs(h2)
        h_hi = h2.astype(BF16)
        h_lo = (h2 - h_hi.astype(F32)).astype(BF16)
        part = _mm(h_hi, w2)
        lg_ref[r, :] = (part[:, :LANES] + part[:, LANES:]) + _mm(h_lo, w_hi) + br_ref[...]


def _merge_call(of, ob, z, fm, gab, x2, g1, sh2, sc2, gn, n2, wg, wf, wm, wr, br, tokens_per_batch, tm):
    n, d = x2.shape
    per = tokens_per_batch // tm
    tok = lambda wd: pl.BlockSpec((tm, wd), lambda i: (i, 0))
    vec = pl.BlockSpec((1, 1, d), lambda i: (i // per, 0, 0))
    full = lambda a: pl.BlockSpec(a.shape, lambda i: (0,) * a.ndim)
    return pl.pallas_call(
        _merge_kernel,
        grid=(n // tm,),
        in_specs=[tok(V_DIM), tok(V_DIM), tok(V_DIM), tok(F_DIM), tok(2 * d), tok(d), vec, vec, vec,
                  full(gn), full(n2), full(wg), full(wf), full(wm), full(wr), full(br)],
        out_specs=[tok(d), tok(d // 2), tok(LANES)],
        out_shape=[jax.ShapeDtypeStruct((n, d), F32), jax.ShapeDtypeStruct((n, d // 2), I32),
                   jax.ShapeDtypeStruct((n, LANES), F32)],
        compiler_params=_cparams(("parallel",), 56),
        name="merge",
    )(of, ob, z, fm, gab, x2, g1, sh2, sc2, gn, n2, wg, wf, wm, wr, br)


def _route_kernel(lg_ref, idx_ref, w_ref, rank_ref, cnt_ref, run_ref):
    i = pl.program_id(0)

    @pl.when(i == 0)
    def _():
        run_ref[...] = jnp.zeros_like(run_ref)

    l = lg_ref[...]
    tm = l.shape[0]
    lane = lax.broadcasted_iota(I32, l.shape, 1).astype(F32)
    vals, idxs = [], []
    for _ in range(TOP_K):
        m = jnp.max(l, axis=-1, keepdims=True)
        idx = jnp.min(jnp.where(l == m, lane, float(LANES)), axis=-1, keepdims=True)
        vals.append(m)
        idxs.append(idx)
        l = jnp.where(lane == idx, NEG_BIG * 2.0, l)
    es = [jnp.exp(v - vals[0]) for v in vals]
    inv = 1.0 / (es[0] + es[1] + es[2] + es[3])
    picked = jnp.zeros(l.shape, F32)
    for idx in idxs:
        picked = picked + (lane == idx).astype(F32)
    r = lax.broadcasted_iota(I32, (tm, tm), 0)
    cidx = lax.broadcasted_iota(I32, (tm, tm), 1)
    before = _mm(jnp.where(cidx < r, 1.0, 0.0), picked) + run_ref[...]
    idx_out = jnp.zeros(l.shape, F32)
    w_out = jnp.zeros(l.shape, F32)
    rank_out = jnp.zeros(l.shape, F32)
    for k in range(TOP_K):
        rk = jnp.sum(jnp.where(lane == idxs[k], before, 0.0), axis=-1, keepdims=True)
        idx_out = jnp.where(lane == k, idxs[k], idx_out)
        w_out = jnp.where(lane == k, es[k] * inv, w_out)
        rank_out = jnp.where(lane == k, rk, rank_out)
    idx_ref[...] = jnp.transpose(idx_out)[:SUBLANES].astype(I32)
    w_ref[...] = w_out
    rank_ref[...] = jnp.transpose(rank_out)[:SUBLANES].astype(I32)
    run_ref[...] = run_ref[...] + jnp.sum(picked, axis=0, keepdims=True)
    cnt_ref[...] = run_ref[...]


def _route_call(logits, tm):
    n = logits.shape[0]
    tok = pl.BlockSpec((tm, LANES), lambda i: (i, 0))
    tok_t = pl.BlockSpec((SUBLANES, tm), lambda i: (0, i))
    return pl.pallas_call(
        _route_kernel,
        grid=(n // tm,),
        in_specs=[tok],
        out_specs=[tok_t, tok, tok_t, pl.BlockSpec((1, LANES), lambda i: (0, 0))],
        out_shape=[jax.ShapeDtypeStruct((SUBLANES, n), I32), jax.ShapeDtypeStruct((n, LANES), F32),
                   jax.ShapeDtypeStruct((SUBLANES, n), I32), jax.ShapeDtypeStruct((1, LANES), F32)],
        scratch_shapes=[pltpu.VMEM((1, LANES), F32)],
        compiler_params=_cparams(("arbitrary",), 32),
        name="route",
    )(logits)


def _sc_mesh():
    return plsc.VectorSubcoreMesh(core_axis_name="c", subcore_axis_name="s",
                                  num_cores=SC_CORES, num_subcores=SC_SUBCORES)


def _sc_worker_id():
    return lax.axis_index("s") * SC_CORES + lax.axis_index("c")


def _sc_scatter_rows(x, idx, n_out):
    n, d = x.shape
    per_w = idx.shape[0] // SC_WORKERS
    nwin = per_w // SC_WINDOW
    assert per_w * SC_WORKERS == idx.shape[0] and nwin * SC_WINDOW == per_w and nwin % 2 == 0 and n % per_w == 0

    def body(x_hbm, idx_hbm, out_hbm, idx_v, rows_v, sem_r, sem_w):
        wid = _sc_worker_id()
        row0 = lax.rem(wid * per_w, n)
        pltpu.sync_copy(idx_hbm.at[wid], idx_v)

        def read(j, b):
            return pltpu.make_async_copy(x_hbm.at[pl.ds(row0 + j * SC_WINDOW, SC_WINDOW)], rows_v.at[b], sem_r.at[b])

        def write(j, b):
            return pltpu.make_async_copy(rows_v.at[b], out_hbm.at[idx_v.at[j]], sem_w.at[b])

        @pl.loop(0, nwin, step=2)
        def _(j):
            read(j, 0).start()
            read(j + 1, 1).start()
            read(j, 0).wait()
            write(j, 0).start()
            read(j + 1, 1).wait()
            write(j + 1, 1).start()
            write(j, 0).wait()
            write(j + 1, 1).wait()

    return pl.kernel(
        body, out_type=jax.ShapeDtypeStruct((n_out, d), x.dtype), mesh=_sc_mesh(),
        scratch_types=[pltpu.VMEM((nwin, SC_WINDOW), I32), pltpu.VMEM((2, SC_WINDOW, d), x.dtype),
                       pltpu.SemaphoreType.DMA((2,)), pltpu.SemaphoreType.DMA((2,))],
        name="sc_scatter_rows",
    )(x, idx.reshape(SC_WORKERS, nwin, SC_WINDOW))


def _sc_gather_rows(y, idx):
    d = y.shape[1]
    total = idx.shape[0]
    per_w = total // SC_WORKERS
    nwin = per_w // SC_WINDOW
    assert per_w * SC_WORKERS == total and nwin * SC_WINDOW == per_w and nwin % 2 == 0

    def body(y_hbm, idx_hbm, out_hbm, idx_v, rows_v, sem_r, sem_w):
        wid = _sc_worker_id()
        row0 = wid * per_w
        pltpu.sync_copy(idx_hbm.at[wid], idx_v)

        def read(j, b):
            return pltpu.make_async_copy(y_hbm.at[idx_v.at[j]], rows_v.at[b], sem_r.at[b])

        def write(j, b):
            return pltpu.make_async_copy(rows_v.at[b], out_hbm.at[pl.ds(row0 + j * SC_WINDOW, SC_WINDOW)], sem_w.at[b])

        @pl.loop(0, nwin, step=2)
        def _(j):
            read(j, 0).start()
            read(j + 1, 1).start()
            read(j, 0).wait()
            write(j, 0).start()
            read(j + 1, 1).wait()
            write(j + 1, 1).start()
            write(j, 0).wait()
            write(j + 1, 1).wait()

    return pl.kernel(
        body, out_type=jax.ShapeDtypeStruct((total, d), y.dtype), mesh=_sc_mesh(),
        scratch_types=[pltpu.VMEM((nwin, SC_WINDOW), I32), pltpu.VMEM((2, SC_WINDOW, d), y.dtype),
                       pltpu.SemaphoreType.DMA((2,)), pltpu.SemaphoreType.DMA((2,))],
        name="sc_gather_rows",
    )(y, idx.reshape(SC_WORKERS, nwin, SC_WINDOW))


def _expert_kernel(be_ref, nv_ref, nxt_ref, par_ref, used_ref, x_ref, wg_hbm, wu_hbm, wd_hbm, bg_ref, bu_ref, bd_ref,
                   y_ref, wgf, wuf, wdf, wgb, wub, wdb, sem):
    i = pl.program_id(0)

    def fetch(e, s):
        return [pltpu.make_async_copy(w_hbm.at[e], w_f.at[s], sem.at[s])
                for w_hbm, w_f in ((wg_hbm, wgf), (wu_hbm, wuf), (wd_hbm, wdf))]

    @pl.when(i < nv_ref[0])
    def _():
        e = be_ref[i]
        s = par_ref[e]

        @pl.when(i == 0)
        def _():
            for cp in fetch(e, s):
                cp.start()

        first = jnp.logical_or(i == 0, be_ref[jnp.maximum(i - 1, 0)] != e)
        quarter = x_ref.shape[0] // 4
        full = used_ref[i] > 3 * quarter

        @pl.when(first)
        def _():
            for cp in fetch(e, s):
                cp.wait()

            @pl.when(nxt_ref[e] >= 0)
            def _():
                for cp in fetch(nxt_ref[e], 1 - s):
                    cp.start()

        def ffn(nrows, cast):
            x = _unpack_bf16_pairs(x_ref[:nrows, :]).astype(BF16)
            if cast:
                wgb[...] = wgf[s].astype(BF16)
            gate = jnp.minimum(_mm(x, wgb[...]) + bg_ref[0], SWIGLU_LIMIT)
            if cast:
                wub[...] = wuf[s].astype(BF16)
            up = jnp.clip(_mm(x, wub[...]) + bu_ref[0], -SWIGLU_LIMIT, SWIGLU_LIMIT)
            if cast:
                wdb[...] = wdf[s].astype(BF16)
            act = (up + 1.0) * gate * _sigmoid(SWIGLU_ALPHA * gate)
            y_ref[:nrows, :] = _pack_bf16_pairs(_mm(act, wdb[...]) + bd_ref[0])

        @pl.when(jnp.logical_and(first, full))
        def _():
            ffn(4 * quarter, True)

        @pl.when(jnp.logical_and(first, jnp.logical_not(full)))
        def _():
            wgb[...] = wgf[s].astype(BF16)
            wub[...] = wuf[s].astype(BF16)
            wdb[...] = wdf[s].astype(BF16)

        for nq in range(1, 5):
            in_range = jnp.logical_and(used_ref[i] > (nq - 1) * quarter, used_ref[i] <= nq * quarter)
            if nq == 4:
                in_range = jnp.logical_and(in_range, jnp.logical_not(first))

            @pl.when(in_range)
            def _(nq=nq):
                ffn(nq * quarter, False)


def _expert_call(block_e, n_valid, next_e, parity, used, xs, w_gate, w_up, w_down, b_gate, b_up, b_down):
    nb = block_e.shape[0]
    tmb = EXPERT_BLOCK
    ne, d, de = w_gate.shape
    bspec = lambda s: pl.BlockSpec((1,) + s, lambda i, be, nv, nx, pa, us: (be[i], 0, 0))
    anyspec = pl.BlockSpec(memory_space=pl.ANY)
    grid_spec = pltpu.PrefetchScalarGridSpec(
        num_scalar_prefetch=5,
        grid=(nb,),
        in_specs=[pl.BlockSpec((tmb, d // 2), lambda i, be, nv, nx, pa, us: (i, 0)),
                  anyspec, anyspec, anyspec, bspec((1, de)), bspec((1, de)), bspec((1, d))],
        out_specs=pl.BlockSpec((tmb, d // 2), lambda i, be, nv, nx, pa, us: (i, 0)),
        scratch_shapes=[pltpu.VMEM((2, d, de), F32), pltpu.VMEM((2, d, de), F32), pltpu.VMEM((2, de, d), F32),
                        pltpu.VMEM((d, de), BF16), pltpu.VMEM((d, de), BF16), pltpu.VMEM((de, d), BF16),
                        pltpu.SemaphoreType.DMA((2,))],
    )
    return pl.pallas_call(
        _expert_kernel,
        grid_spec=grid_spec,
        out_shape=jax.ShapeDtypeStruct(xs.shape, xs.dtype),
        compiler_params=_cparams(("arbitrary",), 56),
        name="expert",
    )(block_e, n_valid, next_e, parity, used, xs, w_gate, w_up, w_down,
      b_gate.reshape(ne, 1, de), b_up.reshape(ne, 1, de), b_down.reshape(ne, 1, d))


def _combine_kernel(y0, y1, y2, y3, w_ref, x1_ref, g2_ref, fg_ref, o_ref):
    w = w_ref[...]
    ys = [_unpack_bf16_pairs(y[...]) for y in (y0, y1, y2, y3)]
    moe = (w[:, 0:1] * ys[0] + w[:, 1:2] * ys[1]) + (w[:, 2:3] * ys[2] + w[:, 3:4] * ys[3])
    o_ref[...] = _rmsnorm(x1_ref[...] + g2_ref[0] * moe, fg_ref[...])


def _combine_call(y4, top_w, x1, g2, fg, tokens_per_batch, tm):
    n, d = x1.shape
    per = tokens_per_batch // tm
    nt = n // tm
    yspec = lambda k: pl.BlockSpec((tm, d // 2), lambda i: (k * nt + i, 0))
    return pl.pallas_call(
        _combine_kernel,
        grid=(nt,),
        in_specs=[yspec(0), yspec(1), yspec(2), yspec(3),
                  pl.BlockSpec((tm, LANES), lambda i: (i, 0)),
                  pl.BlockSpec((tm, d), lambda i: (i, 0)),
                  pl.BlockSpec((1, 1, d), lambda i: (i // per, 0, 0)),
                  pl.BlockSpec((1, d), lambda i: (0, 0))],
        out_specs=pl.BlockSpec((tm, d), lambda i: (i, 0)),
        out_shape=jax.ShapeDtypeStruct((n, d), F32),
        compiler_params=_cparams(("parallel",), 48),
        name="combine",
    )(y4, y4, y4, y4, top_w, x1, g2, fg)


def _slot_kernel(start_ref, idx_ref, rank_ref, o_ref):
    idx = idx_ref[...]
    acc = rank_ref[...]
    for e in range(N_EXPERTS):
        acc = acc + jnp.where(idx == e, start_ref[e], 0)
    o_ref[...] = acc


def _slot_call(pad_start, top_idx, rank):
    full = pl.BlockSpec(top_idx.shape, lambda i, s: (0, 0))
    return pl.pallas_call(
        _slot_kernel,
        grid_spec=pltpu.PrefetchScalarGridSpec(num_scalar_prefetch=1, grid=(1,), in_specs=[full, full],
                                               out_specs=full),
        out_shape=jax.ShapeDtypeStruct(top_idx.shape, I32),
        compiler_params=_cparams(("arbitrary",), 32),
        name="slots",
    )(pad_start, top_idx, rank)


def _routing_tables(top_idx, rank, counts, n):
    tmb = EXPERT_BLOCK
    counts = counts.astype(I32)
    padded = (counts + tmb - 1) // tmb * tmb
    pad_end = jnp.cumsum(padded)
    pad_start = pad_end - padded
    n_blocks = -(-(n * TOP_K + N_EXPERTS * (tmb - 1)) // tmb)
    n_slots = n_blocks * tmb
    dest_flat = _slot_call(pad_start, top_idx, rank)[:TOP_K].reshape(-1)
    block_start = jnp.arange(n_blocks, dtype=I32) * tmb
    block_e = jnp.minimum(jnp.sum((pad_end[None, :] <= block_start[:, None]).astype(I32), axis=1), N_EXPERTS - 1)
    n_valid = (pad_end[-1:] // tmb).astype(I32)
    experts = jnp.arange(N_EXPERTS, dtype=I32)
    used = counts > 0
    later = jnp.where(jnp.logical_and(used[None, :], experts[None, :] > experts[:, None]), experts[None, :], N_EXPERTS)
    next_e = jnp.min(later, axis=1)
    next_e = jnp.where(next_e == N_EXPERTS, -1, next_e).astype(I32)
    parity = ((jnp.cumsum(used.astype(I32)) - used.astype(I32)) % 2).astype(I32)
    row_end = jnp.sum(jnp.where(block_e[:, None] == experts[None, :], (pad_start + counts)[None, :], 0), axis=1)
    rows_used = jnp.clip(row_end - block_start, 0, tmb).astype(I32)
    return block_e, n_valid, next_e, parity, rows_used, dest_flat, n_slots


def _gdn_branch(x3, sh, sc, norm_g, w_qkv, w_gates, w_extra, extra_dtypes, conv_w, par, grid_w, use_rows, tm, s0):
    q, k, v, go, rows, *extra = _inconv_call(x3, sh, sc, norm_g, w_qkv, w_gates, w_extra, extra_dtypes, conv_w, par,
                                             grid_w, use_rows, tm)
    o_f, o_b, s_fin = _gdn_call(q, k, v, go, rows, s0)
    return extra, o_f, o_b, s_fin


def kernel(x, c, ctx, c_ctx, w_mod, b_mod, norm1_g, norm2_g, w_in, conv_w, a_log, dt_bias, gdn_norm_g,
           w_fourier_out, w_gdn_out, w_merge_out, w_router, b_router, w_gate, b_gate, w_up, b_up,
           w_down, b_down, final_norm_g):
    b, l, d = x.shape
    n = b * l
    n_ctx = ctx.shape[1]
    assert w_mod.shape[0] == 1 and l == GRID_W * GRID_W and d == V_DIM

    c8 = jnp.concatenate([c, c_ctx[None, :], jnp.zeros((8 - b - 1, d), F32)], axis=0)
    mod = _mod_call(c8, w_mod[0], b_mod[0])
    sh1, sc1, g1, sh2, sc2, g2 = [mod[:b, j * d:(j + 1) * d].reshape(b, 1, d) for j in range(6)]
    csh1 = jnp.broadcast_to(mod[b:b + 1, 0:d].reshape(1, 1, d), (b, 1, d))
    csc1 = jnp.broadcast_to(mod[b:b + 1, d:2 * d].reshape(1, 1, d), (b, 1, d))

    off_gate = QKV_DIM
    off_z = off_gate + 4 * NV_HEADS
    off_f = off_z + V_DIM
    off_ga = off_f + F_DIM
    w_qkv, w_gates, *w_extra = _wsplit_call(w_in, (0, off_gate, off_z, off_f, off_ga, w_in.shape[2]))
    par = jnp.pad(jnp.stack([a_log[0].reshape(-1), dt_bias[0].reshape(-1)]),
                  ((0, 6), (GATE_LANE0, LANES - 2 * GATE_LANE0)))
    n1 = norm1_g[0].reshape(1, d)
    cw = conv_w[0].reshape(9, QKV_DIM)

    zero_state = jnp.zeros((b, 2, NV_HEADS, HEAD_DIM, HEAD_DIM), F32)
    _, _, _, s_ctx = _gdn_branch(ctx, csh1, csc1, n1, w_qkv, w_gates, [], [], cw, par, n_ctx, False, n_ctx,
                                 zero_state)

    x2 = x.reshape(n, d)
    (z, f, gab), o_f, o_b, _ = _gdn_branch(x, sh1, sc1, n1, w_qkv, w_gates, w_extra, (BF16, F32, BF16),
                                           cw, par, GRID_W, True, TOKEN_TILE, s_ctx)
    z, gab = z.reshape(n, V_DIM), gab.reshape(n, 2 * d)
    fmix = _fnet_call(f).reshape(n, F_DIM)

    wr = jnp.pad(w_router[0], ((0, 0), (0, LANES - N_EXPERTS)))
    br = jnp.pad(b_router[0], (0, LANES - N_EXPERTS), constant_values=NEG_BIG).reshape(1, LANES)
    x1, h2, logits = _merge_call(
        o_f.reshape(n, V_DIM), o_b.reshape(n, V_DIM), z, fmix, gab, x2, g1, sh2, sc2,
        gdn_norm_g[0].reshape(1, HEAD_DIM), norm2_g[0].reshape(1, d),
        w_gdn_out[0].astype(BF16), w_fourier_out[0].astype(BF16), w_merge_out[0].astype(BF16), wr, br, l, TOKEN_TILE)

    top_idx, top_w, rank, counts = _route_call(logits, 2 * TOKEN_TILE)
    block_e, n_valid, next_e, parity, rows_used, dest_flat, n_slots = _routing_tables(
        top_idx, rank, counts[0, :N_EXPERTS], n)
    xs = _sc_scatter_rows(h2, dest_flat, n_slots)
    ys = _expert_call(block_e, n_valid, next_e, parity, rows_used, xs, w_gate[0], w_up[0], w_down[0],
                      b_gate[0], b_up[0], b_down[0])
    y4 = _sc_gather_rows(ys, dest_flat)
    out = _combine_call(y4, top_w, x1, g2, final_norm_g.reshape(1, d), l, 2 * TOKEN_TILE)
    return out.reshape(b, l, d)
```

```python
import functools
import math

import jax
import jax.numpy as jnp
import numpy as np
from jax import lax
from jax.experimental import pallas as pl
from jax.experimental.pallas import tpu as pltpu
from jax.experimental.pallas import tpu_sc as plsc

F32 = jnp.float32
BF16 = jnp.bfloat16
I32 = jnp.int32
HIGHEST = lax.Precision.HIGHEST

GRID_W = 64
NQK_HEADS = 4
NV_HEADS = 8
HEAD_DIM = 128
QK_DIM = NQK_HEADS * HEAD_DIM
V_DIM = NV_HEADS * HEAD_DIM
QKV_DIM = 2 * QK_DIM + V_DIM
F_GROUPS = 4
F_DIM = F_GROUPS * HEAD_DIM
N_EXPERTS = 32
TOP_K = 4
SWIGLU_ALPHA = 1.702
SWIGLU_LIMIT = 7.0
EPS = 1e-6

LANES = 128
SUBLANES = 8
GATE_LANE0 = 16
GDN_CHUNK = 128
EXPERT_BLOCK = 1024
TOKEN_TILE = 512
SC_CORES = 2
SC_SUBCORES = 16
SC_WORKERS = SC_CORES * SC_SUBCORES
SC_WINDOW = 64
NEG_BIG = -1e30
MIB = 2 ** 20


def _cparams(sem, vmem_mib):
    return pltpu.CompilerParams(dimension_semantics=sem, vmem_limit_bytes=vmem_mib * MIB)


def _mm(a, b, prec=None, dims=(((1,), (0,)), ((), ()))):
    if prec is None:
        return lax.dot_general(a.astype(BF16), b.astype(BF16), dims, preferred_element_type=F32)
    return lax.dot_general(a.astype(F32), b.astype(F32), dims, precision=prec, preferred_element_type=F32)


def _sigmoid(x):
    return 0.5 * jnp.tanh(0.5 * x) + 0.5


def _rmsnorm(x, g):
    return x * lax.rsqrt(jnp.mean(x * x, axis=-1, keepdims=True) + EPS) * g


def _pack_bf16_pairs(x):
    half = x.shape[1] // 2
    bits = lax.bitcast_convert_type(x.astype(BF16).astype(F32), jnp.uint32)
    packed = jnp.bitwise_or(jnp.right_shift(bits[:, :half], jnp.uint32(16)),
                            jnp.bitwise_and(bits[:, half:], jnp.uint32(0xFFFF0000)))
    return lax.bitcast_convert_type(packed, I32)


def _unpack_bf16_pairs(p):
    bits = lax.bitcast_convert_type(p, jnp.uint32)
    lo = lax.bitcast_convert_type(jnp.left_shift(bits, jnp.uint32(16)), F32)
    hi = lax.bitcast_convert_type(jnp.bitwise_and(bits, jnp.uint32(0xFFFF0000)), F32)
    return jnp.concatenate([lo, hi], axis=1)


def _mod_kernel(c_ref, w_ref, b_ref, o_ref):
    c = c_ref[...]
    o_ref[...] = _mm(c * _sigmoid(c), w_ref[...], HIGHEST) + b_ref[...]


def _mod_call(c8, w_mod, b_mod):
    d, n = w_mod.shape
    tn = 1536
    return pl.pallas_call(
        _mod_kernel,
        grid=(n // tn,),
        in_specs=[pl.BlockSpec((8, d), lambda j: (0, 0)),
                  pl.BlockSpec((d, tn), lambda j: (0, j)),
                  pl.BlockSpec((1, tn), lambda j: (0, j))],
        out_specs=pl.BlockSpec((8, tn), lambda j: (0, j)),
        out_shape=jax.ShapeDtypeStruct((8, n), F32),
        compiler_params=_cparams(("parallel",), 32),
        name="mod",
    )(c8, w_mod, b_mod.reshape(1, n))


def _wsplit_kernel(w_ref, *o_refs, bounds):
    for o_ref, lo, hi in zip(o_refs, bounds[:-1], bounds[1:]):
        o_ref[:, :hi - lo] = w_ref[0, :, lo:hi].astype(o_ref.dtype)
        if o_ref.shape[1] > hi - lo:
            o_ref[:, hi - lo:] = jnp.zeros((o_ref.shape[0], o_ref.shape[1] - (hi - lo)), o_ref.dtype)


def _wsplit_call(w, bounds):
    d = w.shape[1]
    tr = 256
    widths = [-(-(hi - lo) // LANES) * LANES for lo, hi in zip(bounds[:-1], bounds[1:])]
    return pl.pallas_call(
        functools.partial(_wsplit_kernel, bounds=bounds),
        grid=(d // tr,),
        in_specs=[pl.BlockSpec((1, tr, w.shape[2]), lambda i: (0, i, 0))],
        out_specs=[pl.BlockSpec((tr, wd), lambda i: (i, 0)) for wd in widths],
        out_shape=[jax.ShapeDtypeStruct((d, wd), BF16) for wd in widths],
        compiler_params=_cparams(("parallel",), 32),
        name="wsplit",
    )(w)


def _inconv_kernel(*refs, grid_w, use_rows, tm, cw, n_extra):
    refs = list(refs)
    prev_ref = refs.pop(0) if use_rows else None
    x_ref = refs.pop(0)
    next_ref = refs.pop(0) if use_rows else None
    sh_ref, sc_ref, g_ref, wq_ref, wgate_ref = refs[:5]
    wx_refs = refs[5:5 + n_extra]
    w_ref, par_ref, q_ref, k_ref, v_ref, go_ref, gr_ref = refs[5 + n_extra:12 + n_extra]
    ox_refs = refs[12 + n_extra:]

    def modulated(x):
        return (_rmsnorm(x, g_ref[...]) * (1.0 + sc_ref[0]) + sh_ref[0]).astype(BF16)

    r = pl.program_id(1)
    nr = pl.num_programs(1)
    u = modulated(x_ref[0])

    def project(wx_ref, o_ref, c0, step):
        o_ref[0, :, c0:c0 + step] = _mm(u, wx_ref[:, c0:c0 + step]).astype(o_ref.dtype)

    plain = [functools.partial(project, wx_ref, o_ref, c0, min(wx_ref.shape[1], 512))
             for wx_ref, o_ref in zip(wx_refs, ox_refs) for c0 in range(0, wx_ref.shape[1], min(wx_ref.shape[1], 512))]
    n_conv_chunks = QKV_DIM // cw

    t = lax.broadcasted_iota(I32, (tm, 1), 0)
    col = jnp.bitwise_and(t, grid_w - 1)
    m_left = (col != 0).astype(F32)
    m_right = (col != grid_w - 1).astype(F32)
    if use_rows:
        u_prev = modulated(prev_ref[0])
        u_next = modulated(next_ref[0])
        has_prev = (r > 0).astype(F32)
        has_next = (r < nr - 1).astype(F32)
    for c0 in range(0, QKV_DIM, cw):
        wq = wq_ref[:, c0:c0 + cw]
        xm = _mm(u, wq)
        if use_rows:
            up = jnp.concatenate([_mm(u_prev, wq) * has_prev, xm[:tm - grid_w]], axis=0)
            dn = jnp.concatenate([xm[grid_w:], _mm(u_next, wq) * has_next], axis=0)

        def colsum(kc):
            y = xm * w_ref[3 + kc:4 + kc, c0:c0 + cw]
            if use_rows:
                y = y + up * w_ref[kc:kc + 1, c0:c0 + cw] + dn * w_ref[6 + kc:7 + kc, c0:c0 + cw]
            return y

        acc = (colsum(1) + pltpu.roll(colsum(0), 1, axis=0) * m_left
               + pltpu.roll(colsum(2), tm - 1, axis=0) * m_right)
        s = acc * _sigmoid(acc)
        for h0 in range(0, cw, HEAD_DIM):
            c = c0 + h0
            seg = s[:, h0:h0 + HEAD_DIM]
            if c < 2 * QK_DIM:
                seg = seg * lax.rsqrt(jnp.sum(seg * seg, axis=-1, keepdims=True) + EPS)
            if c < QK_DIM:
                q_ref[0, :, c:c + HEAD_DIM] = (seg * HEAD_DIM ** -0.5).astype(q_ref.dtype)
            elif c < 2 * QK_DIM:
                k_ref[0, :, c - QK_DIM:c - QK_DIM + HEAD_DIM] = seg.astype(k_ref.dtype)
            else:
                v_ref[0, :, c - 2 * QK_DIM:c - 2 * QK_DIM + HEAD_DIM] = seg.astype(v_ref.dtype)
        ci = c0 // cw
        for job in plain[ci * len(plain) // n_conv_chunks:(ci + 1) * len(plain) // n_conv_chunks]:
            job()
    g = _mm(u, wgate_ref[...])
    a = g + par_ref[1:2, :]
    softplus = jnp.maximum(a, 0.0) + jnp.log1p(jnp.exp(-jnp.abs(a)))
    log_g = -jnp.exp(par_ref[0:1, :]) * softplus
    lane = lax.broadcasted_iota(I32, g.shape, 1)
    go_ref[0] = jnp.where(lane < GATE_LANE0, _sigmoid(g), log_g)
    gr_ref[0] = jnp.transpose(log_g)[GATE_LANE0:GATE_LANE0 + 2 * NV_HEADS]


def _inconv_call(x3, sh, sc, g, w_qkv, w_gates, w_extra, extra_dtypes, conv_w, par, grid_w, use_rows, tm):
    b, t, d = x3.shape
    kern = functools.partial(_inconv_kernel, grid_w=grid_w, use_rows=use_rows, tm=tm, cw=512, n_extra=len(w_extra))
    per = tm // grid_w
    nrow = t // grid_w
    tile = lambda wd: pl.BlockSpec((1, tm, wd), lambda i, r: (i, r, 0))
    vec = pl.BlockSpec((1, 1, d), lambda i, r: (i, 0, 0))
    const = lambda a: pl.BlockSpec(a.shape, lambda i, r: (0,) * a.ndim, pipeline_mode=pl.Buffered(1))
    in_specs, args = [], []
    if use_rows:
        in_specs.append(pl.BlockSpec((1, grid_w, d), lambda i, r: (i, jnp.maximum(r * per - 1, 0), 0)))
        args.append(x3)
    in_specs.append(tile(d))
    args.append(x3)
    if use_rows:
        in_specs.append(pl.BlockSpec((1, grid_w, d), lambda i, r: (i, jnp.minimum((r + 1) * per, nrow - 1), 0)))
        args.append(x3)
    consts = [g, w_qkv, w_gates, *w_extra, conv_w, par]
    in_specs += [vec, vec] + [const(a) for a in consts]
    args += [sh, sc] + consts
    widths = [QK_DIM, QK_DIM, V_DIM, LANES] + [w.shape[1] for w in w_extra]
    dtypes = [BF16, BF16, BF16, F32] + list(extra_dtypes)
    out_shape = [jax.ShapeDtypeStruct((b, t, wd), dt) for wd, dt in zip(widths, dtypes)]
    out_specs = [tile(wd) for wd in widths]
    out_shape.insert(4, jax.ShapeDtypeStruct((b, 2 * NV_HEADS, t), F32))
    out_specs.insert(4, pl.BlockSpec((1, 2 * NV_HEADS, tm), lambda i, r: (i, 0, r)))
    return pl.pallas_call(
        kern, grid=(b, t // tm), in_specs=in_specs, out_specs=out_specs, out_shape=out_shape,
        compiler_params=_cparams(("parallel", "parallel"), 56),
        name="inconv_rows" if use_rows else "inconv_seq",
    )(*args)


def _gdn_kernel(qf, kf, vf, gf, rf, qb, kb, vb, gb, rb, s0_ref, of, ob, sfin_ref, s_ref):
    i = pl.program_id(1)
    nc = pl.num_programs(1)

    @pl.when(i == 0)
    def _():
        s_ref[...] = s0_ref[0]

    c = qf.shape[1]
    per = NV_HEADS // NQK_HEADS
    row = lax.broadcasted_iota(I32, (c, c), 0)
    colj = lax.broadcasted_iota(I32, (c, c), 1)
    eye = jnp.where(row == colj, 1.0, 0.0)
    nt_dims = (((1,), (1,)), ((), ()))
    tn_dims = (((0,), (0,)), ((), ()))

    seqs = []
    for d, (q_r, k_r, v_r, g_r, r_r, o_r) in enumerate(((qf, kf, vf, gf, rf, of), (qb, kb, vb, gb, rb, ob))):
        rev = d == 1
        incl = (colj >= row) if rev else (colj <= row)
        strict = (colj > row) if rev else (colj < row)
        gates = g_r[0]
        tri_c = jnp.where(incl, 1.0, 0.0)
        gcm = _mm(tri_c, gates, HIGHEST)
        gcr = _mm(r_r[0], tri_c, HIGHEST, dims=nt_dims)
        for hq in range(NQK_HEADS):
            q = q_r[0, :, hq * HEAD_DIM:(hq + 1) * HEAD_DIM]
            k = k_r[0, :, hq * HEAD_DIM:(hq + 1) * HEAD_DIM]
            kq = lax.dot_general(jnp.concatenate([q, k], axis=0), k, nt_dims, preferred_element_type=F32)
            for j in range(per):
                h = hq * per + j
                idx = d * NV_HEADS + h
                gc_c = gcm[:, GATE_LANE0 + idx:GATE_LANE0 + idx + 1]
                seqs.append(dict(d=d, h=h, o_r=o_r, v_r=v_r, q=q, k=k, qk=kq[:c], kk=kq[c:], incl=incl, strict=strict,
                                 beta=gates[:, idx:idx + 1], gc_c=gc_c, gc_r=gcr[idx:idx + 1, :],
                                 ge=gc_c[0:1] if rev else gc_c[c - 1:c]))

    def same_block(m):
        sh = int(math.log2(m))
        return jnp.right_shift(row, sh) == jnp.right_shift(colj, sh)

    for s in seqs:
        s['decay'] = jnp.where(s['incl'], jnp.exp(jnp.where(s['incl'], s['gc_c'] - s['gc_r'], 0.0)), 0.0)
        s['a'] = jnp.where(s['strict'], s['beta'] * s['kk'] * s['decay'], 0.0)
        s['t'] = eye - jnp.where(same_block(2), s['a'], 0.0)
    m = 4
    while m <= c:
        between = jnp.logical_and(same_block(m), jnp.logical_not(same_block(m // 2)))
        for s in seqs:
            s['te'] = _mm(s['t'], jnp.where(between, s['a'], 0.0))
        for s in seqs:
            s['t'] = s['t'] - _mm(s['te'], s['t'])
        m *= 2
    for s in seqs:
        h = s['h']
        egc = jnp.exp(s['gc_c'])
        kf32 = s['k'].astype(F32)
        v = s['v_r'][0, :, h * HEAD_DIM:(h + 1) * HEAD_DIM].astype(F32)
        rhs = jnp.concatenate([s['beta'] * v, (s['beta'] * egc) * kf32], axis=1)
        s['sol'] = _mm(s['t'], rhs)
        s['q_dec'] = s['q'].astype(F32) * egc
        s['k_dec'] = kf32 * jnp.exp(s['ge'] - s['gc_c'])
    for s in seqs:
        s['ws'] = _mm(jnp.concatenate([s['sol'][:, HEAD_DIM:], s['q_dec']], axis=0), s_ref[s['d'], s['h']])
    for s in seqs:
        s['u'] = s['sol'][:, :HEAD_DIM] - s['ws'][:c]
        s_ref[s['d'], s['h']] = (jnp.exp(s['ge']) * s_ref[s['d'], s['h']]
                                 + _mm(s['k_dec'], s['u'], dims=tn_dims))
    for s in seqs:
        h = s['h']
        o = s['ws'][c:] + _mm(s['qk'] * s['decay'], s['u'])
        s['o_r'][0, :, h * HEAD_DIM:(h + 1) * HEAD_DIM] = o.astype(s['o_r'].dtype)

    @pl.when(i == nc - 1)
    def _():
        sfin_ref[0] = s_ref[...]


def _gdn_call(q, k, v, go, rows, s0):
    b, t, _ = q.shape
    c = GDN_CHUNK
    nc = t // c
    fwd = lambda i, n: (i, n, 0)
    bwd = lambda i, n: (i, nc - 1 - n, 0)
    rfwd = lambda i, n: (i, 0, n)
    rbwd = lambda i, n: (i, 0, nc - 1 - n)
    state_spec = pl.BlockSpec((1,) + s0.shape[1:], lambda i, n: (i, 0, 0, 0, 0))

    def specs(m3, mr):
        return [pl.BlockSpec((1, c, QK_DIM), m3), pl.BlockSpec((1, c, QK_DIM), m3),
                pl.BlockSpec((1, c, V_DIM), m3), pl.BlockSpec((1, c, LANES), m3),
                pl.BlockSpec((1, 2 * NV_HEADS, c), mr)]

    return pl.pallas_call(
        _gdn_kernel,
        grid=(b, nc),
        in_specs=specs(fwd, rfwd) + specs(bwd, rbwd) + [state_spec],
        out_specs=[pl.BlockSpec((1, c, V_DIM), fwd), pl.BlockSpec((1, c, V_DIM), bwd), state_spec],
        out_shape=[jax.ShapeDtypeStruct((b, t, V_DIM), BF16), jax.ShapeDtypeStruct((b, t, V_DIM), BF16),
                   jax.ShapeDtypeStruct(s0.shape, F32)],
        scratch_shapes=[pltpu.VMEM(s0.shape[1:], F32)],
        compiler_params=_cparams(("parallel", "arbitrary"), 48),
        name="gdn",
    )(q, k, v, go, rows, q, k, v, go, rows, s0)


def _fnet1_kernel(x_ref, f_ref, ar_ref, ai_ref):
    n = x_ref.shape[1]
    for j in range(x_ref.shape[2]):
        a = _mm(f_ref[...], x_ref[0, :, j, :])
        ar_ref[0, j] = a[:n]
        ai_ref[0, j] = a[n:]


def _fnet2_kernel(ar_ref, ai_ref, g_ref, wc_ref, o_ref, *, scale):
    n = ar_ref.shape[1]
    cols = ar_ref.shape[2]
    zs = []
    for m in range(cols):
        a2 = jnp.concatenate([ar_ref[0, :, m, :], ai_ref[0, :, m, :]], axis=0)
        zs.append(_mm(g_ref[m], a2))
    zr = jnp.concatenate([z[:n] for z in zs], axis=0)
    zi = jnp.concatenate([z[n:] for z in zs], axis=0)
    for g0 in range(0, zr.shape[1], HEAD_DIM):
        y = _mm(jnp.concatenate([zr[:, g0:g0 + HEAD_DIM], zi[:, g0:g0 + HEAD_DIM]], axis=1), wc_ref[...])
        for m in range(cols):
            o_ref[0, :, m, g0:g0 + HEAD_DIM] = y[m * n:(m + 1) * n] * scale


def _fnet_tables(n):
    a = np.arange(n)
    ang1 = 2.0 * np.pi * np.outer(a, a) / n
    f1 = np.concatenate([np.cos(ang1), -np.sin(ang1)], axis=0)
    m = a[:, None] + n * a[None, :]
    ang2 = 2.0 * np.pi * ((m[:, :, None] * a[None, None, :]) % (n * n)) / (n * n)
    gc, gs = np.cos(ang2), np.sin(ang2)
    g2 = np.concatenate([np.concatenate([gc, gs], axis=2), np.concatenate([-gs, gc], axis=2)], axis=1)
    angc = 2.0 * np.pi * np.outer(np.arange(HEAD_DIM), np.arange(HEAD_DIM)) / HEAD_DIM
    wc = np.concatenate([np.cos(angc), np.sin(angc)], axis=0)
    f = lambda x: jnp.asarray(x, F32).astype(BF16)
    return f(f1), f(g2), f(wc)


def _fnet_call(f):
    b, l, c = f.shape
    n = GRID_W
    assert l == n * n
    f1, g2, wc = _fnet_tables(n)
    cols = 4 * SUBLANES
    ar, ai = pl.pallas_call(
        _fnet1_kernel,
        grid=(b, n // cols),
        in_specs=[pl.BlockSpec((1, n, cols, c), lambda i, j: (i, 0, j, 0)),
                  pl.BlockSpec((2 * n, n), lambda i, j: (0, 0))],
        out_specs=[pl.BlockSpec((1, cols, n, c), lambda i, j: (i, j, 0, 0))] * 2,
        out_shape=[jax.ShapeDtypeStruct((b, n, n, c), F32)] * 2,
        compiler_params=_cparams(("parallel", "parallel"), 32),
        name="fnet1",
    )(f.reshape(b, n, n, c), f1)
    out = pl.pallas_call(
        functools.partial(_fnet2_kernel, scale=1.0 / math.sqrt(l * HEAD_DIM)),
        grid=(b, n // cols),
        in_specs=[pl.BlockSpec((1, n, cols, c), lambda i, j: (i, 0, j, 0)),
                  pl.BlockSpec((1, n, cols, c), lambda i, j: (i, 0, j, 0)),
                  pl.BlockSpec((cols, 2 * n, 2 * n), lambda i, j: (j, 0, 0)),
                  pl.BlockSpec((2 * HEAD_DIM, HEAD_DIM), lambda i, j: (0, 0))],
        out_specs=pl.BlockSpec((1, n, cols, c), lambda i, j: (i, 0, j, 0)),
        out_shape=jax.ShapeDtypeStruct((b, n, n, c), F32),
        compiler_params=_cparams(("parallel", "parallel"), 32),
        name="fnet2",
    )(ar, ai, g2, wc)
    return out.reshape(b, l, c)


def _merge_kernel(of_ref, ob_ref, z_ref, fm_ref, gab_ref, x_ref, g1_ref, sh2_ref, sc2_ref, gn_ref, n2_ref,
                  wg_ref, wf_ref, wm_ref, wr_ref, br_ref, x1_ref, h2_ref, lg_ref):
    d = x_ref.shape[1]
    tm = x_ref.shape[0]
    halves = [slice(0, tm // 2), slice(tm // 2, tm)]
    yb_in = []
    for r in halves:
        o = of_ref[r, :].astype(F32) + ob_ref[r, :].astype(F32)
        z = z_ref[r, :].astype(F32)
        parts = []
        for h0 in range(0, V_DIM, HEAD_DIM):
            oh = o[:, h0:h0 + HEAD_DIM]
            parts.append(oh * lax.rsqrt(jnp.mean(oh * oh, axis=-1, keepdims=True) + EPS) * gn_ref[...])
        yb_in.append(jnp.concatenate(parts, axis=1) * (z * _sigmoid(z)))
    yb = [_mm(v, wg_ref[...]) for v in yb_in]
    ya = [_mm(fm_ref[r, :], wf_ref[...]) for r in halves]
    mixed = [_sigmoid(gab_ref[r, :d].astype(F32)) * a + _sigmoid(gab_ref[r, d:].astype(F32)) * b_
             for r, a, b_ in zip(halves, ya, yb)]
    mm = [_mm(v, wm_ref[...]) for v in mixed]
    w = wr_ref[...]
    w_hi = w.astype(BF16)
    w2 = jnp.concatenate([w_hi, (w - w_hi.astype(F32)).astype(BF16)], axis=1)
    for r, v in zip(halves, mm):
        x1 = x_ref[r, :] + g1_ref[0] * v
        x1_ref[r, :] = x1
        h2 = _rmsnorm(x1, n2_ref[...]) * (1.0 + sc2_ref[0]) + sh2_ref[0]
        h2_ref[r, :] = _pack_bf16_pairs(h2)
        h_hi = h2.astype(BF16)
        h_lo = (h2 - h_hi.astype(F32)).astype(BF16)
        part = _mm(h_hi, w2)
        lg_ref[r, :] = (part[:, :LANES] + part[:, LANES:]) + _mm(h_lo, w_hi) + br_ref[...]


def _merge_call(of, ob, z, fm, gab, x2, g1, sh2, sc2, gn, n2, wg, wf, wm, wr, br, tokens_per_batch, tm):
    n, d = x2.shape
    per = tokens_per_batch // tm
    tok = lambda wd: pl.BlockSpec((tm, wd), lambda i: (i, 0))
    vec = pl.BlockSpec((1, 1, d), lambda i: (i // per, 0, 0))
    full = lambda a: pl.BlockSpec(a.shape, lambda i: (0,) * a.ndim)
    return pl.pallas_call(
        _merge_kernel,
        grid=(n // tm,),
        in_specs=[tok(V_DIM), tok(V_DIM), tok(V_DIM), tok(F_DIM), tok(2 * d), tok(d), vec, vec, vec,
                  full(gn), full(n2), full(wg), full(wf), full(wm), full(wr), full(br)],
        out_specs=[tok(d), tok(d // 2), tok(LANES)],
        out_shape=[jax.ShapeDtypeStruct((n, d), F32), jax.ShapeDtypeStruct((n, d // 2), I32),
                   jax.ShapeDtypeStruct((n, LANES), F32)],
        compiler_params=_cparams(("parallel",), 56),
        name="merge",
    )(of, ob, z, fm, gab, x2, g1, sh2, sc2, gn, n2, wg, wf, wm, wr, br)


def _route_kernel(lg_ref, idx_ref, w_ref, rank_ref, cnt_ref, run_ref):
    i = pl.program_id(0)

    @pl.when(i == 0)
    def _():
        run_ref[...] = jnp.zeros_like(run_ref)

    l = lg_ref[...]
    tm = l.shape[0]
    lane = lax.broadcasted_iota(I32, l.shape, 1).astype(F32)
    vals, idxs = [], []
    for _ in range(TOP_K):
        m = jnp.max(l, axis=-1, keepdims=True)
        idx = jnp.min(jnp.where(l == m, lane, float(LANES)), axis=-1, keepdims=True)
        vals.append(m)
        idxs.append(idx)
        l = jnp.where(lane == idx, NEG_BIG * 2.0, l)
    es = [jnp.exp(v - vals[0]) for v in vals]
    inv = 1.0 / (es[0] + es[1] + es[2] + es[3])
    picked = jnp.zeros(l.shape, F32)
    for idx in idxs:
        picked = picked + (lane == idx).astype(F32)
    r = lax.broadcasted_iota(I32, (tm, tm), 0)
    cidx = lax.broadcasted_iota(I32, (tm, tm), 1)
    before = _mm(jnp.where(cidx < r, 1.0, 0.0), picked) + run_ref[...]
    idx_out = jnp.zeros(l.shape, F32)
    w_out = jnp.zeros(l.shape, F32)
    rank_out = jnp.zeros(l.shape, F32)
    for k in range(TOP_K):
        rk = jnp.sum(jnp.where(lane == idxs[k], before, 0.0), axis=-1, keepdims=True)
        idx_out = jnp.where(lane == k, idxs[k], idx_out)
        w_out = jnp.where(lane == k, es[k] * inv, w_out)
        rank_out = jnp.where(lane == k, rk, rank_out)
    idx_ref[...] = jnp.transpose(idx_out)[:SUBLANES].astype(I32)
    w_ref[...] = w_out
    rank_ref[...] = jnp.transpose(rank_out)[:SUBLANES].astype(I32)
    run_ref[...] = run_ref[...] + jnp.sum(picked, axis=0, keepdims=True)
    cnt_ref[...] = run_ref[...]


def _route_call(logits, tm):
    n = logits.shape[0]
    tok = pl.BlockSpec((tm, LANES), lambda i: (i, 0))
    tok_t = pl.BlockSpec((SUBLANES, tm), lambda i: (0, i))
    return pl.pallas_call(
        _route_kernel,
        grid=(n // tm,),
        in_specs=[tok],
        out_specs=[tok_t, tok, tok_t, pl.BlockSpec((1, LANES), lambda i: (0, 0))],
        out_shape=[jax.ShapeDtypeStruct((SUBLANES, n), I32), jax.ShapeDtypeStruct((n, LANES), F32),
                   jax.ShapeDtypeStruct((SUBLANES, n), I32), jax.ShapeDtypeStruct((1, LANES), F32)],
        scratch_shapes=[pltpu.VMEM((1, LANES), F32)],
        compiler_params=_cparams(("arbitrary",), 32),
        name="route",
    )(logits)


def _sc_mesh():
    return plsc.VectorSubcoreMesh(core_axis_name="c", subcore_axis_name="s",
                                  num_cores=SC_CORES, num_subcores=SC_SUBCORES)


def _sc_worker_id():
    return lax.axis_index("s") * SC_CORES + lax.axis_index("c")


def _sc_scatter_rows(x, idx, n_out):
    n, d = x.shape
    per_w = idx.shape[0] // SC_WORKERS
    nwin = per_w // SC_WINDOW
    assert per_w * SC_WORKERS == idx.shape[0] and nwin * SC_WINDOW == per_w and nwin % 2 == 0 and n % per_w == 0

    def body(x_hbm, idx_hbm, out_hbm, idx_v, rows_v, sem_r, sem_w):
        wid = _sc_worker_id()
        row0 = lax.rem(wid * per_w, n)
        pltpu.sync_copy(idx_hbm.at[wid], idx_v)

        def read(j, b):
            return pltpu.make_async_copy(x_hbm.at[pl.ds(row0 + j * SC_WINDOW, SC_WINDOW)], rows_v.at[b], sem_r.at[b])

        def write(j, b):
            return pltpu.make_async_copy(rows_v.at[b], out_hbm.at[idx_v.at[j]], sem_w.at[b])

        @pl.loop(0, nwin, step=2)
        def _(j):
            read(j, 0).start()
            read(j + 1, 1).start()
            read(j, 0).wait()
            write(j, 0).start()
            read(j + 1, 1).wait()
            write(j + 1, 1).start()
            write(j, 0).wait()
            write(j + 1, 1).wait()

    return pl.kernel(
        body, out_type=jax.ShapeDtypeStruct((n_out, d), x.dtype), mesh=_sc_mesh(),
        scratch_types=[pltpu.VMEM((nwin, SC_WINDOW), I32), pltpu.VMEM((2, SC_WINDOW, d), x.dtype),
                       pltpu.SemaphoreType.DMA((2,)), pltpu.SemaphoreType.DMA((2,))],
        name="sc_scatter_rows",
    )(x, idx.reshape(SC_WORKERS, nwin, SC_WINDOW))


def _sc_gather_rows(y, idx):
    d = y.shape[1]
    total = idx.shape[0]
    per_w = total // SC_WORKERS
    nwin = per_w // SC_WINDOW
    assert per_w * SC_WORKERS == total and nwin * SC_WINDOW == per_w and nwin % 2 == 0

    def body(y_hbm, idx_hbm, out_hbm, idx_v, rows_v, sem_r, sem_w):
        wid = _sc_worker_id()
        row0 = wid * per_w
        pltpu.sync_copy(idx_hbm.at[wid], idx_v)

        def read(j, b):
            return pltpu.make_async_copy(y_hbm.at[idx_v.at[j]], rows_v.at[b], sem_r.at[b])

        def write(j, b):
            return pltpu.make_async_copy(rows_v.at[b], out_hbm.at[pl.ds(row0 + j * SC_WINDOW, SC_WINDOW)], sem_w.at[b])

        @pl.loop(0, nwin, step=2)
        def _(j):
            read(j, 0).start()
            read(j + 1, 1).start()
            read(j, 0).wait()
            write(j, 0).start()
            read(j + 1, 1).wait()
            write(j + 1, 1).start()
            write(j, 0).wait()
            write(j + 1, 1).wait()

    return pl.kernel(
        body, out_type=jax.ShapeDtypeStruct((total, d), y.dtype), mesh=_sc_mesh(),
        scratch_types=[pltpu.VMEM((nwin, SC_WINDOW), I32), pltpu.VMEM((2, SC_WINDOW, d), y.dtype),
                       pltpu.SemaphoreType.DMA((2,)), pltpu.SemaphoreType.DMA((2,))],
        name="sc_gather_rows",
    )(y, idx.reshape(SC_WORKERS, nwin, SC_WINDOW))


def _expert_kernel(be_ref, nv_ref, nxt_ref, par_ref, used_ref, x_ref, wg_hbm, wu_hbm, wd_hbm, bg_ref, bu_ref, bd_ref,
                   y_ref, wgf, wuf, wdf, wgb, wub, wdb, sem):
    i = pl.program_id(0)

    def fetch(e, s):
        return [pltpu.make_async_copy(w_hbm.at[e], w_f.at[s], sem.at[s])
                for w_hbm, w_f in ((wg_hbm, wgf), (wu_hbm, wuf), (wd_hbm, wdf))]

    @pl.when(i < nv_ref[0])
    def _():
        e = be_ref[i]
        s = par_ref[e]

        @pl.when(i == 0)
        def _():
            for cp in fetch(e, s):
                cp.start()

        first = jnp.logical_or(i == 0, be_ref[jnp.maximum(i - 1, 0)] != e)
        quarter = x_ref.shape[0] // 4
        full = used_ref[i] > 3 * quarter

        @pl.when(first)
        def _():
            for cp in fetch(e, s):
                cp.wait()

            @pl.when(nxt_ref[e] >= 0)
            def _():
                for cp in fetch(nxt_ref[e], 1 - s):
                    cp.start()

        def ffn(nrows, cast):
            x = _unpack_bf16_pairs(x_ref[:nrows, :]).astype(BF16)
            if cast:
                wgb[...] = wgf[s].astype(BF16)
            gate = jnp.minimum(_mm(x, wgb[...]) + bg_ref[0], SWIGLU_LIMIT)
            if cast:
                wub[...] = wuf[s].astype(BF16)
            up = jnp.clip(_mm(x, wub[...]) + bu_ref[0], -SWIGLU_LIMIT, SWIGLU_LIMIT)
            if cast:
                wdb[...] = wdf[s].astype(BF16)
            act = (up + 1.0) * gate * _sigmoid(SWIGLU_ALPHA * gate)
            y_ref[:nrows, :] = _pack_bf16_pairs(_mm(act, wdb[...]) + bd_ref[0])

        @pl.when(jnp.logical_and(first, full))
        def _():
            ffn(4 * quarter, True)

        @pl.when(jnp.logical_and(first, jnp.logical_not(full)))
        def _():
            wgb[...] = wgf[s].astype(BF16)
            wub[...] = wuf[s].astype(BF16)
            wdb[...] = wdf[s].astype(BF16)

        for nq in range(1, 5):
            in_range = jnp.logical_and(used_ref[i] > (nq - 1) * quarter, used_ref[i] <= nq * quarter)
            if nq == 4:
                in_range = jnp.logical_and(in_range, jnp.logical_not(first))

            @pl.when(in_range)
            def _(nq=nq):
                ffn(nq * quarter, False)


def _expert_call(block_e, n_valid, next_e, parity, used, xs, w_gate, w_up, w_down, b_gate, b_up, b_down):
    nb = block_e.shape[0]
    tmb = EXPERT_BLOCK
    ne, d, de = w_gate.shape
    bspec = lambda s: pl.BlockSpec((1,) + s, lambda i, be, nv, nx, pa, us: (be[i], 0, 0))
    anyspec = pl.BlockSpec(memory_space=pl.ANY)
    grid_spec = pltpu.PrefetchScalarGridSpec(
        num_scalar_prefetch=5,
        grid=(nb,),
        in_specs=[pl.BlockSpec((tmb, d // 2), lambda i, be, nv, nx, pa, us: (i, 0)),
                  anyspec, anyspec, anyspec, bspec((1, de)), bspec((1, de)), bspec((1, d))],
        out_specs=pl.BlockSpec((tmb, d // 2), lambda i, be, nv, nx, pa, us: (i, 0)),
        scratch_shapes=[pltpu.VMEM((2, d, de), F32), pltpu.VMEM((2, d, de), F32), pltpu.VMEM((2, de, d), F32),
                        pltpu.VMEM((d, de), BF16), pltpu.VMEM((d, de), BF16), pltpu.VMEM((de, d), BF16),
                        pltpu.SemaphoreType.DMA((2,))],
    )
    return pl.pallas_call(
        _expert_kernel,
        grid_spec=grid_spec,
        out_shape=jax.ShapeDtypeStruct(xs.shape, xs.dtype),
        compiler_params=_cparams(("arbitrary",), 56),
        name="expert",
    )(block_e, n_valid, next_e, parity, used, xs, w_gate, w_up, w_down,
      b_gate.reshape(ne, 1, de), b_up.reshape(ne, 1, de), b_down.reshape(ne, 1, d))


def _combine_kernel(y0, y1, y2, y3, w_ref, x1_ref, g2_ref, fg_ref, o_ref):
    w = w_ref[...]
    ys = [_unpack_bf16_pairs(y[...]) for y in (y0, y1, y2, y3)]
    moe = (w[:, 0:1] * ys[0] + w[:, 1:2] * ys[1]) + (w[:, 2:3] * ys[2] + w[:, 3:4] * ys[3])
    o_ref[...] = _rmsnorm(x1_ref[...] + g2_ref[0] * moe, fg_ref[...])


def _combine_call(y4, top_w, x1, g2, fg, tokens_per_batch, tm):
    n, d = x1.shape
    per = tokens_per_batch // tm
    nt = n // tm
    yspec = lambda k: pl.BlockSpec((tm, d // 2), lambda i: (k * nt + i, 0))
    return pl.pallas_call(
        _combine_kernel,
        grid=(nt,),
        in_specs=[yspec(0), yspec(1), yspec(2), yspec(3),
                  pl.BlockSpec((tm, LANES), lambda i: (i, 0)),
                  pl.BlockSpec((tm, d), lambda i: (i, 0)),
                  pl.BlockSpec((1, 1, d), lambda i: (i // per, 0, 0)),
                  pl.BlockSpec((1, d), lambda i: (0, 0))],
        out_specs=pl.BlockSpec((tm, d), lambda i: (i, 0)),
        out_shape=jax.ShapeDtypeStruct((n, d), F32),
        compiler_params=_cparams(("parallel",), 48),
        name="combine",
    )(y4, y4, y4, y4, top_w, x1, g2, fg)


def _slot_kernel(start_ref, idx_ref, rank_ref, o_ref):
    idx = idx_ref[...]
    acc = rank_ref[...]
    for e in range(N_EXPERTS):
        acc = acc + jnp.where(idx == e, start_ref[e], 0)
    o_ref[...] = acc


def _slot_call(pad_start, top_idx, rank):
    full = pl.BlockSpec(top_idx.shape, lambda i, s: (0, 0))
    return pl.pallas_call(
        _slot_kernel,
        grid_spec=pltpu.PrefetchScalarGridSpec(num_scalar_prefetch=1, grid=(1,), in_specs=[full, full],
                                               out_specs=full),
        out_shape=jax.ShapeDtypeStruct(top_idx.shape, I32),
        compiler_params=_cparams(("arbitrary",), 32),
        name="slots",
    )(pad_start, top_idx, rank)


def _routing_tables(top_idx, rank, counts, n):
    tmb = EXPERT_BLOCK
    counts = counts.astype(I32)
    padded = (counts + tmb - 1) // tmb * tmb
    pad_end = jnp.cumsum(padded)
    pad_start = pad_end - padded
    n_blocks = -(-(n * TOP_K + N_EXPERTS * (tmb - 1)) // tmb)
    n_slots = n_blocks * tmb
    dest_flat = _slot_call(pad_start, top_idx, rank)[:TOP_K].reshape(-1)
    block_start = jnp.arange(n_blocks, dtype=I32) * tmb
    block_e = jnp.minimum(jnp.sum((pad_end[None, :] <= block_start[:, None]).astype(I32), axis=1), N_EXPERTS - 1)
    n_valid = (pad_end[-1:] // tmb).astype(I32)
    experts = jnp.arange(N_EXPERTS, dtype=I32)
    used = counts > 0
    later = jnp.where(jnp.logical_and(used[None, :], experts[None, :] > experts[:, None]), experts[None, :], N_EXPERTS)
    next_e = jnp.min(later, axis=1)
    next_e = jnp.where(next_e == N_EXPERTS, -1, next_e).astype(I32)
    parity = ((jnp.cumsum(used.astype(I32)) - used.astype(I32)) % 2).astype(I32)
    row_end = jnp.sum(jnp.where(block_e[:, None] == experts[None, :], (pad_start + counts)[None, :], 0), axis=1)
    rows_used = jnp.clip(row_end - block_start, 0, tmb).astype(I32)
    return block_e, n_valid, next_e, parity, rows_used, dest_flat, n_slots


def _gdn_branch(x3, sh, sc, norm_g, w_qkv, w_gates, w_extra, extra_dtypes, conv_w, par, grid_w, use_rows, tm, s0):
    q, k, v, go, rows, *extra = _inconv_call(x3, sh, sc, norm_g, w_qkv, w_gates, w_extra, extra_dtypes, conv_w, par,
                                             grid_w, use_rows, tm)
    o_f, o_b, s_fin = _gdn_call(q, k, v, go, rows, s0)
    return extra, o_f, o_b, s_fin


def kernel(x, c, ctx, c_ctx, w_mod, b_mod, norm1_g, norm2_g, w_in, conv_w, a_log, dt_bias, gdn_norm_g,
           w_fourier_out, w_gdn_out, w_merge_out, w_router, b_router, w_gate, b_gate, w_up, b_up,
           w_down, b_down, final_norm_g):
    b, l, d = x.shape
    n = b * l
    n_ctx = ctx.shape[1]
    assert w_mod.shape[0] == 1 and l == GRID_W * GRID_W and d == V_DIM

    c8 = jnp.concatenate([c, c_ctx[None, :], jnp.zeros((8 - b - 1, d), F32)], axis=0)
    mod = _mod_call(c8, w_mod[0], b_mod[0])
    sh1, sc1, g1, sh2, sc2, g2 = [mod[:b, j * d:(j + 1) * d].reshape(b, 1, d) for j in range(6)]
    csh1 = jnp.broadcast_to(mod[b:b + 1, 0:d].reshape(1, 1, d), (b, 1, d))
    csc1 = jnp.broadcast_to(mod[b:b + 1, d:2 * d].reshape(1, 1, d), (b, 1, d))

    off_gate = QKV_DIM
    off_z = off_gate + 4 * NV_HEADS
    off_f = off_z + V_DIM
    off_ga = off_f + F_DIM
    w_qkv, w_gates, *w_extra = _wsplit_call(w_in, (0, off_gate, off_z, off_f, off_ga, w_in.shape[2]))
    par = jnp.pad(jnp.stack([a_log[0].reshape(-1), dt_bias[0].reshape(-1)]),
                  ((0, 6), (GATE_LANE0, LANES - 2 * GATE_LANE0)))
    n1 = norm1_g[0].reshape(1, d)
    cw = conv_w[0].reshape(9, QKV_DIM)

    zero_state = jnp.zeros((b, 2, NV_HEADS, HEAD_DIM, HEAD_DIM), F32)
    _, _, _, s_ctx = _gdn_branch(ctx, csh1, csc1, n1, w_qkv, w_gates, [], [], cw, par, n_ctx, False, n_ctx,
                                 zero_state)

    x2 = x.reshape(n, d)
    (z, f, gab), o_f, o_b, _ = _gdn_branch(x, sh1, sc1, n1, w_qkv, w_gates, w_extra, (BF16, F32, BF16),
                                           cw, par, GRID_W, True, TOKEN_TILE, s_ctx)
    z, gab = z.reshape(n, V_DIM), gab.reshape(n, 2 * d)
    fmix = _fnet_call(f).reshape(n, F_DIM)

    wr = jnp.pad(w_router[0], ((0, 0), (0, LANES - N_EXPERTS)))
    br = jnp.pad(b_router[0], (0, LANES - N_EXPERTS), constant_values=NEG_BIG).reshape(1, LANES)
    x1, h2, logits = _merge_call(
        o_f.reshape(n, V_DIM), o_b.reshape(n, V_DIM), z, fmix, gab, x2, g1, sh2, sc2,
        gdn_norm_g[0].reshape(1, HEAD_DIM), norm2_g[0].reshape(1, d),
        w_gdn_out[0].astype(BF16), w_fourier_out[0].astype(BF16), w_merge_out[0].astype(BF16), wr, br, l, TOKEN_TILE)

    top_idx, top_w, rank, counts = _route_call(logits, 2 * TOKEN_TILE)
    block_e, n_valid, next_e, parity, rows_used, dest_flat, n_slots = _routing_tables(
        top_idx, rank, counts[0, :N_EXPERTS], n)
    xs = _sc_scatter_rows(h2, dest_flat, n_slots)
    ys = _expert_call(block_e, n_valid, next_e, parity, rows_used, xs, w_gate[0], w_up[0], w_down[0],
                      b_gate[0], b_up[0], b_down[0])
    y4 = _sc_gather_rows(ys, dest_flat)
    out = _combine_call(y4, top_w, x1, g2, final_norm_g.reshape(1, d), l, 2 * TOKEN_TILE)
    return out.reshape(b, l, d)
```

```python
import functools
import math

import jax
import jax.numpy as jnp
import numpy as np
from jax import lax
from jax.experimental import pallas as pl
from jax.experimental.pallas import tpu as pltpu
from jax.experimental.pallas import tpu_sc as plsc

F32 = jnp.float32
BF16 = jnp.bfloat16
I32 = jnp.int32
HIGHEST = lax.Precision.HIGHEST

GRID_W = 64
NQK_HEADS = 4
NV_HEADS = 8
HEAD_DIM = 128
QK_DIM = NQK_HEADS * HEAD_DIM
V_DIM = NV_HEADS * HEAD_DIM
QKV_DIM = 2 * QK_DIM + V_DIM
F_GROUPS = 4
F_DIM = F_GROUPS * HEAD_DIM
N_EXPERTS = 32
TOP_K = 4
SWIGLU_ALPHA = 1.702
SWIGLU_LIMIT = 7.0
EPS = 1e-6

LANES = 128
SUBLANES = 8
GATE_LANE0 = 16
GDN_CHUNK = 128
EXPERT_BLOCK = 1024
TOKEN_TILE = 512
SC_CORES = 2
SC_SUBCORES = 16
SC_WORKERS = SC_CORES * SC_SUBCORES
SC_WINDOW = 64
NEG_BIG = -1e30
MIB = 2 ** 20


def _cparams(sem, vmem_mib):
    return pltpu.CompilerParams(dimension_semantics=sem, vmem_limit_bytes=vmem_mib * MIB)


def _mm(a, b, prec=None, dims=(((1,), (0,)), ((), ()))):
    if prec is None:
        return lax.dot_general(a.astype(BF16), b.astype(BF16), dims, preferred_element_type=F32)
    return lax.dot_general(a.astype(F32), b.astype(F32), dims, precision=prec, preferred_element_type=F32)


def _sigmoid(x):
    return 0.5 * jnp.tanh(0.5 * x) + 0.5


def _rmsnorm(x, g):
    return x * lax.rsqrt(jnp.mean(x * x, axis=-1, keepdims=True) + EPS) * g


def _pack_bf16_pairs(x):
    half = x.shape[1] // 2
    bits = lax.bitcast_convert_type(x.astype(BF16).astype(F32), jnp.uint32)
    packed = jnp.bitwise_or(jnp.right_shift(bits[:, :half], jnp.uint32(16)),
                            jnp.bitwise_and(bits[:, half:], jnp.uint32(0xFFFF0000)))
    return lax.bitcast_convert_type(packed, I32)


def _unpack_bf16_pairs(p):
    bits = lax.bitcast_convert_type(p, jnp.uint32)
    lo = lax.bitcast_convert_type(jnp.left_shift(bits, jnp.uint32(16)), F32)
    hi = lax.bitcast_convert_type(jnp.bitwise_and(bits, jnp.uint32(0xFFFF0000)), F32)
    return jnp.concatenate([lo, hi], axis=1)


def _mod_kernel(c_ref, w_ref, b_ref, o_ref):
    c = c_ref[...]
    o_ref[...] = _mm(c * _sigmoid(c), w_ref[...], HIGHEST) + b_ref[...]


def _mod_call(c8, w_mod, b_mod):
    d, n = w_mod.shape
    tn = 1536
    return pl.pallas_call(
        _mod_kernel,
        grid=(n // tn,),
        in_specs=[pl.BlockSpec((8, d), lambda j: (0, 0)),
                  pl.BlockSpec((d, tn), lambda j: (0, j)),
                  pl.BlockSpec((1, tn), lambda j: (0, j))],
        out_specs=pl.BlockSpec((8, tn), lambda j: (0, j)),
        out_shape=jax.ShapeDtypeStruct((8, n), F32),
        compiler_params=_cparams(("parallel",), 32),
        name="mod",
    )(c8, w_mod, b_mod.reshape(1, n))


def _wsplit_kernel(w_ref, *o_refs, bounds):
    for o_ref, lo, hi in zip(o_refs, bounds[:-1], bounds[1:]):
        o_ref[:, :hi - lo] = w_ref[0, :, lo:hi].astype(o_ref.dtype)
        if o_ref.shape[1] > hi - lo:
            o_ref[:, hi - lo:] = jnp.zeros((o_ref.shape[0], o_ref.shape[1] - (hi - lo)), o_ref.dtype)


def _wsplit_call(w, bounds):
    d = w.shape[1]
    tr = 256
    widths = [-(-(hi - lo) // LANES) * LANES for lo, hi in zip(bounds[:-1], bounds[1:])]
    return pl.pallas_call(
        functools.partial(_wsplit_kernel, bounds=bounds),
        grid=(d // tr,),
        in_specs=[pl.BlockSpec((1, tr, w.shape[2]), lambda i: (0, i, 0))],
        out_specs=[pl.BlockSpec((tr, wd), lambda i: (i, 0)) for wd in widths],
        out_shape=[jax.ShapeDtypeStruct((d, wd), BF16) for wd in widths],
        compiler_params=_cparams(("parallel",), 32),
        name="wsplit",
    )(w)


def _inconv_kernel(*refs, grid_w, use_rows, tm, cw, n_extra):
    refs = list(refs)
    prev_ref = refs.pop(0) if use_rows else None
    x_ref = refs.pop(0)
    next_ref = refs.pop(0) if use_rows else None
    sh_ref, sc_ref, g_ref, wq_ref, wgate_ref = refs[:5]
    wx_refs = refs[5:5 + n_extra]
    w_ref, par_ref, q_ref, k_ref, v_ref, go_ref, gr_ref = refs[5 + n_extra:12 + n_extra]
    ox_refs = refs[12 + n_extra:]

    def modulated(x):
        return (_rmsnorm(x, g_ref[...]) * (1.0 + sc_ref[0]) + sh_ref[0]).astype(BF16)

    r = pl.program_id(1)
    nr = pl.num_programs(1)
    u = modulated(x_ref[0])

    def project(wx_ref, o_ref, c0, step):
        o_ref[0, :, c0:c0 + step] = _mm(u, wx_ref[:, c0:c0 + step]).astype(o_ref.dtype)

    plain = [functools.partial(project, wx_ref, o_ref, c0, min(wx_ref.shape[1], 512))
             for wx_ref, o_ref in zip(wx_refs, ox_refs) for c0 in range(0, wx_ref.shape[1], min(wx_ref.shape[1], 512))]
    n_conv_chunks = QKV_DIM // cw

    t = lax.broadcasted_iota(I32, (tm, 1), 0)
    col = jnp.bitwise_and(t, grid_w - 1)
    m_left = (col != 0).astype(F32)
    m_right = (col != grid_w - 1).astype(F32)
    if use_rows:
        u_prev = modulated(prev_ref[0])
        u_next = modulated(next_ref[0])
        has_prev = (r > 0).astype(F32)
        has_next = (r < nr - 1).astype(F32)
    for c0 in range(0, QKV_DIM, cw):
        wq = wq_ref[:, c0:c0 + cw]
        xm = _mm(u, wq)
        if use_rows:
            up = jnp.concatenate([_mm(u_prev, wq) * has_prev, xm[:tm - grid_w]], axis=0)
            dn = jnp.concatenate([xm[grid_w:], _mm(u_next, wq) * has_next], axis=0)

        def colsum(kc):
            y = xm * w_ref[3 + kc:4 + kc, c0:c0 + cw]
            if use_rows:
                y = y + up * w_ref[kc:kc + 1, c0:c0 + cw] + dn * w_ref[6 + kc:7 + kc, c0:c0 + cw]
            return y

        acc = (colsum(1) + pltpu.roll(colsum(0), 1, axis=0) * m_left
               + pltpu.roll(colsum(2), tm - 1, axis=0) * m_right)
        s = acc * _sigmoid(acc)
        for h0 in range(0, cw, HEAD_DIM):
            c = c0 + h0
            seg = s[:, h0:h0 + HEAD_DIM]
            if c < 2 * QK_DIM:
                seg = seg * lax.rsqrt(jnp.sum(seg * seg, axis=-1, keepdims=True) + EPS)
            if c < QK_DIM:
                q_ref[0, :, c:c + HEAD_DIM] = (seg * HEAD_DIM ** -0.5).astype(q_ref.dtype)
            elif c < 2 * QK_DIM:
                k_ref[0, :, c - QK_DIM:c - QK_DIM + HEAD_DIM] = seg.astype(k_ref.dtype)
            else:
                v_ref[0, :, c - 2 * QK_DIM:c - 2 * QK_DIM + HEAD_DIM] = seg.astype(v_ref.dtype)
        ci = c0 // cw
        for job in plain[ci * len(plain) // n_conv_chunks:(ci + 1) * len(plain) // n_conv_chunks]:
            job()
    g = _mm(u, wgate_ref[...])
    a = g + par_ref[1:2, :]
    softplus = jnp.maximum(a, 0.0) + jnp.log1p(jnp.exp(-jnp.abs(a)))
    log_g = -jnp.exp(par_ref[0:1, :]) * softplus
    lane = lax.broadcasted_iota(I32, g.shape, 1)
    go_ref[0] = jnp.where(lane < GATE_LANE0, _sigmoid(g), log_g)
    gr_ref[0] = jnp.transpose(log_g)[GATE_LANE0:GATE_LANE0 + 2 * NV_HEADS]


def _inconv_call(x3, sh, sc, g, w_qkv, w_gates, w_extra, extra_dtypes, conv_w, par, grid_w, use_rows, tm):
    b, t, d = x3.shape
    kern = functools.partial(_inconv_kernel, grid_w=grid_w, use_rows=use_rows, tm=tm, cw=512, n_extra=len(w_extra))
    per = tm // grid_w
    nrow = t // grid_w
    tile = lambda wd: pl.BlockSpec((1, tm, wd), lambda i, r: (i, r, 0))
    vec = pl.BlockSpec((1, 1, d), lambda i, r: (i, 0, 0))
    const = lambda a: pl.BlockSpec(a.shape, lambda i, r: (0,) * a.ndim, pipeline_mode=pl.Buffered(1))
    in_specs, args = [], []
    if use_rows:
        in_specs.append(pl.BlockSpec((1, grid_w, d), lambda i, r: (i, jnp.maximum(r * per - 1, 0), 0)))
        args.append(x3)
    in_specs.append(tile(d))
    args.append(x3)
    if use_rows:
        in_specs.append(pl.BlockSpec((1, grid_w, d), lambda i, r: (i, jnp.minimum((r + 1) * per, nrow - 1), 0)))
        args.append(x3)
    consts = [g, w_qkv, w_gates, *w_extra, conv_w, par]
    in_specs += [vec, vec] + [const(a) for a in consts]
    args += [sh, sc] + consts
    widths = [QK_DIM, QK_DIM, V_DIM, LANES] + [w.shape[1] for w in w_extra]
    dtypes = [BF16, BF16, BF16, F32] + list(extra_dtypes)
    out_shape = [jax.ShapeDtypeStruct((b, t, wd), dt) for wd, dt in zip(widths, dtypes)]
    out_specs = [tile(wd) for wd in widths]
    out_shape.insert(4, jax.ShapeDtypeStruct((b, 2 * NV_HEADS, t), F32))
    out_specs.insert(4, pl.BlockSpec((1, 2 * NV_HEADS, tm), lambda i, r: (i, 0, r)))
    return pl.pallas_call(
        kern, grid=(b, t // tm), in_specs=in_specs, out_specs=out_specs, out_shape=out_shape,
        compiler_params=_cparams(("parallel", "parallel"), 56),
        name="inconv_rows" if use_rows else "inconv_seq",
    )(*args)


def _gdn_kernel(qf, kf, vf, gf, rf, qb, kb, vb, gb, rb, s0_ref, of, ob, sfin_ref, s_ref):
    i = pl.program_id(1)
    nc = pl.num_programs(1)

    @pl.when(i == 0)
    def _():
        s_ref[...] = s0_ref[0]

    c = qf.shape[1]
    per = NV_HEADS // NQK_HEADS
    row = lax.broadcasted_iota(I32, (c, c), 0)
    colj = lax.broadcasted_iota(I32, (c, c), 1)
    eye = jnp.where(row == colj, 1.0, 0.0)
    nt_dims = (((1,), (1,)), ((), ()))
    tn_dims = (((0,), (0,)), ((), ()))

    seqs = []
    for d, (q_r, k_r, v_r, g_r, r_r, o_r) in enumerate(((qf, kf, vf, gf, rf, of), (qb, kb, vb, gb, rb, ob))):
        rev = d == 1
        incl = (colj >= row) if rev else (colj <= row)
        strict = (colj > row) if rev else (colj < row)
        gates = g_r[0]
        tri_c = jnp.where(incl, 1.0, 0.0)
        gcm = _mm(tri_c, gates, HIGHEST)
        gcr = _mm(r_r[0], tri_c, HIGHEST, dims=nt_dims)
        for hq in range(NQK_HEADS):
            q = q_r[0, :, hq * HEAD_DIM:(hq + 1) * HEAD_DIM]
            k = k_r[0, :, hq * HEAD_DIM:(hq + 1) * HEAD_DIM]
            kq = lax.dot_general(jnp.concatenate([q, k], axis=0), k, nt_dims, preferred_element_type=F32)
            for j in range(per):
                h = hq * per + j
                idx = d * NV_HEADS + h
                gc_c = gcm[:, GATE_LANE0 + idx:GATE_LANE0 + idx + 1]
                seqs.append(dict(d=d, h=h, o_r=o_r, v_r=v_r, q=q, k=k, qk=kq[:c], kk=kq[c:], incl=incl, strict=strict,
                                 beta=gates[:, idx:idx + 1], gc_c=gc_c, gc_r=gcr[idx:idx + 1, :],
                                 ge=gc_c[0:1] if rev else gc_c[c - 1:c]))

    def same_block(m):
        sh = int(math.log2(m))
        return jnp.right_shift(row, sh) == jnp.right_shift(colj, sh)

    for s in seqs:
        s['decay'] = jnp.where(s['incl'], jnp.exp(jnp.where(s['incl'], s['gc_c'] - s['gc_r'], 0.0)), 0.0)
        s['a'] = jnp.where(s['strict'], s['beta'] * s['kk'] * s['decay'], 0.0)
        s['t'] = eye - jnp.where(same_block(2), s['a'], 0.0)
    m = 4
    while m <= c:
        between = jnp.logical_and(same_block(m), jnp.logical_not(same_block(m // 2)))
        for s in seqs:
            s['te'] = _mm(s['t'], jnp.where(between, s['a'], 0.0))
        for s in seqs:
            s['t'] = s['t'] - _mm(s['te'], s['t'])
        m *= 2
    for s in seqs:
        h = s['h']
        egc = jnp.exp(s['gc_c'])
        kf32 = s['k'].astype(F32)
        v = s['v_r'][0, :, h * HEAD_DIM:(h + 1) * HEAD_DIM].astype(F32)
        rhs = jnp.concatenate([s['beta'] * v, (s['beta'] * egc) * kf32], axis=1)
        s['sol'] = _mm(s['t'], rhs)
        s['q_dec'] = s['q'].astype(F32) * egc
        s['k_dec'] = kf32 * jnp.exp(s['ge'] - s['gc_c'])
    for s in seqs:
        s['ws'] = _mm(jnp.concatenate([s['sol'][:, HEAD_DIM:], s['q_dec']], axis=0), s_ref[s['d'], s['h']])
    for s in seqs:
        s['u'] = s['sol'][:, :HEAD_DIM] - s['ws'][:c]
        s_ref[s['d'], s['h']] = (jnp.exp(s['ge']) * s_ref[s['d'], s['h']]
                                 + _mm(s['k_dec'], s['u'], dims=tn_dims))
    for s in seqs:
        h = s['h']
        o = s['ws'][c:] + _mm(s['qk'] * s['decay'], s['u'])
        s['o_r'][0, :, h * HEAD_DIM:(h + 1) * HEAD_DIM] = o.astype(s['o_r'].dtype)

    @pl.when(i == nc - 1)
    def _():
        sfin_ref[0] = s_ref[...]


def _gdn_call(q, k, v, go, rows, s0):
    b, t, _ = q.shape
    c = GDN_CHUNK
    nc = t // c
    fwd = lambda i, n: (i, n, 0)
    bwd = lambda i, n: (i, nc - 1 - n, 0)
    rfwd = lambda i, n: (i, 0, n)
    rbwd = lambda i, n: (i, 0, nc - 1 - n)
    state_spec = pl.BlockSpec((1,) + s0.shape[1:], lambda i, n: (i, 0, 0, 0, 0))

    def specs(m3, mr):
        return [pl.BlockSpec((1, c, QK_DIM), m3), pl.BlockSpec((1, c, QK_DIM), m3),
                pl.BlockSpec((1, c, V_DIM), m3), pl.BlockSpec((1, c, LANES), m3),
                pl.BlockSpec((1, 2 * NV_HEADS, c), mr)]

    return pl.pallas_call(
        _gdn_kernel,
        grid=(b, nc),
        in_specs=specs(fwd, rfwd) + specs(bwd, rbwd) + [state_spec],
        out_specs=[pl.BlockSpec((1, c, V_DIM), fwd), pl.BlockSpec((1, c, V_DIM), bwd), state_spec],
        out_shape=[jax.ShapeDtypeStruct((b, t, V_DIM), BF16), jax.ShapeDtypeStruct((b, t, V_DIM), BF16),
                   jax.ShapeDtypeStruct(s0.shape, F32)],
        scratch_shapes=[pltpu.VMEM(s0.shape[1:], F32)],
        compiler_params=_cparams(("parallel", "arbitrary"), 48),
        name="gdn",
    )(q, k, v, go, rows, q, k, v, go, rows, s0)


def _fnet1_kernel(x_ref, f_ref, ar_ref, ai_ref):
    n = x_ref.shape[1]
    for j in range(x_ref.shape[2]):
        a = _mm(f_ref[...], x_ref[0, :, j, :])
        ar_ref[0, j] = a[:n]
        ai_ref[0, j] = a[n:]


def _fnet2_kernel(ar_ref, ai_ref, g_ref, wc_ref, o_ref, *, scale):
    n = ar_ref.shape[1]
    cols = ar_ref.shape[2]
    zs = []
    for m in range(cols):
        a2 = jnp.concatenate([ar_ref[0, :, m, :], ai_ref[0, :, m, :]], axis=0)
        zs.append(_mm(g_ref[m], a2))
    zr = jnp.concatenate([z[:n] for z in zs], axis=0)
    zi = jnp.concatenate([z[n:] for z in zs], axis=0)
    for g0 in range(0, zr.shape[1], HEAD_DIM):
        y = _mm(jnp.concatenate([zr[:, g0:g0 + HEAD_DIM], zi[:, g0:g0 + HEAD_DIM]], axis=1), wc_ref[...])
        for m in range(cols):
            o_ref[0, :, m, g0:g0 + HEAD_DIM] = y[m * n:(m + 1) * n] * scale


def _fnet_tables(n):
    a = np.arange(n)
    ang1 = 2.0 * np.pi * np.outer(a, a) / n
    f1 = np.concatenate([np.cos(ang1), -np.sin(ang1)], axis=0)
    m = a[:, None] + n * a[None, :]
    ang2 = 2.0 * np.pi * ((m[:, :, None] * a[None, None, :]) % (n * n)) / (n * n)
    gc, gs = np.cos(ang2), np.sin(ang2)
    g2 = np.concatenate([np.concatenate([gc, gs], axis=2), np.concatenate([-gs, gc], axis=2)], axis=1)
    angc = 2.0 * np.pi * np.outer(np.arange(HEAD_DIM), np.arange(HEAD_DIM)) / HEAD_DIM
    wc = np.concatenate([np.cos(angc), np.sin(angc)], axis=0)
    f = lambda x: jnp.asarray(x, F32).astype(BF16)
    return f(f1), f(g2), f(wc)


def _fnet_call(f):
    b, l, c = f.shape
    n = GRID_W
    assert l == n * n
    f1, g2, wc = _fnet_tables(n)
    cols = 4 * SUBLANES
    ar, ai = pl.pallas_call(
        _fnet1_kernel,
        grid=(b, n // cols),
        in_specs=[pl.BlockSpec((1, n, cols, c), lambda i, j: (i, 0, j, 0)),
                  pl.BlockSpec((2 * n, n), lambda i, j: (0, 0))],
        out_specs=[pl.BlockSpec((1, cols, n, c), lambda i, j: (i, j, 0, 0))] * 2,
        out_shape=[jax.ShapeDtypeStruct((b, n, n, c), F32)] * 2,
        compiler_params=_cparams(("parallel", "parallel"), 32),
        name="fnet1",
    )(f.reshape(b, n, n, c), f1)
    out = pl.pallas_call(
        functools.partial(_fnet2_kernel, scale=1.0 / math.sqrt(l * HEAD_DIM)),
        grid=(b, n // cols),
        in_specs=[pl.BlockSpec((1, n, cols, c), lambda i, j: (i, 0, j, 0)),
                  pl.BlockSpec((1, n, cols, c), lambda i, j: (i, 0, j, 0)),
                  pl.BlockSpec((cols, 2 * n, 2 * n), lambda i, j: (j, 0, 0)),
                  pl.BlockSpec((2 * HEAD_DIM, HEAD_DIM), lambda i, j: (0, 0))],
        out_specs=pl.BlockSpec((1, n, cols, c), lambda i, j: (i, 0, j, 0)),
        out_shape=jax.ShapeDtypeStruct((b, n, n, c), F32),
        compiler_params=_cparams(("parallel", "parallel"), 32),
        name="fnet2",
    )(ar, ai, g2, wc)
    return out.reshape(b, l, c)


def _merge_kernel(of_ref, ob_ref, z_ref, fm_ref, gab_ref, x_ref, g1_ref, sh2_ref, sc2_ref, gn_ref, n2_ref,
                  wg_ref, wf_ref, wm_ref, wr_ref, br_ref, x1_ref, h2_ref, idx_ref, w_ref, rank_ref, cnt_ref, run_ref):
    d = x_ref.shape[1]
    tm = x_ref.shape[0]
    halves = [slice(0, tm // 2), slice(tm // 2, tm)]
    yb_in = []
    for r in halves:
        o = of_ref[r, :].astype(F32) + ob_ref[r, :].astype(F32)
        z = z_ref[r, :].astype(F32)
        parts = []
        for h0 in range(0, V_DIM, HEAD_DIM):
            oh = o[:, h0:h0 + HEAD_DIM]
            parts.append(oh * lax.rsqrt(jnp.mean(oh * oh, axis=-1, keepdims=True) + EPS) * gn_ref[...])
        yb_in.append(jnp.concatenate(parts, axis=1) * (z * _sigmoid(z)))
    yb = [_mm(v, wg_ref[...]) for v in yb_in]
    ya = [_mm(fm_ref[r, :], wf_ref[...]) for r in halves]
    mixed = [_sigmoid(gab_ref[r, :d].astype(F32)) * a + _sigmoid(gab_ref[r, d:].astype(F32)) * b_
             for r, a, b_ in zip(halves, ya, yb)]
    mm = [_mm(v, wm_ref[...]) for v in mixed]
    w = wr_ref[...]
    w_hi = w.astype(BF16)
    w2 = jnp.concatenate([w_hi, (w - w_hi.astype(F32)).astype(BF16)], axis=1)
    logits = []
    for r, v in zip(halves, mm):
        x1 = x_ref[r, :] + g1_ref[0] * v
        x1_ref[r, :] = x1
        h2 = _rmsnorm(x1, n2_ref[...]) * (1.0 + sc2_ref[0]) + sh2_ref[0]
        h2_ref[r, :] = _pack_bf16_pairs(h2)
        h_hi = h2.astype(BF16)
        h_lo = (h2 - h_hi.astype(F32)).astype(BF16)
        part = _mm(h_hi, w2)
        logits.append((part[:, :LANES] + part[:, LANES:]) + _mm(h_lo, w_hi) + br_ref[...])
    _route_tile(jnp.concatenate(logits, axis=0), idx_ref, w_ref, rank_ref, cnt_ref, run_ref)


def _merge_call(of, ob, z, fm, gab, x2, g1, sh2, sc2, gn, n2, wg, wf, wm, wr, br, tokens_per_batch, tm):
    n, d = x2.shape
    per = tokens_per_batch // tm
    tok = lambda wd: pl.BlockSpec((tm, wd), lambda i: (i, 0))
    tok_t = pl.BlockSpec((SUBLANES, tm), lambda i: (0, i))
    vec = pl.BlockSpec((1, 1, d), lambda i: (i // per, 0, 0))
    full = lambda a: pl.BlockSpec(a.shape, lambda i: (0,) * a.ndim)
    return pl.pallas_call(
        _merge_kernel,
        grid=(n // tm,),
        in_specs=[tok(V_DIM), tok(V_DIM), tok(V_DIM), tok(F_DIM), tok(2 * d), tok(d), vec, vec, vec,
                  full(gn), full(n2), full(wg), full(wf), full(wm), full(wr), full(br)],
        out_specs=[tok(d), tok(d // 2), tok_t, tok(LANES), tok_t, pl.BlockSpec((1, LANES), lambda i: (0, 0))],
        out_shape=[jax.ShapeDtypeStruct((n, d), F32), jax.ShapeDtypeStruct((n, d // 2), I32),
                   jax.ShapeDtypeStruct((SUBLANES, n), I32), jax.ShapeDtypeStruct((n, LANES), F32),
                   jax.ShapeDtypeStruct((SUBLANES, n), I32), jax.ShapeDtypeStruct((1, LANES), F32)],
        scratch_shapes=[pltpu.VMEM((1, LANES), F32)],
        compiler_params=_cparams(("arbitrary",), 56),
        name="merge",
    )(of, ob, z, fm, gab, x2, g1, sh2, sc2, gn, n2, wg, wf, wm, wr, br)


def _route_tile(l, idx_ref, w_ref, rank_ref, cnt_ref, run_ref):
    i = pl.program_id(0)

    @pl.when(i == 0)
    def _():
        run_ref[...] = jnp.zeros_like(run_ref)

    tm = l.shape[0]
    lane = lax.broadcasted_iota(I32, l.shape, 1).astype(F32)
    vals, idxs = [], []
    for _ in range(TOP_K):
        m = jnp.max(l, axis=-1, keepdims=True)
        idx = jnp.min(jnp.where(l == m, lane, float(LANES)), axis=-1, keepdims=True)
        vals.append(m)
        idxs.append(idx)
        l = jnp.where(lane == idx, NEG_BIG * 2.0, l)
    es = [jnp.exp(v - vals[0]) for v in vals]
    inv = 1.0 / (es[0] + es[1] + es[2] + es[3])
    picked = jnp.zeros(l.shape, F32)
    for idx in idxs:
        picked = picked + (lane == idx).astype(F32)
    r = lax.broadcasted_iota(I32, (tm, tm), 0)
    cidx = lax.broadcasted_iota(I32, (tm, tm), 1)
    before = _mm(jnp.where(cidx < r, 1.0, 0.0), picked) + run_ref[...]
    idx_out = jnp.zeros(l.shape, F32)
    w_out = jnp.zeros(l.shape, F32)
    rank_out = jnp.zeros(l.shape, F32)
    for k in range(TOP_K):
        rk = jnp.sum(jnp.where(lane == idxs[k], before, 0.0), axis=-1, keepdims=True)
        idx_out = jnp.where(lane == k, idxs[k], idx_out)
        w_out = jnp.where(lane == k, es[k] * inv, w_out)
        rank_out = jnp.where(lane == k, rk, rank_out)
    idx_ref[...] = jnp.transpose(idx_out)[:SUBLANES].astype(I32)
    w_ref[...] = w_out
    rank_ref[...] = jnp.transpose(rank_out)[:SUBLANES].astype(I32)
    run_ref[...] = run_ref[...] + jnp.sum(picked, axis=0, keepdims=True)
    cnt_ref[...] = run_ref[...]


def _sc_mesh():
    return plsc.VectorSubcoreMesh(core_axis_name="c", subcore_axis_name="s",
                                  num_cores=SC_CORES, num_subcores=SC_SUBCORES)


def _sc_worker_id():
    return lax.axis_index("s") * SC_CORES + lax.axis_index("c")


def _sc_scatter_rows(x, idx, n_out):
    n, d = x.shape
    per_w = idx.shape[0] // SC_WORKERS
    nwin = per_w // SC_WINDOW
    assert per_w * SC_WORKERS == idx.shape[0] and nwin * SC_WINDOW == per_w and nwin % 2 == 0 and n % per_w == 0

    def body(x_hbm, idx_hbm, out_hbm, idx_v, rows_v, sem_r, sem_w):
        wid = _sc_worker_id()
        row0 = lax.rem(wid * per_w, n)
        pltpu.sync_copy(idx_hbm.at[wid], idx_v)

        def read(j, b):
            return pltpu.make_async_copy(x_hbm.at[pl.ds(row0 + j * SC_WINDOW, SC_WINDOW)], rows_v.at[b], sem_r.at[b])

        def write(j, b):
            return pltpu.make_async_copy(rows_v.at[b], out_hbm.at[idx_v.at[j]], sem_w.at[b])

        @pl.loop(0, nwin, step=2)
        def _(j):
            read(j, 0).start()
            read(j + 1, 1).start()
            read(j, 0).wait()
            write(j, 0).start()
            read(j + 1, 1).wait()
            write(j + 1, 1).start()
            write(j, 0).wait()
            write(j + 1, 1).wait()

    return pl.kernel(
        body, out_type=jax.ShapeDtypeStruct((n_out, d), x.dtype), mesh=_sc_mesh(),
        scratch_types=[pltpu.VMEM((nwin, SC_WINDOW), I32), pltpu.VMEM((2, SC_WINDOW, d), x.dtype),
                       pltpu.SemaphoreType.DMA((2,)), pltpu.SemaphoreType.DMA((2,))],
        name="sc_scatter_rows",
    )(x, idx.reshape(SC_WORKERS, nwin, SC_WINDOW))


def _sc_gather_rows(y, idx):
    d = y.shape[1]
    total = idx.shape[0]
    per_w = total // SC_WORKERS
    nwin = per_w // SC_WINDOW
    assert per_w * SC_WORKERS == total and nwin * SC_WINDOW == per_w and nwin % 2 == 0

    def body(y_hbm, idx_hbm, out_hbm, idx_v, rows_v, sem_r, sem_w):
        wid = _sc_worker_id()
        row0 = wid * per_w
        pltpu.sync_copy(idx_hbm.at[wid], idx_v)

        def read(j, b):
            return pltpu.make_async_copy(y_hbm.at[idx_v.at[j]], rows_v.at[b], sem_r.at[b])

        def write(j, b):
            return pltpu.make_async_copy(rows_v.at[b], out_hbm.at[pl.ds(row0 + j * SC_WINDOW, SC_WINDOW)], sem_w.at[b])

        @pl.loop(0, nwin, step=2)
        def _(j):
            read(j, 0).start()
            read(j + 1, 1).start()
            read(j, 0).wait()
            write(j, 0).start()
            read(j + 1, 1).wait()
            write(j + 1, 1).start()
            write(j, 0).wait()
            write(j + 1, 1).wait()

    return pl.kernel(
        body, out_type=jax.ShapeDtypeStruct((total, d), y.dtype), mesh=_sc_mesh(),
        scratch_types=[pltpu.VMEM((nwin, SC_WINDOW), I32), pltpu.VMEM((2, SC_WINDOW, d), y.dtype),
                       pltpu.SemaphoreType.DMA((2,)), pltpu.SemaphoreType.DMA((2,))],
        name="sc_gather_rows",
    )(y, idx.reshape(SC_WORKERS, nwin, SC_WINDOW))


def _expert_kernel(be_ref, nv_ref, nxt_ref, par_ref, used_ref, x_ref, wg_hbm, wu_hbm, wd_hbm, bg_ref, bu_ref, bd_ref,
                   y_ref, wgf, wuf, wdf, wgb, wub, wdb, sem):
    i = pl.program_id(0)

    def fetch(e, s):
        return [pltpu.make_async_copy(w_hbm.at[e], w_f.at[s], sem.at[s])
                for w_hbm, w_f in ((wg_hbm, wgf), (wu_hbm, wuf), (wd_hbm, wdf))]

    @pl.when(i < nv_ref[0])
    def _():
        e = be_ref[i]
        s = par_ref[e]

        @pl.when(i == 0)
        def _():
            for cp in fetch(e, s):
                cp.start()

        first = jnp.logical_or(i == 0, be_ref[jnp.maximum(i - 1, 0)] != e)
        quarter = x_ref.shape[0] // 4
        full = used_ref[i] > 3 * quarter

        @pl.when(first)
        def _():
            for cp in fetch(e, s):
                cp.wait()

            @pl.when(nxt_ref[e] >= 0)
            def _():
                for cp in fetch(nxt_ref[e], 1 - s):
                    cp.start()

        def ffn(nrows, cast):
            x = _unpack_bf16_pairs(x_ref[:nrows, :]).astype(BF16)
            if cast:
                wgb[...] = wgf[s].astype(BF16)
            gate = jnp.minimum(_mm(x, wgb[...]) + bg_ref[0], SWIGLU_LIMIT)
            if cast:
                wub[...] = wuf[s].astype(BF16)
            up = jnp.clip(_mm(x, wub[...]) + bu_ref[0], -SWIGLU_LIMIT, SWIGLU_LIMIT)
            if cast:
                wdb[...] = wdf[s].astype(BF16)
            act = (up + 1.0) * gate * _sigmoid(SWIGLU_ALPHA * gate)
            y_ref[:nrows, :] = _pack_bf16_pairs(_mm(act, wdb[...]) + bd_ref[0])

        @pl.when(jnp.logical_and(first, full))
        def _():
            ffn(4 * quarter, True)

        @pl.when(jnp.logical_and(first, jnp.logical_not(full)))
        def _():
            wgb[...] = wgf[s].astype(BF16)
            wub[...] = wuf[s].astype(BF16)
            wdb[...] = wdf[s].astype(BF16)

        for nq in range(1, 5):
            in_range = jnp.logical_and(used_ref[i] > (nq - 1) * quarter, used_ref[i] <= nq * quarter)
            if nq == 4:
                in_range = jnp.logical_and(in_range, jnp.logical_not(first))

            @pl.when(in_range)
            def _(nq=nq):
                ffn(nq * quarter, False)


def _expert_call(block_e, n_valid, next_e, parity, used, xs, w_gate, w_up, w_down, b_gate, b_up, b_down):
    nb = block_e.shape[0]
    tmb = EXPERT_BLOCK
    ne, d, de = w_gate.shape
    bspec = lambda s: pl.BlockSpec((1,) + s, lambda i, be, nv, nx, pa, us: (be[i], 0, 0))
    anyspec = pl.BlockSpec(memory_space=pl.ANY)
    grid_spec = pltpu.PrefetchScalarGridSpec(
        num_scalar_prefetch=5,
        grid=(nb,),
        in_specs=[pl.BlockSpec((tmb, d // 2), lambda i, be, nv, nx, pa, us: (i, 0)),
                  anyspec, anyspec, anyspec, bspec((1, de)), bspec((1, de)), bspec((1, d))],
        out_specs=pl.BlockSpec((tmb, d // 2), lambda i, be, nv, nx, pa, us: (i, 0)),
        scratch_shapes=[pltpu.VMEM((2, d, de), F32), pltpu.VMEM((2, d, de), F32), pltpu.VMEM((2, de, d), F32),
                        pltpu.VMEM((d, de), BF16), pltpu.VMEM((d, de), BF16), pltpu.VMEM((de, d), BF16),
                        pltpu.SemaphoreType.DMA((2,))],
    )
    return pl.pallas_call(
        _expert_kernel,
        grid_spec=grid_spec,
        out_shape=jax.ShapeDtypeStruct(xs.shape, xs.dtype),
        compiler_params=_cparams(("arbitrary",), 56),
        name="expert",
    )(block_e, n_valid, next_e, parity, used, xs, w_gate, w_up, w_down,
      b_gate.reshape(ne, 1, de), b_up.reshape(ne, 1, de), b_down.reshape(ne, 1, d))


def _combine_kernel(y0, y1, y2, y3, w_ref, x1_ref, g2_ref, fg_ref, o_ref):
    w = w_ref[...]
    ys = [_unpack_bf16_pairs(y[...]) for y in (y0, y1, y2, y3)]
    moe = (w[:, 0:1] * ys[0] + w[:, 1:2] * ys[1]) + (w[:, 2:3] * ys[2] + w[:, 3:4] * ys[3])
    o_ref[...] = _rmsnorm(x1_ref[...] + g2_ref[0] * moe, fg_ref[...])


def _combine_call(y4, top_w, x1, g2, fg, tokens_per_batch, tm):
    n, d = x1.shape
    per = tokens_per_batch // tm
    nt = n // tm
    yspec = lambda k: pl.BlockSpec((tm, d // 2), lambda i: (k * nt + i, 0))
    return pl.pallas_call(
        _combine_kernel,
        grid=(nt,),
        in_specs=[yspec(0), yspec(1), yspec(2), yspec(3),
                  pl.BlockSpec((tm, LANES), lambda i: (i, 0)),
                  pl.BlockSpec((tm, d), lambda i: (i, 0)),
                  pl.BlockSpec((1, 1, d), lambda i: (i // per, 0, 0)),
                  pl.BlockSpec((1, d), lambda i: (0, 0))],
        out_specs=pl.BlockSpec((tm, d), lambda i: (i, 0)),
        out_shape=jax.ShapeDtypeStruct((n, d), F32),
        compiler_params=_cparams(("parallel",), 48),
        name="combine",
    )(y4, y4, y4, y4, top_w, x1, g2, fg)


def _slot_kernel(start_ref, idx_ref, rank_ref, o_ref):
    idx = idx_ref[...]
    acc = rank_ref[...]
    for e in range(N_EXPERTS):
        acc = acc + jnp.where(idx == e, start_ref[e], 0)
    o_ref[...] = acc


def _slot_call(pad_start, top_idx, rank):
    full = pl.BlockSpec(top_idx.shape, lambda i, s: (0, 0))
    return pl.pallas_call(
        _slot_kernel,
        grid_spec=pltpu.PrefetchScalarGridSpec(num_scalar_prefetch=1, grid=(1,), in_specs=[full, full],
                                               out_specs=full),
        out_shape=jax.ShapeDtypeStruct(top_idx.shape, I32),
        compiler_params=_cparams(("arbitrary",), 32),
        name="slots",
    )(pad_start, top_idx, rank)


def _routing_tables(top_idx, rank, counts, n):
    tmb = EXPERT_BLOCK
    counts = counts.astype(I32)
    padded = (counts + tmb - 1) // tmb * tmb
    pad_end = jnp.cumsum(padded)
    pad_start = pad_end - padded
    n_blocks = -(-(n * TOP_K + N_EXPERTS * (tmb - 1)) // tmb)
    n_slots = n_blocks * tmb
    dest_flat = _slot_call(pad_start, top_idx, rank)[:TOP_K].reshape(-1)
    block_start = jnp.arange(n_blocks, dtype=I32) * tmb
    block_e = jnp.minimum(jnp.sum((pad_end[None, :] <= block_start[:, None]).astype(I32), axis=1), N_EXPERTS - 1)
    n_valid = (pad_end[-1:] // tmb).astype(I32)
    experts = jnp.arange(N_EXPERTS, dtype=I32)
    used = counts > 0
    later = jnp.where(jnp.logical_and(used[None, :], experts[None, :] > experts[:, None]), experts[None, :], N_EXPERTS)
    next_e = jnp.min(later, axis=1)
    next_e = jnp.where(next_e == N_EXPERTS, -1, next_e).astype(I32)
    parity = ((jnp.cumsum(used.astype(I32)) - used.astype(I32)) % 2).astype(I32)
    row_end = jnp.sum(jnp.where(block_e[:, None] == experts[None, :], (pad_start + counts)[None, :], 0), axis=1)
    rows_used = jnp.clip(row_end - block_start, 0, tmb).astype(I32)
    return block_e, n_valid, next_e, parity, rows_used, dest_flat, n_slots


def _gdn_branch(x3, sh, sc, norm_g, w_qkv, w_gates, w_extra, extra_dtypes, conv_w, par, grid_w, use_rows, tm, s0):
    q, k, v, go, rows, *extra = _inconv_call(x3, sh, sc, norm_g, w_qkv, w_gates, w_extra, extra_dtypes, conv_w, par,
                                             grid_w, use_rows, tm)
    o_f, o_b, s_fin = _gdn_call(q, k, v, go, rows, s0)
    return extra, o_f, o_b, s_fin


def kernel(x, c, ctx, c_ctx, w_mod, b_mod, norm1_g, norm2_g, w_in, conv_w, a_log, dt_bias, gdn_norm_g,
           w_fourier_out, w_gdn_out, w_merge_out, w_router, b_router, w_gate, b_gate, w_up, b_up,
           w_down, b_down, final_norm_g):
    b, l, d = x.shape
    n = b * l
    n_ctx = ctx.shape[1]
    assert w_mod.shape[0] == 1 and l == GRID_W * GRID_W and d == V_DIM

    c8 = jnp.concatenate([c, c_ctx[None, :], jnp.zeros((8 - b - 1, d), F32)], axis=0)
    mod = _mod_call(c8, w_mod[0], b_mod[0])
    sh1, sc1, g1, sh2, sc2, g2 = [mod[:b, j * d:(j + 1) * d].reshape(b, 1, d) for j in range(6)]
    csh1 = jnp.broadcast_to(mod[b:b + 1, 0:d].reshape(1, 1, d), (b, 1, d))
    csc1 = jnp.broadcast_to(mod[b:b + 1, d:2 * d].reshape(1, 1, d), (b, 1, d))

    off_gate = QKV_DIM
    off_z = off_gate + 4 * NV_HEADS
    off_f = off_z + V_DIM
    off_ga = off_f + F_DIM
    w_qkv, w_gates, *w_extra = _wsplit_call(w_in, (0, off_gate, off_z, off_f, off_ga, w_in.shape[2]))
    par = jnp.pad(jnp.stack([a_log[0].reshape(-1), dt_bias[0].reshape(-1)]),
                  ((0, 6), (GATE_LANE0, LANES - 2 * GATE_LANE0)))
    n1 = norm1_g[0].reshape(1, d)
    cw = conv_w[0].reshape(9, QKV_DIM)

    zero_state = jnp.zeros((b, 2, NV_HEADS, HEAD_DIM, HEAD_DIM), F32)
    _, _, _, s_ctx = _gdn_branch(ctx, csh1, csc1, n1, w_qkv, w_gates, [], [], cw, par, n_ctx, False, n_ctx,
                                 zero_state)

    x2 = x.reshape(n, d)
    (z, f, gab), o_f, o_b, _ = _gdn_branch(x, sh1, sc1, n1, w_qkv, w_gates, w_extra, (BF16, F32, BF16),
                                           cw, par, GRID_W, True, TOKEN_TILE, s_ctx)
    z, gab = z.reshape(n, V_DIM), gab.reshape(n, 2 * d)
    fmix = _fnet_call(f).reshape(n, F_DIM)

    wr = jnp.pad(w_router[0], ((0, 0), (0, LANES - N_EXPERTS)))
    br = jnp.pad(b_router[0], (0, LANES - N_EXPERTS), constant_values=NEG_BIG).reshape(1, LANES)
    x1, h2, top_idx, top_w, rank, counts = _merge_call(
        o_f.reshape(n, V_DIM), o_b.reshape(n, V_DIM), z, fmix, gab, x2, g1, sh2, sc2,
        gdn_norm_g[0].reshape(1, HEAD_DIM), norm2_g[0].reshape(1, d),
        w_gdn_out[0].astype(BF16), w_fourier_out[0].astype(BF16), w_merge_out[0].astype(BF16), wr, br, l, TOKEN_TILE)

    block_e, n_valid, next_e, parity, rows_used, dest_flat, n_slots = _routing_tables(
        top_idx, rank, counts[0, :N_EXPERTS], n)
    xs = _sc_scatter_rows(h2, dest_flat, n_slots)
    ys = _expert_call(block_e, n_valid, next_e, parity, rows_used, xs, w_gate[0], w_up[0], w_down[0],
                      b_gate[0], b_up[0], b_down[0])
    y4 = _sc_gather_rows(ys, dest_flat)
    out = _combine_call(y4, top_w, x1, g2, final_norm_g.reshape(1, d), l, 2 * TOKEN_TILE)
    return out.reshape(b, l, d)
```
